```python
import jax, jax.numpy as jnp
from jax import lax
import numpy as np

D_MODEL = 1024
BATCH = 8
SEQ = 16384
DEPTH = 4

N_MIXERS = 3
N_CONV_LAYERS = (DEPTH + 2) // 3
N_SGU_LAYERS = (DEPTH + 1) // 3
N_POOL_LAYERS = DEPTH // 3

CONV_WIDTH = 31
CHUNK = 128
SGU_HEADS = 8
SGU_HEAD_DIM = D_MODEL // SGU_HEADS
POOL_WINDOWS = (2, 4, 8, 16)
POOL_GROUPS = len(POOL_WINDOWS)
POOL_GROUP_DIM = D_MODEL // POOL_GROUPS
D_FF = 2816
FFN_CONV_WIDTH = 3
DEEPNORM_ALPHA = float((2 * DEPTH) ** 0.25)
DEEPNORM_BETA = float((8 * DEPTH) ** -0.25)
LN_EPS = 1e-5

kernel_name = "interleaved_conv_sgu_pool_deepnorm_trunk"


def layer_norm(x, g, b):
    xf = x.astype(jnp.float32)
    mu = jnp.mean(xf, axis=-1, keepdims=True)
    var = jnp.mean(jnp.square(xf - mu), axis=-1, keepdims=True)
    y = (xf - mu) * lax.rsqrt(var + LN_EPS) * g.astype(jnp.float32) + b.astype(jnp.float32)
    return y.astype(x.dtype)


def causal_depthwise_conv(x, w):
    k, c = w.shape
    return lax.conv_general_dilated(
        x, w[:, None, :].astype(x.dtype), window_strides=(1,), padding=[(k - 1, 0)],
        dimension_numbers=("NWC", "WIO", "NWC"), feature_group_count=c)


def conformer_conv(x, w_in, dw, dw_b, ln_g, ln_b, w_out):
    h = x @ w_in
    a, gate = jnp.split(h, 2, axis=-1)
    h = a * jax.nn.sigmoid(gate)
    h = causal_depthwise_conv(h, dw) + dw_b
    h = jax.nn.silu(layer_norm(h, ln_g, ln_b))
    return h @ w_out


def chunked_sgu(x, w_in, ln_g, ln_b, ws, bs, w_out):
    bsz, seq, _ = x.shape
    z = jax.nn.gelu(x @ w_in, approximate=False)
    u, v = jnp.split(z, 2, axis=-1)
    v = layer_norm(v, ln_g, ln_b)
    v = v.reshape(bsz, seq // CHUNK, CHUNK, SGU_HEADS, SGU_HEAD_DIM)
    mask = jnp.tril(jnp.ones((CHUNK, CHUNK), dtype=ws.dtype))
    s = jnp.einsum("hts,bnshc->bnthc", ws * mask, v)
    s = s + jnp.transpose(bs)[None, None, :, :, None]
    s = s.reshape(bsz, seq, D_MODEL)
    return (u * s) @ w_out


def multiscale_pool(x, w_in, w_grp, scale, w_out):
    bsz, seq, _ = x.shape
    y = x @ w_in
    yf = y.astype(jnp.float32)
    cs = jnp.concatenate([jnp.zeros((bsz, 1, D_MODEL), jnp.float32),
                          lax.cumsum(yf, axis=1)], axis=1)
    pos = jnp.arange(seq)
    groups = []
    for g, w in enumerate(POOL_WINDOWS):
        sl = slice(g * POOL_GROUP_DIM, (g + 1) * POOL_GROUP_DIM)
        c = cs[..., sl]
        upper = c[:, 1:]
        lower = jnp.pad(c[:, :seq - w + 1], ((0, 0), (w - 1, 0), (0, 0)))
        count = jnp.minimum(pos + 1, w).astype(jnp.float32)[None, :, None]
        groups.append((upper - lower) / count - yf[..., sl])
    p = jnp.stack(groups, axis=2).astype(y.dtype)
    z = jnp.einsum("bsgc,gcd->bsgd", p, w_grp).reshape(bsz, seq, D_MODEL) * scale
    return z @ w_out


def conv_ffn(x, w_up, dw, w_down):
    h = causal_depthwise_conv(x @ w_up, dw)
    g, v = jnp.split(h, 2, axis=-1)
    return (jax.nn.silu(g) * v) @ w_down


def _fwd_setup_inputs(seed: int = 0) -> dict:
    key = jax.random.key(seed)
    ks = iter(jax.random.split(key, 32))

    def nrm(shape, scale):
        return jax.random.normal(next(ks), shape, jnp.float32) * scale

    d = D_MODEL
    return {
        "x": nrm((BATCH, SEQ, d), 1.0),
        "a_w_in": nrm((N_CONV_LAYERS, d, 2 * d), d ** -0.5),
        "a_dw": nrm((N_CONV_LAYERS, CONV_WIDTH, d), CONV_WIDTH ** -0.5),
        "a_dw_b": nrm((N_CONV_LAYERS, d), 0.02),
        "a_ln_g": 1.0 + nrm((N_CONV_LAYERS, d), 0.02),
        "a_ln_b": nrm((N_CONV_LAYERS, d), 0.02),
        "a_w_out": nrm((N_CONV_LAYERS, d, d), d ** -0.5 * DEEPNORM_BETA),
        "b_w_in": nrm((N_SGU_LAYERS, d, 2 * d), d ** -0.5),
        "b_ln_g": 1.0 + nrm((N_SGU_LAYERS, d), 0.02),
        "b_ln_b": nrm((N_SGU_LAYERS, d), 0.02),
        "b_ws": nrm((N_SGU_LAYERS, SGU_HEADS, CHUNK, CHUNK), CHUNK ** -0.5),
        "b_bs": 1.0 + nrm((N_SGU_LAYERS, SGU_HEADS, CHUNK), 0.01),
        "b_w_out": nrm((N_SGU_LAYERS, d, d), d ** -0.5 * DEEPNORM_BETA),
        "c_w_in": nrm((N_POOL_LAYERS, d, d), d ** -0.5),
        "c_w_grp": nrm((N_POOL_LAYERS, POOL_GROUPS, POOL_GROUP_DIM, POOL_GROUP_DIM), POOL_GROUP_DIM ** -0.5),
        "c_scale": 1.0 + nrm((N_POOL_LAYERS, d), 0.1),
        "c_w_out": nrm((N_POOL_LAYERS, d, d), d ** -0.5 * DEEPNORM_BETA),
        "f_w_up": nrm((DEPTH, d, 2 * D_FF), d ** -0.5),
        "f_dw": nrm((DEPTH, FFN_CONV_WIDTH, 2 * D_FF), FFN_CONV_WIDTH ** -0.5),
        "f_w_down": nrm((DEPTH, D_FF, d), D_FF ** -0.5 * DEEPNORM_BETA),
        "ln1_g": 1.0 + nrm((DEPTH, d), 0.02),
        "ln1_b": nrm((DEPTH, d), 0.02),
        "ln2_g": 1.0 + nrm((DEPTH, d), 0.02),
        "ln2_b": nrm((DEPTH, d), 0.02),
    }


def _fwd_reference(x, a_w_in, a_dw, a_dw_b, a_ln_g, a_ln_b, a_w_out,
              b_w_in, b_ln_g, b_ln_b, b_ws, b_bs, b_w_out,
              c_w_in, c_w_grp, c_scale, c_w_out,
              f_w_up, f_dw, f_w_down,
              ln1_g, ln1_b, ln2_g, ln2_b):
    for i in range(DEPTH):
        kind, j = i % N_MIXERS, i // N_MIXERS
        if kind == 0:
            h = conformer_conv(x, a_w_in[j], a_dw[j], a_dw_b[j], a_ln_g[j], a_ln_b[j], a_w_out[j])
        elif kind == 1:
            h = chunked_sgu(x, b_w_in[j], b_ln_g[j], b_ln_b[j], b_ws[j], b_bs[j], b_w_out[j])
        else:
            h = multiscale_pool(x, c_w_in[j], c_w_grp[j], c_scale[j], c_w_out[j])
        x = layer_norm(DEEPNORM_ALPHA * x + h, ln1_g[i], ln1_b[i])
        x = layer_norm(DEEPNORM_ALPHA * x + conv_ffn(x, f_w_up[i], f_dw[i], f_w_down[i]), ln2_g[i], ln2_b[i])
    return x


import jax as _jax
import jax.numpy as _jnp

TWIN_FORMAT = 'train_step'
FWD_PARAMS = ['x', 'a_w_in', 'a_dw', 'a_dw_b', 'a_ln_g', 'a_ln_b', 'a_w_out', 'b_w_in', 'b_ln_g', 'b_ln_b', 'b_ws', 'b_bs', 'b_w_out', 'c_w_in', 'c_w_grp', 'c_scale', 'c_w_out', 'f_w_up', 'f_dw', 'f_w_down', 'ln1_g', 'ln1_b', 'ln2_g', 'ln2_b']
TWIN_WEIGHTS = ['a_w_in', 'a_dw', 'a_dw_b', 'a_ln_g', 'a_ln_b', 'a_w_out', 'b_w_in', 'b_ln_g', 'b_ln_b', 'b_ws', 'b_bs', 'b_w_out', 'c_w_in', 'c_w_grp', 'c_scale', 'c_w_out', 'f_w_up', 'f_dw', 'f_w_down', 'ln1_g', 'ln1_b', 'ln2_g', 'ln2_b']
TWIN_DIFF_INPUT = 'x'
TWIN_INPUTS = ['x', 'a_w_in', 'a_dw', 'a_dw_b', 'a_ln_g', 'a_ln_b', 'a_w_out', 'b_w_in', 'b_ln_g', 'b_ln_b', 'b_ws', 'b_bs', 'b_w_out', 'c_w_in', 'c_w_grp', 'c_scale', 'c_w_out', 'f_w_up', 'f_dw', 'f_w_down', 'ln1_g', 'ln1_b', 'ln2_g', 'ln2_b', 'loss_target', 'm_a_w_in', 'm_a_dw', 'm_a_dw_b', 'm_a_ln_g', 'm_a_ln_b', 'm_a_w_out', 'm_b_w_in', 'm_b_ln_g', 'm_b_ln_b', 'm_b_ws', 'm_b_bs', 'm_b_w_out', 'm_c_w_in', 'm_c_w_grp', 'm_c_scale', 'm_c_w_out', 'm_f_w_up', 'm_f_dw', 'm_f_w_down', 'm_ln1_g', 'm_ln1_b', 'm_ln2_g', 'm_ln2_b', 'v_a_w_in', 'v_a_dw', 'v_a_dw_b', 'v_a_ln_g', 'v_a_ln_b', 'v_a_w_out', 'v_b_w_in', 'v_b_ln_g', 'v_b_ln_b', 'v_b_ws', 'v_b_bs', 'v_b_w_out', 'v_c_w_in', 'v_c_w_grp', 'v_c_scale', 'v_c_w_out', 'v_f_w_up', 'v_f_dw', 'v_f_w_down', 'v_ln1_g', 'v_ln1_b', 'v_ln2_g', 'v_ln2_b']
TWIN_OUTPUTS = ['loss', 'grad_x', 'grad_a_w_in', 'grad_a_dw', 'grad_a_dw_b', 'grad_a_ln_g', 'grad_a_ln_b', 'grad_a_w_out', 'grad_b_w_in', 'grad_b_ln_g', 'grad_b_ln_b', 'grad_b_ws', 'grad_b_bs', 'grad_b_w_out', 'grad_c_w_in', 'grad_c_w_grp', 'grad_c_scale', 'grad_c_w_out', 'grad_f_w_up', 'grad_f_dw', 'grad_f_w_down', 'grad_ln1_g', 'grad_ln1_b', 'grad_ln2_g', 'grad_ln2_b', 'delta_a_w_in', 'delta_a_dw', 'delta_a_dw_b', 'delta_a_ln_g', 'delta_a_ln_b', 'delta_a_w_out', 'delta_b_w_in', 'delta_b_ln_g', 'delta_b_ln_b', 'delta_b_ws', 'delta_b_bs', 'delta_b_w_out', 'delta_c_w_in', 'delta_c_w_grp', 'delta_c_scale', 'delta_c_w_out', 'delta_f_w_up', 'delta_f_dw', 'delta_f_w_down', 'delta_ln1_g', 'delta_ln1_b', 'delta_ln2_g', 'delta_ln2_b', 'new_m_a_w_in', 'new_m_a_dw', 'new_m_a_dw_b', 'new_m_a_ln_g', 'new_m_a_ln_b', 'new_m_a_w_out', 'new_m_b_w_in', 'new_m_b_ln_g', 'new_m_b_ln_b', 'new_m_b_ws', 'new_m_b_bs', 'new_m_b_w_out', 'new_m_c_w_in', 'new_m_c_w_grp', 'new_m_c_scale', 'new_m_c_w_out', 'new_m_f_w_up', 'new_m_f_dw', 'new_m_f_w_down', 'new_m_ln1_g', 'new_m_ln1_b', 'new_m_ln2_g', 'new_m_ln2_b', 'new_v_a_w_in', 'new_v_a_dw', 'new_v_a_dw_b', 'new_v_a_ln_g', 'new_v_a_ln_b', 'new_v_a_w_out', 'new_v_b_w_in', 'new_v_b_ln_g', 'new_v_b_ln_b', 'new_v_b_ws', 'new_v_b_bs', 'new_v_b_w_out', 'new_v_c_w_in', 'new_v_c_w_grp', 'new_v_c_scale', 'new_v_c_w_out', 'new_v_f_w_up', 'new_v_f_dw', 'new_v_f_w_down', 'new_v_ln1_g', 'new_v_ln1_b', 'new_v_ln2_g', 'new_v_ln2_b']
TWIN_LEAF_KINDS = {'loss': 'loss', 'grad_x': 'grad_x', 'grad_a_w_in': 'grad_w', 'grad_a_dw': 'grad_w', 'grad_a_dw_b': 'grad_w', 'grad_a_ln_g': 'grad_w', 'grad_a_ln_b': 'grad_w', 'grad_a_w_out': 'grad_w', 'grad_b_w_in': 'grad_w', 'grad_b_ln_g': 'grad_w', 'grad_b_ln_b': 'grad_w', 'grad_b_ws': 'grad_w', 'grad_b_bs': 'grad_w', 'grad_b_w_out': 'grad_w', 'grad_c_w_in': 'grad_w', 'grad_c_w_grp': 'grad_w', 'grad_c_scale': 'grad_w', 'grad_c_w_out': 'grad_w', 'grad_f_w_up': 'grad_w', 'grad_f_dw': 'grad_w', 'grad_f_w_down': 'grad_w', 'grad_ln1_g': 'grad_w', 'grad_ln1_b': 'grad_w', 'grad_ln2_g': 'grad_w', 'grad_ln2_b': 'grad_w', 'delta_a_w_in': 'delta_w', 'delta_a_dw': 'delta_w', 'delta_a_dw_b': 'delta_w', 'delta_a_ln_g': 'delta_w', 'delta_a_ln_b': 'delta_w', 'delta_a_w_out': 'delta_w', 'delta_b_w_in': 'delta_w', 'delta_b_ln_g': 'delta_w', 'delta_b_ln_b': 'delta_w', 'delta_b_ws': 'delta_w', 'delta_b_bs': 'delta_w', 'delta_b_w_out': 'delta_w', 'delta_c_w_in': 'delta_w', 'delta_c_w_grp': 'delta_w', 'delta_c_scale': 'delta_w', 'delta_c_w_out': 'delta_w', 'delta_f_w_up': 'delta_w', 'delta_f_dw': 'delta_w', 'delta_f_w_down': 'delta_w', 'delta_ln1_g': 'delta_w', 'delta_ln1_b': 'delta_w', 'delta_ln2_g': 'delta_w', 'delta_ln2_b': 'delta_w', 'new_m_a_w_in': 'new_m', 'new_m_a_dw': 'new_m', 'new_m_a_dw_b': 'new_m', 'new_m_a_ln_g': 'new_m', 'new_m_a_ln_b': 'new_m', 'new_m_a_w_out': 'new_m', 'new_m_b_w_in': 'new_m', 'new_m_b_ln_g': 'new_m', 'new_m_b_ln_b': 'new_m', 'new_m_b_ws': 'new_m', 'new_m_b_bs': 'new_m', 'new_m_b_w_out': 'new_m', 'new_m_c_w_in': 'new_m', 'new_m_c_w_grp': 'new_m', 'new_m_c_scale': 'new_m', 'new_m_c_w_out': 'new_m', 'new_m_f_w_up': 'new_m', 'new_m_f_dw': 'new_m', 'new_m_f_w_down': 'new_m', 'new_m_ln1_g': 'new_m', 'new_m_ln1_b': 'new_m', 'new_m_ln2_g': 'new_m', 'new_m_ln2_b': 'new_m', 'new_v_a_w_in': 'new_v', 'new_v_a_dw': 'new_v', 'new_v_a_dw_b': 'new_v', 'new_v_a_ln_g': 'new_v', 'new_v_a_ln_b': 'new_v', 'new_v_a_w_out': 'new_v', 'new_v_b_w_in': 'new_v', 'new_v_b_ln_g': 'new_v', 'new_v_b_ln_b': 'new_v', 'new_v_b_ws': 'new_v', 'new_v_b_bs': 'new_v', 'new_v_b_w_out': 'new_v', 'new_v_c_w_in': 'new_v', 'new_v_c_w_grp': 'new_v', 'new_v_c_scale': 'new_v', 'new_v_c_w_out': 'new_v', 'new_v_f_w_up': 'new_v', 'new_v_f_dw': 'new_v', 'new_v_f_w_down': 'new_v', 'new_v_ln1_g': 'new_v', 'new_v_ln1_b': 'new_v', 'new_v_ln2_g': 'new_v', 'new_v_ln2_b': 'new_v'}


def _forward(args):
    return _fwd_reference(*[args[k] for k in FWD_PARAMS])


def _output_shape():
    def fwd():
        inp = _fwd_setup_inputs(0)
        return _fwd_reference(*[inp[k] for k in FWD_PARAMS])
    out = _jax.eval_shape(fwd)
    return out.shape, out.dtype

N_MICROBATCH = 1
ADAM_LR = 0.001
ADAM_B1 = 0.9
ADAM_B2 = 0.999
ADAM_EPS = 1e-08
ADAM_WD = 0.01
ADAM_STEP = 10
PER_EXAMPLE_BATCH_AXIS = {'x': 0, 'loss_target': 0}
SHARED_INPUTS = []
_WEIGHT_DTYPES = {'a_w_in': _jnp.float32, 'a_dw': _jnp.float32, 'a_dw_b': _jnp.float32, 'a_ln_g': _jnp.float32, 'a_ln_b': _jnp.float32, 'a_w_out': _jnp.float32, 'b_w_in': _jnp.float32, 'b_ln_g': _jnp.float32, 'b_ln_b': _jnp.float32, 'b_ws': _jnp.float32, 'b_bs': _jnp.float32, 'b_w_out': _jnp.float32, 'c_w_in': _jnp.float32, 'c_w_grp': _jnp.float32, 'c_scale': _jnp.float32, 'c_w_out': _jnp.float32, 'f_w_up': _jnp.float32, 'f_dw': _jnp.float32, 'f_w_down': _jnp.float32, 'ln1_g': _jnp.float32, 'ln1_b': _jnp.float32, 'ln2_g': _jnp.float32, 'ln2_b': _jnp.float32}
MOMENT_SCALE = {'a_w_in': 4.319086e-02, 'a_dw': 5.840319e-02, 'a_dw_b': 3.253053e-01, 'a_ln_g': 1.281794e-01, 'a_ln_b': 2.107965e-01, 'a_w_out': 1.964265e-01, 'b_w_in': 6.317002e-02, 'b_ln_g': 4.285068e-02, 'b_ln_b': 4.253155e-02, 'b_ws': 4.250579e-02, 'b_bs': 6.002206e-02, 'b_w_out': 2.589577e-01, 'c_w_in': 7.874951e-02, 'c_w_grp': 7.942263e-02, 'c_scale': 7.762131e-02, 'c_w_out': 1.886901e-01, 'f_w_up': 3.383554e-02, 'f_dw': 3.469225e-02, 'f_w_down': 1.318292e-01, 'ln1_g': 3.504576e+00, 'ln1_b': 1.887552e+00, 'ln2_g': 6.430094e+01, 'ln2_b': 6.789907e+00}


def _to_microbatches(a, axis):
    t = _jnp.moveaxis(a, axis, 0)
    t = t.reshape((N_MICROBATCH, t.shape[0] // N_MICROBATCH) + t.shape[1:])
    return _jnp.moveaxis(t, 1, axis + 1)


def setup_inputs(seed: int = 0) -> dict:
    inp = _fwd_setup_inputs(seed)
    key = _jax.random.fold_in(_jax.random.key(seed), 7919)
    shape, _ = _output_shape()
    out = dict(inp)
    out["loss_target"] = _jax.random.normal(_jax.random.fold_in(key, 0), shape, _jnp.float32)
    for i, name in enumerate(TWIN_WEIGHTS):
        w = inp[name].astype(_jnp.float32)
        if MOMENT_SCALE is None:
            s = _jnp.sqrt(_jnp.mean(_jnp.square(w)) + 1e-30)
        else:
            s = MOMENT_SCALE[name]
        km, kv = _jax.random.split(_jax.random.fold_in(key, i + 1))
        out[name] = w
        out["m_" + name] = s * _jax.random.normal(km, w.shape, _jnp.float32)
        out["v_" + name] = (s * s) * _jax.random.uniform(kv, w.shape, _jnp.float32, 0.5, 1.5)
    if N_MICROBATCH > 1:
        for name, axis in PER_EXAMPLE_BATCH_AXIS.items():
            out[name] = _to_microbatches(out[name], axis)
    return {'x': out['x'], 'a_w_in': out['a_w_in'], 'a_dw': out['a_dw'], 'a_dw_b': out['a_dw_b'], 'a_ln_g': out['a_ln_g'], 'a_ln_b': out['a_ln_b'], 'a_w_out': out['a_w_out'], 'b_w_in': out['b_w_in'], 'b_ln_g': out['b_ln_g'], 'b_ln_b': out['b_ln_b'], 'b_ws': out['b_ws'], 'b_bs': out['b_bs'], 'b_w_out': out['b_w_out'], 'c_w_in': out['c_w_in'], 'c_w_grp': out['c_w_grp'], 'c_scale': out['c_scale'], 'c_w_out': out['c_w_out'], 'f_w_up': out['f_w_up'], 'f_dw': out['f_dw'], 'f_w_down': out['f_w_down'], 'ln1_g': out['ln1_g'], 'ln1_b': out['ln1_b'], 'ln2_g': out['ln2_g'], 'ln2_b': out['ln2_b'], 'loss_target': out['loss_target'], 'm_a_w_in': out['m_a_w_in'], 'm_a_dw': out['m_a_dw'], 'm_a_dw_b': out['m_a_dw_b'], 'm_a_ln_g': out['m_a_ln_g'], 'm_a_ln_b': out['m_a_ln_b'], 'm_a_w_out': out['m_a_w_out'], 'm_b_w_in': out['m_b_w_in'], 'm_b_ln_g': out['m_b_ln_g'], 'm_b_ln_b': out['m_b_ln_b'], 'm_b_ws': out['m_b_ws'], 'm_b_bs': out['m_b_bs'], 'm_b_w_out': out['m_b_w_out'], 'm_c_w_in': out['m_c_w_in'], 'm_c_w_grp': out['m_c_w_grp'], 'm_c_scale': out['m_c_scale'], 'm_c_w_out': out['m_c_w_out'], 'm_f_w_up': out['m_f_w_up'], 'm_f_dw': out['m_f_dw'], 'm_f_w_down': out['m_f_w_down'], 'm_ln1_g': out['m_ln1_g'], 'm_ln1_b': out['m_ln1_b'], 'm_ln2_g': out['m_ln2_g'], 'm_ln2_b': out['m_ln2_b'], 'v_a_w_in': out['v_a_w_in'], 'v_a_dw': out['v_a_dw'], 'v_a_dw_b': out['v_a_dw_b'], 'v_a_ln_g': out['v_a_ln_g'], 'v_a_ln_b': out['v_a_ln_b'], 'v_a_w_out': out['v_a_w_out'], 'v_b_w_in': out['v_b_w_in'], 'v_b_ln_g': out['v_b_ln_g'], 'v_b_ln_b': out['v_b_ln_b'], 'v_b_ws': out['v_b_ws'], 'v_b_bs': out['v_b_bs'], 'v_b_w_out': out['v_b_w_out'], 'v_c_w_in': out['v_c_w_in'], 'v_c_w_grp': out['v_c_w_grp'], 'v_c_scale': out['v_c_scale'], 'v_c_w_out': out['v_c_w_out'], 'v_f_w_up': out['v_f_w_up'], 'v_f_dw': out['v_f_dw'], 'v_f_w_down': out['v_f_w_down'], 'v_ln1_g': out['v_ln1_g'], 'v_ln1_b': out['v_ln1_b'], 'v_ln2_g': out['v_ln2_g'], 'v_ln2_b': out['v_ln2_b']}


def _loss(weights, diff, rest, loss_target):
    with _jax.named_scope("forward"):
        args = {**rest, TWIN_DIFF_INPUT: diff, **{k: w.astype(_WEIGHT_DTYPES[k]) for k, w in weights.items()}}
        y = _forward(args)
    with _jax.named_scope("loss_head"):
        err = _jnp.square(y.astype(_jnp.float32) - loss_target)
        return 0.5 * _jnp.sum(_jnp.mean(err, axis=-1)) if err.ndim else 0.5 * err


def _adamw(w, g, m, v):
    m = ADAM_B1 * m + (1.0 - ADAM_B1) * g
    v = ADAM_B2 * v + (1.0 - ADAM_B2) * _jnp.square(g)
    m_hat = m / (1.0 - ADAM_B1 ** ADAM_STEP)
    v_hat = v / (1.0 - ADAM_B2 ** ADAM_STEP)
    delta = -ADAM_LR * (m_hat / (_jnp.sqrt(v_hat) + ADAM_EPS) + ADAM_WD * w)
    return delta, m, v


def reference(x, a_w_in, a_dw, a_dw_b, a_ln_g, a_ln_b, a_w_out, b_w_in, b_ln_g, b_ln_b, b_ws, b_bs, b_w_out, c_w_in, c_w_grp, c_scale, c_w_out, f_w_up, f_dw, f_w_down, ln1_g, ln1_b, ln2_g, ln2_b, loss_target, m_a_w_in, m_a_dw, m_a_dw_b, m_a_ln_g, m_a_ln_b, m_a_w_out, m_b_w_in, m_b_ln_g, m_b_ln_b, m_b_ws, m_b_bs, m_b_w_out, m_c_w_in, m_c_w_grp, m_c_scale, m_c_w_out, m_f_w_up, m_f_dw, m_f_w_down, m_ln1_g, m_ln1_b, m_ln2_g, m_ln2_b, v_a_w_in, v_a_dw, v_a_dw_b, v_a_ln_g, v_a_ln_b, v_a_w_out, v_b_w_in, v_b_ln_g, v_b_ln_b, v_b_ws, v_b_bs, v_b_w_out, v_c_w_in, v_c_w_grp, v_c_scale, v_c_w_out, v_f_w_up, v_f_dw, v_f_w_down, v_ln1_g, v_ln1_b, v_ln2_g, v_ln2_b):
    given = dict(x=x, a_w_in=a_w_in, a_dw=a_dw, a_dw_b=a_dw_b, a_ln_g=a_ln_g, a_ln_b=a_ln_b, a_w_out=a_w_out, b_w_in=b_w_in, b_ln_g=b_ln_g, b_ln_b=b_ln_b, b_ws=b_ws, b_bs=b_bs, b_w_out=b_w_out, c_w_in=c_w_in, c_w_grp=c_w_grp, c_scale=c_scale, c_w_out=c_w_out, f_w_up=f_w_up, f_dw=f_dw, f_w_down=f_w_down, ln1_g=ln1_g, ln1_b=ln1_b, ln2_g=ln2_g, ln2_b=ln2_b, loss_target=loss_target, m_a_w_in=m_a_w_in, m_a_dw=m_a_dw, m_a_dw_b=m_a_dw_b, m_a_ln_g=m_a_ln_g, m_a_ln_b=m_a_ln_b, m_a_w_out=m_a_w_out, m_b_w_in=m_b_w_in, m_b_ln_g=m_b_ln_g, m_b_ln_b=m_b_ln_b, m_b_ws=m_b_ws, m_b_bs=m_b_bs, m_b_w_out=m_b_w_out, m_c_w_in=m_c_w_in, m_c_w_grp=m_c_w_grp, m_c_scale=m_c_scale, m_c_w_out=m_c_w_out, m_f_w_up=m_f_w_up, m_f_dw=m_f_dw, m_f_w_down=m_f_w_down, m_ln1_g=m_ln1_g, m_ln1_b=m_ln1_b, m_ln2_g=m_ln2_g, m_ln2_b=m_ln2_b, v_a_w_in=v_a_w_in, v_a_dw=v_a_dw, v_a_dw_b=v_a_dw_b, v_a_ln_g=v_a_ln_g, v_a_ln_b=v_a_ln_b, v_a_w_out=v_a_w_out, v_b_w_in=v_b_w_in, v_b_ln_g=v_b_ln_g, v_b_ln_b=v_b_ln_b, v_b_ws=v_b_ws, v_b_bs=v_b_bs, v_b_w_out=v_b_w_out, v_c_w_in=v_c_w_in, v_c_w_grp=v_c_w_grp, v_c_scale=v_c_scale, v_c_w_out=v_c_w_out, v_f_w_up=v_f_w_up, v_f_dw=v_f_dw, v_f_w_down=v_f_w_down, v_ln1_g=v_ln1_g, v_ln1_b=v_ln1_b, v_ln2_g=v_ln2_g, v_ln2_b=v_ln2_b)
    weights = {n: given[n] for n in TWIN_WEIGHTS}
    shared = {n: given[n] for n in SHARED_INPUTS}
    per_example = {n: given[n] for n in ['x']}
    grad_fn = _jax.value_and_grad(_loss, argnums=(0, 1))

    def one_microbatch(ex, loss_target):
        ex = dict(ex)
        diff = ex.pop(TWIN_DIFF_INPUT)
        return grad_fn(weights, diff, {**shared, **ex}, loss_target)

    if N_MICROBATCH == 1:
        loss, (grad_w, grad_x) = one_microbatch(per_example, given["loss_target"])
    else:
        def body(carry, xs):
            loss_sum, grad_sum = carry
            l_k, (gw_k, gx_k) = one_microbatch(xs[0], xs[1])
            with _jax.named_scope("update"):
                return (loss_sum + l_k, _jax.tree.map(_jnp.add, grad_sum, gw_k)), gx_k

        init = (_jnp.zeros((), _jnp.float32), _jax.tree.map(_jnp.zeros_like, weights))
        (loss, grad_w), grad_x = _jax.lax.scan(body, init, (per_example, given["loss_target"]))
    with _jax.named_scope("update"):
        delta_w, new_m, new_v = {}, {}, {}
        for n in TWIN_WEIGHTS:
            delta_w[n], new_m[n], new_v[n] = _adamw(weights[n], grad_w[n], given["m_" + n], given["v_" + n])
    return (loss, grad_x, *[grad_w[n] for n in TWIN_WEIGHTS], *[delta_w[n] for n in TWIN_WEIGHTS],
            *[new_m[n] for n in TWIN_WEIGHTS], *[new_v[n] for n in TWIN_WEIGHTS])
```

```python
import math

import jax
import jax.numpy as jnp
from jax import lax
from jax.experimental import pallas as pl
from jax.experimental.pallas import tpu as pltpu

F32 = jnp.float32
BF16 = jnp.bfloat16

LN_EPS = 1e-5
POOL_WINDOWS = (2, 4, 8, 16)
CHUNK = 128
ADAM_LR = 0.001
ADAM_B1 = 0.9
ADAM_B2 = 0.999
ADAM_EPS = 1e-08
ADAM_WD = 0.01
ADAM_STEP = 10

LANES = 128
SUBLANES_BF16 = 16
N_CHIPS = 4
N_DEV = 8
VMEM_LIMIT = 60 * 1024 * 1024

TM_FFN = 512
TM_CONV = 256
TM_SGU = 512
TM_POOL = 512
TM_BWD_IN = 512
TS_MM_TN = 1024
CW_FFN = 256
CONV_HALO = 32
POOL_HALO = 16
FFN_HALO = 16

MESH = pl.DeviceIdType.MESH


def _cparams(n_grid=1, parallel=False):
    sem = ("parallel" if parallel else "arbitrary",) * n_grid
    return pltpu.CompilerParams(dimension_semantics=sem, vmem_limit_bytes=VMEM_LIMIT)


def _resident(block, imap):
    return pl.BlockSpec(block, imap, pipeline_mode=pl.Buffered(1))


def _wspec(w, l):
    _, r, c = w.shape
    return _resident((None, r, c), lambda *_: (l, 0, 0))


def _rowspec(d):
    return pl.BlockSpec((1, d), lambda *_: (0, 0))


def _dot(a, b):
    return jnp.dot(a, b, preferred_element_type=F32)


def _dot_nt(a, b):
    return lax.dot_general(a, b, (((1,), (1,)), ((), ())), preferred_element_type=F32)


def _dot_tn(a, b):
    return lax.dot_general(a, b, (((0,), (0,)), ((), ())), preferred_element_type=F32)


def _sigmoid(x):
    return jax.nn.sigmoid(x)


def _ln_stats(r):
    mu = jnp.mean(r, axis=1, keepdims=True)
    xc = r - mu
    var = jnp.mean(xc * xc, axis=1, keepdims=True)
    rstd = lax.rsqrt(var + LN_EPS)
    return xc * rstd, rstd


def _ln_bwd(dy, xhat, rstd, g):
    dxh = dy * g
    m1 = jnp.mean(dxh, axis=1, keepdims=True)
    m2 = jnp.mean(dxh * xhat, axis=1, keepdims=True)
    return rstd * (dxh - m1 - xhat * m2)


def _colsum(v):
    return jnp.sum(v, axis=0, keepdims=True)


def _gelu(z):
    return 0.5 * z * (1.0 + lax.erf(z * (1.0 / math.sqrt(2.0))))


def _gelu_grad(z):
    cdf = 0.5 * (1.0 + lax.erf(z * (1.0 / math.sqrt(2.0))))
    pdf = jnp.exp(-0.5 * z * z) * (1.0 / math.sqrt(2.0 * math.pi))
    return cdf + z * pdf


def _shift_down(v, k, prev_rows):
    tm = v.shape[0]
    rows = lax.broadcasted_iota(jnp.int32, v.shape, 0)
    out = pltpu.roll(v, k, 0)
    for r in range(k):
        out = jnp.where(rows == r, prev_rows[k - 1 - r], out)
    return out


def _shift_up(v, k, next_rows):
    tm = v.shape[0]
    rows = lax.broadcasted_iota(jnp.int32, v.shape, 0)
    out = pltpu.roll(v, tm - k, 0)
    for r in range(k):
        out = jnp.where(rows == tm - k + r, next_rows[r], out)
    return out


def _pick_rows(r, c, itemsize, cap_bytes):
    best = None
    for t in range(16, r + 1, 16):
        if r % t == 0 and t * c * itemsize <= cap_bytes:
            best = t
    return best if best is not None else r


def _ffn_conv_cols(h, dw_ref, c0, cw, prev1, prev2):
    kw = dw_ref.shape[0]
    h1 = _shift_down(h, 1, [prev1])
    h2 = _shift_down(h, 2, [prev1, prev2])
    hc = dw_ref[kw - 1:kw, c0:c0 + cw] * h + dw_ref[kw - 2:kw - 1, c0:c0 + cw] * h1 + dw_ref[kw - 3:kw - 2, c0:c0 + cw] * h2
    return hc, h1, h2


def _ffn_fwd(xh1, g1, b1, wup, l, fdw, wdn, alpha, name):
    s, d = xh1.shape
    f2 = wup.shape[2]
    f = f2 // 2
    tm = min(TM_FFN, s)
    cw = min(CW_FFN, f)
    n, nck = s // tm, f // cw
    assert fdw.shape[1] == 3 and s % tm == 0 and f % cw == 0

    def body(xh_ref, g_ref, b_ref, wup_ref, dw_ref, wdn_ref, xo_ref, rs_ref, hs_ref, carry):
        @pl.when(pl.program_id(0) == 0)
        def _():
            carry[...] = jnp.zeros_like(carry)

        x1 = xh_ref[...] * g_ref[...] + b_ref[...]
        xb = x1.astype(BF16)
        o = jnp.zeros((tm, d), F32)
        for j in range(nck):
            parts = []
            for half in range(2):
                c0 = half * f + j * cw
                h = _dot(xb, wup_ref[:, c0:c0 + cw])
                hs_ref[:, c0:c0 + cw] = h.astype(BF16)
                hc, _, _ = _ffn_conv_cols(h, dw_ref, c0, cw, carry[7:8, c0:c0 + cw], carry[6:7, c0:c0 + cw])
                carry[:, c0:c0 + cw] = h[tm - 8:tm, :]
                parts.append(hc)
            gg, vv = parts
            a = (gg * _sigmoid(gg) * vv).astype(BF16)
            o = o + _dot(a, wdn_ref[j * cw:(j + 1) * cw, :])
        xhat, rstd = _ln_stats(alpha * x1 + o)
        xo_ref[...] = xhat
        rs_ref[...] = rstd

    tile = pl.BlockSpec((tm, d), lambda i: (i, 0))
    return pl.pallas_call(
        body, grid=(n,),
        in_specs=[tile, _rowspec(d), _rowspec(d), _wspec(wup, l),
                  pl.BlockSpec((None, 3, f2), lambda i: (l, 0, 0)), _wspec(wdn, l)],
        out_specs=[tile, pl.BlockSpec((tm, 1), lambda i: (i, 0)), pl.BlockSpec((tm, f2), lambda i: (i, 0))],
        out_shape=[jax.ShapeDtypeStruct((s, d), F32), jax.ShapeDtypeStruct((s, 1), F32),
                   jax.ShapeDtypeStruct((s, f2), BF16)],
        scratch_shapes=[pltpu.VMEM((8, f2), F32)],
        compiler_params=_cparams(), name=name,
    )(xh1, g1, b1, wup, fdw, wdn)


def _ffn_bwd1(dx2, xh2, rstd2, g2, hs, wdn, l, fdw, gwdn_buf, name):
    s, d = dx2.shape
    f2 = hs.shape[1]
    f = f2 // 2
    tm = min(TM_FFN, s)
    cw = min(CW_FFN, f)
    n, nck = s // tm, f // cw
    hb = FFN_HALO

    def body(dx_ref, xh_ref, rs_ref, g_ref, hs_ref, halo_ref, wdn_ref, dw_ref, buf_ref,
             dr_ref, dh_ref, gwdn_ref, gdw_ref, gg_ref, gb_ref, carry):
        i = pl.program_id(0)
        t = n - 1 - i

        @pl.when(i == 0)
        def _():
            carry[...] = jnp.zeros_like(carry)
            gwdn_ref[...] = jnp.zeros_like(gwdn_ref)
            gdw_ref[...] = jnp.zeros_like(gdw_ref)
            gg_ref[...] = jnp.zeros_like(gg_ref)
            gb_ref[...] = jnp.zeros_like(gb_ref)

        dx = dx_ref[...]
        xh = xh_ref[...]
        gg_ref[...] += _colsum(dx * xh)
        gb_ref[...] += _colsum(dx)
        dr = _ln_bwd(dx, xh, rs_ref[...], g_ref[...])
        dr_ref[...] = dr
        dob = dr.astype(BF16)
        has_prev = t > 0
        for j in range(nck):
            hc, hh = [], []
            for half in range(2):
                c0 = half * f + j * cw
                h = hs_ref[:, c0:c0 + cw].astype(F32)
                hal = jnp.where(has_prev, halo_ref[:, c0:c0 + cw].astype(F32), 0.0)
                c, h1, h2 = _ffn_conv_cols(h, dw_ref, c0, cw, hal[hb - 1:hb], hal[hb - 2:hb - 1])
                hc.append(c)
                hh.append((h, h1, h2))
            gt, vv = hc
            sg = _sigmoid(gt)
            sl = gt * sg
            a = (sl * vv).astype(BF16)
            gwdn_ref[j * cw:(j + 1) * cw, :] += _dot_tn(a, dob)
            da = _dot_nt(dob, wdn_ref[j * cw:(j + 1) * cw, :])
            dhc = (da * vv * (sg * (1.0 + gt * (1.0 - sg))), da * sl)
            for half in range(2):
                c0 = half * f + j * cw
                dc = dhc[half]
                h, h1, h2 = hh[half]
                gdw_ref[2:3, c0:c0 + cw] += _colsum(dc * h)
                gdw_ref[1:2, c0:c0 + cw] += _colsum(dc * h1)
                gdw_ref[0:1, c0:c0 + cw] += _colsum(dc * h2)
                nxt = [carry[0:1, c0:c0 + cw], carry[1:2, c0:c0 + cw]]
                u1 = _shift_up(dc, 1, nxt[:1])
                u2 = _shift_up(dc, 2, nxt)
                dh = dw_ref[2:3, c0:c0 + cw] * dc + dw_ref[1:2, c0:c0 + cw] * u1 + dw_ref[0:1, c0:c0 + cw] * u2
                dh_ref[:, c0:c0 + cw] = dh.astype(BF16)
                carry[:, c0:c0 + cw] = dc[0:8, :]

    tile = pl.BlockSpec((tm, d), lambda i: (n - 1 - i, 0))
    halo_blocks = tm // hb
    nl = gwdn_buf.shape[0]
    outs = pl.pallas_call(
        body, grid=(n,),
        in_specs=[tile, tile, pl.BlockSpec((tm, 1), lambda i: (n - 1 - i, 0)), _rowspec(d),
                  pl.BlockSpec((tm, f2), lambda i: (n - 1 - i, 0)),
                  pl.BlockSpec((hb, f2), lambda i: (jnp.maximum((n - 1 - i) * halo_blocks - 1, 0), 0)),
                  _wspec(wdn, l), pl.BlockSpec((None, 3, f2), lambda i: (l, 0, 0)),
                  pl.BlockSpec(memory_space=pl.ANY)],
        out_specs=[tile, pl.BlockSpec((tm, f2), lambda i: (n - 1 - i, 0)),
                   pl.BlockSpec((None, f, d), lambda i: (l, 0, 0)),
                   pl.BlockSpec((3, f2), lambda i: (0, 0)), _rowspec(d), _rowspec(d)],
        out_shape=[jax.ShapeDtypeStruct((s, d), F32), jax.ShapeDtypeStruct((s, f2), BF16),
                   jax.ShapeDtypeStruct((nl, f, d), F32), jax.ShapeDtypeStruct((3, f2), F32),
                   jax.ShapeDtypeStruct((1, d), F32), jax.ShapeDtypeStruct((1, d), F32)],
        scratch_shapes=[pltpu.VMEM((8, f2), F32)],
        input_output_aliases={8: 2},
        compiler_params=_cparams(), name=name,
    )(dx2, xh2, rstd2, g2, hs, hs, wdn, fdw, gwdn_buf)
    return outs


def _bwd_in(dp, dres, w, l, alpha, name, ln=None):
    s, d = dres.shape
    nn = dp.shape[1]
    tm = min(TM_BWD_IN, s)
    n = s // tm
    tile = pl.BlockSpec((tm, d), lambda i: (i, 0))

    if ln is None:
        def body(dp_ref, dres_ref, w_ref, o_ref):
            o_ref[...] = alpha * dres_ref[...] + _dot_nt(dp_ref[...], w_ref[...])

        return pl.pallas_call(
            body, grid=(n,),
            in_specs=[pl.BlockSpec((tm, nn), lambda i: (i, 0)), tile, _wspec(w, l)],
            out_specs=tile, out_shape=jax.ShapeDtypeStruct((s, d), F32),
            compiler_params=_cparams(parallel=True), name=name,
        )(dp, dres, w)

    xh, rstd, g = ln

    def body_ln(dp_ref, dres_ref, w_ref, xh_ref, rs_ref, g_ref, o_ref, gg_ref, gb_ref):
        @pl.when(pl.program_id(0) == 0)
        def _():
            gg_ref[...] = jnp.zeros_like(gg_ref)
            gb_ref[...] = jnp.zeros_like(gb_ref)

        dx = alpha * dres_ref[...] + _dot_nt(dp_ref[...], w_ref[...])
        xhv = xh_ref[...]
        gg_ref[...] += _colsum(dx * xhv)
        gb_ref[...] += _colsum(dx)
        o_ref[...] = _ln_bwd(dx, xhv, rs_ref[...], g_ref[...])

    return pl.pallas_call(
        body_ln, grid=(n,),
        in_specs=[pl.BlockSpec((tm, nn), lambda i: (i, 0)), tile, _wspec(w, l), tile,
                  pl.BlockSpec((tm, 1), lambda i: (i, 0)), _rowspec(d)],
        out_specs=[tile, _rowspec(d), _rowspec(d)],
        out_shape=[jax.ShapeDtypeStruct((s, d), F32), jax.ShapeDtypeStruct((1, d), F32),
                   jax.ShapeDtypeStruct((1, d), F32)],
        compiler_params=_cparams(), name=name,
    )(dp, dres, w, xh, rstd, g)


def _mm_tn(a, ga, ba, bm, buf, l, name):
    s, k = a.shape
    nn = bm.shape[1]
    ts = min(TS_MM_TN, s)
    tn = nn // N_CHIPS if nn > 1024 else nn
    nj, ns = nn // tn, s // ts

    def body(a_ref, g_ref, b_ref, bm_ref, buf_ref, o_ref):
        @pl.when(pl.program_id(1) == 0)
        def _():
            o_ref[...] = jnp.zeros_like(o_ref)

        ab = (a_ref[...] * g_ref[...] + b_ref[...]).astype(BF16)
        o_ref[...] += _dot_tn(ab, bm_ref[...])

    return pl.pallas_call(
        body, grid=(nj, ns),
        in_specs=[pl.BlockSpec((ts, k), lambda j, t: (t, 0)), _rowspec(k), _rowspec(k),
                  pl.BlockSpec((ts, tn), lambda j, t: (t, j)), pl.BlockSpec(memory_space=pl.ANY)],
        out_specs=pl.BlockSpec((None, k, tn), lambda j, t: (l, 0, j)),
        out_shape=jax.ShapeDtypeStruct(buf.shape, F32),
        input_output_aliases={4: 0},
        compiler_params=_cparams(2), name=name,
    )(a, ga, ba, bm, buf)


def _conv_fwd(xin, gin, bin_, win, l, adw, adwb, lng, lnb, wout, alpha, name):
    s, d = xin.shape
    kw = adw.shape[1]
    hb = CONV_HALO
    tm = min(TM_CONV, s)
    n = s // tm
    assert kw - 1 <= hb <= tm

    def body(x_ref, g_ref, b_ref, win_ref, dw_ref, dwb_ref, lng_ref, lnb_ref, wout_ref,
             xo_ref, rs_ref, p_ref, chat_ref, rsc_ref, u_scr):
        @pl.when(pl.program_id(0) == 0)
        def _():
            u_scr[0:hb, :] = jnp.zeros((hb, d), F32)

        x = x_ref[...] * g_ref[...] + b_ref[...]
        pm = _dot(x.astype(BF16), win_ref[...])
        p_ref[...] = pm.astype(BF16)
        u = pm[:, :d] * _sigmoid(pm[:, d:])
        u_scr[hb:hb + tm, :] = u
        acc = dwb_ref[...] + dw_ref[kw - 1:kw, :] * u
        for k in range(kw - 1):
            acc = acc + dw_ref[k:k + 1, :] * u_scr[pl.ds(hb - (kw - 1) + k, tm), :]
        u_scr[0:hb, :] = u_scr[tm:tm + hb, :]
        chat, rstdc = _ln_stats(acc)
        chat_ref[...] = chat.astype(BF16)
        rsc_ref[...] = rstdc
        nv = chat * lng_ref[...] + lnb_ref[...]
        sv = (nv * _sigmoid(nv)).astype(BF16)
        xhat, rstd = _ln_stats(alpha * x + _dot(sv, wout_ref[...]))
        xo_ref[...] = xhat
        rs_ref[...] = rstd

    tile = pl.BlockSpec((tm, d), lambda i: (i, 0))
    col = pl.BlockSpec((tm, 1), lambda i: (i, 0))
    return pl.pallas_call(
        body, grid=(n,),
        in_specs=[tile, _rowspec(d), _rowspec(d), _wspec(win, l),
                  pl.BlockSpec((None, kw, d), lambda i: (l, 0, 0)), _rowspec(d), _rowspec(d), _rowspec(d),
                  _wspec(wout, l)],
        out_specs=[tile, col, pl.BlockSpec((tm, 2 * d), lambda i: (i, 0)), tile, col],
        out_shape=[jax.ShapeDtypeStruct((s, d), F32), jax.ShapeDtypeStruct((s, 1), F32),
                   jax.ShapeDtypeStruct((s, 2 * d), BF16), jax.ShapeDtypeStruct((s, d), BF16),
                   jax.ShapeDtypeStruct((s, 1), F32)],
        scratch_shapes=[pltpu.VMEM((tm + hb, d), F32)],
        compiler_params=_cparams(), name=name,
    )(xin, gin, bin_, win, adw, adwb, lng, lnb, wout)


def _conv_bwd1(dr1, chat, rstdc, p, wout, l, adw, lng, lnb, gwout_buf, name):
    s, d = dr1.shape
    kw = adw.shape[1]
    hb = CONV_HALO
    tm = min(TM_CONV, s)
    n = s // tm
    halo_blocks = tm // hb

    def body(dr_ref, chat_ref, rsc_ref, p_ref, halo_ref, wout_ref, dw_ref, lng_ref, lnb_ref, buf_ref,
             dp_ref, gwout_ref, gdw_ref, gdwb_ref, glng_ref, glnb_ref, u_scr, dc_scr):
        i = pl.program_id(0)
        t = n - 1 - i

        @pl.when(i == 0)
        def _():
            dc_scr[tm:tm + hb, :] = jnp.zeros((hb, d), F32)
            gwout_ref[...] = jnp.zeros_like(gwout_ref)
            gdw_ref[...] = jnp.zeros_like(gdw_ref)
            gdwb_ref[...] = jnp.zeros_like(gdwb_ref)
            glng_ref[...] = jnp.zeros_like(glng_ref)
            glnb_ref[...] = jnp.zeros_like(glnb_ref)

        dob = dr_ref[...].astype(BF16)
        chat = chat_ref[...].astype(F32)
        lng = lng_ref[...]
        nv = chat * lng + lnb_ref[...]
        sgn = _sigmoid(nv)
        gwout_ref[...] += _dot_tn((nv * sgn).astype(BF16), dob)
        dn = _dot_nt(dob, wout_ref[...]) * (sgn * (1.0 + nv * (1.0 - sgn)))
        glng_ref[...] += _colsum(dn * chat)
        glnb_ref[...] += _colsum(dn)
        dc = _ln_bwd(dn, chat, rsc_ref[...], lng)
        gdwb_ref[...] += _colsum(dc)

        pm = p_ref[...].astype(F32)
        a = pm[:, :d]
        sg = _sigmoid(pm[:, d:])
        ph = halo_ref[...].astype(F32)
        u_scr[0:hb, :] = jnp.where(t > 0, ph[:, :d] * _sigmoid(ph[:, d:]), 0.0)
        u_scr[hb:hb + tm, :] = a * sg
        dc_scr[0:tm, :] = dc
        du = dw_ref[kw - 1:kw, :] * dc
        for k in range(kw):
            gdw_ref[k:k + 1, :] += _colsum(dc * u_scr[pl.ds(hb - (kw - 1) + k, tm), :])
            if k < kw - 1:
                du = du + dw_ref[k:k + 1, :] * dc_scr[pl.ds(kw - 1 - k, tm), :]
        dc_scr[tm:tm + hb, :] = dc[0:hb, :]
        dp_ref[:, :d] = (du * sg).astype(BF16)
        dp_ref[:, d:] = (du * a * sg * (1.0 - sg)).astype(BF16)

    tile = pl.BlockSpec((tm, d), lambda i: (n - 1 - i, 0))
    col = pl.BlockSpec((tm, 1), lambda i: (n - 1 - i, 0))
    nl = gwout_buf.shape[0]
    return pl.pallas_call(
        body, grid=(n,),
        in_specs=[tile, tile, col, pl.BlockSpec((tm, 2 * d), lambda i: (n - 1 - i, 0)),
                  pl.BlockSpec((hb, 2 * d), lambda i: (jnp.maximum((n - 1 - i) * halo_blocks - 1, 0), 0)),
                  _wspec(wout, l), pl.BlockSpec((None, kw, d), lambda i: (l, 0, 0)), _rowspec(d), _rowspec(d),
                  pl.BlockSpec(memory_space=pl.ANY)],
        out_specs=[pl.BlockSpec((tm, 2 * d), lambda i: (n - 1 - i, 0)),
                   pl.BlockSpec((None, d, d), lambda i: (l, 0, 0)),
                   pl.BlockSpec((kw, d), lambda i: (0, 0)), _rowspec(d), _rowspec(d), _rowspec(d)],
        out_shape=[jax.ShapeDtypeStruct((s, 2 * d), BF16), jax.ShapeDtypeStruct((nl, d, d), F32),
                   jax.ShapeDtypeStruct((kw, d), F32), jax.ShapeDtypeStruct((1, d), F32),
                   jax.ShapeDtypeStruct((1, d), F32), jax.ShapeDtypeStruct((1, d), F32)],
        scratch_shapes=[pltpu.VMEM((tm + hb, d), F32), pltpu.VMEM((tm + hb, d), F32)],
        input_output_aliases={9: 1},
        compiler_params=_cparams(), name=name,
    )(dr1, chat, rstdc, p, p, wout, adw, lng, lnb, gwout_buf)


def _sgu_gate(vn, wm_ref, bs_ref, s_scr, tm, nh):
    for ch in range(tm // CHUNK):
        r0 = ch * CHUNK
        for h in range(nh):
            c0 = h * CHUNK
            s_scr[r0:r0 + CHUNK, c0:c0 + CHUNK] = (
                _dot(wm_ref[h], vn[r0:r0 + CHUNK, c0:c0 + CHUNK]) + bs_ref[:, c0:c0 + CHUNK])


def _sgu_fwd(xin, gin, bin_, win, lg, lb, wm, bs_exp, wout, alpha, name):
    s, d = xin.shape
    nh = wm.shape[0]
    tm = min(TM_SGU, s)
    n = s // tm
    assert tm % CHUNK == 0 and nh * CHUNK == d

    def body(x_ref, g_ref, b_ref, win_ref, lg_ref, lb_ref, wm_ref, bs_ref, wout_ref,
             xo_ref, rs_ref, zp_ref, s_scr):
        x = x_ref[...] * g_ref[...] + b_ref[...]
        zp = _dot(x.astype(BF16), win_ref[...])
        zp_ref[...] = zp.astype(BF16)
        z = _gelu(zp)
        vhat, _ = _ln_stats(z[:, d:])
        vn = (vhat * lg_ref[...] + lb_ref[...]).astype(BF16)
        _sgu_gate(vn, wm_ref, bs_ref, s_scr, tm, nh)
        q = (z[:, :d] * s_scr[...]).astype(BF16)
        xhat, rstd = _ln_stats(alpha * x + _dot(q, wout_ref[...]))
        xo_ref[...] = xhat
        rs_ref[...] = rstd

    tile = pl.BlockSpec((tm, d), lambda i: (i, 0))
    return pl.pallas_call(
        body, grid=(n,),
        in_specs=[tile, _rowspec(d), _rowspec(d), _wspec(win, 0), _rowspec(d), _rowspec(d),
                  _resident((nh, CHUNK, CHUNK), lambda i: (0, 0, 0)),
                  _resident((CHUNK, d), lambda i: (0, 0)), _wspec(wout, 0)],
        out_specs=[tile, pl.BlockSpec((tm, 1), lambda i: (i, 0)), pl.BlockSpec((tm, 2 * d), lambda i: (i, 0))],
        out_shape=[jax.ShapeDtypeStruct((s, d), F32), jax.ShapeDtypeStruct((s, 1), F32),
                   jax.ShapeDtypeStruct((s, 2 * d), BF16)],
        scratch_shapes=[pltpu.VMEM((tm, d), F32)],
        compiler_params=_cparams(parallel=True), name=name,
    )(xin, gin, bin_, win, lg, lb, wm, bs_exp, wout)


def _sgu_bwd1(dr1, zp, wout, lg, lb, wm, wmt, bs_exp, gwout_buf, name):
    s, d = dr1.shape
    nh = wm.shape[0]
    tm = min(TM_SGU, s)
    n = s // tm

    def body(dr_ref, zp_ref, wout_ref, lg_ref, lb_ref, wm_ref, wmt_ref, bs_ref, buf_ref,
             dzp_ref, gwout_ref, gws_ref, gbs_ref, glg_ref, glb_ref, s_scr, dvn_scr, bs_acc):
        i = pl.program_id(0)

        @pl.when(i == 0)
        def _():
            gwout_ref[...] = jnp.zeros_like(gwout_ref)
            gws_ref[...] = jnp.zeros_like(gws_ref)
            glg_ref[...] = jnp.zeros_like(glg_ref)
            glb_ref[...] = jnp.zeros_like(glb_ref)
            bs_acc[...] = jnp.zeros_like(bs_acc)

        dob = dr_ref[...].astype(BF16)
        zp = zp_ref[...].astype(F32)
        z = _gelu(zp)
        u = z[:, :d]
        lg = lg_ref[...]
        vhat, rstdv = _ln_stats(z[:, d:])
        vn = (vhat * lg + lb_ref[...]).astype(BF16)
        _sgu_gate(vn, wm_ref, bs_ref, s_scr, tm, nh)
        sv = s_scr[...]
        gwout_ref[...] += _dot_tn((u * sv).astype(BF16), dob)
        dq = _dot_nt(dob, wout_ref[...])
        ds = dq * u
        dsb = ds.astype(BF16)
        part = jnp.zeros((CHUNK, d), F32)
        for ch in range(tm // CHUNK):
            r0 = ch * CHUNK
            part = part + ds[r0:r0 + CHUNK, :]
            for h in range(nh):
                c0 = h * CHUNK
                blk = dsb[r0:r0 + CHUNK, c0:c0 + CHUNK]
                gws_ref[h] += _dot_nt(blk, vn[r0:r0 + CHUNK, c0:c0 + CHUNK])
                dvn_scr[r0:r0 + CHUNK, c0:c0 + CHUNK] = _dot(wmt_ref[h], blk)
        bs_acc[...] += part
        dvn = dvn_scr[...]
        glg_ref[...] += _colsum(dvn * vhat)
        glb_ref[...] += _colsum(dvn)
        dv = _ln_bwd(dvn, vhat, rstdv, lg)
        gp = _gelu_grad(zp)
        dzp_ref[:, :d] = (dq * sv * gp[:, :d]).astype(BF16)
        dzp_ref[:, d:] = (dv * gp[:, d:]).astype(BF16)

        @pl.when(i == n - 1)
        def _():
            rows = lax.broadcasted_iota(jnp.int32, (CHUNK, CHUNK), 0)
            cols = lax.broadcasted_iota(jnp.int32, (CHUNK, CHUNK), 1)
            tril = (cols <= rows).astype(F32)
            acc = bs_acc[...]
            for h in range(nh):
                gws_ref[h] = gws_ref[h] * tril
                gbs_ref[:, h:h + 1] = jnp.sum(acc[:, h * CHUNK:(h + 1) * CHUNK], axis=1, keepdims=True)

    tile = pl.BlockSpec((tm, d), lambda i: (i, 0))
    wide = pl.BlockSpec((tm, 2 * d), lambda i: (i, 0))
    hspec = _resident((nh, CHUNK, CHUNK), lambda i: (0, 0, 0))
    return pl.pallas_call(
        body, grid=(n,),
        in_specs=[tile, wide, _wspec(wout, 0), _rowspec(d), _rowspec(d), hspec, hspec,
                  _resident((CHUNK, d), lambda i: (0, 0)), pl.BlockSpec(memory_space=pl.ANY)],
        out_specs=[wide, pl.BlockSpec((None, d, d), lambda i: (0, 0, 0)),
                   pl.BlockSpec((nh, CHUNK, CHUNK), lambda i: (0, 0, 0)),
                   pl.BlockSpec((CHUNK, nh), lambda i: (0, 0)), _rowspec(d), _rowspec(d)],
        out_shape=[jax.ShapeDtypeStruct((s, 2 * d), BF16), jax.ShapeDtypeStruct(gwout_buf.shape, F32),
                   jax.ShapeDtypeStruct((nh, CHUNK, CHUNK), F32), jax.ShapeDtypeStruct((CHUNK, nh), F32),
                   jax.ShapeDtypeStruct((1, d), F32), jax.ShapeDtypeStruct((1, d), F32)],
        scratch_shapes=[pltpu.VMEM((tm, d), F32), pltpu.VMEM((tm, d), F32), pltpu.VMEM((CHUNK, d), F32)],
        input_output_aliases={8: 1},
        compiler_params=_cparams(), name=name,
    )(dr1, zp, wout, lg, lb, wm, wmt, bs_exp, gwout_buf)


def _pool_counts(t0, tm, w):
    pos = t0 + lax.broadcasted_iota(jnp.int32, (tm, 1), 0)
    return jnp.minimum(pos + 1, w).astype(F32)


def _pool_fwd(xin, gin, bin_, win, wg, scale, wout, alpha, name):
    s, d = xin.shape
    ng, dg = wg.shape[0], wg.shape[1]
    hb = POOL_HALO
    tm = min(TM_POOL, s)
    n = s // tm
    assert ng == len(POOL_WINDOWS) and ng * dg == d and max(POOL_WINDOWS) <= hb

    def body(x_ref, g_ref, b_ref, win_ref, wg_ref, sc_ref, wout_ref, xo_ref, rs_ref, ys_ref, y_scr, z_scr):
        i = pl.program_id(0)

        @pl.when(i == 0)
        def _():
            y_scr[0:hb, :] = jnp.zeros((hb, d), F32)

        x = x_ref[...] * g_ref[...] + b_ref[...]
        y = _dot(x.astype(BF16), win_ref[...])
        ys_ref[...] = y.astype(BF16)
        y_scr[hb:hb + tm, :] = y
        for g, w in enumerate(POOL_WINDOWS):
            c0 = g * dg
            acc = y[:, c0:c0 + dg]
            for dd in range(1, w):
                acc = acc + y_scr[pl.ds(hb - dd, tm), c0:c0 + dg]
            pg = acc / _pool_counts(i * tm, tm, w) - y[:, c0:c0 + dg]
            z_scr[:, c0:c0 + dg] = _dot(pg.astype(BF16), wg_ref[g])
        y_scr[0:hb, :] = y_scr[tm:tm + hb, :]
        zz = (z_scr[...] * sc_ref[...]).astype(BF16)
        xhat, rstd = _ln_stats(alpha * x + _dot(zz, wout_ref[...]))
        xo_ref[...] = xhat
        rs_ref[...] = rstd

    tile = pl.BlockSpec((tm, d), lambda i: (i, 0))
    return pl.pallas_call(
        body, grid=(n,),
        in_specs=[tile, _rowspec(d), _rowspec(d), _wspec(win, 0),
                  _resident((ng, dg, dg), lambda i: (0, 0, 0)), _rowspec(d), _wspec(wout, 0)],
        out_specs=[tile, pl.BlockSpec((tm, 1), lambda i: (i, 0)), tile],
        out_shape=[jax.ShapeDtypeStruct((s, d), F32), jax.ShapeDtypeStruct((s, 1), F32),
                   jax.ShapeDtypeStruct((s, d), BF16)],
        scratch_shapes=[pltpu.VMEM((tm + hb, d), F32), pltpu.VMEM((tm, d), F32)],
        compiler_params=_cparams(), name=name,
    )(xin, gin, bin_, win, wg, scale, wout)


def _pool_bwd1(dr1, ys, wout, wg, scale, gwout_buf, name):
    s, d = dr1.shape
    ng, dg = wg.shape[0], wg.shape[1]
    hb = POOL_HALO
    tm = min(TM_POOL, s)
    n = s // tm
    halo_blocks = tm // hb

    def body(dr_ref, ys_ref, halo_ref, wout_ref, wg_ref, sc_ref, buf_ref,
             dy_ref, gwout_ref, gwg_ref, gsc_ref, y_scr, e_scr, z_scr, dp_scr):
        i = pl.program_id(0)
        t = n - 1 - i

        @pl.when(i == 0)
        def _():
            e_scr[tm:tm + hb, :] = jnp.zeros((hb, d), F32)
            gwout_ref[...] = jnp.zeros_like(gwout_ref)
            gwg_ref[...] = jnp.zeros_like(gwg_ref)
            gsc_ref[...] = jnp.zeros_like(gsc_ref)

        dob = dr_ref[...].astype(BF16)
        y = ys_ref[...].astype(F32)
        y_scr[0:hb, :] = jnp.where(t > 0, halo_ref[...].astype(F32), 0.0)
        y_scr[hb:hb + tm, :] = y
        pgs = []
        for g, w in enumerate(POOL_WINDOWS):
            c0 = g * dg
            acc = y[:, c0:c0 + dg]
            for dd in range(1, w):
                acc = acc + y_scr[pl.ds(hb - dd, tm), c0:c0 + dg]
            pg = (acc / _pool_counts(t * tm, tm, w) - y[:, c0:c0 + dg]).astype(BF16)
            pgs.append(pg)
            z_scr[:, c0:c0 + dg] = _dot(pg, wg_ref[g])
        zpre = z_scr[...]
        sc = sc_ref[...]
        gwout_ref[...] += _dot_tn((zpre * sc).astype(BF16), dob)
        dz = _dot_nt(dob, wout_ref[...])
        gsc_ref[...] += _colsum(dz * zpre)
        dzpre = (dz * sc).astype(BF16)
        for g, w in enumerate(POOL_WINDOWS):
            c0 = g * dg
            dzg = dzpre[:, c0:c0 + dg]
            gwg_ref[g] += _dot_tn(pgs[g], dzg)
            dp = _dot_nt(dzg, wg_ref[g])
            dp_scr[:, c0:c0 + dg] = dp
            e_scr[0:tm, c0:c0 + dg] = dp / _pool_counts(t * tm, tm, w)
        for g, w in enumerate(POOL_WINDOWS):
            c0 = g * dg
            acc = e_scr[0:tm, c0:c0 + dg]
            for dd in range(1, w):
                acc = acc + e_scr[pl.ds(dd, tm), c0:c0 + dg]
            dy_ref[:, c0:c0 + dg] = (acc - dp_scr[:, c0:c0 + dg]).astype(BF16)
        e_scr[tm:tm + hb, :] = e_scr[0:hb, :]

    tile = pl.BlockSpec((tm, d), lambda i: (n - 1 - i, 0))
    return pl.pallas_call(
        body, grid=(n,),
        in_specs=[tile, tile,
                  pl.BlockSpec((hb, d), lambda i: (jnp.maximum((n - 1 - i) * halo_blocks - 1, 0), 0)),
                  _wspec(wout, 0), _resident((ng, dg, dg), lambda i: (0, 0, 0)), _rowspec(d),
                  pl.BlockSpec(memory_space=pl.ANY)],
        out_specs=[tile, pl.BlockSpec((None, d, d), lambda i: (0, 0, 0)),
                   pl.BlockSpec((ng, dg, dg), lambda i: (0, 0, 0)), _rowspec(d)],
        out_shape=[jax.ShapeDtypeStruct((s, d), BF16), jax.ShapeDtypeStruct(gwout_buf.shape, F32),
                   jax.ShapeDtypeStruct((ng, dg, dg), F32), jax.ShapeDtypeStruct((1, d), F32)],
        scratch_shapes=[pltpu.VMEM((tm + hb, d), F32), pltpu.VMEM((tm + hb, d), F32),
                        pltpu.VMEM((tm, d), F32), pltpu.VMEM((tm, d), F32)],
        input_output_aliases={6: 1},
        compiler_params=_cparams(), name=name,
    )(dr1, ys, ys, wout, wg, scale, gwout_buf)


def _loss_head(xh, g, b, target, name):
    s, d = xh.shape
    tm = min(512, s)
    n = s // tm

    def body(xh_ref, g_ref, b_ref, t_ref, dy_ref, loss_ref, acc):
        i = pl.program_id(0)

        @pl.when(i == 0)
        def _():
            acc[...] = jnp.zeros_like(acc)

        err = xh_ref[...] * g_ref[...] + b_ref[...] - t_ref[...]
        dy_ref[...] = err * (1.0 / d)
        acc[...] += _colsum(err * err)

        @pl.when(i == n - 1)
        def _():
            loss_ref[...] = (0.5 / d) * jnp.sum(acc[...], axis=1, keepdims=True)

    tile = pl.BlockSpec((tm, d), lambda i: (i, 0))
    return pl.pallas_call(
        body, grid=(n,),
        in_specs=[tile, _rowspec(d), _rowspec(d), tile],
        out_specs=[tile, pl.BlockSpec((1, 1), lambda i: (0, 0))],
        out_shape=[jax.ShapeDtypeStruct((s, d), F32), jax.ShapeDtypeStruct((1, 1), F32)],
        scratch_shapes=[pltpu.VMEM((1, d), F32)],
        compiler_params=_cparams(), name=name,
    )(xh, g, b, target)


def _elementwise(fn, ins, out_dtypes, name, lead=0):
    shape = ins[0].shape
    c = shape[-1]
    r = math.prod(shape[lead:-1])
    lead_shape = tuple(shape[:lead])
    out_shape = tuple(shape[lead:])
    tr = _pick_rows(r, c, 4, (1 << 20) // max(1, math.prod(lead_shape)))
    nlead = len(lead_shape)

    def body(*refs):
        vals = fn(*[ref[...] for ref in refs[:len(ins)]])
        for ref, v in zip(refs[len(ins):], vals):
            ref[...] = v.astype(ref.dtype)

    def spec(a_lead):
        if a_lead:
            return pl.BlockSpec(lead_shape + (tr, c), lambda i: (0,) * nlead + (i, 0))
        return pl.BlockSpec((tr, c), lambda i: (i, 0))

    args = [ins[0].reshape(lead_shape + (r, c))] + [a.reshape(r, c) for a in ins[1:]]
    outs = pl.pallas_call(
        body, grid=(r // tr,),
        in_specs=[spec(bool(lead))] + [spec(False)] * (len(ins) - 1),
        out_specs=[spec(False)] * len(out_dtypes),
        out_shape=[jax.ShapeDtypeStruct((r, c), dt) for dt in out_dtypes],
        compiler_params=_cparams(parallel=True), name=name,
    )(*args)
    return [o.reshape(out_shape) for o in outs]


def _cast_bf16(w, name):
    return _elementwise(lambda v: (v,), [w], [BF16], name)[0]


def _add(a, b, name):
    return _elementwise(lambda u, v: (u + v,), [a, b], [F32], name)[0]


def _sum_lead(stacked, name):
    def fn(v):
        acc = v[0]
        for k in range(1, v.shape[0]):
            acc = acc + v[k]
        return (acc,)

    return _elementwise(fn, [stacked], [F32], name, lead=1)[0]


def _adamw(w, g, m, v, name):
    def fn(w_, g_, m_, v_):
        m2 = ADAM_B1 * m_ + (1.0 - ADAM_B1) * g_
        v2 = ADAM_B2 * v_ + (1.0 - ADAM_B2) * (g_ * g_)
        m_hat = m2 / (1.0 - ADAM_B1 ** ADAM_STEP)
        v_hat = v2 / (1.0 - ADAM_B2 ** ADAM_STEP)
        delta = -ADAM_LR * (m_hat / (jnp.sqrt(v_hat) + ADAM_EPS) + ADAM_WD * w_)
        return delta, m2, v2

    return _elementwise(fn, [w, g, m, v], [F32, F32, F32], name)


ANY = pl.BlockSpec(memory_space=pl.ANY)


def _mesh_pos():
    return lax.axis_index("x"), lax.axis_index("y"), lax.axis_index("c")


def _chip_peers(x, y, c):
    out = []
    for r in (1, 2, 3):
        px = 1 - x if r & 2 else x
        py = 1 - y if r & 1 else y
        out.append((2 * px + py, (px, py, c)))
    return out


def _full_shape(kind, shard_shape):
    l, r, c = shard_shape
    return (l, N_CHIPS, r, c) if kind == "row" else (l, r, N_CHIPS * c)


def _half_rows(shape3):
    hr = shape3[1] // 2
    assert hr % SUBLANES_BF16 == 0
    return hr


def _shard_half(ref, h, hr):
    return ref.at[:, pl.ds(pl.multiple_of(h * hr, SUBLANES_BF16), hr), :]


def _full_piece(ref, kind, k, h, hr, sc):
    rows = pl.ds(pl.multiple_of(h * hr, SUBLANES_BF16), hr)
    if kind == "row":
        return ref.at[:, k, rows, :]
    return ref.at[:, rows, pl.ds(pl.multiple_of(k * sc, LANES), sc)]


def _full_slot(ref, kind, k, sc):
    if kind == "row":
        return ref.at[:, k]
    return ref.at[:, :, pl.ds(pl.multiple_of(k * sc, LANES), sc)]


def _remote(src, dst, ssem, rsem, dev):
    return pltpu.make_async_remote_copy(src_ref=src, dst_ref=dst, send_sem=ssem, recv_sem=rsem,
                                        device_id=dev, device_id_type=MESH)


def _allgather_weights(shards, kinds):
    nw = len(shards)
    hrs = [_half_rows(a.shape) for a in shards]
    scs = [a.shape[2] for a in shards]

    def body(*refs):
        sh, fu = refs[:nw], refs[nw:2 * nw]
        send1, recv1, send2, recv2, lsem = refs[2 * nw:]
        x, y, c = _mesh_pos()
        k_me = 2 * x + y
        sibling = (x, y, 1 - c)
        peers = _chip_peers(x, y, c)

        def piece(w, k, h):
            return _full_piece(fu[w], kinds[w], k, h, hrs[w], scs[w])

        locs, sends = [], []
        for w in range(nw):
            cp = pltpu.make_async_copy(sh[w], _full_slot(fu[w], kinds[w], k_me, scs[w]), lsem.at[w])
            cp.start()
            locs.append(cp)
            for r, (_, dev) in enumerate(peers):
                cp = _remote(_shard_half(sh[w], c, hrs[w]), piece(w, k_me, c),
                             send1.at[3 * w + r], recv1.at[3 * w + r], dev)
                cp.start()
                sends.append(cp)
        for w in range(nw):
            for r, (kj, dev) in enumerate(peers):
                _remote(_shard_half(sh[w], c, hrs[w]), piece(w, kj, c),
                        send1.at[3 * w + r], recv1.at[3 * w + r], dev).wait_recv()
                cp = _remote(piece(w, kj, c), piece(w, kj, c), send2.at[3 * w + r], recv2.at[3 * w + r], sibling)
                cp.start()
                sends.append(cp)
        for w in range(nw):
            for r, (kj, _) in enumerate(peers):
                _remote(piece(w, kj, 1 - c), piece(w, kj, 1 - c),
                        send2.at[3 * w + r], recv2.at[3 * w + r], sibling).wait_recv()
        for cp in sends:
            cp.wait_send()
        for cp in locs:
            cp.wait()

    outs = pl.pallas_call(
        body,
        in_specs=[ANY] * nw, out_specs=[ANY] * nw,
        out_shape=[jax.ShapeDtypeStruct(_full_shape(k, a.shape), a.dtype) for k, a in zip(kinds, shards)],
        scratch_shapes=[pltpu.SemaphoreType.DMA((3 * nw,))] * 4 + [pltpu.SemaphoreType.DMA((nw,))],
        name="allgather_weights",
    )(*shards)
    return outs


def _rs_pair(fulls, kinds):
    nw = len(fulls)

    def half_all(ref, kind, h):
        if kind == "row":
            hr = ref.shape[2] // 2
            return ref.at[:, :, pl.ds(pl.multiple_of(h * hr, SUBLANES_BF16), hr), :]
        hr = ref.shape[1] // 2
        return ref.at[:, pl.ds(pl.multiple_of(h * hr, SUBLANES_BF16), hr), :]

    def half_shape(kind, shape):
        if kind == "row":
            return (shape[0], shape[1], shape[2] // 2, shape[3])
        return (shape[0], shape[1] // 2, shape[2])

    def body(*refs):
        g, own, got = refs[:nw], refs[nw:2 * nw], refs[2 * nw:3 * nw]
        ssem, rsem, lsem = refs[3 * nw:]
        x, y, c = _mesh_pos()
        sibling = (x, y, 1 - c)
        cps, locs = [], []
        for w in range(nw):
            cp = _remote(half_all(g[w], kinds[w], 1 - c), got[w], ssem.at[w], rsem.at[w], sibling)
            cp.start()
            cps.append(cp)
            lc = pltpu.make_async_copy(half_all(g[w], kinds[w], c), own[w], lsem.at[w])
            lc.start()
            locs.append(lc)
        for cp in cps:
            cp.wait_recv()
        for cp in cps:
            cp.wait_send()
        for lc in locs:
            lc.wait()

    shapes = [jax.ShapeDtypeStruct(half_shape(k, a.shape), a.dtype) for k, a in zip(kinds, fulls)]
    outs = pl.pallas_call(
        body, in_specs=[ANY] * nw, out_specs=[ANY] * (2 * nw), out_shape=shapes + shapes,
        scratch_shapes=[pltpu.SemaphoreType.DMA((nw,))] * 3, name="rs_pair",
    )(*fulls)
    return outs[:nw], outs[nw:]


def _rs_chips(parts, kinds):
    nw = len(parts)

    def slot(ref, kind, k):
        if kind == "row":
            return ref.at[:, k]
        sc = ref.shape[2] // N_CHIPS
        return ref.at[:, :, pl.ds(pl.multiple_of(k * sc, LANES), sc)]

    def slot_shape(kind, shape):
        if kind == "row":
            return (shape[0], shape[2], shape[3])
        return (shape[0], shape[1], shape[2] // N_CHIPS)

    def body(*refs):
        t, rb = refs[:nw], refs[nw:2 * nw]
        ssem, rsem, lsem = refs[2 * nw:]
        x, y, c = _mesh_pos()
        k_me = 2 * x + y
        peers = _chip_peers(x, y, c)
        cps, locs = [], []
        for w in range(nw):
            lc = pltpu.make_async_copy(slot(t[w], kinds[w], k_me), rb[w].at[0], lsem.at[w])
            lc.start()
            locs.append(lc)
            for r, (kj, dev) in enumerate(peers):
                cp = _remote(slot(t[w], kinds[w], kj), rb[w].at[r + 1], ssem.at[3 * w + r], rsem.at[3 * w + r], dev)
                cp.start()
                cps.append(cp)
        for cp in cps:
            cp.wait_recv()
        for cp in cps:
            cp.wait_send()
        for lc in locs:
            lc.wait()

    shapes = [jax.ShapeDtypeStruct((N_CHIPS,) + slot_shape(k, a.shape), a.dtype) for k, a in zip(kinds, parts)]
    return pl.pallas_call(
        body, in_specs=[ANY] * nw, out_specs=[ANY] * nw, out_shape=shapes,
        scratch_shapes=[pltpu.SemaphoreType.DMA((3 * nw,))] * 2 + [pltpu.SemaphoreType.DMA((nw,))],
        name="rs_chips",
    )(*parts)


def _rs_join(halves):
    nw = len(halves)

    def body(*refs):
        src, dst = refs[:nw], refs[nw:2 * nw]
        ssem, rsem, lsem = refs[2 * nw:]
        x, y, c = _mesh_pos()
        sibling = (x, y, 1 - c)
        cps, locs = [], []
        for w in range(nw):
            hr = src[w].shape[1]
            cp = _remote(src[w], _shard_half(dst[w], c, hr), ssem.at[w], rsem.at[w], sibling)
            cp.start()
            cps.append(cp)
            lc = pltpu.make_async_copy(src[w], _shard_half(dst[w], c, hr), lsem.at[w])
            lc.start()
            locs.append(lc)
        for w, cp in enumerate(cps):
            hr = src[w].shape[1]
            _remote(src[w], _shard_half(dst[w], 1 - c, hr), ssem.at[w], rsem.at[w], sibling).wait_recv()
        for cp in cps:
            cp.wait_send()
        for lc in locs:
            lc.wait()

    shapes = [jax.ShapeDtypeStruct((a.shape[0], 2 * a.shape[1], a.shape[2]), a.dtype) for a in halves]
    return pl.pallas_call(
        body, in_specs=[ANY] * nw, out_specs=[ANY] * nw, out_shape=shapes,
        scratch_shapes=[pltpu.SemaphoreType.DMA((nw,))] * 3, name="rs_join",
    )(*halves)


def _allgather_small(buf, name):
    def body(in_ref, out_ref, ssem, rsem, lsem):
        x, y, c = _mesh_pos()
        me = 4 * x + 2 * y + c
        lc = pltpu.make_async_copy(in_ref, out_ref.at[me], lsem)
        lc.start()
        cps, waits = [], []
        for r in range(1, N_DEV):
            px = 1 - x if r & 4 else x
            py = 1 - y if r & 2 else y
            pc = 1 - c if r & 1 else c
            cp = _remote(in_ref, out_ref.at[me], ssem.at[r - 1], rsem.at[r - 1], (px, py, pc))
            cp.start()
            cps.append(cp)
            waits.append(_remote(in_ref, out_ref.at[4 * px + 2 * py + pc], ssem.at[r - 1], rsem.at[r - 1], (px, py, pc)))
        for wt in waits:
            wt.wait_recv()
        for cp in cps:
            cp.wait_send()
        lc.wait()

    return pl.pallas_call(
        body, in_specs=[ANY], out_specs=ANY,
        out_shape=jax.ShapeDtypeStruct((N_DEV,) + buf.shape, buf.dtype),
        scratch_shapes=[pltpu.SemaphoreType.DMA((N_DEV - 1,))] * 2 + [pltpu.SemaphoreType.DMA],
        name=name,
    )(buf)


def _pack(arrs):
    flat = jnp.concatenate([a.reshape(-1).astype(F32) for a in arrs])
    rows = -(-flat.shape[0] // (LANES * 16)) * 16
    return jnp.pad(flat, (0, rows * LANES - flat.shape[0])).reshape(rows, LANES)


def _unpack(buf, shapes):
    flat = buf.reshape(-1)
    out, off = [], 0
    for shp in shapes:
        nel = math.prod(shp)
        out.append(flat[off:off + nel].reshape(shp))
        off += nel
    return out


BIG = ("a_w_in", "a_w_out", "b_w_in", "b_w_out", "c_w_in", "c_w_grp", "c_w_out", "f_w_up", "f_w_down")
BIG_KIND = {"a_w_in": "col", "a_w_out": "row", "b_w_in": "col", "b_w_out": "row", "c_w_in": "row",
            "c_w_grp": "row", "c_w_out": "row", "f_w_up": "col", "f_w_down": "row"}
SHARDED_SMALL = ("a_dw", "a_dw_b", "a_ln_g", "a_ln_b", "c_scale", "f_dw")
REPLICATED = ("b_ln_g", "b_ln_b", "b_ws", "b_bs", "ln1_g", "ln1_b", "ln2_g", "ln2_b")
WEIGHTS = ("a_w_in", "a_dw", "a_dw_b", "a_ln_g", "a_ln_b", "a_w_out", "b_w_in", "b_ln_g", "b_ln_b", "b_ws", "b_bs",
           "b_w_out", "c_w_in", "c_w_grp", "c_scale", "c_w_out", "f_w_up", "f_dw", "f_w_down",
           "ln1_g", "ln1_b", "ln2_g", "ln2_b")


def _as3d(a):
    return a.reshape((-1,) + a.shape[-2:])


def kernel(x, a_w_in, a_dw, a_dw_b, a_ln_g, a_ln_b, a_w_out, b_w_in, b_ln_g, b_ln_b, b_ws, b_bs, b_w_out, c_w_in, c_w_grp, c_scale, c_w_out, f_w_up, f_dw, f_w_down, ln1_g, ln1_b, ln2_g, ln2_b, loss_target, m_a_w_in, m_a_dw, m_a_dw_b, m_a_ln_g, m_a_ln_b, m_a_w_out, m_b_w_in, m_b_ln_g, m_b_ln_b, m_b_ws, m_b_bs, m_b_w_out, m_c_w_in, m_c_w_grp, m_c_scale, m_c_w_out, m_f_w_up, m_f_dw, m_f_w_down, m_ln1_g, m_ln1_b, m_ln2_g, m_ln2_b, v_a_w_in, v_a_dw, v_a_dw_b, v_a_ln_g, v_a_ln_b, v_a_w_out, v_b_w_in, v_b_ln_g, v_b_ln_b, v_b_ws, v_b_bs, v_b_w_out, v_c_w_in, v_c_w_grp, v_c_scale, v_c_w_out, v_f_w_up, v_f_dw, v_f_w_down, v_ln1_g, v_ln1_b, v_ln2_g, v_ln2_b):
    w = dict(a_w_in=a_w_in, a_dw=a_dw, a_dw_b=a_dw_b, a_ln_g=a_ln_g, a_ln_b=a_ln_b, a_w_out=a_w_out, b_w_in=b_w_in, b_ln_g=b_ln_g, b_ln_b=b_ln_b, b_ws=b_ws, b_bs=b_bs, b_w_out=b_w_out, c_w_in=c_w_in, c_w_grp=c_w_grp, c_scale=c_scale, c_w_out=c_w_out, f_w_up=f_w_up, f_dw=f_dw, f_w_down=f_w_down, ln1_g=ln1_g, ln1_b=ln1_b, ln2_g=ln2_g, ln2_b=ln2_b)
    mom = dict(a_w_in=m_a_w_in, a_dw=m_a_dw, a_dw_b=m_a_dw_b, a_ln_g=m_a_ln_g, a_ln_b=m_a_ln_b, a_w_out=m_a_w_out, b_w_in=m_b_w_in, b_ln_g=m_b_ln_g, b_ln_b=m_b_ln_b, b_ws=m_b_ws, b_bs=m_b_bs, b_w_out=m_b_w_out, c_w_in=m_c_w_in, c_w_grp=m_c_w_grp, c_scale=m_c_scale, c_w_out=m_c_w_out, f_w_up=m_f_w_up, f_dw=m_f_dw, f_w_down=m_f_w_down, ln1_g=m_ln1_g, ln1_b=m_ln1_b, ln2_g=m_ln2_g, ln2_b=m_ln2_b)
    var = dict(a_w_in=v_a_w_in, a_dw=v_a_dw, a_dw_b=v_a_dw_b, a_ln_g=v_a_ln_g, a_ln_b=v_a_ln_b, a_w_out=v_a_w_out, b_w_in=v_b_w_in, b_ln_g=v_b_ln_g, b_ln_b=v_b_ln_b, b_ws=v_b_ws, b_bs=v_b_bs, b_w_out=v_b_w_out, c_w_in=v_c_w_in, c_w_grp=v_c_w_grp, c_scale=v_c_scale, c_w_out=v_c_w_out, f_w_up=v_f_w_up, f_dw=v_f_dw, f_w_down=v_f_w_down, ln1_g=v_ln1_g, ln1_b=v_ln1_b, ln2_g=v_ln2_g, ln2_b=v_ln2_b)

    depth = ln1_g.shape[0]
    d = x.shape[-1]
    alpha = float((2 * depth) ** 0.25)
    chip = 2 * lax.axis_index("x") + lax.axis_index("y")

    kinds = [BIG_KIND[k] for k in BIG]
    shards_bf = [_cast_bf16(_as3d(w[k]), "cast_" + k) for k in BIG]
    gathered = _allgather_weights(shards_bf, kinds)
    full = {}
    for k, kind, arr in zip(BIG, kinds, gathered):
        full[k] = arr.reshape(arr.shape[0], -1, arr.shape[-1]) if kind == "row" else arr

    small_all = _allgather_small(_pack([w[k] for k in SHARDED_SMALL]), "allgather_small_params")
    per_chip = [_unpack(small_all[2 * k], [w[n].shape for n in SHARDED_SMALL]) for k in range(N_CHIPS)]
    fs = {n: jnp.concatenate([per_chip[k][i] for k in range(N_CHIPS)], axis=-1) for i, n in enumerate(SHARDED_SMALL)}

    nh = b_ws.shape[1]
    tril = jnp.tril(jnp.ones((CHUNK, CHUNK), F32))
    wm = (b_ws[0] * tril).astype(BF16)
    wmt = jnp.swapaxes(wm, 1, 2)
    bs_exp = jnp.repeat(jnp.transpose(b_bs[0]), CHUNK, axis=1)
    ng = c_w_grp.shape[1]
    wgrp = full["c_w_grp"]

    xh, g, b = x[0], jnp.ones((1, d), F32), jnp.zeros((1, d), F32)
    saved = []
    for i in range(depth):
        kind, j = i % 3, i // 3
        rec = dict(xin=xh, gin=g, bin=b)
        if kind == 0:
            xh1, rstd1, p, chat, rstdc = _conv_fwd(
                xh, g, b, full["a_w_in"], j, fs["a_dw"], fs["a_dw_b"][j:j + 1], fs["a_ln_g"][j:j + 1],
                fs["a_ln_b"][j:j + 1], full["a_w_out"], alpha, f"conv_fwd_{i}")
            rec.update(p=p, chat=chat, rstdc=rstdc)
        elif kind == 1:
            xh1, rstd1, zp = _sgu_fwd(xh, g, b, full["b_w_in"], b_ln_g, b_ln_b, wm, bs_exp, full["b_w_out"],
                                      alpha, f"sgu_fwd_{i}")
            rec.update(zp=zp)
        else:
            xh1, rstd1, ys = _pool_fwd(xh, g, b, full["c_w_in"], wgrp, fs["c_scale"], full["c_w_out"],
                                       alpha, f"pool_fwd_{i}")
            rec.update(ys=ys)
        xh2, rstd2, hs = _ffn_fwd(xh1, ln1_g[i:i + 1], ln1_b[i:i + 1], full["f_w_up"], i, fs["f_dw"],
                                  full["f_w_down"], alpha, f"ffn_fwd_{i}")
        rec.update(xh1=xh1, rstd1=rstd1, xh2=xh2, rstd2=rstd2, hs=hs)
        saved.append(rec)
        xh, g, b = xh2, ln2_g[i:i + 1], ln2_b[i:i + 1]

    dxo, loss_part = _loss_head(xh, g, b, loss_target[0], "loss_head")
    loss = lax.psum(loss_part[0, 0], ("x", "y", "c"))

    gbuf = {k: jnp.zeros(full[k].shape, F32) for k in BIG if k != "c_w_grp"}
    gs = {k: [None] * w[k].shape[0] for k in ("a_dw", "a_dw_b", "a_ln_g", "a_ln_b", "f_dw", "ln1_g", "ln1_b", "ln2_g", "ln2_b")}
    for i in reversed(range(depth)):
        kind, j = i % 3, i // 3
        rec = saved[i]
        dr2, dh, gbuf["f_w_down"], gs["f_dw"][i], gs["ln2_g"][i], gs["ln2_b"][i] = _ffn_bwd1(
            dxo, rec["xh2"], rec["rstd2"], ln2_g[i:i + 1], rec["hs"], full["f_w_down"], i, fs["f_dw"],
            gbuf["f_w_down"], f"ffn_bwd1_{i}")
        dr1, gs["ln1_g"][i], gs["ln1_b"][i] = _bwd_in(
            dh, dr2, full["f_w_up"], i, alpha, f"ffn_bwd2_{i}", ln=(rec["xh1"], rec["rstd1"], ln1_g[i:i + 1]))
        gbuf["f_w_up"] = _mm_tn(rec["xh1"], ln1_g[i:i + 1], ln1_b[i:i + 1], dh, gbuf["f_w_up"], i, f"grad_w_up_{i}")
        if kind == 0:
            dp, gbuf["a_w_out"], gs["a_dw"][j], gs["a_dw_b"][j], gs["a_ln_g"][j], gs["a_ln_b"][j] = _conv_bwd1(
                dr1, rec["chat"], rec["rstdc"], rec["p"], full["a_w_out"], j, fs["a_dw"], fs["a_ln_g"][j:j + 1],
                fs["a_ln_b"][j:j + 1], gbuf["a_w_out"], f"conv_bwd1_{i}")
            win_name, lidx = "a_w_in", j
        elif kind == 1:
            dp, gbuf["b_w_out"], g_ws, g_bs_t, g_blg, g_blb = _sgu_bwd1(
                dr1, rec["zp"], full["b_w_out"], b_ln_g, b_ln_b, wm, wmt, bs_exp, gbuf["b_w_out"], f"sgu_bwd1_{i}")
            win_name, lidx = "b_w_in", 0
        else:
            dp, gbuf["c_w_out"], g_wgrp, g_cscale = _pool_bwd1(
                dr1, rec["ys"], full["c_w_out"], wgrp, fs["c_scale"], gbuf["c_w_out"], f"pool_bwd1_{i}")
            win_name, lidx = "c_w_in", 0
        dxo = _bwd_in(dp, dr1, full[win_name], lidx, alpha, f"mixer_bwd2_{i}")
        gbuf[win_name] = _mm_tn(rec["xin"], rec["gin"], rec["bin"], dp, gbuf[win_name], lidx, f"grad_w_in_{i}")
    grad_x = dxo[None]

    gfull = []
    for k, kind in zip(BIG, kinds):
        a = g_wgrp if k == "c_w_grp" else gbuf[k]
        gfull.append(a.reshape(a.shape[0], N_CHIPS, -1, a.shape[-1]) if kind == "row" else a)
    own, got = _rs_pair(gfull, kinds)
    pair_sum = [_add(o, t, "rs_pair_sum_" + k) for k, o, t in zip(BIG, own, got)]
    from_chips = _rs_chips(pair_sum, kinds)
    half_sum = [_sum_lead(a, "rs_chip_sum_" + k) for k, a in zip(BIG, from_chips)]
    joined = _rs_join(half_sum)
    grads = {k: a.reshape(w[k].shape) for k, a in zip(BIG, joined)}

    small_full = {
        "a_dw": jnp.stack(gs["a_dw"]), "a_dw_b": jnp.concatenate(gs["a_dw_b"]), "a_ln_g": jnp.concatenate(gs["a_ln_g"]),
        "a_ln_b": jnp.concatenate(gs["a_ln_b"]), "c_scale": g_cscale, "f_dw": jnp.stack(gs["f_dw"]),
        "b_ln_g": g_blg, "b_ln_b": g_blb, "b_ws": g_ws[None], "b_bs": jnp.transpose(g_bs_t)[None],
        "ln1_g": jnp.concatenate(gs["ln1_g"]), "ln1_b": jnp.concatenate(gs["ln1_b"]),
        "ln2_g": jnp.concatenate(gs["ln2_g"]), "ln2_b": jnp.concatenate(gs["ln2_b"]),
    }
    small_names = SHARDED_SMALL + REPLICATED
    small_shapes = [small_full[n].shape for n in small_names]
    gathered_small = _allgather_small(_pack([small_full[n] for n in small_names]), "allgather_small_grads")
    summed = _unpack(_sum_lead(gathered_small, "small_grad_sum"), small_shapes)
    for n, a in zip(small_names, summed):
        if n in SHARDED_SMALL:
            cs = w[n].shape[-1]
            a = lax.dynamic_slice_in_dim(a, chip * cs, cs, axis=a.ndim - 1)
        grads[n] = a

    delta, new_m, new_v = {}, {}, {}
    for k in BIG:
        delta[k], new_m[k], new_v[k] = _adamw(w[k], grads[k], mom[k], var[k], "adamw_" + k)
    shapes = [w[n].shape for n in small_names]
    ds_, ms_, vs_ = _adamw(_pack([w[n] for n in small_names]), _pack([grads[n] for n in small_names]),
                           _pack([mom[n] for n in small_names]), _pack([var[n] for n in small_names]), "adamw_small")
    for n, a, bb, cc in zip(small_names, _unpack(ds_, shapes), _unpack(ms_, shapes), _unpack(vs_, shapes)):
        delta[n], new_m[n], new_v[n] = a, bb, cc

    return (loss, grad_x, *[grads[n] for n in WEIGHTS], *[delta[n] for n in WEIGHTS],
            *[new_m[n] for n in WEIGHTS], *[new_v[n] for n in WEIGHTS])
```

```python
import math

import jax
import jax.numpy as jnp
from jax import lax
from jax.experimental import pallas as pl
from jax.experimental.pallas import tpu as pltpu

F32 = jnp.float32
BF16 = jnp.bfloat16

LN_EPS = 1e-5
POOL_WINDOWS = (2, 4, 8, 16)
CHUNK = 128
ADAM_LR = 0.001
ADAM_B1 = 0.9
ADAM_B2 = 0.999
ADAM_EPS = 1e-08
ADAM_WD = 0.01
ADAM_STEP = 10

LANES = 128
SUBLANES_BF16 = 16
N_CHIPS = 4
N_DEV = 8
VMEM_LIMIT = 60 * 1024 * 1024

TM_FFN = 512
TM_CONV = 256
TM_SGU = 512
TM_POOL = 512
TM_BWD_IN = 512
TS_MM_TN = 1024
CW_FFN = 256
CONV_HALO = 32
POOL_HALO = 16
FFN_HALO = 16

MESH = pl.DeviceIdType.MESH


def _cparams(n_grid=1, parallel=False):
    sem = ("parallel" if parallel else "arbitrary",) * n_grid
    return pltpu.CompilerParams(dimension_semantics=sem, vmem_limit_bytes=VMEM_LIMIT)


def _resident(block, imap):
    return pl.BlockSpec(block, imap, pipeline_mode=pl.Buffered(1))


def _wspec(w, l):
    _, r, c = w.shape
    return _resident((None, r, c), lambda *_: (l, 0, 0))


def _rowspec(d):
    return pl.BlockSpec((1, d), lambda *_: (0, 0))


def _dot(a, b):
    return jnp.dot(a, b, preferred_element_type=F32)


def _dot_nt(a, b):
    return lax.dot_general(a, b, (((1,), (1,)), ((), ())), preferred_element_type=F32)


def _dot_tn(a, b):
    return lax.dot_general(a, b, (((0,), (0,)), ((), ())), preferred_element_type=F32)


def _sigmoid(x):
    return jax.nn.sigmoid(x)


def _ln_stats(r):
    mu = jnp.mean(r, axis=1, keepdims=True)
    xc = r - mu
    var = jnp.mean(xc * xc, axis=1, keepdims=True)
    rstd = lax.rsqrt(var + LN_EPS)
    return xc * rstd, rstd


def _ln_bwd(dy, xhat, rstd, g):
    dxh = dy * g
    m1 = jnp.mean(dxh, axis=1, keepdims=True)
    m2 = jnp.mean(dxh * xhat, axis=1, keepdims=True)
    return rstd * (dxh - m1 - xhat * m2)


def _colsum(v):
    return jnp.sum(v, axis=0, keepdims=True)


def _gelu(z):
    return 0.5 * z * (1.0 + lax.erf(z * (1.0 / math.sqrt(2.0))))


def _gelu_grad(z):
    cdf = 0.5 * (1.0 + lax.erf(z * (1.0 / math.sqrt(2.0))))
    pdf = jnp.exp(-0.5 * z * z) * (1.0 / math.sqrt(2.0 * math.pi))
    return cdf + z * pdf


def _shift_down(v, k, prev_rows):
    rolled = pltpu.roll(v, k, 0)
    head = rolled[0:8]
    rows = lax.broadcasted_iota(jnp.int32, head.shape, 0)
    for r in range(k):
        head = jnp.where(rows == r, prev_rows[k - 1 - r], head)
    return jnp.concatenate([head, rolled[8:]], axis=0)


def _shift_up(v, k, next_rows):
    tm = v.shape[0]
    rolled = pltpu.roll(v, tm - k, 0)
    tail = rolled[tm - 8:tm]
    rows = lax.broadcasted_iota(jnp.int32, tail.shape, 0)
    for r in range(k):
        tail = jnp.where(rows == 8 - k + r, next_rows[r], tail)
    return jnp.concatenate([rolled[0:tm - 8], tail], axis=0)


def _fill_shifted(base_scr, sh_scr):
    nrows = sh_scr.shape[1]
    for r in range(1, 8):
        sh_scr[r - 1, :, :] = base_scr[pl.ds(r, nrows), :]


def _tap(base_scr, sh_scr, off, tm):
    q, r = divmod(off, 8)
    if r == 0:
        return base_scr[pl.ds(8 * q, tm), :]
    return sh_scr[r - 1, pl.ds(8 * q, tm), :]


def _pick_rows(r, c, itemsize, cap_bytes):
    best = None
    for t in range(16, r + 1, 16):
        if r % t == 0 and t * c * itemsize <= cap_bytes:
            best = t
    return best if best is not None else r


def _ffn_conv_cols(h, dw_ref, c0, cw, prev1, prev2):
    kw = dw_ref.shape[0]
    h1 = _shift_down(h, 1, [prev1])
    h2 = _shift_down(h, 2, [prev1, prev2])
    hc = dw_ref[kw - 1:kw, c0:c0 + cw] * h + dw_ref[kw - 2:kw - 1, c0:c0 + cw] * h1 + dw_ref[kw - 3:kw - 2, c0:c0 + cw] * h2
    return hc, h1, h2


def _ffn_fwd(xh1, g1, b1, wup, l, fdw, wdn, alpha, name):
    s, d = xh1.shape
    f2 = wup.shape[2]
    f = f2 // 2
    tm = min(TM_FFN, s)
    cw = min(CW_FFN, f)
    n, nck = s // tm, f // cw
    assert fdw.shape[1] == 3 and s % tm == 0 and f % cw == 0

    def body(xh_ref, g_ref, b_ref, wup_ref, dw_ref, wdn_ref, xo_ref, rs_ref, hs_ref, carry):
        @pl.when(pl.program_id(0) == 0)
        def _():
            carry[...] = jnp.zeros_like(carry)

        x1 = xh_ref[...] * g_ref[...] + b_ref[...]
        xb = x1.astype(BF16)
        o = jnp.zeros((tm, d), F32)
        for j in range(nck):
            parts = []
            for half in range(2):
                c0 = half * f + j * cw
                h = _dot(xb, wup_ref[:, c0:c0 + cw])
                hs_ref[:, c0:c0 + cw] = h.astype(BF16)
                hc, _, _ = _ffn_conv_cols(h, dw_ref, c0, cw, carry[7:8, c0:c0 + cw], carry[6:7, c0:c0 + cw])
                carry[:, c0:c0 + cw] = h[tm - 8:tm, :]
                parts.append(hc)
            gg, vv = parts
            a = (gg * _sigmoid(gg) * vv).astype(BF16)
            o = o + _dot(a, wdn_ref[j * cw:(j + 1) * cw, :])
        xhat, rstd = _ln_stats(alpha * x1 + o)
        xo_ref[...] = xhat
        rs_ref[...] = rstd

    tile = pl.BlockSpec((tm, d), lambda i: (i, 0))
    return pl.pallas_call(
        body, grid=(n,),
        in_specs=[tile, _rowspec(d), _rowspec(d), _wspec(wup, l),
                  pl.BlockSpec((None, 3, f2), lambda i: (l, 0, 0)), _wspec(wdn, l)],
        out_specs=[tile, pl.BlockSpec((tm, 1), lambda i: (i, 0)), pl.BlockSpec((tm, f2), lambda i: (i, 0))],
        out_shape=[jax.ShapeDtypeStruct((s, d), F32), jax.ShapeDtypeStruct((s, 1), F32),
                   jax.ShapeDtypeStruct((s, f2), BF16)],
        scratch_shapes=[pltpu.VMEM((8, f2), F32)],
        compiler_params=_cparams(), name=name,
    )(xh1, g1, b1, wup, fdw, wdn)


def _ffn_bwd1(dx2, xh2, rstd2, g2, hs, wdn, l, fdw, gwdn_buf, name):
    s, d = dx2.shape
    f2 = hs.shape[1]
    f = f2 // 2
    tm = min(TM_FFN, s)
    cw = min(CW_FFN, f)
    n, nck = s // tm, f // cw
    hb = FFN_HALO

    def body(dx_ref, xh_ref, rs_ref, g_ref, hs_ref, halo_ref, wdn_ref, dw_ref, buf_ref,
             dr_ref, dh_ref, gwdn_ref, gdw_ref, gg_ref, gb_ref, carry):
        i = pl.program_id(0)
        t = n - 1 - i

        @pl.when(i == 0)
        def _():
            carry[...] = jnp.zeros_like(carry)
            gwdn_ref[...] = jnp.zeros_like(gwdn_ref)
            gdw_ref[...] = jnp.zeros_like(gdw_ref)
            gg_ref[...] = jnp.zeros_like(gg_ref)
            gb_ref[...] = jnp.zeros_like(gb_ref)

        dx = dx_ref[...]
        xh = xh_ref[...]
        gg_ref[...] += _colsum(dx * xh)
        gb_ref[...] += _colsum(dx)
        dr = _ln_bwd(dx, xh, rs_ref[...], g_ref[...])
        dr_ref[...] = dr
        dob = dr.astype(BF16)
        has_prev = t > 0
        for j in range(nck):
            hc, hh = [], []
            for half in range(2):
                c0 = half * f + j * cw
                h = hs_ref[:, c0:c0 + cw].astype(F32)
                hal = jnp.where(has_prev, halo_ref[:, c0:c0 + cw].astype(F32), 0.0)
                c, h1, h2 = _ffn_conv_cols(h, dw_ref, c0, cw, hal[hb - 1:hb], hal[hb - 2:hb - 1])
                hc.append(c)
                hh.append((h, h1, h2))
            gt, vv = hc
            sg = _sigmoid(gt)
            sl = gt * sg
            a = (sl * vv).astype(BF16)
            gwdn_ref[j * cw:(j + 1) * cw, :] += _dot_tn(a, dob)
            da = _dot_nt(dob, wdn_ref[j * cw:(j + 1) * cw, :])
            dhc = (da * vv * (sg * (1.0 + gt * (1.0 - sg))), da * sl)
            for half in range(2):
                c0 = half * f + j * cw
                dc = dhc[half]
                h, h1, h2 = hh[half]
                gdw_ref[2:3, c0:c0 + cw] += _colsum(dc * h)
                gdw_ref[1:2, c0:c0 + cw] += _colsum(dc * h1)
                gdw_ref[0:1, c0:c0 + cw] += _colsum(dc * h2)
                nxt = [carry[0:1, c0:c0 + cw], carry[1:2, c0:c0 + cw]]
                u1 = _shift_up(dc, 1, nxt[:1])
                u2 = _shift_up(dc, 2, nxt)
                dh = dw_ref[2:3, c0:c0 + cw] * dc + dw_ref[1:2, c0:c0 + cw] * u1 + dw_ref[0:1, c0:c0 + cw] * u2
                dh_ref[:, c0:c0 + cw] = dh.astype(BF16)
                carry[:, c0:c0 + cw] = dc[0:8, :]

    tile = pl.BlockSpec((tm, d), lambda i: (n - 1 - i, 0))
    halo_blocks = tm // hb
    nl = gwdn_buf.shape[0]
    outs = pl.pallas_call(
        body, grid=(n,),
        in_specs=[tile, tile, pl.BlockSpec((tm, 1), lambda i: (n - 1 - i, 0)), _rowspec(d),
                  pl.BlockSpec((tm, f2), lambda i: (n - 1 - i, 0)),
                  pl.BlockSpec((hb, f2), lambda i: (jnp.maximum((n - 1 - i) * halo_blocks - 1, 0), 0)),
                  _wspec(wdn, l), pl.BlockSpec((None, 3, f2), lambda i: (l, 0, 0)),
                  pl.BlockSpec(memory_space=pl.ANY)],
        out_specs=[tile, pl.BlockSpec((tm, f2), lambda i: (n - 1 - i, 0)),
                   pl.BlockSpec((None, f, d), lambda i: (l, 0, 0)),
                   pl.BlockSpec((3, f2), lambda i: (0, 0)), _rowspec(d), _rowspec(d)],
        out_shape=[jax.ShapeDtypeStruct((s, d), F32), jax.ShapeDtypeStruct((s, f2), BF16),
                   jax.ShapeDtypeStruct((nl, f, d), F32), jax.ShapeDtypeStruct((3, f2), F32),
                   jax.ShapeDtypeStruct((1, d), F32), jax.ShapeDtypeStruct((1, d), F32)],
        scratch_shapes=[pltpu.VMEM((8, f2), F32)],
        input_output_aliases={8: 2},
        compiler_params=_cparams(), name=name,
    )(dx2, xh2, rstd2, g2, hs, hs, wdn, fdw, gwdn_buf)
    return outs


def _bwd_in(dp, dres, w, l, alpha, name, ln=None):
    s, d = dres.shape
    nn = dp.shape[1]
    tm = min(TM_BWD_IN, s)
    n = s // tm
    tile = pl.BlockSpec((tm, d), lambda i: (i, 0))

    if ln is None:
        def body(dp_ref, dres_ref, w_ref, o_ref):
            o_ref[...] = alpha * dres_ref[...] + _dot_nt(dp_ref[...], w_ref[...])

        return pl.pallas_call(
            body, grid=(n,),
            in_specs=[pl.BlockSpec((tm, nn), lambda i: (i, 0)), tile, _wspec(w, l)],
            out_specs=tile, out_shape=jax.ShapeDtypeStruct((s, d), F32),
            compiler_params=_cparams(parallel=True), name=name,
        )(dp, dres, w)

    xh, rstd, g = ln

    def body_ln(dp_ref, dres_ref, w_ref, xh_ref, rs_ref, g_ref, o_ref, gg_ref, gb_ref):
        @pl.when(pl.program_id(0) == 0)
        def _():
            gg_ref[...] = jnp.zeros_like(gg_ref)
            gb_ref[...] = jnp.zeros_like(gb_ref)

        dx = alpha * dres_ref[...] + _dot_nt(dp_ref[...], w_ref[...])
        xhv = xh_ref[...]
        gg_ref[...] += _colsum(dx * xhv)
        gb_ref[...] += _colsum(dx)
        o_ref[...] = _ln_bwd(dx, xhv, rs_ref[...], g_ref[...])

    return pl.pallas_call(
        body_ln, grid=(n,),
        in_specs=[pl.BlockSpec((tm, nn), lambda i: (i, 0)), tile, _wspec(w, l), tile,
                  pl.BlockSpec((tm, 1), lambda i: (i, 0)), _rowspec(d)],
        out_specs=[tile, _rowspec(d), _rowspec(d)],
        out_shape=[jax.ShapeDtypeStruct((s, d), F32), jax.ShapeDtypeStruct((1, d), F32),
                   jax.ShapeDtypeStruct((1, d), F32)],
        compiler_params=_cparams(), name=name,
    )(dp, dres, w, xh, rstd, g)


def _mm_tn(a, ga, ba, bm, buf, l, name):
    s, k = a.shape
    nn = bm.shape[1]
    ts = min(TS_MM_TN, s)
    tn = nn // N_CHIPS if nn > 1024 else nn
    nj, ns = nn // tn, s // ts

    def body(a_ref, g_ref, b_ref, bm_ref, buf_ref, o_ref):
        @pl.when(pl.program_id(1) == 0)
        def _():
            o_ref[...] = jnp.zeros_like(o_ref)

        ab = (a_ref[...] * g_ref[...] + b_ref[...]).astype(BF16)
        o_ref[...] += _dot_tn(ab, bm_ref[...])

    return pl.pallas_call(
        body, grid=(nj, ns),
        in_specs=[pl.BlockSpec((ts, k), lambda j, t: (t, 0)), _rowspec(k), _rowspec(k),
                  pl.BlockSpec((ts, tn), lambda j, t: (t, j)), pl.BlockSpec(memory_space=pl.ANY)],
        out_specs=pl.BlockSpec((None, k, tn), lambda j, t: (l, 0, j)),
        out_shape=jax.ShapeDtypeStruct(buf.shape, F32),
        input_output_aliases={4: 0},
        compiler_params=_cparams(2), name=name,
    )(a, ga, ba, bm, buf)


def _conv_fwd(xin, gin, bin_, win, l, adw, adwb, lng, lnb, wout, alpha, name):
    s, d = xin.shape
    kw = adw.shape[1]
    hb = CONV_HALO
    tm = min(TM_CONV, s)
    n = s // tm
    assert kw - 1 <= hb <= tm

    def body(x_ref, g_ref, b_ref, win_ref, dw_ref, dwb_ref, lng_ref, lnb_ref, wout_ref,
             xo_ref, rs_ref, p_ref, chat_ref, rsc_ref, u_scr, u8_scr):
        @pl.when(pl.program_id(0) == 0)
        def _():
            u_scr[0:hb, :] = jnp.zeros((hb, d), F32)

        x = x_ref[...] * g_ref[...] + b_ref[...]
        pm = _dot(x.astype(BF16), win_ref[...])
        p_ref[...] = pm.astype(BF16)
        u = pm[:, :d] * _sigmoid(pm[:, d:])
        u_scr[hb:hb + tm, :] = u
        _fill_shifted(u_scr, u8_scr)
        acc = dwb_ref[...] + dw_ref[kw - 1:kw, :] * u
        for k in range(kw - 1):
            acc = acc + dw_ref[k:k + 1, :] * _tap(u_scr, u8_scr, hb - (kw - 1) + k, tm)
        u_scr[0:hb, :] = u_scr[tm:tm + hb, :]
        chat, rstdc = _ln_stats(acc)
        chat_ref[...] = chat.astype(BF16)
        rsc_ref[...] = rstdc
        nv = chat * lng_ref[...] + lnb_ref[...]
        sv = (nv * _sigmoid(nv)).astype(BF16)
        xhat, rstd = _ln_stats(alpha * x + _dot(sv, wout_ref[...]))
        xo_ref[...] = xhat
        rs_ref[...] = rstd

    tile = pl.BlockSpec((tm, d), lambda i: (i, 0))
    col = pl.BlockSpec((tm, 1), lambda i: (i, 0))
    return pl.pallas_call(
        body, grid=(n,),
        in_specs=[tile, _rowspec(d), _rowspec(d), _wspec(win, l),
                  pl.BlockSpec((None, kw, d), lambda i: (l, 0, 0)), _rowspec(d), _rowspec(d), _rowspec(d),
                  _wspec(wout, l)],
        out_specs=[tile, col, pl.BlockSpec((tm, 2 * d), lambda i: (i, 0)), tile, col],
        out_shape=[jax.ShapeDtypeStruct((s, d), F32), jax.ShapeDtypeStruct((s, 1), F32),
                   jax.ShapeDtypeStruct((s, 2 * d), BF16), jax.ShapeDtypeStruct((s, d), BF16),
                   jax.ShapeDtypeStruct((s, 1), F32)],
        scratch_shapes=[pltpu.VMEM((tm + hb, d), F32), pltpu.VMEM((7, tm + hb - 8, d), F32)],
        compiler_params=_cparams(), name=name,
    )(xin, gin, bin_, win, adw, adwb, lng, lnb, wout)


def _conv_bwd1(dr1, chat, rstdc, p, wout, l, adw, lng, lnb, gwout_buf, name):
    s, d = dr1.shape
    kw = adw.shape[1]
    hb = CONV_HALO
    tm = min(TM_CONV, s)
    n = s // tm
    halo_blocks = tm // hb

    def body(dr_ref, chat_ref, rsc_ref, p_ref, halo_ref, wout_ref, dw_ref, lng_ref, lnb_ref, buf_ref,
             dp_ref, gwout_ref, gdw_ref, gdwb_ref, glng_ref, glnb_ref, u_scr, dc_scr, u8_scr, dc8_scr):
        i = pl.program_id(0)
        t = n - 1 - i

        @pl.when(i == 0)
        def _():
            dc_scr[tm:tm + hb, :] = jnp.zeros((hb, d), F32)
            gwout_ref[...] = jnp.zeros_like(gwout_ref)
            gdw_ref[...] = jnp.zeros_like(gdw_ref)
            gdwb_ref[...] = jnp.zeros_like(gdwb_ref)
            glng_ref[...] = jnp.zeros_like(glng_ref)
            glnb_ref[...] = jnp.zeros_like(glnb_ref)

        dob = dr_ref[...].astype(BF16)
        chat = chat_ref[...].astype(F32)
        lng = lng_ref[...]
        nv = chat * lng + lnb_ref[...]
        sgn = _sigmoid(nv)
        gwout_ref[...] += _dot_tn((nv * sgn).astype(BF16), dob)
        dn = _dot_nt(dob, wout_ref[...]) * (sgn * (1.0 + nv * (1.0 - sgn)))
        glng_ref[...] += _colsum(dn * chat)
        glnb_ref[...] += _colsum(dn)
        dc = _ln_bwd(dn, chat, rsc_ref[...], lng)
        gdwb_ref[...] += _colsum(dc)

        pm = p_ref[...].astype(F32)
        a = pm[:, :d]
        sg = _sigmoid(pm[:, d:])
        ph = halo_ref[...].astype(F32)
        u_scr[0:hb, :] = jnp.where(t > 0, ph[:, :d] * _sigmoid(ph[:, d:]), 0.0)
        u_scr[hb:hb + tm, :] = a * sg
        dc_scr[0:tm, :] = dc
        _fill_shifted(u_scr, u8_scr)
        _fill_shifted(dc_scr, dc8_scr)
        du = dw_ref[kw - 1:kw, :] * dc
        for k in range(kw):
            gdw_ref[k:k + 1, :] += _colsum(dc * _tap(u_scr, u8_scr, hb - (kw - 1) + k, tm))
            if k < kw - 1:
                du = du + dw_ref[k:k + 1, :] * _tap(dc_scr, dc8_scr, kw - 1 - k, tm)
        dc_scr[tm:tm + hb, :] = dc[0:hb, :]
        dp_ref[:, :d] = (du * sg).astype(BF16)
        dp_ref[:, d:] = (du * a * sg * (1.0 - sg)).astype(BF16)

    tile = pl.BlockSpec((tm, d), lambda i: (n - 1 - i, 0))
    col = pl.BlockSpec((tm, 1), lambda i: (n - 1 - i, 0))
    nl = gwout_buf.shape[0]
    return pl.pallas_call(
        body, grid=(n,),
        in_specs=[tile, tile, col, pl.BlockSpec((tm, 2 * d), lambda i: (n - 1 - i, 0)),
                  pl.BlockSpec((hb, 2 * d), lambda i: (jnp.maximum((n - 1 - i) * halo_blocks - 1, 0), 0)),
                  _wspec(wout, l), pl.BlockSpec((None, kw, d), lambda i: (l, 0, 0)), _rowspec(d), _rowspec(d),
                  pl.BlockSpec(memory_space=pl.ANY)],
        out_specs=[pl.BlockSpec((tm, 2 * d), lambda i: (n - 1 - i, 0)),
                   pl.BlockSpec((None, d, d), lambda i: (l, 0, 0)),
                   pl.BlockSpec((kw, d), lambda i: (0, 0)), _rowspec(d), _rowspec(d), _rowspec(d)],
        out_shape=[jax.ShapeDtypeStruct((s, 2 * d), BF16), jax.ShapeDtypeStruct((nl, d, d), F32),
                   jax.ShapeDtypeStruct((kw, d), F32), jax.ShapeDtypeStruct((1, d), F32),
                   jax.ShapeDtypeStruct((1, d), F32), jax.ShapeDtypeStruct((1, d), F32)],
        scratch_shapes=[pltpu.VMEM((tm + hb, d), F32), pltpu.VMEM((tm + hb, d), F32),
                        pltpu.VMEM((7, tm + hb - 8, d), F32), pltpu.VMEM((7, tm + hb - 8, d), F32)],
        input_output_aliases={9: 1},
        compiler_params=_cparams(), name=name,
    )(dr1, chat, rstdc, p, p, wout, adw, lng, lnb, gwout_buf)


def _sgu_gate(vn, wm_ref, bs_ref, s_scr, tm, nh):
    for ch in range(tm // CHUNK):
        r0 = ch * CHUNK
        for h in range(nh):
            c0 = h * CHUNK
            s_scr[r0:r0 + CHUNK, c0:c0 + CHUNK] = (
                _dot(wm_ref[h], vn[r0:r0 + CHUNK, c0:c0 + CHUNK]) + bs_ref[:, c0:c0 + CHUNK])


def _sgu_fwd(xin, gin, bin_, win, lg, lb, wm, bs_exp, wout, alpha, name):
    s, d = xin.shape
    nh = wm.shape[0]
    tm = min(TM_SGU, s)
    n = s // tm
    assert tm % CHUNK == 0 and nh * CHUNK == d

    def body(x_ref, g_ref, b_ref, win_ref, lg_ref, lb_ref, wm_ref, bs_ref, wout_ref,
             xo_ref, rs_ref, zp_ref, s_scr):
        x = x_ref[...] * g_ref[...] + b_ref[...]
        zp = _dot(x.astype(BF16), win_ref[...])
        zp_ref[...] = zp.astype(BF16)
        z = _gelu(zp)
        vhat, _ = _ln_stats(z[:, d:])
        vn = (vhat * lg_ref[...] + lb_ref[...]).astype(BF16)
        _sgu_gate(vn, wm_ref, bs_ref, s_scr, tm, nh)
        q = (z[:, :d] * s_scr[...]).astype(BF16)
        xhat, rstd = _ln_stats(alpha * x + _dot(q, wout_ref[...]))
        xo_ref[...] = xhat
        rs_ref[...] = rstd

    tile = pl.BlockSpec((tm, d), lambda i: (i, 0))
    return pl.pallas_call(
        body, grid=(n,),
        in_specs=[tile, _rowspec(d), _rowspec(d), _wspec(win, 0), _rowspec(d), _rowspec(d),
                  _resident((nh, CHUNK, CHUNK), lambda i: (0, 0, 0)),
                  _resident((CHUNK, d), lambda i: (0, 0)), _wspec(wout, 0)],
        out_specs=[tile, pl.BlockSpec((tm, 1), lambda i: (i, 0)), pl.BlockSpec((tm, 2 * d), lambda i: (i, 0))],
        out_shape=[jax.ShapeDtypeStruct((s, d), F32), jax.ShapeDtypeStruct((s, 1), F32),
                   jax.ShapeDtypeStruct((s, 2 * d), BF16)],
        scratch_shapes=[pltpu.VMEM((tm, d), F32)],
        compiler_params=_cparams(parallel=True), name=name,
    )(xin, gin, bin_, win, lg, lb, wm, bs_exp, wout)


def _sgu_bwd1(dr1, zp, wout, lg, lb, wm, wmt, bs_exp, gwout_buf, name):
    s, d = dr1.shape
    nh = wm.shape[0]
    tm = min(TM_SGU, s)
    n = s // tm

    def body(dr_ref, zp_ref, wout_ref, lg_ref, lb_ref, wm_ref, wmt_ref, bs_ref, buf_ref,
             dzp_ref, gwout_ref, gws_ref, gbs_ref, glg_ref, glb_ref, s_scr, dvn_scr, bs_acc):
        i = pl.program_id(0)

        @pl.when(i == 0)
        def _():
            gwout_ref[...] = jnp.zeros_like(gwout_ref)
            gws_ref[...] = jnp.zeros_like(gws_ref)
            glg_ref[...] = jnp.zeros_like(glg_ref)
            glb_ref[...] = jnp.zeros_like(glb_ref)
            bs_acc[...] = jnp.zeros_like(bs_acc)

        dob = dr_ref[...].astype(BF16)
        zp = zp_ref[...].astype(F32)
        z = _gelu(zp)
        u = z[:, :d]
        lg = lg_ref[...]
        vhat, rstdv = _ln_stats(z[:, d:])
        vn = (vhat * lg + lb_ref[...]).astype(BF16)
        _sgu_gate(vn, wm_ref, bs_ref, s_scr, tm, nh)
        sv = s_scr[...]
        gwout_ref[...] += _dot_tn((u * sv).astype(BF16), dob)
        dq = _dot_nt(dob, wout_ref[...])
        ds = dq * u
        dsb = ds.astype(BF16)
        part = jnp.zeros((CHUNK, d), F32)
        for ch in range(tm // CHUNK):
            r0 = ch * CHUNK
            part = part + ds[r0:r0 + CHUNK, :]
            for h in range(nh):
                c0 = h * CHUNK
                blk = dsb[r0:r0 + CHUNK, c0:c0 + CHUNK]
                gws_ref[h] += _dot_nt(blk, vn[r0:r0 + CHUNK, c0:c0 + CHUNK])
                dvn_scr[r0:r0 + CHUNK, c0:c0 + CHUNK] = _dot(wmt_ref[h], blk)
        bs_acc[...] += part
        dvn = dvn_scr[...]
        glg_ref[...] += _colsum(dvn * vhat)
        glb_ref[...] += _colsum(dvn)
        dv = _ln_bwd(dvn, vhat, rstdv, lg)
        gp = _gelu_grad(zp)
        dzp_ref[:, :d] = (dq * sv * gp[:, :d]).astype(BF16)
        dzp_ref[:, d:] = (dv * gp[:, d:]).astype(BF16)

        @pl.when(i == n - 1)
        def _():
            rows = lax.broadcasted_iota(jnp.int32, (CHUNK, CHUNK), 0)
            cols = lax.broadcasted_iota(jnp.int32, (CHUNK, CHUNK), 1)
            tril = (cols <= rows).astype(F32)
            acc = bs_acc[...]
            for h in range(nh):
                gws_ref[h] = gws_ref[h] * tril
                gbs_ref[:, h:h + 1] = jnp.sum(acc[:, h * CHUNK:(h + 1) * CHUNK], axis=1, keepdims=True)

    tile = pl.BlockSpec((tm, d), lambda i: (i, 0))
    wide = pl.BlockSpec((tm, 2 * d), lambda i: (i, 0))
    hspec = _resident((nh, CHUNK, CHUNK), lambda i: (0, 0, 0))
    return pl.pallas_call(
        body, grid=(n,),
        in_specs=[tile, wide, _wspec(wout, 0), _rowspec(d), _rowspec(d), hspec, hspec,
                  _resident((CHUNK, d), lambda i: (0, 0)), pl.BlockSpec(memory_space=pl.ANY)],
        out_specs=[wide, pl.BlockSpec((None, d, d), lambda i: (0, 0, 0)),
                   pl.BlockSpec((nh, CHUNK, CHUNK), lambda i: (0, 0, 0)),
                   pl.BlockSpec((CHUNK, nh), lambda i: (0, 0)), _rowspec(d), _rowspec(d)],
        out_shape=[jax.ShapeDtypeStruct((s, 2 * d), BF16), jax.ShapeDtypeStruct(gwout_buf.shape, F32),
                   jax.ShapeDtypeStruct((nh, CHUNK, CHUNK), F32), jax.ShapeDtypeStruct((CHUNK, nh), F32),
                   jax.ShapeDtypeStruct((1, d), F32), jax.ShapeDtypeStruct((1, d), F32)],
        scratch_shapes=[pltpu.VMEM((tm, d), F32), pltpu.VMEM((tm, d), F32), pltpu.VMEM((CHUNK, d), F32)],
        input_output_aliases={8: 1},
        compiler_params=_cparams(), name=name,
    )(dr1, zp, wout, lg, lb, wm, wmt, bs_exp, gwout_buf)


def _pool_counts(t0, tm, w):
    pos = t0 + lax.broadcasted_iota(jnp.int32, (tm, 1), 0)
    return jnp.minimum(pos + 1, w).astype(F32)


def _pool_fwd(xin, gin, bin_, win, wg, scale, wout, alpha, name):
    s, d = xin.shape
    ng, dg = wg.shape[0], wg.shape[1]
    hb = POOL_HALO
    tm = min(TM_POOL, s)
    n = s // tm
    assert ng == len(POOL_WINDOWS) and ng * dg == d and max(POOL_WINDOWS) <= hb

    def body(x_ref, g_ref, b_ref, win_ref, wg_ref, sc_ref, wout_ref, xo_ref, rs_ref, ys_ref, y_scr, z_scr):
        i = pl.program_id(0)

        @pl.when(i == 0)
        def _():
            y_scr[0:hb, :] = jnp.zeros((hb, d), F32)

        x = x_ref[...] * g_ref[...] + b_ref[...]
        y = _dot(x.astype(BF16), win_ref[...])
        ys_ref[...] = y.astype(BF16)
        y_scr[hb:hb + tm, :] = y
        for g, w in enumerate(POOL_WINDOWS):
            c0 = g * dg
            acc = y[:, c0:c0 + dg]
            for dd in range(1, w):
                acc = acc + y_scr[pl.ds(hb - dd, tm), c0:c0 + dg]
            pg = acc / _pool_counts(i * tm, tm, w) - y[:, c0:c0 + dg]
            z_scr[:, c0:c0 + dg] = _dot(pg.astype(BF16), wg_ref[g])
        y_scr[0:hb, :] = y_scr[tm:tm + hb, :]
        zz = (z_scr[...] * sc_ref[...]).astype(BF16)
        xhat, rstd = _ln_stats(alpha * x + _dot(zz, wout_ref[...]))
        xo_ref[...] = xhat
        rs_ref[...] = rstd

    tile = pl.BlockSpec((tm, d), lambda i: (i, 0))
    return pl.pallas_call(
        body, grid=(n,),
        in_specs=[tile, _rowspec(d), _rowspec(d), _wspec(win, 0),
                  _resident((ng, dg, dg), lambda i: (0, 0, 0)), _rowspec(d), _wspec(wout, 0)],
        out_specs=[tile, pl.BlockSpec((tm, 1), lambda i: (i, 0)), tile],
        out_shape=[jax.ShapeDtypeStruct((s, d), F32), jax.ShapeDtypeStruct((s, 1), F32),
                   jax.ShapeDtypeStruct((s, d), BF16)],
        scratch_shapes=[pltpu.VMEM((tm + hb, d), F32), pltpu.VMEM((tm, d), F32)],
        compiler_params=_cparams(), name=name,
    )(xin, gin, bin_, win, wg, scale, wout)


def _pool_bwd1(dr1, ys, wout, wg, scale, gwout_buf, name):
    s, d = dr1.shape
    ng, dg = wg.shape[0], wg.shape[1]
    hb = POOL_HALO
    tm = min(TM_POOL, s)
    n = s // tm
    halo_blocks = tm // hb

    def body(dr_ref, ys_ref, halo_ref, wout_ref, wg_ref, sc_ref, buf_ref,
             dy_ref, gwout_ref, gwg_ref, gsc_ref, y_scr, e_scr, z_scr, dp_scr):
        i = pl.program_id(0)
        t = n - 1 - i

        @pl.when(i == 0)
        def _():
            e_scr[tm:tm + hb, :] = jnp.zeros((hb, d), F32)
            gwout_ref[...] = jnp.zeros_like(gwout_ref)
            gwg_ref[...] = jnp.zeros_like(gwg_ref)
            gsc_ref[...] = jnp.zeros_like(gsc_ref)

        dob = dr_ref[...].astype(BF16)
        y = ys_ref[...].astype(F32)
        y_scr[0:hb, :] = jnp.where(t > 0, halo_ref[...].astype(F32), 0.0)
        y_scr[hb:hb + tm, :] = y
        pgs = []
        for g, w in enumerate(POOL_WINDOWS):
            c0 = g * dg
            acc = y[:, c0:c0 + dg]
            for dd in range(1, w):
                acc = acc + y_scr[pl.ds(hb - dd, tm), c0:c0 + dg]
            pg = (acc / _pool_counts(t * tm, tm, w) - y[:, c0:c0 + dg]).astype(BF16)
            pgs.append(pg)
            z_scr[:, c0:c0 + dg] = _dot(pg, wg_ref[g])
        zpre = z_scr[...]
        sc = sc_ref[...]
        gwout_ref[...] += _dot_tn((zpre * sc).astype(BF16), dob)
        dz = _dot_nt(dob, wout_ref[...])
        gsc_ref[...] += _colsum(dz * zpre)
        dzpre = (dz * sc).astype(BF16)
        for g, w in enumerate(POOL_WINDOWS):
            c0 = g * dg
            dzg = dzpre[:, c0:c0 + dg]
            gwg_ref[g] += _dot_tn(pgs[g], dzg)
            dp = _dot_nt(dzg, wg_ref[g])
            dp_scr[:, c0:c0 + dg] = dp
            e_scr[0:tm, c0:c0 + dg] = dp / _pool_counts(t * tm, tm, w)
        for g, w in enumerate(POOL_WINDOWS):
            c0 = g * dg
            acc = e_scr[0:tm, c0:c0 + dg]
            for dd in range(1, w):
                acc = acc + e_scr[pl.ds(dd, tm), c0:c0 + dg]
            dy_ref[:, c0:c0 + dg] = (acc - dp_scr[:, c0:c0 + dg]).astype(BF16)
        e_scr[tm:tm + hb, :] = e_scr[0:hb, :]

    tile = pl.BlockSpec((tm, d), lambda i: (n - 1 - i, 0))
    return pl.pallas_call(
        body, grid=(n,),
        in_specs=[tile, tile,
                  pl.BlockSpec((hb, d), lambda i: (jnp.maximum((n - 1 - i) * halo_blocks - 1, 0), 0)),
                  _wspec(wout, 0), _resident((ng, dg, dg), lambda i: (0, 0, 0)), _rowspec(d),
                  pl.BlockSpec(memory_space=pl.ANY)],
        out_specs=[tile, pl.BlockSpec((None, d, d), lambda i: (0, 0, 0)),
                   pl.BlockSpec((ng, dg, dg), lambda i: (0, 0, 0)), _rowspec(d)],
        out_shape=[jax.ShapeDtypeStruct((s, d), BF16), jax.ShapeDtypeStruct(gwout_buf.shape, F32),
                   jax.ShapeDtypeStruct((ng, dg, dg), F32), jax.ShapeDtypeStruct((1, d), F32)],
        scratch_shapes=[pltpu.VMEM((tm + hb, d), F32), pltpu.VMEM((tm + hb, d), F32),
                        pltpu.VMEM((tm, d), F32), pltpu.VMEM((tm, d), F32)],
        input_output_aliases={6: 1},
        compiler_params=_cparams(), name=name,
    )(dr1, ys, ys, wout, wg, scale, gwout_buf)


def _loss_head(xh, g, b, target, name):
    s, d = xh.shape
    tm = min(512, s)
    n = s // tm

    def body(xh_ref, g_ref, b_ref, t_ref, dy_ref, loss_ref, acc):
        i = pl.program_id(0)

        @pl.when(i == 0)
        def _():
            acc[...] = jnp.zeros_like(acc)

        err = xh_ref[...] * g_ref[...] + b_ref[...] - t_ref[...]
        dy_ref[...] = err * (1.0 / d)
        acc[...] += _colsum(err * err)

        @pl.when(i == n - 1)
        def _():
            loss_ref[...] = (0.5 / d) * jnp.sum(acc[...], axis=1, keepdims=True)

    tile = pl.BlockSpec((tm, d), lambda i: (i, 0))
    return pl.pallas_call(
        body, grid=(n,),
        in_specs=[tile, _rowspec(d), _rowspec(d), tile],
        out_specs=[tile, pl.BlockSpec((1, 1), lambda i: (0, 0))],
        out_shape=[jax.ShapeDtypeStruct((s, d), F32), jax.ShapeDtypeStruct((1, 1), F32)],
        scratch_shapes=[pltpu.VMEM((1, d), F32)],
        compiler_params=_cparams(), name=name,
    )(xh, g, b, target)


def _elementwise(fn, ins, out_dtypes, name, lead=0):
    shape = ins[0].shape
    c = shape[-1]
    r = math.prod(shape[lead:-1])
    lead_shape = tuple(shape[:lead])
    out_shape = tuple(shape[lead:])
    tr = _pick_rows(r, c, 4, (1 << 20) // max(1, math.prod(lead_shape)))
    nlead = len(lead_shape)

    def body(*refs):
        vals = fn(*[ref[...] for ref in refs[:len(ins)]])
        for ref, v in zip(refs[len(ins):], vals):
            ref[...] = v.astype(ref.dtype)

    def spec(a_lead):
        if a_lead:
            return pl.BlockSpec(lead_shape + (tr, c), lambda i: (0,) * nlead + (i, 0))
        return pl.BlockSpec((tr, c), lambda i: (i, 0))

    args = [ins[0].reshape(lead_shape + (r, c))] + [a.reshape(r, c) for a in ins[1:]]
    outs = pl.pallas_call(
        body, grid=(r // tr,),
        in_specs=[spec(bool(lead))] + [spec(False)] * (len(ins) - 1),
        out_specs=[spec(False)] * len(out_dtypes),
        out_shape=[jax.ShapeDtypeStruct((r, c), dt) for dt in out_dtypes],
        compiler_params=_cparams(parallel=True), name=name,
    )(*args)
    return [o.reshape(out_shape) for o in outs]


def _cast_bf16(w, name):
    return _elementwise(lambda v: (v,), [w], [BF16], name)[0]


def _add(a, b, name):
    return _elementwise(lambda u, v: (u + v,), [a, b], [F32], name)[0]


def _sum_lead(stacked, name):
    def fn(v):
        acc = v[0]
        for k in range(1, v.shape[0]):
            acc = acc + v[k]
        return (acc,)

    return _elementwise(fn, [stacked], [F32], name, lead=1)[0]


def _adamw(w, g, m, v, name):
    def fn(w_, g_, m_, v_):
        m2 = ADAM_B1 * m_ + (1.0 - ADAM_B1) * g_
        v2 = ADAM_B2 * v_ + (1.0 - ADAM_B2) * (g_ * g_)
        m_hat = m2 / (1.0 - ADAM_B1 ** ADAM_STEP)
        v_hat = v2 / (1.0 - ADAM_B2 ** ADAM_STEP)
        delta = -ADAM_LR * (m_hat / (jnp.sqrt(v_hat) + ADAM_EPS) + ADAM_WD * w_)
        return delta, m2, v2

    return _elementwise(fn, [w, g, m, v], [F32, F32, F32], name)


ANY = pl.BlockSpec(memory_space=pl.ANY)


def _mesh_pos():
    return lax.axis_index("x"), lax.axis_index("y"), lax.axis_index("c")


def _chip_peers(x, y, c):
    out = []
    for r in (1, 2, 3):
        px = 1 - x if r & 2 else x
        py = 1 - y if r & 1 else y
        out.append((2 * px + py, (px, py, c)))
    return out


def _full_shape(kind, shard_shape):
    l, r, c = shard_shape
    return (l, N_CHIPS, r, c) if kind == "row" else (l, r, N_CHIPS * c)


def _half_rows(shape3):
    hr = shape3[1] // 2
    assert hr % SUBLANES_BF16 == 0
    return hr


def _shard_half(ref, h, hr):
    return ref.at[:, pl.ds(pl.multiple_of(h * hr, SUBLANES_BF16), hr), :]


def _full_piece(ref, kind, k, h, hr, sc):
    rows = pl.ds(pl.multiple_of(h * hr, SUBLANES_BF16), hr)
    if kind == "row":
        return ref.at[:, k, rows, :]
    return ref.at[:, rows, pl.ds(pl.multiple_of(k * sc, LANES), sc)]


def _full_slot(ref, kind, k, sc):
    if kind == "row":
        return ref.at[:, k]
    return ref.at[:, :, pl.ds(pl.multiple_of(k * sc, LANES), sc)]


def _remote(src, dst, ssem, rsem, dev):
    return pltpu.make_async_remote_copy(src_ref=src, dst_ref=dst, send_sem=ssem, recv_sem=rsem,
                                        device_id=dev, device_id_type=MESH)


DMA_CHUNK_BYTES = 1 << 20
DMA_MAX_CHUNKS = 32


def _chunk_views(src, dst):
    axis = len(src.shape) - 2
    rows = src.shape[axis]
    nbytes = math.prod(src.shape) * jnp.dtype(src.dtype).itemsize
    n = max(1, min(DMA_MAX_CHUNKS, nbytes // DMA_CHUNK_BYTES))
    while n > 1 and (rows % n or (rows // n) % SUBLANES_BF16):
        n -= 1
    cr = rows // n
    out = []
    for i in range(n):
        idx = (slice(None),) * axis + (pl.ds(i * cr, cr), slice(None))
        out.append((src.at[idx], dst.at[idx]))
    return out


def _start_remote(src, dst, ssem, rsem, dev):
    for s, t in _chunk_views(src, dst):
        _remote(s, t, ssem, rsem, dev).start()
    return _remote(src, dst, ssem, rsem, dev)


def _start_local(src, dst, sem):
    for s, t in _chunk_views(src, dst):
        pltpu.make_async_copy(s, t, sem).start()
    return pltpu.make_async_copy(src, dst, sem)


def _allgather_weights(shards, kinds):
    nw = len(shards)
    hrs = [_half_rows(a.shape) for a in shards]
    scs = [a.shape[2] for a in shards]

    def body(*refs):
        sh, fu = refs[:nw], refs[nw:2 * nw]
        send1, recv1, send2, recv2, lsem = refs[2 * nw:]
        x, y, c = _mesh_pos()
        k_me = 2 * x + y
        sibling = (x, y, 1 - c)
        peers = _chip_peers(x, y, c)

        def piece(w, k, h):
            return _full_piece(fu[w], kinds[w], k, h, hrs[w], scs[w])

        locs, sends = [], []
        for w in range(nw):
            locs.append(_start_local(sh[w], _full_slot(fu[w], kinds[w], k_me, scs[w]), lsem.at[w]))
            for r, (_, dev) in enumerate(peers):
                sends.append(_start_remote(_shard_half(sh[w], c, hrs[w]), piece(w, k_me, c),
                                           send1.at[3 * w + r], recv1.at[3 * w + r], dev))
        for w in range(nw):
            for r, (kj, dev) in enumerate(peers):
                _remote(_shard_half(sh[w], c, hrs[w]), piece(w, kj, c),
                        send1.at[3 * w + r], recv1.at[3 * w + r], dev).wait_recv()
                sends.append(_start_remote(piece(w, kj, c), piece(w, kj, c),
                                           send2.at[3 * w + r], recv2.at[3 * w + r], sibling))
        for w in range(nw):
            for r, (kj, _) in enumerate(peers):
                _remote(piece(w, kj, 1 - c), piece(w, kj, 1 - c),
                        send2.at[3 * w + r], recv2.at[3 * w + r], sibling).wait_recv()
        for cp in sends:
            cp.wait_send()
        for cp in locs:
            cp.wait()

    outs = pl.pallas_call(
        body,
        in_specs=[ANY] * nw, out_specs=[ANY] * nw,
        out_shape=[jax.ShapeDtypeStruct(_full_shape(k, a.shape), a.dtype) for k, a in zip(kinds, shards)],
        scratch_shapes=[pltpu.SemaphoreType.DMA((3 * nw,))] * 4 + [pltpu.SemaphoreType.DMA((nw,))],
        name="allgather_weights",
    )(*shards)
    return outs


def _rs_pair(fulls, kinds):
    nw = len(fulls)

    def half_all(ref, kind, h):
        if kind == "row":
            hr = ref.shape[2] // 2
            return ref.at[:, :, pl.ds(pl.multiple_of(h * hr, SUBLANES_BF16), hr), :]
        hr = ref.shape[1] // 2
        return ref.at[:, pl.ds(pl.multiple_of(h * hr, SUBLANES_BF16), hr), :]

    def half_shape(kind, shape):
        if kind == "row":
            return (shape[0], shape[1], shape[2] // 2, shape[3])
        return (shape[0], shape[1] // 2, shape[2])

    def body(*refs):
        g, own, got = refs[:nw], refs[nw:2 * nw], refs[2 * nw:3 * nw]
        ssem, rsem, lsem = refs[3 * nw:]
        x, y, c = _mesh_pos()
        sibling = (x, y, 1 - c)
        cps, locs = [], []
        for w in range(nw):
            cps.append(_start_remote(half_all(g[w], kinds[w], 1 - c), got[w], ssem.at[w], rsem.at[w], sibling))
            locs.append(_start_local(half_all(g[w], kinds[w], c), own[w], lsem.at[w]))
        for cp in cps:
            cp.wait_recv()
        for cp in cps:
            cp.wait_send()
        for lc in locs:
            lc.wait()

    shapes = [jax.ShapeDtypeStruct(half_shape(k, a.shape), a.dtype) for k, a in zip(kinds, fulls)]
    outs = pl.pallas_call(
        body, in_specs=[ANY] * nw, out_specs=[ANY] * (2 * nw), out_shape=shapes + shapes,
        scratch_shapes=[pltpu.SemaphoreType.DMA((nw,))] * 3, name="rs_pair",
    )(*fulls)
    return outs[:nw], outs[nw:]


def _rs_chips(parts, kinds):
    nw = len(parts)

    def slot(ref, kind, k):
        if kind == "row":
            return ref.at[:, k]
        sc = ref.shape[2] // N_CHIPS
        return ref.at[:, :, pl.ds(pl.multiple_of(k * sc, LANES), sc)]

    def slot_shape(kind, shape):
        if kind == "row":
            return (shape[0], shape[2], shape[3])
        return (shape[0], shape[1], shape[2] // N_CHIPS)

    def body(*refs):
        t, rb = refs[:nw], refs[nw:2 * nw]
        ssem, rsem, lsem = refs[2 * nw:]
        x, y, c = _mesh_pos()
        k_me = 2 * x + y
        peers = _chip_peers(x, y, c)
        cps, locs = [], []
        for w in range(nw):
            locs.append(_start_local(slot(t[w], kinds[w], k_me), rb[w].at[0], lsem.at[w]))
            for r, (kj, dev) in enumerate(peers):
                cps.append(_start_remote(slot(t[w], kinds[w], kj), rb[w].at[r + 1],
                                         ssem.at[3 * w + r], rsem.at[3 * w + r], dev))
        for cp in cps:
            cp.wait_recv()
        for cp in cps:
            cp.wait_send()
        for lc in locs:
            lc.wait()

    shapes = [jax.ShapeDtypeStruct((N_CHIPS,) + slot_shape(k, a.shape), a.dtype) for k, a in zip(kinds, parts)]
    return pl.pallas_call(
        body, in_specs=[ANY] * nw, out_specs=[ANY] * nw, out_shape=shapes,
        scratch_shapes=[pltpu.SemaphoreType.DMA((3 * nw,))] * 2 + [pltpu.SemaphoreType.DMA((nw,))],
        name="rs_chips",
    )(*parts)


def _rs_join(halves):
    nw = len(halves)

    def body(*refs):
        src, dst = refs[:nw], refs[nw:2 * nw]
        ssem, rsem, lsem = refs[2 * nw:]
        x, y, c = _mesh_pos()
        sibling = (x, y, 1 - c)
        cps, locs = [], []
        for w in range(nw):
            hr = src[w].shape[1]
            cps.append(_start_remote(src[w], _shard_half(dst[w], c, hr), ssem.at[w], rsem.at[w], sibling))
            locs.append(_start_local(src[w], _shard_half(dst[w], c, hr), lsem.at[w]))
        for w, cp in enumerate(cps):
            hr = src[w].shape[1]
            _remote(src[w], _shard_half(dst[w], 1 - c, hr), ssem.at[w], rsem.at[w], sibling).wait_recv()
        for cp in cps:
            cp.wait_send()
        for lc in locs:
            lc.wait()

    shapes = [jax.ShapeDtypeStruct((a.shape[0], 2 * a.shape[1], a.shape[2]), a.dtype) for a in halves]
    return pl.pallas_call(
        body, in_specs=[ANY] * nw, out_specs=[ANY] * nw, out_shape=shapes,
        scratch_shapes=[pltpu.SemaphoreType.DMA((nw,))] * 3, name="rs_join",
    )(*halves)


def _allgather_small(buf, name):
    def body(in_ref, out_ref, ssem, rsem, lsem):
        x, y, c = _mesh_pos()
        me = 4 * x + 2 * y + c
        lc = pltpu.make_async_copy(in_ref, out_ref.at[me], lsem)
        lc.start()
        cps, waits = [], []
        for r in range(1, N_DEV):
            px = 1 - x if r & 4 else x
            py = 1 - y if r & 2 else y
            pc = 1 - c if r & 1 else c
            cp = _remote(in_ref, out_ref.at[me], ssem.at[r - 1], rsem.at[r - 1], (px, py, pc))
            cp.start()
            cps.append(cp)
            waits.append(_remote(in_ref, out_ref.at[4 * px + 2 * py + pc], ssem.at[r - 1], rsem.at[r - 1], (px, py, pc)))
        for wt in waits:
            wt.wait_recv()
        for cp in cps:
            cp.wait_send()
        lc.wait()

    return pl.pallas_call(
        body, in_specs=[ANY], out_specs=ANY,
        out_shape=jax.ShapeDtypeStruct((N_DEV,) + buf.shape, buf.dtype),
        scratch_shapes=[pltpu.SemaphoreType.DMA((N_DEV - 1,))] * 2 + [pltpu.SemaphoreType.DMA],
        name=name,
    )(buf)


def _pack(arrs):
    flat = jnp.concatenate([a.reshape(-1).astype(F32) for a in arrs])
    rows = -(-flat.shape[0] // (LANES * 16)) * 16
    return jnp.pad(flat, (0, rows * LANES - flat.shape[0])).reshape(rows, LANES)


def _unpack(buf, shapes):
    flat = buf.reshape(-1)
    out, off = [], 0
    for shp in shapes:
        nel = math.prod(shp)
        out.append(flat[off:off + nel].reshape(shp))
        off += nel
    return out


BIG = ("a_w_in", "a_w_out", "b_w_in", "b_w_out", "c_w_in", "c_w_grp", "c_w_out", "f_w_up", "f_w_down")
BIG_KIND = {"a_w_in": "col", "a_w_out": "row", "b_w_in": "col", "b_w_out": "row", "c_w_in": "row",
            "c_w_grp": "row", "c_w_out": "row", "f_w_up": "col", "f_w_down": "row"}
SHARDED_SMALL = ("a_dw", "a_dw_b", "a_ln_g", "a_ln_b", "c_scale", "f_dw")
REPLICATED = ("b_ln_g", "b_ln_b", "b_ws", "b_bs", "ln1_g", "ln1_b", "ln2_g", "ln2_b")
WEIGHTS = ("a_w_in", "a_dw", "a_dw_b", "a_ln_g", "a_ln_b", "a_w_out", "b_w_in", "b_ln_g", "b_ln_b", "b_ws", "b_bs",
           "b_w_out", "c_w_in", "c_w_grp", "c_scale", "c_w_out", "f_w_up", "f_dw", "f_w_down",
           "ln1_g", "ln1_b", "ln2_g", "ln2_b")


def _as3d(a):
    return a.reshape((-1,) + a.shape[-2:])


def kernel(x, a_w_in, a_dw, a_dw_b, a_ln_g, a_ln_b, a_w_out, b_w_in, b_ln_g, b_ln_b, b_ws, b_bs, b_w_out, c_w_in, c_w_grp, c_scale, c_w_out, f_w_up, f_dw, f_w_down, ln1_g, ln1_b, ln2_g, ln2_b, loss_target, m_a_w_in, m_a_dw, m_a_dw_b, m_a_ln_g, m_a_ln_b, m_a_w_out, m_b_w_in, m_b_ln_g, m_b_ln_b, m_b_ws, m_b_bs, m_b_w_out, m_c_w_in, m_c_w_grp, m_c_scale, m_c_w_out, m_f_w_up, m_f_dw, m_f_w_down, m_ln1_g, m_ln1_b, m_ln2_g, m_ln2_b, v_a_w_in, v_a_dw, v_a_dw_b, v_a_ln_g, v_a_ln_b, v_a_w_out, v_b_w_in, v_b_ln_g, v_b_ln_b, v_b_ws, v_b_bs, v_b_w_out, v_c_w_in, v_c_w_grp, v_c_scale, v_c_w_out, v_f_w_up, v_f_dw, v_f_w_down, v_ln1_g, v_ln1_b, v_ln2_g, v_ln2_b):
    w = dict(a_w_in=a_w_in, a_dw=a_dw, a_dw_b=a_dw_b, a_ln_g=a_ln_g, a_ln_b=a_ln_b, a_w_out=a_w_out, b_w_in=b_w_in, b_ln_g=b_ln_g, b_ln_b=b_ln_b, b_ws=b_ws, b_bs=b_bs, b_w_out=b_w_out, c_w_in=c_w_in, c_w_grp=c_w_grp, c_scale=c_scale, c_w_out=c_w_out, f_w_up=f_w_up, f_dw=f_dw, f_w_down=f_w_down, ln1_g=ln1_g, ln1_b=ln1_b, ln2_g=ln2_g, ln2_b=ln2_b)
    mom = dict(a_w_in=m_a_w_in, a_dw=m_a_dw, a_dw_b=m_a_dw_b, a_ln_g=m_a_ln_g, a_ln_b=m_a_ln_b, a_w_out=m_a_w_out, b_w_in=m_b_w_in, b_ln_g=m_b_ln_g, b_ln_b=m_b_ln_b, b_ws=m_b_ws, b_bs=m_b_bs, b_w_out=m_b_w_out, c_w_in=m_c_w_in, c_w_grp=m_c_w_grp, c_scale=m_c_scale, c_w_out=m_c_w_out, f_w_up=m_f_w_up, f_dw=m_f_dw, f_w_down=m_f_w_down, ln1_g=m_ln1_g, ln1_b=m_ln1_b, ln2_g=m_ln2_g, ln2_b=m_ln2_b)
    var = dict(a_w_in=v_a_w_in, a_dw=v_a_dw, a_dw_b=v_a_dw_b, a_ln_g=v_a_ln_g, a_ln_b=v_a_ln_b, a_w_out=v_a_w_out, b_w_in=v_b_w_in, b_ln_g=v_b_ln_g, b_ln_b=v_b_ln_b, b_ws=v_b_ws, b_bs=v_b_bs, b_w_out=v_b_w_out, c_w_in=v_c_w_in, c_w_grp=v_c_w_grp, c_scale=v_c_scale, c_w_out=v_c_w_out, f_w_up=v_f_w_up, f_dw=v_f_dw, f_w_down=v_f_w_down, ln1_g=v_ln1_g, ln1_b=v_ln1_b, ln2_g=v_ln2_g, ln2_b=v_ln2_b)

    depth = ln1_g.shape[0]
    d = x.shape[-1]
    alpha = float((2 * depth) ** 0.25)
    chip = 2 * lax.axis_index("x") + lax.axis_index("y")

    kinds = [BIG_KIND[k] for k in BIG]
    shards_bf = [_cast_bf16(_as3d(w[k]), "cast_" + k) for k in BIG]
    gathered = _allgather_weights(shards_bf, kinds)
    full = {}
    for k, kind, arr in zip(BIG, kinds, gathered):
        full[k] = arr.reshape(arr.shape[0], -1, arr.shape[-1]) if kind == "row" else arr

    small_all = _allgather_small(_pack([w[k] for k in SHARDED_SMALL]), "allgather_small_params")
    per_chip = [_unpack(small_all[2 * k], [w[n].shape for n in SHARDED_SMALL]) for k in range(N_CHIPS)]
    fs = {n: jnp.concatenate([per_chip[k][i] for k in range(N_CHIPS)], axis=-1) for i, n in enumerate(SHARDED_SMALL)}

    nh = b_ws.shape[1]
    tril = jnp.tril(jnp.ones((CHUNK, CHUNK), F32))
    wm = (b_ws[0] * tril).astype(BF16)
    wmt = jnp.swapaxes(wm, 1, 2)
    bs_exp = jnp.repeat(jnp.transpose(b_bs[0]), CHUNK, axis=1)
    ng = c_w_grp.shape[1]
    wgrp = full["c_w_grp"]

    xh, g, b = x[0], jnp.ones((1, d), F32), jnp.zeros((1, d), F32)
    saved = []
    for i in range(depth):
        kind, j = i % 3, i // 3
        rec = dict(xin=xh, gin=g, bin=b)
        if kind == 0:
            xh1, rstd1, p, chat, rstdc = _conv_fwd(
                xh, g, b, full["a_w_in"], j, fs["a_dw"], fs["a_dw_b"][j:j + 1], fs["a_ln_g"][j:j + 1],
                fs["a_ln_b"][j:j + 1], full["a_w_out"], alpha, f"conv_fwd_{i}")
            rec.update(p=p, chat=chat, rstdc=rstdc)
        elif kind == 1:
            xh1, rstd1, zp = _sgu_fwd(xh, g, b, full["b_w_in"], b_ln_g, b_ln_b, wm, bs_exp, full["b_w_out"],
                                      alpha, f"sgu_fwd_{i}")
            rec.update(zp=zp)
        else:
            xh1, rstd1, ys = _pool_fwd(xh, g, b, full["c_w_in"], wgrp, fs["c_scale"], full["c_w_out"],
                                       alpha, f"pool_fwd_{i}")
            rec.update(ys=ys)
        xh2, rstd2, hs = _ffn_fwd(xh1, ln1_g[i:i + 1], ln1_b[i:i + 1], full["f_w_up"], i, fs["f_dw"],
                                  full["f_w_down"], alpha, f"ffn_fwd_{i}")
        rec.update(xh1=xh1, rstd1=rstd1, xh2=xh2, rstd2=rstd2, hs=hs)
        saved.append(rec)
        xh, g, b = xh2, ln2_g[i:i + 1], ln2_b[i:i + 1]

    dxo, loss_part = _loss_head(xh, g, b, loss_target[0], "loss_head")
    loss = lax.psum(loss_part[0, 0], ("x", "y", "c"))

    gbuf = {k: jnp.zeros(full[k].shape, F32) for k in BIG if k != "c_w_grp"}
    gs = {k: [None] * w[k].shape[0] for k in ("a_dw", "a_dw_b", "a_ln_g", "a_ln_b", "f_dw", "ln1_g", "ln1_b", "ln2_g", "ln2_b")}
    for i in reversed(range(depth)):
        kind, j = i % 3, i // 3
        rec = saved[i]
        dr2, dh, gbuf["f_w_down"], gs["f_dw"][i], gs["ln2_g"][i], gs["ln2_b"][i] = _ffn_bwd1(
            dxo, rec["xh2"], rec["rstd2"], ln2_g[i:i + 1], rec["hs"], full["f_w_down"], i, fs["f_dw"],
            gbuf["f_w_down"], f"ffn_bwd1_{i}")
        dr1, gs["ln1_g"][i], gs["ln1_b"][i] = _bwd_in(
            dh, dr2, full["f_w_up"], i, alpha, f"ffn_bwd2_{i}", ln=(rec["xh1"], rec["rstd1"], ln1_g[i:i + 1]))
        gbuf["f_w_up"] = _mm_tn(rec["xh1"], ln1_g[i:i + 1], ln1_b[i:i + 1], dh, gbuf["f_w_up"], i, f"grad_w_up_{i}")
        if kind == 0:
            dp, gbuf["a_w_out"], gs["a_dw"][j], gs["a_dw_b"][j], gs["a_ln_g"][j], gs["a_ln_b"][j] = _conv_bwd1(
                dr1, rec["chat"], rec["rstdc"], rec["p"], full["a_w_out"], j, fs["a_dw"], fs["a_ln_g"][j:j + 1],
                fs["a_ln_b"][j:j + 1], gbuf["a_w_out"], f"conv_bwd1_{i}")
            win_name, lidx = "a_w_in", j
        elif kind == 1:
            dp, gbuf["b_w_out"], g_ws, g_bs_t, g_blg, g_blb = _sgu_bwd1(
                dr1, rec["zp"], full["b_w_out"], b_ln_g, b_ln_b, wm, wmt, bs_exp, gbuf["b_w_out"], f"sgu_bwd1_{i}")
            win_name, lidx = "b_w_in", 0
        else:
            dp, gbuf["c_w_out"], g_wgrp, g_cscale = _pool_bwd1(
                dr1, rec["ys"], full["c_w_out"], wgrp, fs["c_scale"], gbuf["c_w_out"], f"pool_bwd1_{i}")
            win_name, lidx = "c_w_in", 0
        dxo = _bwd_in(dp, dr1, full[win_name], lidx, alpha, f"mixer_bwd2_{i}")
        gbuf[win_name] = _mm_tn(rec["xin"], rec["gin"], rec["bin"], dp, gbuf[win_name], lidx, f"grad_w_in_{i}")
    grad_x = dxo[None]

    gfull = []
    for k, kind in zip(BIG, kinds):
        a = g_wgrp if k == "c_w_grp" else gbuf[k]
        gfull.append(a.reshape(a.shape[0], N_CHIPS, -1, a.shape[-1]) if kind == "row" else a)
    own, got = _rs_pair(gfull, kinds)
    pair_sum = [_add(o, t, "rs_pair_sum_" + k) for k, o, t in zip(BIG, own, got)]
    from_chips = _rs_chips(pair_sum, kinds)
    half_sum = [_sum_lead(a, "rs_chip_sum_" + k) for k, a in zip(BIG, from_chips)]
    joined = _rs_join(half_sum)
    grads = {k: a.reshape(w[k].shape) for k, a in zip(BIG, joined)}

    small_full = {
        "a_dw": jnp.stack(gs["a_dw"]), "a_dw_b": jnp.concatenate(gs["a_dw_b"]), "a_ln_g": jnp.concatenate(gs["a_ln_g"]),
        "a_ln_b": jnp.concatenate(gs["a_ln_b"]), "c_scale": g_cscale, "f_dw": jnp.stack(gs["f_dw"]),
        "b_ln_g": g_blg, "b_ln_b": g_blb, "b_ws": g_ws[None], "b_bs": jnp.transpose(g_bs_t)[None],
        "ln1_g": jnp.concatenate(gs["ln1_g"]), "ln1_b": jnp.concatenate(gs["ln1_b"]),
        "ln2_g": jnp.concatenate(gs["ln2_g"]), "ln2_b": jnp.concatenate(gs["ln2_b"]),
    }
    small_names = SHARDED_SMALL + REPLICATED
    small_shapes = [small_full[n].shape for n in small_names]
    gathered_small = _allgather_small(_pack([small_full[n] for n in small_names]), "allgather_small_grads")
    summed = _unpack(_sum_lead(gathered_small, "small_grad_sum"), small_shapes)
    for n, a in zip(small_names, summed):
        if n in SHARDED_SMALL:
            cs = w[n].shape[-1]
            a = lax.dynamic_slice_in_dim(a, chip * cs, cs, axis=a.ndim - 1)
        grads[n] = a

    delta, new_m, new_v = {}, {}, {}
    for k in BIG:
        delta[k], new_m[k], new_v[k] = _adamw(w[k], grads[k], mom[k], var[k], "adamw_" + k)
    shapes = [w[n].shape for n in small_names]
    ds_, ms_, vs_ = _adamw(_pack([w[n] for n in small_names]), _pack([grads[n] for n in small_names]),
                           _pack([mom[n] for n in small_names]), _pack([var[n] for n in small_names]), "adamw_small")
    for n, a, bb, cc in zip(small_names, _unpack(ds_, shapes), _unpack(ms_, shapes), _unpack(vs_, shapes)):
        delta[n], new_m[n], new_v[n] = a, bb, cc

    return (loss, grad_x, *[grads[n] for n in WEIGHTS], *[delta[n] for n in WEIGHTS],
            *[new_m[n] for n in WEIGHTS], *[new_v[n] for n in WEIGHTS])
```

```python
import math

import jax
import jax.numpy as jnp
from jax import lax
from jax.experimental import pallas as pl
from jax.experimental.pallas import tpu as pltpu

F32 = jnp.float32
BF16 = jnp.bfloat16

LN_EPS = 1e-5
POOL_WINDOWS = (2, 4, 8, 16)
CHUNK = 128
ADAM_LR = 0.001
ADAM_B1 = 0.9
ADAM_B2 = 0.999
ADAM_EPS = 1e-08
ADAM_WD = 0.01
ADAM_STEP = 10

LANES = 128
SUBLANES_BF16 = 16
N_CHIPS = 4
N_DEV = 8
VMEM_LIMIT = 60 * 1024 * 1024

TM_FFN = 512
TM_CONV = 256
TM_SGU = 512
TM_POOL = 512
TM_BWD_IN = 512
TS_MM_TN = 1024
CW_FFN = 256
CONV_HALO = 32
POOL_HALO = 16
FFN_HALO = 16

MESH = pl.DeviceIdType.MESH


def _cparams(n_grid=1, parallel=False):
    sem = ("parallel" if parallel else "arbitrary",) * n_grid
    return pltpu.CompilerParams(dimension_semantics=sem, vmem_limit_bytes=VMEM_LIMIT)


def _resident(block, imap):
    return pl.BlockSpec(block, imap, pipeline_mode=pl.Buffered(1))


def _wspec(w, l):
    _, r, c = w.shape
    return _resident((None, r, c), lambda *_: (l, 0, 0))


def _rowspec(d):
    return pl.BlockSpec((1, d), lambda *_: (0, 0))


def _dot(a, b):
    return jnp.dot(a, b, preferred_element_type=F32)


def _dot_nt(a, b):
    return lax.dot_general(a, b, (((1,), (1,)), ((), ())), preferred_element_type=F32)


def _dot_tn(a, b):
    return lax.dot_general(a, b, (((0,), (0,)), ((), ())), preferred_element_type=F32)


def _sigmoid(x):
    return jax.nn.sigmoid(x)


def _ln_stats(r):
    mu = jnp.mean(r, axis=1, keepdims=True)
    xc = r - mu
    var = jnp.mean(xc * xc, axis=1, keepdims=True)
    rstd = lax.rsqrt(var + LN_EPS)
    return xc * rstd, rstd


def _ln_bwd(dy, xhat, rstd, g):
    dxh = dy * g
    m1 = jnp.mean(dxh, axis=1, keepdims=True)
    m2 = jnp.mean(dxh * xhat, axis=1, keepdims=True)
    return rstd * (dxh - m1 - xhat * m2)


def _colsum(v):
    return jnp.sum(v, axis=0, keepdims=True)


def _gelu(z):
    return 0.5 * z * (1.0 + lax.erf(z * (1.0 / math.sqrt(2.0))))


def _gelu_grad(z):
    cdf = 0.5 * (1.0 + lax.erf(z * (1.0 / math.sqrt(2.0))))
    pdf = jnp.exp(-0.5 * z * z) * (1.0 / math.sqrt(2.0 * math.pi))
    return cdf + z * pdf


def _shift_down(v, k, prev_rows):
    rolled = pltpu.roll(v, k, 0)
    head = rolled[0:8]
    rows = lax.broadcasted_iota(jnp.int32, head.shape, 0)
    for r in range(k):
        head = jnp.where(rows == r, prev_rows[k - 1 - r], head)
    return jnp.concatenate([head, rolled[8:]], axis=0)


def _shift_up(v, k, next_rows):
    tm = v.shape[0]
    rolled = pltpu.roll(v, tm - k, 0)
    tail = rolled[tm - 8:tm]
    rows = lax.broadcasted_iota(jnp.int32, tail.shape, 0)
    for r in range(k):
        tail = jnp.where(rows == 8 - k + r, next_rows[r], tail)
    return jnp.concatenate([rolled[0:tm - 8], tail], axis=0)


def _fill_shifted(base_scr, sh_scr):
    nrows = sh_scr.shape[1]
    for r in range(1, 8):
        sh_scr[r - 1, :, :] = base_scr[pl.ds(r, nrows), :]


def _tap(base_scr, sh_scr, off, tm):
    q, r = divmod(off, 8)
    if r == 0:
        return base_scr[pl.ds(8 * q, tm), :]
    return sh_scr[r - 1, pl.ds(8 * q, tm), :]


def _pick_rows(r, c, itemsize, cap_bytes):
    best = None
    for t in range(16, r + 1, 16):
        if r % t == 0 and t * c * itemsize <= cap_bytes:
            best = t
    return best if best is not None else r


def _ffn_conv_cols(h, dw_ref, c0, cw, prev1, prev2):
    kw = dw_ref.shape[0]
    h1 = _shift_down(h, 1, [prev1])
    h2 = _shift_down(h, 2, [prev1, prev2])
    hc = dw_ref[kw - 1:kw, c0:c0 + cw] * h + dw_ref[kw - 2:kw - 1, c0:c0 + cw] * h1 + dw_ref[kw - 3:kw - 2, c0:c0 + cw] * h2
    return hc, h1, h2


def _ffn_fwd(xh1, g1, b1, wup, l, fdw, wdn, alpha, name):
    s, d = xh1.shape
    f2 = wup.shape[2]
    f = f2 // 2
    tm = min(TM_FFN, s)
    cw = min(CW_FFN, f)
    n, nck = s // tm, f // cw
    assert fdw.shape[1] == 3 and s % tm == 0 and f % cw == 0

    def body(xh_ref, g_ref, b_ref, wup_ref, dw_ref, wdn_ref, xo_ref, rs_ref, hs_ref, carry):
        @pl.when(pl.program_id(0) == 0)
        def _():
            carry[...] = jnp.zeros_like(carry)

        x1 = xh_ref[...] * g_ref[...] + b_ref[...]
        xb = x1.astype(BF16)
        o = jnp.zeros((tm, d), F32)
        for j in range(nck):
            parts = []
            for half in range(2):
                c0 = half * f + j * cw
                h = _dot(xb, wup_ref[:, c0:c0 + cw])
                hs_ref[:, c0:c0 + cw] = h.astype(BF16)
                hc, _, _ = _ffn_conv_cols(h, dw_ref, c0, cw, carry[7:8, c0:c0 + cw], carry[6:7, c0:c0 + cw])
                carry[:, c0:c0 + cw] = h[tm - 8:tm, :]
                parts.append(hc)
            gg, vv = parts
            a = (gg * _sigmoid(gg) * vv).astype(BF16)
            o = o + _dot(a, wdn_ref[j * cw:(j + 1) * cw, :])
        xhat, rstd = _ln_stats(alpha * x1 + o)
        xo_ref[...] = xhat
        rs_ref[...] = rstd

    tile = pl.BlockSpec((tm, d), lambda i: (i, 0))
    return pl.pallas_call(
        body, grid=(n,),
        in_specs=[tile, _rowspec(d), _rowspec(d), _wspec(wup, l),
                  pl.BlockSpec((None, 3, f2), lambda i: (l, 0, 0)), _wspec(wdn, l)],
        out_specs=[tile, pl.BlockSpec((tm, 1), lambda i: (i, 0)), pl.BlockSpec((tm, f2), lambda i: (i, 0))],
        out_shape=[jax.ShapeDtypeStruct((s, d), F32), jax.ShapeDtypeStruct((s, 1), F32),
                   jax.ShapeDtypeStruct((s, f2), BF16)],
        scratch_shapes=[pltpu.VMEM((8, f2), F32)],
        compiler_params=_cparams(), name=name,
    )(xh1, g1, b1, wup, fdw, wdn)


def _ffn_bwd1(dx2, xh2, rstd2, g2, hs, wdn, l, fdw, gwdn_buf, name):
    s, d = dx2.shape
    f2 = hs.shape[1]
    f = f2 // 2
    tm = min(TM_FFN, s)
    cw = min(CW_FFN, f)
    n, nck = s // tm, f // cw
    hb = FFN_HALO

    def body(dx_ref, xh_ref, rs_ref, g_ref, hs_ref, halo_ref, wdn_ref, dw_ref, buf_ref,
             dr_ref, dh_ref, gwdn_ref, gdw_ref, gg_ref, gb_ref, carry):
        i = pl.program_id(0)
        t = n - 1 - i

        @pl.when(i == 0)
        def _():
            carry[...] = jnp.zeros_like(carry)
            gwdn_ref[...] = jnp.zeros_like(gwdn_ref)
            gdw_ref[...] = jnp.zeros_like(gdw_ref)
            gg_ref[...] = jnp.zeros_like(gg_ref)
            gb_ref[...] = jnp.zeros_like(gb_ref)

        dx = dx_ref[...]
        xh = xh_ref[...]
        gg_ref[...] += _colsum(dx * xh)
        gb_ref[...] += _colsum(dx)
        dr = _ln_bwd(dx, xh, rs_ref[...], g_ref[...])
        dr_ref[...] = dr
        dob = dr.astype(BF16)
        has_prev = t > 0
        for j in range(nck):
            hc, hh = [], []
            for half in range(2):
                c0 = half * f + j * cw
                h = hs_ref[:, c0:c0 + cw].astype(F32)
                hal = jnp.where(has_prev, halo_ref[:, c0:c0 + cw].astype(F32), 0.0)
                c, h1, h2 = _ffn_conv_cols(h, dw_ref, c0, cw, hal[hb - 1:hb], hal[hb - 2:hb - 1])
                hc.append(c)
                hh.append((h, h1, h2))
            gt, vv = hc
            sg = _sigmoid(gt)
            sl = gt * sg
            a = (sl * vv).astype(BF16)
            gwdn_ref[j * cw:(j + 1) * cw, :] += _dot_tn(a, dob)
            da = _dot_nt(dob, wdn_ref[j * cw:(j + 1) * cw, :])
            dhc = (da * vv * (sg * (1.0 + gt * (1.0 - sg))), da * sl)
            for half in range(2):
                c0 = half * f + j * cw
                dc = dhc[half]
                h, h1, h2 = hh[half]
                gdw_ref[2:3, c0:c0 + cw] += _colsum(dc * h)
                gdw_ref[1:2, c0:c0 + cw] += _colsum(dc * h1)
                gdw_ref[0:1, c0:c0 + cw] += _colsum(dc * h2)
                nxt = [carry[0:1, c0:c0 + cw], carry[1:2, c0:c0 + cw]]
                u1 = _shift_up(dc, 1, nxt[:1])
                u2 = _shift_up(dc, 2, nxt)
                dh = dw_ref[2:3, c0:c0 + cw] * dc + dw_ref[1:2, c0:c0 + cw] * u1 + dw_ref[0:1, c0:c0 + cw] * u2
                dh_ref[:, c0:c0 + cw] = dh.astype(BF16)
                carry[:, c0:c0 + cw] = dc[0:8, :]

    tile = pl.BlockSpec((tm, d), lambda i: (n - 1 - i, 0))
    halo_blocks = tm // hb
    nl = gwdn_buf.shape[0]
    outs = pl.pallas_call(
        body, grid=(n,),
        in_specs=[tile, tile, pl.BlockSpec((tm, 1), lambda i: (n - 1 - i, 0)), _rowspec(d),
                  pl.BlockSpec((tm, f2), lambda i: (n - 1 - i, 0)),
                  pl.BlockSpec((hb, f2), lambda i: (jnp.maximum((n - 1 - i) * halo_blocks - 1, 0), 0)),
                  _wspec(wdn, l), pl.BlockSpec((None, 3, f2), lambda i: (l, 0, 0)),
                  pl.BlockSpec(memory_space=pl.ANY)],
        out_specs=[tile, pl.BlockSpec((tm, f2), lambda i: (n - 1 - i, 0)),
                   pl.BlockSpec((None, f, d), lambda i: (l, 0, 0)),
                   pl.BlockSpec((3, f2), lambda i: (0, 0)), _rowspec(d), _rowspec(d)],
        out_shape=[jax.ShapeDtypeStruct((s, d), F32), jax.ShapeDtypeStruct((s, f2), BF16),
                   jax.ShapeDtypeStruct((nl, f, d), F32), jax.ShapeDtypeStruct((3, f2), F32),
                   jax.ShapeDtypeStruct((1, d), F32), jax.ShapeDtypeStruct((1, d), F32)],
        scratch_shapes=[pltpu.VMEM((8, f2), F32)],
        input_output_aliases={8: 2},
        compiler_params=_cparams(), name=name,
    )(dx2, xh2, rstd2, g2, hs, hs, wdn, fdw, gwdn_buf)
    return outs


def _bwd_in(dp, dres, w, l, alpha, name, ln=None):
    s, d = dres.shape
    nn = dp.shape[1]
    tm = min(TM_BWD_IN, s)
    n = s // tm
    tile = pl.BlockSpec((tm, d), lambda i: (i, 0))

    if ln is None:
        def body(dp_ref, dres_ref, w_ref, o_ref):
            o_ref[...] = alpha * dres_ref[...] + _dot_nt(dp_ref[...], w_ref[...])

        return pl.pallas_call(
            body, grid=(n,),
            in_specs=[pl.BlockSpec((tm, nn), lambda i: (i, 0)), tile, _wspec(w, l)],
            out_specs=tile, out_shape=jax.ShapeDtypeStruct((s, d), F32),
            compiler_params=_cparams(parallel=True), name=name,
        )(dp, dres, w)

    xh, rstd, g = ln

    def body_ln(dp_ref, dres_ref, w_ref, xh_ref, rs_ref, g_ref, o_ref, gg_ref, gb_ref):
        @pl.when(pl.program_id(0) == 0)
        def _():
            gg_ref[...] = jnp.zeros_like(gg_ref)
            gb_ref[...] = jnp.zeros_like(gb_ref)

        dx = alpha * dres_ref[...] + _dot_nt(dp_ref[...], w_ref[...])
        xhv = xh_ref[...]
        gg_ref[...] += _colsum(dx * xhv)
        gb_ref[...] += _colsum(dx)
        o_ref[...] = _ln_bwd(dx, xhv, rs_ref[...], g_ref[...])

    return pl.pallas_call(
        body_ln, grid=(n,),
        in_specs=[pl.BlockSpec((tm, nn), lambda i: (i, 0)), tile, _wspec(w, l), tile,
                  pl.BlockSpec((tm, 1), lambda i: (i, 0)), _rowspec(d)],
        out_specs=[tile, _rowspec(d), _rowspec(d)],
        out_shape=[jax.ShapeDtypeStruct((s, d), F32), jax.ShapeDtypeStruct((1, d), F32),
                   jax.ShapeDtypeStruct((1, d), F32)],
        compiler_params=_cparams(), name=name,
    )(dp, dres, w, xh, rstd, g)


def _mm_tn(a, ga, ba, bm, buf, l, name):
    s, k = a.shape
    nn = bm.shape[1]
    ts = min(TS_MM_TN, s)
    tn = nn // N_CHIPS if nn > 1024 else nn
    nj, ns = nn // tn, s // ts

    def body(a_ref, g_ref, b_ref, bm_ref, buf_ref, o_ref):
        @pl.when(pl.program_id(1) == 0)
        def _():
            o_ref[...] = jnp.zeros_like(o_ref)

        ab = (a_ref[...] * g_ref[...] + b_ref[...]).astype(BF16)
        o_ref[...] += _dot_tn(ab, bm_ref[...])

    return pl.pallas_call(
        body, grid=(nj, ns),
        in_specs=[pl.BlockSpec((ts, k), lambda j, t: (t, 0)), _rowspec(k), _rowspec(k),
                  pl.BlockSpec((ts, tn), lambda j, t: (t, j)), pl.BlockSpec(memory_space=pl.ANY)],
        out_specs=pl.BlockSpec((None, k, tn), lambda j, t: (l, 0, j)),
        out_shape=jax.ShapeDtypeStruct(buf.shape, F32),
        input_output_aliases={4: 0},
        compiler_params=_cparams(2), name=name,
    )(a, ga, ba, bm, buf)


def _conv_fwd(xin, gin, bin_, win, l, adw, adwb, lng, lnb, wout, alpha, name):
    s, d = xin.shape
    kw = adw.shape[1]
    hb = CONV_HALO
    tm = min(TM_CONV, s)
    n = s // tm
    assert kw - 1 <= hb <= tm

    def body(x_ref, g_ref, b_ref, win_ref, dw_ref, dwb_ref, lng_ref, lnb_ref, wout_ref,
             xo_ref, rs_ref, p_ref, chat_ref, rsc_ref, u_scr, u8_scr):
        @pl.when(pl.program_id(0) == 0)
        def _():
            u_scr[0:hb, :] = jnp.zeros((hb, d), F32)

        x = x_ref[...] * g_ref[...] + b_ref[...]
        pm = _dot(x.astype(BF16), win_ref[...])
        p_ref[...] = pm.astype(BF16)
        u = pm[:, :d] * _sigmoid(pm[:, d:])
        u_scr[hb:hb + tm, :] = u
        _fill_shifted(u_scr, u8_scr)
        acc = dwb_ref[...] + dw_ref[kw - 1:kw, :] * u
        for k in range(kw - 1):
            acc = acc + dw_ref[k:k + 1, :] * _tap(u_scr, u8_scr, hb - (kw - 1) + k, tm)
        u_scr[0:hb, :] = u_scr[tm:tm + hb, :]
        chat, rstdc = _ln_stats(acc)
        chat_ref[...] = chat.astype(BF16)
        rsc_ref[...] = rstdc
        nv = chat * lng_ref[...] + lnb_ref[...]
        sv = (nv * _sigmoid(nv)).astype(BF16)
        xhat, rstd = _ln_stats(alpha * x + _dot(sv, wout_ref[...]))
        xo_ref[...] = xhat
        rs_ref[...] = rstd

    tile = pl.BlockSpec((tm, d), lambda i: (i, 0))
    col = pl.BlockSpec((tm, 1), lambda i: (i, 0))
    return pl.pallas_call(
        body, grid=(n,),
        in_specs=[tile, _rowspec(d), _rowspec(d), _wspec(win, l),
                  pl.BlockSpec((None, kw, d), lambda i: (l, 0, 0)), _rowspec(d), _rowspec(d), _rowspec(d),
                  _wspec(wout, l)],
        out_specs=[tile, col, pl.BlockSpec((tm, 2 * d), lambda i: (i, 0)), tile, col],
        out_shape=[jax.ShapeDtypeStruct((s, d), F32), jax.ShapeDtypeStruct((s, 1), F32),
                   jax.ShapeDtypeStruct((s, 2 * d), BF16), jax.ShapeDtypeStruct((s, d), BF16),
                   jax.ShapeDtypeStruct((s, 1), F32)],
        scratch_shapes=[pltpu.VMEM((tm + hb, d), F32), pltpu.VMEM((7, tm + hb - 8, d), F32)],
        compiler_params=_cparams(), name=name,
    )(xin, gin, bin_, win, adw, adwb, lng, lnb, wout)


def _conv_bwd1(dr1, chat, rstdc, p, wout, l, adw, lng, lnb, gwout_buf, name):
    s, d = dr1.shape
    kw = adw.shape[1]
    hb = CONV_HALO
    tm = min(TM_CONV, s)
    n = s // tm
    halo_blocks = tm // hb

    def body(dr_ref, chat_ref, rsc_ref, p_ref, halo_ref, wout_ref, dw_ref, lng_ref, lnb_ref, buf_ref,
             dp_ref, gwout_ref, gdw_ref, gdwb_ref, glng_ref, glnb_ref, u_scr, dc_scr, u8_scr, dc8_scr):
        i = pl.program_id(0)
        t = n - 1 - i

        @pl.when(i == 0)
        def _():
            dc_scr[tm:tm + hb, :] = jnp.zeros((hb, d), F32)
            gwout_ref[...] = jnp.zeros_like(gwout_ref)
            gdw_ref[...] = jnp.zeros_like(gdw_ref)
            gdwb_ref[...] = jnp.zeros_like(gdwb_ref)
            glng_ref[...] = jnp.zeros_like(glng_ref)
            glnb_ref[...] = jnp.zeros_like(glnb_ref)

        dob = dr_ref[...].astype(BF16)
        chat = chat_ref[...].astype(F32)
        lng = lng_ref[...]
        nv = chat * lng + lnb_ref[...]
        sgn = _sigmoid(nv)
        gwout_ref[...] += _dot_tn((nv * sgn).astype(BF16), dob)
        dn = _dot_nt(dob, wout_ref[...]) * (sgn * (1.0 + nv * (1.0 - sgn)))
        glng_ref[...] += _colsum(dn * chat)
        glnb_ref[...] += _colsum(dn)
        dc = _ln_bwd(dn, chat, rsc_ref[...], lng)
        gdwb_ref[...] += _colsum(dc)

        pm = p_ref[...].astype(F32)
        a = pm[:, :d]
        sg = _sigmoid(pm[:, d:])
        ph = halo_ref[...].astype(F32)
        u_scr[0:hb, :] = jnp.where(t > 0, ph[:, :d] * _sigmoid(ph[:, d:]), 0.0)
        u_scr[hb:hb + tm, :] = a * sg
        dc_scr[0:tm, :] = dc
        _fill_shifted(u_scr, u8_scr)
        _fill_shifted(dc_scr, dc8_scr)
        du = dw_ref[kw - 1:kw, :] * dc
        for k in range(kw):
            gdw_ref[k:k + 1, :] += _colsum(dc * _tap(u_scr, u8_scr, hb - (kw - 1) + k, tm))
            if k < kw - 1:
                du = du + dw_ref[k:k + 1, :] * _tap(dc_scr, dc8_scr, kw - 1 - k, tm)
        dc_scr[tm:tm + hb, :] = dc[0:hb, :]
        dp_ref[:, :d] = (du * sg).astype(BF16)
        dp_ref[:, d:] = (du * a * sg * (1.0 - sg)).astype(BF16)

    tile = pl.BlockSpec((tm, d), lambda i: (n - 1 - i, 0))
    col = pl.BlockSpec((tm, 1), lambda i: (n - 1 - i, 0))
    nl = gwout_buf.shape[0]
    return pl.pallas_call(
        body, grid=(n,),
        in_specs=[tile, tile, col, pl.BlockSpec((tm, 2 * d), lambda i: (n - 1 - i, 0)),
                  pl.BlockSpec((hb, 2 * d), lambda i: (jnp.maximum((n - 1 - i) * halo_blocks - 1, 0), 0)),
                  _wspec(wout, l), pl.BlockSpec((None, kw, d), lambda i: (l, 0, 0)), _rowspec(d), _rowspec(d),
                  pl.BlockSpec(memory_space=pl.ANY)],
        out_specs=[pl.BlockSpec((tm, 2 * d), lambda i: (n - 1 - i, 0)),
                   pl.BlockSpec((None, d, d), lambda i: (l, 0, 0)),
                   pl.BlockSpec((kw, d), lambda i: (0, 0)), _rowspec(d), _rowspec(d), _rowspec(d)],
        out_shape=[jax.ShapeDtypeStruct((s, 2 * d), BF16), jax.ShapeDtypeStruct((nl, d, d), F32),
                   jax.ShapeDtypeStruct((kw, d), F32), jax.ShapeDtypeStruct((1, d), F32),
                   jax.ShapeDtypeStruct((1, d), F32), jax.ShapeDtypeStruct((1, d), F32)],
        scratch_shapes=[pltpu.VMEM((tm + hb, d), F32), pltpu.VMEM((tm + hb, d), F32),
                        pltpu.VMEM((7, tm + hb - 8, d), F32), pltpu.VMEM((7, tm + hb - 8, d), F32)],
        input_output_aliases={9: 1},
        compiler_params=_cparams(), name=name,
    )(dr1, chat, rstdc, p, p, wout, adw, lng, lnb, gwout_buf)


def _sgu_gate(vn, wm_ref, bs_ref, s_scr, tm, nh):
    for ch in range(tm // CHUNK):
        r0 = ch * CHUNK
        for h in range(nh):
            c0 = h * CHUNK
            s_scr[r0:r0 + CHUNK, c0:c0 + CHUNK] = (
                _dot(wm_ref[h], vn[r0:r0 + CHUNK, c0:c0 + CHUNK]) + bs_ref[:, c0:c0 + CHUNK])


def _sgu_fwd(xin, gin, bin_, win, lg, lb, wm, bs_exp, wout, alpha, name):
    s, d = xin.shape
    nh = wm.shape[0]
    tm = min(TM_SGU, s)
    n = s // tm
    assert tm % CHUNK == 0 and nh * CHUNK == d

    def body(x_ref, g_ref, b_ref, win_ref, lg_ref, lb_ref, wm_ref, bs_ref, wout_ref,
             xo_ref, rs_ref, zp_ref, s_scr):
        x = x_ref[...] * g_ref[...] + b_ref[...]
        zp = _dot(x.astype(BF16), win_ref[...])
        zp_ref[...] = zp.astype(BF16)
        z = _gelu(zp)
        vhat, _ = _ln_stats(z[:, d:])
        vn = (vhat * lg_ref[...] + lb_ref[...]).astype(BF16)
        _sgu_gate(vn, wm_ref, bs_ref, s_scr, tm, nh)
        q = (z[:, :d] * s_scr[...]).astype(BF16)
        xhat, rstd = _ln_stats(alpha * x + _dot(q, wout_ref[...]))
        xo_ref[...] = xhat
        rs_ref[...] = rstd

    tile = pl.BlockSpec((tm, d), lambda i: (i, 0))
    return pl.pallas_call(
        body, grid=(n,),
        in_specs=[tile, _rowspec(d), _rowspec(d), _wspec(win, 0), _rowspec(d), _rowspec(d),
                  _resident((nh, CHUNK, CHUNK), lambda i: (0, 0, 0)),
                  _resident((CHUNK, d), lambda i: (0, 0)), _wspec(wout, 0)],
        out_specs=[tile, pl.BlockSpec((tm, 1), lambda i: (i, 0)), pl.BlockSpec((tm, 2 * d), lambda i: (i, 0))],
        out_shape=[jax.ShapeDtypeStruct((s, d), F32), jax.ShapeDtypeStruct((s, 1), F32),
                   jax.ShapeDtypeStruct((s, 2 * d), BF16)],
        scratch_shapes=[pltpu.VMEM((tm, d), F32)],
        compiler_params=_cparams(parallel=True), name=name,
    )(xin, gin, bin_, win, lg, lb, wm, bs_exp, wout)


def _sgu_bwd1(dr1, zp, wout, lg, lb, wm, wmt, bs_exp, gwout_buf, name):
    s, d = dr1.shape
    nh = wm.shape[0]
    tm = min(TM_SGU, s)
    n = s // tm

    def body(dr_ref, zp_ref, wout_ref, lg_ref, lb_ref, wm_ref, wmt_ref, bs_ref, buf_ref,
             dzp_ref, gwout_ref, gws_ref, gbs_ref, glg_ref, glb_ref, s_scr, dvn_scr, bs_acc):
        i = pl.program_id(0)

        @pl.when(i == 0)
        def _():
            gwout_ref[...] = jnp.zeros_like(gwout_ref)
            gws_ref[...] = jnp.zeros_like(gws_ref)
            glg_ref[...] = jnp.zeros_like(glg_ref)
            glb_ref[...] = jnp.zeros_like(glb_ref)
            bs_acc[...] = jnp.zeros_like(bs_acc)

        dob = dr_ref[...].astype(BF16)
        zp = zp_ref[...].astype(F32)
        z = _gelu(zp)
        u = z[:, :d]
        lg = lg_ref[...]
        vhat, rstdv = _ln_stats(z[:, d:])
        vn = (vhat * lg + lb_ref[...]).astype(BF16)
        _sgu_gate(vn, wm_ref, bs_ref, s_scr, tm, nh)
        sv = s_scr[...]
        gwout_ref[...] += _dot_tn((u * sv).astype(BF16), dob)
        dq = _dot_nt(dob, wout_ref[...])
        ds = dq * u
        dsb = ds.astype(BF16)
        part = jnp.zeros((CHUNK, d), F32)
        for ch in range(tm // CHUNK):
            r0 = ch * CHUNK
            part = part + ds[r0:r0 + CHUNK, :]
            for h in range(nh):
                c0 = h * CHUNK
                blk = dsb[r0:r0 + CHUNK, c0:c0 + CHUNK]
                gws_ref[h] += _dot_nt(blk, vn[r0:r0 + CHUNK, c0:c0 + CHUNK])
                dvn_scr[r0:r0 + CHUNK, c0:c0 + CHUNK] = _dot(wmt_ref[h], blk)
        bs_acc[...] += part
        dvn = dvn_scr[...]
        glg_ref[...] += _colsum(dvn * vhat)
        glb_ref[...] += _colsum(dvn)
        dv = _ln_bwd(dvn, vhat, rstdv, lg)
        gp = _gelu_grad(zp)
        dzp_ref[:, :d] = (dq * sv * gp[:, :d]).astype(BF16)
        dzp_ref[:, d:] = (dv * gp[:, d:]).astype(BF16)

        @pl.when(i == n - 1)
        def _():
            rows = lax.broadcasted_iota(jnp.int32, (CHUNK, CHUNK), 0)
            cols = lax.broadcasted_iota(jnp.int32, (CHUNK, CHUNK), 1)
            tril = (cols <= rows).astype(F32)
            acc = bs_acc[...]
            for h in range(nh):
                gws_ref[h] = gws_ref[h] * tril
                gbs_ref[:, h:h + 1] = jnp.sum(acc[:, h * CHUNK:(h + 1) * CHUNK], axis=1, keepdims=True)

    tile = pl.BlockSpec((tm, d), lambda i: (i, 0))
    wide = pl.BlockSpec((tm, 2 * d), lambda i: (i, 0))
    hspec = _resident((nh, CHUNK, CHUNK), lambda i: (0, 0, 0))
    return pl.pallas_call(
        body, grid=(n,),
        in_specs=[tile, wide, _wspec(wout, 0), _rowspec(d), _rowspec(d), hspec, hspec,
                  _resident((CHUNK, d), lambda i: (0, 0)), pl.BlockSpec(memory_space=pl.ANY)],
        out_specs=[wide, pl.BlockSpec((None, d, d), lambda i: (0, 0, 0)),
                   pl.BlockSpec((nh, CHUNK, CHUNK), lambda i: (0, 0, 0)),
                   pl.BlockSpec((CHUNK, nh), lambda i: (0, 0)), _rowspec(d), _rowspec(d)],
        out_shape=[jax.ShapeDtypeStruct((s, 2 * d), BF16), jax.ShapeDtypeStruct(gwout_buf.shape, F32),
                   jax.ShapeDtypeStruct((nh, CHUNK, CHUNK), F32), jax.ShapeDtypeStruct((CHUNK, nh), F32),
                   jax.ShapeDtypeStruct((1, d), F32), jax.ShapeDtypeStruct((1, d), F32)],
        scratch_shapes=[pltpu.VMEM((tm, d), F32), pltpu.VMEM((tm, d), F32), pltpu.VMEM((CHUNK, d), F32)],
        input_output_aliases={8: 1},
        compiler_params=_cparams(), name=name,
    )(dr1, zp, wout, lg, lb, wm, wmt, bs_exp, gwout_buf)


def _pool_counts(t0, tm, w):
    pos = t0 + lax.broadcasted_iota(jnp.int32, (tm, 1), 0)
    return jnp.minimum(pos + 1, w).astype(F32)


def _pool_fwd(xin, gin, bin_, win, wg, scale, wout, alpha, name):
    s, d = xin.shape
    ng, dg = wg.shape[0], wg.shape[1]
    hb = POOL_HALO
    tm = min(TM_POOL, s)
    n = s // tm
    assert ng == len(POOL_WINDOWS) and ng * dg == d and max(POOL_WINDOWS) <= hb

    def body(x_ref, g_ref, b_ref, win_ref, wg_ref, sc_ref, wout_ref, xo_ref, rs_ref, ys_ref, y_scr, z_scr):
        i = pl.program_id(0)

        @pl.when(i == 0)
        def _():
            y_scr[0:hb, :] = jnp.zeros((hb, d), F32)

        x = x_ref[...] * g_ref[...] + b_ref[...]
        y = _dot(x.astype(BF16), win_ref[...])
        ys_ref[...] = y.astype(BF16)
        y_scr[hb:hb + tm, :] = y
        for g, w in enumerate(POOL_WINDOWS):
            c0 = g * dg
            acc = y[:, c0:c0 + dg]
            for dd in range(1, w):
                acc = acc + y_scr[pl.ds(hb - dd, tm), c0:c0 + dg]
            pg = acc / _pool_counts(i * tm, tm, w) - y[:, c0:c0 + dg]
            z_scr[:, c0:c0 + dg] = _dot(pg.astype(BF16), wg_ref[g])
        y_scr[0:hb, :] = y_scr[tm:tm + hb, :]
        zz = (z_scr[...] * sc_ref[...]).astype(BF16)
        xhat, rstd = _ln_stats(alpha * x + _dot(zz, wout_ref[...]))
        xo_ref[...] = xhat
        rs_ref[...] = rstd

    tile = pl.BlockSpec((tm, d), lambda i: (i, 0))
    return pl.pallas_call(
        body, grid=(n,),
        in_specs=[tile, _rowspec(d), _rowspec(d), _wspec(win, 0),
                  _resident((ng, dg, dg), lambda i: (0, 0, 0)), _rowspec(d), _wspec(wout, 0)],
        out_specs=[tile, pl.BlockSpec((tm, 1), lambda i: (i, 0)), tile],
        out_shape=[jax.ShapeDtypeStruct((s, d), F32), jax.ShapeDtypeStruct((s, 1), F32),
                   jax.ShapeDtypeStruct((s, d), BF16)],
        scratch_shapes=[pltpu.VMEM((tm + hb, d), F32), pltpu.VMEM((tm, d), F32)],
        compiler_params=_cparams(), name=name,
    )(xin, gin, bin_, win, wg, scale, wout)


def _pool_bwd1(dr1, ys, wout, wg, scale, gwout_buf, name):
    s, d = dr1.shape
    ng, dg = wg.shape[0], wg.shape[1]
    hb = POOL_HALO
    tm = min(TM_POOL, s)
    n = s // tm
    halo_blocks = tm // hb

    def body(dr_ref, ys_ref, halo_ref, wout_ref, wg_ref, sc_ref, buf_ref,
             dy_ref, gwout_ref, gwg_ref, gsc_ref, y_scr, e_scr, z_scr, dp_scr):
        i = pl.program_id(0)
        t = n - 1 - i

        @pl.when(i == 0)
        def _():
            e_scr[tm:tm + hb, :] = jnp.zeros((hb, d), F32)
            gwout_ref[...] = jnp.zeros_like(gwout_ref)
            gwg_ref[...] = jnp.zeros_like(gwg_ref)
            gsc_ref[...] = jnp.zeros_like(gsc_ref)

        dob = dr_ref[...].astype(BF16)
        y = ys_ref[...].astype(F32)
        y_scr[0:hb, :] = jnp.where(t > 0, halo_ref[...].astype(F32), 0.0)
        y_scr[hb:hb + tm, :] = y
        pgs = []
        for g, w in enumerate(POOL_WINDOWS):
            c0 = g * dg
            acc = y[:, c0:c0 + dg]
            for dd in range(1, w):
                acc = acc + y_scr[pl.ds(hb - dd, tm), c0:c0 + dg]
            pg = (acc / _pool_counts(t * tm, tm, w) - y[:, c0:c0 + dg]).astype(BF16)
            pgs.append(pg)
            z_scr[:, c0:c0 + dg] = _dot(pg, wg_ref[g])
        zpre = z_scr[...]
        sc = sc_ref[...]
        gwout_ref[...] += _dot_tn((zpre * sc).astype(BF16), dob)
        dz = _dot_nt(dob, wout_ref[...])
        gsc_ref[...] += _colsum(dz * zpre)
        dzpre = (dz * sc).astype(BF16)
        for g, w in enumerate(POOL_WINDOWS):
            c0 = g * dg
            dzg = dzpre[:, c0:c0 + dg]
            gwg_ref[g] += _dot_tn(pgs[g], dzg)
            dp = _dot_nt(dzg, wg_ref[g])
            dp_scr[:, c0:c0 + dg] = dp
            e_scr[0:tm, c0:c0 + dg] = dp / _pool_counts(t * tm, tm, w)
        for g, w in enumerate(POOL_WINDOWS):
            c0 = g * dg
            acc = e_scr[0:tm, c0:c0 + dg]
            for dd in range(1, w):
                acc = acc + e_scr[pl.ds(dd, tm), c0:c0 + dg]
            dy_ref[:, c0:c0 + dg] = (acc - dp_scr[:, c0:c0 + dg]).astype(BF16)
        e_scr[tm:tm + hb, :] = e_scr[0:hb, :]

    tile = pl.BlockSpec((tm, d), lambda i: (n - 1 - i, 0))
    return pl.pallas_call(
        body, grid=(n,),
        in_specs=[tile, tile,
                  pl.BlockSpec((hb, d), lambda i: (jnp.maximum((n - 1 - i) * halo_blocks - 1, 0), 0)),
                  _wspec(wout, 0), _resident((ng, dg, dg), lambda i: (0, 0, 0)), _rowspec(d),
                  pl.BlockSpec(memory_space=pl.ANY)],
        out_specs=[tile, pl.BlockSpec((None, d, d), lambda i: (0, 0, 0)),
                   pl.BlockSpec((ng, dg, dg), lambda i: (0, 0, 0)), _rowspec(d)],
        out_shape=[jax.ShapeDtypeStruct((s, d), BF16), jax.ShapeDtypeStruct(gwout_buf.shape, F32),
                   jax.ShapeDtypeStruct((ng, dg, dg), F32), jax.ShapeDtypeStruct((1, d), F32)],
        scratch_shapes=[pltpu.VMEM((tm + hb, d), F32), pltpu.VMEM((tm + hb, d), F32),
                        pltpu.VMEM((tm, d), F32), pltpu.VMEM((tm, d), F32)],
        input_output_aliases={6: 1},
        compiler_params=_cparams(), name=name,
    )(dr1, ys, ys, wout, wg, scale, gwout_buf)


def _loss_head(xh, g, b, target, name):
    s, d = xh.shape
    tm = min(512, s)
    n = s // tm

    def body(xh_ref, g_ref, b_ref, t_ref, dy_ref, loss_ref, acc):
        i = pl.program_id(0)

        @pl.when(i == 0)
        def _():
            acc[...] = jnp.zeros_like(acc)

        err = xh_ref[...] * g_ref[...] + b_ref[...] - t_ref[...]
        dy_ref[...] = err * (1.0 / d)
        acc[...] += _colsum(err * err)

        @pl.when(i == n - 1)
        def _():
            loss_ref[...] = (0.5 / d) * jnp.sum(acc[...], axis=1, keepdims=True)

    tile = pl.BlockSpec((tm, d), lambda i: (i, 0))
    return pl.pallas_call(
        body, grid=(n,),
        in_specs=[tile, _rowspec(d), _rowspec(d), tile],
        out_specs=[tile, pl.BlockSpec((1, 1), lambda i: (0, 0))],
        out_shape=[jax.ShapeDtypeStruct((s, d), F32), jax.ShapeDtypeStruct((1, 1), F32)],
        scratch_shapes=[pltpu.VMEM((1, d), F32)],
        compiler_params=_cparams(), name=name,
    )(xh, g, b, target)


def _elementwise(fn, ins, out_dtypes, name, lead=0):
    shape = ins[0].shape
    c = shape[-1]
    r = math.prod(shape[lead:-1])
    lead_shape = tuple(shape[:lead])
    out_shape = tuple(shape[lead:])
    tr = _pick_rows(r, c, 4, (1 << 20) // max(1, math.prod(lead_shape)))
    nlead = len(lead_shape)

    def body(*refs):
        vals = fn(*[ref[...] for ref in refs[:len(ins)]])
        for ref, v in zip(refs[len(ins):], vals):
            ref[...] = v.astype(ref.dtype)

    def spec(a_lead):
        if a_lead:
            return pl.BlockSpec(lead_shape + (tr, c), lambda i: (0,) * nlead + (i, 0))
        return pl.BlockSpec((tr, c), lambda i: (i, 0))

    args = [ins[0].reshape(lead_shape + (r, c))] + [a.reshape(r, c) for a in ins[1:]]
    outs = pl.pallas_call(
        body, grid=(r // tr,),
        in_specs=[spec(bool(lead))] + [spec(False)] * (len(ins) - 1),
        out_specs=[spec(False)] * len(out_dtypes),
        out_shape=[jax.ShapeDtypeStruct((r, c), dt) for dt in out_dtypes],
        compiler_params=_cparams(parallel=True), name=name,
    )(*args)
    return [o.reshape(out_shape) for o in outs]


def _prefetch_call(body, grid, in_specs, out_specs, out_shape, name):
    return pl.pallas_call(
        body,
        grid_spec=pltpu.PrefetchScalarGridSpec(num_scalar_prefetch=1, grid=grid, in_specs=in_specs, out_specs=out_specs),
        out_shape=out_shape,
        compiler_params=_cparams(len(grid), parallel=True), name=name)


def _cast_into_full(w3, kind, chip1, name):
    l, r, c = w3.shape
    tr = _pick_rows(r, c, 4, 1 << 20)

    def body(k_ref, w_ref, o_ref):
        o_ref[...] = w_ref[...].astype(BF16)

    if kind == "row":
        out_spec = pl.BlockSpec((None, None, tr, c), lambda a, j, k: (a, k[0], j, 0))
    else:
        out_spec = pl.BlockSpec((None, tr, c), lambda a, j, k: (a, j, k[0]))
    return _prefetch_call(
        body, (l, r // tr), [pl.BlockSpec((None, tr, c), lambda a, j, k: (a, j, 0))], out_spec,
        jax.ShapeDtypeStruct(_full_shape(kind, w3.shape), BF16), name)(chip1, w3)


def _pair_sum(g, got, kind, core1, name):
    if kind == "row":
        l, nc, sr, c = g.shape
        g5, got3 = g.reshape(l * nc, 2, sr // 2, c), got.reshape(l * nc, sr // 2, c)
    else:
        l, r, c = g.shape
        g5, got3 = g.reshape(l, 2, r // 2, c), got
    a, _, hr, c = g5.shape
    tr = _pick_rows(hr, c, 4, 1 << 20)

    def body(c_ref, g_ref, t_ref, o_ref):
        o_ref[...] = (g_ref[...] + t_ref[...]).astype(BF16)

    half = pl.BlockSpec((None, tr, c), lambda i, j, cc: (i, j, 0))
    out = _prefetch_call(
        body, (a, hr // tr), [pl.BlockSpec((None, None, tr, c), lambda i, j, cc: (i, cc[0], j, 0)), half], half,
        jax.ShapeDtypeStruct(got3.shape, BF16), name)(core1, g5, got3)
    return out.reshape(got.shape)


def _chip_sum(t, rb, kind, chip1, name):
    _, l, hr, sc = rb.shape
    tr = _pick_rows(hr, sc, 4, 1 << 19)

    def body(k_ref, t_ref, rb_ref, o_ref):
        acc = t_ref[...].astype(F32)
        for r in range(N_CHIPS - 1):
            acc = acc + rb_ref[r].astype(F32)
        o_ref[...] = acc

    if kind == "row":
        t_spec = pl.BlockSpec((None, None, tr, sc), lambda a, j, k: (a, k[0], j, 0))
    else:
        t_spec = pl.BlockSpec((None, tr, sc), lambda a, j, k: (a, j, k[0]))
    return _prefetch_call(
        body, (l, hr // tr),
        [t_spec, pl.BlockSpec((N_CHIPS - 1, None, tr, sc), lambda a, j, k: (0, a, j, 0))],
        pl.BlockSpec((None, tr, sc), lambda a, j, k: (a, j, 0)),
        jax.ShapeDtypeStruct((l, hr, sc), F32), name)(chip1, t, rb)


def _adamw_math(w_, g_, m_, v_):
    m2 = ADAM_B1 * m_ + (1.0 - ADAM_B1) * g_
    v2 = ADAM_B2 * v_ + (1.0 - ADAM_B2) * (g_ * g_)
    m_hat = m2 / (1.0 - ADAM_B1 ** ADAM_STEP)
    v_hat = v2 / (1.0 - ADAM_B2 ** ADAM_STEP)
    delta = -ADAM_LR * (m_hat / (jnp.sqrt(v_hat) + ADAM_EPS) + ADAM_WD * w_)
    return delta, m2, v2


def _adamw_big(w, m, v, own, other, core1, name):
    l, hr, c = own.shape
    view = lambda a: a.reshape(l, 2, hr, c)
    tr = _pick_rows(hr, c, 4, 1 << 18)

    def body(c_ref, w_ref, m_ref, v_ref, own_ref, oth_ref, g_ref, d_ref, m2_ref, v2_ref):
        g = jnp.where(pl.program_id(1) == c_ref[0], own_ref[...], oth_ref[...])
        g_ref[...] = g
        d_ref[...], m2_ref[...], v2_ref[...] = _adamw_math(w_ref[...], g, m_ref[...], v_ref[...])

    s4 = pl.BlockSpec((None, None, tr, c), lambda a, h, j, cc: (a, h, j, 0))
    s3 = pl.BlockSpec((None, tr, c), lambda a, h, j, cc: (a, j, 0))
    outs = _prefetch_call(
        body, (l, 2, hr // tr), [s4, s4, s4, s3, s3], [s4] * 4,
        [jax.ShapeDtypeStruct((l, 2, hr, c), F32)] * 4, name)(core1, view(w), view(m), view(v), own, other)
    return [o.reshape(w.shape) for o in outs]


def _sum_lead(stacked, name):
    def fn(v):
        acc = v[0]
        for k in range(1, v.shape[0]):
            acc = acc + v[k]
        return (acc,)

    return _elementwise(fn, [stacked], [F32], name, lead=1)[0]


def _adamw(w, g, m, v, name):
    return _elementwise(_adamw_math, [w, g, m, v], [F32, F32, F32], name)


ANY = pl.BlockSpec(memory_space=pl.ANY)


def _mesh_pos():
    return lax.axis_index("x"), lax.axis_index("y"), lax.axis_index("c")


def _chip_peers(x, y, c):
    out = []
    for r in (1, 2, 3):
        px = 1 - x if r & 2 else x
        py = 1 - y if r & 1 else y
        out.append((2 * px + py, (px, py, c)))
    return out


def _full_shape(kind, shard_shape):
    l, r, c = shard_shape
    return (l, N_CHIPS, r, c) if kind == "row" else (l, r, N_CHIPS * c)


def _full_piece(ref, kind, k, h, hr, sc):
    rows = pl.ds(pl.multiple_of(h * hr, SUBLANES_BF16), hr)
    if kind == "row":
        return ref.at[:, k, rows, :]
    return ref.at[:, rows, pl.ds(pl.multiple_of(k * sc, LANES), sc)]


def _remote(src, dst, ssem, rsem, dev):
    return pltpu.make_async_remote_copy(src_ref=src, dst_ref=dst, send_sem=ssem, recv_sem=rsem,
                                        device_id=dev, device_id_type=MESH)


DMA_CHUNK_BYTES = 1 << 20
DMA_MAX_CHUNKS = 32


def _chunk_views(src, dst):
    axis = len(src.shape) - 2
    rows = src.shape[axis]
    nbytes = math.prod(src.shape) * jnp.dtype(src.dtype).itemsize
    n = max(1, min(DMA_MAX_CHUNKS, nbytes // DMA_CHUNK_BYTES))
    while n > 1 and (rows % n or (rows // n) % SUBLANES_BF16):
        n -= 1
    cr = rows // n
    out = []
    for i in range(n):
        idx = (slice(None),) * axis + (pl.ds(i * cr, cr), slice(None))
        out.append((src.at[idx], dst.at[idx]))
    return out


def _start_remote(src, dst, ssem, rsem, dev):
    for s, t in _chunk_views(src, dst):
        _remote(s, t, ssem, rsem, dev).start()
    return _remote(src, dst, ssem, rsem, dev)


def _allgather_weights(fulls, kinds):
    nw = len(fulls)

    def dims(a, kind):
        return (a.shape[2] // 2, a.shape[3]) if kind == "row" else (a.shape[1] // 2, a.shape[2] // N_CHIPS)

    hrs = [dims(a, k)[0] for a, k in zip(fulls, kinds)]
    scs = [dims(a, k)[1] for a, k in zip(fulls, kinds)]

    def body(*refs):
        mine, fu = refs[:nw], refs[nw:2 * nw]
        send1, recv1, send2, recv2 = refs[2 * nw:]
        x, y, c = _mesh_pos()
        k_me = 2 * x + y
        sibling = (x, y, 1 - c)
        peers = _chip_peers(x, y, c)

        def piece(ref, w, k, h):
            return _full_piece(ref, kinds[w], k, h, hrs[w], scs[w])

        sends = []
        for w in range(nw):
            for r, (_, dev) in enumerate(peers):
                sends.append(_start_remote(piece(mine[w], w, k_me, c), piece(fu[w], w, k_me, c),
                                           send1.at[3 * w + r], recv1.at[3 * w + r], dev))
        for w in range(nw):
            for r, (kj, dev) in enumerate(peers):
                _remote(piece(mine[w], w, k_me, c), piece(fu[w], w, kj, c),
                        send1.at[3 * w + r], recv1.at[3 * w + r], dev).wait_recv()
                sends.append(_start_remote(piece(fu[w], w, kj, c), piece(fu[w], w, kj, c),
                                           send2.at[3 * w + r], recv2.at[3 * w + r], sibling))
        for w in range(nw):
            for r, (kj, _) in enumerate(peers):
                _remote(piece(fu[w], w, kj, 1 - c), piece(fu[w], w, kj, 1 - c),
                        send2.at[3 * w + r], recv2.at[3 * w + r], sibling).wait_recv()
        for cp in sends:
            cp.wait_send()

    return pl.pallas_call(
        body,
        in_specs=[ANY] * nw, out_specs=[ANY] * nw,
        out_shape=[jax.ShapeDtypeStruct(a.shape, a.dtype) for a in fulls],
        scratch_shapes=[pltpu.SemaphoreType.DMA((3 * nw,))] * 4,
        input_output_aliases={w: w for w in range(nw)},
        name="allgather_weights",
    )(*fulls)


def _rs_pair(fulls, kinds):
    nw = len(fulls)

    def half_all(ref, kind, h):
        if kind == "row":
            hr = ref.shape[2] // 2
            return ref.at[:, :, pl.ds(pl.multiple_of(h * hr, SUBLANES_BF16), hr), :]
        hr = ref.shape[1] // 2
        return ref.at[:, pl.ds(pl.multiple_of(h * hr, SUBLANES_BF16), hr), :]

    def half_shape(kind, shape):
        if kind == "row":
            return (shape[0], shape[1], shape[2] // 2, shape[3])
        return (shape[0], shape[1] // 2, shape[2])

    def body(*refs):
        g, got = refs[:nw], refs[nw:2 * nw]
        ssem, rsem = refs[2 * nw:]
        x, y, c = _mesh_pos()
        sibling = (x, y, 1 - c)
        cps = [_start_remote(half_all(g[w], kinds[w], 1 - c), got[w], ssem.at[w], rsem.at[w], sibling)
               for w in range(nw)]
        for cp in cps:
            cp.wait_recv()
        for cp in cps:
            cp.wait_send()

    shapes = [jax.ShapeDtypeStruct(half_shape(k, a.shape), a.dtype) for k, a in zip(kinds, fulls)]
    return pl.pallas_call(
        body, in_specs=[ANY] * nw, out_specs=[ANY] * nw, out_shape=shapes,
        scratch_shapes=[pltpu.SemaphoreType.DMA((nw,))] * 2, name="rs_pair",
    )(*fulls)


def _rs_chips(parts, kinds):
    nw = len(parts)

    def slot(ref, kind, k):
        if kind == "row":
            return ref.at[:, k]
        sc = ref.shape[2] // N_CHIPS
        return ref.at[:, :, pl.ds(pl.multiple_of(k * sc, LANES), sc)]

    def slot_shape(kind, shape):
        if kind == "row":
            return (shape[0], shape[2], shape[3])
        return (shape[0], shape[1], shape[2] // N_CHIPS)

    def body(*refs):
        t, rb = refs[:nw], refs[nw:2 * nw]
        ssem, rsem = refs[2 * nw:]
        x, y, c = _mesh_pos()
        cps = []
        for w in range(nw):
            for r, (kj, dev) in enumerate(_chip_peers(x, y, c)):
                cps.append(_start_remote(slot(t[w], kinds[w], kj), rb[w].at[r],
                                         ssem.at[3 * w + r], rsem.at[3 * w + r], dev))
        for cp in cps:
            cp.wait_recv()
        for cp in cps:
            cp.wait_send()

    shapes = [jax.ShapeDtypeStruct((N_CHIPS - 1,) + slot_shape(k, a.shape), a.dtype) for k, a in zip(kinds, parts)]
    return pl.pallas_call(
        body, in_specs=[ANY] * nw, out_specs=[ANY] * nw, out_shape=shapes,
        scratch_shapes=[pltpu.SemaphoreType.DMA((3 * nw,))] * 2, name="rs_chips",
    )(*parts)


def _rs_join(halves):
    nw = len(halves)

    def body(*refs):
        src, dst = refs[:nw], refs[nw:2 * nw]
        ssem, rsem = refs[2 * nw:]
        x, y, c = _mesh_pos()
        cps = [_start_remote(src[w], dst[w], ssem.at[w], rsem.at[w], (x, y, 1 - c)) for w in range(nw)]
        for cp in cps:
            cp.wait_recv()
        for cp in cps:
            cp.wait_send()

    return pl.pallas_call(
        body, in_specs=[ANY] * nw, out_specs=[ANY] * nw,
        out_shape=[jax.ShapeDtypeStruct(a.shape, a.dtype) for a in halves],
        scratch_shapes=[pltpu.SemaphoreType.DMA((nw,))] * 2, name="rs_join",
    )(*halves)


def _allgather_small(buf, name):
    def body(in_ref, out_ref, ssem, rsem, lsem):
        x, y, c = _mesh_pos()
        me = 4 * x + 2 * y + c
        lc = pltpu.make_async_copy(in_ref, out_ref.at[me], lsem)
        lc.start()
        cps, waits = [], []
        for r in range(1, N_DEV):
            px = 1 - x if r & 4 else x
            py = 1 - y if r & 2 else y
            pc = 1 - c if r & 1 else c
            cp = _remote(in_ref, out_ref.at[me], ssem.at[r - 1], rsem.at[r - 1], (px, py, pc))
            cp.start()
            cps.append(cp)
            waits.append(_remote(in_ref, out_ref.at[4 * px + 2 * py + pc], ssem.at[r - 1], rsem.at[r - 1], (px, py, pc)))
        for wt in waits:
            wt.wait_recv()
        for cp in cps:
            cp.wait_send()
        lc.wait()

    return pl.pallas_call(
        body, in_specs=[ANY], out_specs=ANY,
        out_shape=jax.ShapeDtypeStruct((N_DEV,) + buf.shape, buf.dtype),
        scratch_shapes=[pltpu.SemaphoreType.DMA((N_DEV - 1,))] * 2 + [pltpu.SemaphoreType.DMA],
        name=name,
    )(buf)


def _pack(arrs):
    flat = jnp.concatenate([a.reshape(-1).astype(F32) for a in arrs])
    rows = -(-flat.shape[0] // (LANES * 16)) * 16
    return jnp.pad(flat, (0, rows * LANES - flat.shape[0])).reshape(rows, LANES)


def _unpack(buf, shapes):
    flat = buf.reshape(-1)
    out, off = [], 0
    for shp in shapes:
        nel = math.prod(shp)
        out.append(flat[off:off + nel].reshape(shp))
        off += nel
    return out


BIG = ("a_w_in", "a_w_out", "b_w_in", "b_w_out", "c_w_in", "c_w_grp", "c_w_out", "f_w_up", "f_w_down")
BIG_KIND = {"a_w_in": "col", "a_w_out": "row", "b_w_in": "col", "b_w_out": "row", "c_w_in": "row",
            "c_w_grp": "row", "c_w_out": "row", "f_w_up": "col", "f_w_down": "row"}
SHARDED_SMALL = ("a_dw", "a_dw_b", "a_ln_g", "a_ln_b", "c_scale", "f_dw")
REPLICATED = ("b_ln_g", "b_ln_b", "b_ws", "b_bs", "ln1_g", "ln1_b", "ln2_g", "ln2_b")
WEIGHTS = ("a_w_in", "a_dw", "a_dw_b", "a_ln_g", "a_ln_b", "a_w_out", "b_w_in", "b_ln_g", "b_ln_b", "b_ws", "b_bs",
           "b_w_out", "c_w_in", "c_w_grp", "c_scale", "c_w_out", "f_w_up", "f_dw", "f_w_down",
           "ln1_g", "ln1_b", "ln2_g", "ln2_b")


def _as3d(a):
    return a.reshape((-1,) + a.shape[-2:])


def kernel(x, a_w_in, a_dw, a_dw_b, a_ln_g, a_ln_b, a_w_out, b_w_in, b_ln_g, b_ln_b, b_ws, b_bs, b_w_out, c_w_in, c_w_grp, c_scale, c_w_out, f_w_up, f_dw, f_w_down, ln1_g, ln1_b, ln2_g, ln2_b, loss_target, m_a_w_in, m_a_dw, m_a_dw_b, m_a_ln_g, m_a_ln_b, m_a_w_out, m_b_w_in, m_b_ln_g, m_b_ln_b, m_b_ws, m_b_bs, m_b_w_out, m_c_w_in, m_c_w_grp, m_c_scale, m_c_w_out, m_f_w_up, m_f_dw, m_f_w_down, m_ln1_g, m_ln1_b, m_ln2_g, m_ln2_b, v_a_w_in, v_a_dw, v_a_dw_b, v_a_ln_g, v_a_ln_b, v_a_w_out, v_b_w_in, v_b_ln_g, v_b_ln_b, v_b_ws, v_b_bs, v_b_w_out, v_c_w_in, v_c_w_grp, v_c_scale, v_c_w_out, v_f_w_up, v_f_dw, v_f_w_down, v_ln1_g, v_ln1_b, v_ln2_g, v_ln2_b):
    w = dict(a_w_in=a_w_in, a_dw=a_dw, a_dw_b=a_dw_b, a_ln_g=a_ln_g, a_ln_b=a_ln_b, a_w_out=a_w_out, b_w_in=b_w_in, b_ln_g=b_ln_g, b_ln_b=b_ln_b, b_ws=b_ws, b_bs=b_bs, b_w_out=b_w_out, c_w_in=c_w_in, c_w_grp=c_w_grp, c_scale=c_scale, c_w_out=c_w_out, f_w_up=f_w_up, f_dw=f_dw, f_w_down=f_w_down, ln1_g=ln1_g, ln1_b=ln1_b, ln2_g=ln2_g, ln2_b=ln2_b)
    mom = dict(a_w_in=m_a_w_in, a_dw=m_a_dw, a_dw_b=m_a_dw_b, a_ln_g=m_a_ln_g, a_ln_b=m_a_ln_b, a_w_out=m_a_w_out, b_w_in=m_b_w_in, b_ln_g=m_b_ln_g, b_ln_b=m_b_ln_b, b_ws=m_b_ws, b_bs=m_b_bs, b_w_out=m_b_w_out, c_w_in=m_c_w_in, c_w_grp=m_c_w_grp, c_scale=m_c_scale, c_w_out=m_c_w_out, f_w_up=m_f_w_up, f_dw=m_f_dw, f_w_down=m_f_w_down, ln1_g=m_ln1_g, ln1_b=m_ln1_b, ln2_g=m_ln2_g, ln2_b=m_ln2_b)
    var = dict(a_w_in=v_a_w_in, a_dw=v_a_dw, a_dw_b=v_a_dw_b, a_ln_g=v_a_ln_g, a_ln_b=v_a_ln_b, a_w_out=v_a_w_out, b_w_in=v_b_w_in, b_ln_g=v_b_ln_g, b_ln_b=v_b_ln_b, b_ws=v_b_ws, b_bs=v_b_bs, b_w_out=v_b_w_out, c_w_in=v_c_w_in, c_w_grp=v_c_w_grp, c_scale=v_c_scale, c_w_out=v_c_w_out, f_w_up=v_f_w_up, f_dw=v_f_dw, f_w_down=v_f_w_down, ln1_g=v_ln1_g, ln1_b=v_ln1_b, ln2_g=v_ln2_g, ln2_b=v_ln2_b)

    depth = ln1_g.shape[0]
    d = x.shape[-1]
    alpha = float((2 * depth) ** 0.25)
    chip = 2 * lax.axis_index("x") + lax.axis_index("y")
    chip1 = chip.astype(jnp.int32).reshape(1)
    core1 = lax.axis_index("c").astype(jnp.int32).reshape(1)

    kinds = [BIG_KIND[k] for k in BIG]
    mine = [_cast_into_full(_as3d(w[k]), kind, chip1, "cast_" + k) for k, kind in zip(BIG, kinds)]
    gathered = _allgather_weights(mine, kinds)
    full = {}
    for k, kind, arr in zip(BIG, kinds, gathered):
        full[k] = arr.reshape(arr.shape[0], -1, arr.shape[-1]) if kind == "row" else arr

    small_all = _allgather_small(_pack([w[k] for k in SHARDED_SMALL]), "allgather_small_params")
    per_chip = [_unpack(small_all[2 * k], [w[n].shape for n in SHARDED_SMALL]) for k in range(N_CHIPS)]
    fs = {n: jnp.concatenate([per_chip[k][i] for k in range(N_CHIPS)], axis=-1) for i, n in enumerate(SHARDED_SMALL)}

    nh = b_ws.shape[1]
    tril = jnp.tril(jnp.ones((CHUNK, CHUNK), F32))
    wm = (b_ws[0] * tril).astype(BF16)
    wmt = jnp.swapaxes(wm, 1, 2)
    bs_exp = jnp.repeat(jnp.transpose(b_bs[0]), CHUNK, axis=1)
    ng = c_w_grp.shape[1]
    wgrp = full["c_w_grp"]

    xh, g, b = x[0], jnp.ones((1, d), F32), jnp.zeros((1, d), F32)
    saved = []
    for i in range(depth):
        kind, j = i % 3, i // 3
        rec = dict(xin=xh, gin=g, bin=b)
        if kind == 0:
            xh1, rstd1, p, chat, rstdc = _conv_fwd(
                xh, g, b, full["a_w_in"], j, fs["a_dw"], fs["a_dw_b"][j:j + 1], fs["a_ln_g"][j:j + 1],
                fs["a_ln_b"][j:j + 1], full["a_w_out"], alpha, f"conv_fwd_{i}")
            rec.update(p=p, chat=chat, rstdc=rstdc)
        elif kind == 1:
            xh1, rstd1, zp = _sgu_fwd(xh, g, b, full["b_w_in"], b_ln_g, b_ln_b, wm, bs_exp, full["b_w_out"],
                                      alpha, f"sgu_fwd_{i}")
            rec.update(zp=zp)
        else:
            xh1, rstd1, ys = _pool_fwd(xh, g, b, full["c_w_in"], wgrp, fs["c_scale"], full["c_w_out"],
                                       alpha, f"pool_fwd_{i}")
            rec.update(ys=ys)
        xh2, rstd2, hs = _ffn_fwd(xh1, ln1_g[i:i + 1], ln1_b[i:i + 1], full["f_w_up"], i, fs["f_dw"],
                                  full["f_w_down"], alpha, f"ffn_fwd_{i}")
        rec.update(xh1=xh1, rstd1=rstd1, xh2=xh2, rstd2=rstd2, hs=hs)
        saved.append(rec)
        xh, g, b = xh2, ln2_g[i:i + 1], ln2_b[i:i + 1]

    dxo, loss_part = _loss_head(xh, g, b, loss_target[0], "loss_head")
    loss = lax.psum(loss_part[0, 0], ("x", "y", "c"))

    gbuf = {k: jnp.zeros(full[k].shape, F32) for k in BIG if k != "c_w_grp"}
    gs = {k: [None] * w[k].shape[0] for k in ("a_dw", "a_dw_b", "a_ln_g", "a_ln_b", "f_dw", "ln1_g", "ln1_b", "ln2_g", "ln2_b")}
    for i in reversed(range(depth)):
        kind, j = i % 3, i // 3
        rec = saved[i]
        dr2, dh, gbuf["f_w_down"], gs["f_dw"][i], gs["ln2_g"][i], gs["ln2_b"][i] = _ffn_bwd1(
            dxo, rec["xh2"], rec["rstd2"], ln2_g[i:i + 1], rec["hs"], full["f_w_down"], i, fs["f_dw"],
            gbuf["f_w_down"], f"ffn_bwd1_{i}")
        dr1, gs["ln1_g"][i], gs["ln1_b"][i] = _bwd_in(
            dh, dr2, full["f_w_up"], i, alpha, f"ffn_bwd2_{i}", ln=(rec["xh1"], rec["rstd1"], ln1_g[i:i + 1]))
        gbuf["f_w_up"] = _mm_tn(rec["xh1"], ln1_g[i:i + 1], ln1_b[i:i + 1], dh, gbuf["f_w_up"], i, f"grad_w_up_{i}")
        if kind == 0:
            dp, gbuf["a_w_out"], gs["a_dw"][j], gs["a_dw_b"][j], gs["a_ln_g"][j], gs["a_ln_b"][j] = _conv_bwd1(
                dr1, rec["chat"], rec["rstdc"], rec["p"], full["a_w_out"], j, fs["a_dw"], fs["a_ln_g"][j:j + 1],
                fs["a_ln_b"][j:j + 1], gbuf["a_w_out"], f"conv_bwd1_{i}")
            win_name, lidx = "a_w_in", j
        elif kind == 1:
            dp, gbuf["b_w_out"], g_ws, g_bs_t, g_blg, g_blb = _sgu_bwd1(
                dr1, rec["zp"], full["b_w_out"], b_ln_g, b_ln_b, wm, wmt, bs_exp, gbuf["b_w_out"], f"sgu_bwd1_{i}")
            win_name, lidx = "b_w_in", 0
        else:
            dp, gbuf["c_w_out"], g_wgrp, g_cscale = _pool_bwd1(
                dr1, rec["ys"], full["c_w_out"], wgrp, fs["c_scale"], gbuf["c_w_out"], f"pool_bwd1_{i}")
            win_name, lidx = "c_w_in", 0
        dxo = _bwd_in(dp, dr1, full[win_name], lidx, alpha, f"mixer_bwd2_{i}")
        gbuf[win_name] = _mm_tn(rec["xin"], rec["gin"], rec["bin"], dp, gbuf[win_name], lidx, f"grad_w_in_{i}")
    grad_x = dxo[None]

    gfull = []
    for k, kind in zip(BIG, kinds):
        a = g_wgrp if k == "c_w_grp" else gbuf[k]
        gfull.append(a.reshape(a.shape[0], N_CHIPS, -1, a.shape[-1]) if kind == "row" else a)
    got = _rs_pair(gfull, kinds)
    pair_sum = [_pair_sum(a, t, kind, core1, "rs_pair_sum_" + k) for k, kind, a, t in zip(BIG, kinds, gfull, got)]
    from_chips = _rs_chips(pair_sum, kinds)
    half_sum = [_chip_sum(t, rb, kind, chip1, "rs_chip_sum_" + k)
                for k, kind, t, rb in zip(BIG, kinds, pair_sum, from_chips)]
    other_half = _rs_join(half_sum)
    grads, delta, new_m, new_v = {}, {}, {}, {}
    for k, own, oth in zip(BIG, half_sum, other_half):
        outs = _adamw_big(_as3d(w[k]), _as3d(mom[k]), _as3d(var[k]), own, oth, core1, "adamw_" + k)
        grads[k], delta[k], new_m[k], new_v[k] = [o.reshape(w[k].shape) for o in outs]

    small_full = {
        "a_dw": jnp.stack(gs["a_dw"]), "a_dw_b": jnp.concatenate(gs["a_dw_b"]), "a_ln_g": jnp.concatenate(gs["a_ln_g"]),
        "a_ln_b": jnp.concatenate(gs["a_ln_b"]), "c_scale": g_cscale, "f_dw": jnp.stack(gs["f_dw"]),
        "b_ln_g": g_blg, "b_ln_b": g_blb, "b_ws": g_ws[None], "b_bs": jnp.transpose(g_bs_t)[None],
        "ln1_g": jnp.concatenate(gs["ln1_g"]), "ln1_b": jnp.concatenate(gs["ln1_b"]),
        "ln2_g": jnp.concatenate(gs["ln2_g"]), "ln2_b": jnp.concatenate(gs["ln2_b"]),
    }
    small_names = SHARDED_SMALL + REPLICATED
    small_shapes = [small_full[n].shape for n in small_names]
    gathered_small = _allgather_small(_pack([small_full[n] for n in small_names]), "allgather_small_grads")
    summed = _unpack(_sum_lead(gathered_small, "small_grad_sum"), small_shapes)
    for n, a in zip(small_names, summed):
        if n in SHARDED_SMALL:
            cs = w[n].shape[-1]
            a = lax.dynamic_slice_in_dim(a, chip * cs, cs, axis=a.ndim - 1)
        grads[n] = a

    shapes = [w[n].shape for n in small_names]
    ds_, ms_, vs_ = _adamw(_pack([w[n] for n in small_names]), _pack([grads[n] for n in small_names]),
                           _pack([mom[n] for n in small_names]), _pack([var[n] for n in small_names]), "adamw_small")
    for n, a, bb, cc in zip(small_names, _unpack(ds_, shapes), _unpack(ms_, shapes), _unpack(vs_, shapes)):
        delta[n], new_m[n], new_v[n] = a, bb, cc

    return (loss, grad_x, *[grads[n] for n in WEIGHTS], *[delta[n] for n in WEIGHTS],
            *[new_m[n] for n in WEIGHTS], *[new_v[n] for n in WEIGHTS])
```

```python
import math

import jax
import jax.numpy as jnp
from jax import lax
from jax.experimental import pallas as pl
from jax.experimental.pallas import tpu as pltpu

F32 = jnp.float32
BF16 = jnp.bfloat16

LN_EPS = 1e-5
POOL_WINDOWS = (2, 4, 8, 16)
CHUNK = 128
ADAM_LR = 0.001
ADAM_B1 = 0.9
ADAM_B2 = 0.999
ADAM_EPS = 1e-08
ADAM_WD = 0.01
ADAM_STEP = 10

LANES = 128
SUBLANES_BF16 = 16
N_CHIPS = 4
N_DEV = 8
VMEM_LIMIT = 60 * 1024 * 1024

TM_FFN = 512
TM_FFN_BWD1 = 256
TM_CONV = 256
TM_SGU = 512
TM_POOL = 512
TM_BWD_IN = 512
TS_MM_TN = 1024
CW_FFN = 256
CW_FFN_BWD2 = 512
CONV_HALO = 32
CONV_ROW_BLOCK = 64
POOL_HALO = 16
FFN_HALO = 16

MESH = pl.DeviceIdType.MESH


def _cparams(n_grid=1, parallel=False):
    sem = ("parallel" if parallel else "arbitrary",) * n_grid
    return pltpu.CompilerParams(dimension_semantics=sem, vmem_limit_bytes=VMEM_LIMIT)


def _resident(block, imap):
    return pl.BlockSpec(block, imap, pipeline_mode=pl.Buffered(1))


def _wspec(w, l):
    _, r, c = w.shape
    return _resident((None, r, c), lambda *_: (l, 0, 0))


def _rowspec(d):
    return pl.BlockSpec((1, d), lambda *_: (0, 0))


def _dot(a, b):
    return jnp.dot(a, b, preferred_element_type=F32)


def _dot_nt(a, b):
    return lax.dot_general(a, b, (((1,), (1,)), ((), ())), preferred_element_type=F32)


def _dot_tn(a, b):
    return lax.dot_general(a, b, (((0,), (0,)), ((), ())), preferred_element_type=F32)


def _sigmoid(x):
    return jax.nn.sigmoid(x)


def _ln_stats(r):
    mu = jnp.mean(r, axis=1, keepdims=True)
    xc = r - mu
    var = jnp.mean(xc * xc, axis=1, keepdims=True)
    rstd = lax.rsqrt(var + LN_EPS)
    return xc * rstd, rstd


def _ln_bwd(dy, xhat, rstd, g):
    dxh = dy * g
    m1 = jnp.mean(dxh, axis=1, keepdims=True)
    m2 = jnp.mean(dxh * xhat, axis=1, keepdims=True)
    return rstd * (dxh - m1 - xhat * m2)


def _colsum(v):
    return jnp.sum(v, axis=0, keepdims=True)


def _gelu(z):
    return 0.5 * z * (1.0 + lax.erf(z * (1.0 / math.sqrt(2.0))))


def _gelu_grad(z):
    cdf = 0.5 * (1.0 + lax.erf(z * (1.0 / math.sqrt(2.0))))
    pdf = jnp.exp(-0.5 * z * z) * (1.0 / math.sqrt(2.0 * math.pi))
    return cdf + z * pdf


def _shift_down(v, k, prev_rows):
    rolled = pltpu.roll(v, k, 0)
    head = rolled[0:8]
    rows = lax.broadcasted_iota(jnp.int32, head.shape, 0)
    for r in range(k):
        head = jnp.where(rows == r, prev_rows[k - 1 - r], head)
    return jnp.concatenate([head, rolled[8:]], axis=0)


def _shift_up(v, k, next_rows):
    tm = v.shape[0]
    rolled = pltpu.roll(v, tm - k, 0)
    tail = rolled[tm - 8:tm]
    rows = lax.broadcasted_iota(jnp.int32, tail.shape, 0)
    for r in range(k):
        tail = jnp.where(rows == 8 - k + r, next_rows[r], tail)
    return jnp.concatenate([rolled[0:tm - 8], tail], axis=0)


def _fill_shifted(base_scr, sh_scr):
    nrows = sh_scr.shape[1]
    for r in range(1, 8):
        sh_scr[r - 1, :, :] = base_scr[pl.ds(r, nrows), :]


def _tap(base_scr, sh_scr, off, r0, nrows, cols):
    q, r = divmod(off, 8)
    if r == 0:
        return base_scr[pl.ds(r0 + 8 * q, nrows), cols]
    return sh_scr[r - 1, pl.ds(r0 + 8 * q, nrows), cols]


def _pick_rows(r, c, itemsize, cap_bytes):
    best = None
    for t in range(16, r + 1, 16):
        if r % t == 0 and t * c * itemsize <= cap_bytes:
            best = t
    return best if best is not None else r


def _ffn_conv_cols(h, dw_ref, c0, cw, prev1, prev2):
    kw = dw_ref.shape[0]
    h1 = _shift_down(h, 1, [prev1])
    h2 = _shift_down(h, 2, [prev1, prev2])
    hc = dw_ref[kw - 1:kw, c0:c0 + cw] * h + dw_ref[kw - 2:kw - 1, c0:c0 + cw] * h1 + dw_ref[kw - 3:kw - 2, c0:c0 + cw] * h2
    return hc, h1, h2


def _ffn_fwd(xh1, g1, b1, wup, l, fdw, wdn, alpha, name):
    s, d = xh1.shape
    f2 = wup.shape[2]
    f = f2 // 2
    tm = min(TM_FFN, s)
    cw = min(CW_FFN, f)
    n, nck = s // tm, f // cw
    assert fdw.shape[1] == 3 and s % tm == 0 and f % cw == 0

    def body(xh_ref, g_ref, b_ref, wup_ref, dw_ref, wdn_ref, xo_ref, rs_ref, hs_ref, hcs_ref, carry):
        @pl.when(pl.program_id(0) == 0)
        def _():
            carry[...] = jnp.zeros_like(carry)

        x1 = xh_ref[...] * g_ref[...] + b_ref[...]
        xb = x1.astype(BF16)
        o = jnp.zeros((tm, d), F32)

        def up_proj(j):
            return [_dot(xb, wup_ref[:, half * f + j * cw:half * f + (j + 1) * cw]) for half in range(2)]

        ahead = up_proj(0)
        for j in range(nck):
            hh = ahead
            if j + 1 < nck:
                ahead = up_proj(j + 1)
            parts = []
            for half in range(2):
                c0 = half * f + j * cw
                h = hh[half]
                hs_ref[:, c0:c0 + cw] = h.astype(BF16)
                hc, _, _ = _ffn_conv_cols(h, dw_ref, c0, cw, carry[7:8, c0:c0 + cw], carry[6:7, c0:c0 + cw])
                carry[:, c0:c0 + cw] = h[tm - 8:tm, :]
                hcs_ref[:, c0:c0 + cw] = hc.astype(BF16)
                parts.append(hc)
            gg, vv = parts
            a = (gg * _sigmoid(gg) * vv).astype(BF16)
            o = o + _dot(a, wdn_ref[j * cw:(j + 1) * cw, :])
        xhat, rstd = _ln_stats(alpha * x1 + o)
        xo_ref[...] = xhat
        rs_ref[...] = rstd

    tile = pl.BlockSpec((tm, d), lambda i: (i, 0))
    return pl.pallas_call(
        body, grid=(n,),
        in_specs=[tile, _rowspec(d), _rowspec(d), _wspec(wup, l),
                  pl.BlockSpec((None, 3, f2), lambda i: (l, 0, 0)), _wspec(wdn, l)],
        out_specs=[tile, pl.BlockSpec((tm, 1), lambda i: (i, 0)), pl.BlockSpec((tm, f2), lambda i: (i, 0)),
                   pl.BlockSpec((tm, f2), lambda i: (i, 0))],
        out_shape=[jax.ShapeDtypeStruct((s, d), F32), jax.ShapeDtypeStruct((s, 1), F32),
                   jax.ShapeDtypeStruct((s, f2), BF16), jax.ShapeDtypeStruct((s, f2), BF16)],
        scratch_shapes=[pltpu.VMEM((8, f2), F32)],
        compiler_params=_cparams(), name=name,
    )(xh1, g1, b1, wup, fdw, wdn)


def _ffn_bwd1(dx2, xh2, rstd2, g2, hs, hcs, wdn, l, gwdn_buf, name):
    s, d = dx2.shape
    f2 = hs.shape[1]
    f = f2 // 2
    tm = min(TM_FFN_BWD1, s)
    cw = min(CW_FFN, f)
    n, nck = s // tm, f // cw
    hb = FFN_HALO

    def body(dx_ref, xh_ref, rs_ref, g_ref, hs_ref, halo_ref, hcs_ref, wdn_ref, buf_ref,
             dr_ref, dhc_ref, gwdn_ref, gdw_ref, gg_ref, gb_ref):
        i = pl.program_id(0)

        @pl.when(i == 0)
        def _():
            gwdn_ref[...] = jnp.zeros_like(gwdn_ref)
            gdw_ref[...] = jnp.zeros_like(gdw_ref)
            gg_ref[...] = jnp.zeros_like(gg_ref)
            gb_ref[...] = jnp.zeros_like(gb_ref)

        dx = dx_ref[...]
        xh = xh_ref[...]
        gg_ref[...] += _colsum(dx * xh)
        gb_ref[...] += _colsum(dx)
        dr = _ln_bwd(dx, xh, rs_ref[...], g_ref[...])
        dr_ref[...] = dr
        dob = dr.astype(BF16)
        has_prev = i > 0

        def d_act(j):
            return _dot_nt(dob, wdn_ref[j * cw:(j + 1) * cw, :])

        da_ahead = d_act(0)
        for j in range(nck):
            da = da_ahead
            if j + 1 < nck:
                da_ahead = d_act(j + 1)
            gt = hcs_ref[:, j * cw:(j + 1) * cw].astype(F32)
            vv = hcs_ref[:, f + j * cw:f + (j + 1) * cw].astype(F32)
            sg = _sigmoid(gt)
            sl = gt * sg
            a = (sl * vv).astype(BF16)
            gwdn_ref[j * cw:(j + 1) * cw, :] += _dot_tn(a, dob)
            dhc = (da * vv * (sg * (1.0 + gt * (1.0 - sg))), da * sl)
            for half in range(2):
                c0 = half * f + j * cw
                dc = dhc[half]
                dhc_ref[:, c0:c0 + cw] = dc.astype(BF16)
                h = hs_ref[:, c0:c0 + cw].astype(F32)
                hal = jnp.where(has_prev, halo_ref[:, c0:c0 + cw].astype(F32), 0.0)
                prev = [hal[hb - 1:hb], hal[hb - 2:hb - 1]]
                gdw_ref[2:3, c0:c0 + cw] += _colsum(dc * h)
                gdw_ref[1:2, c0:c0 + cw] += _colsum(dc * _shift_down(h, 1, prev[:1]))
                gdw_ref[0:1, c0:c0 + cw] += _colsum(dc * _shift_down(h, 2, prev))

    tile = pl.BlockSpec((tm, d), lambda i: (i, 0))
    wide = pl.BlockSpec((tm, f2), lambda i: (i, 0))
    halo_blocks = tm // hb
    nl = gwdn_buf.shape[0]
    outs = pl.pallas_call(
        body, grid=(n,),
        in_specs=[tile, tile, pl.BlockSpec((tm, 1), lambda i: (i, 0)), _rowspec(d), wide,
                  pl.BlockSpec((hb, f2), lambda i: (jnp.maximum(i * halo_blocks - 1, 0), 0)), wide,
                  _wspec(wdn, l), pl.BlockSpec(memory_space=pl.ANY)],
        out_specs=[tile, wide, pl.BlockSpec((None, f, d), lambda i: (l, 0, 0)),
                   pl.BlockSpec((3, f2), lambda i: (0, 0)), _rowspec(d), _rowspec(d)],
        out_shape=[jax.ShapeDtypeStruct((s, d), F32), jax.ShapeDtypeStruct((s, f2), BF16),
                   jax.ShapeDtypeStruct((nl, f, d), F32), jax.ShapeDtypeStruct((3, f2), F32),
                   jax.ShapeDtypeStruct((1, d), F32), jax.ShapeDtypeStruct((1, d), F32)],
        input_output_aliases={8: 2},
        compiler_params=_cparams(), name=name,
    )(dx2, xh2, rstd2, g2, hs, hs, hcs, wdn, gwdn_buf)
    return outs


def _bwd_in(dp, dres, w, l, alpha, name):
    s, d = dres.shape
    nn = dp.shape[1]
    tm = min(TM_BWD_IN, s)
    n = s // tm
    tile = pl.BlockSpec((tm, d), lambda i: (i, 0))

    def body(dp_ref, dres_ref, w_ref, o_ref):
        o_ref[...] = alpha * dres_ref[...] + _dot_nt(dp_ref[...], w_ref[...])

    return pl.pallas_call(
        body, grid=(n,),
        in_specs=[pl.BlockSpec((tm, nn), lambda i: (i, 0)), tile, _wspec(w, l)],
        out_specs=tile, out_shape=jax.ShapeDtypeStruct((s, d), F32),
        compiler_params=_cparams(parallel=True), name=name,
    )(dp, dres, w)


def _ffn_bwd2(dhc, dres, wup, l, fdw, xh, rstd, g, alpha, name):
    s, d = dres.shape
    f2 = dhc.shape[1]
    tm = min(TM_BWD_IN, s)
    n = s // tm
    hb = FFN_HALO
    cw = min(CW_FFN_BWD2, f2)
    nck = f2 // cw
    halo_blocks = tm // hb
    assert f2 % cw == 0 and fdw.shape[1] == 3

    def body(dhc_ref, halo_ref, dres_ref, w_ref, dw_ref, xh_ref, rs_ref, g_ref, o_ref, dh_ref, gg_ref, gb_ref):
        i = pl.program_id(0)

        @pl.when(i == 0)
        def _():
            gg_ref[...] = jnp.zeros_like(gg_ref)
            gb_ref[...] = jnp.zeros_like(gb_ref)

        has_next = i < n - 1
        dx = alpha * dres_ref[...]
        for j in range(nck):
            c0 = j * cw
            dc = dhc_ref[:, c0:c0 + cw].astype(F32)
            hal = jnp.where(has_next, halo_ref[:, c0:c0 + cw].astype(F32), 0.0)
            nxt = [hal[0:1], hal[1:2]]
            dh = (dw_ref[2:3, c0:c0 + cw] * dc + dw_ref[1:2, c0:c0 + cw] * _shift_up(dc, 1, nxt[:1])
                  + dw_ref[0:1, c0:c0 + cw] * _shift_up(dc, 2, nxt)).astype(BF16)
            dh_ref[:, c0:c0 + cw] = dh
            dx = dx + _dot_nt(dh, w_ref[:, c0:c0 + cw])
        xhv = xh_ref[...]
        gg_ref[...] += _colsum(dx * xhv)
        gb_ref[...] += _colsum(dx)
        o_ref[...] = _ln_bwd(dx, xhv, rs_ref[...], g_ref[...])

    tile = pl.BlockSpec((tm, d), lambda i: (i, 0))
    wide = pl.BlockSpec((tm, f2), lambda i: (i, 0))
    return pl.pallas_call(
        body, grid=(n,),
        in_specs=[wide, pl.BlockSpec((hb, f2), lambda i: (jnp.minimum((i + 1) * halo_blocks, s // hb - 1), 0)),
                  tile, _wspec(wup, l), pl.BlockSpec((None, 3, f2), lambda i: (l, 0, 0)), tile,
                  pl.BlockSpec((tm, 1), lambda i: (i, 0)), _rowspec(d)],
        out_specs=[tile, wide, _rowspec(d), _rowspec(d)],
        out_shape=[jax.ShapeDtypeStruct((s, d), F32), jax.ShapeDtypeStruct((s, f2), BF16),
                   jax.ShapeDtypeStruct((1, d), F32), jax.ShapeDtypeStruct((1, d), F32)],
        compiler_params=_cparams(), name=name,
    )(dhc, dhc, dres, wup, fdw, xh, rstd, g)


def _mm_tn(a, ga, ba, bm, buf, l, name):
    s, k = a.shape
    nn = bm.shape[1]
    ts = min(TS_MM_TN, s)
    tn = nn // N_CHIPS if nn > 1024 else nn
    nj, ns = nn // tn, s // ts

    def body(a_ref, g_ref, b_ref, bm_ref, buf_ref, o_ref):
        @pl.when(pl.program_id(1) == 0)
        def _():
            o_ref[...] = jnp.zeros_like(o_ref)

        ab = (a_ref[...] * g_ref[...] + b_ref[...]).astype(BF16)
        o_ref[...] += _dot_tn(ab, bm_ref[...])

    return pl.pallas_call(
        body, grid=(nj, ns),
        in_specs=[pl.BlockSpec((ts, k), lambda j, t: (t, 0)), _rowspec(k), _rowspec(k),
                  pl.BlockSpec((ts, tn), lambda j, t: (t, j)), pl.BlockSpec(memory_space=pl.ANY)],
        out_specs=pl.BlockSpec((None, k, tn), lambda j, t: (l, 0, j)),
        out_shape=jax.ShapeDtypeStruct(buf.shape, F32),
        input_output_aliases={4: 0},
        compiler_params=_cparams(2), name=name,
    )(a, ga, ba, bm, buf)


def _conv_fwd(xin, gin, bin_, win, l, adw, adwb, lng, lnb, wout, alpha, name):
    s, d = xin.shape
    kw = adw.shape[1]
    hb = CONV_HALO
    tm = min(TM_CONV, s)
    n = s // tm
    assert kw - 1 <= hb <= tm

    def body(x_ref, g_ref, b_ref, win_ref, dw_ref, dwb_ref, lng_ref, lnb_ref, wout_ref,
             xo_ref, rs_ref, p_ref, chat_ref, rsc_ref, u_scr, u8_scr):
        @pl.when(pl.program_id(0) == 0)
        def _():
            u_scr[0:hb, :] = jnp.zeros((hb, d), F32)

        x = x_ref[...] * g_ref[...] + b_ref[...]
        pm = _dot(x.astype(BF16), win_ref[...])
        p_ref[...] = pm.astype(BF16)
        u = pm[:, :d] * _sigmoid(pm[:, d:])
        u_scr[hb:hb + tm, :] = u
        _fill_shifted(u_scr, u8_scr)
        acc = dwb_ref[...] + dw_ref[kw - 1:kw, :] * u
        for k in range(kw - 1):
            acc = acc + dw_ref[k:k + 1, :] * _tap(u_scr, u8_scr, hb - (kw - 1) + k, 0, tm, slice(None))
        u_scr[0:hb, :] = u_scr[tm:tm + hb, :]
        chat, rstdc = _ln_stats(acc)
        chat_ref[...] = chat.astype(BF16)
        rsc_ref[...] = rstdc
        nv = chat * lng_ref[...] + lnb_ref[...]
        sv = (nv * _sigmoid(nv)).astype(BF16)
        xhat, rstd = _ln_stats(alpha * x + _dot(sv, wout_ref[...]))
        xo_ref[...] = xhat
        rs_ref[...] = rstd

    tile = pl.BlockSpec((tm, d), lambda i: (i, 0))
    col = pl.BlockSpec((tm, 1), lambda i: (i, 0))
    return pl.pallas_call(
        body, grid=(n,),
        in_specs=[tile, _rowspec(d), _rowspec(d), _wspec(win, l),
                  pl.BlockSpec((None, kw, d), lambda i: (l, 0, 0)), _rowspec(d), _rowspec(d), _rowspec(d),
                  _wspec(wout, l)],
        out_specs=[tile, col, pl.BlockSpec((tm, 2 * d), lambda i: (i, 0)), tile, col],
        out_shape=[jax.ShapeDtypeStruct((s, d), F32), jax.ShapeDtypeStruct((s, 1), F32),
                   jax.ShapeDtypeStruct((s, 2 * d), BF16), jax.ShapeDtypeStruct((s, d), BF16),
                   jax.ShapeDtypeStruct((s, 1), F32)],
        scratch_shapes=[pltpu.VMEM((tm + hb, d), F32), pltpu.VMEM((7, tm + hb - 8, d), F32)],
        compiler_params=_cparams(), name=name,
    )(xin, gin, bin_, win, adw, adwb, lng, lnb, wout)


def _conv_bwd1(dr1, chat, rstdc, p, wout, l, adw, lng, lnb, gwout_buf, name):
    s, d = dr1.shape
    kw = adw.shape[1]
    hb = CONV_HALO
    tm = min(TM_CONV, s)
    n = s // tm
    halo_blocks = tm // hb
    rbl = CONV_ROW_BLOCK

    def body(dr_ref, chat_ref, rsc_ref, p_ref, halo_ref, wout_ref, dw_ref, lng_ref, lnb_ref, buf_ref,
             dp_ref, gwout_ref, gdw_ref, gdwb_ref, glng_ref, glnb_ref, u_scr, dc_scr, u8_scr, dc8_scr):
        i = pl.program_id(0)
        t = n - 1 - i

        @pl.when(i == 0)
        def _():
            dc_scr[tm:tm + hb, :] = jnp.zeros((hb, d), F32)
            gwout_ref[...] = jnp.zeros_like(gwout_ref)
            gdw_ref[...] = jnp.zeros_like(gdw_ref)
            gdwb_ref[...] = jnp.zeros_like(gdwb_ref)
            glng_ref[...] = jnp.zeros_like(glng_ref)
            glnb_ref[...] = jnp.zeros_like(glnb_ref)

        dob = dr_ref[...].astype(BF16)
        chat = chat_ref[...].astype(F32)
        lng = lng_ref[...]
        nv = chat * lng + lnb_ref[...]
        sgn = _sigmoid(nv)
        gwout_ref[...] += _dot_tn((nv * sgn).astype(BF16), dob)
        dn = _dot_nt(dob, wout_ref[...]) * (sgn * (1.0 + nv * (1.0 - sgn)))
        glng_ref[...] += _colsum(dn * chat)
        glnb_ref[...] += _colsum(dn)
        dc = _ln_bwd(dn, chat, rsc_ref[...], lng)
        gdwb_ref[...] += _colsum(dc)

        pm = p_ref[...].astype(F32)
        a = pm[:, :d]
        sg = _sigmoid(pm[:, d:])
        ph = halo_ref[...].astype(F32)
        u_scr[0:hb, :] = jnp.where(t > 0, ph[:, :d] * _sigmoid(ph[:, d:]), 0.0)
        u_scr[hb:hb + tm, :] = a * sg
        dc_scr[0:tm, :] = dc
        _fill_shifted(u_scr, u8_scr)
        _fill_shifted(dc_scr, dc8_scr)
        du = dw_ref[kw - 1:kw, :] * dc
        for k in range(kw - 1):
            du = du + dw_ref[k:k + 1, :] * _tap(dc_scr, dc8_scr, kw - 1 - k, 0, tm, slice(None))
        for cb in range(d // LANES):
            cols = pl.ds(cb * LANES, LANES)

            def rows_step(rb, accs, cols=cols):
                r0 = pl.multiple_of(rb * rbl, rbl)
                dcb = dc_scr[pl.ds(r0, rbl), cols]
                out = []
                for k in range(kw):
                    prod = dcb * _tap(u_scr, u8_scr, hb - (kw - 1) + k, r0, rbl, cols)
                    part = prod[0:8]
                    for g8 in range(1, rbl // 8):
                        part = part + prod[8 * g8:8 * g8 + 8]
                    out.append(accs[k] + part)
                return tuple(out)

            accs = lax.fori_loop(0, tm // rbl, rows_step, tuple(jnp.zeros((8, LANES), F32) for _ in range(kw)))
            for k in range(kw):
                gdw_ref[k:k + 1, cols] += _colsum(accs[k])
        dc_scr[tm:tm + hb, :] = dc[0:hb, :]
        dp_ref[:, :d] = (du * sg).astype(BF16)
        dp_ref[:, d:] = (du * a * sg * (1.0 - sg)).astype(BF16)

    tile = pl.BlockSpec((tm, d), lambda i: (n - 1 - i, 0))
    col = pl.BlockSpec((tm, 1), lambda i: (n - 1 - i, 0))
    nl = gwout_buf.shape[0]
    return pl.pallas_call(
        body, grid=(n,),
        in_specs=[tile, tile, col, pl.BlockSpec((tm, 2 * d), lambda i: (n - 1 - i, 0)),
                  pl.BlockSpec((hb, 2 * d), lambda i: (jnp.maximum((n - 1 - i) * halo_blocks - 1, 0), 0)),
                  _wspec(wout, l), pl.BlockSpec((None, kw, d), lambda i: (l, 0, 0)), _rowspec(d), _rowspec(d),
                  pl.BlockSpec(memory_space=pl.ANY)],
        out_specs=[pl.BlockSpec((tm, 2 * d), lambda i: (n - 1 - i, 0)),
                   pl.BlockSpec((None, d, d), lambda i: (l, 0, 0)),
                   pl.BlockSpec((kw, d), lambda i: (0, 0)), _rowspec(d), _rowspec(d), _rowspec(d)],
        out_shape=[jax.ShapeDtypeStruct((s, 2 * d), BF16), jax.ShapeDtypeStruct((nl, d, d), F32),
                   jax.ShapeDtypeStruct((kw, d), F32), jax.ShapeDtypeStruct((1, d), F32),
                   jax.ShapeDtypeStruct((1, d), F32), jax.ShapeDtypeStruct((1, d), F32)],
        scratch_shapes=[pltpu.VMEM((tm + hb, d), F32), pltpu.VMEM((tm + hb, d), F32),
                        pltpu.VMEM((7, tm + hb - 8, d), F32), pltpu.VMEM((7, tm + hb - 8, d), F32)],
        input_output_aliases={9: 1},
        compiler_params=_cparams(), name=name,
    )(dr1, chat, rstdc, p, p, wout, adw, lng, lnb, gwout_buf)


def _sgu_gate(vn, wm_ref, bs_ref, s_scr, tm, nh):
    for ch in range(tm // CHUNK):
        r0 = ch * CHUNK
        for h in range(nh):
            c0 = h * CHUNK
            s_scr[r0:r0 + CHUNK, c0:c0 + CHUNK] = (
                _dot(wm_ref[h], vn[r0:r0 + CHUNK, c0:c0 + CHUNK]) + bs_ref[:, c0:c0 + CHUNK])


def _sgu_fwd(xin, gin, bin_, win, lg, lb, wm, bs_exp, wout, alpha, name):
    s, d = xin.shape
    nh = wm.shape[0]
    tm = min(TM_SGU, s)
    n = s // tm
    assert tm % CHUNK == 0 and nh * CHUNK == d

    def body(x_ref, g_ref, b_ref, win_ref, lg_ref, lb_ref, wm_ref, bs_ref, wout_ref,
             xo_ref, rs_ref, zp_ref, s_scr):
        x = x_ref[...] * g_ref[...] + b_ref[...]
        zp = _dot(x.astype(BF16), win_ref[...])
        zp_ref[...] = zp.astype(BF16)
        z = _gelu(zp)
        vhat, _ = _ln_stats(z[:, d:])
        vn = (vhat * lg_ref[...] + lb_ref[...]).astype(BF16)
        _sgu_gate(vn, wm_ref, bs_ref, s_scr, tm, nh)
        q = (z[:, :d] * s_scr[...]).astype(BF16)
        xhat, rstd = _ln_stats(alpha * x + _dot(q, wout_ref[...]))
        xo_ref[...] = xhat
        rs_ref[...] = rstd

    tile = pl.BlockSpec((tm, d), lambda i: (i, 0))
    return pl.pallas_call(
        body, grid=(n,),
        in_specs=[tile, _rowspec(d), _rowspec(d), _wspec(win, 0), _rowspec(d), _rowspec(d),
                  _resident((nh, CHUNK, CHUNK), lambda i: (0, 0, 0)),
                  _resident((CHUNK, d), lambda i: (0, 0)), _wspec(wout, 0)],
        out_specs=[tile, pl.BlockSpec((tm, 1), lambda i: (i, 0)), pl.BlockSpec((tm, 2 * d), lambda i: (i, 0))],
        out_shape=[jax.ShapeDtypeStruct((s, d), F32), jax.ShapeDtypeStruct((s, 1), F32),
                   jax.ShapeDtypeStruct((s, 2 * d), BF16)],
        scratch_shapes=[pltpu.VMEM((tm, d), F32)],
        compiler_params=_cparams(parallel=True), name=name,
    )(xin, gin, bin_, win, lg, lb, wm, bs_exp, wout)


def _sgu_bwd1(dr1, zp, wout, lg, lb, wm, wmt, bs_exp, gwout_buf, name):
    s, d = dr1.shape
    nh = wm.shape[0]
    tm = min(TM_SGU, s)
    n = s // tm

    def body(dr_ref, zp_ref, wout_ref, lg_ref, lb_ref, wm_ref, wmt_ref, bs_ref, buf_ref,
             dzp_ref, gwout_ref, gws_ref, gbs_ref, glg_ref, glb_ref, s_scr, dvn_scr, bs_acc):
        i = pl.program_id(0)

        @pl.when(i == 0)
        def _():
            gwout_ref[...] = jnp.zeros_like(gwout_ref)
            gws_ref[...] = jnp.zeros_like(gws_ref)
            glg_ref[...] = jnp.zeros_like(glg_ref)
            glb_ref[...] = jnp.zeros_like(glb_ref)
            bs_acc[...] = jnp.zeros_like(bs_acc)

        dob = dr_ref[...].astype(BF16)
        zp = zp_ref[...].astype(F32)
        z = _gelu(zp)
        u = z[:, :d]
        lg = lg_ref[...]
        vhat, rstdv = _ln_stats(z[:, d:])
        vn = (vhat * lg + lb_ref[...]).astype(BF16)
        _sgu_gate(vn, wm_ref, bs_ref, s_scr, tm, nh)
        sv = s_scr[...]
        gwout_ref[...] += _dot_tn((u * sv).astype(BF16), dob)
        dq = _dot_nt(dob, wout_ref[...])
        ds = dq * u
        dsb = ds.astype(BF16)
        part = jnp.zeros((CHUNK, d), F32)
        for ch in range(tm // CHUNK):
            r0 = ch * CHUNK
            part = part + ds[r0:r0 + CHUNK, :]
            for h in range(nh):
                c0 = h * CHUNK
                blk = dsb[r0:r0 + CHUNK, c0:c0 + CHUNK]
                gws_ref[h] += _dot_nt(blk, vn[r0:r0 + CHUNK, c0:c0 + CHUNK])
                dvn_scr[r0:r0 + CHUNK, c0:c0 + CHUNK] = _dot(wmt_ref[h], blk)
        bs_acc[...] += part
        dvn = dvn_scr[...]
        glg_ref[...] += _colsum(dvn * vhat)
        glb_ref[...] += _colsum(dvn)
        dv = _ln_bwd(dvn, vhat, rstdv, lg)
        gp = _gelu_grad(zp)
        dzp_ref[:, :d] = (dq * sv * gp[:, :d]).astype(BF16)
        dzp_ref[:, d:] = (dv * gp[:, d:]).astype(BF16)

        @pl.when(i == n - 1)
        def _():
            rows = lax.broadcasted_iota(jnp.int32, (CHUNK, CHUNK), 0)
            cols = lax.broadcasted_iota(jnp.int32, (CHUNK, CHUNK), 1)
            tril = (cols <= rows).astype(F32)
            acc = bs_acc[...]
            for h in range(nh):
                gws_ref[h] = gws_ref[h] * tril
                gbs_ref[:, h:h + 1] = jnp.sum(acc[:, h * CHUNK:(h + 1) * CHUNK], axis=1, keepdims=True)

    tile = pl.BlockSpec((tm, d), lambda i: (i, 0))
    wide = pl.BlockSpec((tm, 2 * d), lambda i: (i, 0))
    hspec = _resident((nh, CHUNK, CHUNK), lambda i: (0, 0, 0))
    return pl.pallas_call(
        body, grid=(n,),
        in_specs=[tile, wide, _wspec(wout, 0), _rowspec(d), _rowspec(d), hspec, hspec,
                  _resident((CHUNK, d), lambda i: (0, 0)), pl.BlockSpec(memory_space=pl.ANY)],
        out_specs=[wide, pl.BlockSpec((None, d, d), lambda i: (0, 0, 0)),
                   pl.BlockSpec((nh, CHUNK, CHUNK), lambda i: (0, 0, 0)),
                   pl.BlockSpec((CHUNK, nh), lambda i: (0, 0)), _rowspec(d), _rowspec(d)],
        out_shape=[jax.ShapeDtypeStruct((s, 2 * d), BF16), jax.ShapeDtypeStruct(gwout_buf.shape, F32),
                   jax.ShapeDtypeStruct((nh, CHUNK, CHUNK), F32), jax.ShapeDtypeStruct((CHUNK, nh), F32),
                   jax.ShapeDtypeStruct((1, d), F32), jax.ShapeDtypeStruct((1, d), F32)],
        scratch_shapes=[pltpu.VMEM((tm, d), F32), pltpu.VMEM((tm, d), F32), pltpu.VMEM((CHUNK, d), F32)],
        input_output_aliases={8: 1},
        compiler_params=_cparams(), name=name,
    )(dr1, zp, wout, lg, lb, wm, wmt, bs_exp, gwout_buf)


def _pool_counts(t0, tm, w):
    pos = t0 + lax.broadcasted_iota(jnp.int32, (tm, 1), 0)
    return jnp.minimum(pos + 1, w).astype(F32)


def _pool_fwd(xin, gin, bin_, win, wg, scale, wout, alpha, name):
    s, d = xin.shape
    ng, dg = wg.shape[0], wg.shape[1]
    hb = POOL_HALO
    tm = min(TM_POOL, s)
    n = s // tm
    assert ng == len(POOL_WINDOWS) and ng * dg == d and max(POOL_WINDOWS) <= hb

    def body(x_ref, g_ref, b_ref, win_ref, wg_ref, sc_ref, wout_ref, xo_ref, rs_ref, ys_ref, y_scr, z_scr):
        i = pl.program_id(0)

        @pl.when(i == 0)
        def _():
            y_scr[0:hb, :] = jnp.zeros((hb, d), F32)

        x = x_ref[...] * g_ref[...] + b_ref[...]
        y = _dot(x.astype(BF16), win_ref[...])
        ys_ref[...] = y.astype(BF16)
        y_scr[hb:hb + tm, :] = y
        for g, w in enumerate(POOL_WINDOWS):
            c0 = g * dg
            acc = y[:, c0:c0 + dg]
            for dd in range(1, w):
                acc = acc + y_scr[pl.ds(hb - dd, tm), c0:c0 + dg]
            pg = acc / _pool_counts(i * tm, tm, w) - y[:, c0:c0 + dg]
            z_scr[:, c0:c0 + dg] = _dot(pg.astype(BF16), wg_ref[g])
        y_scr[0:hb, :] = y_scr[tm:tm + hb, :]
        zz = (z_scr[...] * sc_ref[...]).astype(BF16)
        xhat, rstd = _ln_stats(alpha * x + _dot(zz, wout_ref[...]))
        xo_ref[...] = xhat
        rs_ref[...] = rstd

    tile = pl.BlockSpec((tm, d), lambda i: (i, 0))
    return pl.pallas_call(
        body, grid=(n,),
        in_specs=[tile, _rowspec(d), _rowspec(d), _wspec(win, 0),
                  _resident((ng, dg, dg), lambda i: (0, 0, 0)), _rowspec(d), _wspec(wout, 0)],
        out_specs=[tile, pl.BlockSpec((tm, 1), lambda i: (i, 0)), tile],
        out_shape=[jax.ShapeDtypeStruct((s, d), F32), jax.ShapeDtypeStruct((s, 1), F32),
                   jax.ShapeDtypeStruct((s, d), BF16)],
        scratch_shapes=[pltpu.VMEM((tm + hb, d), F32), pltpu.VMEM((tm, d), F32)],
        compiler_params=_cparams(), name=name,
    )(xin, gin, bin_, win, wg, scale, wout)


def _pool_bwd1(dr1, ys, wout, wg, scale, gwout_buf, name):
    s, d = dr1.shape
    ng, dg = wg.shape[0], wg.shape[1]
    hb = POOL_HALO
    tm = min(TM_POOL, s)
    n = s // tm
    halo_blocks = tm // hb

    def body(dr_ref, ys_ref, halo_ref, wout_ref, wg_ref, sc_ref, buf_ref,
             dy_ref, gwout_ref, gwg_ref, gsc_ref, y_scr, e_scr, z_scr, dp_scr):
        i = pl.program_id(0)
        t = n - 1 - i

        @pl.when(i == 0)
        def _():
            e_scr[tm:tm + hb, :] = jnp.zeros((hb, d), F32)
            gwout_ref[...] = jnp.zeros_like(gwout_ref)
            gwg_ref[...] = jnp.zeros_like(gwg_ref)
            gsc_ref[...] = jnp.zeros_like(gsc_ref)

        dob = dr_ref[...].astype(BF16)
        y = ys_ref[...].astype(F32)
        y_scr[0:hb, :] = jnp.where(t > 0, halo_ref[...].astype(F32), 0.0)
        y_scr[hb:hb + tm, :] = y
        pgs = []
        for g, w in enumerate(POOL_WINDOWS):
            c0 = g * dg
            acc = y[:, c0:c0 + dg]
            for dd in range(1, w):
                acc = acc + y_scr[pl.ds(hb - dd, tm), c0:c0 + dg]
            pg = (acc / _pool_counts(t * tm, tm, w) - y[:, c0:c0 + dg]).astype(BF16)
            pgs.append(pg)
            z_scr[:, c0:c0 + dg] = _dot(pg, wg_ref[g])
        zpre = z_scr[...]
        sc = sc_ref[...]
        gwout_ref[...] += _dot_tn((zpre * sc).astype(BF16), dob)
        dz = _dot_nt(dob, wout_ref[...])
        gsc_ref[...] += _colsum(dz * zpre)
        dzpre = (dz * sc).astype(BF16)
        for g, w in enumerate(POOL_WINDOWS):
            c0 = g * dg
            dzg = dzpre[:, c0:c0 + dg]
            gwg_ref[g] += _dot_tn(pgs[g], dzg)
            dp = _dot_nt(dzg, wg_ref[g])
            dp_scr[:, c0:c0 + dg] = dp
            e_scr[0:tm, c0:c0 + dg] = dp / _pool_counts(t * tm, tm, w)
        for g, w in enumerate(POOL_WINDOWS):
            c0 = g * dg
            acc = e_scr[0:tm, c0:c0 + dg]
            for dd in range(1, w):
                acc = acc + e_scr[pl.ds(dd, tm), c0:c0 + dg]
            dy_ref[:, c0:c0 + dg] = (acc - dp_scr[:, c0:c0 + dg]).astype(BF16)
        e_scr[tm:tm + hb, :] = e_scr[0:hb, :]

    tile = pl.BlockSpec((tm, d), lambda i: (n - 1 - i, 0))
    return pl.pallas_call(
        body, grid=(n,),
        in_specs=[tile, tile,
                  pl.BlockSpec((hb, d), lambda i: (jnp.maximum((n - 1 - i) * halo_blocks - 1, 0), 0)),
                  _wspec(wout, 0), _resident((ng, dg, dg), lambda i: (0, 0, 0)), _rowspec(d),
                  pl.BlockSpec(memory_space=pl.ANY)],
        out_specs=[tile, pl.BlockSpec((None, d, d), lambda i: (0, 0, 0)),
                   pl.BlockSpec((ng, dg, dg), lambda i: (0, 0, 0)), _rowspec(d)],
        out_shape=[jax.ShapeDtypeStruct((s, d), BF16), jax.ShapeDtypeStruct(gwout_buf.shape, F32),
                   jax.ShapeDtypeStruct((ng, dg, dg), F32), jax.ShapeDtypeStruct((1, d), F32)],
        scratch_shapes=[pltpu.VMEM((tm + hb, d), F32), pltpu.VMEM((tm + hb, d), F32),
                        pltpu.VMEM((tm, d), F32), pltpu.VMEM((tm, d), F32)],
        input_output_aliases={6: 1},
        compiler_params=_cparams(), name=name,
    )(dr1, ys, ys, wout, wg, scale, gwout_buf)


def _loss_head(xh, g, b, target, name):
    s, d = xh.shape
    tm = min(512, s)
    n = s // tm

    def body(xh_ref, g_ref, b_ref, t_ref, dy_ref, loss_ref, acc):
        i = pl.program_id(0)

        @pl.when(i == 0)
        def _():
            acc[...] = jnp.zeros_like(acc)

        err = xh_ref[...] * g_ref[...] + b_ref[...] - t_ref[...]
        dy_ref[...] = err * (1.0 / d)
        acc[...] += _colsum(err * err)

        @pl.when(i == n - 1)
        def _():
            loss_ref[...] = (0.5 / d) * jnp.sum(acc[...], axis=1, keepdims=True)

    tile = pl.BlockSpec((tm, d), lambda i: (i, 0))
    return pl.pallas_call(
        body, grid=(n,),
        in_specs=[tile, _rowspec(d), _rowspec(d), tile],
        out_specs=[tile, pl.BlockSpec((1, 1), lambda i: (0, 0))],
        out_shape=[jax.ShapeDtypeStruct((s, d), F32), jax.ShapeDtypeStruct((1, 1), F32)],
        scratch_shapes=[pltpu.VMEM((1, d), F32)],
        compiler_params=_cparams(), name=name,
    )(xh, g, b, target)


def _elementwise(fn, ins, out_dtypes, name, lead=0):
    shape = ins[0].shape
    c = shape[-1]
    r = math.prod(shape[lead:-1])
    lead_shape = tuple(shape[:lead])
    out_shape = tuple(shape[lead:])
    tr = _pick_rows(r, c, 4, (1 << 20) // max(1, math.prod(lead_shape)))
    nlead = len(lead_shape)

    def body(*refs):
        vals = fn(*[ref[...] for ref in refs[:len(ins)]])
        for ref, v in zip(refs[len(ins):], vals):
            ref[...] = v.astype(ref.dtype)

    def spec(a_lead):
        if a_lead:
            return pl.BlockSpec(lead_shape + (tr, c), lambda i: (0,) * nlead + (i, 0))
        return pl.BlockSpec((tr, c), lambda i: (i, 0))

    args = [ins[0].reshape(lead_shape + (r, c))] + [a.reshape(r, c) for a in ins[1:]]
    outs = pl.pallas_call(
        body, grid=(r // tr,),
        in_specs=[spec(bool(lead))] + [spec(False)] * (len(ins) - 1),
        out_specs=[spec(False)] * len(out_dtypes),
        out_shape=[jax.ShapeDtypeStruct((r, c), dt) for dt in out_dtypes],
        compiler_params=_cparams(parallel=True), name=name,
    )(*args)
    return [o.reshape(out_shape) for o in outs]


def _prefetch_call(body, grid, in_specs, out_specs, out_shape, name):
    return pl.pallas_call(
        body,
        grid_spec=pltpu.PrefetchScalarGridSpec(num_scalar_prefetch=1, grid=grid, in_specs=in_specs, out_specs=out_specs),
        out_shape=out_shape,
        compiler_params=_cparams(len(grid), parallel=True), name=name)


def _cast_into_full(w3, kind, chip1, name):
    l, r, c = w3.shape
    tr = _pick_rows(r, c, 4, 1 << 20)

    def body(k_ref, w_ref, o_ref):
        o_ref[...] = w_ref[...].astype(BF16)

    if kind == "row":
        out_spec = pl.BlockSpec((None, None, tr, c), lambda a, j, k: (a, k[0], j, 0))
    else:
        out_spec = pl.BlockSpec((None, tr, c), lambda a, j, k: (a, j, k[0]))
    return _prefetch_call(
        body, (l, r // tr), [pl.BlockSpec((None, tr, c), lambda a, j, k: (a, j, 0))], out_spec,
        jax.ShapeDtypeStruct(_full_shape(kind, w3.shape), BF16), name)(chip1, w3)


def _pair_sum(g, got, kind, core1, name):
    if kind == "row":
        l, nc, sr, c = g.shape
        g5, got3 = g.reshape(l * nc, 2, sr // 2, c), got.reshape(l * nc, sr // 2, c)
    else:
        l, r, c = g.shape
        g5, got3 = g.reshape(l, 2, r // 2, c), got
    a, _, hr, c = g5.shape
    tr = _pick_rows(hr, c, 4, 1 << 20)

    def body(c_ref, g_ref, t_ref, o_ref):
        o_ref[...] = (g_ref[...] + t_ref[...]).astype(BF16)

    half = pl.BlockSpec((None, tr, c), lambda i, j, cc: (i, j, 0))
    out = _prefetch_call(
        body, (a, hr // tr), [pl.BlockSpec((None, None, tr, c), lambda i, j, cc: (i, cc[0], j, 0)), half], half,
        jax.ShapeDtypeStruct(got3.shape, BF16), name)(core1, g5, got3)
    return out.reshape(got.shape)


def _chip_sum(t, rb, kind, chip1, name):
    _, l, hr, sc = rb.shape
    tr = _pick_rows(hr, sc, 4, 1 << 19)

    def body(k_ref, t_ref, rb_ref, o_ref):
        acc = t_ref[...].astype(F32)
        for r in range(N_CHIPS - 1):
            acc = acc + rb_ref[r].astype(F32)
        o_ref[...] = acc

    if kind == "row":
        t_spec = pl.BlockSpec((None, None, tr, sc), lambda a, j, k: (a, k[0], j, 0))
    else:
        t_spec = pl.BlockSpec((None, tr, sc), lambda a, j, k: (a, j, k[0]))
    return _prefetch_call(
        body, (l, hr // tr),
        [t_spec, pl.BlockSpec((N_CHIPS - 1, None, tr, sc), lambda a, j, k: (0, a, j, 0))],
        pl.BlockSpec((None, tr, sc), lambda a, j, k: (a, j, 0)),
        jax.ShapeDtypeStruct((l, hr, sc), F32), name)(chip1, t, rb)


def _adamw_math(w_, g_, m_, v_):
    m2 = ADAM_B1 * m_ + (1.0 - ADAM_B1) * g_
    v2 = ADAM_B2 * v_ + (1.0 - ADAM_B2) * (g_ * g_)
    m_hat = m2 / (1.0 - ADAM_B1 ** ADAM_STEP)
    v_hat = v2 / (1.0 - ADAM_B2 ** ADAM_STEP)
    delta = -ADAM_LR * (m_hat / (jnp.sqrt(v_hat) + ADAM_EPS) + ADAM_WD * w_)
    return delta, m2, v2


def _adamw_big(w, m, v, own, other, core1, name):
    l, hr, c = own.shape
    view = lambda a: a.reshape(l, 2, hr, c)
    tr = _pick_rows(hr, c, 4, 1 << 20)

    def body(c_ref, w_ref, m_ref, v_ref, own_ref, oth_ref, g_ref, d_ref, m2_ref, v2_ref):
        g = jnp.where(pl.program_id(1) == c_ref[0], own_ref[...], oth_ref[...])
        g_ref[...] = g
        d_ref[...], m2_ref[...], v2_ref[...] = _adamw_math(w_ref[...], g, m_ref[...], v_ref[...])

    s4 = pl.BlockSpec((None, None, tr, c), lambda a, h, j, cc: (a, h, j, 0))
    s3 = pl.BlockSpec((None, tr, c), lambda a, h, j, cc: (a, j, 0))
    outs = _prefetch_call(
        body, (l, 2, hr // tr), [s4, s4, s4, s3, s3], [s4] * 4,
        [jax.ShapeDtypeStruct((l, 2, hr, c), F32)] * 4, name)(core1, view(w), view(m), view(v), own, other)
    return [o.reshape(w.shape) for o in outs]


def _sum_lead(stacked, name):
    def fn(v):
        acc = v[0]
        for k in range(1, v.shape[0]):
            acc = acc + v[k]
        return (acc,)

    return _elementwise(fn, [stacked], [F32], name, lead=1)[0]


def _adamw(w, g, m, v, name):
    return _elementwise(_adamw_math, [w, g, m, v], [F32, F32, F32], name)


ANY = pl.BlockSpec(memory_space=pl.ANY)


def _mesh_pos():
    return lax.axis_index("x"), lax.axis_index("y"), lax.axis_index("c")


def _chip_peers(x, y, c):
    out = []
    for r in (1, 2, 3):
        px = 1 - x if r & 2 else x
        py = 1 - y if r & 1 else y
        out.append((2 * px + py, (px, py, c)))
    return out


def _full_shape(kind, shard_shape):
    l, r, c = shard_shape
    return (l, N_CHIPS, r, c) if kind == "row" else (l, r, N_CHIPS * c)


def _full_piece(ref, kind, k, h, hr, sc):
    rows = pl.ds(pl.multiple_of(h * hr, SUBLANES_BF16), hr)
    if kind == "row":
        return ref.at[:, k, rows, :]
    return ref.at[:, rows, pl.ds(pl.multiple_of(k * sc, LANES), sc)]


def _remote(src, dst, ssem, rsem, dev):
    return pltpu.make_async_remote_copy(src_ref=src, dst_ref=dst, send_sem=ssem, recv_sem=rsem,
                                        device_id=dev, device_id_type=MESH)


DMA_CHUNK_BYTES = 1 << 20
DMA_MAX_CHUNKS = 32


def _chunk_views(src, dst):
    axis = len(src.shape) - 2
    rows = src.shape[axis]
    nbytes = math.prod(src.shape) * jnp.dtype(src.dtype).itemsize
    n = max(1, min(DMA_MAX_CHUNKS, nbytes // DMA_CHUNK_BYTES))
    while n > 1 and (rows % n or (rows // n) % SUBLANES_BF16):
        n -= 1
    cr = rows // n
    out = []
    for i in range(n):
        idx = (slice(None),) * axis + (pl.ds(i * cr, cr), slice(None))
        out.append((src.at[idx], dst.at[idx]))
    return out


def _start_remote(src, dst, ssem, rsem, dev):
    for s, t in _chunk_views(src, dst):
        _remote(s, t, ssem, rsem, dev).start()
    return _remote(src, dst, ssem, rsem, dev)


def _allgather_weights(fulls, kinds):
    nw = len(fulls)

    def dims(a, kind):
        return (a.shape[2] // 2, a.shape[3]) if kind == "row" else (a.shape[1] // 2, a.shape[2] // N_CHIPS)

    hrs = [dims(a, k)[0] for a, k in zip(fulls, kinds)]
    scs = [dims(a, k)[1] for a, k in zip(fulls, kinds)]

    def body(*refs):
        mine, fu = refs[:nw], refs[nw:2 * nw]
        send1, recv1, send2, recv2 = refs[2 * nw:]
        x, y, c = _mesh_pos()
        k_me = 2 * x + y
        sibling = (x, y, 1 - c)
        peers = _chip_peers(x, y, c)

        def piece(ref, w, k, h):
            return _full_piece(ref, kinds[w], k, h, hrs[w], scs[w])

        sends = []
        for w in range(nw):
            for r, (_, dev) in enumerate(peers):
                sends.append(_start_remote(piece(mine[w], w, k_me, c), piece(fu[w], w, k_me, c),
                                           send1.at[3 * w + r], recv1.at[3 * w + r], dev))
        for w in range(nw):
            for r, (kj, dev) in enumerate(peers):
                _remote(piece(mine[w], w, k_me, c), piece(fu[w], w, kj, c),
                        send1.at[3 * w + r], recv1.at[3 * w + r], dev).wait_recv()
                sends.append(_start_remote(piece(fu[w], w, kj, c), piece(fu[w], w, kj, c),
                                           send2.at[3 * w + r], recv2.at[3 * w + r], sibling))
        for w in range(nw):
            for r, (kj, _) in enumerate(peers):
                _remote(piece(fu[w], w, kj, 1 - c), piece(fu[w], w, kj, 1 - c),
                        send2.at[3 * w + r], recv2.at[3 * w + r], sibling).wait_recv()
        for cp in sends:
            cp.wait_send()

    return pl.pallas_call(
        body,
        in_specs=[ANY] * nw, out_specs=[ANY] * nw,
        out_shape=[jax.ShapeDtypeStruct(a.shape, a.dtype) for a in fulls],
        scratch_shapes=[pltpu.SemaphoreType.DMA((3 * nw,))] * 4,
        input_output_aliases={w: w for w in range(nw)},
        name="allgather_weights",
    )(*fulls)


def _rs_pair(fulls, kinds):
    nw = len(fulls)

    def half_all(ref, kind, h):
        if kind == "row":
            hr = ref.shape[2] // 2
            return ref.at[:, :, pl.ds(pl.multiple_of(h * hr, SUBLANES_BF16), hr), :]
        hr = ref.shape[1] // 2
        return ref.at[:, pl.ds(pl.multiple_of(h * hr, SUBLANES_BF16), hr), :]

    def half_shape(kind, shape):
        if kind == "row":
            return (shape[0], shape[1], shape[2] // 2, shape[3])
        return (shape[0], shape[1] // 2, shape[2])

    def body(*refs):
        g, got = refs[:nw], refs[nw:2 * nw]
        ssem, rsem = refs[2 * nw:]
        x, y, c = _mesh_pos()
        sibling = (x, y, 1 - c)
        cps = [_start_remote(half_all(g[w], kinds[w], 1 - c), got[w], ssem.at[w], rsem.at[w], sibling)
               for w in range(nw)]
        for cp in cps:
            cp.wait_recv()
        for cp in cps:
            cp.wait_send()

    shapes = [jax.ShapeDtypeStruct(half_shape(k, a.shape), a.dtype) for k, a in zip(kinds, fulls)]
    return pl.pallas_call(
        body, in_specs=[ANY] * nw, out_specs=[ANY] * nw, out_shape=shapes,
        scratch_shapes=[pltpu.SemaphoreType.DMA((nw,))] * 2, name="rs_pair",
    )(*fulls)


def _rs_chips(parts, kinds):
    nw = len(parts)

    def slot(ref, kind, k):
        if kind == "row":
            return ref.at[:, k]
        sc = ref.shape[2] // N_CHIPS
        return ref.at[:, :, pl.ds(pl.multiple_of(k * sc, LANES), sc)]

    def slot_shape(kind, shape):
        if kind == "row":
            return (shape[0], shape[2], shape[3])
        return (shape[0], shape[1], shape[2] // N_CHIPS)

    def body(*refs):
        t, rb = refs[:nw], refs[nw:2 * nw]
        ssem, rsem = refs[2 * nw:]
        x, y, c = _mesh_pos()
        cps = []
        for w in range(nw):
            for r, (kj, dev) in enumerate(_chip_peers(x, y, c)):
                cps.append(_start_remote(slot(t[w], kinds[w], kj), rb[w].at[r],
                                         ssem.at[3 * w + r], rsem.at[3 * w + r], dev))
        for cp in cps:
            cp.wait_recv()
        for cp in cps:
            cp.wait_send()

    shapes = [jax.ShapeDtypeStruct((N_CHIPS - 1,) + slot_shape(k, a.shape), a.dtype) for k, a in zip(kinds, parts)]
    return pl.pallas_call(
        body, in_specs=[ANY] * nw, out_specs=[ANY] * nw, out_shape=shapes,
        scratch_shapes=[pltpu.SemaphoreType.DMA((3 * nw,))] * 2, name="rs_chips",
    )(*parts)


def _rs_join(halves):
    nw = len(halves)

    def body(*refs):
        src, dst = refs[:nw], refs[nw:2 * nw]
        ssem, rsem = refs[2 * nw:]
        x, y, c = _mesh_pos()
        cps = [_start_remote(src[w], dst[w], ssem.at[w], rsem.at[w], (x, y, 1 - c)) for w in range(nw)]
        for cp in cps:
            cp.wait_recv()
        for cp in cps:
            cp.wait_send()

    return pl.pallas_call(
        body, in_specs=[ANY] * nw, out_specs=[ANY] * nw,
        out_shape=[jax.ShapeDtypeStruct(a.shape, a.dtype) for a in halves],
        scratch_shapes=[pltpu.SemaphoreType.DMA((nw,))] * 2, name="rs_join",
    )(*halves)


def _allgather_small(buf, name):
    def body(in_ref, out_ref, ssem, rsem, lsem):
        x, y, c = _mesh_pos()
        me = 4 * x + 2 * y + c
        lc = pltpu.make_async_copy(in_ref, out_ref.at[me], lsem)
        lc.start()
        cps, waits = [], []
        for r in range(1, N_DEV):
            px = 1 - x if r & 4 else x
            py = 1 - y if r & 2 else y
            pc = 1 - c if r & 1 else c
            cp = _remote(in_ref, out_ref.at[me], ssem.at[r - 1], rsem.at[r - 1], (px, py, pc))
            cp.start()
            cps.append(cp)
            waits.append(_remote(in_ref, out_ref.at[4 * px + 2 * py + pc], ssem.at[r - 1], rsem.at[r - 1], (px, py, pc)))
        for wt in waits:
            wt.wait_recv()
        for cp in cps:
            cp.wait_send()
        lc.wait()

    return pl.pallas_call(
        body, in_specs=[ANY], out_specs=ANY,
        out_shape=jax.ShapeDtypeStruct((N_DEV,) + buf.shape, buf.dtype),
        scratch_shapes=[pltpu.SemaphoreType.DMA((N_DEV - 1,))] * 2 + [pltpu.SemaphoreType.DMA],
        name=name,
    )(buf)


def _pack(arrs):
    flat = jnp.concatenate([a.reshape(-1).astype(F32) for a in arrs])
    rows = -(-flat.shape[0] // (LANES * 16)) * 16
    return jnp.pad(flat, (0, rows * LANES - flat.shape[0])).reshape(rows, LANES)


def _unpack(buf, shapes):
    flat = buf.reshape(-1)
    out, off = [], 0
    for shp in shapes:
        nel = math.prod(shp)
        out.append(flat[off:off + nel].reshape(shp))
        off += nel
    return out


BIG = ("a_w_in", "a_w_out", "b_w_in", "b_w_out", "c_w_in", "c_w_grp", "c_w_out", "f_w_up", "f_w_down")
BIG_KIND = {"a_w_in": "col", "a_w_out": "row", "b_w_in": "col", "b_w_out": "row", "c_w_in": "row",
            "c_w_grp": "row", "c_w_out": "row", "f_w_up": "col", "f_w_down": "row"}
SHARDED_SMALL = ("a_dw", "a_dw_b", "a_ln_g", "a_ln_b", "c_scale", "f_dw")
REPLICATED = ("b_ln_g", "b_ln_b", "b_ws", "b_bs", "ln1_g", "ln1_b", "ln2_g", "ln2_b")
WEIGHTS = ("a_w_in", "a_dw", "a_dw_b", "a_ln_g", "a_ln_b", "a_w_out", "b_w_in", "b_ln_g", "b_ln_b", "b_ws", "b_bs",
           "b_w_out", "c_w_in", "c_w_grp", "c_scale", "c_w_out", "f_w_up", "f_dw", "f_w_down",
           "ln1_g", "ln1_b", "ln2_g", "ln2_b")


def _as3d(a):
    return a.reshape((-1,) + a.shape[-2:])


def kernel(x, a_w_in, a_dw, a_dw_b, a_ln_g, a_ln_b, a_w_out, b_w_in, b_ln_g, b_ln_b, b_ws, b_bs, b_w_out, c_w_in, c_w_grp, c_scale, c_w_out, f_w_up, f_dw, f_w_down, ln1_g, ln1_b, ln2_g, ln2_b, loss_target, m_a_w_in, m_a_dw, m_a_dw_b, m_a_ln_g, m_a_ln_b, m_a_w_out, m_b_w_in, m_b_ln_g, m_b_ln_b, m_b_ws, m_b_bs, m_b_w_out, m_c_w_in, m_c_w_grp, m_c_scale, m_c_w_out, m_f_w_up, m_f_dw, m_f_w_down, m_ln1_g, m_ln1_b, m_ln2_g, m_ln2_b, v_a_w_in, v_a_dw, v_a_dw_b, v_a_ln_g, v_a_ln_b, v_a_w_out, v_b_w_in, v_b_ln_g, v_b_ln_b, v_b_ws, v_b_bs, v_b_w_out, v_c_w_in, v_c_w_grp, v_c_scale, v_c_w_out, v_f_w_up, v_f_dw, v_f_w_down, v_ln1_g, v_ln1_b, v_ln2_g, v_ln2_b):
    w = dict(a_w_in=a_w_in, a_dw=a_dw, a_dw_b=a_dw_b, a_ln_g=a_ln_g, a_ln_b=a_ln_b, a_w_out=a_w_out, b_w_in=b_w_in, b_ln_g=b_ln_g, b_ln_b=b_ln_b, b_ws=b_ws, b_bs=b_bs, b_w_out=b_w_out, c_w_in=c_w_in, c_w_grp=c_w_grp, c_scale=c_scale, c_w_out=c_w_out, f_w_up=f_w_up, f_dw=f_dw, f_w_down=f_w_down, ln1_g=ln1_g, ln1_b=ln1_b, ln2_g=ln2_g, ln2_b=ln2_b)
    mom = dict(a_w_in=m_a_w_in, a_dw=m_a_dw, a_dw_b=m_a_dw_b, a_ln_g=m_a_ln_g, a_ln_b=m_a_ln_b, a_w_out=m_a_w_out, b_w_in=m_b_w_in, b_ln_g=m_b_ln_g, b_ln_b=m_b_ln_b, b_ws=m_b_ws, b_bs=m_b_bs, b_w_out=m_b_w_out, c_w_in=m_c_w_in, c_w_grp=m_c_w_grp, c_scale=m_c_scale, c_w_out=m_c_w_out, f_w_up=m_f_w_up, f_dw=m_f_dw, f_w_down=m_f_w_down, ln1_g=m_ln1_g, ln1_b=m_ln1_b, ln2_g=m_ln2_g, ln2_b=m_ln2_b)
    var = dict(a_w_in=v_a_w_in, a_dw=v_a_dw, a_dw_b=v_a_dw_b, a_ln_g=v_a_ln_g, a_ln_b=v_a_ln_b, a_w_out=v_a_w_out, b_w_in=v_b_w_in, b_ln_g=v_b_ln_g, b_ln_b=v_b_ln_b, b_ws=v_b_ws, b_bs=v_b_bs, b_w_out=v_b_w_out, c_w_in=v_c_w_in, c_w_grp=v_c_w_grp, c_scale=v_c_scale, c_w_out=v_c_w_out, f_w_up=v_f_w_up, f_dw=v_f_dw, f_w_down=v_f_w_down, ln1_g=v_ln1_g, ln1_b=v_ln1_b, ln2_g=v_ln2_g, ln2_b=v_ln2_b)

    depth = ln1_g.shape[0]
    d = x.shape[-1]
    alpha = float((2 * depth) ** 0.25)
    chip = 2 * lax.axis_index("x") + lax.axis_index("y")
    chip1 = chip.astype(jnp.int32).reshape(1)
    core1 = lax.axis_index("c").astype(jnp.int32).reshape(1)

    kinds = [BIG_KIND[k] for k in BIG]
    mine = [_cast_into_full(_as3d(w[k]), kind, chip1, "cast_" + k) for k, kind in zip(BIG, kinds)]
    gathered = _allgather_weights(mine, kinds)
    full = {}
    for k, kind, arr in zip(BIG, kinds, gathered):
        full[k] = arr.reshape(arr.shape[0], -1, arr.shape[-1]) if kind == "row" else arr

    small_all = _allgather_small(_pack([w[k] for k in SHARDED_SMALL]), "allgather_small_params")
    per_chip = [_unpack(small_all[2 * k], [w[n].shape for n in SHARDED_SMALL]) for k in range(N_CHIPS)]
    fs = {n: jnp.concatenate([per_chip[k][i] for k in range(N_CHIPS)], axis=-1) for i, n in enumerate(SHARDED_SMALL)}

    nh = b_ws.shape[1]
    tril = jnp.tril(jnp.ones((CHUNK, CHUNK), F32))
    wm = (b_ws[0] * tril).astype(BF16)
    wmt = jnp.swapaxes(wm, 1, 2)
    bs_exp = jnp.repeat(jnp.transpose(b_bs[0]), CHUNK, axis=1)
    ng = c_w_grp.shape[1]
    wgrp = full["c_w_grp"]

    xh, g, b = x[0], jnp.ones((1, d), F32), jnp.zeros((1, d), F32)
    saved = []
    for i in range(depth):
        kind, j = i % 3, i // 3
        rec = dict(xin=xh, gin=g, bin=b)
        if kind == 0:
            xh1, rstd1, p, chat, rstdc = _conv_fwd(
                xh, g, b, full["a_w_in"], j, fs["a_dw"], fs["a_dw_b"][j:j + 1], fs["a_ln_g"][j:j + 1],
                fs["a_ln_b"][j:j + 1], full["a_w_out"], alpha, f"conv_fwd_{i}")
            rec.update(p=p, chat=chat, rstdc=rstdc)
        elif kind == 1:
            xh1, rstd1, zp = _sgu_fwd(xh, g, b, full["b_w_in"], b_ln_g, b_ln_b, wm, bs_exp, full["b_w_out"],
                                      alpha, f"sgu_fwd_{i}")
            rec.update(zp=zp)
        else:
            xh1, rstd1, ys = _pool_fwd(xh, g, b, full["c_w_in"], wgrp, fs["c_scale"], full["c_w_out"],
                                       alpha, f"pool_fwd_{i}")
            rec.update(ys=ys)
        xh2, rstd2, hs, hcs = _ffn_fwd(xh1, ln1_g[i:i + 1], ln1_b[i:i + 1], full["f_w_up"], i, fs["f_dw"],
                                       full["f_w_down"], alpha, f"ffn_fwd_{i}")
        rec.update(xh1=xh1, rstd1=rstd1, xh2=xh2, rstd2=rstd2, hs=hs, hcs=hcs)
        saved.append(rec)
        xh, g, b = xh2, ln2_g[i:i + 1], ln2_b[i:i + 1]

    dxo, loss_part = _loss_head(xh, g, b, loss_target[0], "loss_head")
    loss = lax.psum(loss_part[0, 0], ("x", "y", "c"))

    gbuf = {k: lax.empty(full[k].shape, F32) for k in BIG if k != "c_w_grp"}
    gs = {k: [None] * w[k].shape[0] for k in ("a_dw", "a_dw_b", "a_ln_g", "a_ln_b", "f_dw", "ln1_g", "ln1_b", "ln2_g", "ln2_b")}
    for i in reversed(range(depth)):
        kind, j = i % 3, i // 3
        rec = saved[i]
        dr2, dhc, gbuf["f_w_down"], gs["f_dw"][i], gs["ln2_g"][i], gs["ln2_b"][i] = _ffn_bwd1(
            dxo, rec["xh2"], rec["rstd2"], ln2_g[i:i + 1], rec["hs"], rec["hcs"], full["f_w_down"], i,
            gbuf["f_w_down"], f"ffn_bwd1_{i}")
        dr1, dh, gs["ln1_g"][i], gs["ln1_b"][i] = _ffn_bwd2(
            dhc, dr2, full["f_w_up"], i, fs["f_dw"], rec["xh1"], rec["rstd1"], ln1_g[i:i + 1], alpha, f"ffn_bwd2_{i}")
        gbuf["f_w_up"] = _mm_tn(rec["xh1"], ln1_g[i:i + 1], ln1_b[i:i + 1], dh, gbuf["f_w_up"], i, f"grad_w_up_{i}")
        if kind == 0:
            dp, gbuf["a_w_out"], gs["a_dw"][j], gs["a_dw_b"][j], gs["a_ln_g"][j], gs["a_ln_b"][j] = _conv_bwd1(
                dr1, rec["chat"], rec["rstdc"], rec["p"], full["a_w_out"], j, fs["a_dw"], fs["a_ln_g"][j:j + 1],
                fs["a_ln_b"][j:j + 1], gbuf["a_w_out"], f"conv_bwd1_{i}")
            win_name, lidx = "a_w_in", j
        elif kind == 1:
            dp, gbuf["b_w_out"], g_ws, g_bs_t, g_blg, g_blb = _sgu_bwd1(
                dr1, rec["zp"], full["b_w_out"], b_ln_g, b_ln_b, wm, wmt, bs_exp, gbuf["b_w_out"], f"sgu_bwd1_{i}")
            win_name, lidx = "b_w_in", 0
        else:
            dp, gbuf["c_w_out"], g_wgrp, g_cscale = _pool_bwd1(
                dr1, rec["ys"], full["c_w_out"], wgrp, fs["c_scale"], gbuf["c_w_out"], f"pool_bwd1_{i}")
            win_name, lidx = "c_w_in", 0
        dxo = _bwd_in(dp, dr1, full[win_name], lidx, alpha, f"mixer_bwd2_{i}")
        gbuf[win_name] = _mm_tn(rec["xin"], rec["gin"], rec["bin"], dp, gbuf[win_name], lidx, f"grad_w_in_{i}")
    grad_x = dxo[None]

    gfull = []
    for k, kind in zip(BIG, kinds):
        a = g_wgrp if k == "c_w_grp" else gbuf[k]
        gfull.append(a.reshape(a.shape[0], N_CHIPS, -1, a.shape[-1]) if kind == "row" else a)
    got = _rs_pair(gfull, kinds)
    pair_sum = [_pair_sum(a, t, kind, core1, "rs_pair_sum_" + k) for k, kind, a, t in zip(BIG, kinds, gfull, got)]
    from_chips = _rs_chips(pair_sum, kinds)
    half_sum = [_chip_sum(t, rb, kind, chip1, "rs_chip_sum_" + k)
                for k, kind, t, rb in zip(BIG, kinds, pair_sum, from_chips)]
    other_half = _rs_join(half_sum)
    grads, delta, new_m, new_v = {}, {}, {}, {}
    for k, own, oth in zip(BIG, half_sum, other_half):
        outs = _adamw_big(_as3d(w[k]), _as3d(mom[k]), _as3d(var[k]), own, oth, core1, "adamw_" + k)
        grads[k], delta[k], new_m[k], new_v[k] = [o.reshape(w[k].shape) for o in outs]

    small_full = {
        "a_dw": jnp.stack(gs["a_dw"]), "a_dw_b": jnp.concatenate(gs["a_dw_b"]), "a_ln_g": jnp.concatenate(gs["a_ln_g"]),
        "a_ln_b": jnp.concatenate(gs["a_ln_b"]), "c_scale": g_cscale, "f_dw": jnp.stack(gs["f_dw"]),
        "b_ln_g": g_blg, "b_ln_b": g_blb, "b_ws": g_ws[None], "b_bs": jnp.transpose(g_bs_t)[None],
        "ln1_g": jnp.concatenate(gs["ln1_g"]), "ln1_b": jnp.concatenate(gs["ln1_b"]),
        "ln2_g": jnp.concatenate(gs["ln2_g"]), "ln2_b": jnp.concatenate(gs["ln2_b"]),
    }
    small_names = SHARDED_SMALL + REPLICATED
    small_shapes = [small_full[n].shape for n in small_names]
    gathered_small = _allgather_small(_pack([small_full[n] for n in small_names]), "allgather_small_grads")
    summed = _unpack(_sum_lead(gathered_small, "small_grad_sum"), small_shapes)
    for n, a in zip(small_names, summed):
        if n in SHARDED_SMALL:
            cs = w[n].shape[-1]
            a = lax.dynamic_slice_in_dim(a, chip * cs, cs, axis=a.ndim - 1)
        grads[n] = a

    shapes = [w[n].shape for n in small_names]
    ds_, ms_, vs_ = _adamw(_pack([w[n] for n in small_names]), _pack([grads[n] for n in small_names]),
                           _pack([mom[n] for n in small_names]), _pack([var[n] for n in small_names]), "adamw_small")
    for n, a, bb, cc in zip(small_names, _unpack(ds_, shapes), _unpack(ms_, shapes), _unpack(vs_, shapes)):
        delta[n], new_m[n], new_v[n] = a, bb, cc

    return (loss, grad_x, *[grads[n] for n in WEIGHTS], *[delta[n] for n in WEIGHTS],
            *[new_m[n] for n in WEIGHTS], *[new_v[n] for n in WEIGHTS])
```

```python
import math

import jax
import jax.numpy as jnp
from jax import lax
from jax.experimental import pallas as pl
from jax.experimental.pallas import tpu as pltpu

F32 = jnp.float32
BF16 = jnp.bfloat16

LN_EPS = 1e-5
POOL_WINDOWS = (2, 4, 8, 16)
CHUNK = 128
ADAM_LR = 0.001
ADAM_B1 = 0.9
ADAM_B2 = 0.999
ADAM_EPS = 1e-08
ADAM_WD = 0.01
ADAM_STEP = 10

LANES = 128
SUBLANES_BF16 = 16
N_CHIPS = 4
N_DEV = 8
VMEM_LIMIT = 60 * 1024 * 1024

TM_FFN = 512
TM_FFN_BWD1 = 256
TM_CONV = 256
TM_SGU = 512
TM_POOL = 512
TM_BWD_IN = 512
TS_MM_TN = 1024
CW_FFN = 256
CW_FFN_BWD2 = 512
CONV_HALO = 32
CONV_ROW_BLOCK = 64
POOL_HALO = 16
FFN_HALO = 16

MESH = pl.DeviceIdType.MESH


def _cparams(n_grid=1, parallel=False):
    sem = ("parallel" if parallel else "arbitrary",) * n_grid
    return pltpu.CompilerParams(dimension_semantics=sem, vmem_limit_bytes=VMEM_LIMIT)


def _resident(block, imap):
    return pl.BlockSpec(block, imap, pipeline_mode=pl.Buffered(1))


def _wspec(w, l):
    _, r, c = w.shape
    return _resident((None, r, c), lambda *_: (l, 0, 0))


def _rowspec(d):
    return pl.BlockSpec((1, d), lambda *_: (0, 0))


def _dot(a, b):
    return jnp.dot(a, b, preferred_element_type=F32)


def _dot_nt(a, b):
    return lax.dot_general(a, b, (((1,), (1,)), ((), ())), preferred_element_type=F32)


def _dot_tn(a, b):
    return lax.dot_general(a, b, (((0,), (0,)), ((), ())), preferred_element_type=F32)


def _sigmoid(x):
    return jax.nn.sigmoid(x)


def _ln_stats(r):
    mu = jnp.mean(r, axis=1, keepdims=True)
    xc = r - mu
    var = jnp.mean(xc * xc, axis=1, keepdims=True)
    rstd = lax.rsqrt(var + LN_EPS)
    return xc * rstd, rstd


def _ln_bwd(dy, xhat, rstd, g):
    dxh = dy * g
    m1 = jnp.mean(dxh, axis=1, keepdims=True)
    m2 = jnp.mean(dxh * xhat, axis=1, keepdims=True)
    return rstd * (dxh - m1 - xhat * m2)


def _colsum(v):
    return jnp.sum(v, axis=0, keepdims=True)


def _gelu(z):
    return 0.5 * z * (1.0 + lax.erf(z * (1.0 / math.sqrt(2.0))))


def _gelu_grad(z):
    cdf = 0.5 * (1.0 + lax.erf(z * (1.0 / math.sqrt(2.0))))
    pdf = jnp.exp(-0.5 * z * z) * (1.0 / math.sqrt(2.0 * math.pi))
    return cdf + z * pdf


def _shift_down(v, k, prev_rows):
    rolled = pltpu.roll(v, k, 0)
    head = rolled[0:8]
    rows = lax.broadcasted_iota(jnp.int32, head.shape, 0)
    for r in range(k):
        head = jnp.where(rows == r, prev_rows[k - 1 - r], head)
    return jnp.concatenate([head, rolled[8:]], axis=0)


def _shift_up(v, k, next_rows):
    tm = v.shape[0]
    rolled = pltpu.roll(v, tm - k, 0)
    tail = rolled[tm - 8:tm]
    rows = lax.broadcasted_iota(jnp.int32, tail.shape, 0)
    for r in range(k):
        tail = jnp.where(rows == 8 - k + r, next_rows[r], tail)
    return jnp.concatenate([rolled[0:tm - 8], tail], axis=0)


def _fill_shifted(base_scr, sh_scr):
    nrows = sh_scr.shape[1]
    for r in range(1, 8):
        sh_scr[r - 1, :, :] = base_scr[pl.ds(r, nrows), :]


def _tap(base_scr, sh_scr, off, r0, nrows, cols):
    q, r = divmod(off, 8)
    if r == 0:
        return base_scr[pl.ds(r0 + 8 * q, nrows), cols]
    return sh_scr[r - 1, pl.ds(r0 + 8 * q, nrows), cols]


def _pick_rows(r, c, itemsize, cap_bytes):
    best = None
    for t in range(16, r + 1, 16):
        if r % t == 0 and t * c * itemsize <= cap_bytes:
            best = t
    return best if best is not None else r


def _ffn_conv_cols(h, dw_ref, c0, cw, prev1, prev2):
    kw = dw_ref.shape[0]
    h1 = _shift_down(h, 1, [prev1])
    h2 = _shift_down(h, 2, [prev1, prev2])
    hc = dw_ref[kw - 1:kw, c0:c0 + cw] * h + dw_ref[kw - 2:kw - 1, c0:c0 + cw] * h1 + dw_ref[kw - 3:kw - 2, c0:c0 + cw] * h2
    return hc, h1, h2


def _ffn_fwd(xh1, g1, b1, wup, l, fdw, wdn, alpha, name):
    s, d = xh1.shape
    f2 = wup.shape[2]
    f = f2 // 2
    tm = min(TM_FFN, s)
    cw = min(CW_FFN, f)
    n, nck = s // tm, f // cw
    assert fdw.shape[1] == 3 and s % tm == 0 and f % cw == 0

    def body(xh_ref, g_ref, b_ref, wup_ref, dw_ref, wdn_ref, xo_ref, rs_ref, hs_ref, hcs_ref, carry):
        @pl.when(pl.program_id(0) == 0)
        def _():
            carry[...] = jnp.zeros_like(carry)

        x1 = xh_ref[...] * g_ref[...] + b_ref[...]
        xb = x1.astype(BF16)
        o = jnp.zeros((tm, d), F32)

        def up_proj(j):
            return [_dot(xb, wup_ref[:, half * f + j * cw:half * f + (j + 1) * cw]) for half in range(2)]

        ahead = up_proj(0)
        for j in range(nck):
            hh = ahead
            if j + 1 < nck:
                ahead = up_proj(j + 1)
            parts = []
            for half in range(2):
                c0 = half * f + j * cw
                h = hh[half]
                hs_ref[:, c0:c0 + cw] = h.astype(BF16)
                hc, _, _ = _ffn_conv_cols(h, dw_ref, c0, cw, carry[7:8, c0:c0 + cw], carry[6:7, c0:c0 + cw])
                carry[:, c0:c0 + cw] = h[tm - 8:tm, :]
                hcs_ref[:, c0:c0 + cw] = hc.astype(BF16)
                parts.append(hc)
            gg, vv = parts
            a = (gg * _sigmoid(gg) * vv).astype(BF16)
            o = o + _dot(a, wdn_ref[j * cw:(j + 1) * cw, :])
        xhat, rstd = _ln_stats(alpha * x1 + o)
        xo_ref[...] = xhat
        rs_ref[...] = rstd

    tile = pl.BlockSpec((tm, d), lambda i: (i, 0))
    return pl.pallas_call(
        body, grid=(n,),
        in_specs=[tile, _rowspec(d), _rowspec(d), _wspec(wup, l),
                  pl.BlockSpec((None, 3, f2), lambda i: (l, 0, 0)), _wspec(wdn, l)],
        out_specs=[tile, pl.BlockSpec((tm, 1), lambda i: (i, 0)), pl.BlockSpec((tm, f2), lambda i: (i, 0)),
                   pl.BlockSpec((tm, f2), lambda i: (i, 0))],
        out_shape=[jax.ShapeDtypeStruct((s, d), F32), jax.ShapeDtypeStruct((s, 1), F32),
                   jax.ShapeDtypeStruct((s, f2), BF16), jax.ShapeDtypeStruct((s, f2), BF16)],
        scratch_shapes=[pltpu.VMEM((8, f2), F32)],
        compiler_params=_cparams(), name=name,
    )(xh1, g1, b1, wup, fdw, wdn)


def _ffn_bwd1(dx2, xh2, rstd2, g2, hcs, wdn, l, gwdn_buf, name):
    s, d = dx2.shape
    f2 = hcs.shape[1]
    f = f2 // 2
    tm = min(TM_FFN_BWD1, s)
    cw = min(CW_FFN, f)
    n, nck = s // tm, f // cw

    def body(dx_ref, xh_ref, rs_ref, g_ref, hcs_ref, wdn_ref, buf_ref,
             dr_ref, dhc_ref, gwdn_ref, gg_ref, gb_ref):
        @pl.when(pl.program_id(0) == 0)
        def _():
            gwdn_ref[...] = jnp.zeros_like(gwdn_ref)
            gg_ref[...] = jnp.zeros_like(gg_ref)
            gb_ref[...] = jnp.zeros_like(gb_ref)

        dx = dx_ref[...]
        xh = xh_ref[...]
        gg_ref[...] += _colsum(dx * xh)
        gb_ref[...] += _colsum(dx)
        dr = _ln_bwd(dx, xh, rs_ref[...], g_ref[...])
        dr_ref[...] = dr
        dob = dr.astype(BF16)

        def d_act(j):
            return _dot_nt(dob, wdn_ref[j * cw:(j + 1) * cw, :])

        da_ahead = d_act(0)
        for j in range(nck):
            da = da_ahead
            if j + 1 < nck:
                da_ahead = d_act(j + 1)
            gt = hcs_ref[:, j * cw:(j + 1) * cw].astype(F32)
            vv = hcs_ref[:, f + j * cw:f + (j + 1) * cw].astype(F32)
            sg = _sigmoid(gt)
            sl = gt * sg
            a = (sl * vv).astype(BF16)
            gwdn_ref[j * cw:(j + 1) * cw, :] += _dot_tn(a, dob)
            dhc_ref[:, j * cw:(j + 1) * cw] = (da * vv * (sg * (1.0 + gt * (1.0 - sg)))).astype(BF16)
            dhc_ref[:, f + j * cw:f + (j + 1) * cw] = (da * sl).astype(BF16)

    tile = pl.BlockSpec((tm, d), lambda i: (i, 0))
    wide = pl.BlockSpec((tm, f2), lambda i: (i, 0))
    nl = gwdn_buf.shape[0]
    return pl.pallas_call(
        body, grid=(n,),
        in_specs=[tile, tile, pl.BlockSpec((tm, 1), lambda i: (i, 0)), _rowspec(d), wide,
                  _wspec(wdn, l), pl.BlockSpec(memory_space=pl.ANY)],
        out_specs=[tile, wide, pl.BlockSpec((None, f, d), lambda i: (l, 0, 0)), _rowspec(d), _rowspec(d)],
        out_shape=[jax.ShapeDtypeStruct((s, d), F32), jax.ShapeDtypeStruct((s, f2), BF16),
                   jax.ShapeDtypeStruct((nl, f, d), F32),
                   jax.ShapeDtypeStruct((1, d), F32), jax.ShapeDtypeStruct((1, d), F32)],
        input_output_aliases={6: 2},
        compiler_params=_cparams(), name=name,
    )(dx2, xh2, rstd2, g2, hcs, wdn, gwdn_buf)


def _bwd_in(dp, dres, w, l, alpha, name):
    s, d = dres.shape
    nn = dp.shape[1]
    tm = min(TM_BWD_IN, s)
    n = s // tm
    tile = pl.BlockSpec((tm, d), lambda i: (i, 0))

    def body(dp_ref, dres_ref, w_ref, o_ref):
        o_ref[...] = alpha * dres_ref[...] + _dot_nt(dp_ref[...], w_ref[...])

    return pl.pallas_call(
        body, grid=(n,),
        in_specs=[pl.BlockSpec((tm, nn), lambda i: (i, 0)), tile, _wspec(w, l)],
        out_specs=tile, out_shape=jax.ShapeDtypeStruct((s, d), F32),
        compiler_params=_cparams(parallel=True), name=name,
    )(dp, dres, w)


def _ffn_bwd2(dhc, hs, dres, wup, l, fdw, xh, rstd, g, alpha, name):
    s, d = dres.shape
    f2 = dhc.shape[1]
    tm = min(TM_BWD_IN, s)
    n = s // tm
    hb = FFN_HALO
    cw = min(CW_FFN_BWD2, f2)
    nck = f2 // cw
    halo_blocks = tm // hb
    assert f2 % cw == 0 and fdw.shape[1] == 3

    def body(dhc_ref, halo_ref, hs_ref, dres_ref, w_ref, dw_ref, xh_ref, rs_ref, g_ref,
             o_ref, dh_ref, gdw_ref, gg_ref, gb_ref):
        i = pl.program_id(0)

        @pl.when(i == 0)
        def _():
            gdw_ref[...] = jnp.zeros_like(gdw_ref)
            gg_ref[...] = jnp.zeros_like(gg_ref)
            gb_ref[...] = jnp.zeros_like(gb_ref)

        has_next = i < n - 1
        dx = alpha * dres_ref[...]
        for j in range(nck):
            c0 = j * cw
            dc = dhc_ref[:, c0:c0 + cw].astype(F32)
            hal = jnp.where(has_next, halo_ref[:, c0:c0 + cw].astype(F32), 0.0)
            nxt = [hal[0:1], hal[1:2]]
            u1 = _shift_up(dc, 1, nxt[:1])
            u2 = _shift_up(dc, 2, nxt)
            h = hs_ref[:, c0:c0 + cw].astype(F32)
            gdw_ref[2:3, c0:c0 + cw] += _colsum(dc * h)
            gdw_ref[1:2, c0:c0 + cw] += _colsum(u1 * h)
            gdw_ref[0:1, c0:c0 + cw] += _colsum(u2 * h)
            dh = (dw_ref[2:3, c0:c0 + cw] * dc + dw_ref[1:2, c0:c0 + cw] * u1
                  + dw_ref[0:1, c0:c0 + cw] * u2).astype(BF16)
            dh_ref[:, c0:c0 + cw] = dh
            dx = dx + _dot_nt(dh, w_ref[:, c0:c0 + cw])
        xhv = xh_ref[...]
        gg_ref[...] += _colsum(dx * xhv)
        gb_ref[...] += _colsum(dx)
        o_ref[...] = _ln_bwd(dx, xhv, rs_ref[...], g_ref[...])

    tile = pl.BlockSpec((tm, d), lambda i: (i, 0))
    wide = pl.BlockSpec((tm, f2), lambda i: (i, 0))
    return pl.pallas_call(
        body, grid=(n,),
        in_specs=[wide, pl.BlockSpec((hb, f2), lambda i: (jnp.minimum((i + 1) * halo_blocks, s // hb - 1), 0)),
                  wide, tile, _wspec(wup, l), pl.BlockSpec((None, 3, f2), lambda i: (l, 0, 0)), tile,
                  pl.BlockSpec((tm, 1), lambda i: (i, 0)), _rowspec(d)],
        out_specs=[tile, wide, pl.BlockSpec((3, f2), lambda i: (0, 0)), _rowspec(d), _rowspec(d)],
        out_shape=[jax.ShapeDtypeStruct((s, d), F32), jax.ShapeDtypeStruct((s, f2), BF16),
                   jax.ShapeDtypeStruct((3, f2), F32),
                   jax.ShapeDtypeStruct((1, d), F32), jax.ShapeDtypeStruct((1, d), F32)],
        compiler_params=_cparams(), name=name,
    )(dhc, dhc, hs, dres, wup, fdw, xh, rstd, g)


def _mm_tn(a, ga, ba, bm, buf, l, name):
    s, k = a.shape
    nn = bm.shape[1]
    ts = min(TS_MM_TN, s)
    tn = nn // N_CHIPS if nn > 1024 else nn
    nj, ns = nn // tn, s // ts

    def body(a_ref, g_ref, b_ref, bm_ref, buf_ref, o_ref):
        @pl.when(pl.program_id(1) == 0)
        def _():
            o_ref[...] = jnp.zeros_like(o_ref)

        ab = (a_ref[...] * g_ref[...] + b_ref[...]).astype(BF16)
        o_ref[...] += _dot_tn(ab, bm_ref[...])

    return pl.pallas_call(
        body, grid=(nj, ns),
        in_specs=[pl.BlockSpec((ts, k), lambda j, t: (t, 0)), _rowspec(k), _rowspec(k),
                  pl.BlockSpec((ts, tn), lambda j, t: (t, j)), pl.BlockSpec(memory_space=pl.ANY)],
        out_specs=pl.BlockSpec((None, k, tn), lambda j, t: (l, 0, j)),
        out_shape=jax.ShapeDtypeStruct(buf.shape, F32),
        input_output_aliases={4: 0},
        compiler_params=_cparams(2), name=name,
    )(a, ga, ba, bm, buf)


def _conv_fwd(xin, gin, bin_, win, l, adw, adwb, lng, lnb, wout, alpha, name):
    s, d = xin.shape
    kw = adw.shape[1]
    hb = CONV_HALO
    tm = min(TM_CONV, s)
    n = s // tm
    assert kw - 1 <= hb <= tm

    def body(x_ref, g_ref, b_ref, win_ref, dw_ref, dwb_ref, lng_ref, lnb_ref, wout_ref,
             xo_ref, rs_ref, p_ref, chat_ref, rsc_ref, u_scr, u8_scr):
        @pl.when(pl.program_id(0) == 0)
        def _():
            u_scr[0:hb, :] = jnp.zeros((hb, d), F32)

        x = x_ref[...] * g_ref[...] + b_ref[...]
        pm = _dot(x.astype(BF16), win_ref[...])
        p_ref[...] = pm.astype(BF16)
        u = pm[:, :d] * _sigmoid(pm[:, d:])
        u_scr[hb:hb + tm, :] = u
        _fill_shifted(u_scr, u8_scr)
        acc = dwb_ref[...] + dw_ref[kw - 1:kw, :] * u
        for k in range(kw - 1):
            acc = acc + dw_ref[k:k + 1, :] * _tap(u_scr, u8_scr, hb - (kw - 1) + k, 0, tm, slice(None))
        u_scr[0:hb, :] = u_scr[tm:tm + hb, :]
        chat, rstdc = _ln_stats(acc)
        chat_ref[...] = chat.astype(BF16)
        rsc_ref[...] = rstdc
        nv = chat * lng_ref[...] + lnb_ref[...]
        sv = (nv * _sigmoid(nv)).astype(BF16)
        xhat, rstd = _ln_stats(alpha * x + _dot(sv, wout_ref[...]))
        xo_ref[...] = xhat
        rs_ref[...] = rstd

    tile = pl.BlockSpec((tm, d), lambda i: (i, 0))
    col = pl.BlockSpec((tm, 1), lambda i: (i, 0))
    return pl.pallas_call(
        body, grid=(n,),
        in_specs=[tile, _rowspec(d), _rowspec(d), _wspec(win, l),
                  pl.BlockSpec((None, kw, d), lambda i: (l, 0, 0)), _rowspec(d), _rowspec(d), _rowspec(d),
                  _wspec(wout, l)],
        out_specs=[tile, col, pl.BlockSpec((tm, 2 * d), lambda i: (i, 0)), tile, col],
        out_shape=[jax.ShapeDtypeStruct((s, d), F32), jax.ShapeDtypeStruct((s, 1), F32),
                   jax.ShapeDtypeStruct((s, 2 * d), BF16), jax.ShapeDtypeStruct((s, d), BF16),
                   jax.ShapeDtypeStruct((s, 1), F32)],
        scratch_shapes=[pltpu.VMEM((tm + hb, d), F32), pltpu.VMEM((7, tm + hb - 8, d), F32)],
        compiler_params=_cparams(), name=name,
    )(xin, gin, bin_, win, adw, adwb, lng, lnb, wout)


def _conv_bwd1(dr1, chat, rstdc, p, wout, l, adw, lng, lnb, gwout_buf, name):
    s, d = dr1.shape
    kw = adw.shape[1]
    hb = CONV_HALO
    tm = min(TM_CONV, s)
    n = s // tm
    halo_blocks = tm // hb
    rbl = CONV_ROW_BLOCK

    def body(dr_ref, chat_ref, rsc_ref, p_ref, halo_ref, wout_ref, dw_ref, lng_ref, lnb_ref, buf_ref,
             dp_ref, gwout_ref, gdw_ref, gdwb_ref, glng_ref, glnb_ref, u_scr, dc_scr, u8_scr, dc8_scr):
        i = pl.program_id(0)
        t = n - 1 - i

        @pl.when(i == 0)
        def _():
            dc_scr[tm:tm + hb, :] = jnp.zeros((hb, d), F32)
            gwout_ref[...] = jnp.zeros_like(gwout_ref)
            gdw_ref[...] = jnp.zeros_like(gdw_ref)
            gdwb_ref[...] = jnp.zeros_like(gdwb_ref)
            glng_ref[...] = jnp.zeros_like(glng_ref)
            glnb_ref[...] = jnp.zeros_like(glnb_ref)

        dob = dr_ref[...].astype(BF16)
        chat = chat_ref[...].astype(F32)
        lng = lng_ref[...]
        nv = chat * lng + lnb_ref[...]
        sgn = _sigmoid(nv)
        gwout_ref[...] += _dot_tn((nv * sgn).astype(BF16), dob)
        dn = _dot_nt(dob, wout_ref[...]) * (sgn * (1.0 + nv * (1.0 - sgn)))
        glng_ref[...] += _colsum(dn * chat)
        glnb_ref[...] += _colsum(dn)
        dc = _ln_bwd(dn, chat, rsc_ref[...], lng)
        gdwb_ref[...] += _colsum(dc)

        pm = p_ref[...].astype(F32)
        a = pm[:, :d]
        sg = _sigmoid(pm[:, d:])
        ph = halo_ref[...].astype(F32)
        u_scr[0:hb, :] = jnp.where(t > 0, ph[:, :d] * _sigmoid(ph[:, d:]), 0.0)
        u_scr[hb:hb + tm, :] = a * sg
        dc_scr[0:tm, :] = dc
        _fill_shifted(u_scr, u8_scr)
        _fill_shifted(dc_scr, dc8_scr)
        du = dw_ref[kw - 1:kw, :] * dc
        for k in range(kw - 1):
            du = du + dw_ref[k:k + 1, :] * _tap(dc_scr, dc8_scr, kw - 1 - k, 0, tm, slice(None))
        for cb in range(d // LANES):
            cols = pl.ds(cb * LANES, LANES)

            def rows_step(rb, accs, cols=cols):
                r0 = pl.multiple_of(rb * rbl, rbl)
                dcb = dc_scr[pl.ds(r0, rbl), cols]
                out = []
                for k in range(kw):
                    prod = dcb * _tap(u_scr, u8_scr, hb - (kw - 1) + k, r0, rbl, cols)
                    part = prod[0:8]
                    for g8 in range(1, rbl // 8):
                        part = part + prod[8 * g8:8 * g8 + 8]
                    out.append(accs[k] + part)
                return tuple(out)

            accs = lax.fori_loop(0, tm // rbl, rows_step, tuple(jnp.zeros((8, LANES), F32) for _ in range(kw)))
            for k in range(kw):
                gdw_ref[k:k + 1, cols] += _colsum(accs[k])
        dc_scr[tm:tm + hb, :] = dc[0:hb, :]
        dp_ref[:, :d] = (du * sg).astype(BF16)
        dp_ref[:, d:] = (du * a * sg * (1.0 - sg)).astype(BF16)

    tile = pl.BlockSpec((tm, d), lambda i: (n - 1 - i, 0))
    col = pl.BlockSpec((tm, 1), lambda i: (n - 1 - i, 0))
    nl = gwout_buf.shape[0]
    return pl.pallas_call(
        body, grid=(n,),
        in_specs=[tile, tile, col, pl.BlockSpec((tm, 2 * d), lambda i: (n - 1 - i, 0)),
                  pl.BlockSpec((hb, 2 * d), lambda i: (jnp.maximum((n - 1 - i) * halo_blocks - 1, 0), 0)),
                  _wspec(wout, l), pl.BlockSpec((None, kw, d), lambda i: (l, 0, 0)), _rowspec(d), _rowspec(d),
                  pl.BlockSpec(memory_space=pl.ANY)],
        out_specs=[pl.BlockSpec((tm, 2 * d), lambda i: (n - 1 - i, 0)),
                   pl.BlockSpec((None, d, d), lambda i: (l, 0, 0)),
                   pl.BlockSpec((kw, d), lambda i: (0, 0)), _rowspec(d), _rowspec(d), _rowspec(d)],
        out_shape=[jax.ShapeDtypeStruct((s, 2 * d), BF16), jax.ShapeDtypeStruct((nl, d, d), F32),
                   jax.ShapeDtypeStruct((kw, d), F32), jax.ShapeDtypeStruct((1, d), F32),
                   jax.ShapeDtypeStruct((1, d), F32), jax.ShapeDtypeStruct((1, d), F32)],
        scratch_shapes=[pltpu.VMEM((tm + hb, d), F32), pltpu.VMEM((tm + hb, d), F32),
                        pltpu.VMEM((7, tm + hb - 8, d), F32), pltpu.VMEM((7, tm + hb - 8, d), F32)],
        input_output_aliases={9: 1},
        compiler_params=_cparams(), name=name,
    )(dr1, chat, rstdc, p, p, wout, adw, lng, lnb, gwout_buf)


def _sgu_gate(vn, wm_ref, bs_ref, s_scr, tm, nh):
    for ch in range(tm // CHUNK):
        r0 = ch * CHUNK
        for h in range(nh):
            c0 = h * CHUNK
            s_scr[r0:r0 + CHUNK, c0:c0 + CHUNK] = (
                _dot(wm_ref[h], vn[r0:r0 + CHUNK, c0:c0 + CHUNK]) + bs_ref[:, c0:c0 + CHUNK])


def _sgu_fwd(xin, gin, bin_, win, lg, lb, wm, bs_exp, wout, alpha, name):
    s, d = xin.shape
    nh = wm.shape[0]
    tm = min(TM_SGU, s)
    n = s // tm
    assert tm % CHUNK == 0 and nh * CHUNK == d

    def body(x_ref, g_ref, b_ref, win_ref, lg_ref, lb_ref, wm_ref, bs_ref, wout_ref,
             xo_ref, rs_ref, zp_ref, s_scr):
        x = x_ref[...] * g_ref[...] + b_ref[...]
        zp = _dot(x.astype(BF16), win_ref[...])
        zp_ref[...] = zp.astype(BF16)
        z = _gelu(zp)
        vhat, _ = _ln_stats(z[:, d:])
        vn = (vhat * lg_ref[...] + lb_ref[...]).astype(BF16)
        _sgu_gate(vn, wm_ref, bs_ref, s_scr, tm, nh)
        q = (z[:, :d] * s_scr[...]).astype(BF16)
        xhat, rstd = _ln_stats(alpha * x + _dot(q, wout_ref[...]))
        xo_ref[...] = xhat
        rs_ref[...] = rstd

    tile = pl.BlockSpec((tm, d), lambda i: (i, 0))
    return pl.pallas_call(
        body, grid=(n,),
        in_specs=[tile, _rowspec(d), _rowspec(d), _wspec(win, 0), _rowspec(d), _rowspec(d),
                  _resident((nh, CHUNK, CHUNK), lambda i: (0, 0, 0)),
                  _resident((CHUNK, d), lambda i: (0, 0)), _wspec(wout, 0)],
        out_specs=[tile, pl.BlockSpec((tm, 1), lambda i: (i, 0)), pl.BlockSpec((tm, 2 * d), lambda i: (i, 0))],
        out_shape=[jax.ShapeDtypeStruct((s, d), F32), jax.ShapeDtypeStruct((s, 1), F32),
                   jax.ShapeDtypeStruct((s, 2 * d), BF16)],
        scratch_shapes=[pltpu.VMEM((tm, d), F32)],
        compiler_params=_cparams(parallel=True), name=name,
    )(xin, gin, bin_, win, lg, lb, wm, bs_exp, wout)


def _sgu_bwd1(dr1, zp, wout, lg, lb, wm, wmt, bs_exp, gwout_buf, name):
    s, d = dr1.shape
    nh = wm.shape[0]
    tm = min(TM_SGU, s)
    n = s // tm

    def body(dr_ref, zp_ref, wout_ref, lg_ref, lb_ref, wm_ref, wmt_ref, bs_ref, buf_ref,
             dzp_ref, gwout_ref, gws_ref, gbs_ref, glg_ref, glb_ref, s_scr, dvn_scr, bs_acc):
        i = pl.program_id(0)

        @pl.when(i == 0)
        def _():
            gwout_ref[...] = jnp.zeros_like(gwout_ref)
            gws_ref[...] = jnp.zeros_like(gws_ref)
            glg_ref[...] = jnp.zeros_like(glg_ref)
            glb_ref[...] = jnp.zeros_like(glb_ref)
            bs_acc[...] = jnp.zeros_like(bs_acc)

        dob = dr_ref[...].astype(BF16)
        zp = zp_ref[...].astype(F32)
        z = _gelu(zp)
        u = z[:, :d]
        lg = lg_ref[...]
        vhat, rstdv = _ln_stats(z[:, d:])
        vn = (vhat * lg + lb_ref[...]).astype(BF16)
        _sgu_gate(vn, wm_ref, bs_ref, s_scr, tm, nh)
        sv = s_scr[...]
        gwout_ref[...] += _dot_tn((u * sv).astype(BF16), dob)
        dq = _dot_nt(dob, wout_ref[...])
        ds = dq * u
        dsb = ds.astype(BF16)
        part = jnp.zeros((CHUNK, d), F32)
        for ch in range(tm // CHUNK):
            r0 = ch * CHUNK
            part = part + ds[r0:r0 + CHUNK, :]
            for h in range(nh):
                c0 = h * CHUNK
                blk = dsb[r0:r0 + CHUNK, c0:c0 + CHUNK]
                gws_ref[h] += _dot_nt(blk, vn[r0:r0 + CHUNK, c0:c0 + CHUNK])
                dvn_scr[r0:r0 + CHUNK, c0:c0 + CHUNK] = _dot(wmt_ref[h], blk)
        bs_acc[...] += part
        dvn = dvn_scr[...]
        glg_ref[...] += _colsum(dvn * vhat)
        glb_ref[...] += _colsum(dvn)
        dv = _ln_bwd(dvn, vhat, rstdv, lg)
        gp = _gelu_grad(zp)
        dzp_ref[:, :d] = (dq * sv * gp[:, :d]).astype(BF16)
        dzp_ref[:, d:] = (dv * gp[:, d:]).astype(BF16)

        @pl.when(i == n - 1)
        def _():
            rows = lax.broadcasted_iota(jnp.int32, (CHUNK, CHUNK), 0)
            cols = lax.broadcasted_iota(jnp.int32, (CHUNK, CHUNK), 1)
            tril = (cols <= rows).astype(F32)
            acc = bs_acc[...]
            for h in range(nh):
                gws_ref[h] = gws_ref[h] * tril
                gbs_ref[:, h:h + 1] = jnp.sum(acc[:, h * CHUNK:(h + 1) * CHUNK], axis=1, keepdims=True)

    tile = pl.BlockSpec((tm, d), lambda i: (i, 0))
    wide = pl.BlockSpec((tm, 2 * d), lambda i: (i, 0))
    hspec = _resident((nh, CHUNK, CHUNK), lambda i: (0, 0, 0))
    return pl.pallas_call(
        body, grid=(n,),
        in_specs=[tile, wide, _wspec(wout, 0), _rowspec(d), _rowspec(d), hspec, hspec,
                  _resident((CHUNK, d), lambda i: (0, 0)), pl.BlockSpec(memory_space=pl.ANY)],
        out_specs=[wide, pl.BlockSpec((None, d, d), lambda i: (0, 0, 0)),
                   pl.BlockSpec((nh, CHUNK, CHUNK), lambda i: (0, 0, 0)),
                   pl.BlockSpec((CHUNK, nh), lambda i: (0, 0)), _rowspec(d), _rowspec(d)],
        out_shape=[jax.ShapeDtypeStruct((s, 2 * d), BF16), jax.ShapeDtypeStruct(gwout_buf.shape, F32),
                   jax.ShapeDtypeStruct((nh, CHUNK, CHUNK), F32), jax.ShapeDtypeStruct((CHUNK, nh), F32),
                   jax.ShapeDtypeStruct((1, d), F32), jax.ShapeDtypeStruct((1, d), F32)],
        scratch_shapes=[pltpu.VMEM((tm, d), F32), pltpu.VMEM((tm, d), F32), pltpu.VMEM((CHUNK, d), F32)],
        input_output_aliases={8: 1},
        compiler_params=_cparams(), name=name,
    )(dr1, zp, wout, lg, lb, wm, wmt, bs_exp, gwout_buf)


def _pool_counts(t0, tm, w):
    pos = t0 + lax.broadcasted_iota(jnp.int32, (tm, 1), 0)
    return jnp.minimum(pos + 1, w).astype(F32)


def _pool_fwd(xin, gin, bin_, win, wg, scale, wout, alpha, name):
    s, d = xin.shape
    ng, dg = wg.shape[0], wg.shape[1]
    hb = POOL_HALO
    tm = min(TM_POOL, s)
    n = s // tm
    assert ng == len(POOL_WINDOWS) and ng * dg == d and max(POOL_WINDOWS) <= hb

    def body(x_ref, g_ref, b_ref, win_ref, wg_ref, sc_ref, wout_ref, xo_ref, rs_ref, ys_ref, y_scr, z_scr):
        i = pl.program_id(0)

        @pl.when(i == 0)
        def _():
            y_scr[0:hb, :] = jnp.zeros((hb, d), F32)

        x = x_ref[...] * g_ref[...] + b_ref[...]
        y = _dot(x.astype(BF16), win_ref[...])
        ys_ref[...] = y.astype(BF16)
        y_scr[hb:hb + tm, :] = y
        for g, w in enumerate(POOL_WINDOWS):
            c0 = g * dg
            acc = y[:, c0:c0 + dg]
            for dd in range(1, w):
                acc = acc + y_scr[pl.ds(hb - dd, tm), c0:c0 + dg]
            pg = acc / _pool_counts(i * tm, tm, w) - y[:, c0:c0 + dg]
            z_scr[:, c0:c0 + dg] = _dot(pg.astype(BF16), wg_ref[g])
        y_scr[0:hb, :] = y_scr[tm:tm + hb, :]
        zz = (z_scr[...] * sc_ref[...]).astype(BF16)
        xhat, rstd = _ln_stats(alpha * x + _dot(zz, wout_ref[...]))
        xo_ref[...] = xhat
        rs_ref[...] = rstd

    tile = pl.BlockSpec((tm, d), lambda i: (i, 0))
    return pl.pallas_call(
        body, grid=(n,),
        in_specs=[tile, _rowspec(d), _rowspec(d), _wspec(win, 0),
                  _resident((ng, dg, dg), lambda i: (0, 0, 0)), _rowspec(d), _wspec(wout, 0)],
        out_specs=[tile, pl.BlockSpec((tm, 1), lambda i: (i, 0)), tile],
        out_shape=[jax.ShapeDtypeStruct((s, d), F32), jax.ShapeDtypeStruct((s, 1), F32),
                   jax.ShapeDtypeStruct((s, d), BF16)],
        scratch_shapes=[pltpu.VMEM((tm + hb, d), F32), pltpu.VMEM((tm, d), F32)],
        compiler_params=_cparams(), name=name,
    )(xin, gin, bin_, win, wg, scale, wout)


def _pool_bwd1(dr1, ys, wout, wg, scale, gwout_buf, name):
    s, d = dr1.shape
    ng, dg = wg.shape[0], wg.shape[1]
    hb = POOL_HALO
    tm = min(TM_POOL, s)
    n = s // tm
    halo_blocks = tm // hb

    def body(dr_ref, ys_ref, halo_ref, wout_ref, wg_ref, sc_ref, buf_ref,
             dy_ref, gwout_ref, gwg_ref, gsc_ref, y_scr, e_scr, z_scr, dp_scr):
        i = pl.program_id(0)
        t = n - 1 - i

        @pl.when(i == 0)
        def _():
            e_scr[tm:tm + hb, :] = jnp.zeros((hb, d), F32)
            gwout_ref[...] = jnp.zeros_like(gwout_ref)
            gwg_ref[...] = jnp.zeros_like(gwg_ref)
            gsc_ref[...] = jnp.zeros_like(gsc_ref)

        dob = dr_ref[...].astype(BF16)
        y = ys_ref[...].astype(F32)
        y_scr[0:hb, :] = jnp.where(t > 0, halo_ref[...].astype(F32), 0.0)
        y_scr[hb:hb + tm, :] = y
        pgs = []
        for g, w in enumerate(POOL_WINDOWS):
            c0 = g * dg
            acc = y[:, c0:c0 + dg]
            for dd in range(1, w):
                acc = acc + y_scr[pl.ds(hb - dd, tm), c0:c0 + dg]
            pg = (acc / _pool_counts(t * tm, tm, w) - y[:, c0:c0 + dg]).astype(BF16)
            pgs.append(pg)
            z_scr[:, c0:c0 + dg] = _dot(pg, wg_ref[g])
        zpre = z_scr[...]
        sc = sc_ref[...]
        gwout_ref[...] += _dot_tn((zpre * sc).astype(BF16), dob)
        dz = _dot_nt(dob, wout_ref[...])
        gsc_ref[...] += _colsum(dz * zpre)
        dzpre = (dz * sc).astype(BF16)
        for g, w in enumerate(POOL_WINDOWS):
            c0 = g * dg
            dzg = dzpre[:, c0:c0 + dg]
            gwg_ref[g] += _dot_tn(pgs[g], dzg)
            dp = _dot_nt(dzg, wg_ref[g])
            dp_scr[:, c0:c0 + dg] = dp
            e_scr[0:tm, c0:c0 + dg] = dp / _pool_counts(t * tm, tm, w)
        for g, w in enumerate(POOL_WINDOWS):
            c0 = g * dg
            acc = e_scr[0:tm, c0:c0 + dg]
            for dd in range(1, w):
                acc = acc + e_scr[pl.ds(dd, tm), c0:c0 + dg]
            dy_ref[:, c0:c0 + dg] = (acc - dp_scr[:, c0:c0 + dg]).astype(BF16)
        e_scr[tm:tm + hb, :] = e_scr[0:hb, :]

    tile = pl.BlockSpec((tm, d), lambda i: (n - 1 - i, 0))
    return pl.pallas_call(
        body, grid=(n,),
        in_specs=[tile, tile,
                  pl.BlockSpec((hb, d), lambda i: (jnp.maximum((n - 1 - i) * halo_blocks - 1, 0), 0)),
                  _wspec(wout, 0), _resident((ng, dg, dg), lambda i: (0, 0, 0)), _rowspec(d),
                  pl.BlockSpec(memory_space=pl.ANY)],
        out_specs=[tile, pl.BlockSpec((None, d, d), lambda i: (0, 0, 0)),
                   pl.BlockSpec((ng, dg, dg), lambda i: (0, 0, 0)), _rowspec(d)],
        out_shape=[jax.ShapeDtypeStruct((s, d), BF16), jax.ShapeDtypeStruct(gwout_buf.shape, F32),
                   jax.ShapeDtypeStruct((ng, dg, dg), F32), jax.ShapeDtypeStruct((1, d), F32)],
        scratch_shapes=[pltpu.VMEM((tm + hb, d), F32), pltpu.VMEM((tm + hb, d), F32),
                        pltpu.VMEM((tm, d), F32), pltpu.VMEM((tm, d), F32)],
        input_output_aliases={6: 1},
        compiler_params=_cparams(), name=name,
    )(dr1, ys, ys, wout, wg, scale, gwout_buf)


def _loss_head(xh, g, b, target, name):
    s, d = xh.shape
    tm = min(512, s)
    n = s // tm

    def body(xh_ref, g_ref, b_ref, t_ref, dy_ref, loss_ref, acc):
        i = pl.program_id(0)

        @pl.when(i == 0)
        def _():
            acc[...] = jnp.zeros_like(acc)

        err = xh_ref[...] * g_ref[...] + b_ref[...] - t_ref[...]
        dy_ref[...] = err * (1.0 / d)
        acc[...] += _colsum(err * err)

        @pl.when(i == n - 1)
        def _():
            loss_ref[...] = (0.5 / d) * jnp.sum(acc[...], axis=1, keepdims=True)

    tile = pl.BlockSpec((tm, d), lambda i: (i, 0))
    return pl.pallas_call(
        body, grid=(n,),
        in_specs=[tile, _rowspec(d), _rowspec(d), tile],
        out_specs=[tile, pl.BlockSpec((1, 1), lambda i: (0, 0))],
        out_shape=[jax.ShapeDtypeStruct((s, d), F32), jax.ShapeDtypeStruct((1, 1), F32)],
        scratch_shapes=[pltpu.VMEM((1, d), F32)],
        compiler_params=_cparams(), name=name,
    )(xh, g, b, target)


def _elementwise(fn, ins, out_dtypes, name):
    shape = ins[0].shape
    c = shape[-1]
    r = math.prod(shape[:-1])
    tr = _pick_rows(r, c, 4, 1 << 20)

    def body(*refs):
        vals = fn(*[ref[...] for ref in refs[:len(ins)]])
        for ref, v in zip(refs[len(ins):], vals):
            ref[...] = v.astype(ref.dtype)

    spec = pl.BlockSpec((tr, c), lambda i: (i, 0))
    outs = pl.pallas_call(
        body, grid=(r // tr,),
        in_specs=[spec] * len(ins), out_specs=[spec] * len(out_dtypes),
        out_shape=[jax.ShapeDtypeStruct((r, c), dt) for dt in out_dtypes],
        compiler_params=_cparams(parallel=True), name=name,
    )(*[a.reshape(r, c) for a in ins])
    return [o.reshape(shape) for o in outs]


def _prefetch_call(body, grid, in_specs, out_specs, out_shape, name):
    return pl.pallas_call(
        body,
        grid_spec=pltpu.PrefetchScalarGridSpec(num_scalar_prefetch=1, grid=grid, in_specs=in_specs, out_specs=out_specs),
        out_shape=out_shape,
        compiler_params=_cparams(len(grid), parallel=True), name=name)


def _cast_into_full(w3, kind, chip1, name):
    l, r, c = w3.shape
    tr = _pick_rows(r, c, 4, 1 << 20)

    def body(k_ref, w_ref, o_ref):
        o_ref[...] = w_ref[...].astype(BF16)

    if kind == "row":
        out_spec = pl.BlockSpec((None, None, tr, c), lambda a, j, k: (a, k[0], j, 0))
    else:
        out_spec = pl.BlockSpec((None, tr, c), lambda a, j, k: (a, j, k[0]))
    return _prefetch_call(
        body, (l, r // tr), [pl.BlockSpec((None, tr, c), lambda a, j, k: (a, j, 0))], out_spec,
        jax.ShapeDtypeStruct(_full_shape(kind, w3.shape), BF16), name)(chip1, w3)


def _pair_sum(g, got, kind, core1, name):
    if kind == "row":
        l, nc, sr, c = g.shape
        g5, got3 = g.reshape(l * nc, 2, sr // 2, c), got.reshape(l * nc, sr // 2, c)
    else:
        l, r, c = g.shape
        g5, got3 = g.reshape(l, 2, r // 2, c), got
    a, _, hr, c = g5.shape
    tr = _pick_rows(hr, c, 4, 1 << 20)

    def body(c_ref, g_ref, t_ref, o_ref):
        o_ref[...] = (g_ref[...] + t_ref[...]).astype(BF16)

    half = pl.BlockSpec((None, tr, c), lambda i, j, cc: (i, j, 0))
    out = _prefetch_call(
        body, (a, hr // tr), [pl.BlockSpec((None, None, tr, c), lambda i, j, cc: (i, cc[0], j, 0)), half], half,
        jax.ShapeDtypeStruct(got3.shape, BF16), name)(core1, g5, got3)
    return out.reshape(got.shape)


def _chip_sum(t, rb, kind, chip1, name):
    _, l, hr, sc = rb.shape
    tr = _pick_rows(hr, sc, 4, 1 << 19)

    def body(k_ref, t_ref, rb_ref, o_ref):
        acc = t_ref[...].astype(F32)
        for r in range(N_CHIPS - 1):
            acc = acc + rb_ref[r].astype(F32)
        o_ref[...] = acc

    if kind == "row":
        t_spec = pl.BlockSpec((None, None, tr, sc), lambda a, j, k: (a, k[0], j, 0))
    else:
        t_spec = pl.BlockSpec((None, tr, sc), lambda a, j, k: (a, j, k[0]))
    return _prefetch_call(
        body, (l, hr // tr),
        [t_spec, pl.BlockSpec((N_CHIPS - 1, None, tr, sc), lambda a, j, k: (0, a, j, 0))],
        pl.BlockSpec((None, tr, sc), lambda a, j, k: (a, j, 0)),
        jax.ShapeDtypeStruct((l, hr, sc), F32), name)(chip1, t, rb)


def _adamw_math(w_, g_, m_, v_):
    m2 = ADAM_B1 * m_ + (1.0 - ADAM_B1) * g_
    v2 = ADAM_B2 * v_ + (1.0 - ADAM_B2) * (g_ * g_)
    m_hat = m2 / (1.0 - ADAM_B1 ** ADAM_STEP)
    v_hat = v2 / (1.0 - ADAM_B2 ** ADAM_STEP)
    delta = -ADAM_LR * (m_hat / (jnp.sqrt(v_hat) + ADAM_EPS) + ADAM_WD * w_)
    return delta, m2, v2


def _adamw_big(w, m, v, own, other, core1, name):
    l, hr, c = own.shape
    view = lambda a: a.reshape(l, 2, hr, c)
    tr = _pick_rows(hr, c, 4, 1 << 20)

    def body(c_ref, w_ref, m_ref, v_ref, own_ref, oth_ref, g_ref, d_ref, m2_ref, v2_ref):
        g = jnp.where(pl.program_id(1) == c_ref[0], own_ref[...], oth_ref[...])
        g_ref[...] = g
        d_ref[...], m2_ref[...], v2_ref[...] = _adamw_math(w_ref[...], g, m_ref[...], v_ref[...])

    s4 = pl.BlockSpec((None, None, tr, c), lambda a, h, j, cc: (a, h, j, 0))
    s3 = pl.BlockSpec((None, tr, c), lambda a, h, j, cc: (a, j, 0))
    outs = _prefetch_call(
        body, (l, 2, hr // tr), [s4, s4, s4, s3, s3], [s4] * 4,
        [jax.ShapeDtypeStruct((l, 2, hr, c), F32)] * 4, name)(core1, view(w), view(m), view(v), own, other)
    return [o.reshape(w.shape) for o in outs]


def _sum_devices(own, gathered, me1, name):
    r, c = own.shape
    tr = _pick_rows(r, c, 4, 1 << 17)

    def body(me_ref, own_ref, g_ref, o_ref):
        acc = None
        for k in range(N_DEV):
            v = jnp.where(me_ref[0] == k, own_ref[...], g_ref[k])
            acc = v if acc is None else acc + v
        o_ref[...] = acc

    return _prefetch_call(
        body, (r // tr,),
        [pl.BlockSpec((tr, c), lambda i, m: (i, 0)), pl.BlockSpec((N_DEV, tr, c), lambda i, m: (0, i, 0))],
        pl.BlockSpec((tr, c), lambda i, m: (i, 0)), jax.ShapeDtypeStruct((r, c), F32), name)(me1, own, gathered)


def _adamw(w, g, m, v, name):
    return _elementwise(_adamw_math, [w, g, m, v], [F32, F32, F32], name)


ANY = pl.BlockSpec(memory_space=pl.ANY)


def _mesh_pos():
    return lax.axis_index("x"), lax.axis_index("y"), lax.axis_index("c")


def _chip_peers(x, y, c):
    out = []
    for r in (1, 2, 3):
        px = 1 - x if r & 2 else x
        py = 1 - y if r & 1 else y
        out.append((2 * px + py, (px, py, c)))
    return out


def _full_shape(kind, shard_shape):
    l, r, c = shard_shape
    return (l, N_CHIPS, r, c) if kind == "row" else (l, r, N_CHIPS * c)


def _full_piece(ref, kind, k, h, hr, sc):
    rows = pl.ds(pl.multiple_of(h * hr, SUBLANES_BF16), hr)
    if kind == "row":
        return ref.at[:, k, rows, :]
    return ref.at[:, rows, pl.ds(pl.multiple_of(k * sc, LANES), sc)]


def _remote(src, dst, ssem, rsem, dev):
    return pltpu.make_async_remote_copy(src_ref=src, dst_ref=dst, send_sem=ssem, recv_sem=rsem,
                                        device_id=dev, device_id_type=MESH)


DMA_CHUNK_BYTES = 1 << 20
DMA_MAX_CHUNKS = 32


def _chunk_views(src, dst):
    axis = len(src.shape) - 2
    rows = src.shape[axis]
    nbytes = math.prod(src.shape) * jnp.dtype(src.dtype).itemsize
    n = max(1, min(DMA_MAX_CHUNKS, nbytes // DMA_CHUNK_BYTES))
    while n > 1 and (rows % n or (rows // n) % SUBLANES_BF16):
        n -= 1
    cr = rows // n
    out = []
    for i in range(n):
        idx = (slice(None),) * axis + (pl.ds(i * cr, cr), slice(None))
        out.append((src.at[idx], dst.at[idx]))
    return out


def _start_remote(src, dst, ssem, rsem, dev):
    for s, t in _chunk_views(src, dst):
        _remote(s, t, ssem, rsem, dev).start()
    return _remote(src, dst, ssem, rsem, dev)


def _allgather_weights(fulls, kinds):
    nw = len(fulls)

    def dims(a, kind):
        return (a.shape[2] // 2, a.shape[3]) if kind == "row" else (a.shape[1] // 2, a.shape[2] // N_CHIPS)

    hrs = [dims(a, k)[0] for a, k in zip(fulls, kinds)]
    scs = [dims(a, k)[1] for a, k in zip(fulls, kinds)]

    def body(*refs):
        mine, fu = refs[:nw], refs[nw:2 * nw]
        send1, recv1, send2, recv2 = refs[2 * nw:]
        x, y, c = _mesh_pos()
        k_me = 2 * x + y
        sibling = (x, y, 1 - c)
        peers = _chip_peers(x, y, c)

        def piece(ref, w, k, h):
            return _full_piece(ref, kinds[w], k, h, hrs[w], scs[w])

        sends = []
        for w in range(nw):
            for r, (_, dev) in enumerate(peers):
                sends.append(_start_remote(piece(mine[w], w, k_me, c), piece(fu[w], w, k_me, c),
                                           send1.at[3 * w + r], recv1.at[3 * w + r], dev))
        for w in range(nw):
            for r, (kj, dev) in enumerate(peers):
                _remote(piece(mine[w], w, k_me, c), piece(fu[w], w, kj, c),
                        send1.at[3 * w + r], recv1.at[3 * w + r], dev).wait_recv()
                sends.append(_start_remote(piece(fu[w], w, kj, c), piece(fu[w], w, kj, c),
                                           send2.at[3 * w + r], recv2.at[3 * w + r], sibling))
        for w in range(nw):
            for r, (kj, _) in enumerate(peers):
                _remote(piece(fu[w], w, kj, 1 - c), piece(fu[w], w, kj, 1 - c),
                        send2.at[3 * w + r], recv2.at[3 * w + r], sibling).wait_recv()
        for cp in sends:
            cp.wait_send()

    return pl.pallas_call(
        body,
        in_specs=[ANY] * nw, out_specs=[ANY] * nw,
        out_shape=[jax.ShapeDtypeStruct(a.shape, a.dtype) for a in fulls],
        scratch_shapes=[pltpu.SemaphoreType.DMA((3 * nw,))] * 4,
        input_output_aliases={w: w for w in range(nw)},
        name="allgather_weights",
    )(*fulls)


def _rs_pair(fulls, kinds):
    nw = len(fulls)

    def half_all(ref, kind, h):
        if kind == "row":
            hr = ref.shape[2] // 2
            return ref.at[:, :, pl.ds(pl.multiple_of(h * hr, SUBLANES_BF16), hr), :]
        hr = ref.shape[1] // 2
        return ref.at[:, pl.ds(pl.multiple_of(h * hr, SUBLANES_BF16), hr), :]

    def half_shape(kind, shape):
        if kind == "row":
            return (shape[0], shape[1], shape[2] // 2, shape[3])
        return (shape[0], shape[1] // 2, shape[2])

    def body(*refs):
        g, got = refs[:nw], refs[nw:2 * nw]
        ssem, rsem = refs[2 * nw:]
        x, y, c = _mesh_pos()
        sibling = (x, y, 1 - c)
        cps = [_start_remote(half_all(g[w], kinds[w], 1 - c), got[w], ssem.at[w], rsem.at[w], sibling)
               for w in range(nw)]
        for cp in cps:
            cp.wait_recv()
        for cp in cps:
            cp.wait_send()

    shapes = [jax.ShapeDtypeStruct(half_shape(k, a.shape), a.dtype) for k, a in zip(kinds, fulls)]
    return pl.pallas_call(
        body, in_specs=[ANY] * nw, out_specs=[ANY] * nw, out_shape=shapes,
        scratch_shapes=[pltpu.SemaphoreType.DMA((nw,))] * 2, name="rs_pair",
    )(*fulls)


def _rs_chips(parts, kinds):
    nw = len(parts)

    def slot(ref, kind, k):
        if kind == "row":
            return ref.at[:, k]
        sc = ref.shape[2] // N_CHIPS
        return ref.at[:, :, pl.ds(pl.multiple_of(k * sc, LANES), sc)]

    def slot_shape(kind, shape):
        if kind == "row":
            return (shape[0], shape[2], shape[3])
        return (shape[0], shape[1], shape[2] // N_CHIPS)

    def body(*refs):
        t, rb = refs[:nw], refs[nw:2 * nw]
        ssem, rsem = refs[2 * nw:]
        x, y, c = _mesh_pos()
        cps = []
        for w in range(nw):
            for r, (kj, dev) in enumerate(_chip_peers(x, y, c)):
                cps.append(_start_remote(slot(t[w], kinds[w], kj), rb[w].at[r],
                                         ssem.at[3 * w + r], rsem.at[3 * w + r], dev))
        for cp in cps:
            cp.wait_recv()
        for cp in cps:
            cp.wait_send()

    shapes = [jax.ShapeDtypeStruct((N_CHIPS - 1,) + slot_shape(k, a.shape), a.dtype) for k, a in zip(kinds, parts)]
    return pl.pallas_call(
        body, in_specs=[ANY] * nw, out_specs=[ANY] * nw, out_shape=shapes,
        scratch_shapes=[pltpu.SemaphoreType.DMA((3 * nw,))] * 2, name="rs_chips",
    )(*parts)


def _rs_join(halves):
    nw = len(halves)

    def body(*refs):
        src, dst = refs[:nw], refs[nw:2 * nw]
        ssem, rsem = refs[2 * nw:]
        x, y, c = _mesh_pos()
        cps = [_start_remote(src[w], dst[w], ssem.at[w], rsem.at[w], (x, y, 1 - c)) for w in range(nw)]
        for cp in cps:
            cp.wait_recv()
        for cp in cps:
            cp.wait_send()

    return pl.pallas_call(
        body, in_specs=[ANY] * nw, out_specs=[ANY] * nw,
        out_shape=[jax.ShapeDtypeStruct(a.shape, a.dtype) for a in halves],
        scratch_shapes=[pltpu.SemaphoreType.DMA((nw,))] * 2, name="rs_join",
    )(*halves)


def _allgather_small(buf, name):
    def body(in_ref, out_ref, ssem, rsem):
        x, y, c = _mesh_pos()
        me = 4 * x + 2 * y + c
        cps, waits = [], []
        for r in range(1, N_DEV):
            px = 1 - x if r & 4 else x
            py = 1 - y if r & 2 else y
            pc = 1 - c if r & 1 else c
            cp = _remote(in_ref, out_ref.at[me], ssem.at[r - 1], rsem.at[r - 1], (px, py, pc))
            cp.start()
            cps.append(cp)
            waits.append(_remote(in_ref, out_ref.at[4 * px + 2 * py + pc], ssem.at[r - 1], rsem.at[r - 1], (px, py, pc)))
        for wt in waits:
            wt.wait_recv()
        for cp in cps:
            cp.wait_send()

    return pl.pallas_call(
        body, in_specs=[ANY], out_specs=ANY,
        out_shape=jax.ShapeDtypeStruct((N_DEV,) + buf.shape, buf.dtype),
        scratch_shapes=[pltpu.SemaphoreType.DMA((N_DEV - 1,))] * 2,
        name=name,
    )(buf)


def _pack(arrs):
    flat = jnp.concatenate([a.reshape(-1).astype(F32) for a in arrs])
    rows = -(-flat.shape[0] // (LANES * 16)) * 16
    return jnp.pad(flat, (0, rows * LANES - flat.shape[0])).reshape(rows, LANES)


def _unpack(buf, shapes):
    flat = buf.reshape(-1)
    out, off = [], 0
    for shp in shapes:
        nel = math.prod(shp)
        out.append(flat[off:off + nel].reshape(shp))
        off += nel
    return out


BIG = ("a_w_in", "a_w_out", "b_w_in", "b_w_out", "c_w_in", "c_w_grp", "c_w_out", "f_w_up", "f_w_down")
BIG_KIND = {"a_w_in": "col", "a_w_out": "row", "b_w_in": "col", "b_w_out": "row", "c_w_in": "row",
            "c_w_grp": "row", "c_w_out": "row", "f_w_up": "col", "f_w_down": "row"}
SHARDED_SMALL = ("a_dw", "a_dw_b", "a_ln_g", "a_ln_b", "c_scale", "f_dw")
REPLICATED = ("b_ln_g", "b_ln_b", "b_ws", "b_bs", "ln1_g", "ln1_b", "ln2_g", "ln2_b")
WEIGHTS = ("a_w_in", "a_dw", "a_dw_b", "a_ln_g", "a_ln_b", "a_w_out", "b_w_in", "b_ln_g", "b_ln_b", "b_ws", "b_bs",
           "b_w_out", "c_w_in", "c_w_grp", "c_scale", "c_w_out", "f_w_up", "f_dw", "f_w_down",
           "ln1_g", "ln1_b", "ln2_g", "ln2_b")


def _as3d(a):
    return a.reshape((-1,) + a.shape[-2:])


def kernel(x, a_w_in, a_dw, a_dw_b, a_ln_g, a_ln_b, a_w_out, b_w_in, b_ln_g, b_ln_b, b_ws, b_bs, b_w_out, c_w_in, c_w_grp, c_scale, c_w_out, f_w_up, f_dw, f_w_down, ln1_g, ln1_b, ln2_g, ln2_b, loss_target, m_a_w_in, m_a_dw, m_a_dw_b, m_a_ln_g, m_a_ln_b, m_a_w_out, m_b_w_in, m_b_ln_g, m_b_ln_b, m_b_ws, m_b_bs, m_b_w_out, m_c_w_in, m_c_w_grp, m_c_scale, m_c_w_out, m_f_w_up, m_f_dw, m_f_w_down, m_ln1_g, m_ln1_b, m_ln2_g, m_ln2_b, v_a_w_in, v_a_dw, v_a_dw_b, v_a_ln_g, v_a_ln_b, v_a_w_out, v_b_w_in, v_b_ln_g, v_b_ln_b, v_b_ws, v_b_bs, v_b_w_out, v_c_w_in, v_c_w_grp, v_c_scale, v_c_w_out, v_f_w_up, v_f_dw, v_f_w_down, v_ln1_g, v_ln1_b, v_ln2_g, v_ln2_b):
    w = dict(a_w_in=a_w_in, a_dw=a_dw, a_dw_b=a_dw_b, a_ln_g=a_ln_g, a_ln_b=a_ln_b, a_w_out=a_w_out, b_w_in=b_w_in, b_ln_g=b_ln_g, b_ln_b=b_ln_b, b_ws=b_ws, b_bs=b_bs, b_w_out=b_w_out, c_w_in=c_w_in, c_w_grp=c_w_grp, c_scale=c_scale, c_w_out=c_w_out, f_w_up=f_w_up, f_dw=f_dw, f_w_down=f_w_down, ln1_g=ln1_g, ln1_b=ln1_b, ln2_g=ln2_g, ln2_b=ln2_b)
    mom = dict(a_w_in=m_a_w_in, a_dw=m_a_dw, a_dw_b=m_a_dw_b, a_ln_g=m_a_ln_g, a_ln_b=m_a_ln_b, a_w_out=m_a_w_out, b_w_in=m_b_w_in, b_ln_g=m_b_ln_g, b_ln_b=m_b_ln_b, b_ws=m_b_ws, b_bs=m_b_bs, b_w_out=m_b_w_out, c_w_in=m_c_w_in, c_w_grp=m_c_w_grp, c_scale=m_c_scale, c_w_out=m_c_w_out, f_w_up=m_f_w_up, f_dw=m_f_dw, f_w_down=m_f_w_down, ln1_g=m_ln1_g, ln1_b=m_ln1_b, ln2_g=m_ln2_g, ln2_b=m_ln2_b)
    var = dict(a_w_in=v_a_w_in, a_dw=v_a_dw, a_dw_b=v_a_dw_b, a_ln_g=v_a_ln_g, a_ln_b=v_a_ln_b, a_w_out=v_a_w_out, b_w_in=v_b_w_in, b_ln_g=v_b_ln_g, b_ln_b=v_b_ln_b, b_ws=v_b_ws, b_bs=v_b_bs, b_w_out=v_b_w_out, c_w_in=v_c_w_in, c_w_grp=v_c_w_grp, c_scale=v_c_scale, c_w_out=v_c_w_out, f_w_up=v_f_w_up, f_dw=v_f_dw, f_w_down=v_f_w_down, ln1_g=v_ln1_g, ln1_b=v_ln1_b, ln2_g=v_ln2_g, ln2_b=v_ln2_b)

    depth = ln1_g.shape[0]
    d = x.shape[-1]
    alpha = float((2 * depth) ** 0.25)
    chip = 2 * lax.axis_index("x") + lax.axis_index("y")
    chip1 = chip.astype(jnp.int32).reshape(1)
    core1 = lax.axis_index("c").astype(jnp.int32).reshape(1)

    kinds = [BIG_KIND[k] for k in BIG]
    mine = [_cast_into_full(_as3d(w[k]), kind, chip1, "cast_" + k) for k, kind in zip(BIG, kinds)]
    gathered = _allgather_weights(mine, kinds)
    full = {}
    for k, kind, arr in zip(BIG, kinds, gathered):
        full[k] = arr.reshape(arr.shape[0], -1, arr.shape[-1]) if kind == "row" else arr

    small_all = _allgather_small(_pack([w[k] for k in SHARDED_SMALL]), "allgather_small_params")
    other_core = 1 - lax.axis_index("c")
    per_chip = [_unpack(lax.dynamic_index_in_dim(small_all, 2 * k + other_core, keepdims=False),
                        [w[n].shape for n in SHARDED_SMALL]) for k in range(N_CHIPS)]
    fs = {n: jnp.concatenate([per_chip[k][i] for k in range(N_CHIPS)], axis=-1) for i, n in enumerate(SHARDED_SMALL)}

    nh = b_ws.shape[1]
    tril = jnp.tril(jnp.ones((CHUNK, CHUNK), F32))
    wm = (b_ws[0] * tril).astype(BF16)
    wmt = jnp.swapaxes(wm, 1, 2)
    bs_exp = jnp.repeat(jnp.transpose(b_bs[0]), CHUNK, axis=1)
    ng = c_w_grp.shape[1]
    wgrp = full["c_w_grp"]

    xh, g, b = x[0], jnp.ones((1, d), F32), jnp.zeros((1, d), F32)
    saved = []
    for i in range(depth):
        kind, j = i % 3, i // 3
        rec = dict(xin=xh, gin=g, bin=b)
        if kind == 0:
            xh1, rstd1, p, chat, rstdc = _conv_fwd(
                xh, g, b, full["a_w_in"], j, fs["a_dw"], fs["a_dw_b"][j:j + 1], fs["a_ln_g"][j:j + 1],
                fs["a_ln_b"][j:j + 1], full["a_w_out"], alpha, f"conv_fwd_{i}")
            rec.update(p=p, chat=chat, rstdc=rstdc)
        elif kind == 1:
            xh1, rstd1, zp = _sgu_fwd(xh, g, b, full["b_w_in"], b_ln_g, b_ln_b, wm, bs_exp, full["b_w_out"],
                                      alpha, f"sgu_fwd_{i}")
            rec.update(zp=zp)
        else:
            xh1, rstd1, ys = _pool_fwd(xh, g, b, full["c_w_in"], wgrp, fs["c_scale"], full["c_w_out"],
                                       alpha, f"pool_fwd_{i}")
            rec.update(ys=ys)
        xh2, rstd2, hs, hcs = _ffn_fwd(xh1, ln1_g[i:i + 1], ln1_b[i:i + 1], full["f_w_up"], i, fs["f_dw"],
                                       full["f_w_down"], alpha, f"ffn_fwd_{i}")
        rec.update(xh1=xh1, rstd1=rstd1, xh2=xh2, rstd2=rstd2, hs=hs, hcs=hcs)
        saved.append(rec)
        xh, g, b = xh2, ln2_g[i:i + 1], ln2_b[i:i + 1]

    dxo, loss_part = _loss_head(xh, g, b, loss_target[0], "loss_head")
    loss = lax.psum(loss_part[0, 0], ("x", "y", "c"))

    gbuf = {k: lax.empty(full[k].shape, F32) for k in BIG if k != "c_w_grp"}
    gs = {k: [None] * w[k].shape[0] for k in ("a_dw", "a_dw_b", "a_ln_g", "a_ln_b", "f_dw", "ln1_g", "ln1_b", "ln2_g", "ln2_b")}
    for i in reversed(range(depth)):
        kind, j = i % 3, i // 3
        rec = saved[i]
        dr2, dhc, gbuf["f_w_down"], gs["ln2_g"][i], gs["ln2_b"][i] = _ffn_bwd1(
            dxo, rec["xh2"], rec["rstd2"], ln2_g[i:i + 1], rec["hcs"], full["f_w_down"], i,
            gbuf["f_w_down"], f"ffn_bwd1_{i}")
        dr1, dh, gs["f_dw"][i], gs["ln1_g"][i], gs["ln1_b"][i] = _ffn_bwd2(
            dhc, rec["hs"], dr2, full["f_w_up"], i, fs["f_dw"], rec["xh1"], rec["rstd1"], ln1_g[i:i + 1],
            alpha, f"ffn_bwd2_{i}")
        gbuf["f_w_up"] = _mm_tn(rec["xh1"], ln1_g[i:i + 1], ln1_b[i:i + 1], dh, gbuf["f_w_up"], i, f"grad_w_up_{i}")
        if kind == 0:
            dp, gbuf["a_w_out"], gs["a_dw"][j], gs["a_dw_b"][j], gs["a_ln_g"][j], gs["a_ln_b"][j] = _conv_bwd1(
                dr1, rec["chat"], rec["rstdc"], rec["p"], full["a_w_out"], j, fs["a_dw"], fs["a_ln_g"][j:j + 1],
                fs["a_ln_b"][j:j + 1], gbuf["a_w_out"], f"conv_bwd1_{i}")
            win_name, lidx = "a_w_in", j
        elif kind == 1:
            dp, gbuf["b_w_out"], g_ws, g_bs_t, g_blg, g_blb = _sgu_bwd1(
                dr1, rec["zp"], full["b_w_out"], b_ln_g, b_ln_b, wm, wmt, bs_exp, gbuf["b_w_out"], f"sgu_bwd1_{i}")
            win_name, lidx = "b_w_in", 0
        else:
            dp, gbuf["c_w_out"], g_wgrp, g_cscale = _pool_bwd1(
                dr1, rec["ys"], full["c_w_out"], wgrp, fs["c_scale"], gbuf["c_w_out"], f"pool_bwd1_{i}")
            win_name, lidx = "c_w_in", 0
        dxo = _bwd_in(dp, dr1, full[win_name], lidx, alpha, f"mixer_bwd2_{i}")
        gbuf[win_name] = _mm_tn(rec["xin"], rec["gin"], rec["bin"], dp, gbuf[win_name], lidx, f"grad_w_in_{i}")
    grad_x = dxo[None]

    gfull = []
    for k, kind in zip(BIG, kinds):
        a = g_wgrp if k == "c_w_grp" else gbuf[k]
        gfull.append(a.reshape(a.shape[0], N_CHIPS, -1, a.shape[-1]) if kind == "row" else a)
    got = _rs_pair(gfull, kinds)
    pair_sum = [_pair_sum(a, t, kind, core1, "rs_pair_sum_" + k) for k, kind, a, t in zip(BIG, kinds, gfull, got)]
    from_chips = _rs_chips(pair_sum, kinds)
    half_sum = [_chip_sum(t, rb, kind, chip1, "rs_chip_sum_" + k)
                for k, kind, t, rb in zip(BIG, kinds, pair_sum, from_chips)]
    other_half = _rs_join(half_sum)
    grads, delta, new_m, new_v = {}, {}, {}, {}
    for k, own, oth in zip(BIG, half_sum, other_half):
        outs = _adamw_big(_as3d(w[k]), _as3d(mom[k]), _as3d(var[k]), own, oth, core1, "adamw_" + k)
        grads[k], delta[k], new_m[k], new_v[k] = [o.reshape(w[k].shape) for o in outs]

    small_full = {
        "a_dw": jnp.stack(gs["a_dw"]), "a_dw_b": jnp.concatenate(gs["a_dw_b"]), "a_ln_g": jnp.concatenate(gs["a_ln_g"]),
        "a_ln_b": jnp.concatenate(gs["a_ln_b"]), "c_scale": g_cscale, "f_dw": jnp.stack(gs["f_dw"]),
        "b_ln_g": g_blg, "b_ln_b": g_blb, "b_ws": g_ws[None], "b_bs": jnp.transpose(g_bs_t)[None],
        "ln1_g": jnp.concatenate(gs["ln1_g"]), "ln1_b": jnp.concatenate(gs["ln1_b"]),
        "ln2_g": jnp.concatenate(gs["ln2_g"]), "ln2_b": jnp.concatenate(gs["ln2_b"]),
    }
    small_names = SHARDED_SMALL + REPLICATED
    small_shapes = [small_full[n].shape for n in small_names]
    small_packed = _pack([small_full[n] for n in small_names])
    gathered_small = _allgather_small(small_packed, "allgather_small_grads")
    me1 = (2 * chip + lax.axis_index("c")).astype(jnp.int32).reshape(1)
    summed = _unpack(_sum_devices(small_packed, gathered_small, me1, "small_grad_sum"), small_shapes)
    for n, a in zip(small_names, summed):
        if n in SHARDED_SMALL:
            cs = w[n].shape[-1]
            a = lax.dynamic_slice_in_dim(a, chip * cs, cs, axis=a.ndim - 1)
        grads[n] = a

    shapes = [w[n].shape for n in small_names]
    ds_, ms_, vs_ = _adamw(_pack([w[n] for n in small_names]), _pack([grads[n] for n in small_names]),
                           _pack([mom[n] for n in small_names]), _pack([var[n] for n in small_names]), "adamw_small")
    for n, a, bb, cc in zip(small_names, _unpack(ds_, shapes), _unpack(ms_, shapes), _unpack(vs_, shapes)):
        delta[n], new_m[n], new_v[n] = a, bb, cc

    return (loss, grad_x, *[grads[n] for n in WEIGHTS], *[delta[n] for n in WEIGHTS],
            *[new_m[n] for n in WEIGHTS], *[new_v[n] for n in WEIGHTS])
```

```python
import math

import jax
import jax.numpy as jnp
from jax import lax
from jax.experimental import pallas as pl
from jax.experimental.pallas import tpu as pltpu

F32 = jnp.float32
BF16 = jnp.bfloat16

LN_EPS = 1e-5
POOL_WINDOWS = (2, 4, 8, 16)
CHUNK = 128
ADAM_LR = 0.001
ADAM_B1 = 0.9
ADAM_B2 = 0.999
ADAM_EPS = 1e-08
ADAM_WD = 0.01
ADAM_STEP = 10

LANES = 128
SUBLANES_BF16 = 16
N_CHIPS = 4
N_DEV = 8
VMEM_LIMIT = 60 * 1024 * 1024

TM_FFN = 512
TM_FFN_BWD1 = 256
TM_CONV = 256
TM_SGU = 512
TM_POOL = 512
TM_BWD_IN = 512
TS_MM_TN = 1024
CW_FFN = 256
CW_FFN_BWD2 = 512
CONV_HALO = 32
CONV_ROW_BLOCK = 64
POOL_HALO = 16
FFN_HALO = 16

MESH = pl.DeviceIdType.MESH


def _cparams(n_grid=1, parallel=False):
    sem = ("parallel" if parallel else "arbitrary",) * n_grid
    return pltpu.CompilerParams(dimension_semantics=sem, vmem_limit_bytes=VMEM_LIMIT)


def _resident(block, imap):
    return pl.BlockSpec(block, imap, pipeline_mode=pl.Buffered(1))


def _wspec(w, l):
    _, r, c = w.shape
    return _resident((None, r, c), lambda *_: (l, 0, 0))


def _rowspec(d):
    return pl.BlockSpec((1, d), lambda *_: (0, 0))


def _dot(a, b):
    return jnp.dot(a, b, preferred_element_type=F32)


def _dot_nt(a, b):
    return lax.dot_general(a, b, (((1,), (1,)), ((), ())), preferred_element_type=F32)


def _dot_tn(a, b):
    return lax.dot_general(a, b, (((0,), (0,)), ((), ())), preferred_element_type=F32)


def _sigmoid(x):
    return jax.nn.sigmoid(x)


def _ln_stats(r):
    mu = jnp.mean(r, axis=1, keepdims=True)
    xc = r - mu
    var = jnp.mean(xc * xc, axis=1, keepdims=True)
    rstd = lax.rsqrt(var + LN_EPS)
    return xc * rstd, rstd


def _ln_bwd(dy, xhat, rstd, g):
    dxh = dy * g
    m1 = jnp.mean(dxh, axis=1, keepdims=True)
    m2 = jnp.mean(dxh * xhat, axis=1, keepdims=True)
    return rstd * (dxh - m1 - xhat * m2)


def _colsum(v):
    return jnp.sum(v, axis=0, keepdims=True)


def _gelu(z):
    return 0.5 * z * (1.0 + lax.erf(z * (1.0 / math.sqrt(2.0))))


def _gelu_grad(z):
    cdf = 0.5 * (1.0 + lax.erf(z * (1.0 / math.sqrt(2.0))))
    pdf = jnp.exp(-0.5 * z * z) * (1.0 / math.sqrt(2.0 * math.pi))
    return cdf + z * pdf


def _shift_down(v, k, prev_rows):
    rolled = pltpu.roll(v, k, 0)
    head = rolled[0:8]
    rows = lax.broadcasted_iota(jnp.int32, head.shape, 0)
    for r in range(k):
        head = jnp.where(rows == r, prev_rows[k - 1 - r], head)
    return jnp.concatenate([head, rolled[8:]], axis=0)


def _shift_up(v, k, next_rows):
    tm = v.shape[0]
    rolled = pltpu.roll(v, tm - k, 0)
    tail = rolled[tm - 8:tm]
    rows = lax.broadcasted_iota(jnp.int32, tail.shape, 0)
    for r in range(k):
        tail = jnp.where(rows == 8 - k + r, next_rows[r], tail)
    return jnp.concatenate([rolled[0:tm - 8], tail], axis=0)


def _fill_shifted(base_scr, sh_scr):
    nrows = sh_scr.shape[1]
    for r in range(1, 8):
        sh_scr[r - 1, :, :] = base_scr[pl.ds(r, nrows), :]


def _tap(base_scr, sh_scr, off, r0, nrows, cols):
    q, r = divmod(off, 8)
    if r == 0:
        return base_scr[pl.ds(r0 + 8 * q, nrows), cols]
    return sh_scr[r - 1, pl.ds(r0 + 8 * q, nrows), cols]


def _pick_rows(r, c, itemsize, cap_bytes):
    best = None
    for t in range(16, r + 1, 16):
        if r % t == 0 and t * c * itemsize <= cap_bytes:
            best = t
    return best if best is not None else r


def _ffn_conv_cols(h, dw_ref, c0, cw, prev1, prev2):
    kw = dw_ref.shape[0]
    h1 = _shift_down(h, 1, [prev1])
    h2 = _shift_down(h, 2, [prev1, prev2])
    hc = dw_ref[kw - 1:kw, c0:c0 + cw] * h + dw_ref[kw - 2:kw - 1, c0:c0 + cw] * h1 + dw_ref[kw - 3:kw - 2, c0:c0 + cw] * h2
    return hc, h1, h2


def _ffn_fwd(xh1, g1, b1, wup, wdn, lw, fdw, l, alpha, name, hook=None):
    s, d = xh1.shape
    f2 = wup.shape[2]
    f = f2 // 2
    tm = min(TM_FFN, s)
    cw = min(CW_FFN, f)
    n, nck = s // tm, f // cw
    assert fdw.shape[1] == 3 and s % tm == 0 and f % cw == 0

    def body(xh_ref, g_ref, b_ref, wup_ref, dw_ref, wdn_ref, xo_ref, rs_ref, hs_ref, hcs_ref, carry):
        @pl.when(pl.program_id(0) == 0)
        def _():
            carry[...] = jnp.zeros_like(carry)

        x1 = xh_ref[...] * g_ref[...] + b_ref[...]
        xb = x1.astype(BF16)
        o = jnp.zeros((tm, d), F32)

        def up_proj(j):
            return [_dot(xb, wup_ref[:, half * f + j * cw:half * f + (j + 1) * cw]) for half in range(2)]

        ahead = up_proj(0)
        for j in range(nck):
            hh = ahead
            if j + 1 < nck:
                ahead = up_proj(j + 1)
            parts = []
            for half in range(2):
                c0 = half * f + j * cw
                h = hh[half]
                hs_ref[:, c0:c0 + cw] = h.astype(BF16)
                hc, _, _ = _ffn_conv_cols(h, dw_ref, c0, cw, carry[7:8, c0:c0 + cw], carry[6:7, c0:c0 + cw])
                carry[:, c0:c0 + cw] = h[tm - 8:tm, :]
                hcs_ref[:, c0:c0 + cw] = hc.astype(BF16)
                parts.append(hc)
            gg, vv = parts
            a = (gg * _sigmoid(gg) * vv).astype(BF16)
            o = o + _dot(a, wdn_ref[j * cw:(j + 1) * cw, :])
        xhat, rstd = _ln_stats(alpha * x1 + o)
        xo_ref[...] = xhat
        rs_ref[...] = rstd

    tile = pl.BlockSpec((tm, d), lambda i: (i, 0))
    return _hosted_call(
        body, hook, n,
        in_specs=[tile, _rowspec(d), _rowspec(d), _wspec(wup, lw),
                  pl.BlockSpec((None, 3, f2), lambda i: (l, 0, 0)), _wspec(wdn, lw)],
        out_specs=[tile, pl.BlockSpec((tm, 1), lambda i: (i, 0)), pl.BlockSpec((tm, f2), lambda i: (i, 0)),
                   pl.BlockSpec((tm, f2), lambda i: (i, 0))],
        out_shape=[jax.ShapeDtypeStruct((s, d), F32), jax.ShapeDtypeStruct((s, 1), F32),
                   jax.ShapeDtypeStruct((s, f2), BF16), jax.ShapeDtypeStruct((s, f2), BF16)],
        scratch_shapes=[pltpu.VMEM((8, f2), F32)],
        args=(xh1, g1, b1, wup, fdw, wdn), name=name)


def _ffn_bwd1(dx2, xh2, rstd2, g2, hcs, wdn, lw, l, gwdn_buf, name):
    s, d = dx2.shape
    f2 = hcs.shape[1]
    f = f2 // 2
    tm = min(TM_FFN_BWD1, s)
    cw = min(CW_FFN, f)
    n, nck = s // tm, f // cw

    def body(dx_ref, xh_ref, rs_ref, g_ref, hcs_ref, wdn_ref, buf_ref,
             dr_ref, dhc_ref, gwdn_ref, gg_ref, gb_ref):
        @pl.when(pl.program_id(0) == 0)
        def _():
            gwdn_ref[...] = jnp.zeros_like(gwdn_ref)
            gg_ref[...] = jnp.zeros_like(gg_ref)
            gb_ref[...] = jnp.zeros_like(gb_ref)

        dx = dx_ref[...]
        xh = xh_ref[...]
        gg_ref[...] += _colsum(dx * xh)
        gb_ref[...] += _colsum(dx)
        dr = _ln_bwd(dx, xh, rs_ref[...], g_ref[...])
        dr_ref[...] = dr
        dob = dr.astype(BF16)

        def d_act(j):
            return _dot_nt(dob, wdn_ref[j * cw:(j + 1) * cw, :])

        da_ahead = d_act(0)
        for j in range(nck):
            da = da_ahead
            if j + 1 < nck:
                da_ahead = d_act(j + 1)
            gt = hcs_ref[:, j * cw:(j + 1) * cw].astype(F32)
            vv = hcs_ref[:, f + j * cw:f + (j + 1) * cw].astype(F32)
            sg = _sigmoid(gt)
            sl = gt * sg
            a = (sl * vv).astype(BF16)
            gwdn_ref[j * cw:(j + 1) * cw, :] += _dot_tn(a, dob)
            dhc_ref[:, j * cw:(j + 1) * cw] = (da * vv * (sg * (1.0 + gt * (1.0 - sg)))).astype(BF16)
            dhc_ref[:, f + j * cw:f + (j + 1) * cw] = (da * sl).astype(BF16)

    tile = pl.BlockSpec((tm, d), lambda i: (i, 0))
    wide = pl.BlockSpec((tm, f2), lambda i: (i, 0))
    nl = gwdn_buf.shape[0]
    return pl.pallas_call(
        body, grid=(n,),
        in_specs=[tile, tile, pl.BlockSpec((tm, 1), lambda i: (i, 0)), _rowspec(d), wide,
                  _wspec(wdn, lw), pl.BlockSpec(memory_space=pl.ANY)],
        out_specs=[tile, wide, pl.BlockSpec((None, f, d), lambda i: (l, 0, 0)), _rowspec(d), _rowspec(d)],
        out_shape=[jax.ShapeDtypeStruct((s, d), F32), jax.ShapeDtypeStruct((s, f2), BF16),
                   jax.ShapeDtypeStruct((nl, f, d), F32),
                   jax.ShapeDtypeStruct((1, d), F32), jax.ShapeDtypeStruct((1, d), F32)],
        input_output_aliases={6: 2},
        compiler_params=_cparams(), name=name,
    )(dx2, xh2, rstd2, g2, hcs, wdn, gwdn_buf)


def _bwd_in(dp, dres, w, l, alpha, name):
    s, d = dres.shape
    nn = dp.shape[1]
    tm = min(TM_BWD_IN, s)
    n = s // tm
    tile = pl.BlockSpec((tm, d), lambda i: (i, 0))

    def body(dp_ref, dres_ref, w_ref, o_ref):
        o_ref[...] = alpha * dres_ref[...] + _dot_nt(dp_ref[...], w_ref[...])

    return pl.pallas_call(
        body, grid=(n,),
        in_specs=[pl.BlockSpec((tm, nn), lambda i: (i, 0)), tile, _wspec(w, l)],
        out_specs=tile, out_shape=jax.ShapeDtypeStruct((s, d), F32),
        compiler_params=_cparams(parallel=True), name=name,
    )(dp, dres, w)


def _ffn_bwd2(dhc, hs, dres, wup, lw, fdw, l, xh, rstd, g, alpha, name):
    s, d = dres.shape
    f2 = dhc.shape[1]
    tm = min(TM_BWD_IN, s)
    n = s // tm
    hb = FFN_HALO
    cw = min(CW_FFN_BWD2, f2)
    nck = f2 // cw
    halo_blocks = tm // hb
    assert f2 % cw == 0 and fdw.shape[1] == 3

    def body(dhc_ref, halo_ref, hs_ref, dres_ref, w_ref, dw_ref, xh_ref, rs_ref, g_ref,
             o_ref, dh_ref, gdw_ref, gg_ref, gb_ref):
        i = pl.program_id(0)

        @pl.when(i == 0)
        def _():
            gdw_ref[...] = jnp.zeros_like(gdw_ref)
            gg_ref[...] = jnp.zeros_like(gg_ref)
            gb_ref[...] = jnp.zeros_like(gb_ref)

        has_next = i < n - 1
        dx = alpha * dres_ref[...]
        for j in range(nck):
            c0 = j * cw
            dc = dhc_ref[:, c0:c0 + cw].astype(F32)
            hal = jnp.where(has_next, halo_ref[:, c0:c0 + cw].astype(F32), 0.0)
            nxt = [hal[0:1], hal[1:2]]
            u1 = _shift_up(dc, 1, nxt[:1])
            u2 = _shift_up(dc, 2, nxt)
            h = hs_ref[:, c0:c0 + cw].astype(F32)
            gdw_ref[2:3, c0:c0 + cw] += _colsum(dc * h)
            gdw_ref[1:2, c0:c0 + cw] += _colsum(u1 * h)
            gdw_ref[0:1, c0:c0 + cw] += _colsum(u2 * h)
            dh = (dw_ref[2:3, c0:c0 + cw] * dc + dw_ref[1:2, c0:c0 + cw] * u1
                  + dw_ref[0:1, c0:c0 + cw] * u2).astype(BF16)
            dh_ref[:, c0:c0 + cw] = dh
            dx = dx + _dot_nt(dh, w_ref[:, c0:c0 + cw])
        xhv = xh_ref[...]
        gg_ref[...] += _colsum(dx * xhv)
        gb_ref[...] += _colsum(dx)
        o_ref[...] = _ln_bwd(dx, xhv, rs_ref[...], g_ref[...])

    tile = pl.BlockSpec((tm, d), lambda i: (i, 0))
    wide = pl.BlockSpec((tm, f2), lambda i: (i, 0))
    return pl.pallas_call(
        body, grid=(n,),
        in_specs=[wide, pl.BlockSpec((hb, f2), lambda i: (jnp.minimum((i + 1) * halo_blocks, s // hb - 1), 0)),
                  wide, tile, _wspec(wup, lw), pl.BlockSpec((None, 3, f2), lambda i: (l, 0, 0)), tile,
                  pl.BlockSpec((tm, 1), lambda i: (i, 0)), _rowspec(d)],
        out_specs=[tile, wide, pl.BlockSpec((3, f2), lambda i: (0, 0)), _rowspec(d), _rowspec(d)],
        out_shape=[jax.ShapeDtypeStruct((s, d), F32), jax.ShapeDtypeStruct((s, f2), BF16),
                   jax.ShapeDtypeStruct((3, f2), F32),
                   jax.ShapeDtypeStruct((1, d), F32), jax.ShapeDtypeStruct((1, d), F32)],
        compiler_params=_cparams(), name=name,
    )(dhc, dhc, hs, dres, wup, fdw, xh, rstd, g)


def _mm_tn(a, ga, ba, bm, buf, l, name):
    s, k = a.shape
    nn = bm.shape[1]
    ts = min(TS_MM_TN, s)
    tn = nn // N_CHIPS if nn > 1024 else nn
    nj, ns = nn // tn, s // ts

    def body(a_ref, g_ref, b_ref, bm_ref, buf_ref, o_ref):
        @pl.when(pl.program_id(1) == 0)
        def _():
            o_ref[...] = jnp.zeros_like(o_ref)

        ab = (a_ref[...] * g_ref[...] + b_ref[...]).astype(BF16)
        o_ref[...] += _dot_tn(ab, bm_ref[...])

    return pl.pallas_call(
        body, grid=(nj, ns),
        in_specs=[pl.BlockSpec((ts, k), lambda j, t: (t, 0)), _rowspec(k), _rowspec(k),
                  pl.BlockSpec((ts, tn), lambda j, t: (t, j)), pl.BlockSpec(memory_space=pl.ANY)],
        out_specs=pl.BlockSpec((None, k, tn), lambda j, t: (l, 0, j)),
        out_shape=jax.ShapeDtypeStruct(buf.shape, F32),
        input_output_aliases={4: 0},
        compiler_params=_cparams(2), name=name,
    )(a, ga, ba, bm, buf)


def _conv_fwd(xin, gin, bin_, win, wout, lw, adw, l, adwb, lng, lnb, alpha, name, hook=None):
    s, d = xin.shape
    kw = adw.shape[1]
    hb = CONV_HALO
    tm = min(TM_CONV, s)
    n = s // tm
    assert kw - 1 <= hb <= tm

    def body(x_ref, g_ref, b_ref, win_ref, dw_ref, dwb_ref, lng_ref, lnb_ref, wout_ref,
             xo_ref, rs_ref, p_ref, chat_ref, rsc_ref, u_scr, u8_scr):
        @pl.when(pl.program_id(0) == 0)
        def _():
            u_scr[0:hb, :] = jnp.zeros((hb, d), F32)

        x = x_ref[...] * g_ref[...] + b_ref[...]
        pm = _dot(x.astype(BF16), win_ref[...])
        p_ref[...] = pm.astype(BF16)
        u = pm[:, :d] * _sigmoid(pm[:, d:])
        u_scr[hb:hb + tm, :] = u
        _fill_shifted(u_scr, u8_scr)
        acc = dwb_ref[...] + dw_ref[kw - 1:kw, :] * u
        for k in range(kw - 1):
            acc = acc + dw_ref[k:k + 1, :] * _tap(u_scr, u8_scr, hb - (kw - 1) + k, 0, tm, slice(None))
        u_scr[0:hb, :] = u_scr[tm:tm + hb, :]
        chat, rstdc = _ln_stats(acc)
        chat_ref[...] = chat.astype(BF16)
        rsc_ref[...] = rstdc
        nv = chat * lng_ref[...] + lnb_ref[...]
        sv = (nv * _sigmoid(nv)).astype(BF16)
        xhat, rstd = _ln_stats(alpha * x + _dot(sv, wout_ref[...]))
        xo_ref[...] = xhat
        rs_ref[...] = rstd

    tile = pl.BlockSpec((tm, d), lambda i: (i, 0))
    col = pl.BlockSpec((tm, 1), lambda i: (i, 0))
    return _hosted_call(
        body, hook, n,
        in_specs=[tile, _rowspec(d), _rowspec(d), _wspec(win, lw),
                  pl.BlockSpec((None, kw, d), lambda i: (l, 0, 0)), _rowspec(d), _rowspec(d), _rowspec(d),
                  _wspec(wout, lw)],
        out_specs=[tile, col, pl.BlockSpec((tm, 2 * d), lambda i: (i, 0)), tile, col],
        out_shape=[jax.ShapeDtypeStruct((s, d), F32), jax.ShapeDtypeStruct((s, 1), F32),
                   jax.ShapeDtypeStruct((s, 2 * d), BF16), jax.ShapeDtypeStruct((s, d), BF16),
                   jax.ShapeDtypeStruct((s, 1), F32)],
        scratch_shapes=[pltpu.VMEM((tm + hb, d), F32), pltpu.VMEM((7, tm + hb - 8, d), F32)],
        args=(xin, gin, bin_, win, adw, adwb, lng, lnb, wout), name=name)


def _conv_bwd1(dr1, chat, rstdc, p, wout, lw, adw, l, lng, lnb, gwout_buf, name):
    s, d = dr1.shape
    kw = adw.shape[1]
    hb = CONV_HALO
    tm = min(TM_CONV, s)
    n = s // tm
    halo_blocks = tm // hb
    rbl = CONV_ROW_BLOCK

    def body(dr_ref, chat_ref, rsc_ref, p_ref, halo_ref, wout_ref, dw_ref, lng_ref, lnb_ref, buf_ref,
             dp_ref, gwout_ref, gdw_ref, gdwb_ref, glng_ref, glnb_ref, u_scr, dc_scr, u8_scr, dc8_scr):
        i = pl.program_id(0)
        t = n - 1 - i

        @pl.when(i == 0)
        def _():
            dc_scr[tm:tm + hb, :] = jnp.zeros((hb, d), F32)
            gwout_ref[...] = jnp.zeros_like(gwout_ref)
            gdw_ref[...] = jnp.zeros_like(gdw_ref)
            gdwb_ref[...] = jnp.zeros_like(gdwb_ref)
            glng_ref[...] = jnp.zeros_like(glng_ref)
            glnb_ref[...] = jnp.zeros_like(glnb_ref)

        dob = dr_ref[...].astype(BF16)
        chat = chat_ref[...].astype(F32)
        lng = lng_ref[...]
        nv = chat * lng + lnb_ref[...]
        sgn = _sigmoid(nv)
        gwout_ref[...] += _dot_tn((nv * sgn).astype(BF16), dob)
        dn = _dot_nt(dob, wout_ref[...]) * (sgn * (1.0 + nv * (1.0 - sgn)))
        glng_ref[...] += _colsum(dn * chat)
        glnb_ref[...] += _colsum(dn)
        dc = _ln_bwd(dn, chat, rsc_ref[...], lng)
        gdwb_ref[...] += _colsum(dc)

        pm = p_ref[...].astype(F32)
        a = pm[:, :d]
        sg = _sigmoid(pm[:, d:])
        ph = halo_ref[...].astype(F32)
        u_scr[0:hb, :] = jnp.where(t > 0, ph[:, :d] * _sigmoid(ph[:, d:]), 0.0)
        u_scr[hb:hb + tm, :] = a * sg
        dc_scr[0:tm, :] = dc
        _fill_shifted(u_scr, u8_scr)
        _fill_shifted(dc_scr, dc8_scr)
        du = dw_ref[kw - 1:kw, :] * dc
        for k in range(kw - 1):
            du = du + dw_ref[k:k + 1, :] * _tap(dc_scr, dc8_scr, kw - 1 - k, 0, tm, slice(None))
        for cb in range(d // LANES):
            cols = pl.ds(cb * LANES, LANES)

            def rows_step(rb, accs, cols=cols):
                r0 = pl.multiple_of(rb * rbl, rbl)
                dcb = dc_scr[pl.ds(r0, rbl), cols]
                out = []
                for k in range(kw):
                    prod = dcb * _tap(u_scr, u8_scr, hb - (kw - 1) + k, r0, rbl, cols)
                    part = prod[0:8]
                    for g8 in range(1, rbl // 8):
                        part = part + prod[8 * g8:8 * g8 + 8]
                    out.append(accs[k] + part)
                return tuple(out)

            accs = lax.fori_loop(0, tm // rbl, rows_step, tuple(jnp.zeros((8, LANES), F32) for _ in range(kw)))
            for k in range(kw):
                gdw_ref[k:k + 1, cols] += _colsum(accs[k])
        dc_scr[tm:tm + hb, :] = dc[0:hb, :]
        dp_ref[:, :d] = (du * sg).astype(BF16)
        dp_ref[:, d:] = (du * a * sg * (1.0 - sg)).astype(BF16)

    tile = pl.BlockSpec((tm, d), lambda i: (n - 1 - i, 0))
    col = pl.BlockSpec((tm, 1), lambda i: (n - 1 - i, 0))
    nl = gwout_buf.shape[0]
    return pl.pallas_call(
        body, grid=(n,),
        in_specs=[tile, tile, col, pl.BlockSpec((tm, 2 * d), lambda i: (n - 1 - i, 0)),
                  pl.BlockSpec((hb, 2 * d), lambda i: (jnp.maximum((n - 1 - i) * halo_blocks - 1, 0), 0)),
                  _wspec(wout, lw), pl.BlockSpec((None, kw, d), lambda i: (l, 0, 0)), _rowspec(d), _rowspec(d),
                  pl.BlockSpec(memory_space=pl.ANY)],
        out_specs=[pl.BlockSpec((tm, 2 * d), lambda i: (n - 1 - i, 0)),
                   pl.BlockSpec((None, d, d), lambda i: (l, 0, 0)),
                   pl.BlockSpec((kw, d), lambda i: (0, 0)), _rowspec(d), _rowspec(d), _rowspec(d)],
        out_shape=[jax.ShapeDtypeStruct((s, 2 * d), BF16), jax.ShapeDtypeStruct((nl, d, d), F32),
                   jax.ShapeDtypeStruct((kw, d), F32), jax.ShapeDtypeStruct((1, d), F32),
                   jax.ShapeDtypeStruct((1, d), F32), jax.ShapeDtypeStruct((1, d), F32)],
        scratch_shapes=[pltpu.VMEM((tm + hb, d), F32), pltpu.VMEM((tm + hb, d), F32),
                        pltpu.VMEM((7, tm + hb - 8, d), F32), pltpu.VMEM((7, tm + hb - 8, d), F32)],
        input_output_aliases={9: 1},
        compiler_params=_cparams(), name=name,
    )(dr1, chat, rstdc, p, p, wout, adw, lng, lnb, gwout_buf)


def _sgu_gate(vn, wm_ref, bs_ref, s_scr, tm, nh):
    for ch in range(tm // CHUNK):
        r0 = ch * CHUNK
        for h in range(nh):
            c0 = h * CHUNK
            s_scr[r0:r0 + CHUNK, c0:c0 + CHUNK] = (
                _dot(wm_ref[h], vn[r0:r0 + CHUNK, c0:c0 + CHUNK]) + bs_ref[:, c0:c0 + CHUNK])


def _sgu_fwd(xin, gin, bin_, win, lg, lb, wm, bs_exp, wout, alpha, name):
    s, d = xin.shape
    nh = wm.shape[0]
    tm = min(TM_SGU, s)
    n = s // tm
    assert tm % CHUNK == 0 and nh * CHUNK == d

    def body(x_ref, g_ref, b_ref, win_ref, lg_ref, lb_ref, wm_ref, bs_ref, wout_ref,
             xo_ref, rs_ref, zp_ref, s_scr):
        x = x_ref[...] * g_ref[...] + b_ref[...]
        zp = _dot(x.astype(BF16), win_ref[...])
        zp_ref[...] = zp.astype(BF16)
        z = _gelu(zp)
        vhat, _ = _ln_stats(z[:, d:])
        vn = (vhat * lg_ref[...] + lb_ref[...]).astype(BF16)
        _sgu_gate(vn, wm_ref, bs_ref, s_scr, tm, nh)
        q = (z[:, :d] * s_scr[...]).astype(BF16)
        xhat, rstd = _ln_stats(alpha * x + _dot(q, wout_ref[...]))
        xo_ref[...] = xhat
        rs_ref[...] = rstd

    tile = pl.BlockSpec((tm, d), lambda i: (i, 0))
    return pl.pallas_call(
        body, grid=(n,),
        in_specs=[tile, _rowspec(d), _rowspec(d), _wspec(win, 0), _rowspec(d), _rowspec(d),
                  _resident((nh, CHUNK, CHUNK), lambda i: (0, 0, 0)),
                  _resident((CHUNK, d), lambda i: (0, 0)), _wspec(wout, 0)],
        out_specs=[tile, pl.BlockSpec((tm, 1), lambda i: (i, 0)), pl.BlockSpec((tm, 2 * d), lambda i: (i, 0))],
        out_shape=[jax.ShapeDtypeStruct((s, d), F32), jax.ShapeDtypeStruct((s, 1), F32),
                   jax.ShapeDtypeStruct((s, 2 * d), BF16)],
        scratch_shapes=[pltpu.VMEM((tm, d), F32)],
        compiler_params=_cparams(parallel=True), name=name,
    )(xin, gin, bin_, win, lg, lb, wm, bs_exp, wout)


def _sgu_bwd1(dr1, zp, wout, lg, lb, wm, wmt, bs_exp, gwout_buf, name):
    s, d = dr1.shape
    nh = wm.shape[0]
    tm = min(TM_SGU, s)
    n = s // tm

    def body(dr_ref, zp_ref, wout_ref, lg_ref, lb_ref, wm_ref, wmt_ref, bs_ref, buf_ref,
             dzp_ref, gwout_ref, gws_ref, gbs_ref, glg_ref, glb_ref, s_scr, dvn_scr, bs_acc):
        i = pl.program_id(0)

        @pl.when(i == 0)
        def _():
            gwout_ref[...] = jnp.zeros_like(gwout_ref)
            gws_ref[...] = jnp.zeros_like(gws_ref)
            glg_ref[...] = jnp.zeros_like(glg_ref)
            glb_ref[...] = jnp.zeros_like(glb_ref)
            bs_acc[...] = jnp.zeros_like(bs_acc)

        dob = dr_ref[...].astype(BF16)
        zp = zp_ref[...].astype(F32)
        z = _gelu(zp)
        u = z[:, :d]
        lg = lg_ref[...]
        vhat, rstdv = _ln_stats(z[:, d:])
        vn = (vhat * lg + lb_ref[...]).astype(BF16)
        _sgu_gate(vn, wm_ref, bs_ref, s_scr, tm, nh)
        sv = s_scr[...]
        gwout_ref[...] += _dot_tn((u * sv).astype(BF16), dob)
        dq = _dot_nt(dob, wout_ref[...])
        ds = dq * u
        dsb = ds.astype(BF16)
        part = jnp.zeros((CHUNK, d), F32)
        for ch in range(tm // CHUNK):
            r0 = ch * CHUNK
            part = part + ds[r0:r0 + CHUNK, :]
            for h in range(nh):
                c0 = h * CHUNK
                blk = dsb[r0:r0 + CHUNK, c0:c0 + CHUNK]
                gws_ref[h] += _dot_nt(blk, vn[r0:r0 + CHUNK, c0:c0 + CHUNK])
                dvn_scr[r0:r0 + CHUNK, c0:c0 + CHUNK] = _dot(wmt_ref[h], blk)
        bs_acc[...] += part
        dvn = dvn_scr[...]
        glg_ref[...] += _colsum(dvn * vhat)
        glb_ref[...] += _colsum(dvn)
        dv = _ln_bwd(dvn, vhat, rstdv, lg)
        gp = _gelu_grad(zp)
        dzp_ref[:, :d] = (dq * sv * gp[:, :d]).astype(BF16)
        dzp_ref[:, d:] = (dv * gp[:, d:]).astype(BF16)

        @pl.when(i == n - 1)
        def _():
            rows = lax.broadcasted_iota(jnp.int32, (CHUNK, CHUNK), 0)
            cols = lax.broadcasted_iota(jnp.int32, (CHUNK, CHUNK), 1)
            tril = (cols <= rows).astype(F32)
            acc = bs_acc[...]
            for h in range(nh):
                gws_ref[h] = gws_ref[h] * tril
                gbs_ref[:, h:h + 1] = jnp.sum(acc[:, h * CHUNK:(h + 1) * CHUNK], axis=1, keepdims=True)

    tile = pl.BlockSpec((tm, d), lambda i: (i, 0))
    wide = pl.BlockSpec((tm, 2 * d), lambda i: (i, 0))
    hspec = _resident((nh, CHUNK, CHUNK), lambda i: (0, 0, 0))
    return pl.pallas_call(
        body, grid=(n,),
        in_specs=[tile, wide, _wspec(wout, 0), _rowspec(d), _rowspec(d), hspec, hspec,
                  _resident((CHUNK, d), lambda i: (0, 0)), pl.BlockSpec(memory_space=pl.ANY)],
        out_specs=[wide, pl.BlockSpec((None, d, d), lambda i: (0, 0, 0)),
                   pl.BlockSpec((nh, CHUNK, CHUNK), lambda i: (0, 0, 0)),
                   pl.BlockSpec((CHUNK, nh), lambda i: (0, 0)), _rowspec(d), _rowspec(d)],
        out_shape=[jax.ShapeDtypeStruct((s, 2 * d), BF16), jax.ShapeDtypeStruct(gwout_buf.shape, F32),
                   jax.ShapeDtypeStruct((nh, CHUNK, CHUNK), F32), jax.ShapeDtypeStruct((CHUNK, nh), F32),
                   jax.ShapeDtypeStruct((1, d), F32), jax.ShapeDtypeStruct((1, d), F32)],
        scratch_shapes=[pltpu.VMEM((tm, d), F32), pltpu.VMEM((tm, d), F32), pltpu.VMEM((CHUNK, d), F32)],
        input_output_aliases={8: 1},
        compiler_params=_cparams(), name=name,
    )(dr1, zp, wout, lg, lb, wm, wmt, bs_exp, gwout_buf)


def _pool_counts(t0, tm, w):
    pos = t0 + lax.broadcasted_iota(jnp.int32, (tm, 1), 0)
    return jnp.minimum(pos + 1, w).astype(F32)


def _pool_fwd(xin, gin, bin_, win, wg, scale, wout, alpha, name):
    s, d = xin.shape
    ng, dg = wg.shape[0], wg.shape[1]
    hb = POOL_HALO
    tm = min(TM_POOL, s)
    n = s // tm
    assert ng == len(POOL_WINDOWS) and ng * dg == d and max(POOL_WINDOWS) <= hb

    def body(x_ref, g_ref, b_ref, win_ref, wg_ref, sc_ref, wout_ref, xo_ref, rs_ref, ys_ref, y_scr, z_scr):
        i = pl.program_id(0)

        @pl.when(i == 0)
        def _():
            y_scr[0:hb, :] = jnp.zeros((hb, d), F32)

        x = x_ref[...] * g_ref[...] + b_ref[...]
        y = _dot(x.astype(BF16), win_ref[...])
        ys_ref[...] = y.astype(BF16)
        y_scr[hb:hb + tm, :] = y
        for g, w in enumerate(POOL_WINDOWS):
            c0 = g * dg
            acc = y[:, c0:c0 + dg]
            for dd in range(1, w):
                acc = acc + y_scr[pl.ds(hb - dd, tm), c0:c0 + dg]
            pg = acc / _pool_counts(i * tm, tm, w) - y[:, c0:c0 + dg]
            z_scr[:, c0:c0 + dg] = _dot(pg.astype(BF16), wg_ref[g])
        y_scr[0:hb, :] = y_scr[tm:tm + hb, :]
        zz = (z_scr[...] * sc_ref[...]).astype(BF16)
        xhat, rstd = _ln_stats(alpha * x + _dot(zz, wout_ref[...]))
        xo_ref[...] = xhat
        rs_ref[...] = rstd

    tile = pl.BlockSpec((tm, d), lambda i: (i, 0))
    return pl.pallas_call(
        body, grid=(n,),
        in_specs=[tile, _rowspec(d), _rowspec(d), _wspec(win, 0),
                  _resident((ng, dg, dg), lambda i: (0, 0, 0)), _rowspec(d), _wspec(wout, 0)],
        out_specs=[tile, pl.BlockSpec((tm, 1), lambda i: (i, 0)), tile],
        out_shape=[jax.ShapeDtypeStruct((s, d), F32), jax.ShapeDtypeStruct((s, 1), F32),
                   jax.ShapeDtypeStruct((s, d), BF16)],
        scratch_shapes=[pltpu.VMEM((tm + hb, d), F32), pltpu.VMEM((tm, d), F32)],
        compiler_params=_cparams(), name=name,
    )(xin, gin, bin_, win, wg, scale, wout)


def _pool_bwd1(dr1, ys, wout, wg, scale, gwout_buf, name):
    s, d = dr1.shape
    ng, dg = wg.shape[0], wg.shape[1]
    hb = POOL_HALO
    tm = min(TM_POOL, s)
    n = s // tm
    halo_blocks = tm // hb

    def body(dr_ref, ys_ref, halo_ref, wout_ref, wg_ref, sc_ref, buf_ref,
             dy_ref, gwout_ref, gwg_ref, gsc_ref, y_scr, e_scr, z_scr, dp_scr):
        i = pl.program_id(0)
        t = n - 1 - i

        @pl.when(i == 0)
        def _():
            e_scr[tm:tm + hb, :] = jnp.zeros((hb, d), F32)
            gwout_ref[...] = jnp.zeros_like(gwout_ref)
            gwg_ref[...] = jnp.zeros_like(gwg_ref)
            gsc_ref[...] = jnp.zeros_like(gsc_ref)

        dob = dr_ref[...].astype(BF16)
        y = ys_ref[...].astype(F32)
        y_scr[0:hb, :] = jnp.where(t > 0, halo_ref[...].astype(F32), 0.0)
        y_scr[hb:hb + tm, :] = y
        pgs = []
        for g, w in enumerate(POOL_WINDOWS):
            c0 = g * dg
            acc = y[:, c0:c0 + dg]
            for dd in range(1, w):
                acc = acc + y_scr[pl.ds(hb - dd, tm), c0:c0 + dg]
            pg = (acc / _pool_counts(t * tm, tm, w) - y[:, c0:c0 + dg]).astype(BF16)
            pgs.append(pg)
            z_scr[:, c0:c0 + dg] = _dot(pg, wg_ref[g])
        zpre = z_scr[...]
        sc = sc_ref[...]
        gwout_ref[...] += _dot_tn((zpre * sc).astype(BF16), dob)
        dz = _dot_nt(dob, wout_ref[...])
        gsc_ref[...] += _colsum(dz * zpre)
        dzpre = (dz * sc).astype(BF16)
        for g, w in enumerate(POOL_WINDOWS):
            c0 = g * dg
            dzg = dzpre[:, c0:c0 + dg]
            gwg_ref[g] += _dot_tn(pgs[g], dzg)
            dp = _dot_nt(dzg, wg_ref[g])
            dp_scr[:, c0:c0 + dg] = dp
            e_scr[0:tm, c0:c0 + dg] = dp / _pool_counts(t * tm, tm, w)
        for g, w in enumerate(POOL_WINDOWS):
            c0 = g * dg
            acc = e_scr[0:tm, c0:c0 + dg]
            for dd in range(1, w):
                acc = acc + e_scr[pl.ds(dd, tm), c0:c0 + dg]
            dy_ref[:, c0:c0 + dg] = (acc - dp_scr[:, c0:c0 + dg]).astype(BF16)
        e_scr[tm:tm + hb, :] = e_scr[0:hb, :]

    tile = pl.BlockSpec((tm, d), lambda i: (n - 1 - i, 0))
    return pl.pallas_call(
        body, grid=(n,),
        in_specs=[tile, tile,
                  pl.BlockSpec((hb, d), lambda i: (jnp.maximum((n - 1 - i) * halo_blocks - 1, 0), 0)),
                  _wspec(wout, 0), _resident((ng, dg, dg), lambda i: (0, 0, 0)), _rowspec(d),
                  pl.BlockSpec(memory_space=pl.ANY)],
        out_specs=[tile, pl.BlockSpec((None, d, d), lambda i: (0, 0, 0)),
                   pl.BlockSpec((ng, dg, dg), lambda i: (0, 0, 0)), _rowspec(d)],
        out_shape=[jax.ShapeDtypeStruct((s, d), BF16), jax.ShapeDtypeStruct(gwout_buf.shape, F32),
                   jax.ShapeDtypeStruct((ng, dg, dg), F32), jax.ShapeDtypeStruct((1, d), F32)],
        scratch_shapes=[pltpu.VMEM((tm + hb, d), F32), pltpu.VMEM((tm + hb, d), F32),
                        pltpu.VMEM((tm, d), F32), pltpu.VMEM((tm, d), F32)],
        input_output_aliases={6: 1},
        compiler_params=_cparams(), name=name,
    )(dr1, ys, ys, wout, wg, scale, gwout_buf)


def _loss_head(xh, g, b, target, name):
    s, d = xh.shape
    tm = min(512, s)
    n = s // tm

    def body(xh_ref, g_ref, b_ref, t_ref, dy_ref, loss_ref, acc):
        i = pl.program_id(0)

        @pl.when(i == 0)
        def _():
            acc[...] = jnp.zeros_like(acc)

        err = xh_ref[...] * g_ref[...] + b_ref[...] - t_ref[...]
        dy_ref[...] = err * (1.0 / d)
        acc[...] += _colsum(err * err)

        @pl.when(i == n - 1)
        def _():
            loss_ref[...] = (0.5 / d) * jnp.sum(acc[...], axis=1, keepdims=True)

    tile = pl.BlockSpec((tm, d), lambda i: (i, 0))
    return pl.pallas_call(
        body, grid=(n,),
        in_specs=[tile, _rowspec(d), _rowspec(d), tile],
        out_specs=[tile, pl.BlockSpec((1, 1), lambda i: (0, 0))],
        out_shape=[jax.ShapeDtypeStruct((s, d), F32), jax.ShapeDtypeStruct((1, 1), F32)],
        scratch_shapes=[pltpu.VMEM((1, d), F32)],
        compiler_params=_cparams(), name=name,
    )(xh, g, b, target)


def _elementwise(fn, ins, out_dtypes, name):
    shape = ins[0].shape
    c = shape[-1]
    r = math.prod(shape[:-1])
    tr = _pick_rows(r, c, 4, 1 << 20)

    def body(*refs):
        vals = fn(*[ref[...] for ref in refs[:len(ins)]])
        for ref, v in zip(refs[len(ins):], vals):
            ref[...] = v.astype(ref.dtype)

    spec = pl.BlockSpec((tr, c), lambda i: (i, 0))
    outs = pl.pallas_call(
        body, grid=(r // tr,),
        in_specs=[spec] * len(ins), out_specs=[spec] * len(out_dtypes),
        out_shape=[jax.ShapeDtypeStruct((r, c), dt) for dt in out_dtypes],
        compiler_params=_cparams(parallel=True), name=name,
    )(*[a.reshape(r, c) for a in ins])
    return [o.reshape(shape) for o in outs]


def _prefetch_call(body, grid, in_specs, out_specs, out_shape, name):
    return pl.pallas_call(
        body,
        grid_spec=pltpu.PrefetchScalarGridSpec(num_scalar_prefetch=1, grid=grid, in_specs=in_specs, out_specs=out_specs),
        out_shape=out_shape,
        compiler_params=_cparams(len(grid), parallel=True), name=name)


def _cast_into_full(w3, l0, l, kind, chip1, name):
    _, r, c = w3.shape
    tr = _pick_rows(r, c, 4, 1 << 20)

    def body(k_ref, w_ref, o_ref):
        o_ref[...] = w_ref[...].astype(BF16)

    if kind == "row":
        out_spec = pl.BlockSpec((None, None, tr, c), lambda a, j, k: (a, k[0], j, 0))
    else:
        out_spec = pl.BlockSpec((None, tr, c), lambda a, j, k: (a, j, k[0]))
    return _prefetch_call(
        body, (l, r // tr), [pl.BlockSpec((None, tr, c), lambda a, j, k: (a + l0, j, 0))], out_spec,
        jax.ShapeDtypeStruct(_full_shape(kind, (l, r, c)), BF16), name)(chip1, w3)


def _pair_sum(g, got, kind, core1, name):
    if kind == "row":
        l, nc, sr, c = g.shape
        g5, got3 = g.reshape(l * nc, 2, sr // 2, c), got.reshape(l * nc, sr // 2, c)
    else:
        l, r, c = g.shape
        g5, got3 = g.reshape(l, 2, r // 2, c), got
    a, _, hr, c = g5.shape
    tr = _pick_rows(hr, c, 4, 1 << 20)

    def body(c_ref, g_ref, t_ref, o_ref):
        o_ref[...] = (g_ref[...] + t_ref[...]).astype(BF16)

    half = pl.BlockSpec((None, tr, c), lambda i, j, cc: (i, j, 0))
    out = _prefetch_call(
        body, (a, hr // tr), [pl.BlockSpec((None, None, tr, c), lambda i, j, cc: (i, cc[0], j, 0)), half], half,
        jax.ShapeDtypeStruct(got3.shape, BF16), name)(core1, g5, got3)
    return out.reshape(got.shape)


def _chip_sum(t, rb, kind, chip1, name):
    _, l, hr, sc = rb.shape
    tr = _pick_rows(hr, sc, 4, 1 << 19)

    def body(k_ref, t_ref, rb_ref, o_ref):
        acc = t_ref[...].astype(F32)
        for r in range(N_CHIPS - 1):
            acc = acc + rb_ref[r].astype(F32)
        o_ref[...] = acc

    if kind == "row":
        t_spec = pl.BlockSpec((None, None, tr, sc), lambda a, j, k: (a, k[0], j, 0))
    else:
        t_spec = pl.BlockSpec((None, tr, sc), lambda a, j, k: (a, j, k[0]))
    return _prefetch_call(
        body, (l, hr // tr),
        [t_spec, pl.BlockSpec((N_CHIPS - 1, None, tr, sc), lambda a, j, k: (0, a, j, 0))],
        pl.BlockSpec((None, tr, sc), lambda a, j, k: (a, j, 0)),
        jax.ShapeDtypeStruct((l, hr, sc), F32), name)(chip1, t, rb)


def _adamw_math(w_, g_, m_, v_):
    m2 = ADAM_B1 * m_ + (1.0 - ADAM_B1) * g_
    v2 = ADAM_B2 * v_ + (1.0 - ADAM_B2) * (g_ * g_)
    m_hat = m2 / (1.0 - ADAM_B1 ** ADAM_STEP)
    v_hat = v2 / (1.0 - ADAM_B2 ** ADAM_STEP)
    delta = -ADAM_LR * (m_hat / (jnp.sqrt(v_hat) + ADAM_EPS) + ADAM_WD * w_)
    return delta, m2, v2


def _adamw_big(w, m, v, own, other, core1, name):
    l, hr, c = own.shape
    view = lambda a: a.reshape(l, 2, hr, c)
    tr = _pick_rows(hr, c, 4, 1 << 20)

    def body(c_ref, w_ref, m_ref, v_ref, own_ref, oth_ref, g_ref, d_ref, m2_ref, v2_ref):
        g = jnp.where(pl.program_id(1) == c_ref[0], own_ref[...], oth_ref[...])
        g_ref[...] = g
        d_ref[...], m2_ref[...], v2_ref[...] = _adamw_math(w_ref[...], g, m_ref[...], v_ref[...])

    s4 = pl.BlockSpec((None, None, tr, c), lambda a, h, j, cc: (a, h, j, 0))
    s3 = pl.BlockSpec((None, tr, c), lambda a, h, j, cc: (a, j, 0))
    outs = _prefetch_call(
        body, (l, 2, hr // tr), [s4, s4, s4, s3, s3], [s4] * 4,
        [jax.ShapeDtypeStruct((l, 2, hr, c), F32)] * 4, name)(core1, view(w), view(m), view(v), own, other)
    return [o.reshape(w.shape) for o in outs]


def _sum_devices(own, gathered, me1, name):
    r, c = own.shape
    tr = _pick_rows(r, c, 4, 1 << 17)

    def body(me_ref, own_ref, g_ref, o_ref):
        acc = None
        for k in range(N_DEV):
            v = jnp.where(me_ref[0] == k, own_ref[...], g_ref[k])
            acc = v if acc is None else acc + v
        o_ref[...] = acc

    return _prefetch_call(
        body, (r // tr,),
        [pl.BlockSpec((tr, c), lambda i, m: (i, 0)), pl.BlockSpec((N_DEV, tr, c), lambda i, m: (0, i, 0))],
        pl.BlockSpec((tr, c), lambda i, m: (i, 0)), jax.ShapeDtypeStruct((r, c), F32), name)(me1, own, gathered)


def _adamw(w, g, m, v, name):
    return _elementwise(_adamw_math, [w, g, m, v], [F32, F32, F32], name)


ANY = pl.BlockSpec(memory_space=pl.ANY)


def _mesh_pos():
    return lax.axis_index("x"), lax.axis_index("y"), lax.axis_index("c")


def _chip_peers(x, y, c):
    out = []
    for r in (1, 2, 3):
        px = 1 - x if r & 2 else x
        py = 1 - y if r & 1 else y
        out.append((2 * px + py, (px, py, c)))
    return out


def _full_shape(kind, shard_shape):
    l, r, c = shard_shape
    return (l, N_CHIPS, r, c) if kind == "row" else (l, r, N_CHIPS * c)


def _full_piece(ref, kind, k, h, hr, sc):
    rows = pl.ds(pl.multiple_of(h * hr, SUBLANES_BF16), hr)
    if kind == "row":
        return ref.at[:, k, rows, :]
    return ref.at[:, rows, pl.ds(pl.multiple_of(k * sc, LANES), sc)]


def _remote(src, dst, ssem, rsem, dev):
    return pltpu.make_async_remote_copy(src_ref=src, dst_ref=dst, send_sem=ssem, recv_sem=rsem,
                                        device_id=dev, device_id_type=MESH)


DMA_CHUNK_BYTES = 1 << 20
DMA_MAX_CHUNKS = 32


def _chunk_views(src, dst):
    axis = len(src.shape) - 2
    rows = src.shape[axis]
    nbytes = math.prod(src.shape) * jnp.dtype(src.dtype).itemsize
    n = max(1, min(DMA_MAX_CHUNKS, nbytes // DMA_CHUNK_BYTES))
    while n > 1 and (rows % n or (rows // n) % SUBLANES_BF16):
        n -= 1
    cr = rows // n
    out = []
    for i in range(n):
        idx = (slice(None),) * axis + (pl.ds(i * cr, cr), slice(None))
        out.append((src.at[idx], dst.at[idx]))
    return out


def _start_remote(src, dst, ssem, rsem, dev):
    for s, t in _chunk_views(src, dst):
        _remote(s, t, ssem, rsem, dev).start()
    return _remote(src, dst, ssem, rsem, dev)


def _allgather_steps(fulls, kinds):
    nw = len(fulls)

    def dims(a, kind):
        return (a.shape[2] // 2, a.shape[3]) if kind == "row" else (a.shape[1] // 2, a.shape[2] // N_CHIPS)

    hrs = [dims(a, k)[0] for a, k in zip(fulls, kinds)]
    scs = [dims(a, k)[1] for a, k in zip(fulls, kinds)]

    def piece(ref, w, k, h):
        return _full_piece(ref, kinds[w], k, h, hrs[w], scs[w])

    def copies1(src, dst, sems, start):
        x, y, c = _mesh_pos()
        k_me = 2 * x + y
        out = []
        for w in range(nw):
            for r, (kj, dev) in enumerate(_chip_peers(x, y, c)):
                args = (sems[0].at[3 * w + r], sems[1].at[3 * w + r], dev)
                if start:
                    out.append(_start_remote(piece(src[w], w, k_me, c), piece(dst[w], w, k_me, c), *args))
                else:
                    out.append(_remote(piece(src[w], w, k_me, c), piece(dst[w], w, kj, c), *args))
        return out

    def copies2(src, dst, sems, start):
        x, y, c = _mesh_pos()
        out = []
        for w in range(nw):
            for r, (kj, _) in enumerate(_chip_peers(x, y, c)):
                args = (sems[0].at[3 * w + r], sems[1].at[3 * w + r], (x, y, 1 - c))
                if start:
                    out.append(_start_remote(piece(src[w], w, kj, c), piece(dst[w], w, kj, c), *args))
                else:
                    out.append(_remote(piece(src[w], w, kj, 1 - c), piece(dst[w], w, kj, 1 - c), *args))
        return out

    def finish(copies):
        def fn(src, dst, sems):
            for cp in copies(src, dst, sems, False):
                cp.wait_recv()
            for cp in copies(src, dst, sems, False):
                cp.wait_send()
        return fn

    step1 = (lambda s, d, m: copies1(s, d, m, True), finish(copies1))
    step2 = (lambda s, d, m: copies2(s, d, m, True), finish(copies2))
    return step1, step2


def _exchange_hook(arrays, step):
    return dict(arrays=list(arrays), sem_len=3 * len(arrays), first=step[0], last=step[1])


def _allgather_weights(fulls, kinds):
    nw = len(fulls)
    step1, step2 = _allgather_steps(fulls, kinds)

    def body(*refs):
        mine, fu = refs[:nw], refs[nw:2 * nw]
        sems1, sems2 = refs[2 * nw:2 * nw + 2], refs[2 * nw + 2:]
        step1[0](mine, fu, sems1)
        step1[1](mine, fu, sems1)
        step2[0](fu, fu, sems2)
        step2[1](fu, fu, sems2)

    return pl.pallas_call(
        body,
        in_specs=[ANY] * nw, out_specs=[ANY] * nw,
        out_shape=[jax.ShapeDtypeStruct(a.shape, a.dtype) for a in fulls],
        scratch_shapes=[pltpu.SemaphoreType.DMA((3 * nw,))] * 4,
        input_output_aliases={w: w for w in range(nw)},
        name="allgather_weights",
    )(*fulls)


def _hosted(body, n_in, n_out, hook, n_steps):
    if hook is None:
        return body
    nh = len(hook["arrays"])

    def wrapped(*refs):
        ins, h_in = refs[:n_in], refs[n_in:n_in + nh]
        outs = refs[n_in + nh:n_in + nh + n_out]
        h_out = refs[n_in + nh + n_out:n_in + 2 * nh + n_out]
        rest = refs[n_in + 2 * nh + n_out:]
        scr, sems = rest[:-2], rest[-2:]
        i = pl.program_id(0)

        @pl.when(i == 0)
        def _():
            hook["first"](h_in, h_out, sems)

        body(*ins, *outs, *scr)

        @pl.when(i == n_steps - 1)
        def _():
            hook["last"](h_in, h_out, sems)

    return wrapped


def _hosted_call(body, hook, n_steps, in_specs, out_specs, out_shape, scratch_shapes, args, name):
    n_in, n_out = len(in_specs), len(out_specs)
    aliases = {}
    if hook is not None:
        nh = len(hook["arrays"])
        in_specs = list(in_specs) + [ANY] * nh
        out_specs = list(out_specs) + [ANY] * nh
        out_shape = list(out_shape) + [jax.ShapeDtypeStruct(a.shape, a.dtype) for a in hook["arrays"]]
        scratch_shapes = list(scratch_shapes) + [pltpu.SemaphoreType.DMA((hook["sem_len"],))] * 2
        aliases = {n_in + k: n_out + k for k in range(nh)}
        args = list(args) + hook["arrays"]
    outs = pl.pallas_call(
        _hosted(body, n_in, n_out, hook, n_steps), grid=(n_steps,),
        in_specs=in_specs, out_specs=out_specs, out_shape=out_shape, scratch_shapes=scratch_shapes,
        input_output_aliases=aliases, compiler_params=_cparams(), name=name,
    )(*args)
    return outs[:n_out], outs[n_out:]


def _rs_pair(fulls, kinds):
    nw = len(fulls)

    def half_all(ref, kind, h):
        if kind == "row":
            hr = ref.shape[2] // 2
            return ref.at[:, :, pl.ds(pl.multiple_of(h * hr, SUBLANES_BF16), hr), :]
        hr = ref.shape[1] // 2
        return ref.at[:, pl.ds(pl.multiple_of(h * hr, SUBLANES_BF16), hr), :]

    def half_shape(kind, shape):
        if kind == "row":
            return (shape[0], shape[1], shape[2] // 2, shape[3])
        return (shape[0], shape[1] // 2, shape[2])

    def body(*refs):
        g, got = refs[:nw], refs[nw:2 * nw]
        ssem, rsem = refs[2 * nw:]
        x, y, c = _mesh_pos()
        sibling = (x, y, 1 - c)
        cps = [_start_remote(half_all(g[w], kinds[w], 1 - c), got[w], ssem.at[w], rsem.at[w], sibling)
               for w in range(nw)]
        for cp in cps:
            cp.wait_recv()
        for cp in cps:
            cp.wait_send()

    shapes = [jax.ShapeDtypeStruct(half_shape(k, a.shape), a.dtype) for k, a in zip(kinds, fulls)]
    return pl.pallas_call(
        body, in_specs=[ANY] * nw, out_specs=[ANY] * nw, out_shape=shapes,
        scratch_shapes=[pltpu.SemaphoreType.DMA((nw,))] * 2, name="rs_pair",
    )(*fulls)


def _rs_chips(parts, kinds):
    nw = len(parts)

    def slot(ref, kind, k):
        if kind == "row":
            return ref.at[:, k]
        sc = ref.shape[2] // N_CHIPS
        return ref.at[:, :, pl.ds(pl.multiple_of(k * sc, LANES), sc)]

    def slot_shape(kind, shape):
        if kind == "row":
            return (shape[0], shape[2], shape[3])
        return (shape[0], shape[1], shape[2] // N_CHIPS)

    def body(*refs):
        t, rb = refs[:nw], refs[nw:2 * nw]
        ssem, rsem = refs[2 * nw:]
        x, y, c = _mesh_pos()
        cps = []
        for w in range(nw):
            for r, (kj, dev) in enumerate(_chip_peers(x, y, c)):
                cps.append(_start_remote(slot(t[w], kinds[w], kj), rb[w].at[r],
                                         ssem.at[3 * w + r], rsem.at[3 * w + r], dev))
        for cp in cps:
            cp.wait_recv()
        for cp in cps:
            cp.wait_send()

    shapes = [jax.ShapeDtypeStruct((N_CHIPS - 1,) + slot_shape(k, a.shape), a.dtype) for k, a in zip(kinds, parts)]
    return pl.pallas_call(
        body, in_specs=[ANY] * nw, out_specs=[ANY] * nw, out_shape=shapes,
        scratch_shapes=[pltpu.SemaphoreType.DMA((3 * nw,))] * 2, name="rs_chips",
    )(*parts)


def _rs_join(halves):
    nw = len(halves)

    def body(*refs):
        src, dst = refs[:nw], refs[nw:2 * nw]
        ssem, rsem = refs[2 * nw:]
        x, y, c = _mesh_pos()
        cps = [_start_remote(src[w], dst[w], ssem.at[w], rsem.at[w], (x, y, 1 - c)) for w in range(nw)]
        for cp in cps:
            cp.wait_recv()
        for cp in cps:
            cp.wait_send()

    return pl.pallas_call(
        body, in_specs=[ANY] * nw, out_specs=[ANY] * nw,
        out_shape=[jax.ShapeDtypeStruct(a.shape, a.dtype) for a in halves],
        scratch_shapes=[pltpu.SemaphoreType.DMA((nw,))] * 2, name="rs_join",
    )(*halves)


def _allgather_small(buf, name):
    def body(in_ref, out_ref, ssem, rsem):
        x, y, c = _mesh_pos()
        me = 4 * x + 2 * y + c
        cps, waits = [], []
        for r in range(1, N_DEV):
            px = 1 - x if r & 4 else x
            py = 1 - y if r & 2 else y
            pc = 1 - c if r & 1 else c
            cp = _remote(in_ref, out_ref.at[me], ssem.at[r - 1], rsem.at[r - 1], (px, py, pc))
            cp.start()
            cps.append(cp)
            waits.append(_remote(in_ref, out_ref.at[4 * px + 2 * py + pc], ssem.at[r - 1], rsem.at[r - 1], (px, py, pc)))
        for wt in waits:
            wt.wait_recv()
        for cp in cps:
            cp.wait_send()

    return pl.pallas_call(
        body, in_specs=[ANY], out_specs=ANY,
        out_shape=jax.ShapeDtypeStruct((N_DEV,) + buf.shape, buf.dtype),
        scratch_shapes=[pltpu.SemaphoreType.DMA((N_DEV - 1,))] * 2,
        name=name,
    )(buf)


def _pack(arrs):
    flat = jnp.concatenate([a.reshape(-1).astype(F32) for a in arrs])
    rows = -(-flat.shape[0] // (LANES * 16)) * 16
    return jnp.pad(flat, (0, rows * LANES - flat.shape[0])).reshape(rows, LANES)


def _unpack(buf, shapes):
    flat = buf.reshape(-1)
    out, off = [], 0
    for shp in shapes:
        nel = math.prod(shp)
        out.append(flat[off:off + nel].reshape(shp))
        off += nel
    return out


BIG = ("a_w_in", "a_w_out", "b_w_in", "b_w_out", "c_w_in", "c_w_grp", "c_w_out", "f_w_up", "f_w_down")
BIG_KIND = {"a_w_in": "col", "a_w_out": "row", "b_w_in": "col", "b_w_out": "row", "c_w_in": "row",
            "c_w_grp": "row", "c_w_out": "row", "f_w_up": "col", "f_w_down": "row"}
FIRST_LAYER = ("a_w_in", "a_w_out", "f_w_up", "f_w_down")
SHARDED_SMALL = ("a_dw", "a_dw_b", "a_ln_g", "a_ln_b", "c_scale", "f_dw")
REPLICATED = ("b_ln_g", "b_ln_b", "b_ws", "b_bs", "ln1_g", "ln1_b", "ln2_g", "ln2_b")
WEIGHTS = ("a_w_in", "a_dw", "a_dw_b", "a_ln_g", "a_ln_b", "a_w_out", "b_w_in", "b_ln_g", "b_ln_b", "b_ws", "b_bs",
           "b_w_out", "c_w_in", "c_w_grp", "c_scale", "c_w_out", "f_w_up", "f_dw", "f_w_down",
           "ln1_g", "ln1_b", "ln2_g", "ln2_b")


def _as3d(a):
    return a.reshape((-1,) + a.shape[-2:])


def kernel(x, a_w_in, a_dw, a_dw_b, a_ln_g, a_ln_b, a_w_out, b_w_in, b_ln_g, b_ln_b, b_ws, b_bs, b_w_out, c_w_in, c_w_grp, c_scale, c_w_out, f_w_up, f_dw, f_w_down, ln1_g, ln1_b, ln2_g, ln2_b, loss_target, m_a_w_in, m_a_dw, m_a_dw_b, m_a_ln_g, m_a_ln_b, m_a_w_out, m_b_w_in, m_b_ln_g, m_b_ln_b, m_b_ws, m_b_bs, m_b_w_out, m_c_w_in, m_c_w_grp, m_c_scale, m_c_w_out, m_f_w_up, m_f_dw, m_f_w_down, m_ln1_g, m_ln1_b, m_ln2_g, m_ln2_b, v_a_w_in, v_a_dw, v_a_dw_b, v_a_ln_g, v_a_ln_b, v_a_w_out, v_b_w_in, v_b_ln_g, v_b_ln_b, v_b_ws, v_b_bs, v_b_w_out, v_c_w_in, v_c_w_grp, v_c_scale, v_c_w_out, v_f_w_up, v_f_dw, v_f_w_down, v_ln1_g, v_ln1_b, v_ln2_g, v_ln2_b):
    w = dict(a_w_in=a_w_in, a_dw=a_dw, a_dw_b=a_dw_b, a_ln_g=a_ln_g, a_ln_b=a_ln_b, a_w_out=a_w_out, b_w_in=b_w_in, b_ln_g=b_ln_g, b_ln_b=b_ln_b, b_ws=b_ws, b_bs=b_bs, b_w_out=b_w_out, c_w_in=c_w_in, c_w_grp=c_w_grp, c_scale=c_scale, c_w_out=c_w_out, f_w_up=f_w_up, f_dw=f_dw, f_w_down=f_w_down, ln1_g=ln1_g, ln1_b=ln1_b, ln2_g=ln2_g, ln2_b=ln2_b)
    mom = dict(a_w_in=m_a_w_in, a_dw=m_a_dw, a_dw_b=m_a_dw_b, a_ln_g=m_a_ln_g, a_ln_b=m_a_ln_b, a_w_out=m_a_w_out, b_w_in=m_b_w_in, b_ln_g=m_b_ln_g, b_ln_b=m_b_ln_b, b_ws=m_b_ws, b_bs=m_b_bs, b_w_out=m_b_w_out, c_w_in=m_c_w_in, c_w_grp=m_c_w_grp, c_scale=m_c_scale, c_w_out=m_c_w_out, f_w_up=m_f_w_up, f_dw=m_f_dw, f_w_down=m_f_w_down, ln1_g=m_ln1_g, ln1_b=m_ln1_b, ln2_g=m_ln2_g, ln2_b=m_ln2_b)
    var = dict(a_w_in=v_a_w_in, a_dw=v_a_dw, a_dw_b=v_a_dw_b, a_ln_g=v_a_ln_g, a_ln_b=v_a_ln_b, a_w_out=v_a_w_out, b_w_in=v_b_w_in, b_ln_g=v_b_ln_g, b_ln_b=v_b_ln_b, b_ws=v_b_ws, b_bs=v_b_bs, b_w_out=v_b_w_out, c_w_in=v_c_w_in, c_w_grp=v_c_w_grp, c_scale=v_c_scale, c_w_out=v_c_w_out, f_w_up=v_f_w_up, f_dw=v_f_dw, f_w_down=v_f_w_down, ln1_g=v_ln1_g, ln1_b=v_ln1_b, ln2_g=v_ln2_g, ln2_b=v_ln2_b)

    depth = ln1_g.shape[0]
    d = x.shape[-1]
    alpha = float((2 * depth) ** 0.25)
    chip = 2 * lax.axis_index("x") + lax.axis_index("y")
    chip1 = chip.astype(jnp.int32).reshape(1)
    core1 = lax.axis_index("c").astype(jnp.int32).reshape(1)

    kinds = [BIG_KIND[k] for k in BIG]
    early = [(k, 0, 1) for k in FIRST_LAYER]
    late = [(k, 1, w[k].shape[0] - 1) for k in FIRST_LAYER] + [(k, 0, _as3d(w[k]).shape[0]) for k in BIG if k not in FIRST_LAYER]

    def cast_group(group, tag):
        return [_cast_into_full(_as3d(w[k]), l0, nl, BIG_KIND[k], chip1, f"cast_{tag}_{k}") for k, l0, nl in group]

    def as_stacks(group, arrays):
        return {k: (a.reshape(a.shape[0], -1, a.shape[-1]) if BIG_KIND[k] == "row" else a)
                for (k, _, _), a in zip(group, arrays)}

    early_full = as_stacks(early, _allgather_weights(cast_group(early, "first"), [BIG_KIND[k] for k, _, _ in early]))
    late_mine = cast_group(late, "rest")
    late_step1, late_step2 = _allgather_steps(late_mine, [BIG_KIND[k] for k, _, _ in late])
    late_full = {}

    def weight(k, l):
        if k in FIRST_LAYER:
            return (early_full[k], 0) if l == 0 else (late_full[k], l - 1)
        return late_full[k], l

    small_all = _allgather_small(_pack([w[k] for k in SHARDED_SMALL]), "allgather_small_params")
    other_core = 1 - lax.axis_index("c")
    per_chip = [_unpack(lax.dynamic_index_in_dim(small_all, 2 * k + other_core, keepdims=False),
                        [w[n].shape for n in SHARDED_SMALL]) for k in range(N_CHIPS)]
    fs = {n: jnp.concatenate([per_chip[k][i] for k in range(N_CHIPS)], axis=-1) for i, n in enumerate(SHARDED_SMALL)}

    nh = b_ws.shape[1]
    tril = jnp.tril(jnp.ones((CHUNK, CHUNK), F32))
    wm = (b_ws[0] * tril).astype(BF16)
    wmt = jnp.swapaxes(wm, 1, 2)
    bs_exp = jnp.repeat(jnp.transpose(b_bs[0]), CHUNK, axis=1)

    xh, g, b = x[0], jnp.ones((1, d), F32), jnp.zeros((1, d), F32)
    saved = []
    for i in range(depth):
        kind, j = i % 3, i // 3
        rec = dict(xin=xh, gin=g, bin=b)
        if kind == 0:
            hook = _exchange_hook(late_mine, late_step1) if i == 0 else None
            (xh1, rstd1, p, chat, rstdc), landed = _conv_fwd(
                xh, g, b, weight("a_w_in", j)[0], weight("a_w_out", j)[0], weight("a_w_in", j)[1], fs["a_dw"], j,
                fs["a_dw_b"][j:j + 1], fs["a_ln_g"][j:j + 1], fs["a_ln_b"][j:j + 1], alpha, f"conv_fwd_{i}", hook)
            rec.update(p=p, chat=chat, rstdc=rstdc)
        elif kind == 1:
            xh1, rstd1, zp = _sgu_fwd(xh, g, b, late_full["b_w_in"], b_ln_g, b_ln_b, wm, bs_exp, late_full["b_w_out"],
                                      alpha, f"sgu_fwd_{i}")
            rec.update(zp=zp)
        else:
            xh1, rstd1, ys = _pool_fwd(xh, g, b, late_full["c_w_in"], late_full["c_w_grp"], fs["c_scale"],
                                       late_full["c_w_out"], alpha, f"pool_fwd_{i}")
            rec.update(ys=ys)
        hook = _exchange_hook(landed, late_step2) if i == 0 else None
        (xh2, rstd2, hs, hcs), passed_on = _ffn_fwd(
            xh1, ln1_g[i:i + 1], ln1_b[i:i + 1], weight("f_w_up", i)[0], weight("f_w_down", i)[0],
            weight("f_w_up", i)[1], fs["f_dw"], i, alpha, f"ffn_fwd_{i}", hook)
        if i == 0:
            late_full = as_stacks(late, passed_on)
        rec.update(xh1=xh1, rstd1=rstd1, xh2=xh2, rstd2=rstd2, hs=hs, hcs=hcs)
        saved.append(rec)
        xh, g, b = xh2, ln2_g[i:i + 1], ln2_b[i:i + 1]

    dxo, loss_part = _loss_head(xh, g, b, loss_target[0], "loss_head")
    loss = lax.psum(loss_part[0, 0], ("x", "y", "c"))

    def stack_shape(k):
        l, r, c = _as3d(w[k]).shape
        return (l, N_CHIPS * r, c) if BIG_KIND[k] == "row" else (l, r, N_CHIPS * c)

    gbuf = {k: lax.empty(stack_shape(k), F32) for k in BIG if k != "c_w_grp"}
    gs = {k: [None] * w[k].shape[0] for k in ("a_dw", "a_dw_b", "a_ln_g", "a_ln_b", "f_dw", "ln1_g", "ln1_b", "ln2_g", "ln2_b")}
    for i in reversed(range(depth)):
        kind, j = i % 3, i // 3
        rec = saved[i]
        dr2, dhc, gbuf["f_w_down"], gs["ln2_g"][i], gs["ln2_b"][i] = _ffn_bwd1(
            dxo, rec["xh2"], rec["rstd2"], ln2_g[i:i + 1], rec["hcs"], *weight("f_w_down", i), i,
            gbuf["f_w_down"], f"ffn_bwd1_{i}")
        dr1, dh, gs["f_dw"][i], gs["ln1_g"][i], gs["ln1_b"][i] = _ffn_bwd2(
            dhc, rec["hs"], dr2, *weight("f_w_up", i), fs["f_dw"], i, rec["xh1"], rec["rstd1"], ln1_g[i:i + 1],
            alpha, f"ffn_bwd2_{i}")
        gbuf["f_w_up"] = _mm_tn(rec["xh1"], ln1_g[i:i + 1], ln1_b[i:i + 1], dh, gbuf["f_w_up"], i, f"grad_w_up_{i}")
        if kind == 0:
            dp, gbuf["a_w_out"], gs["a_dw"][j], gs["a_dw_b"][j], gs["a_ln_g"][j], gs["a_ln_b"][j] = _conv_bwd1(
                dr1, rec["chat"], rec["rstdc"], rec["p"], *weight("a_w_out", j), fs["a_dw"], j,
                fs["a_ln_g"][j:j + 1], fs["a_ln_b"][j:j + 1], gbuf["a_w_out"], f"conv_bwd1_{i}")
            win_name, lidx = "a_w_in", j
        elif kind == 1:
            dp, gbuf["b_w_out"], g_ws, g_bs_t, g_blg, g_blb = _sgu_bwd1(
                dr1, rec["zp"], late_full["b_w_out"], b_ln_g, b_ln_b, wm, wmt, bs_exp, gbuf["b_w_out"],
                f"sgu_bwd1_{i}")
            win_name, lidx = "b_w_in", 0
        else:
            dp, gbuf["c_w_out"], g_wgrp, g_cscale = _pool_bwd1(
                dr1, rec["ys"], late_full["c_w_out"], late_full["c_w_grp"], fs["c_scale"], gbuf["c_w_out"],
                f"pool_bwd1_{i}")
            win_name, lidx = "c_w_in", 0
        dxo = _bwd_in(dp, dr1, *weight(win_name, lidx), alpha, f"mixer_bwd2_{i}")
        gbuf[win_name] = _mm_tn(rec["xin"], rec["gin"], rec["bin"], dp, gbuf[win_name], lidx, f"grad_w_in_{i}")
    grad_x = dxo[None]

    gfull = []
    for k, kind in zip(BIG, kinds):
        a = g_wgrp if k == "c_w_grp" else gbuf[k]
        gfull.append(a.reshape(a.shape[0], N_CHIPS, -1, a.shape[-1]) if kind == "row" else a)
    got = _rs_pair(gfull, kinds)
    pair_sum = [_pair_sum(a, t, kind, core1, "rs_pair_sum_" + k) for k, kind, a, t in zip(BIG, kinds, gfull, got)]
    from_chips = _rs_chips(pair_sum, kinds)
    half_sum = [_chip_sum(t, rb, kind, chip1, "rs_chip_sum_" + k)
                for k, kind, t, rb in zip(BIG, kinds, pair_sum, from_chips)]
    other_half = _rs_join(half_sum)
    grads, delta, new_m, new_v = {}, {}, {}, {}
    for k, own, oth in zip(BIG, half_sum, other_half):
        outs = _adamw_big(_as3d(w[k]), _as3d(mom[k]), _as3d(var[k]), own, oth, core1, "adamw_" + k)
        grads[k], delta[k], new_m[k], new_v[k] = [o.reshape(w[k].shape) for o in outs]

    small_full = {
        "a_dw": jnp.stack(gs["a_dw"]), "a_dw_b": jnp.concatenate(gs["a_dw_b"]), "a_ln_g": jnp.concatenate(gs["a_ln_g"]),
        "a_ln_b": jnp.concatenate(gs["a_ln_b"]), "c_scale": g_cscale, "f_dw": jnp.stack(gs["f_dw"]),
        "b_ln_g": g_blg, "b_ln_b": g_blb, "b_ws": g_ws[None], "b_bs": jnp.transpose(g_bs_t)[None],
        "ln1_g": jnp.concatenate(gs["ln1_g"]), "ln1_b": jnp.concatenate(gs["ln1_b"]),
        "ln2_g": jnp.concatenate(gs["ln2_g"]), "ln2_b": jnp.concatenate(gs["ln2_b"]),
    }
    small_names = SHARDED_SMALL + REPLICATED
    small_shapes = [small_full[n].shape for n in small_names]
    small_packed = _pack([small_full[n] for n in small_names])
    gathered_small = _allgather_small(small_packed, "allgather_small_grads")
    me1 = (2 * chip + lax.axis_index("c")).astype(jnp.int32).reshape(1)
    summed = _unpack(_sum_devices(small_packed, gathered_small, me1, "small_grad_sum"), small_shapes)
    for n, a in zip(small_names, summed):
        if n in SHARDED_SMALL:
            cs = w[n].shape[-1]
            a = lax.dynamic_slice_in_dim(a, chip * cs, cs, axis=a.ndim - 1)
        grads[n] = a

    shapes = [w[n].shape for n in small_names]
    ds_, ms_, vs_ = _adamw(_pack([w[n] for n in small_names]), _pack([grads[n] for n in small_names]),
                           _pack([mom[n] for n in small_names]), _pack([var[n] for n in small_names]), "adamw_small")
    for n, a, bb, cc in zip(small_names, _unpack(ds_, shapes), _unpack(ms_, shapes), _unpack(vs_, shapes)):
        delta[n], new_m[n], new_v[n] = a, bb, cc

    return (loss, grad_x, *[grads[n] for n in WEIGHTS], *[delta[n] for n in WEIGHTS],
            *[new_m[n] for n in WEIGHTS], *[new_v[n] for n in WEIGHTS])
```

```python
import math

import jax
import jax.numpy as jnp
from jax import lax
from jax.experimental import pallas as pl
from jax.experimental.pallas import tpu as pltpu

F32 = jnp.float32
BF16 = jnp.bfloat16

LN_EPS = 1e-5
POOL_WINDOWS = (2, 4, 8, 16)
CHUNK = 128
ADAM_LR = 0.001
ADAM_B1 = 0.9
ADAM_B2 = 0.999
ADAM_EPS = 1e-08
ADAM_WD = 0.01
ADAM_STEP = 10

LANES = 128
SUBLANES_BF16 = 16
N_CHIPS = 4
N_DEV = 8
VMEM_LIMIT = 60 * 1024 * 1024

TM_FFN = 512
TM_FFN_BWD1 = 256
TM_CONV = 256
TM_SGU = 512
TM_POOL = 512
TM_BWD_IN = 512
TS_MM_TN = 1024
CW_FFN = 256
CW_FFN_BWD2 = 512
CONV_HALO = 32
CONV_ROW_BLOCK = 64
POOL_HALO = 16
FFN_HALO = 16

MESH = pl.DeviceIdType.MESH


def _cparams(n_grid=1, parallel=False):
    sem = ("parallel" if parallel else "arbitrary",) * n_grid
    return pltpu.CompilerParams(dimension_semantics=sem, vmem_limit_bytes=VMEM_LIMIT)


def _resident(block, imap):
    return pl.BlockSpec(block, imap, pipeline_mode=pl.Buffered(1))


def _wspec(w, l):
    _, r, c = w.shape
    return _resident((None, r, c), lambda *_: (l, 0, 0))


def _rowspec(d):
    return pl.BlockSpec((1, d), lambda *_: (0, 0))


def _dot(a, b):
    return jnp.dot(a, b, preferred_element_type=F32)


def _dot_nt(a, b):
    return lax.dot_general(a, b, (((1,), (1,)), ((), ())), preferred_element_type=F32)


def _dot_tn(a, b):
    return lax.dot_general(a, b, (((0,), (0,)), ((), ())), preferred_element_type=F32)


def _sigmoid(x):
    return jax.nn.sigmoid(x)


def _ln_stats(r):
    mu = jnp.mean(r, axis=1, keepdims=True)
    xc = r - mu
    var = jnp.mean(xc * xc, axis=1, keepdims=True)
    rstd = lax.rsqrt(var + LN_EPS)
    return xc * rstd, rstd


def _ln_bwd(dy, xhat, rstd, g):
    dxh = dy * g
    m1 = jnp.mean(dxh, axis=1, keepdims=True)
    m2 = jnp.mean(dxh * xhat, axis=1, keepdims=True)
    return rstd * (dxh - m1 - xhat * m2)


def _colsum(v):
    return jnp.sum(v, axis=0, keepdims=True)


def _gelu(z):
    return 0.5 * z * (1.0 + lax.erf(z * (1.0 / math.sqrt(2.0))))


def _gelu_grad(z):
    cdf = 0.5 * (1.0 + lax.erf(z * (1.0 / math.sqrt(2.0))))
    pdf = jnp.exp(-0.5 * z * z) * (1.0 / math.sqrt(2.0 * math.pi))
    return cdf + z * pdf


def _shift_down(v, k, prev_rows):
    rolled = pltpu.roll(v, k, 0)
    head = rolled[0:8]
    rows = lax.broadcasted_iota(jnp.int32, head.shape, 0)
    for r in range(k):
        head = jnp.where(rows == r, prev_rows[k - 1 - r], head)
    return jnp.concatenate([head, rolled[8:]], axis=0)


def _shift_up(v, k, next_rows):
    tm = v.shape[0]
    rolled = pltpu.roll(v, tm - k, 0)
    tail = rolled[tm - 8:tm]
    rows = lax.broadcasted_iota(jnp.int32, tail.shape, 0)
    for r in range(k):
        tail = jnp.where(rows == 8 - k + r, next_rows[r], tail)
    return jnp.concatenate([rolled[0:tm - 8], tail], axis=0)


def _fill_shifted(base_scr, sh_scr):
    nrows = sh_scr.shape[1]
    for r in range(1, 8):
        sh_scr[r - 1, :, :] = base_scr[pl.ds(r, nrows), :]


def _tap(base_scr, sh_scr, off, r0, nrows, cols):
    q, r = divmod(off, 8)
    if r == 0:
        return base_scr[pl.ds(r0 + 8 * q, nrows), cols]
    return sh_scr[r - 1, pl.ds(r0 + 8 * q, nrows), cols]


def _pick_rows(r, c, itemsize, cap_bytes):
    best = None
    for t in range(16, r + 1, 16):
        if r % t == 0 and t * c * itemsize <= cap_bytes:
            best = t
    return best if best is not None else r


def _ffn_conv_cols(h, dw_ref, c0, cw, prev1, prev2):
    kw = dw_ref.shape[0]
    h1 = _shift_down(h, 1, [prev1])
    h2 = _shift_down(h, 2, [prev1, prev2])
    hc = dw_ref[kw - 1:kw, c0:c0 + cw] * h + dw_ref[kw - 2:kw - 1, c0:c0 + cw] * h1 + dw_ref[kw - 3:kw - 2, c0:c0 + cw] * h2
    return hc, h1, h2


def _ffn_fwd(xh1, g1, b1, wup, wdn, lw, fdw, l, alpha, name, hook=None):
    s, d = xh1.shape
    f2 = wup.shape[2]
    f = f2 // 2
    tm = min(TM_FFN, s)
    cw = min(CW_FFN, f)
    n, nck = s // tm, f // cw
    assert fdw.shape[1] == 3 and s % tm == 0 and f % cw == 0

    def body(xh_ref, g_ref, b_ref, wup_ref, dw_ref, wdn_ref, xo_ref, rs_ref, hs_ref, hcs_ref, carry):
        @pl.when(pl.program_id(0) == 0)
        def _():
            carry[...] = jnp.zeros_like(carry)

        x1 = xh_ref[...] * g_ref[...] + b_ref[...]
        xb = x1.astype(BF16)
        o = jnp.zeros((tm, d), F32)

        def up_proj(j):
            return [_dot(xb, wup_ref[:, half * f + j * cw:half * f + (j + 1) * cw]) for half in range(2)]

        ahead = up_proj(0)
        for j in range(nck):
            hh = ahead
            if j + 1 < nck:
                ahead = up_proj(j + 1)
            parts = []
            for half in range(2):
                c0 = half * f + j * cw
                h = hh[half]
                hs_ref[:, c0:c0 + cw] = h.astype(BF16)
                hc, _, _ = _ffn_conv_cols(h, dw_ref, c0, cw, carry[7:8, c0:c0 + cw], carry[6:7, c0:c0 + cw])
                carry[:, c0:c0 + cw] = h[tm - 8:tm, :]
                hcs_ref[:, c0:c0 + cw] = hc.astype(BF16)
                parts.append(hc)
            gg, vv = parts
            a = (gg * _sigmoid(gg) * vv).astype(BF16)
            o = o + _dot(a, wdn_ref[j * cw:(j + 1) * cw, :])
        xhat, rstd = _ln_stats(alpha * x1 + o)
        xo_ref[...] = xhat
        rs_ref[...] = rstd

    tile = pl.BlockSpec((tm, d), lambda i: (i, 0))
    return _hosted_call(
        body, hook, n,
        in_specs=[tile, _rowspec(d), _rowspec(d), _wspec(wup, lw),
                  pl.BlockSpec((None, 3, f2), lambda i: (l, 0, 0)), _wspec(wdn, lw)],
        out_specs=[tile, pl.BlockSpec((tm, 1), lambda i: (i, 0)), pl.BlockSpec((tm, f2), lambda i: (i, 0)),
                   pl.BlockSpec((tm, f2), lambda i: (i, 0))],
        out_shape=[jax.ShapeDtypeStruct((s, d), F32), jax.ShapeDtypeStruct((s, 1), F32),
                   jax.ShapeDtypeStruct((s, f2), BF16), jax.ShapeDtypeStruct((s, f2), BF16)],
        scratch_shapes=[pltpu.VMEM((8, f2), F32)],
        args=(xh1, g1, b1, wup, fdw, wdn), name=name)


def _ffn_bwd1(dx2, xh2, rstd2, g2, hcs, wdn, lw, gwdn_buf, l, name, hook=None):
    s, d = dx2.shape
    f2 = hcs.shape[1]
    f = f2 // 2
    tm = min(TM_FFN_BWD1, s)
    cw = min(CW_FFN, f)
    n, nck = s // tm, f // cw

    def body(dx_ref, xh_ref, rs_ref, g_ref, hcs_ref, wdn_ref, buf_ref,
             dr_ref, dhc_ref, gwdn_ref, gg_ref, gb_ref):
        @pl.when(pl.program_id(0) == 0)
        def _():
            gwdn_ref[...] = jnp.zeros_like(gwdn_ref)
            gg_ref[...] = jnp.zeros_like(gg_ref)
            gb_ref[...] = jnp.zeros_like(gb_ref)

        dx = dx_ref[...]
        xh = xh_ref[...]
        gg_ref[...] += _colsum(dx * xh)
        gb_ref[...] += _colsum(dx)
        dr = _ln_bwd(dx, xh, rs_ref[...], g_ref[...])
        dr_ref[...] = dr
        dob = dr.astype(BF16)

        def d_act(j):
            return _dot_nt(dob, wdn_ref[j * cw:(j + 1) * cw, :])

        da_ahead = d_act(0)
        for j in range(nck):
            da = da_ahead
            if j + 1 < nck:
                da_ahead = d_act(j + 1)
            gt = hcs_ref[:, j * cw:(j + 1) * cw].astype(F32)
            vv = hcs_ref[:, f + j * cw:f + (j + 1) * cw].astype(F32)
            sg = _sigmoid(gt)
            sl = gt * sg
            a = (sl * vv).astype(BF16)
            gwdn_ref[j * cw:(j + 1) * cw, :] += _dot_tn(a, dob)
            dhc_ref[:, j * cw:(j + 1) * cw] = (da * vv * (sg * (1.0 + gt * (1.0 - sg)))).astype(BF16)
            dhc_ref[:, f + j * cw:f + (j + 1) * cw] = (da * sl).astype(BF16)

    tile = pl.BlockSpec((tm, d), lambda i: (i, 0))
    wide = pl.BlockSpec((tm, f2), lambda i: (i, 0))
    nl = gwdn_buf.shape[0]
    return _hosted_call(
        body, hook, n,
        in_specs=[tile, tile, pl.BlockSpec((tm, 1), lambda i: (i, 0)), _rowspec(d), wide,
                  _wspec(wdn, lw), pl.BlockSpec(memory_space=pl.ANY)],
        out_specs=[tile, wide, pl.BlockSpec((None, f, d), lambda i: (l, 0, 0)), _rowspec(d), _rowspec(d)],
        out_shape=[jax.ShapeDtypeStruct((s, d), F32), jax.ShapeDtypeStruct((s, f2), BF16),
                   jax.ShapeDtypeStruct((nl, f, d), F32),
                   jax.ShapeDtypeStruct((1, d), F32), jax.ShapeDtypeStruct((1, d), F32)],
        scratch_shapes=[], args=(dx2, xh2, rstd2, g2, hcs, wdn, gwdn_buf), name=name, aliases={6: 2})


def _bwd_in(dp, dres, w, l, alpha, name):
    s, d = dres.shape
    nn = dp.shape[1]
    tm = min(TM_BWD_IN, s)
    n = s // tm
    tile = pl.BlockSpec((tm, d), lambda i: (i, 0))

    def body(dp_ref, dres_ref, w_ref, o_ref):
        o_ref[...] = alpha * dres_ref[...] + _dot_nt(dp_ref[...], w_ref[...])

    return pl.pallas_call(
        body, grid=(n,),
        in_specs=[pl.BlockSpec((tm, nn), lambda i: (i, 0)), tile, _wspec(w, l)],
        out_specs=tile, out_shape=jax.ShapeDtypeStruct((s, d), F32),
        compiler_params=_cparams(parallel=True), name=name,
    )(dp, dres, w)


def _ffn_bwd2(dhc, hs, dres, wup, lw, fdw, l, xh, rstd, g, alpha, name):
    s, d = dres.shape
    f2 = dhc.shape[1]
    tm = min(TM_BWD_IN, s)
    n = s // tm
    hb = FFN_HALO
    cw = min(CW_FFN_BWD2, f2)
    nck = f2 // cw
    halo_blocks = tm // hb
    assert f2 % cw == 0 and fdw.shape[1] == 3

    def body(dhc_ref, halo_ref, hs_ref, dres_ref, w_ref, dw_ref, xh_ref, rs_ref, g_ref,
             o_ref, dh_ref, gdw_ref, gg_ref, gb_ref):
        i = pl.program_id(0)

        @pl.when(i == 0)
        def _():
            gdw_ref[...] = jnp.zeros_like(gdw_ref)
            gg_ref[...] = jnp.zeros_like(gg_ref)
            gb_ref[...] = jnp.zeros_like(gb_ref)

        has_next = i < n - 1
        dx = alpha * dres_ref[...]
        for j in range(nck):
            c0 = j * cw
            dc = dhc_ref[:, c0:c0 + cw].astype(F32)
            hal = jnp.where(has_next, halo_ref[:, c0:c0 + cw].astype(F32), 0.0)
            nxt = [hal[0:1], hal[1:2]]
            u1 = _shift_up(dc, 1, nxt[:1])
            u2 = _shift_up(dc, 2, nxt)
            h = hs_ref[:, c0:c0 + cw].astype(F32)
            gdw_ref[2:3, c0:c0 + cw] += _colsum(dc * h)
            gdw_ref[1:2, c0:c0 + cw] += _colsum(u1 * h)
            gdw_ref[0:1, c0:c0 + cw] += _colsum(u2 * h)
            dh = (dw_ref[2:3, c0:c0 + cw] * dc + dw_ref[1:2, c0:c0 + cw] * u1
                  + dw_ref[0:1, c0:c0 + cw] * u2).astype(BF16)
            dh_ref[:, c0:c0 + cw] = dh
            dx = dx + _dot_nt(dh, w_ref[:, c0:c0 + cw])
        xhv = xh_ref[...]
        gg_ref[...] += _colsum(dx * xhv)
        gb_ref[...] += _colsum(dx)
        o_ref[...] = _ln_bwd(dx, xhv, rs_ref[...], g_ref[...])

    tile = pl.BlockSpec((tm, d), lambda i: (i, 0))
    wide = pl.BlockSpec((tm, f2), lambda i: (i, 0))
    return pl.pallas_call(
        body, grid=(n,),
        in_specs=[wide, pl.BlockSpec((hb, f2), lambda i: (jnp.minimum((i + 1) * halo_blocks, s // hb - 1), 0)),
                  wide, tile, _wspec(wup, lw), pl.BlockSpec((None, 3, f2), lambda i: (l, 0, 0)), tile,
                  pl.BlockSpec((tm, 1), lambda i: (i, 0)), _rowspec(d)],
        out_specs=[tile, wide, pl.BlockSpec((3, f2), lambda i: (0, 0)), _rowspec(d), _rowspec(d)],
        out_shape=[jax.ShapeDtypeStruct((s, d), F32), jax.ShapeDtypeStruct((s, f2), BF16),
                   jax.ShapeDtypeStruct((3, f2), F32),
                   jax.ShapeDtypeStruct((1, d), F32), jax.ShapeDtypeStruct((1, d), F32)],
        compiler_params=_cparams(), name=name,
    )(dhc, dhc, hs, dres, wup, fdw, xh, rstd, g)


def _mm_tn(a, ga, ba, bm, buf, l, name):
    s, k = a.shape
    nn = bm.shape[1]
    ts = min(TS_MM_TN, s)
    tn = nn // N_CHIPS if nn > 1024 else nn
    nj, ns = nn // tn, s // ts

    def body(a_ref, g_ref, b_ref, bm_ref, buf_ref, o_ref):
        @pl.when(pl.program_id(1) == 0)
        def _():
            o_ref[...] = jnp.zeros_like(o_ref)

        ab = (a_ref[...] * g_ref[...] + b_ref[...]).astype(BF16)
        o_ref[...] += _dot_tn(ab, bm_ref[...])

    return pl.pallas_call(
        body, grid=(nj, ns),
        in_specs=[pl.BlockSpec((ts, k), lambda j, t: (t, 0)), _rowspec(k), _rowspec(k),
                  pl.BlockSpec((ts, tn), lambda j, t: (t, j)), pl.BlockSpec(memory_space=pl.ANY)],
        out_specs=pl.BlockSpec((None, k, tn), lambda j, t: (l, 0, j)),
        out_shape=jax.ShapeDtypeStruct(buf.shape, F32),
        input_output_aliases={4: 0},
        compiler_params=_cparams(2), name=name,
    )(a, ga, ba, bm, buf)


def _conv_fwd(xin, gin, bin_, win, wout, lw, adw, l, adwb, lng, lnb, alpha, name, hook=None):
    s, d = xin.shape
    kw = adw.shape[1]
    hb = CONV_HALO
    tm = min(TM_CONV, s)
    n = s // tm
    assert kw - 1 <= hb <= tm

    def body(x_ref, g_ref, b_ref, win_ref, dw_ref, dwb_ref, lng_ref, lnb_ref, wout_ref,
             xo_ref, rs_ref, p_ref, chat_ref, rsc_ref, u_scr, u8_scr):
        @pl.when(pl.program_id(0) == 0)
        def _():
            u_scr[0:hb, :] = jnp.zeros((hb, d), F32)

        x = x_ref[...] * g_ref[...] + b_ref[...]
        pm = _dot(x.astype(BF16), win_ref[...])
        p_ref[...] = pm.astype(BF16)
        u = pm[:, :d] * _sigmoid(pm[:, d:])
        u_scr[hb:hb + tm, :] = u
        _fill_shifted(u_scr, u8_scr)
        acc = dwb_ref[...] + dw_ref[kw - 1:kw, :] * u
        for k in range(kw - 1):
            acc = acc + dw_ref[k:k + 1, :] * _tap(u_scr, u8_scr, hb - (kw - 1) + k, 0, tm, slice(None))
        u_scr[0:hb, :] = u_scr[tm:tm + hb, :]
        chat, rstdc = _ln_stats(acc)
        chat_ref[...] = chat.astype(BF16)
        rsc_ref[...] = rstdc
        nv = chat * lng_ref[...] + lnb_ref[...]
        sv = (nv * _sigmoid(nv)).astype(BF16)
        xhat, rstd = _ln_stats(alpha * x + _dot(sv, wout_ref[...]))
        xo_ref[...] = xhat
        rs_ref[...] = rstd

    tile = pl.BlockSpec((tm, d), lambda i: (i, 0))
    col = pl.BlockSpec((tm, 1), lambda i: (i, 0))
    return _hosted_call(
        body, hook, n,
        in_specs=[tile, _rowspec(d), _rowspec(d), _wspec(win, lw),
                  pl.BlockSpec((None, kw, d), lambda i: (l, 0, 0)), _rowspec(d), _rowspec(d), _rowspec(d),
                  _wspec(wout, lw)],
        out_specs=[tile, col, pl.BlockSpec((tm, 2 * d), lambda i: (i, 0)), tile, col],
        out_shape=[jax.ShapeDtypeStruct((s, d), F32), jax.ShapeDtypeStruct((s, 1), F32),
                   jax.ShapeDtypeStruct((s, 2 * d), BF16), jax.ShapeDtypeStruct((s, d), BF16),
                   jax.ShapeDtypeStruct((s, 1), F32)],
        scratch_shapes=[pltpu.VMEM((tm + hb, d), F32), pltpu.VMEM((7, tm + hb - 8, d), F32)],
        args=(xin, gin, bin_, win, adw, adwb, lng, lnb, wout), name=name)


def _conv_bwd1(dr1, chat, rstdc, p, wout, lw, adw, lt, lng, lnb, gwout_buf, l, name, hook=None):
    s, d = dr1.shape
    kw = adw.shape[1]
    hb = CONV_HALO
    tm = min(TM_CONV, s)
    n = s // tm
    halo_blocks = tm // hb
    rbl = CONV_ROW_BLOCK

    def body(dr_ref, chat_ref, rsc_ref, p_ref, halo_ref, wout_ref, dw_ref, lng_ref, lnb_ref, buf_ref,
             dp_ref, gwout_ref, gdw_ref, gdwb_ref, glng_ref, glnb_ref, u_scr, dc_scr, u8_scr, dc8_scr):
        i = pl.program_id(0)
        t = n - 1 - i

        @pl.when(i == 0)
        def _():
            dc_scr[tm:tm + hb, :] = jnp.zeros((hb, d), F32)
            gwout_ref[...] = jnp.zeros_like(gwout_ref)
            gdw_ref[...] = jnp.zeros_like(gdw_ref)
            gdwb_ref[...] = jnp.zeros_like(gdwb_ref)
            glng_ref[...] = jnp.zeros_like(glng_ref)
            glnb_ref[...] = jnp.zeros_like(glnb_ref)

        dob = dr_ref[...].astype(BF16)
        chat = chat_ref[...].astype(F32)
        lng = lng_ref[...]
        nv = chat * lng + lnb_ref[...]
        sgn = _sigmoid(nv)
        gwout_ref[...] += _dot_tn((nv * sgn).astype(BF16), dob)
        dn = _dot_nt(dob, wout_ref[...]) * (sgn * (1.0 + nv * (1.0 - sgn)))
        glng_ref[...] += _colsum(dn * chat)
        glnb_ref[...] += _colsum(dn)
        dc = _ln_bwd(dn, chat, rsc_ref[...], lng)
        gdwb_ref[...] += _colsum(dc)

        pm = p_ref[...].astype(F32)
        a = pm[:, :d]
        sg = _sigmoid(pm[:, d:])
        ph = halo_ref[...].astype(F32)
        u_scr[0:hb, :] = jnp.where(t > 0, ph[:, :d] * _sigmoid(ph[:, d:]), 0.0)
        u_scr[hb:hb + tm, :] = a * sg
        dc_scr[0:tm, :] = dc
        _fill_shifted(u_scr, u8_scr)
        _fill_shifted(dc_scr, dc8_scr)
        du = dw_ref[kw - 1:kw, :] * dc
        for k in range(kw - 1):
            du = du + dw_ref[k:k + 1, :] * _tap(dc_scr, dc8_scr, kw - 1 - k, 0, tm, slice(None))
        for cb in range(d // LANES):
            cols = pl.ds(cb * LANES, LANES)

            def rows_step(rb, accs, cols=cols):
                r0 = pl.multiple_of(rb * rbl, rbl)
                dcb = dc_scr[pl.ds(r0, rbl), cols]
                out = []
                for k in range(kw):
                    prod = dcb * _tap(u_scr, u8_scr, hb - (kw - 1) + k, r0, rbl, cols)
                    part = prod[0:8]
                    for g8 in range(1, rbl // 8):
                        part = part + prod[8 * g8:8 * g8 + 8]
                    out.append(accs[k] + part)
                return tuple(out)

            accs = lax.fori_loop(0, tm // rbl, rows_step, tuple(jnp.zeros((8, LANES), F32) for _ in range(kw)))
            for k in range(kw):
                gdw_ref[k:k + 1, cols] += _colsum(accs[k])
        dc_scr[tm:tm + hb, :] = dc[0:hb, :]
        dp_ref[:, :d] = (du * sg).astype(BF16)
        dp_ref[:, d:] = (du * a * sg * (1.0 - sg)).astype(BF16)

    tile = pl.BlockSpec((tm, d), lambda i: (n - 1 - i, 0))
    col = pl.BlockSpec((tm, 1), lambda i: (n - 1 - i, 0))
    nl = gwout_buf.shape[0]
    return _hosted_call(
        body, hook, n,
        in_specs=[tile, tile, col, pl.BlockSpec((tm, 2 * d), lambda i: (n - 1 - i, 0)),
                  pl.BlockSpec((hb, 2 * d), lambda i: (jnp.maximum((n - 1 - i) * halo_blocks - 1, 0), 0)),
                  _wspec(wout, lw), pl.BlockSpec((None, kw, d), lambda i: (lt, 0, 0)), _rowspec(d), _rowspec(d),
                  pl.BlockSpec(memory_space=pl.ANY)],
        out_specs=[pl.BlockSpec((tm, 2 * d), lambda i: (n - 1 - i, 0)),
                   pl.BlockSpec((None, d, d), lambda i: (l, 0, 0)),
                   pl.BlockSpec((kw, d), lambda i: (0, 0)), _rowspec(d), _rowspec(d), _rowspec(d)],
        out_shape=[jax.ShapeDtypeStruct((s, 2 * d), BF16), jax.ShapeDtypeStruct((nl, d, d), F32),
                   jax.ShapeDtypeStruct((kw, d), F32), jax.ShapeDtypeStruct((1, d), F32),
                   jax.ShapeDtypeStruct((1, d), F32), jax.ShapeDtypeStruct((1, d), F32)],
        scratch_shapes=[pltpu.VMEM((tm + hb, d), F32), pltpu.VMEM((tm + hb, d), F32),
                        pltpu.VMEM((7, tm + hb - 8, d), F32), pltpu.VMEM((7, tm + hb - 8, d), F32)],
        args=(dr1, chat, rstdc, p, p, wout, adw, lng, lnb, gwout_buf), name=name, aliases={9: 1})


def _sgu_gate(vn, wm_ref, bs_ref, s_scr, tm, nh):
    for ch in range(tm // CHUNK):
        r0 = ch * CHUNK
        for h in range(nh):
            c0 = h * CHUNK
            s_scr[r0:r0 + CHUNK, c0:c0 + CHUNK] = (
                _dot(wm_ref[h], vn[r0:r0 + CHUNK, c0:c0 + CHUNK]) + bs_ref[:, c0:c0 + CHUNK])


def _sgu_fwd(xin, gin, bin_, win, lg, lb, wm, bs_exp, wout, alpha, name):
    s, d = xin.shape
    nh = wm.shape[0]
    tm = min(TM_SGU, s)
    n = s // tm
    assert tm % CHUNK == 0 and nh * CHUNK == d

    def body(x_ref, g_ref, b_ref, win_ref, lg_ref, lb_ref, wm_ref, bs_ref, wout_ref,
             xo_ref, rs_ref, zp_ref, s_scr):
        x = x_ref[...] * g_ref[...] + b_ref[...]
        zp = _dot(x.astype(BF16), win_ref[...])
        zp_ref[...] = zp.astype(BF16)
        z = _gelu(zp)
        vhat, _ = _ln_stats(z[:, d:])
        vn = (vhat * lg_ref[...] + lb_ref[...]).astype(BF16)
        _sgu_gate(vn, wm_ref, bs_ref, s_scr, tm, nh)
        q = (z[:, :d] * s_scr[...]).astype(BF16)
        xhat, rstd = _ln_stats(alpha * x + _dot(q, wout_ref[...]))
        xo_ref[...] = xhat
        rs_ref[...] = rstd

    tile = pl.BlockSpec((tm, d), lambda i: (i, 0))
    return pl.pallas_call(
        body, grid=(n,),
        in_specs=[tile, _rowspec(d), _rowspec(d), _wspec(win, 0), _rowspec(d), _rowspec(d),
                  _resident((nh, CHUNK, CHUNK), lambda i: (0, 0, 0)),
                  _resident((CHUNK, d), lambda i: (0, 0)), _wspec(wout, 0)],
        out_specs=[tile, pl.BlockSpec((tm, 1), lambda i: (i, 0)), pl.BlockSpec((tm, 2 * d), lambda i: (i, 0))],
        out_shape=[jax.ShapeDtypeStruct((s, d), F32), jax.ShapeDtypeStruct((s, 1), F32),
                   jax.ShapeDtypeStruct((s, 2 * d), BF16)],
        scratch_shapes=[pltpu.VMEM((tm, d), F32)],
        compiler_params=_cparams(parallel=True), name=name,
    )(xin, gin, bin_, win, lg, lb, wm, bs_exp, wout)


def _sgu_bwd1(dr1, zp, wout, lg, lb, wm, wmt, bs_exp, gwout_buf, name):
    s, d = dr1.shape
    nh = wm.shape[0]
    tm = min(TM_SGU, s)
    n = s // tm

    def body(dr_ref, zp_ref, wout_ref, lg_ref, lb_ref, wm_ref, wmt_ref, bs_ref, buf_ref,
             dzp_ref, gwout_ref, gws_ref, gbs_ref, glg_ref, glb_ref, s_scr, dvn_scr, bs_acc):
        i = pl.program_id(0)

        @pl.when(i == 0)
        def _():
            gwout_ref[...] = jnp.zeros_like(gwout_ref)
            gws_ref[...] = jnp.zeros_like(gws_ref)
            glg_ref[...] = jnp.zeros_like(glg_ref)
            glb_ref[...] = jnp.zeros_like(glb_ref)
            bs_acc[...] = jnp.zeros_like(bs_acc)

        dob = dr_ref[...].astype(BF16)
        zp = zp_ref[...].astype(F32)
        z = _gelu(zp)
        u = z[:, :d]
        lg = lg_ref[...]
        vhat, rstdv = _ln_stats(z[:, d:])
        vn = (vhat * lg + lb_ref[...]).astype(BF16)
        _sgu_gate(vn, wm_ref, bs_ref, s_scr, tm, nh)
        sv = s_scr[...]
        gwout_ref[...] += _dot_tn((u * sv).astype(BF16), dob)
        dq = _dot_nt(dob, wout_ref[...])
        ds = dq * u
        dsb = ds.astype(BF16)
        part = jnp.zeros((CHUNK, d), F32)
        for ch in range(tm // CHUNK):
            r0 = ch * CHUNK
            part = part + ds[r0:r0 + CHUNK, :]
            for h in range(nh):
                c0 = h * CHUNK
                blk = dsb[r0:r0 + CHUNK, c0:c0 + CHUNK]
                gws_ref[h] += _dot_nt(blk, vn[r0:r0 + CHUNK, c0:c0 + CHUNK])
                dvn_scr[r0:r0 + CHUNK, c0:c0 + CHUNK] = _dot(wmt_ref[h], blk)
        bs_acc[...] += part
        dvn = dvn_scr[...]
        glg_ref[...] += _colsum(dvn * vhat)
        glb_ref[...] += _colsum(dvn)
        dv = _ln_bwd(dvn, vhat, rstdv, lg)
        gp = _gelu_grad(zp)
        dzp_ref[:, :d] = (dq * sv * gp[:, :d]).astype(BF16)
        dzp_ref[:, d:] = (dv * gp[:, d:]).astype(BF16)

        @pl.when(i == n - 1)
        def _():
            rows = lax.broadcasted_iota(jnp.int32, (CHUNK, CHUNK), 0)
            cols = lax.broadcasted_iota(jnp.int32, (CHUNK, CHUNK), 1)
            tril = (cols <= rows).astype(F32)
            acc = bs_acc[...]
            for h in range(nh):
                gws_ref[h] = gws_ref[h] * tril
                gbs_ref[:, h:h + 1] = jnp.sum(acc[:, h * CHUNK:(h + 1) * CHUNK], axis=1, keepdims=True)

    tile = pl.BlockSpec((tm, d), lambda i: (i, 0))
    wide = pl.BlockSpec((tm, 2 * d), lambda i: (i, 0))
    hspec = _resident((nh, CHUNK, CHUNK), lambda i: (0, 0, 0))
    return pl.pallas_call(
        body, grid=(n,),
        in_specs=[tile, wide, _wspec(wout, 0), _rowspec(d), _rowspec(d), hspec, hspec,
                  _resident((CHUNK, d), lambda i: (0, 0)), pl.BlockSpec(memory_space=pl.ANY)],
        out_specs=[wide, pl.BlockSpec((None, d, d), lambda i: (0, 0, 0)),
                   pl.BlockSpec((nh, CHUNK, CHUNK), lambda i: (0, 0, 0)),
                   pl.BlockSpec((CHUNK, nh), lambda i: (0, 0)), _rowspec(d), _rowspec(d)],
        out_shape=[jax.ShapeDtypeStruct((s, 2 * d), BF16), jax.ShapeDtypeStruct(gwout_buf.shape, F32),
                   jax.ShapeDtypeStruct((nh, CHUNK, CHUNK), F32), jax.ShapeDtypeStruct((CHUNK, nh), F32),
                   jax.ShapeDtypeStruct((1, d), F32), jax.ShapeDtypeStruct((1, d), F32)],
        scratch_shapes=[pltpu.VMEM((tm, d), F32), pltpu.VMEM((tm, d), F32), pltpu.VMEM((CHUNK, d), F32)],
        input_output_aliases={8: 1},
        compiler_params=_cparams(), name=name,
    )(dr1, zp, wout, lg, lb, wm, wmt, bs_exp, gwout_buf)


def _pool_counts(t0, tm, w):
    pos = t0 + lax.broadcasted_iota(jnp.int32, (tm, 1), 0)
    return jnp.minimum(pos + 1, w).astype(F32)


def _pool_fwd(xin, gin, bin_, win, wg, scale, wout, alpha, name):
    s, d = xin.shape
    ng, dg = wg.shape[0], wg.shape[1]
    hb = POOL_HALO
    tm = min(TM_POOL, s)
    n = s // tm
    assert ng == len(POOL_WINDOWS) and ng * dg == d and max(POOL_WINDOWS) <= hb

    def body(x_ref, g_ref, b_ref, win_ref, wg_ref, sc_ref, wout_ref, xo_ref, rs_ref, ys_ref, y_scr, z_scr):
        i = pl.program_id(0)

        @pl.when(i == 0)
        def _():
            y_scr[0:hb, :] = jnp.zeros((hb, d), F32)

        x = x_ref[...] * g_ref[...] + b_ref[...]
        y = _dot(x.astype(BF16), win_ref[...])
        ys_ref[...] = y.astype(BF16)
        y_scr[hb:hb + tm, :] = y
        for g, w in enumerate(POOL_WINDOWS):
            c0 = g * dg
            acc = y[:, c0:c0 + dg]
            for dd in range(1, w):
                acc = acc + y_scr[pl.ds(hb - dd, tm), c0:c0 + dg]
            pg = acc / _pool_counts(i * tm, tm, w) - y[:, c0:c0 + dg]
            z_scr[:, c0:c0 + dg] = _dot(pg.astype(BF16), wg_ref[g])
        y_scr[0:hb, :] = y_scr[tm:tm + hb, :]
        zz = (z_scr[...] * sc_ref[...]).astype(BF16)
        xhat, rstd = _ln_stats(alpha * x + _dot(zz, wout_ref[...]))
        xo_ref[...] = xhat
        rs_ref[...] = rstd

    tile = pl.BlockSpec((tm, d), lambda i: (i, 0))
    return pl.pallas_call(
        body, grid=(n,),
        in_specs=[tile, _rowspec(d), _rowspec(d), _wspec(win, 0),
                  _resident((ng, dg, dg), lambda i: (0, 0, 0)), _rowspec(d), _wspec(wout, 0)],
        out_specs=[tile, pl.BlockSpec((tm, 1), lambda i: (i, 0)), tile],
        out_shape=[jax.ShapeDtypeStruct((s, d), F32), jax.ShapeDtypeStruct((s, 1), F32),
                   jax.ShapeDtypeStruct((s, d), BF16)],
        scratch_shapes=[pltpu.VMEM((tm + hb, d), F32), pltpu.VMEM((tm, d), F32)],
        compiler_params=_cparams(), name=name,
    )(xin, gin, bin_, win, wg, scale, wout)


def _pool_bwd1(dr1, ys, wout, wg, scale, gwout_buf, name):
    s, d = dr1.shape
    ng, dg = wg.shape[0], wg.shape[1]
    hb = POOL_HALO
    tm = min(TM_POOL, s)
    n = s // tm
    halo_blocks = tm // hb

    def body(dr_ref, ys_ref, halo_ref, wout_ref, wg_ref, sc_ref, buf_ref,
             dy_ref, gwout_ref, gwg_ref, gsc_ref, y_scr, e_scr, z_scr, dp_scr):
        i = pl.program_id(0)
        t = n - 1 - i

        @pl.when(i == 0)
        def _():
            e_scr[tm:tm + hb, :] = jnp.zeros((hb, d), F32)
            gwout_ref[...] = jnp.zeros_like(gwout_ref)
            gwg_ref[...] = jnp.zeros_like(gwg_ref)
            gsc_ref[...] = jnp.zeros_like(gsc_ref)

        dob = dr_ref[...].astype(BF16)
        y = ys_ref[...].astype(F32)
        y_scr[0:hb, :] = jnp.where(t > 0, halo_ref[...].astype(F32), 0.0)
        y_scr[hb:hb + tm, :] = y
        pgs = []
        for g, w in enumerate(POOL_WINDOWS):
            c0 = g * dg
            acc = y[:, c0:c0 + dg]
            for dd in range(1, w):
                acc = acc + y_scr[pl.ds(hb - dd, tm), c0:c0 + dg]
            pg = (acc / _pool_counts(t * tm, tm, w) - y[:, c0:c0 + dg]).astype(BF16)
            pgs.append(pg)
            z_scr[:, c0:c0 + dg] = _dot(pg, wg_ref[g])
        zpre = z_scr[...]
        sc = sc_ref[...]
        gwout_ref[...] += _dot_tn((zpre * sc).astype(BF16), dob)
        dz = _dot_nt(dob, wout_ref[...])
        gsc_ref[...] += _colsum(dz * zpre)
        dzpre = (dz * sc).astype(BF16)
        for g, w in enumerate(POOL_WINDOWS):
            c0 = g * dg
            dzg = dzpre[:, c0:c0 + dg]
            gwg_ref[g] += _dot_tn(pgs[g], dzg)
            dp = _dot_nt(dzg, wg_ref[g])
            dp_scr[:, c0:c0 + dg] = dp
            e_scr[0:tm, c0:c0 + dg] = dp / _pool_counts(t * tm, tm, w)
        for g, w in enumerate(POOL_WINDOWS):
            c0 = g * dg
            acc = e_scr[0:tm, c0:c0 + dg]
            for dd in range(1, w):
                acc = acc + e_scr[pl.ds(dd, tm), c0:c0 + dg]
            dy_ref[:, c0:c0 + dg] = (acc - dp_scr[:, c0:c0 + dg]).astype(BF16)
        e_scr[tm:tm + hb, :] = e_scr[0:hb, :]

    tile = pl.BlockSpec((tm, d), lambda i: (n - 1 - i, 0))
    return pl.pallas_call(
        body, grid=(n,),
        in_specs=[tile, tile,
                  pl.BlockSpec((hb, d), lambda i: (jnp.maximum((n - 1 - i) * halo_blocks - 1, 0), 0)),
                  _wspec(wout, 0), _resident((ng, dg, dg), lambda i: (0, 0, 0)), _rowspec(d),
                  pl.BlockSpec(memory_space=pl.ANY)],
        out_specs=[tile, pl.BlockSpec((None, d, d), lambda i: (0, 0, 0)),
                   pl.BlockSpec((ng, dg, dg), lambda i: (0, 0, 0)), _rowspec(d)],
        out_shape=[jax.ShapeDtypeStruct((s, d), BF16), jax.ShapeDtypeStruct(gwout_buf.shape, F32),
                   jax.ShapeDtypeStruct((ng, dg, dg), F32), jax.ShapeDtypeStruct((1, d), F32)],
        scratch_shapes=[pltpu.VMEM((tm + hb, d), F32), pltpu.VMEM((tm + hb, d), F32),
                        pltpu.VMEM((tm, d), F32), pltpu.VMEM((tm, d), F32)],
        input_output_aliases={6: 1},
        compiler_params=_cparams(), name=name,
    )(dr1, ys, ys, wout, wg, scale, gwout_buf)


def _loss_head(xh, g, b, target, name):
    s, d = xh.shape
    tm = min(512, s)
    n = s // tm

    def body(xh_ref, g_ref, b_ref, t_ref, dy_ref, loss_ref, acc):
        i = pl.program_id(0)

        @pl.when(i == 0)
        def _():
            acc[...] = jnp.zeros_like(acc)

        err = xh_ref[...] * g_ref[...] + b_ref[...] - t_ref[...]
        dy_ref[...] = err * (1.0 / d)
        acc[...] += _colsum(err * err)

        @pl.when(i == n - 1)
        def _():
            loss_ref[...] = (0.5 / d) * jnp.sum(acc[...], axis=1, keepdims=True)

    tile = pl.BlockSpec((tm, d), lambda i: (i, 0))
    return pl.pallas_call(
        body, grid=(n,),
        in_specs=[tile, _rowspec(d), _rowspec(d), tile],
        out_specs=[tile, pl.BlockSpec((1, 1), lambda i: (0, 0))],
        out_shape=[jax.ShapeDtypeStruct((s, d), F32), jax.ShapeDtypeStruct((1, 1), F32)],
        scratch_shapes=[pltpu.VMEM((1, d), F32)],
        compiler_params=_cparams(), name=name,
    )(xh, g, b, target)


def _elementwise(fn, ins, out_dtypes, name):
    shape = ins[0].shape
    c = shape[-1]
    r = math.prod(shape[:-1])
    tr = _pick_rows(r, c, 4, 1 << 20)

    def body(*refs):
        vals = fn(*[ref[...] for ref in refs[:len(ins)]])
        for ref, v in zip(refs[len(ins):], vals):
            ref[...] = v.astype(ref.dtype)

    spec = pl.BlockSpec((tr, c), lambda i: (i, 0))
    outs = pl.pallas_call(
        body, grid=(r // tr,),
        in_specs=[spec] * len(ins), out_specs=[spec] * len(out_dtypes),
        out_shape=[jax.ShapeDtypeStruct((r, c), dt) for dt in out_dtypes],
        compiler_params=_cparams(parallel=True), name=name,
    )(*[a.reshape(r, c) for a in ins])
    return [o.reshape(shape) for o in outs]


def _prefetch_call(body, grid, in_specs, out_specs, out_shape, name, aliases=None):
    return pl.pallas_call(
        body,
        grid_spec=pltpu.PrefetchScalarGridSpec(num_scalar_prefetch=1, grid=grid, in_specs=in_specs, out_specs=out_specs),
        out_shape=out_shape, input_output_aliases=aliases or {},
        compiler_params=_cparams(len(grid), parallel=True), name=name)


def _cast_into_full(w3, l0, l, kind, chip1, name):
    _, r, c = w3.shape
    tr = _pick_rows(r, c, 4, 1 << 20)

    def body(k_ref, w_ref, o_ref):
        o_ref[...] = w_ref[...].astype(BF16)

    if kind == "row":
        out_spec = pl.BlockSpec((None, None, tr, c), lambda a, j, k: (a, k[0], j, 0))
    else:
        out_spec = pl.BlockSpec((None, tr, c), lambda a, j, k: (a, j, k[0]))
    return _prefetch_call(
        body, (l, r // tr), [pl.BlockSpec((None, tr, c), lambda a, j, k: (a + l0, j, 0))], out_spec,
        jax.ShapeDtypeStruct(_full_shape(kind, (l, r, c)), BF16), name)(chip1, w3)


def _pair_sum(g, got, kind, core1, name):
    if kind == "row":
        l, nc, sr, c = g.shape
        g5, got3 = g.reshape(l * nc, 2, sr // 2, c), got.reshape(l * nc, sr // 2, c)
    else:
        l, r, c = g.shape
        g5, got3 = g.reshape(l, 2, r // 2, c), got
    a, _, hr, c = g5.shape
    tr = _pick_rows(hr, c, 4, 1 << 20)

    def body(c_ref, g_ref, t_ref, o_ref):
        o_ref[...] = (g_ref[...] + t_ref[...]).astype(BF16)

    half = pl.BlockSpec((None, tr, c), lambda i, j, cc: (i, j, 0))
    out = _prefetch_call(
        body, (a, hr // tr), [pl.BlockSpec((None, None, tr, c), lambda i, j, cc: (i, cc[0], j, 0)), half], half,
        jax.ShapeDtypeStruct(got3.shape, BF16), name)(core1, g5, got3)
    return out.reshape(got.shape)


def _chip_sum(t, rb, kind, chip1, name):
    _, l, hr, sc = rb.shape
    tr = _pick_rows(hr, sc, 4, 1 << 19)

    def body(k_ref, t_ref, rb_ref, o_ref):
        acc = t_ref[...].astype(F32)
        for r in range(N_CHIPS - 1):
            acc = acc + rb_ref[r].astype(F32)
        o_ref[...] = acc

    if kind == "row":
        t_spec = pl.BlockSpec((None, None, tr, sc), lambda a, j, k: (a, k[0], j, 0))
    else:
        t_spec = pl.BlockSpec((None, tr, sc), lambda a, j, k: (a, j, k[0]))
    return _prefetch_call(
        body, (l, hr // tr),
        [t_spec, pl.BlockSpec((N_CHIPS - 1, None, tr, sc), lambda a, j, k: (0, a, j, 0))],
        pl.BlockSpec((None, tr, sc), lambda a, j, k: (a, j, 0)),
        jax.ShapeDtypeStruct((l, hr, sc), F32), name)(chip1, t, rb)


def _adamw_math(w_, g_, m_, v_):
    m2 = ADAM_B1 * m_ + (1.0 - ADAM_B1) * g_
    v2 = ADAM_B2 * v_ + (1.0 - ADAM_B2) * (g_ * g_)
    m_hat = m2 / (1.0 - ADAM_B1 ** ADAM_STEP)
    v_hat = v2 / (1.0 - ADAM_B2 ** ADAM_STEP)
    delta = -ADAM_LR * (m_hat / (jnp.sqrt(v_hat) + ADAM_EPS) + ADAM_WD * w_)
    return delta, m2, v2


def _adamw_big(w, m, v, own, other, core1, name, l0=0, into=None):
    lg, hr, c = own.shape
    lw = w.shape[0]
    view = lambda a: a.reshape(lw, 2, hr, c)
    tr = _pick_rows(hr, c, 4, 1 << 20)
    n_keep = 0 if into is None else len(into)

    def body(c_ref, w_ref, m_ref, v_ref, own_ref, oth_ref, *rest):
        g_ref, d_ref, m2_ref, v2_ref = rest[n_keep:]
        g = jnp.where(pl.program_id(1) == c_ref[0], own_ref[...], oth_ref[...])
        g_ref[...] = g
        d_ref[...], m2_ref[...], v2_ref[...] = _adamw_math(w_ref[...], g, m_ref[...], v_ref[...])

    s4 = pl.BlockSpec((None, None, tr, c), lambda a, h, j, cc: (a + l0, h, j, 0))
    s3 = pl.BlockSpec((None, tr, c), lambda a, h, j, cc: (a, j, 0))
    aliases = {6 + k: k for k in range(n_keep)}
    return _prefetch_call(
        body, (lg, 2, hr // tr), [s4, s4, s4, s3, s3] + [ANY] * n_keep, [s4] * 4,
        [jax.ShapeDtypeStruct((lw, 2, hr, c), F32)] * 4, name, aliases,
    )(core1, view(w), view(m), view(v), own, other, *(into or []))


def _sum_devices(own, gathered, me1, name):
    r, c = own.shape
    tr = _pick_rows(r, c, 4, 1 << 17)

    def body(me_ref, own_ref, g_ref, o_ref):
        acc = None
        for k in range(N_DEV):
            v = jnp.where(me_ref[0] == k, own_ref[...], g_ref[k])
            acc = v if acc is None else acc + v
        o_ref[...] = acc

    return _prefetch_call(
        body, (r // tr,),
        [pl.BlockSpec((tr, c), lambda i, m: (i, 0)), pl.BlockSpec((N_DEV, tr, c), lambda i, m: (0, i, 0))],
        pl.BlockSpec((tr, c), lambda i, m: (i, 0)), jax.ShapeDtypeStruct((r, c), F32), name)(me1, own, gathered)


def _adamw(w, g, m, v, name):
    return _elementwise(_adamw_math, [w, g, m, v], [F32, F32, F32], name)


ANY = pl.BlockSpec(memory_space=pl.ANY)


def _mesh_pos():
    return lax.axis_index("x"), lax.axis_index("y"), lax.axis_index("c")


def _chip_peers(x, y, c):
    out = []
    for r in (1, 2, 3):
        px = 1 - x if r & 2 else x
        py = 1 - y if r & 1 else y
        out.append((2 * px + py, (px, py, c)))
    return out


def _full_shape(kind, shard_shape):
    l, r, c = shard_shape
    return (l, N_CHIPS, r, c) if kind == "row" else (l, r, N_CHIPS * c)


def _full_piece(ref, kind, k, h, hr, sc):
    rows = pl.ds(pl.multiple_of(h * hr, SUBLANES_BF16), hr)
    if kind == "row":
        return ref.at[:, k, rows, :]
    return ref.at[:, rows, pl.ds(pl.multiple_of(k * sc, LANES), sc)]


def _remote(src, dst, ssem, rsem, dev):
    return pltpu.make_async_remote_copy(src_ref=src, dst_ref=dst, send_sem=ssem, recv_sem=rsem,
                                        device_id=dev, device_id_type=MESH)


DMA_CHUNK_BYTES = 1 << 20
DMA_MAX_CHUNKS = 32


def _chunk_views(src, dst):
    axis = len(src.shape) - 2
    rows = src.shape[axis]
    nbytes = math.prod(src.shape) * jnp.dtype(src.dtype).itemsize
    n = max(1, min(DMA_MAX_CHUNKS, nbytes // DMA_CHUNK_BYTES))
    while n > 1 and (rows % n or (rows // n) % SUBLANES_BF16):
        n -= 1
    cr = rows // n
    out = []
    for i in range(n):
        idx = (slice(None),) * axis + (pl.ds(i * cr, cr), slice(None))
        out.append((src.at[idx], dst.at[idx]))
    return out


def _start_remote(src, dst, ssem, rsem, dev):
    for s, t in _chunk_views(src, dst):
        _remote(s, t, ssem, rsem, dev).start()
    return _remote(src, dst, ssem, rsem, dev)


def _allgather_steps(fulls, kinds):
    nw = len(fulls)

    def dims(a, kind):
        return (a.shape[2] // 2, a.shape[3]) if kind == "row" else (a.shape[1] // 2, a.shape[2] // N_CHIPS)

    hrs = [dims(a, k)[0] for a, k in zip(fulls, kinds)]
    scs = [dims(a, k)[1] for a, k in zip(fulls, kinds)]

    def piece(ref, w, k, h):
        return _full_piece(ref, kinds[w], k, h, hrs[w], scs[w])

    def copies1(src, dst, sems, start):
        x, y, c = _mesh_pos()
        k_me = 2 * x + y
        out = []
        for w in range(nw):
            for r, (kj, dev) in enumerate(_chip_peers(x, y, c)):
                args = (sems[0].at[3 * w + r], sems[1].at[3 * w + r], dev)
                if start:
                    out.append(_start_remote(piece(src[w], w, k_me, c), piece(dst[w], w, k_me, c), *args))
                else:
                    out.append(_remote(piece(src[w], w, k_me, c), piece(dst[w], w, kj, c), *args))
        return out

    def copies2(src, dst, sems, start):
        x, y, c = _mesh_pos()
        out = []
        for w in range(nw):
            for r, (kj, _) in enumerate(_chip_peers(x, y, c)):
                args = (sems[0].at[3 * w + r], sems[1].at[3 * w + r], (x, y, 1 - c))
                if start:
                    out.append(_start_remote(piece(src[w], w, kj, c), piece(dst[w], w, kj, c), *args))
                else:
                    out.append(_remote(piece(src[w], w, kj, 1 - c), piece(dst[w], w, kj, 1 - c), *args))
        return out

    def finish(copies):
        def fn(src, dst, sems):
            for cp in copies(src, dst, sems, False):
                cp.wait_recv()
            for cp in copies(src, dst, sems, False):
                cp.wait_send()
        return fn

    step1 = (lambda s, d, m: copies1(s, d, m, True), finish(copies1))
    step2 = (lambda s, d, m: copies2(s, d, m, True), finish(copies2))
    return step1, step2


def _exchange_hook(arrays, step, sem_len, out_shapes=None):
    shapes = out_shapes or [jax.ShapeDtypeStruct(a.shape, a.dtype) for a in arrays]
    return dict(arrays=list(arrays), out_shapes=shapes, in_place=out_shapes is None, sem_len=sem_len,
                first=step[0], last=step[1])


def _exchange_call(hook, name):
    nh, nho = len(hook["arrays"]), len(hook["out_shapes"])

    def body(*refs):
        h_in, h_out, sems = refs[:nh], refs[nh:nh + nho], refs[nh + nho:]
        hook["first"](h_in, h_out, sems)
        hook["last"](h_in, h_out, sems)

    return pl.pallas_call(
        body, in_specs=[ANY] * nh, out_specs=[ANY] * nho, out_shape=hook["out_shapes"],
        scratch_shapes=[pltpu.SemaphoreType.DMA((hook["sem_len"],))] * 2,
        input_output_aliases={k: k for k in range(nh)} if hook["in_place"] else {}, name=name,
    )(*hook["arrays"])


def _allgather_weights(fulls, kinds):
    nw = len(fulls)
    step1, step2 = _allgather_steps(fulls, kinds)

    def body(*refs):
        mine, fu = refs[:nw], refs[nw:2 * nw]
        sems1, sems2 = refs[2 * nw:2 * nw + 2], refs[2 * nw + 2:]
        step1[0](mine, fu, sems1)
        step1[1](mine, fu, sems1)
        step2[0](fu, fu, sems2)
        step2[1](fu, fu, sems2)

    return pl.pallas_call(
        body,
        in_specs=[ANY] * nw, out_specs=[ANY] * nw,
        out_shape=[jax.ShapeDtypeStruct(a.shape, a.dtype) for a in fulls],
        scratch_shapes=[pltpu.SemaphoreType.DMA((3 * nw,))] * 4,
        input_output_aliases={w: w for w in range(nw)},
        name="allgather_weights",
    )(*fulls)


def _hosted(body, n_in, n_out, hook, n_steps):
    if hook is None:
        return body
    nh, nho = len(hook["arrays"]), len(hook["out_shapes"])

    def wrapped(*refs):
        ins, h_in = refs[:n_in], refs[n_in:n_in + nh]
        outs = refs[n_in + nh:n_in + nh + n_out]
        h_out = refs[n_in + nh + n_out:n_in + nh + n_out + nho]
        rest = refs[n_in + nh + n_out + nho:]
        scr, sems = rest[:-2], rest[-2:]
        i = pl.program_id(0)

        @pl.when(i == 0)
        def _():
            hook["first"](h_in, h_out, sems)

        body(*ins, *outs, *scr)

        @pl.when(i == n_steps - 1)
        def _():
            hook["last"](h_in, h_out, sems)

    return wrapped


def _hosted_call(body, hook, n_steps, in_specs, out_specs, out_shape, scratch_shapes, args, name, aliases=None):
    n_in, n_out = len(in_specs), len(out_specs)
    aliases = dict(aliases or {})
    if hook is not None:
        nh = len(hook["arrays"])
        in_specs = list(in_specs) + [ANY] * nh
        out_specs = list(out_specs) + [ANY] * len(hook["out_shapes"])
        out_shape = list(out_shape) + list(hook["out_shapes"])
        scratch_shapes = list(scratch_shapes) + [pltpu.SemaphoreType.DMA((hook["sem_len"],))] * 2
        if hook["in_place"]:
            aliases.update({n_in + k: n_out + k for k in range(nh)})
        args = list(args) + hook["arrays"]
    outs = pl.pallas_call(
        _hosted(body, n_in, n_out, hook, n_steps), grid=(n_steps,),
        in_specs=in_specs, out_specs=out_specs, out_shape=out_shape, scratch_shapes=scratch_shapes,
        input_output_aliases=aliases, compiler_params=_cparams(), name=name,
    )(*args)
    return outs[:n_out], outs[n_out:]


def _pair_exchange(copies):
    def finish(src, dst, sems):
        for cp in copies(src, dst, sems, False):
            cp.wait_recv()
        for cp in copies(src, dst, sems, False):
            cp.wait_send()
    return (lambda s, d, m: copies(s, d, m, True), finish)


def _rs_pair(fulls, kinds):
    nw = len(fulls)

    def half_all(ref, kind, h):
        if kind == "row":
            hr = ref.shape[2] // 2
            return ref.at[:, :, pl.ds(pl.multiple_of(h * hr, SUBLANES_BF16), hr), :]
        hr = ref.shape[1] // 2
        return ref.at[:, pl.ds(pl.multiple_of(h * hr, SUBLANES_BF16), hr), :]

    def half_shape(kind, shape):
        if kind == "row":
            return (shape[0], shape[1], shape[2] // 2, shape[3])
        return (shape[0], shape[1] // 2, shape[2])

    def copies(g, got, sems, start):
        x, y, c = _mesh_pos()
        make = _start_remote if start else _remote
        return [make(half_all(g[w], kinds[w], 1 - c), got[w], sems[0].at[w], sems[1].at[w], (x, y, 1 - c))
                for w in range(nw)]

    shapes = [jax.ShapeDtypeStruct(half_shape(k, a.shape), a.dtype) for k, a in zip(kinds, fulls)]
    return _exchange_hook(fulls, _pair_exchange(copies), nw, shapes)


def _rs_chips(parts, kinds):
    nw = len(parts)

    def slot(ref, kind, k):
        if kind == "row":
            return ref.at[:, k]
        sc = ref.shape[2] // N_CHIPS
        return ref.at[:, :, pl.ds(pl.multiple_of(k * sc, LANES), sc)]

    def slot_shape(kind, shape):
        if kind == "row":
            return (shape[0], shape[2], shape[3])
        return (shape[0], shape[1], shape[2] // N_CHIPS)

    def copies(t, rb, sems, start):
        x, y, c = _mesh_pos()
        make = _start_remote if start else _remote
        return [make(slot(t[w], kinds[w], kj), rb[w].at[r], sems[0].at[3 * w + r], sems[1].at[3 * w + r], dev)
                for w in range(nw) for r, (kj, dev) in enumerate(_chip_peers(x, y, c))]

    shapes = [jax.ShapeDtypeStruct((N_CHIPS - 1,) + slot_shape(k, a.shape), a.dtype) for k, a in zip(kinds, parts)]
    return _exchange_hook(parts, _pair_exchange(copies), 3 * nw, shapes)


def _rs_join(halves):
    nw = len(halves)

    def copies(src, dst, sems, start):
        x, y, c = _mesh_pos()
        make = _start_remote if start else _remote
        return [make(src[w], dst[w], sems[0].at[w], sems[1].at[w], (x, y, 1 - c)) for w in range(nw)]

    return _exchange_hook(halves, _pair_exchange(copies), nw,
                          [jax.ShapeDtypeStruct(a.shape, a.dtype) for a in halves])


def _allgather_small(buf, name):
    def body(in_ref, out_ref, ssem, rsem):
        x, y, c = _mesh_pos()
        me = 4 * x + 2 * y + c
        cps, waits = [], []
        for r in range(1, N_DEV):
            px = 1 - x if r & 4 else x
            py = 1 - y if r & 2 else y
            pc = 1 - c if r & 1 else c
            cp = _remote(in_ref, out_ref.at[me], ssem.at[r - 1], rsem.at[r - 1], (px, py, pc))
            cp.start()
            cps.append(cp)
            waits.append(_remote(in_ref, out_ref.at[4 * px + 2 * py + pc], ssem.at[r - 1], rsem.at[r - 1], (px, py, pc)))
        for wt in waits:
            wt.wait_recv()
        for cp in cps:
            cp.wait_send()

    return pl.pallas_call(
        body, in_specs=[ANY], out_specs=ANY,
        out_shape=jax.ShapeDtypeStruct((N_DEV,) + buf.shape, buf.dtype),
        scratch_shapes=[pltpu.SemaphoreType.DMA((N_DEV - 1,))] * 2,
        name=name,
    )(buf)


def _pack(arrs):
    flat = jnp.concatenate([a.reshape(-1).astype(F32) for a in arrs])
    rows = -(-flat.shape[0] // (LANES * 16)) * 16
    return jnp.pad(flat, (0, rows * LANES - flat.shape[0])).reshape(rows, LANES)


def _unpack(buf, shapes):
    flat = buf.reshape(-1)
    out, off = [], 0
    for shp in shapes:
        nel = math.prod(shp)
        out.append(flat[off:off + nel].reshape(shp))
        off += nel
    return out


BIG = ("a_w_in", "a_w_out", "b_w_in", "b_w_out", "c_w_in", "c_w_grp", "c_w_out", "f_w_up", "f_w_down")
BIG_KIND = {"a_w_in": "col", "a_w_out": "row", "b_w_in": "col", "b_w_out": "row", "c_w_in": "row",
            "c_w_grp": "row", "c_w_out": "row", "f_w_up": "col", "f_w_down": "row"}
FIRST_LAYER = ("a_w_in", "a_w_out", "f_w_up", "f_w_down")
SHARDED_SMALL = ("a_dw", "a_dw_b", "a_ln_g", "a_ln_b", "c_scale", "f_dw")
REPLICATED = ("b_ln_g", "b_ln_b", "b_ws", "b_bs", "ln1_g", "ln1_b", "ln2_g", "ln2_b")
WEIGHTS = ("a_w_in", "a_dw", "a_dw_b", "a_ln_g", "a_ln_b", "a_w_out", "b_w_in", "b_ln_g", "b_ln_b", "b_ws", "b_bs",
           "b_w_out", "c_w_in", "c_w_grp", "c_scale", "c_w_out", "f_w_up", "f_dw", "f_w_down",
           "ln1_g", "ln1_b", "ln2_g", "ln2_b")


def _as3d(a):
    return a.reshape((-1,) + a.shape[-2:])


def kernel(x, a_w_in, a_dw, a_dw_b, a_ln_g, a_ln_b, a_w_out, b_w_in, b_ln_g, b_ln_b, b_ws, b_bs, b_w_out, c_w_in, c_w_grp, c_scale, c_w_out, f_w_up, f_dw, f_w_down, ln1_g, ln1_b, ln2_g, ln2_b, loss_target, m_a_w_in, m_a_dw, m_a_dw_b, m_a_ln_g, m_a_ln_b, m_a_w_out, m_b_w_in, m_b_ln_g, m_b_ln_b, m_b_ws, m_b_bs, m_b_w_out, m_c_w_in, m_c_w_grp, m_c_scale, m_c_w_out, m_f_w_up, m_f_dw, m_f_w_down, m_ln1_g, m_ln1_b, m_ln2_g, m_ln2_b, v_a_w_in, v_a_dw, v_a_dw_b, v_a_ln_g, v_a_ln_b, v_a_w_out, v_b_w_in, v_b_ln_g, v_b_ln_b, v_b_ws, v_b_bs, v_b_w_out, v_c_w_in, v_c_w_grp, v_c_scale, v_c_w_out, v_f_w_up, v_f_dw, v_f_w_down, v_ln1_g, v_ln1_b, v_ln2_g, v_ln2_b):
    w = dict(a_w_in=a_w_in, a_dw=a_dw, a_dw_b=a_dw_b, a_ln_g=a_ln_g, a_ln_b=a_ln_b, a_w_out=a_w_out, b_w_in=b_w_in, b_ln_g=b_ln_g, b_ln_b=b_ln_b, b_ws=b_ws, b_bs=b_bs, b_w_out=b_w_out, c_w_in=c_w_in, c_w_grp=c_w_grp, c_scale=c_scale, c_w_out=c_w_out, f_w_up=f_w_up, f_dw=f_dw, f_w_down=f_w_down, ln1_g=ln1_g, ln1_b=ln1_b, ln2_g=ln2_g, ln2_b=ln2_b)
    mom = dict(a_w_in=m_a_w_in, a_dw=m_a_dw, a_dw_b=m_a_dw_b, a_ln_g=m_a_ln_g, a_ln_b=m_a_ln_b, a_w_out=m_a_w_out, b_w_in=m_b_w_in, b_ln_g=m_b_ln_g, b_ln_b=m_b_ln_b, b_ws=m_b_ws, b_bs=m_b_bs, b_w_out=m_b_w_out, c_w_in=m_c_w_in, c_w_grp=m_c_w_grp, c_scale=m_c_scale, c_w_out=m_c_w_out, f_w_up=m_f_w_up, f_dw=m_f_dw, f_w_down=m_f_w_down, ln1_g=m_ln1_g, ln1_b=m_ln1_b, ln2_g=m_ln2_g, ln2_b=m_ln2_b)
    var = dict(a_w_in=v_a_w_in, a_dw=v_a_dw, a_dw_b=v_a_dw_b, a_ln_g=v_a_ln_g, a_ln_b=v_a_ln_b, a_w_out=v_a_w_out, b_w_in=v_b_w_in, b_ln_g=v_b_ln_g, b_ln_b=v_b_ln_b, b_ws=v_b_ws, b_bs=v_b_bs, b_w_out=v_b_w_out, c_w_in=v_c_w_in, c_w_grp=v_c_w_grp, c_scale=v_c_scale, c_w_out=v_c_w_out, f_w_up=v_f_w_up, f_dw=v_f_dw, f_w_down=v_f_w_down, ln1_g=v_ln1_g, ln1_b=v_ln1_b, ln2_g=v_ln2_g, ln2_b=v_ln2_b)

    depth = ln1_g.shape[0]
    d = x.shape[-1]
    alpha = float((2 * depth) ** 0.25)
    chip = 2 * lax.axis_index("x") + lax.axis_index("y")
    chip1 = chip.astype(jnp.int32).reshape(1)
    core1 = lax.axis_index("c").astype(jnp.int32).reshape(1)

    kinds = [BIG_KIND[k] for k in BIG]
    early = [(k, 0, 1) for k in FIRST_LAYER]
    late = [(k, 1, w[k].shape[0] - 1) for k in FIRST_LAYER] + [(k, 0, _as3d(w[k]).shape[0]) for k in BIG if k not in FIRST_LAYER]

    def cast_group(group, tag):
        return [_cast_into_full(_as3d(w[k]), l0, nl, BIG_KIND[k], chip1, f"cast_{tag}_{k}") for k, l0, nl in group]

    def as_stacks(group, arrays):
        return {k: (a.reshape(a.shape[0], -1, a.shape[-1]) if BIG_KIND[k] == "row" else a)
                for (k, _, _), a in zip(group, arrays)}

    early_full = as_stacks(early, _allgather_weights(cast_group(early, "first"), [BIG_KIND[k] for k, _, _ in early]))
    late_mine = cast_group(late, "rest")
    late_step1, late_step2 = _allgather_steps(late_mine, [BIG_KIND[k] for k, _, _ in late])
    late_full = {}

    def weight(k, l):
        if k in FIRST_LAYER:
            return (early_full[k], 0) if l == 0 else (late_full[k], l - 1)
        return late_full[k], l

    small_all = _allgather_small(_pack([w[k] for k in SHARDED_SMALL]), "allgather_small_params")
    other_core = 1 - lax.axis_index("c")
    per_chip = [_unpack(lax.dynamic_index_in_dim(small_all, 2 * k + other_core, keepdims=False),
                        [w[n].shape for n in SHARDED_SMALL]) for k in range(N_CHIPS)]
    fs = {n: jnp.concatenate([per_chip[k][i] for k in range(N_CHIPS)], axis=-1) for i, n in enumerate(SHARDED_SMALL)}

    nh = b_ws.shape[1]
    tril = jnp.tril(jnp.ones((CHUNK, CHUNK), F32))
    wm = (b_ws[0] * tril).astype(BF16)
    wmt = jnp.swapaxes(wm, 1, 2)
    bs_exp = jnp.repeat(jnp.transpose(b_bs[0]), CHUNK, axis=1)

    xh, g, b = x[0], jnp.ones((1, d), F32), jnp.zeros((1, d), F32)
    saved = []
    for i in range(depth):
        kind, j = i % 3, i // 3
        rec = dict(xin=xh, gin=g, bin=b)
        if kind == 0:
            hook = _exchange_hook(late_mine, late_step1, 3 * len(late)) if i == 0 else None
            (xh1, rstd1, p, chat, rstdc), landed = _conv_fwd(
                xh, g, b, weight("a_w_in", j)[0], weight("a_w_out", j)[0], weight("a_w_in", j)[1], fs["a_dw"], j,
                fs["a_dw_b"][j:j + 1], fs["a_ln_g"][j:j + 1], fs["a_ln_b"][j:j + 1], alpha, f"conv_fwd_{i}", hook)
            rec.update(p=p, chat=chat, rstdc=rstdc)
        elif kind == 1:
            xh1, rstd1, zp = _sgu_fwd(xh, g, b, late_full["b_w_in"], b_ln_g, b_ln_b, wm, bs_exp, late_full["b_w_out"],
                                      alpha, f"sgu_fwd_{i}")
            rec.update(zp=zp)
        else:
            xh1, rstd1, ys = _pool_fwd(xh, g, b, late_full["c_w_in"], late_full["c_w_grp"], fs["c_scale"],
                                       late_full["c_w_out"], alpha, f"pool_fwd_{i}")
            rec.update(ys=ys)
        hook = _exchange_hook(landed, late_step2, 3 * len(late)) if i == 0 else None
        (xh2, rstd2, hs, hcs), passed_on = _ffn_fwd(
            xh1, ln1_g[i:i + 1], ln1_b[i:i + 1], weight("f_w_up", i)[0], weight("f_w_down", i)[0],
            weight("f_w_up", i)[1], fs["f_dw"], i, alpha, f"ffn_fwd_{i}", hook)
        if i == 0:
            late_full = as_stacks(late, passed_on)
        rec.update(xh1=xh1, rstd1=rstd1, xh2=xh2, rstd2=rstd2, hs=hs, hcs=hcs)
        saved.append(rec)
        xh, g, b = xh2, ln2_g[i:i + 1], ln2_b[i:i + 1]

    dxo, loss_part = _loss_head(xh, g, b, loss_target[0], "loss_head")
    loss = lax.psum(loss_part[0, 0], ("x", "y", "c"))

    assert depth >= 3

    def stack_shape(k, nl):
        _, r, c = _as3d(w[k]).shape
        return (nl, N_CHIPS * r, c) if BIG_KIND[k] == "row" else (nl, r, N_CHIPS * c)

    g_first = {k: lax.empty(stack_shape(k, nl), F32) for k, _, nl in early}
    g_rest = {k: lax.empty(stack_shape(k, nl), F32) for k, _, nl in late if k != "c_w_grp"}

    def gslot(k, l):
        if k in FIRST_LAYER:
            return (g_first, 0) if l == 0 else (g_rest, l - 1)
        return g_rest, l

    def rs_views(group, store):
        out = []
        for k, _, _ in group:
            a = store[k]
            out.append(a.reshape(a.shape[0], N_CHIPS, -1, a.shape[-1]) if BIG_KIND[k] == "row" else a)
        return out

    def pair_sums(group, views, got, tag):
        return [_pair_sum(a, t, BIG_KIND[k], core1, f"rs_pair_sum_{tag}_{k}") for (k, _, _), a, t in zip(group, views, got)]

    def reduce_and_join(group, pair, from_chips, tag):
        half = [_chip_sum(t, rb, BIG_KIND[k], chip1, f"rs_chip_sum_{tag}_{k}")
                for (k, _, _), t, rb in zip(group, pair, from_chips)]
        return half, _exchange_call(_rs_join(half), f"rs_join_{tag}")

    early_kinds = [BIG_KIND[k] for k, _, _ in early]
    late_kinds = [BIG_KIND[k] for k, _, _ in late]
    gs = {k: [None] * w[k].shape[0] for k in ("a_dw", "a_dw_b", "a_ln_g", "a_ln_b", "f_dw", "ln1_g", "ln1_b", "ln2_g", "ln2_b")}
    for i in reversed(range(depth)):
        kind, j = i % 3, i // 3
        rec = saved[i]
        hook = None
        if i == 0:
            late_views = rs_views(late, g_rest)
            hook = _rs_pair(late_views, late_kinds)
        st, idx = gslot("f_w_down", i)
        (dr2, dhc, st["f_w_down"], gs["ln2_g"][i], gs["ln2_b"][i]), got = _ffn_bwd1(
            dxo, rec["xh2"], rec["rstd2"], ln2_g[i:i + 1], rec["hcs"], *weight("f_w_down", i),
            st["f_w_down"], idx, f"ffn_bwd1_{i}", hook)
        if i == 0:
            late_pair = pair_sums(late, late_views, got, "rest")
        dr1, dh, gs["f_dw"][i], gs["ln1_g"][i], gs["ln1_b"][i] = _ffn_bwd2(
            dhc, rec["hs"], dr2, *weight("f_w_up", i), fs["f_dw"], i, rec["xh1"], rec["rstd1"], ln1_g[i:i + 1],
            alpha, f"ffn_bwd2_{i}")
        st, idx = gslot("f_w_up", i)
        st["f_w_up"] = _mm_tn(rec["xh1"], ln1_g[i:i + 1], ln1_b[i:i + 1], dh, st["f_w_up"], idx, f"grad_w_up_{i}")
        if kind == 0:
            hook = _rs_chips(late_pair, late_kinds) if i == 0 else None
            st, idx = gslot("a_w_out", j)
            (dp, st["a_w_out"], gs["a_dw"][j], gs["a_dw_b"][j], gs["a_ln_g"][j], gs["a_ln_b"][j]), landed = _conv_bwd1(
                dr1, rec["chat"], rec["rstdc"], rec["p"], *weight("a_w_out", j), fs["a_dw"], j,
                fs["a_ln_g"][j:j + 1], fs["a_ln_b"][j:j + 1], st["a_w_out"], idx, f"conv_bwd1_{i}", hook)
            if i == 0:
                late_from_chips = landed
            win_name, lidx = "a_w_in", j
        elif kind == 1:
            dp, g_rest["b_w_out"], g_ws, g_bs_t, g_blg, g_blb = _sgu_bwd1(
                dr1, rec["zp"], late_full["b_w_out"], b_ln_g, b_ln_b, wm, wmt, bs_exp, g_rest["b_w_out"],
                f"sgu_bwd1_{i}")
            win_name, lidx = "b_w_in", 0
        else:
            dp, g_rest["c_w_out"], g_rest["c_w_grp"], g_cscale = _pool_bwd1(
                dr1, rec["ys"], late_full["c_w_out"], late_full["c_w_grp"], fs["c_scale"], g_rest["c_w_out"],
                f"pool_bwd1_{i}")
            win_name, lidx = "c_w_in", 0
        dxo = _bwd_in(dp, dr1, *weight(win_name, lidx), alpha, f"mixer_bwd2_{i}")
        st, idx = gslot(win_name, lidx)
        st[win_name] = _mm_tn(rec["xin"], rec["gin"], rec["bin"], dp, st[win_name], idx, f"grad_w_in_{i}")
    grad_x = dxo[None]

    late_half, late_other = reduce_and_join(late, late_pair, late_from_chips, "rest")
    early_views = rs_views(early, g_first)
    early_got = _exchange_call(_rs_pair(early_views, early_kinds), "rs_pair_first")
    early_pair = pair_sums(early, early_views, early_got, "first")
    early_from_chips = _exchange_call(_rs_chips(early_pair, early_kinds), "rs_chips_first")
    early_half, early_other = reduce_and_join(early, early_pair, early_from_chips, "first")

    updates = {}
    for (k, l0, _), own, oth in zip(late, late_half, late_other):
        updates[k] = _adamw_big(_as3d(w[k]), _as3d(mom[k]), _as3d(var[k]), own, oth, core1, f"adamw_rest_{k}", l0)
    for (k, l0, _), own, oth in zip(early, early_half, early_other):
        updates[k] = _adamw_big(_as3d(w[k]), _as3d(mom[k]), _as3d(var[k]), own, oth, core1, f"adamw_first_{k}",
                                l0, into=updates[k])
    grads, delta, new_m, new_v = {}, {}, {}, {}
    for k in BIG:
        grads[k], delta[k], new_m[k], new_v[k] = [o.reshape(w[k].shape) for o in updates[k]]

    small_full = {
        "a_dw": jnp.stack(gs["a_dw"]), "a_dw_b": jnp.concatenate(gs["a_dw_b"]), "a_ln_g": jnp.concatenate(gs["a_ln_g"]),
        "a_ln_b": jnp.concatenate(gs["a_ln_b"]), "c_scale": g_cscale, "f_dw": jnp.stack(gs["f_dw"]),
        "b_ln_g": g_blg, "b_ln_b": g_blb, "b_ws": g_ws[None], "b_bs": jnp.transpose(g_bs_t)[None],
        "ln1_g": jnp.concatenate(gs["ln1_g"]), "ln1_b": jnp.concatenate(gs["ln1_b"]),
        "ln2_g": jnp.concatenate(gs["ln2_g"]), "ln2_b": jnp.concatenate(gs["ln2_b"]),
    }
    small_names = SHARDED_SMALL + REPLICATED
    small_shapes = [small_full[n].shape for n in small_names]
    small_packed = _pack([small_full[n] for n in small_names])
    gathered_small = _allgather_small(small_packed, "allgather_small_grads")
    me1 = (2 * chip + lax.axis_index("c")).astype(jnp.int32).reshape(1)
    summed = _unpack(_sum_devices(small_packed, gathered_small, me1, "small_grad_sum"), small_shapes)
    for n, a in zip(small_names, summed):
        if n in SHARDED_SMALL:
            cs = w[n].shape[-1]
            a = lax.dynamic_slice_in_dim(a, chip * cs, cs, axis=a.ndim - 1)
        grads[n] = a

    shapes = [w[n].shape for n in small_names]
    ds_, ms_, vs_ = _adamw(_pack([w[n] for n in small_names]), _pack([grads[n] for n in small_names]),
                           _pack([mom[n] for n in small_names]), _pack([var[n] for n in small_names]), "adamw_small")
    for n, a, bb, cc in zip(small_names, _unpack(ds_, shapes), _unpack(ms_, shapes), _unpack(vs_, shapes)):
        delta[n], new_m[n], new_v[n] = a, bb, cc

    return (loss, grad_x, *[grads[n] for n in WEIGHTS], *[delta[n] for n in WEIGHTS],
            *[new_m[n] for n in WEIGHTS], *[new_v[n] for n in WEIGHTS])
```

```python
import math

import jax
import jax.numpy as jnp
from jax import lax
from jax.experimental import pallas as pl
from jax.experimental.pallas import tpu as pltpu

F32 = jnp.float32
BF16 = jnp.bfloat16

LN_EPS = 1e-5
POOL_WINDOWS = (2, 4, 8, 16)
CHUNK = 128
ADAM_LR = 0.001
ADAM_B1 = 0.9
ADAM_B2 = 0.999
ADAM_EPS = 1e-08
ADAM_WD = 0.01
ADAM_STEP = 10

LANES = 128
SUBLANES_BF16 = 16
N_CHIPS = 4
N_DEV = 8
VMEM_LIMIT = 60 * 1024 * 1024

TM_FFN = 512
TM_FFN_BWD1 = 512
TM_CONV = 256
TM_SGU = 512
TM_POOL = 512
TM_BWD_IN = 512
TS_MM_TN = 1024
CW_FFN = 256
CW_FFN_BWD2 = 512
CONV_HALO = 32
CONV_ROW_BLOCK = 64
POOL_HALO = 16
FFN_HALO = 16

MESH = pl.DeviceIdType.MESH


def _cparams(n_grid=1, parallel=False):
    sem = ("parallel" if parallel else "arbitrary",) * n_grid
    return pltpu.CompilerParams(dimension_semantics=sem, vmem_limit_bytes=VMEM_LIMIT)


def _resident(block, imap):
    return pl.BlockSpec(block, imap, pipeline_mode=pl.Buffered(1))


def _wspec(w, l):
    _, r, c = w.shape
    return _resident((None, r, c), lambda *_: (l, 0, 0))


def _rowspec(d):
    return pl.BlockSpec((1, d), lambda *_: (0, 0))


def _dot(a, b):
    return jnp.dot(a, b, preferred_element_type=F32)


def _dot_nt(a, b):
    return lax.dot_general(a, b, (((1,), (1,)), ((), ())), preferred_element_type=F32)


def _dot_tn(a, b):
    return lax.dot_general(a, b, (((0,), (0,)), ((), ())), preferred_element_type=F32)


def _sigmoid(x):
    return jax.nn.sigmoid(x)


def _ln_stats(r):
    mu = jnp.mean(r, axis=1, keepdims=True)
    xc = r - mu
    var = jnp.mean(xc * xc, axis=1, keepdims=True)
    rstd = lax.rsqrt(var + LN_EPS)
    return xc * rstd, rstd


def _ln_bwd(dy, xhat, rstd, g):
    dxh = dy * g
    m1 = jnp.mean(dxh, axis=1, keepdims=True)
    m2 = jnp.mean(dxh * xhat, axis=1, keepdims=True)
    return rstd * (dxh - m1 - xhat * m2)


def _colsum(v):
    return jnp.sum(v, axis=0, keepdims=True)


def _gelu(z):
    return 0.5 * z * (1.0 + lax.erf(z * (1.0 / math.sqrt(2.0))))


def _gelu_grad(z):
    cdf = 0.5 * (1.0 + lax.erf(z * (1.0 / math.sqrt(2.0))))
    pdf = jnp.exp(-0.5 * z * z) * (1.0 / math.sqrt(2.0 * math.pi))
    return cdf + z * pdf


def _shift_down(v, k, prev_rows):
    rolled = pltpu.roll(v, k, 0)
    head = rolled[0:8]
    rows = lax.broadcasted_iota(jnp.int32, head.shape, 0)
    for r in range(k):
        head = jnp.where(rows == r, prev_rows[k - 1 - r], head)
    return jnp.concatenate([head, rolled[8:]], axis=0)


def _shift_up(v, k, next_rows):
    tm = v.shape[0]
    rolled = pltpu.roll(v, tm - k, 0)
    tail = rolled[tm - 8:tm]
    rows = lax.broadcasted_iota(jnp.int32, tail.shape, 0)
    for r in range(k):
        tail = jnp.where(rows == 8 - k + r, next_rows[r], tail)
    return jnp.concatenate([rolled[0:tm - 8], tail], axis=0)


def _fill_shifted(base_scr, sh_scr):
    nrows = sh_scr.shape[1]
    for r in range(1, 8):
        sh_scr[r - 1, :, :] = base_scr[pl.ds(r, nrows), :]


def _tap(base_scr, sh_scr, off, r0, nrows, cols):
    q, r = divmod(off, 8)
    if r == 0:
        return base_scr[pl.ds(r0 + 8 * q, nrows), cols]
    return sh_scr[r - 1, pl.ds(r0 + 8 * q, nrows), cols]


def _pick_rows(r, c, itemsize, cap_bytes):
    best = None
    for t in range(16, r + 1, 16):
        if r % t == 0 and t * c * itemsize <= cap_bytes:
            best = t
    return best if best is not None else r


def _ffn_conv_cols(h, dw_ref, c0, cw, prev1, prev2):
    kw = dw_ref.shape[0]
    h1 = _shift_down(h, 1, [prev1])
    h2 = _shift_down(h, 2, [prev1, prev2])
    hc = dw_ref[kw - 1:kw, c0:c0 + cw] * h + dw_ref[kw - 2:kw - 1, c0:c0 + cw] * h1 + dw_ref[kw - 3:kw - 2, c0:c0 + cw] * h2
    return hc, h1, h2


def _ffn_fwd(xh1, g1, b1, wup, wdn, lw, fdw, l, alpha, name, hook=None):
    s, d = xh1.shape
    f2 = wup.shape[2]
    f = f2 // 2
    tm = min(TM_FFN, s)
    cw = min(CW_FFN, f)
    n, nck = s // tm, f // cw
    assert fdw.shape[1] == 3 and s % tm == 0 and f % cw == 0

    def body(xh_ref, g_ref, b_ref, wup_ref, dw_ref, wdn_ref, xo_ref, rs_ref, hs_ref, hcs_ref, carry):
        @pl.when(pl.program_id(0) == 0)
        def _():
            carry[...] = jnp.zeros_like(carry)

        x1 = xh_ref[...] * g_ref[...] + b_ref[...]
        xb = x1.astype(BF16)
        o = jnp.zeros((tm, d), F32)

        def up_proj(j):
            return [_dot(xb, wup_ref[:, half * f + j * cw:half * f + (j + 1) * cw]) for half in range(2)]

        ahead = up_proj(0)
        for j in range(nck):
            hh = ahead
            if j + 1 < nck:
                ahead = up_proj(j + 1)
            parts = []
            for half in range(2):
                c0 = half * f + j * cw
                h = hh[half]
                hs_ref[:, c0:c0 + cw] = h.astype(BF16)
                hc, _, _ = _ffn_conv_cols(h, dw_ref, c0, cw, carry[7:8, c0:c0 + cw], carry[6:7, c0:c0 + cw])
                carry[:, c0:c0 + cw] = h[tm - 8:tm, :]
                hcs_ref[:, c0:c0 + cw] = hc.astype(BF16)
                parts.append(hc)
            gg, vv = parts
            a = (gg * _sigmoid(gg) * vv).astype(BF16)
            o = o + _dot(a, wdn_ref[j * cw:(j + 1) * cw, :])
        xhat, rstd = _ln_stats(alpha * x1 + o)
        xo_ref[...] = xhat
        rs_ref[...] = rstd

    tile = pl.BlockSpec((tm, d), lambda i: (i, 0))
    return _hosted_call(
        body, hook, n,
        in_specs=[tile, _rowspec(d), _rowspec(d), _wspec(wup, lw),
                  pl.BlockSpec((None, 3, f2), lambda i: (l, 0, 0)), _wspec(wdn, lw)],
        out_specs=[tile, pl.BlockSpec((tm, 1), lambda i: (i, 0)), pl.BlockSpec((tm, f2), lambda i: (i, 0)),
                   pl.BlockSpec((tm, f2), lambda i: (i, 0))],
        out_shape=[jax.ShapeDtypeStruct((s, d), F32), jax.ShapeDtypeStruct((s, 1), F32),
                   jax.ShapeDtypeStruct((s, f2), BF16), jax.ShapeDtypeStruct((s, f2), BF16)],
        scratch_shapes=[pltpu.VMEM((8, f2), F32)],
        args=(xh1, g1, b1, wup, fdw, wdn), name=name)


def _ffn_bwd1(dx2, xh2, rstd2, g2, b2, hcs, wdn, lw, gwdn_buf, l, name, hook=None, loss_head=False):
    s, d = dx2.shape
    f2 = hcs.shape[1]
    f = f2 // 2
    tm = min(TM_FFN_BWD1, s)
    cw = min(CW_FFN, f)
    n, nck = s // tm, f // cw

    def body(dx_ref, xh_ref, rs_ref, g_ref, b_ref, hcs_ref, wdn_ref, buf_ref,
             dr_ref, dhc_ref, gwdn_ref, gg_ref, gb_ref, loss_ref):
        @pl.when(pl.program_id(0) == 0)
        def _():
            gwdn_ref[...] = jnp.zeros_like(gwdn_ref)
            gg_ref[...] = jnp.zeros_like(gg_ref)
            gb_ref[...] = jnp.zeros_like(gb_ref)
            loss_ref[...] = jnp.zeros_like(loss_ref)

        xh = xh_ref[...]
        if loss_head:
            err = xh * g_ref[...] + b_ref[...] - dx_ref[...]
            dx = err * (1.0 / d)
            loss_ref[...] += (0.5 / d) * jnp.sum(_colsum(err * err), axis=1, keepdims=True)
        else:
            dx = dx_ref[...]
        gg_ref[...] += _colsum(dx * xh)
        gb_ref[...] += _colsum(dx)
        dr = _ln_bwd(dx, xh, rs_ref[...], g_ref[...])
        dr_ref[...] = dr
        dob = dr.astype(BF16)

        def d_act(j):
            return _dot_nt(dob, wdn_ref[j * cw:(j + 1) * cw, :])

        da_ahead = d_act(0)
        for j in range(nck):
            da = da_ahead
            if j + 1 < nck:
                da_ahead = d_act(j + 1)
            gt = hcs_ref[:, j * cw:(j + 1) * cw].astype(F32)
            vv = hcs_ref[:, f + j * cw:f + (j + 1) * cw].astype(F32)
            sg = _sigmoid(gt)
            sl = gt * sg
            a = (sl * vv).astype(BF16)
            gwdn_ref[j * cw:(j + 1) * cw, :] += _dot_tn(a, dob)
            dhc_ref[:, j * cw:(j + 1) * cw] = (da * vv * (sg * (1.0 + gt * (1.0 - sg)))).astype(BF16)
            dhc_ref[:, f + j * cw:f + (j + 1) * cw] = (da * sl).astype(BF16)

    tile = pl.BlockSpec((tm, d), lambda i: (i, 0))
    wide = pl.BlockSpec((tm, f2), lambda i: (i, 0))
    nl = gwdn_buf.shape[0]
    return _hosted_call(
        body, hook, n,
        in_specs=[tile, tile, pl.BlockSpec((tm, 1), lambda i: (i, 0)), _rowspec(d), _rowspec(d), wide,
                  _wspec(wdn, lw), pl.BlockSpec(memory_space=pl.ANY)],
        out_specs=[tile, wide, pl.BlockSpec((None, f, d), lambda i: (l, 0, 0)), _rowspec(d), _rowspec(d),
                   pl.BlockSpec((1, 1), lambda i: (0, 0))],
        out_shape=[jax.ShapeDtypeStruct((s, d), F32), jax.ShapeDtypeStruct((s, f2), BF16),
                   jax.ShapeDtypeStruct((nl, f, d), F32),
                   jax.ShapeDtypeStruct((1, d), F32), jax.ShapeDtypeStruct((1, d), F32),
                   jax.ShapeDtypeStruct((1, 1), F32)],
        scratch_shapes=[], args=(dx2, xh2, rstd2, g2, b2, hcs, wdn, gwdn_buf), name=name, aliases={7: 2})


def _bwd_in(dp, dres, w, l, alpha, name):
    s, d = dres.shape
    nn = dp.shape[1]
    tm = min(TM_BWD_IN, s)
    n = s // tm
    tile = pl.BlockSpec((tm, d), lambda i: (i, 0))

    def body(dp_ref, dres_ref, w_ref, o_ref):
        o_ref[...] = alpha * dres_ref[...] + _dot_nt(dp_ref[...], w_ref[...])

    return pl.pallas_call(
        body, grid=(n,),
        in_specs=[pl.BlockSpec((tm, nn), lambda i: (i, 0)), tile, _wspec(w, l)],
        out_specs=tile, out_shape=jax.ShapeDtypeStruct((s, d), F32),
        compiler_params=_cparams(parallel=True), name=name,
    )(dp, dres, w)


def _ffn_bwd2(dhc, hs, dres, wup, lw, fdw, l, xh, rstd, g, alpha, name):
    s, d = dres.shape
    f2 = dhc.shape[1]
    tm = min(TM_BWD_IN, s)
    n = s // tm
    hb = FFN_HALO
    cw = min(CW_FFN_BWD2, f2)
    nck = f2 // cw
    halo_blocks = tm // hb
    assert f2 % cw == 0 and fdw.shape[1] == 3

    def body(dhc_ref, halo_ref, hs_ref, dres_ref, w_ref, dw_ref, xh_ref, rs_ref, g_ref,
             o_ref, dh_ref, gdw_ref, gg_ref, gb_ref):
        i = pl.program_id(0)

        @pl.when(i == 0)
        def _():
            gdw_ref[...] = jnp.zeros_like(gdw_ref)
            gg_ref[...] = jnp.zeros_like(gg_ref)
            gb_ref[...] = jnp.zeros_like(gb_ref)

        has_next = i < n - 1
        dx = alpha * dres_ref[...]
        for j in range(nck):
            c0 = j * cw
            dc = dhc_ref[:, c0:c0 + cw].astype(F32)
            hal = jnp.where(has_next, halo_ref[:, c0:c0 + cw].astype(F32), 0.0)
            nxt = [hal[0:1], hal[1:2]]
            u1 = _shift_up(dc, 1, nxt[:1])
            u2 = _shift_up(dc, 2, nxt)
            h = hs_ref[:, c0:c0 + cw].astype(F32)
            gdw_ref[2:3, c0:c0 + cw] += _colsum(dc * h)
            gdw_ref[1:2, c0:c0 + cw] += _colsum(u1 * h)
            gdw_ref[0:1, c0:c0 + cw] += _colsum(u2 * h)
            dh = (dw_ref[2:3, c0:c0 + cw] * dc + dw_ref[1:2, c0:c0 + cw] * u1
                  + dw_ref[0:1, c0:c0 + cw] * u2).astype(BF16)
            dh_ref[:, c0:c0 + cw] = dh
            dx = dx + _dot_nt(dh, w_ref[:, c0:c0 + cw])
        xhv = xh_ref[...]
        gg_ref[...] += _colsum(dx * xhv)
        gb_ref[...] += _colsum(dx)
        o_ref[...] = _ln_bwd(dx, xhv, rs_ref[...], g_ref[...])

    tile = pl.BlockSpec((tm, d), lambda i: (i, 0))
    wide = pl.BlockSpec((tm, f2), lambda i: (i, 0))
    return pl.pallas_call(
        body, grid=(n,),
        in_specs=[wide, pl.BlockSpec((hb, f2), lambda i: (jnp.minimum((i + 1) * halo_blocks, s // hb - 1), 0)),
                  wide, tile, _wspec(wup, lw), pl.BlockSpec((None, 3, f2), lambda i: (l, 0, 0)), tile,
                  pl.BlockSpec((tm, 1), lambda i: (i, 0)), _rowspec(d)],
        out_specs=[tile, wide, pl.BlockSpec((3, f2), lambda i: (0, 0)), _rowspec(d), _rowspec(d)],
        out_shape=[jax.ShapeDtypeStruct((s, d), F32), jax.ShapeDtypeStruct((s, f2), BF16),
                   jax.ShapeDtypeStruct((3, f2), F32),
                   jax.ShapeDtypeStruct((1, d), F32), jax.ShapeDtypeStruct((1, d), F32)],
        compiler_params=_cparams(), name=name,
    )(dhc, dhc, hs, dres, wup, fdw, xh, rstd, g)


def _mm_tn(a, ga, ba, bm, buf, l, name):
    s, k = a.shape
    nn = bm.shape[1]
    ts = min(TS_MM_TN, s)
    tn = nn // N_CHIPS if nn > 1024 else nn
    nj, ns = nn // tn, s // ts

    def body(a_ref, g_ref, b_ref, bm_ref, buf_ref, o_ref):
        @pl.when(pl.program_id(1) == 0)
        def _():
            o_ref[...] = jnp.zeros_like(o_ref)

        ab = (a_ref[...] * g_ref[...] + b_ref[...]).astype(BF16)
        o_ref[...] += _dot_tn(ab, bm_ref[...])

    return pl.pallas_call(
        body, grid=(nj, ns),
        in_specs=[pl.BlockSpec((ts, k), lambda j, t: (t, 0)), _rowspec(k), _rowspec(k),
                  pl.BlockSpec((ts, tn), lambda j, t: (t, j)), pl.BlockSpec(memory_space=pl.ANY)],
        out_specs=pl.BlockSpec((None, k, tn), lambda j, t: (l, 0, j)),
        out_shape=jax.ShapeDtypeStruct(buf.shape, F32),
        input_output_aliases={4: 0},
        compiler_params=_cparams(2), name=name,
    )(a, ga, ba, bm, buf)


def _conv_fwd(xin, gin, bin_, win, wout, lw, adw, l, adwb, lng, lnb, alpha, name, hook=None):
    s, d = xin.shape
    kw = adw.shape[1]
    hb = CONV_HALO
    tm = min(TM_CONV, s)
    n = s // tm
    assert kw - 1 <= hb <= tm

    def body(x_ref, g_ref, b_ref, win_ref, dw_ref, dwb_ref, lng_ref, lnb_ref, wout_ref,
             xo_ref, rs_ref, p_ref, chat_ref, rsc_ref, u_scr, u8_scr):
        @pl.when(pl.program_id(0) == 0)
        def _():
            u_scr[0:hb, :] = jnp.zeros((hb, d), F32)

        x = x_ref[...] * g_ref[...] + b_ref[...]
        pm = _dot(x.astype(BF16), win_ref[...])
        p_ref[...] = pm.astype(BF16)
        u = pm[:, :d] * _sigmoid(pm[:, d:])
        u_scr[hb:hb + tm, :] = u
        _fill_shifted(u_scr, u8_scr)
        acc = dwb_ref[...] + dw_ref[kw - 1:kw, :] * u
        for k in range(kw - 1):
            acc = acc + dw_ref[k:k + 1, :] * _tap(u_scr, u8_scr, hb - (kw - 1) + k, 0, tm, slice(None))
        u_scr[0:hb, :] = u_scr[tm:tm + hb, :]
        chat, rstdc = _ln_stats(acc)
        chat_ref[...] = chat.astype(BF16)
        rsc_ref[...] = rstdc
        nv = chat * lng_ref[...] + lnb_ref[...]
        sv = (nv * _sigmoid(nv)).astype(BF16)
        xhat, rstd = _ln_stats(alpha * x + _dot(sv, wout_ref[...]))
        xo_ref[...] = xhat
        rs_ref[...] = rstd

    tile = pl.BlockSpec((tm, d), lambda i: (i, 0))
    col = pl.BlockSpec((tm, 1), lambda i: (i, 0))
    return _hosted_call(
        body, hook, n,
        in_specs=[tile, _rowspec(d), _rowspec(d), _wspec(win, lw),
                  pl.BlockSpec((None, kw, d), lambda i: (l, 0, 0)), _rowspec(d), _rowspec(d), _rowspec(d),
                  _wspec(wout, lw)],
        out_specs=[tile, col, pl.BlockSpec((tm, 2 * d), lambda i: (i, 0)), tile, col],
        out_shape=[jax.ShapeDtypeStruct((s, d), F32), jax.ShapeDtypeStruct((s, 1), F32),
                   jax.ShapeDtypeStruct((s, 2 * d), BF16), jax.ShapeDtypeStruct((s, d), BF16),
                   jax.ShapeDtypeStruct((s, 1), F32)],
        scratch_shapes=[pltpu.VMEM((tm + hb, d), F32), pltpu.VMEM((7, tm + hb - 8, d), F32)],
        args=(xin, gin, bin_, win, adw, adwb, lng, lnb, wout), name=name)


def _conv_bwd1(dr1, chat, rstdc, p, wout, lw, adw, lt, lng, lnb, gwout_buf, l, name, hook=None):
    s, d = dr1.shape
    kw = adw.shape[1]
    hb = CONV_HALO
    tm = min(TM_CONV, s)
    n = s // tm
    halo_blocks = tm // hb
    rbl = CONV_ROW_BLOCK

    def body(dr_ref, chat_ref, rsc_ref, p_ref, halo_ref, wout_ref, dw_ref, lng_ref, lnb_ref, buf_ref,
             dp_ref, gwout_ref, gdw_ref, gdwb_ref, glng_ref, glnb_ref, u_scr, dc_scr, u8_scr, dc8_scr):
        i = pl.program_id(0)
        t = n - 1 - i

        @pl.when(i == 0)
        def _():
            dc_scr[tm:tm + hb, :] = jnp.zeros((hb, d), F32)
            gwout_ref[...] = jnp.zeros_like(gwout_ref)
            gdw_ref[...] = jnp.zeros_like(gdw_ref)
            gdwb_ref[...] = jnp.zeros_like(gdwb_ref)
            glng_ref[...] = jnp.zeros_like(glng_ref)
            glnb_ref[...] = jnp.zeros_like(glnb_ref)

        dob = dr_ref[...].astype(BF16)
        chat = chat_ref[...].astype(F32)
        lng = lng_ref[...]
        nv = chat * lng + lnb_ref[...]
        sgn = _sigmoid(nv)
        gwout_ref[...] += _dot_tn((nv * sgn).astype(BF16), dob)
        dn = _dot_nt(dob, wout_ref[...]) * (sgn * (1.0 + nv * (1.0 - sgn)))
        glng_ref[...] += _colsum(dn * chat)
        glnb_ref[...] += _colsum(dn)
        dc = _ln_bwd(dn, chat, rsc_ref[...], lng)
        gdwb_ref[...] += _colsum(dc)

        pm = p_ref[...].astype(F32)
        a = pm[:, :d]
        sg = _sigmoid(pm[:, d:])
        ph = halo_ref[...].astype(F32)
        u_scr[0:hb, :] = jnp.where(t > 0, ph[:, :d] * _sigmoid(ph[:, d:]), 0.0)
        u_scr[hb:hb + tm, :] = a * sg
        dc_scr[0:tm, :] = dc
        _fill_shifted(u_scr, u8_scr)
        _fill_shifted(dc_scr, dc8_scr)
        du = dw_ref[kw - 1:kw, :] * dc
        for k in range(kw - 1):
            du = du + dw_ref[k:k + 1, :] * _tap(dc_scr, dc8_scr, kw - 1 - k, 0, tm, slice(None))
        for cb in range(d // LANES):
            cols = pl.ds(cb * LANES, LANES)

            def rows_step(rb, accs, cols=cols):
                r0 = pl.multiple_of(rb * rbl, rbl)
                dcb = dc_scr[pl.ds(r0, rbl), cols]
                out = []
                for k in range(kw):
                    prod = dcb * _tap(u_scr, u8_scr, hb - (kw - 1) + k, r0, rbl, cols)
                    part = prod[0:8]
                    for g8 in range(1, rbl // 8):
                        part = part + prod[8 * g8:8 * g8 + 8]
                    out.append(accs[k] + part)
                return tuple(out)

            accs = lax.fori_loop(0, tm // rbl, rows_step, tuple(jnp.zeros((8, LANES), F32) for _ in range(kw)))
            for k in range(kw):
                gdw_ref[k:k + 1, cols] += _colsum(accs[k])
        dc_scr[tm:tm + hb, :] = dc[0:hb, :]
        dp_ref[:, :d] = (du * sg).astype(BF16)
        dp_ref[:, d:] = (du * a * sg * (1.0 - sg)).astype(BF16)

    tile = pl.BlockSpec((tm, d), lambda i: (n - 1 - i, 0))
    col = pl.BlockSpec((tm, 1), lambda i: (n - 1 - i, 0))
    nl = gwout_buf.shape[0]
    return _hosted_call(
        body, hook, n,
        in_specs=[tile, tile, col, pl.BlockSpec((tm, 2 * d), lambda i: (n - 1 - i, 0)),
                  pl.BlockSpec((hb, 2 * d), lambda i: (jnp.maximum((n - 1 - i) * halo_blocks - 1, 0), 0)),
                  _wspec(wout, lw), pl.BlockSpec((None, kw, d), lambda i: (lt, 0, 0)), _rowspec(d), _rowspec(d),
                  pl.BlockSpec(memory_space=pl.ANY)],
        out_specs=[pl.BlockSpec((tm, 2 * d), lambda i: (n - 1 - i, 0)),
                   pl.BlockSpec((None, d, d), lambda i: (l, 0, 0)),
                   pl.BlockSpec((kw, d), lambda i: (0, 0)), _rowspec(d), _rowspec(d), _rowspec(d)],
        out_shape=[jax.ShapeDtypeStruct((s, 2 * d), BF16), jax.ShapeDtypeStruct((nl, d, d), F32),
                   jax.ShapeDtypeStruct((kw, d), F32), jax.ShapeDtypeStruct((1, d), F32),
                   jax.ShapeDtypeStruct((1, d), F32), jax.ShapeDtypeStruct((1, d), F32)],
        scratch_shapes=[pltpu.VMEM((tm + hb, d), F32), pltpu.VMEM((tm + hb, d), F32),
                        pltpu.VMEM((7, tm + hb - 8, d), F32), pltpu.VMEM((7, tm + hb - 8, d), F32)],
        args=(dr1, chat, rstdc, p, p, wout, adw, lng, lnb, gwout_buf), name=name, aliases={9: 1})


def _sgu_gate(vn, wm_ref, bs_ref, s_scr, tm, nh):
    for ch in range(tm // CHUNK):
        r0 = ch * CHUNK
        for h in range(nh):
            c0 = h * CHUNK
            s_scr[r0:r0 + CHUNK, c0:c0 + CHUNK] = (
                _dot(wm_ref[h], vn[r0:r0 + CHUNK, c0:c0 + CHUNK]) + bs_ref[:, c0:c0 + CHUNK])


def _sgu_fwd(xin, gin, bin_, win, lg, lb, wm, bs_exp, wout, alpha, name):
    s, d = xin.shape
    nh = wm.shape[0]
    tm = min(TM_SGU, s)
    n = s // tm
    assert tm % CHUNK == 0 and nh * CHUNK == d

    def body(x_ref, g_ref, b_ref, win_ref, lg_ref, lb_ref, wm_ref, bs_ref, wout_ref,
             xo_ref, rs_ref, zp_ref, s_scr):
        x = x_ref[...] * g_ref[...] + b_ref[...]
        zp = _dot(x.astype(BF16), win_ref[...])
        zp_ref[...] = zp.astype(BF16)
        z = _gelu(zp)
        vhat, _ = _ln_stats(z[:, d:])
        vn = (vhat * lg_ref[...] + lb_ref[...]).astype(BF16)
        _sgu_gate(vn, wm_ref, bs_ref, s_scr, tm, nh)
        q = (z[:, :d] * s_scr[...]).astype(BF16)
        xhat, rstd = _ln_stats(alpha * x + _dot(q, wout_ref[...]))
        xo_ref[...] = xhat
        rs_ref[...] = rstd

    tile = pl.BlockSpec((tm, d), lambda i: (i, 0))
    return pl.pallas_call(
        body, grid=(n,),
        in_specs=[tile, _rowspec(d), _rowspec(d), _wspec(win, 0), _rowspec(d), _rowspec(d),
                  _resident((nh, CHUNK, CHUNK), lambda i: (0, 0, 0)),
                  _resident((CHUNK, d), lambda i: (0, 0)), _wspec(wout, 0)],
        out_specs=[tile, pl.BlockSpec((tm, 1), lambda i: (i, 0)), pl.BlockSpec((tm, 2 * d), lambda i: (i, 0))],
        out_shape=[jax.ShapeDtypeStruct((s, d), F32), jax.ShapeDtypeStruct((s, 1), F32),
                   jax.ShapeDtypeStruct((s, 2 * d), BF16)],
        scratch_shapes=[pltpu.VMEM((tm, d), F32)],
        compiler_params=_cparams(parallel=True), name=name,
    )(xin, gin, bin_, win, lg, lb, wm, bs_exp, wout)


def _sgu_bwd1(dr1, zp, wout, lg, lb, wm, wmt, bs_exp, gwout_buf, name):
    s, d = dr1.shape
    nh = wm.shape[0]
    tm = min(TM_SGU, s)
    n = s // tm

    def body(dr_ref, zp_ref, wout_ref, lg_ref, lb_ref, wm_ref, wmt_ref, bs_ref, buf_ref,
             dzp_ref, gwout_ref, gws_ref, gbs_ref, glg_ref, glb_ref, s_scr, dvn_scr, bs_acc):
        i = pl.program_id(0)

        @pl.when(i == 0)
        def _():
            gwout_ref[...] = jnp.zeros_like(gwout_ref)
            gws_ref[...] = jnp.zeros_like(gws_ref)
            glg_ref[...] = jnp.zeros_like(glg_ref)
            glb_ref[...] = jnp.zeros_like(glb_ref)
            bs_acc[...] = jnp.zeros_like(bs_acc)

        dob = dr_ref[...].astype(BF16)
        zp = zp_ref[...].astype(F32)
        z = _gelu(zp)
        u = z[:, :d]
        lg = lg_ref[...]
        vhat, rstdv = _ln_stats(z[:, d:])
        vn = (vhat * lg + lb_ref[...]).astype(BF16)
        _sgu_gate(vn, wm_ref, bs_ref, s_scr, tm, nh)
        sv = s_scr[...]
        gwout_ref[...] += _dot_tn((u * sv).astype(BF16), dob)
        dq = _dot_nt(dob, wout_ref[...])
        ds = dq * u
        dsb = ds.astype(BF16)
        part = jnp.zeros((CHUNK, d), F32)
        for ch in range(tm // CHUNK):
            r0 = ch * CHUNK
            part = part + ds[r0:r0 + CHUNK, :]
            for h in range(nh):
                c0 = h * CHUNK
                blk = dsb[r0:r0 + CHUNK, c0:c0 + CHUNK]
                gws_ref[h] += _dot_nt(blk, vn[r0:r0 + CHUNK, c0:c0 + CHUNK])
                dvn_scr[r0:r0 + CHUNK, c0:c0 + CHUNK] = _dot(wmt_ref[h], blk)
        bs_acc[...] += part
        dvn = dvn_scr[...]
        glg_ref[...] += _colsum(dvn * vhat)
        glb_ref[...] += _colsum(dvn)
        dv = _ln_bwd(dvn, vhat, rstdv, lg)
        gp = _gelu_grad(zp)
        dzp_ref[:, :d] = (dq * sv * gp[:, :d]).astype(BF16)
        dzp_ref[:, d:] = (dv * gp[:, d:]).astype(BF16)

        @pl.when(i == n - 1)
        def _():
            rows = lax.broadcasted_iota(jnp.int32, (CHUNK, CHUNK), 0)
            cols = lax.broadcasted_iota(jnp.int32, (CHUNK, CHUNK), 1)
            tril = (cols <= rows).astype(F32)
            acc = bs_acc[...]
            for h in range(nh):
                gws_ref[h] = gws_ref[h] * tril
                gbs_ref[:, h:h + 1] = jnp.sum(acc[:, h * CHUNK:(h + 1) * CHUNK], axis=1, keepdims=True)

    tile = pl.BlockSpec((tm, d), lambda i: (i, 0))
    wide = pl.BlockSpec((tm, 2 * d), lambda i: (i, 0))
    hspec = _resident((nh, CHUNK, CHUNK), lambda i: (0, 0, 0))
    return pl.pallas_call(
        body, grid=(n,),
        in_specs=[tile, wide, _wspec(wout, 0), _rowspec(d), _rowspec(d), hspec, hspec,
                  _resident((CHUNK, d), lambda i: (0, 0)), pl.BlockSpec(memory_space=pl.ANY)],
        out_specs=[wide, pl.BlockSpec((None, d, d), lambda i: (0, 0, 0)),
                   pl.BlockSpec((nh, CHUNK, CHUNK), lambda i: (0, 0, 0)),
                   pl.BlockSpec((CHUNK, nh), lambda i: (0, 0)), _rowspec(d), _rowspec(d)],
        out_shape=[jax.ShapeDtypeStruct((s, 2 * d), BF16), jax.ShapeDtypeStruct(gwout_buf.shape, F32),
                   jax.ShapeDtypeStruct((nh, CHUNK, CHUNK), F32), jax.ShapeDtypeStruct((CHUNK, nh), F32),
                   jax.ShapeDtypeStruct((1, d), F32), jax.ShapeDtypeStruct((1, d), F32)],
        scratch_shapes=[pltpu.VMEM((tm, d), F32), pltpu.VMEM((tm, d), F32), pltpu.VMEM((CHUNK, d), F32)],
        input_output_aliases={8: 1},
        compiler_params=_cparams(), name=name,
    )(dr1, zp, wout, lg, lb, wm, wmt, bs_exp, gwout_buf)


def _pool_counts(t0, tm, w):
    pos = t0 + lax.broadcasted_iota(jnp.int32, (tm, 1), 0)
    return jnp.minimum(pos + 1, w).astype(F32)


def _pool_fwd(xin, gin, bin_, win, wg, scale, wout, alpha, name):
    s, d = xin.shape
    ng, dg = wg.shape[0], wg.shape[1]
    hb = POOL_HALO
    tm = min(TM_POOL, s)
    n = s // tm
    assert ng == len(POOL_WINDOWS) and ng * dg == d and max(POOL_WINDOWS) <= hb

    def body(x_ref, g_ref, b_ref, win_ref, wg_ref, sc_ref, wout_ref, xo_ref, rs_ref, ys_ref, y_scr, z_scr):
        i = pl.program_id(0)

        @pl.when(i == 0)
        def _():
            y_scr[0:hb, :] = jnp.zeros((hb, d), F32)

        x = x_ref[...] * g_ref[...] + b_ref[...]
        y = _dot(x.astype(BF16), win_ref[...])
        ys_ref[...] = y.astype(BF16)
        y_scr[hb:hb + tm, :] = y
        for g, w in enumerate(POOL_WINDOWS):
            c0 = g * dg
            acc = y[:, c0:c0 + dg]
            for dd in range(1, w):
                acc = acc + y_scr[pl.ds(hb - dd, tm), c0:c0 + dg]
            pg = acc / _pool_counts(i * tm, tm, w) - y[:, c0:c0 + dg]
            z_scr[:, c0:c0 + dg] = _dot(pg.astype(BF16), wg_ref[g])
        y_scr[0:hb, :] = y_scr[tm:tm + hb, :]
        zz = (z_scr[...] * sc_ref[...]).astype(BF16)
        xhat, rstd = _ln_stats(alpha * x + _dot(zz, wout_ref[...]))
        xo_ref[...] = xhat
        rs_ref[...] = rstd

    tile = pl.BlockSpec((tm, d), lambda i: (i, 0))
    return pl.pallas_call(
        body, grid=(n,),
        in_specs=[tile, _rowspec(d), _rowspec(d), _wspec(win, 0),
                  _resident((ng, dg, dg), lambda i: (0, 0, 0)), _rowspec(d), _wspec(wout, 0)],
        out_specs=[tile, pl.BlockSpec((tm, 1), lambda i: (i, 0)), tile],
        out_shape=[jax.ShapeDtypeStruct((s, d), F32), jax.ShapeDtypeStruct((s, 1), F32),
                   jax.ShapeDtypeStruct((s, d), BF16)],
        scratch_shapes=[pltpu.VMEM((tm + hb, d), F32), pltpu.VMEM((tm, d), F32)],
        compiler_params=_cparams(), name=name,
    )(xin, gin, bin_, win, wg, scale, wout)


def _pool_bwd1(dr1, ys, wout, wg, scale, gwout_buf, name):
    s, d = dr1.shape
    ng, dg = wg.shape[0], wg.shape[1]
    hb = POOL_HALO
    tm = min(TM_POOL, s)
    n = s // tm
    halo_blocks = tm // hb

    def body(dr_ref, ys_ref, halo_ref, wout_ref, wg_ref, sc_ref, buf_ref,
             dy_ref, gwout_ref, gwg_ref, gsc_ref, y_scr, e_scr, z_scr, dp_scr):
        i = pl.program_id(0)
        t = n - 1 - i

        @pl.when(i == 0)
        def _():
            e_scr[tm:tm + hb, :] = jnp.zeros((hb, d), F32)
            gwout_ref[...] = jnp.zeros_like(gwout_ref)
            gwg_ref[...] = jnp.zeros_like(gwg_ref)
            gsc_ref[...] = jnp.zeros_like(gsc_ref)

        dob = dr_ref[...].astype(BF16)
        y = ys_ref[...].astype(F32)
        y_scr[0:hb, :] = jnp.where(t > 0, halo_ref[...].astype(F32), 0.0)
        y_scr[hb:hb + tm, :] = y
        pgs = []
        for g, w in enumerate(POOL_WINDOWS):
            c0 = g * dg
            acc = y[:, c0:c0 + dg]
            for dd in range(1, w):
                acc = acc + y_scr[pl.ds(hb - dd, tm), c0:c0 + dg]
            pg = (acc / _pool_counts(t * tm, tm, w) - y[:, c0:c0 + dg]).astype(BF16)
            pgs.append(pg)
            z_scr[:, c0:c0 + dg] = _dot(pg, wg_ref[g])
        zpre = z_scr[...]
        sc = sc_ref[...]
        gwout_ref[...] += _dot_tn((zpre * sc).astype(BF16), dob)
        dz = _dot_nt(dob, wout_ref[...])
        gsc_ref[...] += _colsum(dz * zpre)
        dzpre = (dz * sc).astype(BF16)
        for g, w in enumerate(POOL_WINDOWS):
            c0 = g * dg
            dzg = dzpre[:, c0:c0 + dg]
            gwg_ref[g] += _dot_tn(pgs[g], dzg)
            dp = _dot_nt(dzg, wg_ref[g])
            dp_scr[:, c0:c0 + dg] = dp
            e_scr[0:tm, c0:c0 + dg] = dp / _pool_counts(t * tm, tm, w)
        for g, w in enumerate(POOL_WINDOWS):
            c0 = g * dg
            acc = e_scr[0:tm, c0:c0 + dg]
            for dd in range(1, w):
                acc = acc + e_scr[pl.ds(dd, tm), c0:c0 + dg]
            dy_ref[:, c0:c0 + dg] = (acc - dp_scr[:, c0:c0 + dg]).astype(BF16)
        e_scr[tm:tm + hb, :] = e_scr[0:hb, :]

    tile = pl.BlockSpec((tm, d), lambda i: (n - 1 - i, 0))
    return pl.pallas_call(
        body, grid=(n,),
        in_specs=[tile, tile,
                  pl.BlockSpec((hb, d), lambda i: (jnp.maximum((n - 1 - i) * halo_blocks - 1, 0), 0)),
                  _wspec(wout, 0), _resident((ng, dg, dg), lambda i: (0, 0, 0)), _rowspec(d),
                  pl.BlockSpec(memory_space=pl.ANY)],
        out_specs=[tile, pl.BlockSpec((None, d, d), lambda i: (0, 0, 0)),
                   pl.BlockSpec((ng, dg, dg), lambda i: (0, 0, 0)), _rowspec(d)],
        out_shape=[jax.ShapeDtypeStruct((s, d), BF16), jax.ShapeDtypeStruct(gwout_buf.shape, F32),
                   jax.ShapeDtypeStruct((ng, dg, dg), F32), jax.ShapeDtypeStruct((1, d), F32)],
        scratch_shapes=[pltpu.VMEM((tm + hb, d), F32), pltpu.VMEM((tm + hb, d), F32),
                        pltpu.VMEM((tm, d), F32), pltpu.VMEM((tm, d), F32)],
        input_output_aliases={6: 1},
        compiler_params=_cparams(), name=name,
    )(dr1, ys, ys, wout, wg, scale, gwout_buf)


def _elementwise(fn, ins, out_dtypes, name):
    shape = ins[0].shape
    c = shape[-1]
    r = math.prod(shape[:-1])
    tr = _pick_rows(r, c, 4, 1 << 20)

    def body(*refs):
        vals = fn(*[ref[...] for ref in refs[:len(ins)]])
        for ref, v in zip(refs[len(ins):], vals):
            ref[...] = v.astype(ref.dtype)

    spec = pl.BlockSpec((tr, c), lambda i: (i, 0))
    outs = pl.pallas_call(
        body, grid=(r // tr,),
        in_specs=[spec] * len(ins), out_specs=[spec] * len(out_dtypes),
        out_shape=[jax.ShapeDtypeStruct((r, c), dt) for dt in out_dtypes],
        compiler_params=_cparams(parallel=True), name=name,
    )(*[a.reshape(r, c) for a in ins])
    return [o.reshape(shape) for o in outs]


def _prefetch_call(body, grid, in_specs, out_specs, out_shape, name, aliases=None):
    return pl.pallas_call(
        body,
        grid_spec=pltpu.PrefetchScalarGridSpec(num_scalar_prefetch=1, grid=grid, in_specs=in_specs, out_specs=out_specs),
        out_shape=out_shape, input_output_aliases=aliases or {},
        compiler_params=_cparams(len(grid), parallel=True), name=name)


def _cast_into_full(w3, l0, l, kind, chip1, name):
    _, r, c = w3.shape
    tr = _pick_rows(r, c, 4, 1 << 20)

    def body(k_ref, w_ref, o_ref):
        o_ref[...] = w_ref[...].astype(BF16)

    if kind == "row":
        out_spec = pl.BlockSpec((None, None, tr, c), lambda a, j, k: (a, k[0], j, 0))
    else:
        out_spec = pl.BlockSpec((None, tr, c), lambda a, j, k: (a, j, k[0]))
    return _prefetch_call(
        body, (l, r // tr), [pl.BlockSpec((None, tr, c), lambda a, j, k: (a + l0, j, 0))], out_spec,
        jax.ShapeDtypeStruct(_full_shape(kind, (l, r, c)), BF16), name)(chip1, w3)


def _pair_sum(g, got, kind, core1, name):
    if kind == "row":
        l, nc, sr, c = g.shape
        g5, got3 = g.reshape(l * nc, 2, sr // 2, c), got.reshape(l * nc, sr // 2, c)
    else:
        l, r, c = g.shape
        g5, got3 = g.reshape(l, 2, r // 2, c), got
    a, _, hr, c = g5.shape
    tr = _pick_rows(hr, c, 4, 1 << 20)

    def body(c_ref, g_ref, t_ref, o_ref):
        o_ref[...] = (g_ref[...] + t_ref[...]).astype(BF16)

    half = pl.BlockSpec((None, tr, c), lambda i, j, cc: (i, j, 0))
    out = _prefetch_call(
        body, (a, hr // tr), [pl.BlockSpec((None, None, tr, c), lambda i, j, cc: (i, cc[0], j, 0)), half], half,
        jax.ShapeDtypeStruct(got3.shape, BF16), name)(core1, g5, got3)
    return out.reshape(got.shape)


def _chip_sum(t, rb, kind, chip1, name):
    _, l, hr, sc = rb.shape
    tr = _pick_rows(hr, sc, 4, 1 << 19)

    def body(k_ref, t_ref, rb_ref, o_ref):
        acc = t_ref[...].astype(F32)
        for r in range(N_CHIPS - 1):
            acc = acc + rb_ref[r].astype(F32)
        o_ref[...] = acc

    if kind == "row":
        t_spec = pl.BlockSpec((None, None, tr, sc), lambda a, j, k: (a, k[0], j, 0))
    else:
        t_spec = pl.BlockSpec((None, tr, sc), lambda a, j, k: (a, j, k[0]))
    return _prefetch_call(
        body, (l, hr // tr),
        [t_spec, pl.BlockSpec((N_CHIPS - 1, None, tr, sc), lambda a, j, k: (0, a, j, 0))],
        pl.BlockSpec((None, tr, sc), lambda a, j, k: (a, j, 0)),
        jax.ShapeDtypeStruct((l, hr, sc), F32), name)(chip1, t, rb)


def _adamw_math(w_, g_, m_, v_):
    m2 = ADAM_B1 * m_ + (1.0 - ADAM_B1) * g_
    v2 = ADAM_B2 * v_ + (1.0 - ADAM_B2) * (g_ * g_)
    m_hat = m2 / (1.0 - ADAM_B1 ** ADAM_STEP)
    v_hat = v2 / (1.0 - ADAM_B2 ** ADAM_STEP)
    delta = -ADAM_LR * (m_hat / (jnp.sqrt(v_hat) + ADAM_EPS) + ADAM_WD * w_)
    return delta, m2, v2


def _adamw_big(w, m, v, own, other, core1, name, l0=0, into=None):
    lg, hr, c = own.shape
    lw = w.shape[0]
    view = lambda a: a.reshape(lw, 2, hr, c)
    tr = _pick_rows(hr, c, 4, 1 << 20)
    n_keep = 0 if into is None else len(into)

    def body(c_ref, w_ref, m_ref, v_ref, own_ref, oth_ref, *rest):
        g_ref, d_ref, m2_ref, v2_ref = rest[n_keep:]
        g = jnp.where(pl.program_id(1) == c_ref[0], own_ref[...], oth_ref[...])
        g_ref[...] = g
        d_ref[...], m2_ref[...], v2_ref[...] = _adamw_math(w_ref[...], g, m_ref[...], v_ref[...])

    s4 = pl.BlockSpec((None, None, tr, c), lambda a, h, j, cc: (a + l0, h, j, 0))
    s3 = pl.BlockSpec((None, tr, c), lambda a, h, j, cc: (a, j, 0))
    aliases = {6 + k: k for k in range(n_keep)}
    return _prefetch_call(
        body, (lg, 2, hr // tr), [s4, s4, s4, s3, s3] + [ANY] * n_keep, [s4] * 4,
        [jax.ShapeDtypeStruct((lw, 2, hr, c), F32)] * 4, name, aliases,
    )(core1, view(w), view(m), view(v), own, other, *(into or []))


def _sum_devices(own, gathered, me1, name):
    r, c = own.shape
    tr = _pick_rows(r, c, 4, 1 << 17)

    def body(me_ref, own_ref, g_ref, o_ref):
        acc = None
        for k in range(N_DEV):
            v = jnp.where(me_ref[0] == k, own_ref[...], g_ref[k])
            acc = v if acc is None else acc + v
        o_ref[...] = acc

    return _prefetch_call(
        body, (r // tr,),
        [pl.BlockSpec((tr, c), lambda i, m: (i, 0)), pl.BlockSpec((N_DEV, tr, c), lambda i, m: (0, i, 0))],
        pl.BlockSpec((tr, c), lambda i, m: (i, 0)), jax.ShapeDtypeStruct((r, c), F32), name)(me1, own, gathered)


def _adamw(w, g, m, v, name):
    return _elementwise(_adamw_math, [w, g, m, v], [F32, F32, F32], name)


ANY = pl.BlockSpec(memory_space=pl.ANY)


def _mesh_pos():
    return lax.axis_index("x"), lax.axis_index("y"), lax.axis_index("c")


def _chip_peers(x, y, c):
    out = []
    for r in (1, 2, 3):
        px = 1 - x if r & 2 else x
        py = 1 - y if r & 1 else y
        out.append((2 * px + py, (px, py, c)))
    return out


def _full_shape(kind, shard_shape):
    l, r, c = shard_shape
    return (l, N_CHIPS, r, c) if kind == "row" else (l, r, N_CHIPS * c)


def _full_piece(ref, kind, k, h, hr, sc):
    rows = pl.ds(pl.multiple_of(h * hr, SUBLANES_BF16), hr)
    if kind == "row":
        return ref.at[:, k, rows, :]
    return ref.at[:, rows, pl.ds(pl.multiple_of(k * sc, LANES), sc)]


def _remote(src, dst, ssem, rsem, dev):
    return pltpu.make_async_remote_copy(src_ref=src, dst_ref=dst, send_sem=ssem, recv_sem=rsem,
                                        device_id=dev, device_id_type=MESH)


DMA_CHUNK_BYTES = 1 << 20
DMA_MAX_CHUNKS = 32


def _chunk_views(src, dst):
    axis = len(src.shape) - 2
    rows = src.shape[axis]
    nbytes = math.prod(src.shape) * jnp.dtype(src.dtype).itemsize
    n = max(1, min(DMA_MAX_CHUNKS, nbytes // DMA_CHUNK_BYTES))
    while n > 1 and (rows % n or (rows // n) % SUBLANES_BF16):
        n -= 1
    cr = rows // n
    out = []
    for i in range(n):
        idx = (slice(None),) * axis + (pl.ds(i * cr, cr), slice(None))
        out.append((src.at[idx], dst.at[idx]))
    return out


def _start_remote(src, dst, ssem, rsem, dev):
    for s, t in _chunk_views(src, dst):
        _remote(s, t, ssem, rsem, dev).start()
    return _remote(src, dst, ssem, rsem, dev)


def _allgather_steps(fulls, kinds):
    nw = len(fulls)

    def dims(a, kind):
        return (a.shape[2] // 2, a.shape[3]) if kind == "row" else (a.shape[1] // 2, a.shape[2] // N_CHIPS)

    hrs = [dims(a, k)[0] for a, k in zip(fulls, kinds)]
    scs = [dims(a, k)[1] for a, k in zip(fulls, kinds)]

    def piece(ref, w, k, h):
        return _full_piece(ref, kinds[w], k, h, hrs[w], scs[w])

    def copies1(src, dst, sems, start):
        x, y, c = _mesh_pos()
        k_me = 2 * x + y
        out = []
        for w in range(nw):
            for r, (kj, dev) in enumerate(_chip_peers(x, y, c)):
                args = (sems[0].at[3 * w + r], sems[1].at[3 * w + r], dev)
                if start:
                    out.append(_start_remote(piece(src[w], w, k_me, c), piece(dst[w], w, k_me, c), *args))
                else:
                    out.append(_remote(piece(src[w], w, k_me, c), piece(dst[w], w, kj, c), *args))
        return out

    def copies2(src, dst, sems, start):
        x, y, c = _mesh_pos()
        out = []
        for w in range(nw):
            for r, (kj, _) in enumerate(_chip_peers(x, y, c)):
                args = (sems[0].at[3 * w + r], sems[1].at[3 * w + r], (x, y, 1 - c))
                if start:
                    out.append(_start_remote(piece(src[w], w, kj, c), piece(dst[w], w, kj, c), *args))
                else:
                    out.append(_remote(piece(src[w], w, kj, 1 - c), piece(dst[w], w, kj, 1 - c), *args))
        return out

    def finish(copies):
        def fn(src, dst, sems):
            for cp in copies(src, dst, sems, False):
                cp.wait_recv()
            for cp in copies(src, dst, sems, False):
                cp.wait_send()
        return fn

    step1 = (lambda s, d, m: copies1(s, d, m, True), finish(copies1))
    step2 = (lambda s, d, m: copies2(s, d, m, True), finish(copies2))
    return step1, step2


def _exchange_hook(arrays, step, sem_len, out_shapes=None):
    shapes = out_shapes or [jax.ShapeDtypeStruct(a.shape, a.dtype) for a in arrays]
    return dict(arrays=list(arrays), out_shapes=shapes, in_place=out_shapes is None, sem_len=sem_len,
                first=step[0], last=step[1])


def _exchange_call(hook, name):
    nh, nho = len(hook["arrays"]), len(hook["out_shapes"])

    def body(*refs):
        h_in, h_out, sems = refs[:nh], refs[nh:nh + nho], refs[nh + nho:]
        hook["first"](h_in, h_out, sems)
        hook["last"](h_in, h_out, sems)

    return pl.pallas_call(
        body, in_specs=[ANY] * nh, out_specs=[ANY] * nho, out_shape=hook["out_shapes"],
        scratch_shapes=[pltpu.SemaphoreType.DMA((hook["sem_len"],))] * 2,
        input_output_aliases={k: k for k in range(nh)} if hook["in_place"] else {}, name=name,
    )(*hook["arrays"])


def _allgather_weights(fulls, kinds):
    nw = len(fulls)
    step1, step2 = _allgather_steps(fulls, kinds)

    def body(*refs):
        mine, fu = refs[:nw], refs[nw:2 * nw]
        sems1, sems2 = refs[2 * nw:2 * nw + 2], refs[2 * nw + 2:]
        step1[0](mine, fu, sems1)
        step1[1](mine, fu, sems1)
        step2[0](fu, fu, sems2)
        step2[1](fu, fu, sems2)

    return pl.pallas_call(
        body,
        in_specs=[ANY] * nw, out_specs=[ANY] * nw,
        out_shape=[jax.ShapeDtypeStruct(a.shape, a.dtype) for a in fulls],
        scratch_shapes=[pltpu.SemaphoreType.DMA((3 * nw,))] * 4,
        input_output_aliases={w: w for w in range(nw)},
        name="allgather_weights",
    )(*fulls)


def _hosted(body, n_in, n_out, hook, n_steps):
    if hook is None:
        return body
    nh, nho = len(hook["arrays"]), len(hook["out_shapes"])

    def wrapped(*refs):
        ins, h_in = refs[:n_in], refs[n_in:n_in + nh]
        outs = refs[n_in + nh:n_in + nh + n_out]
        h_out = refs[n_in + nh + n_out:n_in + nh + n_out + nho]
        rest = refs[n_in + nh + n_out + nho:]
        scr, sems = rest[:-2], rest[-2:]
        i = pl.program_id(0)

        @pl.when(i == 0)
        def _():
            hook["first"](h_in, h_out, sems)

        body(*ins, *outs, *scr)

        @pl.when(i == n_steps - 1)
        def _():
            hook["last"](h_in, h_out, sems)

    return wrapped


def _hosted_call(body, hook, n_steps, in_specs, out_specs, out_shape, scratch_shapes, args, name, aliases=None):
    n_in, n_out = len(in_specs), len(out_specs)
    aliases = dict(aliases or {})
    if hook is not None:
        nh = len(hook["arrays"])
        in_specs = list(in_specs) + [ANY] * nh
        out_specs = list(out_specs) + [ANY] * len(hook["out_shapes"])
        out_shape = list(out_shape) + list(hook["out_shapes"])
        scratch_shapes = list(scratch_shapes) + [pltpu.SemaphoreType.DMA((hook["sem_len"],))] * 2
        if hook["in_place"]:
            aliases.update({n_in + k: n_out + k for k in range(nh)})
        args = list(args) + hook["arrays"]
    outs = pl.pallas_call(
        _hosted(body, n_in, n_out, hook, n_steps), grid=(n_steps,),
        in_specs=in_specs, out_specs=out_specs, out_shape=out_shape, scratch_shapes=scratch_shapes,
        input_output_aliases=aliases, compiler_params=_cparams(), name=name,
    )(*args)
    return outs[:n_out], outs[n_out:]


def _pair_exchange(copies):
    def finish(src, dst, sems):
        for cp in copies(src, dst, sems, False):
            cp.wait_recv()
        for cp in copies(src, dst, sems, False):
            cp.wait_send()
    return (lambda s, d, m: copies(s, d, m, True), finish)


def _rs_pair(fulls, kinds):
    nw = len(fulls)

    def half_all(ref, kind, h):
        if kind == "row":
            hr = ref.shape[2] // 2
            return ref.at[:, :, pl.ds(pl.multiple_of(h * hr, SUBLANES_BF16), hr), :]
        hr = ref.shape[1] // 2
        return ref.at[:, pl.ds(pl.multiple_of(h * hr, SUBLANES_BF16), hr), :]

    def half_shape(kind, shape):
        if kind == "row":
            return (shape[0], shape[1], shape[2] // 2, shape[3])
        return (shape[0], shape[1] // 2, shape[2])

    def copies(g, got, sems, start):
        x, y, c = _mesh_pos()
        make = _start_remote if start else _remote
        return [make(half_all(g[w], kinds[w], 1 - c), got[w], sems[0].at[w], sems[1].at[w], (x, y, 1 - c))
                for w in range(nw)]

    shapes = [jax.ShapeDtypeStruct(half_shape(k, a.shape), a.dtype) for k, a in zip(kinds, fulls)]
    return _exchange_hook(fulls, _pair_exchange(copies), nw, shapes)


def _rs_chips(parts, kinds):
    nw = len(parts)

    def slot(ref, kind, k):
        if kind == "row":
            return ref.at[:, k]
        sc = ref.shape[2] // N_CHIPS
        return ref.at[:, :, pl.ds(pl.multiple_of(k * sc, LANES), sc)]

    def slot_shape(kind, shape):
        if kind == "row":
            return (shape[0], shape[2], shape[3])
        return (shape[0], shape[1], shape[2] // N_CHIPS)

    def copies(t, rb, sems, start):
        x, y, c = _mesh_pos()
        make = _start_remote if start else _remote
        return [make(slot(t[w], kinds[w], kj), rb[w].at[r], sems[0].at[3 * w + r], sems[1].at[3 * w + r], dev)
                for w in range(nw) for r, (kj, dev) in enumerate(_chip_peers(x, y, c))]

    shapes = [jax.ShapeDtypeStruct((N_CHIPS - 1,) + slot_shape(k, a.shape), a.dtype) for k, a in zip(kinds, parts)]
    return _exchange_hook(parts, _pair_exchange(copies), 3 * nw, shapes)


def _rs_join(halves):
    nw = len(halves)

    def copies(src, dst, sems, start):
        x, y, c = _mesh_pos()
        make = _start_remote if start else _remote
        return [make(src[w], dst[w], sems[0].at[w], sems[1].at[w], (x, y, 1 - c)) for w in range(nw)]

    return _exchange_hook(halves, _pair_exchange(copies), nw,
                          [jax.ShapeDtypeStruct(a.shape, a.dtype) for a in halves])


def _allgather_small(buf, name):
    def body(in_ref, out_ref, ssem, rsem):
        x, y, c = _mesh_pos()
        me = 4 * x + 2 * y + c
        cps, waits = [], []
        for r in range(1, N_DEV):
            px = 1 - x if r & 4 else x
            py = 1 - y if r & 2 else y
            pc = 1 - c if r & 1 else c
            cp = _remote(in_ref, out_ref.at[me], ssem.at[r - 1], rsem.at[r - 1], (px, py, pc))
            cp.start()
            cps.append(cp)
            waits.append(_remote(in_ref, out_ref.at[4 * px + 2 * py + pc], ssem.at[r - 1], rsem.at[r - 1], (px, py, pc)))
        for wt in waits:
            wt.wait_recv()
        for cp in cps:
            cp.wait_send()

    return pl.pallas_call(
        body, in_specs=[ANY], out_specs=ANY,
        out_shape=jax.ShapeDtypeStruct((N_DEV,) + buf.shape, buf.dtype),
        scratch_shapes=[pltpu.SemaphoreType.DMA((N_DEV - 1,))] * 2,
        name=name,
    )(buf)


def _pack(arrs):
    flat = jnp.concatenate([a.reshape(-1).astype(F32) for a in arrs])
    rows = -(-flat.shape[0] // (LANES * 16)) * 16
    return jnp.pad(flat, (0, rows * LANES - flat.shape[0])).reshape(rows, LANES)


def _unpack(buf, shapes):
    flat = buf.reshape(-1)
    out, off = [], 0
    for shp in shapes:
        nel = math.prod(shp)
        out.append(flat[off:off + nel].reshape(shp))
        off += nel
    return out


BIG = ("a_w_in", "a_w_out", "b_w_in", "b_w_out", "c_w_in", "c_w_grp", "c_w_out", "f_w_up", "f_w_down")
BIG_KIND = {"a_w_in": "col", "a_w_out": "row", "b_w_in": "col", "b_w_out": "row", "c_w_in": "row",
            "c_w_grp": "row", "c_w_out": "row", "f_w_up": "col", "f_w_down": "row"}
FIRST_LAYER = ("a_w_in", "a_w_out", "f_w_up", "f_w_down")
MIXER_FIRST_LAYER = ("a_w_in", "a_w_out")
SHARDED_SMALL = ("a_dw", "a_dw_b", "a_ln_g", "a_ln_b", "c_scale", "f_dw")
REPLICATED = ("b_ln_g", "b_ln_b", "b_ws", "b_bs", "ln1_g", "ln1_b", "ln2_g", "ln2_b")
WEIGHTS = ("a_w_in", "a_dw", "a_dw_b", "a_ln_g", "a_ln_b", "a_w_out", "b_w_in", "b_ln_g", "b_ln_b", "b_ws", "b_bs",
           "b_w_out", "c_w_in", "c_w_grp", "c_scale", "c_w_out", "f_w_up", "f_dw", "f_w_down",
           "ln1_g", "ln1_b", "ln2_g", "ln2_b")


def _as3d(a):
    return a.reshape((-1,) + a.shape[-2:])


def kernel(x, a_w_in, a_dw, a_dw_b, a_ln_g, a_ln_b, a_w_out, b_w_in, b_ln_g, b_ln_b, b_ws, b_bs, b_w_out, c_w_in, c_w_grp, c_scale, c_w_out, f_w_up, f_dw, f_w_down, ln1_g, ln1_b, ln2_g, ln2_b, loss_target, m_a_w_in, m_a_dw, m_a_dw_b, m_a_ln_g, m_a_ln_b, m_a_w_out, m_b_w_in, m_b_ln_g, m_b_ln_b, m_b_ws, m_b_bs, m_b_w_out, m_c_w_in, m_c_w_grp, m_c_scale, m_c_w_out, m_f_w_up, m_f_dw, m_f_w_down, m_ln1_g, m_ln1_b, m_ln2_g, m_ln2_b, v_a_w_in, v_a_dw, v_a_dw_b, v_a_ln_g, v_a_ln_b, v_a_w_out, v_b_w_in, v_b_ln_g, v_b_ln_b, v_b_ws, v_b_bs, v_b_w_out, v_c_w_in, v_c_w_grp, v_c_scale, v_c_w_out, v_f_w_up, v_f_dw, v_f_w_down, v_ln1_g, v_ln1_b, v_ln2_g, v_ln2_b):
    w = dict(a_w_in=a_w_in, a_dw=a_dw, a_dw_b=a_dw_b, a_ln_g=a_ln_g, a_ln_b=a_ln_b, a_w_out=a_w_out, b_w_in=b_w_in, b_ln_g=b_ln_g, b_ln_b=b_ln_b, b_ws=b_ws, b_bs=b_bs, b_w_out=b_w_out, c_w_in=c_w_in, c_w_grp=c_w_grp, c_scale=c_scale, c_w_out=c_w_out, f_w_up=f_w_up, f_dw=f_dw, f_w_down=f_w_down, ln1_g=ln1_g, ln1_b=ln1_b, ln2_g=ln2_g, ln2_b=ln2_b)
    mom = dict(a_w_in=m_a_w_in, a_dw=m_a_dw, a_dw_b=m_a_dw_b, a_ln_g=m_a_ln_g, a_ln_b=m_a_ln_b, a_w_out=m_a_w_out, b_w_in=m_b_w_in, b_ln_g=m_b_ln_g, b_ln_b=m_b_ln_b, b_ws=m_b_ws, b_bs=m_b_bs, b_w_out=m_b_w_out, c_w_in=m_c_w_in, c_w_grp=m_c_w_grp, c_scale=m_c_scale, c_w_out=m_c_w_out, f_w_up=m_f_w_up, f_dw=m_f_dw, f_w_down=m_f_w_down, ln1_g=m_ln1_g, ln1_b=m_ln1_b, ln2_g=m_ln2_g, ln2_b=m_ln2_b)
    var = dict(a_w_in=v_a_w_in, a_dw=v_a_dw, a_dw_b=v_a_dw_b, a_ln_g=v_a_ln_g, a_ln_b=v_a_ln_b, a_w_out=v_a_w_out, b_w_in=v_b_w_in, b_ln_g=v_b_ln_g, b_ln_b=v_b_ln_b, b_ws=v_b_ws, b_bs=v_b_bs, b_w_out=v_b_w_out, c_w_in=v_c_w_in, c_w_grp=v_c_w_grp, c_scale=v_c_scale, c_w_out=v_c_w_out, f_w_up=v_f_w_up, f_dw=v_f_dw, f_w_down=v_f_w_down, ln1_g=v_ln1_g, ln1_b=v_ln1_b, ln2_g=v_ln2_g, ln2_b=v_ln2_b)

    depth = ln1_g.shape[0]
    d = x.shape[-1]
    alpha = float((2 * depth) ** 0.25)
    chip = 2 * lax.axis_index("x") + lax.axis_index("y")
    chip1 = chip.astype(jnp.int32).reshape(1)
    core1 = lax.axis_index("c").astype(jnp.int32).reshape(1)

    early = [(k, 0, 1) for k in FIRST_LAYER]
    up_front = [g for g in early if g[0] in MIXER_FIRST_LAYER]
    soon = [g for g in early if g[0] not in MIXER_FIRST_LAYER]
    late = [(k, 1, w[k].shape[0] - 1) for k in FIRST_LAYER] + [(k, 0, _as3d(w[k]).shape[0]) for k in BIG if k not in FIRST_LAYER]
    riding = soon + late

    def group_kinds(group):
        return [BIG_KIND[k] for k, _, _ in group]

    def cast_group(group, tag):
        return [_cast_into_full(_as3d(w[k]), l0, nl, BIG_KIND[k], chip1, f"cast_{tag}_{k}") for k, l0, nl in group]

    def as_stacks(group, arrays):
        return {k: (a.reshape(a.shape[0], -1, a.shape[-1]) if BIG_KIND[k] == "row" else a)
                for (k, _, _), a in zip(group, arrays)}

    early_full = as_stacks(up_front, _allgather_weights(cast_group(up_front, "first"), group_kinds(up_front)))
    riding_mine = cast_group(soon, "soon") + cast_group(late, "rest")
    riding_step1, _ = _allgather_steps(riding_mine, group_kinds(riding))
    _, soon_step2 = _allgather_steps(riding_mine[:len(soon)], group_kinds(soon))
    _, late_step2 = _allgather_steps(riding_mine[len(soon):], group_kinds(late))
    late_full = {}

    def weight(k, l):
        if k in FIRST_LAYER:
            return (early_full[k], 0) if l == 0 else (late_full[k], l - 1)
        return late_full[k], l

    small_all = _allgather_small(_pack([w[k] for k in SHARDED_SMALL]), "allgather_small_params")
    other_core = 1 - lax.axis_index("c")
    per_chip = [_unpack(lax.dynamic_index_in_dim(small_all, 2 * k + other_core, keepdims=False),
                        [w[n].shape for n in SHARDED_SMALL]) for k in range(N_CHIPS)]
    fs = {n: jnp.concatenate([per_chip[k][i] for k in range(N_CHIPS)], axis=-1) for i, n in enumerate(SHARDED_SMALL)}

    nh = b_ws.shape[1]
    tril = jnp.tril(jnp.ones((CHUNK, CHUNK), F32))
    wm = (b_ws[0] * tril).astype(BF16)
    wmt = jnp.swapaxes(wm, 1, 2)
    bs_exp = jnp.repeat(jnp.transpose(b_bs[0]), CHUNK, axis=1)

    xh, g, b = x[0], jnp.ones((1, d), F32), jnp.zeros((1, d), F32)
    saved = []
    for i in range(depth):
        kind, j = i % 3, i // 3
        rec = dict(xin=xh, gin=g, bin=b)
        if kind == 0:
            hook = _exchange_hook(riding_mine, riding_step1, 3 * len(riding)) if i == 0 else None
            (xh1, rstd1, p, chat, rstdc), landed = _conv_fwd(
                xh, g, b, weight("a_w_in", j)[0], weight("a_w_out", j)[0], weight("a_w_in", j)[1], fs["a_dw"], j,
                fs["a_dw_b"][j:j + 1], fs["a_ln_g"][j:j + 1], fs["a_ln_b"][j:j + 1], alpha, f"conv_fwd_{i}", hook)
            if i == 0:
                soon_hook = _exchange_hook(landed[:len(soon)], soon_step2, 3 * len(soon))
                early_full.update(as_stacks(soon, _exchange_call(soon_hook, "allgather_soon_d2d")))
                landed = landed[len(soon):]
            rec.update(p=p, chat=chat, rstdc=rstdc)
        elif kind == 1:
            xh1, rstd1, zp = _sgu_fwd(xh, g, b, late_full["b_w_in"], b_ln_g, b_ln_b, wm, bs_exp, late_full["b_w_out"],
                                      alpha, f"sgu_fwd_{i}")
            rec.update(zp=zp)
        else:
            xh1, rstd1, ys = _pool_fwd(xh, g, b, late_full["c_w_in"], late_full["c_w_grp"], fs["c_scale"],
                                       late_full["c_w_out"], alpha, f"pool_fwd_{i}")
            rec.update(ys=ys)
        hook = _exchange_hook(landed, late_step2, 3 * len(late)) if i == 0 else None
        (xh2, rstd2, hs, hcs), passed_on = _ffn_fwd(
            xh1, ln1_g[i:i + 1], ln1_b[i:i + 1], weight("f_w_up", i)[0], weight("f_w_down", i)[0],
            weight("f_w_up", i)[1], fs["f_dw"], i, alpha, f"ffn_fwd_{i}", hook)
        if i == 0:
            late_full = as_stacks(late, passed_on)
        rec.update(xh1=xh1, rstd1=rstd1, xh2=xh2, rstd2=rstd2, hs=hs, hcs=hcs)
        saved.append(rec)
        xh, g, b = xh2, ln2_g[i:i + 1], ln2_b[i:i + 1]

    dxo = loss_target[0]
    assert depth >= 3

    def stack_shape(k, nl):
        _, r, c = _as3d(w[k]).shape
        return (nl, N_CHIPS * r, c) if BIG_KIND[k] == "row" else (nl, r, N_CHIPS * c)

    g_first = {k: lax.empty(stack_shape(k, nl), F32) for k, _, nl in early}
    g_rest = {k: lax.empty(stack_shape(k, nl), F32) for k, _, nl in late if k != "c_w_grp"}

    def gslot(k, l):
        if k in FIRST_LAYER:
            return (g_first, 0) if l == 0 else (g_rest, l - 1)
        return g_rest, l

    def rs_views(group, store):
        out = []
        for k, _, _ in group:
            a = store[k]
            out.append(a.reshape(a.shape[0], N_CHIPS, -1, a.shape[-1]) if BIG_KIND[k] == "row" else a)
        return out

    def pair_sums(group, views, got, tag):
        return [_pair_sum(a, t, BIG_KIND[k], core1, f"rs_pair_sum_{tag}_{k}") for (k, _, _), a, t in zip(group, views, got)]

    def reduce_and_join(group, pair, from_chips, tag):
        half = [_chip_sum(t, rb, BIG_KIND[k], chip1, f"rs_chip_sum_{tag}_{k}")
                for (k, _, _), t, rb in zip(group, pair, from_chips)]
        return half, _exchange_call(_rs_join(half), f"rs_join_{tag}")

    early_kinds = [BIG_KIND[k] for k, _, _ in early]
    late_kinds = [BIG_KIND[k] for k, _, _ in late]
    gs = {k: [None] * w[k].shape[0] for k in ("a_dw", "a_dw_b", "a_ln_g", "a_ln_b", "f_dw", "ln1_g", "ln1_b", "ln2_g", "ln2_b")}
    for i in reversed(range(depth)):
        kind, j = i % 3, i // 3
        rec = saved[i]
        hook = None
        if i == 0:
            late_views = rs_views(late, g_rest)
            hook = _rs_pair(late_views, late_kinds)
        st, idx = gslot("f_w_down", i)
        (dr2, dhc, st["f_w_down"], gs["ln2_g"][i], gs["ln2_b"][i], loss_term), got = _ffn_bwd1(
            dxo, rec["xh2"], rec["rstd2"], ln2_g[i:i + 1], ln2_b[i:i + 1], rec["hcs"], *weight("f_w_down", i),
            st["f_w_down"], idx, f"ffn_bwd1_{i}", hook, loss_head=(i == depth - 1))
        if i == depth - 1:
            loss = lax.psum(loss_term[0, 0], ("x", "y", "c"))
        if i == 0:
            late_pair = pair_sums(late, late_views, got, "rest")
        dr1, dh, gs["f_dw"][i], gs["ln1_g"][i], gs["ln1_b"][i] = _ffn_bwd2(
            dhc, rec["hs"], dr2, *weight("f_w_up", i), fs["f_dw"], i, rec["xh1"], rec["rstd1"], ln1_g[i:i + 1],
            alpha, f"ffn_bwd2_{i}")
        st, idx = gslot("f_w_up", i)
        st["f_w_up"] = _mm_tn(rec["xh1"], ln1_g[i:i + 1], ln1_b[i:i + 1], dh, st["f_w_up"], idx, f"grad_w_up_{i}")
        if kind == 0:
            hook = _rs_chips(late_pair, late_kinds) if i == 0 else None
            st, idx = gslot("a_w_out", j)
            (dp, st["a_w_out"], gs["a_dw"][j], gs["a_dw_b"][j], gs["a_ln_g"][j], gs["a_ln_b"][j]), landed = _conv_bwd1(
                dr1, rec["chat"], rec["rstdc"], rec["p"], *weight("a_w_out", j), fs["a_dw"], j,
                fs["a_ln_g"][j:j + 1], fs["a_ln_b"][j:j + 1], st["a_w_out"], idx, f"conv_bwd1_{i}", hook)
            if i == 0:
                late_from_chips = landed
            win_name, lidx = "a_w_in", j
        elif kind == 1:
            dp, g_rest["b_w_out"], g_ws, g_bs_t, g_blg, g_blb = _sgu_bwd1(
                dr1, rec["zp"], late_full["b_w_out"], b_ln_g, b_ln_b, wm, wmt, bs_exp, g_rest["b_w_out"],
                f"sgu_bwd1_{i}")
            win_name, lidx = "b_w_in", 0
        else:
            dp, g_rest["c_w_out"], g_rest["c_w_grp"], g_cscale = _pool_bwd1(
                dr1, rec["ys"], late_full["c_w_out"], late_full["c_w_grp"], fs["c_scale"], g_rest["c_w_out"],
                f"pool_bwd1_{i}")
            win_name, lidx = "c_w_in", 0
        dxo = _bwd_in(dp, dr1, *weight(win_name, lidx), alpha, f"mixer_bwd2_{i}")
        st, idx = gslot(win_name, lidx)
        st[win_name] = _mm_tn(rec["xin"], rec["gin"], rec["bin"], dp, st[win_name], idx, f"grad_w_in_{i}")
    grad_x = dxo[None]

    late_half, late_other = reduce_and_join(late, late_pair, late_from_chips, "rest")
    early_views = rs_views(early, g_first)
    early_got = _exchange_call(_rs_pair(early_views, early_kinds), "rs_pair_first")
    early_pair = pair_sums(early, early_views, early_got, "first")
    early_from_chips = _exchange_call(_rs_chips(early_pair, early_kinds), "rs_chips_first")
    early_half, early_other = reduce_and_join(early, early_pair, early_from_chips, "first")

    updates = {}
    for (k, l0, _), own, oth in zip(late, late_half, late_other):
        updates[k] = _adamw_big(_as3d(w[k]), _as3d(mom[k]), _as3d(var[k]), own, oth, core1, f"adamw_rest_{k}", l0)
    for (k, l0, _), own, oth in zip(early, early_half, early_other):
        updates[k] = _adamw_big(_as3d(w[k]), _as3d(mom[k]), _as3d(var[k]), own, oth, core1, f"adamw_first_{k}",
                                l0, into=updates[k])
    grads, delta, new_m, new_v = {}, {}, {}, {}
    for k in BIG:
        grads[k], delta[k], new_m[k], new_v[k] = [o.reshape(w[k].shape) for o in updates[k]]

    small_full = {
        "a_dw": jnp.stack(gs["a_dw"]), "a_dw_b": jnp.concatenate(gs["a_dw_b"]), "a_ln_g": jnp.concatenate(gs["a_ln_g"]),
        "a_ln_b": jnp.concatenate(gs["a_ln_b"]), "c_scale": g_cscale, "f_dw": jnp.stack(gs["f_dw"]),
        "b_ln_g": g_blg, "b_ln_b": g_blb, "b_ws": g_ws[None], "b_bs": jnp.transpose(g_bs_t)[None],
        "ln1_g": jnp.concatenate(gs["ln1_g"]), "ln1_b": jnp.concatenate(gs["ln1_b"]),
        "ln2_g": jnp.concatenate(gs["ln2_g"]), "ln2_b": jnp.concatenate(gs["ln2_b"]),
    }
    small_names = SHARDED_SMALL + REPLICATED
    small_shapes = [small_full[n].shape for n in small_names]
    small_packed = _pack([small_full[n] for n in small_names])
    gathered_small = _allgather_small(small_packed, "allgather_small_grads")
    me1 = (2 * chip + lax.axis_index("c")).astype(jnp.int32).reshape(1)
    summed = _unpack(_sum_devices(small_packed, gathered_small, me1, "small_grad_sum"), small_shapes)
    for n, a in zip(small_names, summed):
        if n in SHARDED_SMALL:
            cs = w[n].shape[-1]
            a = lax.dynamic_slice_in_dim(a, chip * cs, cs, axis=a.ndim - 1)
        grads[n] = a

    shapes = [w[n].shape for n in small_names]
    ds_, ms_, vs_ = _adamw(_pack([w[n] for n in small_names]), _pack([grads[n] for n in small_names]),
                           _pack([mom[n] for n in small_names]), _pack([var[n] for n in small_names]), "adamw_small")
    for n, a, bb, cc in zip(small_names, _unpack(ds_, shapes), _unpack(ms_, shapes), _unpack(vs_, shapes)):
        delta[n], new_m[n], new_v[n] = a, bb, cc

    return (loss, grad_x, *[grads[n] for n in WEIGHTS], *[delta[n] for n in WEIGHTS],
            *[new_m[n] for n in WEIGHTS], *[new_v[n] for n in WEIGHTS])
```

```python
import math

import jax
import jax.numpy as jnp
from jax import lax
from jax.experimental import pallas as pl
from jax.experimental.pallas import tpu as pltpu

F32 = jnp.float32
BF16 = jnp.bfloat16

LN_EPS = 1e-5
POOL_WINDOWS = (2, 4, 8, 16)
CHUNK = 128
ADAM_LR = 0.001
ADAM_B1 = 0.9
ADAM_B2 = 0.999
ADAM_EPS = 1e-08
ADAM_WD = 0.01
ADAM_STEP = 10

LANES = 128
SUBLANES_BF16 = 16
N_CHIPS = 4
N_DEV = 8
VMEM_LIMIT = 60 * 1024 * 1024

TM_FFN = 512
TM_FFN_BWD1 = 256
TM_CONV = 256
TM_SGU = 512
TM_POOL = 512
TM_BWD_IN = 512
TS_MM_TN = 1024
CW_FFN = 256
CW_FFN_BWD2 = 512
CONV_HALO = 32
CONV_ROW_BLOCK = 64
POOL_HALO = 16
FFN_HALO = 16

MESH = pl.DeviceIdType.MESH


def _cparams(n_grid=1, parallel=False):
    sem = ("parallel" if parallel else "arbitrary",) * n_grid
    return pltpu.CompilerParams(dimension_semantics=sem, vmem_limit_bytes=VMEM_LIMIT)


def _resident(block, imap):
    return pl.BlockSpec(block, imap, pipeline_mode=pl.Buffered(1))


def _wspec(w, l):
    _, r, c = w.shape
    return _resident((None, r, c), lambda *_: (l, 0, 0))


def _rowspec(d):
    return pl.BlockSpec((1, d), lambda *_: (0, 0))


def _dot(a, b):
    return jnp.dot(a, b, preferred_element_type=F32)


def _dot_nt(a, b):
    return lax.dot_general(a, b, (((1,), (1,)), ((), ())), preferred_element_type=F32)


def _dot_tn(a, b):
    return lax.dot_general(a, b, (((0,), (0,)), ((), ())), preferred_element_type=F32)


def _sigmoid(x):
    return jax.nn.sigmoid(x)


def _ln_stats(r):
    mu = jnp.mean(r, axis=1, keepdims=True)
    xc = r - mu
    var = jnp.mean(xc * xc, axis=1, keepdims=True)
    rstd = lax.rsqrt(var + LN_EPS)
    return xc * rstd, rstd


def _ln_bwd(dy, xhat, rstd, g):
    dxh = dy * g
    m1 = jnp.mean(dxh, axis=1, keepdims=True)
    m2 = jnp.mean(dxh * xhat, axis=1, keepdims=True)
    return rstd * (dxh - m1 - xhat * m2)


def _colsum(v):
    return jnp.sum(v, axis=0, keepdims=True)


def _gelu(z):
    return 0.5 * z * (1.0 + lax.erf(z * (1.0 / math.sqrt(2.0))))


def _gelu_grad(z):
    cdf = 0.5 * (1.0 + lax.erf(z * (1.0 / math.sqrt(2.0))))
    pdf = jnp.exp(-0.5 * z * z) * (1.0 / math.sqrt(2.0 * math.pi))
    return cdf + z * pdf


def _shift_down(v, k, prev_rows):
    rolled = pltpu.roll(v, k, 0)
    head = rolled[0:8]
    rows = lax.broadcasted_iota(jnp.int32, head.shape, 0)
    for r in range(k):
        head = jnp.where(rows == r, prev_rows[k - 1 - r], head)
    return jnp.concatenate([head, rolled[8:]], axis=0)


def _shift_up(v, k, next_rows):
    tm = v.shape[0]
    rolled = pltpu.roll(v, tm - k, 0)
    tail = rolled[tm - 8:tm]
    rows = lax.broadcasted_iota(jnp.int32, tail.shape, 0)
    for r in range(k):
        tail = jnp.where(rows == 8 - k + r, next_rows[r], tail)
    return jnp.concatenate([rolled[0:tm - 8], tail], axis=0)


def _fill_shifted(base_scr, sh_scr):
    nrows = sh_scr.shape[1]
    for r in range(1, 8):
        sh_scr[r - 1, :, :] = base_scr[pl.ds(r, nrows), :]


def _tap(base_scr, sh_scr, off, r0, nrows, cols):
    q, r = divmod(off, 8)
    if r == 0:
        return base_scr[pl.ds(r0 + 8 * q, nrows), cols]
    return sh_scr[r - 1, pl.ds(r0 + 8 * q, nrows), cols]


def _pick_rows(r, c, itemsize, cap_bytes):
    best = None
    for t in range(16, r + 1, 16):
        if r % t == 0 and t * c * itemsize <= cap_bytes:
            best = t
    return best if best is not None else r


def _ffn_conv_cols(h, dw_ref, c0, cw, prev1, prev2):
    kw = dw_ref.shape[0]
    h1 = _shift_down(h, 1, [prev1])
    h2 = _shift_down(h, 2, [prev1, prev2])
    hc = dw_ref[kw - 1:kw, c0:c0 + cw] * h + dw_ref[kw - 2:kw - 1, c0:c0 + cw] * h1 + dw_ref[kw - 3:kw - 2, c0:c0 + cw] * h2
    return hc, h1, h2


def _ffn_fwd(xh1, g1, b1, wup, wdn, lw, fdw, l, alpha, name, hook=None):
    s, d = xh1.shape
    f2 = wup.shape[2]
    f = f2 // 2
    tm = min(TM_FFN, s)
    cw = min(CW_FFN, f)
    n, nck = s // tm, f // cw
    assert fdw.shape[1] == 3 and s % tm == 0 and f % cw == 0

    def body(xh_ref, g_ref, b_ref, wup_ref, dw_ref, wdn_ref, xo_ref, rs_ref, hs_ref, hcs_ref, carry):
        @pl.when(pl.program_id(0) == 0)
        def _():
            carry[...] = jnp.zeros_like(carry)

        x1 = xh_ref[...] * g_ref[...] + b_ref[...]
        xb = x1.astype(BF16)
        o = jnp.zeros((tm, d), F32)

        def up_proj(j):
            return [_dot(xb, wup_ref[:, half * f + j * cw:half * f + (j + 1) * cw]) for half in range(2)]

        ahead = up_proj(0)
        for j in range(nck):
            hh = ahead
            if j + 1 < nck:
                ahead = up_proj(j + 1)
            parts = []
            for half in range(2):
                c0 = half * f + j * cw
                h = hh[half]
                hs_ref[:, c0:c0 + cw] = h.astype(BF16)
                hc, _, _ = _ffn_conv_cols(h, dw_ref, c0, cw, carry[7:8, c0:c0 + cw], carry[6:7, c0:c0 + cw])
                carry[:, c0:c0 + cw] = h[tm - 8:tm, :]
                hcs_ref[:, c0:c0 + cw] = hc.astype(BF16)
                parts.append(hc)
            gg, vv = parts
            a = (gg * _sigmoid(gg) * vv).astype(BF16)
            o = o + _dot(a, wdn_ref[j * cw:(j + 1) * cw, :])
        xhat, rstd = _ln_stats(alpha * x1 + o)
        xo_ref[...] = xhat
        rs_ref[...] = rstd

    tile = pl.BlockSpec((tm, d), lambda i: (i, 0))
    return _hosted_call(
        body, hook, n,
        in_specs=[tile, _rowspec(d), _rowspec(d), _wspec(wup, lw),
                  pl.BlockSpec((None, 3, f2), lambda i: (l, 0, 0)), _wspec(wdn, lw)],
        out_specs=[tile, pl.BlockSpec((tm, 1), lambda i: (i, 0)), pl.BlockSpec((tm, f2), lambda i: (i, 0)),
                   pl.BlockSpec((tm, f2), lambda i: (i, 0))],
        out_shape=[jax.ShapeDtypeStruct((s, d), F32), jax.ShapeDtypeStruct((s, 1), F32),
                   jax.ShapeDtypeStruct((s, f2), BF16), jax.ShapeDtypeStruct((s, f2), BF16)],
        scratch_shapes=[pltpu.VMEM((8, f2), F32)],
        args=(xh1, g1, b1, wup, fdw, wdn), name=name)


def _ffn_bwd1(dx2, xh2, rstd2, g2, b2, hcs, wdn, lw, gwdn_buf, l, name, hook=None, loss_head=False):
    s, d = dx2.shape
    f2 = hcs.shape[1]
    f = f2 // 2
    tm = min(TM_FFN_BWD1, s)
    cw = min(CW_FFN, f)
    n, nck = s // tm, f // cw

    def body(dx_ref, xh_ref, rs_ref, g_ref, b_ref, hcs_ref, wdn_ref, buf_ref,
             dr_ref, dhc_ref, gwdn_ref, gg_ref, gb_ref, loss_ref):
        @pl.when(pl.program_id(0) == 0)
        def _():
            gwdn_ref[...] = jnp.zeros_like(gwdn_ref)
            gg_ref[...] = jnp.zeros_like(gg_ref)
            gb_ref[...] = jnp.zeros_like(gb_ref)
            loss_ref[...] = jnp.zeros_like(loss_ref)

        xh = xh_ref[...]
        if loss_head:
            err = xh * g_ref[...] + b_ref[...] - dx_ref[...]
            dx = err * (1.0 / d)
            loss_ref[...] += (0.5 / d) * jnp.sum(_colsum(err * err), axis=1, keepdims=True)
        else:
            dx = dx_ref[...]
        gg_ref[...] += _colsum(dx * xh)
        gb_ref[...] += _colsum(dx)
        dr = _ln_bwd(dx, xh, rs_ref[...], g_ref[...])
        dr_ref[...] = dr
        dob = dr.astype(BF16)

        def d_act(j):
            return _dot_nt(dob, wdn_ref[j * cw:(j + 1) * cw, :])

        da_ahead = d_act(0)
        for j in range(nck):
            da = da_ahead
            if j + 1 < nck:
                da_ahead = d_act(j + 1)
            gt = hcs_ref[:, j * cw:(j + 1) * cw].astype(F32)
            vv = hcs_ref[:, f + j * cw:f + (j + 1) * cw].astype(F32)
            sg = _sigmoid(gt)
            sl = gt * sg
            a = (sl * vv).astype(BF16)
            gwdn_ref[j * cw:(j + 1) * cw, :] += _dot_tn(a, dob)
            dhc_ref[:, j * cw:(j + 1) * cw] = (da * vv * (sg * (1.0 + gt * (1.0 - sg)))).astype(BF16)
            dhc_ref[:, f + j * cw:f + (j + 1) * cw] = (da * sl).astype(BF16)

    tile = pl.BlockSpec((tm, d), lambda i: (i, 0))
    wide = pl.BlockSpec((tm, f2), lambda i: (i, 0))
    nl = gwdn_buf.shape[0]
    return _hosted_call(
        body, hook, n,
        in_specs=[tile, tile, pl.BlockSpec((tm, 1), lambda i: (i, 0)), _rowspec(d), _rowspec(d), wide,
                  _wspec(wdn, lw), pl.BlockSpec(memory_space=pl.ANY)],
        out_specs=[tile, wide, pl.BlockSpec((None, f, d), lambda i: (l, 0, 0)), _rowspec(d), _rowspec(d),
                   pl.BlockSpec((1, 1), lambda i: (0, 0))],
        out_shape=[jax.ShapeDtypeStruct((s, d), F32), jax.ShapeDtypeStruct((s, f2), BF16),
                   jax.ShapeDtypeStruct((nl, f, d), F32),
                   jax.ShapeDtypeStruct((1, d), F32), jax.ShapeDtypeStruct((1, d), F32),
                   jax.ShapeDtypeStruct((1, 1), F32)],
        scratch_shapes=[], args=(dx2, xh2, rstd2, g2, b2, hcs, wdn, gwdn_buf), name=name, aliases={7: 2})


def _bwd_in(dp, dres, w, l, alpha, name):
    s, d = dres.shape
    nn = dp.shape[1]
    tm = min(TM_BWD_IN, s)
    n = s // tm
    tile = pl.BlockSpec((tm, d), lambda i: (i, 0))

    def body(dp_ref, dres_ref, w_ref, o_ref):
        o_ref[...] = alpha * dres_ref[...] + _dot_nt(dp_ref[...], w_ref[...])

    return pl.pallas_call(
        body, grid=(n,),
        in_specs=[pl.BlockSpec((tm, nn), lambda i: (i, 0)), tile, _wspec(w, l)],
        out_specs=tile, out_shape=jax.ShapeDtypeStruct((s, d), F32),
        compiler_params=_cparams(parallel=True), name=name,
    )(dp, dres, w)


def _ffn_bwd2(dhc, hs, dres, wup, lw, fdw, l, xh, rstd, g, alpha, name):
    s, d = dres.shape
    f2 = dhc.shape[1]
    tm = min(TM_BWD_IN, s)
    n = s // tm
    hb = FFN_HALO
    cw = min(CW_FFN_BWD2, f2)
    nck = f2 // cw
    halo_blocks = tm // hb
    assert f2 % cw == 0 and fdw.shape[1] == 3

    def body(dhc_ref, halo_ref, hs_ref, dres_ref, w_ref, dw_ref, xh_ref, rs_ref, g_ref,
             o_ref, dh_ref, gdw_ref, gg_ref, gb_ref):
        i = pl.program_id(0)

        @pl.when(i == 0)
        def _():
            gdw_ref[...] = jnp.zeros_like(gdw_ref)
            gg_ref[...] = jnp.zeros_like(gg_ref)
            gb_ref[...] = jnp.zeros_like(gb_ref)

        has_next = i < n - 1
        dx = alpha * dres_ref[...]
        for j in range(nck):
            c0 = j * cw
            dc = dhc_ref[:, c0:c0 + cw].astype(F32)
            hal = jnp.where(has_next, halo_ref[:, c0:c0 + cw].astype(F32), 0.0)
            nxt = [hal[0:1], hal[1:2]]
            u1 = _shift_up(dc, 1, nxt[:1])
            u2 = _shift_up(dc, 2, nxt)
            h = hs_ref[:, c0:c0 + cw].astype(F32)
            gdw_ref[2:3, c0:c0 + cw] += _colsum(dc * h)
            gdw_ref[1:2, c0:c0 + cw] += _colsum(u1 * h)
            gdw_ref[0:1, c0:c0 + cw] += _colsum(u2 * h)
            dh = (dw_ref[2:3, c0:c0 + cw] * dc + dw_ref[1:2, c0:c0 + cw] * u1
                  + dw_ref[0:1, c0:c0 + cw] * u2).astype(BF16)
            dh_ref[:, c0:c0 + cw] = dh
            dx = dx + _dot_nt(dh, w_ref[:, c0:c0 + cw])
        xhv = xh_ref[...]
        gg_ref[...] += _colsum(dx * xhv)
        gb_ref[...] += _colsum(dx)
        o_ref[...] = _ln_bwd(dx, xhv, rs_ref[...], g_ref[...])

    tile = pl.BlockSpec((tm, d), lambda i: (i, 0))
    wide = pl.BlockSpec((tm, f2), lambda i: (i, 0))
    return pl.pallas_call(
        body, grid=(n,),
        in_specs=[wide, pl.BlockSpec((hb, f2), lambda i: (jnp.minimum((i + 1) * halo_blocks, s // hb - 1), 0)),
                  wide, tile, _wspec(wup, lw), pl.BlockSpec((None, 3, f2), lambda i: (l, 0, 0)), tile,
                  pl.BlockSpec((tm, 1), lambda i: (i, 0)), _rowspec(d)],
        out_specs=[tile, wide, pl.BlockSpec((3, f2), lambda i: (0, 0)), _rowspec(d), _rowspec(d)],
        out_shape=[jax.ShapeDtypeStruct((s, d), F32), jax.ShapeDtypeStruct((s, f2), BF16),
                   jax.ShapeDtypeStruct((3, f2), F32),
                   jax.ShapeDtypeStruct((1, d), F32), jax.ShapeDtypeStruct((1, d), F32)],
        compiler_params=_cparams(), name=name,
    )(dhc, dhc, hs, dres, wup, fdw, xh, rstd, g)


def _mm_tn(a, ga, ba, bm, buf, l, name):
    s, k = a.shape
    nn = bm.shape[1]
    ts = min(TS_MM_TN, s)
    tn = nn // N_CHIPS if nn > 1024 else nn
    nj, ns = nn // tn, s // ts

    def body(a_ref, g_ref, b_ref, bm_ref, buf_ref, o_ref):
        @pl.when(pl.program_id(1) == 0)
        def _():
            o_ref[...] = jnp.zeros_like(o_ref)

        ab = (a_ref[...] * g_ref[...] + b_ref[...]).astype(BF16)
        o_ref[...] += _dot_tn(ab, bm_ref[...])

    return pl.pallas_call(
        body, grid=(nj, ns),
        in_specs=[pl.BlockSpec((ts, k), lambda j, t: (t, 0)), _rowspec(k), _rowspec(k),
                  pl.BlockSpec((ts, tn), lambda j, t: (t, j)), pl.BlockSpec(memory_space=pl.ANY)],
        out_specs=pl.BlockSpec((None, k, tn), lambda j, t: (l, 0, j)),
        out_shape=jax.ShapeDtypeStruct(buf.shape, F32),
        input_output_aliases={4: 0},
        compiler_params=_cparams(2), name=name,
    )(a, ga, ba, bm, buf)


def _conv_fwd(xin, gin, bin_, win, wout, lw, adw, l, adwb, lng, lnb, alpha, name, hook=None):
    s, d = xin.shape
    kw = adw.shape[1]
    hb = CONV_HALO
    tm = min(TM_CONV, s)
    n = s // tm
    assert kw - 1 <= hb <= tm

    def body(x_ref, g_ref, b_ref, win_ref, dw_ref, dwb_ref, lng_ref, lnb_ref, wout_ref,
             xo_ref, rs_ref, p_ref, chat_ref, rsc_ref, u_scr, u8_scr):
        @pl.when(pl.program_id(0) == 0)
        def _():
            u_scr[0:hb, :] = jnp.zeros((hb, d), F32)

        x = x_ref[...] * g_ref[...] + b_ref[...]
        pm = _dot(x.astype(BF16), win_ref[...])
        p_ref[...] = pm.astype(BF16)
        u = pm[:, :d] * _sigmoid(pm[:, d:])
        u_scr[hb:hb + tm, :] = u
        _fill_shifted(u_scr, u8_scr)
        acc = dwb_ref[...] + dw_ref[kw - 1:kw, :] * u
        for k in range(kw - 1):
            acc = acc + dw_ref[k:k + 1, :] * _tap(u_scr, u8_scr, hb - (kw - 1) + k, 0, tm, slice(None))
        u_scr[0:hb, :] = u_scr[tm:tm + hb, :]
        chat, rstdc = _ln_stats(acc)
        chat_ref[...] = chat.astype(BF16)
        rsc_ref[...] = rstdc
        nv = chat * lng_ref[...] + lnb_ref[...]
        sv = (nv * _sigmoid(nv)).astype(BF16)
        xhat, rstd = _ln_stats(alpha * x + _dot(sv, wout_ref[...]))
        xo_ref[...] = xhat
        rs_ref[...] = rstd

    tile = pl.BlockSpec((tm, d), lambda i: (i, 0))
    col = pl.BlockSpec((tm, 1), lambda i: (i, 0))
    return _hosted_call(
        body, hook, n,
        in_specs=[tile, _rowspec(d), _rowspec(d), _wspec(win, lw),
                  pl.BlockSpec((None, kw, d), lambda i: (l, 0, 0)), _rowspec(d), _rowspec(d), _rowspec(d),
                  _wspec(wout, lw)],
        out_specs=[tile, col, pl.BlockSpec((tm, 2 * d), lambda i: (i, 0)), tile, col],
        out_shape=[jax.ShapeDtypeStruct((s, d), F32), jax.ShapeDtypeStruct((s, 1), F32),
                   jax.ShapeDtypeStruct((s, 2 * d), BF16), jax.ShapeDtypeStruct((s, d), BF16),
                   jax.ShapeDtypeStruct((s, 1), F32)],
        scratch_shapes=[pltpu.VMEM((tm + hb, d), F32), pltpu.VMEM((7, tm + hb - 8, d), F32)],
        args=(xin, gin, bin_, win, adw, adwb, lng, lnb, wout), name=name)


def _conv_bwd1(dr1, chat, rstdc, p, wout, lw, adw, lt, lng, lnb, gwout_buf, l, name, hook=None):
    s, d = dr1.shape
    kw = adw.shape[1]
    hb = CONV_HALO
    tm = min(TM_CONV, s)
    n = s // tm
    halo_blocks = tm // hb
    rbl = CONV_ROW_BLOCK

    def body(dr_ref, chat_ref, rsc_ref, p_ref, halo_ref, wout_ref, dw_ref, lng_ref, lnb_ref, buf_ref,
             dp_ref, gwout_ref, gdw_ref, gdwb_ref, glng_ref, glnb_ref, u_scr, dc_scr, u8_scr, dc8_scr):
        i = pl.program_id(0)
        t = n - 1 - i

        @pl.when(i == 0)
        def _():
            dc_scr[tm:tm + hb, :] = jnp.zeros((hb, d), F32)
            gwout_ref[...] = jnp.zeros_like(gwout_ref)
            gdw_ref[...] = jnp.zeros_like(gdw_ref)
            gdwb_ref[...] = jnp.zeros_like(gdwb_ref)
            glng_ref[...] = jnp.zeros_like(glng_ref)
            glnb_ref[...] = jnp.zeros_like(glnb_ref)

        dob = dr_ref[...].astype(BF16)
        chat = chat_ref[...].astype(F32)
        lng = lng_ref[...]
        nv = chat * lng + lnb_ref[...]
        sgn = _sigmoid(nv)
        gwout_ref[...] += _dot_tn((nv * sgn).astype(BF16), dob)
        dn = _dot_nt(dob, wout_ref[...]) * (sgn * (1.0 + nv * (1.0 - sgn)))
        glng_ref[...] += _colsum(dn * chat)
        glnb_ref[...] += _colsum(dn)
        dc = _ln_bwd(dn, chat, rsc_ref[...], lng)
        gdwb_ref[...] += _colsum(dc)

        pm = p_ref[...].astype(F32)
        a = pm[:, :d]
        sg = _sigmoid(pm[:, d:])
        ph = halo_ref[...].astype(F32)
        u_scr[0:hb, :] = jnp.where(t > 0, ph[:, :d] * _sigmoid(ph[:, d:]), 0.0)
        u_scr[hb:hb + tm, :] = a * sg
        dc_scr[0:tm, :] = dc
        _fill_shifted(u_scr, u8_scr)
        _fill_shifted(dc_scr, dc8_scr)
        du = dw_ref[kw - 1:kw, :] * dc
        for k in range(kw - 1):
            du = du + dw_ref[k:k + 1, :] * _tap(dc_scr, dc8_scr, kw - 1 - k, 0, tm, slice(None))
        for cb in range(d // LANES):
            cols = pl.ds(cb * LANES, LANES)

            def rows_step(rb, accs, cols=cols):
                r0 = pl.multiple_of(rb * rbl, rbl)
                dcb = dc_scr[pl.ds(r0, rbl), cols]
                out = []
                for k in range(kw):
                    prod = dcb * _tap(u_scr, u8_scr, hb - (kw - 1) + k, r0, rbl, cols)
                    part = prod[0:8]
                    for g8 in range(1, rbl // 8):
                        part = part + prod[8 * g8:8 * g8 + 8]
                    out.append(accs[k] + part)
                return tuple(out)

            accs = lax.fori_loop(0, tm // rbl, rows_step, tuple(jnp.zeros((8, LANES), F32) for _ in range(kw)))
            for k in range(kw):
                gdw_ref[k:k + 1, cols] += _colsum(accs[k])
        dc_scr[tm:tm + hb, :] = dc[0:hb, :]
        dp_ref[:, :d] = (du * sg).astype(BF16)
        dp_ref[:, d:] = (du * a * sg * (1.0 - sg)).astype(BF16)

    tile = pl.BlockSpec((tm, d), lambda i: (n - 1 - i, 0))
    col = pl.BlockSpec((tm, 1), lambda i: (n - 1 - i, 0))
    nl = gwout_buf.shape[0]
    return _hosted_call(
        body, hook, n,
        in_specs=[tile, tile, col, pl.BlockSpec((tm, 2 * d), lambda i: (n - 1 - i, 0)),
                  pl.BlockSpec((hb, 2 * d), lambda i: (jnp.maximum((n - 1 - i) * halo_blocks - 1, 0), 0)),
                  _wspec(wout, lw), pl.BlockSpec((None, kw, d), lambda i: (lt, 0, 0)), _rowspec(d), _rowspec(d),
                  pl.BlockSpec(memory_space=pl.ANY)],
        out_specs=[pl.BlockSpec((tm, 2 * d), lambda i: (n - 1 - i, 0)),
                   pl.BlockSpec((None, d, d), lambda i: (l, 0, 0)),
                   pl.BlockSpec((kw, d), lambda i: (0, 0)), _rowspec(d), _rowspec(d), _rowspec(d)],
        out_shape=[jax.ShapeDtypeStruct((s, 2 * d), BF16), jax.ShapeDtypeStruct((nl, d, d), F32),
                   jax.ShapeDtypeStruct((kw, d), F32), jax.ShapeDtypeStruct((1, d), F32),
                   jax.ShapeDtypeStruct((1, d), F32), jax.ShapeDtypeStruct((1, d), F32)],
        scratch_shapes=[pltpu.VMEM((tm + hb, d), F32), pltpu.VMEM((tm + hb, d), F32),
                        pltpu.VMEM((7, tm + hb - 8, d), F32), pltpu.VMEM((7, tm + hb - 8, d), F32)],
        args=(dr1, chat, rstdc, p, p, wout, adw, lng, lnb, gwout_buf), name=name, aliases={9: 1})


def _sgu_gate(vn, wm_ref, bs_ref, s_scr, tm, nh):
    for ch in range(tm // CHUNK):
        r0 = ch * CHUNK
        for h in range(nh):
            c0 = h * CHUNK
            s_scr[r0:r0 + CHUNK, c0:c0 + CHUNK] = (
                _dot(wm_ref[h], vn[r0:r0 + CHUNK, c0:c0 + CHUNK]) + bs_ref[:, c0:c0 + CHUNK])


def _sgu_fwd(xin, gin, bin_, win, lg, lb, wm, bs_exp, wout, alpha, name):
    s, d = xin.shape
    nh = wm.shape[0]
    tm = min(TM_SGU, s)
    n = s // tm
    assert tm % CHUNK == 0 and nh * CHUNK == d

    def body(x_ref, g_ref, b_ref, win_ref, lg_ref, lb_ref, wm_ref, bs_ref, wout_ref,
             xo_ref, rs_ref, zp_ref, s_scr):
        x = x_ref[...] * g_ref[...] + b_ref[...]
        zp = _dot(x.astype(BF16), win_ref[...])
        zp_ref[...] = zp.astype(BF16)
        z = _gelu(zp)
        vhat, _ = _ln_stats(z[:, d:])
        vn = (vhat * lg_ref[...] + lb_ref[...]).astype(BF16)
        _sgu_gate(vn, wm_ref, bs_ref, s_scr, tm, nh)
        q = (z[:, :d] * s_scr[...]).astype(BF16)
        xhat, rstd = _ln_stats(alpha * x + _dot(q, wout_ref[...]))
        xo_ref[...] = xhat
        rs_ref[...] = rstd

    tile = pl.BlockSpec((tm, d), lambda i: (i, 0))
    return pl.pallas_call(
        body, grid=(n,),
        in_specs=[tile, _rowspec(d), _rowspec(d), _wspec(win, 0), _rowspec(d), _rowspec(d),
                  _resident((nh, CHUNK, CHUNK), lambda i: (0, 0, 0)),
                  _resident((CHUNK, d), lambda i: (0, 0)), _wspec(wout, 0)],
        out_specs=[tile, pl.BlockSpec((tm, 1), lambda i: (i, 0)), pl.BlockSpec((tm, 2 * d), lambda i: (i, 0))],
        out_shape=[jax.ShapeDtypeStruct((s, d), F32), jax.ShapeDtypeStruct((s, 1), F32),
                   jax.ShapeDtypeStruct((s, 2 * d), BF16)],
        scratch_shapes=[pltpu.VMEM((tm, d), F32)],
        compiler_params=_cparams(parallel=True), name=name,
    )(xin, gin, bin_, win, lg, lb, wm, bs_exp, wout)


def _sgu_bwd1(dr1, zp, wout, lg, lb, wm, wmt, bs_exp, gwout_buf, name):
    s, d = dr1.shape
    nh = wm.shape[0]
    tm = min(TM_SGU, s)
    n = s // tm

    def body(dr_ref, zp_ref, wout_ref, lg_ref, lb_ref, wm_ref, wmt_ref, bs_ref, buf_ref,
             dzp_ref, gwout_ref, gws_ref, gbs_ref, glg_ref, glb_ref, s_scr, dvn_scr, bs_acc):
        i = pl.program_id(0)

        @pl.when(i == 0)
        def _():
            gwout_ref[...] = jnp.zeros_like(gwout_ref)
            gws_ref[...] = jnp.zeros_like(gws_ref)
            glg_ref[...] = jnp.zeros_like(glg_ref)
            glb_ref[...] = jnp.zeros_like(glb_ref)
            bs_acc[...] = jnp.zeros_like(bs_acc)

        dob = dr_ref[...].astype(BF16)
        zp = zp_ref[...].astype(F32)
        z = _gelu(zp)
        u = z[:, :d]
        lg = lg_ref[...]
        vhat, rstdv = _ln_stats(z[:, d:])
        vn = (vhat * lg + lb_ref[...]).astype(BF16)
        _sgu_gate(vn, wm_ref, bs_ref, s_scr, tm, nh)
        sv = s_scr[...]
        gwout_ref[...] += _dot_tn((u * sv).astype(BF16), dob)
        dq = _dot_nt(dob, wout_ref[...])
        ds = dq * u
        dsb = ds.astype(BF16)
        part = jnp.zeros((CHUNK, d), F32)
        for ch in range(tm // CHUNK):
            r0 = ch * CHUNK
            part = part + ds[r0:r0 + CHUNK, :]
            for h in range(nh):
                c0 = h * CHUNK
                blk = dsb[r0:r0 + CHUNK, c0:c0 + CHUNK]
                gws_ref[h] += _dot_nt(blk, vn[r0:r0 + CHUNK, c0:c0 + CHUNK])
                dvn_scr[r0:r0 + CHUNK, c0:c0 + CHUNK] = _dot(wmt_ref[h], blk)
        bs_acc[...] += part
        dvn = dvn_scr[...]
        glg_ref[...] += _colsum(dvn * vhat)
        glb_ref[...] += _colsum(dvn)
        dv = _ln_bwd(dvn, vhat, rstdv, lg)
        gp = _gelu_grad(zp)
        dzp_ref[:, :d] = (dq * sv * gp[:, :d]).astype(BF16)
        dzp_ref[:, d:] = (dv * gp[:, d:]).astype(BF16)

        @pl.when(i == n - 1)
        def _():
            rows = lax.broadcasted_iota(jnp.int32, (CHUNK, CHUNK), 0)
            cols = lax.broadcasted_iota(jnp.int32, (CHUNK, CHUNK), 1)
            tril = (cols <= rows).astype(F32)
            acc = bs_acc[...]
            for h in range(nh):
                gws_ref[h] = gws_ref[h] * tril
                gbs_ref[:, h:h + 1] = jnp.sum(acc[:, h * CHUNK:(h + 1) * CHUNK], axis=1, keepdims=True)

    tile = pl.BlockSpec((tm, d), lambda i: (i, 0))
    wide = pl.BlockSpec((tm, 2 * d), lambda i: (i, 0))
    hspec = _resident((nh, CHUNK, CHUNK), lambda i: (0, 0, 0))
    return pl.pallas_call(
        body, grid=(n,),
        in_specs=[tile, wide, _wspec(wout, 0), _rowspec(d), _rowspec(d), hspec, hspec,
                  _resident((CHUNK, d), lambda i: (0, 0)), pl.BlockSpec(memory_space=pl.ANY)],
        out_specs=[wide, pl.BlockSpec((None, d, d), lambda i: (0, 0, 0)),
                   pl.BlockSpec((nh, CHUNK, CHUNK), lambda i: (0, 0, 0)),
                   pl.BlockSpec((CHUNK, nh), lambda i: (0, 0)), _rowspec(d), _rowspec(d)],
        out_shape=[jax.ShapeDtypeStruct((s, 2 * d), BF16), jax.ShapeDtypeStruct(gwout_buf.shape, F32),
                   jax.ShapeDtypeStruct((nh, CHUNK, CHUNK), F32), jax.ShapeDtypeStruct((CHUNK, nh), F32),
                   jax.ShapeDtypeStruct((1, d), F32), jax.ShapeDtypeStruct((1, d), F32)],
        scratch_shapes=[pltpu.VMEM((tm, d), F32), pltpu.VMEM((tm, d), F32), pltpu.VMEM((CHUNK, d), F32)],
        input_output_aliases={8: 1},
        compiler_params=_cparams(), name=name,
    )(dr1, zp, wout, lg, lb, wm, wmt, bs_exp, gwout_buf)


def _pool_counts(t0, tm, w):
    pos = t0 + lax.broadcasted_iota(jnp.int32, (tm, 1), 0)
    return jnp.minimum(pos + 1, w).astype(F32)


def _pool_fwd(xin, gin, bin_, win, wg, scale, wout, alpha, name):
    s, d = xin.shape
    ng, dg = wg.shape[0], wg.shape[1]
    hb = POOL_HALO
    tm = min(TM_POOL, s)
    n = s // tm
    assert ng == len(POOL_WINDOWS) and ng * dg == d and max(POOL_WINDOWS) <= hb

    def body(x_ref, g_ref, b_ref, win_ref, wg_ref, sc_ref, wout_ref, xo_ref, rs_ref, ys_ref, y_scr, z_scr):
        i = pl.program_id(0)

        @pl.when(i == 0)
        def _():
            y_scr[0:hb, :] = jnp.zeros((hb, d), F32)

        x = x_ref[...] * g_ref[...] + b_ref[...]
        y = _dot(x.astype(BF16), win_ref[...])
        ys_ref[...] = y.astype(BF16)
        y_scr[hb:hb + tm, :] = y
        for g, w in enumerate(POOL_WINDOWS):
            c0 = g * dg
            acc = y[:, c0:c0 + dg]
            for dd in range(1, w):
                acc = acc + y_scr[pl.ds(hb - dd, tm), c0:c0 + dg]
            pg = acc / _pool_counts(i * tm, tm, w) - y[:, c0:c0 + dg]
            z_scr[:, c0:c0 + dg] = _dot(pg.astype(BF16), wg_ref[g])
        y_scr[0:hb, :] = y_scr[tm:tm + hb, :]
        zz = (z_scr[...] * sc_ref[...]).astype(BF16)
        xhat, rstd = _ln_stats(alpha * x + _dot(zz, wout_ref[...]))
        xo_ref[...] = xhat
        rs_ref[...] = rstd

    tile = pl.BlockSpec((tm, d), lambda i: (i, 0))
    return pl.pallas_call(
        body, grid=(n,),
        in_specs=[tile, _rowspec(d), _rowspec(d), _wspec(win, 0),
                  _resident((ng, dg, dg), lambda i: (0, 0, 0)), _rowspec(d), _wspec(wout, 0)],
        out_specs=[tile, pl.BlockSpec((tm, 1), lambda i: (i, 0)), tile],
        out_shape=[jax.ShapeDtypeStruct((s, d), F32), jax.ShapeDtypeStruct((s, 1), F32),
                   jax.ShapeDtypeStruct((s, d), BF16)],
        scratch_shapes=[pltpu.VMEM((tm + hb, d), F32), pltpu.VMEM((tm, d), F32)],
        compiler_params=_cparams(), name=name,
    )(xin, gin, bin_, win, wg, scale, wout)


def _pool_bwd1(dr1, ys, wout, wg, scale, gwout_buf, name):
    s, d = dr1.shape
    ng, dg = wg.shape[0], wg.shape[1]
    hb = POOL_HALO
    tm = min(TM_POOL, s)
    n = s // tm
    halo_blocks = tm // hb

    def body(dr_ref, ys_ref, halo_ref, wout_ref, wg_ref, sc_ref, buf_ref,
             dy_ref, gwout_ref, gwg_ref, gsc_ref, y_scr, e_scr, z_scr, dp_scr):
        i = pl.program_id(0)
        t = n - 1 - i

        @pl.when(i == 0)
        def _():
            e_scr[tm:tm + hb, :] = jnp.zeros((hb, d), F32)
            gwout_ref[...] = jnp.zeros_like(gwout_ref)
            gwg_ref[...] = jnp.zeros_like(gwg_ref)
            gsc_ref[...] = jnp.zeros_like(gsc_ref)

        dob = dr_ref[...].astype(BF16)
        y = ys_ref[...].astype(F32)
        y_scr[0:hb, :] = jnp.where(t > 0, halo_ref[...].astype(F32), 0.0)
        y_scr[hb:hb + tm, :] = y
        pgs = []
        for g, w in enumerate(POOL_WINDOWS):
            c0 = g * dg
            acc = y[:, c0:c0 + dg]
            for dd in range(1, w):
                acc = acc + y_scr[pl.ds(hb - dd, tm), c0:c0 + dg]
            pg = (acc / _pool_counts(t * tm, tm, w) - y[:, c0:c0 + dg]).astype(BF16)
            pgs.append(pg)
            z_scr[:, c0:c0 + dg] = _dot(pg, wg_ref[g])
        zpre = z_scr[...]
        sc = sc_ref[...]
        gwout_ref[...] += _dot_tn((zpre * sc).astype(BF16), dob)
        dz = _dot_nt(dob, wout_ref[...])
        gsc_ref[...] += _colsum(dz * zpre)
        dzpre = (dz * sc).astype(BF16)
        for g, w in enumerate(POOL_WINDOWS):
            c0 = g * dg
            dzg = dzpre[:, c0:c0 + dg]
            gwg_ref[g] += _dot_tn(pgs[g], dzg)
            dp = _dot_nt(dzg, wg_ref[g])
            dp_scr[:, c0:c0 + dg] = dp
            e_scr[0:tm, c0:c0 + dg] = dp / _pool_counts(t * tm, tm, w)
        for g, w in enumerate(POOL_WINDOWS):
            c0 = g * dg
            acc = e_scr[0:tm, c0:c0 + dg]
            for dd in range(1, w):
                acc = acc + e_scr[pl.ds(dd, tm), c0:c0 + dg]
            dy_ref[:, c0:c0 + dg] = (acc - dp_scr[:, c0:c0 + dg]).astype(BF16)
        e_scr[tm:tm + hb, :] = e_scr[0:hb, :]

    tile = pl.BlockSpec((tm, d), lambda i: (n - 1 - i, 0))
    return pl.pallas_call(
        body, grid=(n,),
        in_specs=[tile, tile,
                  pl.BlockSpec((hb, d), lambda i: (jnp.maximum((n - 1 - i) * halo_blocks - 1, 0), 0)),
                  _wspec(wout, 0), _resident((ng, dg, dg), lambda i: (0, 0, 0)), _rowspec(d),
                  pl.BlockSpec(memory_space=pl.ANY)],
        out_specs=[tile, pl.BlockSpec((None, d, d), lambda i: (0, 0, 0)),
                   pl.BlockSpec((ng, dg, dg), lambda i: (0, 0, 0)), _rowspec(d)],
        out_shape=[jax.ShapeDtypeStruct((s, d), BF16), jax.ShapeDtypeStruct(gwout_buf.shape, F32),
                   jax.ShapeDtypeStruct((ng, dg, dg), F32), jax.ShapeDtypeStruct((1, d), F32)],
        scratch_shapes=[pltpu.VMEM((tm + hb, d), F32), pltpu.VMEM((tm + hb, d), F32),
                        pltpu.VMEM((tm, d), F32), pltpu.VMEM((tm, d), F32)],
        input_output_aliases={6: 1},
        compiler_params=_cparams(), name=name,
    )(dr1, ys, ys, wout, wg, scale, gwout_buf)


def _elementwise(fn, ins, out_dtypes, name):
    shape = ins[0].shape
    c = shape[-1]
    r = math.prod(shape[:-1])
    tr = _pick_rows(r, c, 4, 1 << 20)

    def body(*refs):
        vals = fn(*[ref[...] for ref in refs[:len(ins)]])
        for ref, v in zip(refs[len(ins):], vals):
            ref[...] = v.astype(ref.dtype)

    spec = pl.BlockSpec((tr, c), lambda i: (i, 0))
    outs = pl.pallas_call(
        body, grid=(r // tr,),
        in_specs=[spec] * len(ins), out_specs=[spec] * len(out_dtypes),
        out_shape=[jax.ShapeDtypeStruct((r, c), dt) for dt in out_dtypes],
        compiler_params=_cparams(parallel=True), name=name,
    )(*[a.reshape(r, c) for a in ins])
    return [o.reshape(shape) for o in outs]


def _prefetch_call(body, grid, in_specs, out_specs, out_shape, name, aliases=None):
    return pl.pallas_call(
        body,
        grid_spec=pltpu.PrefetchScalarGridSpec(num_scalar_prefetch=1, grid=grid, in_specs=in_specs, out_specs=out_specs),
        out_shape=out_shape, input_output_aliases=aliases or {},
        compiler_params=_cparams(len(grid), parallel=True), name=name)


def _cast_into_full(w3, l0, l, kind, chip1, name):
    _, r, c = w3.shape
    tr = _pick_rows(r, c, 4, 1 << 20)

    def body(k_ref, w_ref, o_ref):
        o_ref[...] = w_ref[...].astype(BF16)

    if kind == "row":
        out_spec = pl.BlockSpec((None, None, tr, c), lambda a, j, k: (a, k[0], j, 0))
    else:
        out_spec = pl.BlockSpec((None, tr, c), lambda a, j, k: (a, j, k[0]))
    return _prefetch_call(
        body, (l, r // tr), [pl.BlockSpec((None, tr, c), lambda a, j, k: (a + l0, j, 0))], out_spec,
        jax.ShapeDtypeStruct(_full_shape(kind, (l, r, c)), BF16), name)(chip1, w3)


def _pair_sum(g, got, kind, core1, name):
    if kind == "row":
        l, nc, sr, c = g.shape
        g5, got3 = g.reshape(l * nc, 2, sr // 2, c), got.reshape(l * nc, sr // 2, c)
    else:
        l, r, c = g.shape
        g5, got3 = g.reshape(l, 2, r // 2, c), got
    a, _, hr, c = g5.shape
    tr = _pick_rows(hr, c, 4, 1 << 20)

    def body(c_ref, g_ref, t_ref, o_ref):
        o_ref[...] = (g_ref[...] + t_ref[...]).astype(BF16)

    half = pl.BlockSpec((None, tr, c), lambda i, j, cc: (i, j, 0))
    out = _prefetch_call(
        body, (a, hr // tr), [pl.BlockSpec((None, None, tr, c), lambda i, j, cc: (i, cc[0], j, 0)), half], half,
        jax.ShapeDtypeStruct(got3.shape, BF16), name)(core1, g5, got3)
    return out.reshape(got.shape)


def _chip_sum(t, rb, kind, chip1, name):
    _, l, hr, sc = rb.shape
    tr = _pick_rows(hr, sc, 4, 1 << 19)

    def body(k_ref, t_ref, rb_ref, o_ref):
        acc = t_ref[...].astype(F32)
        for r in range(N_CHIPS - 1):
            acc = acc + rb_ref[r].astype(F32)
        o_ref[...] = acc

    if kind == "row":
        t_spec = pl.BlockSpec((None, None, tr, sc), lambda a, j, k: (a, k[0], j, 0))
    else:
        t_spec = pl.BlockSpec((None, tr, sc), lambda a, j, k: (a, j, k[0]))
    return _prefetch_call(
        body, (l, hr // tr),
        [t_spec, pl.BlockSpec((N_CHIPS - 1, None, tr, sc), lambda a, j, k: (0, a, j, 0))],
        pl.BlockSpec((None, tr, sc), lambda a, j, k: (a, j, 0)),
        jax.ShapeDtypeStruct((l, hr, sc), F32), name)(chip1, t, rb)


def _adamw_math(w_, g_, m_, v_):
    m2 = ADAM_B1 * m_ + (1.0 - ADAM_B1) * g_
    v2 = ADAM_B2 * v_ + (1.0 - ADAM_B2) * (g_ * g_)
    m_hat = m2 / (1.0 - ADAM_B1 ** ADAM_STEP)
    v_hat = v2 / (1.0 - ADAM_B2 ** ADAM_STEP)
    delta = -ADAM_LR * (m_hat / (jnp.sqrt(v_hat) + ADAM_EPS) + ADAM_WD * w_)
    return delta, m2, v2


def _adamw_big(w, m, v, own, other, core1, name, l0=0, into=None):
    lg, hr, c = own.shape
    lw = w.shape[0]
    view = lambda a: a.reshape(lw, 2, hr, c)
    tr = _pick_rows(hr, c, 4, 1 << 20)
    n_keep = 0 if into is None else len(into)

    def body(c_ref, w_ref, m_ref, v_ref, own_ref, oth_ref, *rest):
        g_ref, d_ref, m2_ref, v2_ref = rest[n_keep:]
        g = jnp.where(pl.program_id(1) == c_ref[0], own_ref[...], oth_ref[...])
        g_ref[...] = g
        d_ref[...], m2_ref[...], v2_ref[...] = _adamw_math(w_ref[...], g, m_ref[...], v_ref[...])

    s4 = pl.BlockSpec((None, None, tr, c), lambda a, h, j, cc: (a + l0, h, j, 0))
    s3 = pl.BlockSpec((None, tr, c), lambda a, h, j, cc: (a, j, 0))
    aliases = {6 + k: k for k in range(n_keep)}
    return _prefetch_call(
        body, (lg, 2, hr // tr), [s4, s4, s4, s3, s3] + [ANY] * n_keep, [s4] * 4,
        [jax.ShapeDtypeStruct((lw, 2, hr, c), F32)] * 4, name, aliases,
    )(core1, view(w), view(m), view(v), own, other, *(into or []))


def _sum_devices(own, gathered, me1, name):
    r, c = own.shape
    tr = _pick_rows(r, c, 4, 1 << 17)

    def body(me_ref, own_ref, g_ref, o_ref):
        acc = None
        for k in range(N_DEV):
            v = jnp.where(me_ref[0] == k, own_ref[...], g_ref[k])
            acc = v if acc is None else acc + v
        o_ref[...] = acc

    return _prefetch_call(
        body, (r // tr,),
        [pl.BlockSpec((tr, c), lambda i, m: (i, 0)), pl.BlockSpec((N_DEV, tr, c), lambda i, m: (0, i, 0))],
        pl.BlockSpec((tr, c), lambda i, m: (i, 0)), jax.ShapeDtypeStruct((r, c), F32), name)(me1, own, gathered)


def _adamw(w, g, m, v, name):
    return _elementwise(_adamw_math, [w, g, m, v], [F32, F32, F32], name)


ANY = pl.BlockSpec(memory_space=pl.ANY)


def _mesh_pos():
    return lax.axis_index("x"), lax.axis_index("y"), lax.axis_index("c")


def _chip_peers(x, y, c):
    out = []
    for r in (1, 2, 3):
        px = 1 - x if r & 2 else x
        py = 1 - y if r & 1 else y
        out.append((2 * px + py, (px, py, c)))
    return out


def _full_shape(kind, shard_shape):
    l, r, c = shard_shape
    return (l, N_CHIPS, r, c) if kind == "row" else (l, r, N_CHIPS * c)


def _full_piece(ref, kind, k, h, hr, sc):
    rows = pl.ds(pl.multiple_of(h * hr, SUBLANES_BF16), hr)
    if kind == "row":
        return ref.at[:, k, rows, :]
    return ref.at[:, rows, pl.ds(pl.multiple_of(k * sc, LANES), sc)]


def _remote(src, dst, ssem, rsem, dev):
    return pltpu.make_async_remote_copy(src_ref=src, dst_ref=dst, send_sem=ssem, recv_sem=rsem,
                                        device_id=dev, device_id_type=MESH)


DMA_CHUNK_BYTES = 1 << 20
DMA_MAX_CHUNKS = 32


def _chunk_views(src, dst):
    axis = len(src.shape) - 2
    rows = src.shape[axis]
    nbytes = math.prod(src.shape) * jnp.dtype(src.dtype).itemsize
    n = max(1, min(DMA_MAX_CHUNKS, nbytes // DMA_CHUNK_BYTES))
    while n > 1 and (rows % n or (rows // n) % SUBLANES_BF16):
        n -= 1
    cr = rows // n
    out = []
    for i in range(n):
        idx = (slice(None),) * axis + (pl.ds(i * cr, cr), slice(None))
        out.append((src.at[idx], dst.at[idx]))
    return out


def _start_remote(src, dst, ssem, rsem, dev):
    for s, t in _chunk_views(src, dst):
        _remote(s, t, ssem, rsem, dev).start()
    return _remote(src, dst, ssem, rsem, dev)


def _allgather_steps(fulls, kinds):
    nw = len(fulls)

    def dims(a, kind):
        return (a.shape[2] // 2, a.shape[3]) if kind == "row" else (a.shape[1] // 2, a.shape[2] // N_CHIPS)

    hrs = [dims(a, k)[0] for a, k in zip(fulls, kinds)]
    scs = [dims(a, k)[1] for a, k in zip(fulls, kinds)]

    def piece(ref, w, k, h):
        return _full_piece(ref, kinds[w], k, h, hrs[w], scs[w])

    def copies1(src, dst, sems, start):
        x, y, c = _mesh_pos()
        k_me = 2 * x + y
        out = []
        for w in range(nw):
            for r, (kj, dev) in enumerate(_chip_peers(x, y, c)):
                args = (sems[0].at[3 * w + r], sems[1].at[3 * w + r], dev)
                if start:
                    out.append(_start_remote(piece(src[w], w, k_me, c), piece(dst[w], w, k_me, c), *args))
                else:
                    out.append(_remote(piece(src[w], w, k_me, c), piece(dst[w], w, kj, c), *args))
        return out

    def copies2(src, dst, sems, start):
        x, y, c = _mesh_pos()
        out = []
        for w in range(nw):
            for r, (kj, _) in enumerate(_chip_peers(x, y, c)):
                args = (sems[0].at[3 * w + r], sems[1].at[3 * w + r], (x, y, 1 - c))
                if start:
                    out.append(_start_remote(piece(src[w], w, kj, c), piece(dst[w], w, kj, c), *args))
                else:
                    out.append(_remote(piece(src[w], w, kj, 1 - c), piece(dst[w], w, kj, 1 - c), *args))
        return out

    def finish(copies):
        def fn(src, dst, sems):
            for cp in copies(src, dst, sems, False):
                cp.wait_recv()
            for cp in copies(src, dst, sems, False):
                cp.wait_send()
        return fn

    step1 = (lambda s, d, m: copies1(s, d, m, True), finish(copies1))
    step2 = (lambda s, d, m: copies2(s, d, m, True), finish(copies2))
    return step1, step2


def _exchange_hook(arrays, step, sem_len, out_shapes=None):
    shapes = out_shapes or [jax.ShapeDtypeStruct(a.shape, a.dtype) for a in arrays]
    return dict(arrays=list(arrays), out_shapes=shapes, in_place=out_shapes is None, sem_len=sem_len,
                first=step[0], last=step[1])


def _combine_hooks(h1, h2):
    assert h1["in_place"] and h2["in_place"]
    n1, s1, s2 = len(h1["arrays"]), h1["sem_len"], h2["sem_len"]

    def both(which):
        def fn(src, dst, sems):
            h1[which](src[:n1], dst[:n1], [s.at[pl.ds(0, s1)] for s in sems])
            h2[which](src[n1:], dst[n1:], [s.at[pl.ds(s1, s2)] for s in sems])
        return fn

    return dict(arrays=h1["arrays"] + h2["arrays"], out_shapes=h1["out_shapes"] + h2["out_shapes"],
                in_place=True, sem_len=s1 + s2, first=both("first"), last=both("last"))


def _exchange_call(hook, name):
    nh, nho = len(hook["arrays"]), len(hook["out_shapes"])

    def body(*refs):
        h_in, h_out, sems = refs[:nh], refs[nh:nh + nho], refs[nh + nho:]
        hook["first"](h_in, h_out, sems)
        hook["last"](h_in, h_out, sems)

    return pl.pallas_call(
        body, in_specs=[ANY] * nh, out_specs=[ANY] * nho, out_shape=hook["out_shapes"],
        scratch_shapes=[pltpu.SemaphoreType.DMA((hook["sem_len"],))] * 2,
        input_output_aliases={k: k for k in range(nh)} if hook["in_place"] else {}, name=name,
    )(*hook["arrays"])


def _allgather_weights(fulls, kinds):
    nw = len(fulls)
    step1, step2 = _allgather_steps(fulls, kinds)

    def body(*refs):
        mine, fu = refs[:nw], refs[nw:2 * nw]
        sems1, sems2 = refs[2 * nw:2 * nw + 2], refs[2 * nw + 2:]
        step1[0](mine, fu, sems1)
        step1[1](mine, fu, sems1)
        step2[0](fu, fu, sems2)
        step2[1](fu, fu, sems2)

    return pl.pallas_call(
        body,
        in_specs=[ANY] * nw, out_specs=[ANY] * nw,
        out_shape=[jax.ShapeDtypeStruct(a.shape, a.dtype) for a in fulls],
        scratch_shapes=[pltpu.SemaphoreType.DMA((3 * nw,))] * 4,
        input_output_aliases={w: w for w in range(nw)},
        name="allgather_weights",
    )(*fulls)


def _hosted(body, n_in, n_out, hook, n_steps):
    if hook is None:
        return body
    nh, nho = len(hook["arrays"]), len(hook["out_shapes"])

    def wrapped(*refs):
        ins, h_in = refs[:n_in], refs[n_in:n_in + nh]
        outs = refs[n_in + nh:n_in + nh + n_out]
        h_out = refs[n_in + nh + n_out:n_in + nh + n_out + nho]
        rest = refs[n_in + nh + n_out + nho:]
        scr, sems = rest[:-2], rest[-2:]
        i = pl.program_id(0)

        @pl.when(i == 0)
        def _():
            hook["first"](h_in, h_out, sems)

        body(*ins, *outs, *scr)

        @pl.when(i == n_steps - 1)
        def _():
            hook["last"](h_in, h_out, sems)

    return wrapped


def _hosted_call(body, hook, n_steps, in_specs, out_specs, out_shape, scratch_shapes, args, name, aliases=None):
    n_in, n_out = len(in_specs), len(out_specs)
    aliases = dict(aliases or {})
    if hook is not None:
        nh = len(hook["arrays"])
        in_specs = list(in_specs) + [ANY] * nh
        out_specs = list(out_specs) + [ANY] * len(hook["out_shapes"])
        out_shape = list(out_shape) + list(hook["out_shapes"])
        scratch_shapes = list(scratch_shapes) + [pltpu.SemaphoreType.DMA((hook["sem_len"],))] * 2
        if hook["in_place"]:
            aliases.update({n_in + k: n_out + k for k in range(nh)})
        args = list(args) + hook["arrays"]
    outs = pl.pallas_call(
        _hosted(body, n_in, n_out, hook, n_steps), grid=(n_steps,),
        in_specs=in_specs, out_specs=out_specs, out_shape=out_shape, scratch_shapes=scratch_shapes,
        input_output_aliases=aliases, compiler_params=_cparams(), name=name,
    )(*args)
    return outs[:n_out], outs[n_out:]


def _pair_exchange(copies):
    def finish(src, dst, sems):
        for cp in copies(src, dst, sems, False):
            cp.wait_recv()
        for cp in copies(src, dst, sems, False):
            cp.wait_send()
    return (lambda s, d, m: copies(s, d, m, True), finish)


def _rs_pair(fulls, kinds):
    nw = len(fulls)

    def half_all(ref, kind, h):
        if kind == "row":
            hr = ref.shape[2] // 2
            return ref.at[:, :, pl.ds(pl.multiple_of(h * hr, SUBLANES_BF16), hr), :]
        hr = ref.shape[1] // 2
        return ref.at[:, pl.ds(pl.multiple_of(h * hr, SUBLANES_BF16), hr), :]

    def half_shape(kind, shape):
        if kind == "row":
            return (shape[0], shape[1], shape[2] // 2, shape[3])
        return (shape[0], shape[1] // 2, shape[2])

    def copies(g, got, sems, start):
        x, y, c = _mesh_pos()
        make = _start_remote if start else _remote
        return [make(half_all(g[w], kinds[w], 1 - c), got[w], sems[0].at[w], sems[1].at[w], (x, y, 1 - c))
                for w in range(nw)]

    shapes = [jax.ShapeDtypeStruct(half_shape(k, a.shape), a.dtype) for k, a in zip(kinds, fulls)]
    return _exchange_hook(fulls, _pair_exchange(copies), nw, shapes)


def _rs_chips(parts, kinds):
    nw = len(parts)

    def slot(ref, kind, k):
        if kind == "row":
            return ref.at[:, k]
        sc = ref.shape[2] // N_CHIPS
        return ref.at[:, :, pl.ds(pl.multiple_of(k * sc, LANES), sc)]

    def slot_shape(kind, shape):
        if kind == "row":
            return (shape[0], shape[2], shape[3])
        return (shape[0], shape[1], shape[2] // N_CHIPS)

    def copies(t, rb, sems, start):
        x, y, c = _mesh_pos()
        make = _start_remote if start else _remote
        return [make(slot(t[w], kinds[w], kj), rb[w].at[r], sems[0].at[3 * w + r], sems[1].at[3 * w + r], dev)
                for w in range(nw) for r, (kj, dev) in enumerate(_chip_peers(x, y, c))]

    shapes = [jax.ShapeDtypeStruct((N_CHIPS - 1,) + slot_shape(k, a.shape), a.dtype) for k, a in zip(kinds, parts)]
    return _exchange_hook(parts, _pair_exchange(copies), 3 * nw, shapes)


def _rs_join(halves):
    nw = len(halves)

    def copies(src, dst, sems, start):
        x, y, c = _mesh_pos()
        make = _start_remote if start else _remote
        return [make(src[w], dst[w], sems[0].at[w], sems[1].at[w], (x, y, 1 - c)) for w in range(nw)]

    return _exchange_hook(halves, _pair_exchange(copies), nw,
                          [jax.ShapeDtypeStruct(a.shape, a.dtype) for a in halves])


def _allgather_small(buf, name):
    def body(in_ref, out_ref, ssem, rsem):
        x, y, c = _mesh_pos()
        me = 4 * x + 2 * y + c
        cps, waits = [], []
        for r in range(1, N_DEV):
            px = 1 - x if r & 4 else x
            py = 1 - y if r & 2 else y
            pc = 1 - c if r & 1 else c
            cp = _remote(in_ref, out_ref.at[me], ssem.at[r - 1], rsem.at[r - 1], (px, py, pc))
            cp.start()
            cps.append(cp)
            waits.append(_remote(in_ref, out_ref.at[4 * px + 2 * py + pc], ssem.at[r - 1], rsem.at[r - 1], (px, py, pc)))
        for wt in waits:
            wt.wait_recv()
        for cp in cps:
            cp.wait_send()

    return pl.pallas_call(
        body, in_specs=[ANY], out_specs=ANY,
        out_shape=jax.ShapeDtypeStruct((N_DEV,) + buf.shape, buf.dtype),
        scratch_shapes=[pltpu.SemaphoreType.DMA((N_DEV - 1,))] * 2,
        name=name,
    )(buf)


def _pack(arrs):
    flat = jnp.concatenate([a.reshape(-1).astype(F32) for a in arrs])
    rows = -(-flat.shape[0] // (LANES * 16)) * 16
    return jnp.pad(flat, (0, rows * LANES - flat.shape[0])).reshape(rows, LANES)


def _unpack(buf, shapes):
    flat = buf.reshape(-1)
    out, off = [], 0
    for shp in shapes:
        nel = math.prod(shp)
        out.append(flat[off:off + nel].reshape(shp))
        off += nel
    return out


BIG = ("a_w_in", "a_w_out", "b_w_in", "b_w_out", "c_w_in", "c_w_grp", "c_w_out", "f_w_up", "f_w_down")
BIG_KIND = {"a_w_in": "col", "a_w_out": "row", "b_w_in": "col", "b_w_out": "row", "c_w_in": "row",
            "c_w_grp": "row", "c_w_out": "row", "f_w_up": "col", "f_w_down": "row"}
FIRST_LAYER = ("a_w_in", "a_w_out", "f_w_up", "f_w_down")
SHARDED_SMALL =("a_dw", "a_dw_b", "a_ln_g", "a_ln_b", "c_scale", "f_dw")
REPLICATED = ("b_ln_g", "b_ln_b", "b_ws", "b_bs", "ln1_g", "ln1_b", "ln2_g", "ln2_b")
WEIGHTS = ("a_w_in", "a_dw", "a_dw_b", "a_ln_g", "a_ln_b", "a_w_out", "b_w_in", "b_ln_g", "b_ln_b", "b_ws", "b_bs",
           "b_w_out", "c_w_in", "c_w_grp", "c_scale", "c_w_out", "f_w_up", "f_dw", "f_w_down",
           "ln1_g", "ln1_b", "ln2_g", "ln2_b")


def _as3d(a):
    return a.reshape((-1,) + a.shape[-2:])


def kernel(x, a_w_in, a_dw, a_dw_b, a_ln_g, a_ln_b, a_w_out, b_w_in, b_ln_g, b_ln_b, b_ws, b_bs, b_w_out, c_w_in, c_w_grp, c_scale, c_w_out, f_w_up, f_dw, f_w_down, ln1_g, ln1_b, ln2_g, ln2_b, loss_target, m_a_w_in, m_a_dw, m_a_dw_b, m_a_ln_g, m_a_ln_b, m_a_w_out, m_b_w_in, m_b_ln_g, m_b_ln_b, m_b_ws, m_b_bs, m_b_w_out, m_c_w_in, m_c_w_grp, m_c_scale, m_c_w_out, m_f_w_up, m_f_dw, m_f_w_down, m_ln1_g, m_ln1_b, m_ln2_g, m_ln2_b, v_a_w_in, v_a_dw, v_a_dw_b, v_a_ln_g, v_a_ln_b, v_a_w_out, v_b_w_in, v_b_ln_g, v_b_ln_b, v_b_ws, v_b_bs, v_b_w_out, v_c_w_in, v_c_w_grp, v_c_scale, v_c_w_out, v_f_w_up, v_f_dw, v_f_w_down, v_ln1_g, v_ln1_b, v_ln2_g, v_ln2_b):
    w = dict(a_w_in=a_w_in, a_dw=a_dw, a_dw_b=a_dw_b, a_ln_g=a_ln_g, a_ln_b=a_ln_b, a_w_out=a_w_out, b_w_in=b_w_in, b_ln_g=b_ln_g, b_ln_b=b_ln_b, b_ws=b_ws, b_bs=b_bs, b_w_out=b_w_out, c_w_in=c_w_in, c_w_grp=c_w_grp, c_scale=c_scale, c_w_out=c_w_out, f_w_up=f_w_up, f_dw=f_dw, f_w_down=f_w_down, ln1_g=ln1_g, ln1_b=ln1_b, ln2_g=ln2_g, ln2_b=ln2_b)
    mom = dict(a_w_in=m_a_w_in, a_dw=m_a_dw, a_dw_b=m_a_dw_b, a_ln_g=m_a_ln_g, a_ln_b=m_a_ln_b, a_w_out=m_a_w_out, b_w_in=m_b_w_in, b_ln_g=m_b_ln_g, b_ln_b=m_b_ln_b, b_ws=m_b_ws, b_bs=m_b_bs, b_w_out=m_b_w_out, c_w_in=m_c_w_in, c_w_grp=m_c_w_grp, c_scale=m_c_scale, c_w_out=m_c_w_out, f_w_up=m_f_w_up, f_dw=m_f_dw, f_w_down=m_f_w_down, ln1_g=m_ln1_g, ln1_b=m_ln1_b, ln2_g=m_ln2_g, ln2_b=m_ln2_b)
    var = dict(a_w_in=v_a_w_in, a_dw=v_a_dw, a_dw_b=v_a_dw_b, a_ln_g=v_a_ln_g, a_ln_b=v_a_ln_b, a_w_out=v_a_w_out, b_w_in=v_b_w_in, b_ln_g=v_b_ln_g, b_ln_b=v_b_ln_b, b_ws=v_b_ws, b_bs=v_b_bs, b_w_out=v_b_w_out, c_w_in=v_c_w_in, c_w_grp=v_c_w_grp, c_scale=v_c_scale, c_w_out=v_c_w_out, f_w_up=v_f_w_up, f_dw=v_f_dw, f_w_down=v_f_w_down, ln1_g=v_ln1_g, ln1_b=v_ln1_b, ln2_g=v_ln2_g, ln2_b=v_ln2_b)

    depth = ln1_g.shape[0]
    d = x.shape[-1]
    alpha = float((2 * depth) ** 0.25)
    chip = 2 * lax.axis_index("x") + lax.axis_index("y")
    chip1 = chip.astype(jnp.int32).reshape(1)
    core1 = lax.axis_index("c").astype(jnp.int32).reshape(1)

    assert depth == 4
    early = [(k, 0, 1) for k in FIRST_LAYER]
    late = [(k, 1, w[k].shape[0] - 1) for k in FIRST_LAYER] + [(k, 0, _as3d(w[k]).shape[0]) for k in BIG if k not in FIRST_LAYER]
    groups = {
        "up_front": [("a_w_in", 0, 1), ("a_w_out", 0, 1)],
        "soon": [("f_w_up", 0, 1), ("f_w_down", 0, 1)],
        "mid": [("f_w_up", 1, 2), ("f_w_down", 1, 2)] + [(k, 0, _as3d(w[k]).shape[0]) for k in BIG if k not in FIRST_LAYER],
        "last": [("a_w_in", 1, 1), ("a_w_out", 1, 1), ("f_w_up", 3, 1), ("f_w_down", 3, 1)],
    }
    where = {(k, l0 + t): (name, t) for name, group in groups.items() for k, l0, nl in group for t in range(nl)}

    def group_kinds(group):
        return [BIG_KIND[k] for k, _, _ in group]

    def cast_group(name):
        return [_cast_into_full(_as3d(w[k]), l0, nl, BIG_KIND[k], chip1, f"cast_{name}_{k}") for k, l0, nl in groups[name]]

    def as_stacks(name, arrays):
        return {k: (a.reshape(a.shape[0], -1, a.shape[-1]) if BIG_KIND[k] == "row" else a)
                for (k, _, _), a in zip(groups[name], arrays)}

    def ag_hook(name, arrays, step):
        steps = _allgather_steps(arrays, group_kinds(groups[name]))
        return _exchange_hook(arrays, steps[step], 3 * len(arrays))

    full = {"up_front": as_stacks("up_front", _allgather_weights(cast_group("up_front"), group_kinds(groups["up_front"])))}

    def weight(k, l):
        name, idx = where[(k, l)]
        return full[name][k], idx

    small_all = _allgather_small(_pack([w[k] for k in SHARDED_SMALL]), "allgather_small_params")
    other_core = 1 - lax.axis_index("c")
    per_chip = [_unpack(lax.dynamic_index_in_dim(small_all, 2 * k + other_core, keepdims=False),
                        [w[n].shape for n in SHARDED_SMALL]) for k in range(N_CHIPS)]
    fs = {n: jnp.concatenate([per_chip[k][i] for k in range(N_CHIPS)], axis=-1) for i, n in enumerate(SHARDED_SMALL)}

    nh = b_ws.shape[1]
    tril = jnp.tril(jnp.ones((CHUNK, CHUNK), F32))
    wm = (b_ws[0] * tril).astype(BF16)
    wmt = jnp.swapaxes(wm, 1, 2)
    bs_exp = jnp.repeat(jnp.transpose(b_bs[0]), CHUNK, axis=1)

    xh, g, b = x[0], jnp.ones((1, d), F32), jnp.zeros((1, d), F32)
    saved = []
    for i in range(depth):
        kind, j = i % 3, i // 3
        rec = dict(xin=xh, gin=g, bin=b)
        if kind == 0:
            hook = None
            if i == 0:
                n_soon = len(groups["soon"])
                hook = _combine_hooks(ag_hook("soon", cast_group("soon"), 0), ag_hook("mid", cast_group("mid"), 0))
            (xh1, rstd1, p, chat, rstdc), landed = _conv_fwd(
                xh, g, b, weight("a_w_in", j)[0], weight("a_w_out", j)[0], weight("a_w_in", j)[1], fs["a_dw"], j,
                fs["a_dw_b"][j:j + 1], fs["a_ln_g"][j:j + 1], fs["a_ln_b"][j:j + 1], alpha, f"conv_fwd_{i}", hook)
            if i == 0:
                full["soon"] = as_stacks("soon", _exchange_call(ag_hook("soon", landed[:n_soon], 1), "allgather_soon_d2d"))
                mid_landed = landed[n_soon:]
            rec.update(p=p, chat=chat, rstdc=rstdc)
        elif kind == 1:
            xh1, rstd1, zp = _sgu_fwd(xh, g, b, full["mid"]["b_w_in"], b_ln_g, b_ln_b, wm, bs_exp,
                                      full["mid"]["b_w_out"], alpha, f"sgu_fwd_{i}")
            rec.update(zp=zp)
        else:
            xh1, rstd1, ys = _pool_fwd(xh, g, b, full["mid"]["c_w_in"], full["mid"]["c_w_grp"], fs["c_scale"],
                                       full["mid"]["c_w_out"], alpha, f"pool_fwd_{i}")
            rec.update(ys=ys)
        hook = None
        if i == 0:
            hook = _combine_hooks(ag_hook("mid", mid_landed, 1), ag_hook("last", cast_group("last"), 0))
        elif i == 1:
            hook = ag_hook("last", last_landed, 1)
        (xh2, rstd2, hs, hcs), passed_on = _ffn_fwd(
            xh1, ln1_g[i:i + 1], ln1_b[i:i + 1], weight("f_w_up", i)[0], weight("f_w_down", i)[0],
            weight("f_w_up", i)[1], fs["f_dw"], i, alpha, f"ffn_fwd_{i}", hook)
        if i == 0:
            full["mid"] = as_stacks("mid", passed_on[:len(mid_landed)])
            last_landed = passed_on[len(mid_landed):]
        elif i == 1:
            full["last"] = as_stacks("last", passed_on)
        rec.update(xh1=xh1, rstd1=rstd1, xh2=xh2, rstd2=rstd2, hs=hs, hcs=hcs)
        saved.append(rec)
        xh, g, b = xh2, ln2_g[i:i + 1], ln2_b[i:i + 1]

    dxo = loss_target[0]
    assert depth >= 3

    def stack_shape(k, nl):
        _, r, c = _as3d(w[k]).shape
        return (nl, N_CHIPS * r, c) if BIG_KIND[k] == "row" else (nl, r, N_CHIPS * c)

    g_first = {k: lax.empty(stack_shape(k, nl), F32) for k, _, nl in early}
    g_rest = {k: lax.empty(stack_shape(k, nl), F32) for k, _, nl in late if k != "c_w_grp"}

    def gslot(k, l):
        if k in FIRST_LAYER:
            return (g_first, 0) if l == 0 else (g_rest, l - 1)
        return g_rest, l

    def rs_views(group, store):
        out = []
        for k, _, _ in group:
            a = store[k]
            out.append(a.reshape(a.shape[0], N_CHIPS, -1, a.shape[-1]) if BIG_KIND[k] == "row" else a)
        return out

    def pair_sums(group, views, got, tag):
        return [_pair_sum(a, t, BIG_KIND[k], core1, f"rs_pair_sum_{tag}_{k}") for (k, _, _), a, t in zip(group, views, got)]

    def reduce_and_join(group, pair, from_chips, tag):
        half = [_chip_sum(t, rb, BIG_KIND[k], chip1, f"rs_chip_sum_{tag}_{k}")
                for (k, _, _), t, rb in zip(group, pair, from_chips)]
        return half, _exchange_call(_rs_join(half), f"rs_join_{tag}")

    early_kinds = [BIG_KIND[k] for k, _, _ in early]
    late_kinds = [BIG_KIND[k] for k, _, _ in late]
    gs = {k: [None] * w[k].shape[0] for k in ("a_dw", "a_dw_b", "a_ln_g", "a_ln_b", "f_dw", "ln1_g", "ln1_b", "ln2_g", "ln2_b")}
    for i in reversed(range(depth)):
        kind, j = i % 3, i // 3
        rec = saved[i]
        hook = None
        if i == 0:
            late_views = rs_views(late, g_rest)
            hook = _rs_pair(late_views, late_kinds)
        st, idx = gslot("f_w_down", i)
        (dr2, dhc, st["f_w_down"], gs["ln2_g"][i], gs["ln2_b"][i], loss_term), got = _ffn_bwd1(
            dxo, rec["xh2"], rec["rstd2"], ln2_g[i:i + 1], ln2_b[i:i + 1], rec["hcs"], *weight("f_w_down", i),
            st["f_w_down"], idx, f"ffn_bwd1_{i}", hook, loss_head=(i == depth - 1))
        if i == depth - 1:
            loss = lax.psum(loss_term[0, 0], ("x", "y", "c"))
        if i == 0:
            late_pair = pair_sums(late, late_views, got, "rest")
        dr1, dh, gs["f_dw"][i], gs["ln1_g"][i], gs["ln1_b"][i] = _ffn_bwd2(
            dhc, rec["hs"], dr2, *weight("f_w_up", i), fs["f_dw"], i, rec["xh1"], rec["rstd1"], ln1_g[i:i + 1],
            alpha, f"ffn_bwd2_{i}")
        st, idx = gslot("f_w_up", i)
        st["f_w_up"] = _mm_tn(rec["xh1"], ln1_g[i:i + 1], ln1_b[i:i + 1], dh, st["f_w_up"], idx, f"grad_w_up_{i}")
        if kind == 0:
            hook = _rs_chips(late_pair, late_kinds) if i == 0 else None
            st, idx = gslot("a_w_out", j)
            (dp, st["a_w_out"], gs["a_dw"][j], gs["a_dw_b"][j], gs["a_ln_g"][j], gs["a_ln_b"][j]), landed = _conv_bwd1(
                dr1, rec["chat"], rec["rstdc"], rec["p"], *weight("a_w_out", j), fs["a_dw"], j,
                fs["a_ln_g"][j:j + 1], fs["a_ln_b"][j:j + 1], st["a_w_out"], idx, f"conv_bwd1_{i}", hook)
            if i == 0:
                late_from_chips = landed
            win_name, lidx = "a_w_in", j
        elif kind == 1:
            dp, g_rest["b_w_out"], g_ws, g_bs_t, g_blg, g_blb = _sgu_bwd1(
                dr1, rec["zp"], full["mid"]["b_w_out"], b_ln_g, b_ln_b, wm, wmt, bs_exp, g_rest["b_w_out"],
                f"sgu_bwd1_{i}")
            win_name, lidx = "b_w_in", 0
        else:
            dp, g_rest["c_w_out"], g_rest["c_w_grp"], g_cscale = _pool_bwd1(
                dr1, rec["ys"], full["mid"]["c_w_out"], full["mid"]["c_w_grp"], fs["c_scale"], g_rest["c_w_out"],
                f"pool_bwd1_{i}")
            win_name, lidx = "c_w_in", 0
        dxo = _bwd_in(dp, dr1, *weight(win_name, lidx), alpha, f"mixer_bwd2_{i}")
        st, idx = gslot(win_name, lidx)
        st[win_name] = _mm_tn(rec["xin"], rec["gin"], rec["bin"], dp, st[win_name], idx, f"grad_w_in_{i}")
    grad_x = dxo[None]

    late_half, late_other = reduce_and_join(late, late_pair, late_from_chips, "rest")
    early_views = rs_views(early, g_first)
    early_got = _exchange_call(_rs_pair(early_views, early_kinds), "rs_pair_first")
    early_pair = pair_sums(early, early_views, early_got, "first")
    early_from_chips = _exchange_call(_rs_chips(early_pair, early_kinds), "rs_chips_first")
    early_half, early_other = reduce_and_join(early, early_pair, early_from_chips, "first")

    updates = {}
    for (k, l0, _), own, oth in zip(late, late_half, late_other):
        updates[k] = _adamw_big(_as3d(w[k]), _as3d(mom[k]), _as3d(var[k]), own, oth, core1, f"adamw_rest_{k}", l0)
    for (k, l0, _), own, oth in zip(early, early_half, early_other):
        updates[k] = _adamw_big(_as3d(w[k]), _as3d(mom[k]), _as3d(var[k]), own, oth, core1, f"adamw_first_{k}",
                                l0, into=updates[k])
    grads, delta, new_m, new_v = {}, {}, {}, {}
    for k in BIG:
        grads[k], delta[k], new_m[k], new_v[k] = [o.reshape(w[k].shape) for o in updates[k]]

    small_full = {
        "a_dw": jnp.stack(gs["a_dw"]), "a_dw_b": jnp.concatenate(gs["a_dw_b"]), "a_ln_g": jnp.concatenate(gs["a_ln_g"]),
        "a_ln_b": jnp.concatenate(gs["a_ln_b"]), "c_scale": g_cscale, "f_dw": jnp.stack(gs["f_dw"]),
        "b_ln_g": g_blg, "b_ln_b": g_blb, "b_ws": g_ws[None], "b_bs": jnp.transpose(g_bs_t)[None],
        "ln1_g": jnp.concatenate(gs["ln1_g"]), "ln1_b": jnp.concatenate(gs["ln1_b"]),
        "ln2_g": jnp.concatenate(gs["ln2_g"]), "ln2_b": jnp.concatenate(gs["ln2_b"]),
    }
    small_names = SHARDED_SMALL + REPLICATED
    small_shapes = [small_full[n].shape for n in small_names]
    small_packed = _pack([small_full[n] for n in small_names])
    gathered_small = _allgather_small(small_packed, "allgather_small_grads")
    me1 = (2 * chip + lax.axis_index("c")).astype(jnp.int32).reshape(1)
    summed = _unpack(_sum_devices(small_packed, gathered_small, me1, "small_grad_sum"), small_shapes)
    for n, a in zip(small_names, summed):
        if n in SHARDED_SMALL:
            cs = w[n].shape[-1]
            a = lax.dynamic_slice_in_dim(a, chip * cs, cs, axis=a.ndim - 1)
        grads[n] = a

    shapes = [w[n].shape for n in small_names]
    ds_, ms_, vs_ = _adamw(_pack([w[n] for n in small_names]), _pack([grads[n] for n in small_names]),
                           _pack([mom[n] for n in small_names]), _pack([var[n] for n in small_names]), "adamw_small")
    for n, a, bb, cc in zip(small_names, _unpack(ds_, shapes), _unpack(ms_, shapes), _unpack(vs_, shapes)):
        delta[n], new_m[n], new_v[n] = a, bb, cc

    return (loss, grad_x, *[grads[n] for n in WEIGHTS], *[delta[n] for n in WEIGHTS],
            *[new_m[n] for n in WEIGHTS], *[new_v[n] for n in WEIGHTS])
```

```python
import math

import jax
import jax.numpy as jnp
from jax import lax
from jax.experimental import pallas as pl
from jax.experimental.pallas import tpu as pltpu

F32 = jnp.float32
BF16 = jnp.bfloat16

LN_EPS = 1e-5
POOL_WINDOWS = (2, 4, 8, 16)
CHUNK = 128
ADAM_LR = 0.001
ADAM_B1 = 0.9
ADAM_B2 = 0.999
ADAM_EPS = 1e-08
ADAM_WD = 0.01
ADAM_STEP = 10

LANES = 128
SUBLANES_BF16 = 16
N_CHIPS = 4
N_DEV = 8
VMEM_LIMIT = 60 * 1024 * 1024

TM_FFN = 512
TM_FFN_BWD1 = 256
TM_CONV = 256
TM_SGU = 512
TM_POOL = 512
TM_BWD_IN = 512
TS_MM_TN = 1024
CW_FFN = 256
CW_FFN_BWD2 = 512
CONV_HALO = 32
CONV_ROW_BLOCK = 64
POOL_HALO = 16
FFN_HALO = 16

MESH = pl.DeviceIdType.MESH


def _cparams(n_grid=1, parallel=False):
    sem = ("parallel" if parallel else "arbitrary",) * n_grid
    return pltpu.CompilerParams(dimension_semantics=sem, vmem_limit_bytes=VMEM_LIMIT)


def _resident(block, imap):
    return pl.BlockSpec(block, imap, pipeline_mode=pl.Buffered(1))


def _wspec(w, l):
    _, r, c = w.shape
    return _resident((None, r, c), lambda *_: (l, 0, 0))


def _rowspec(d):
    return pl.BlockSpec((1, d), lambda *_: (0, 0))


def _dot(a, b):
    return jnp.dot(a, b, preferred_element_type=F32)


def _dot_nt(a, b):
    return lax.dot_general(a, b, (((1,), (1,)), ((), ())), preferred_element_type=F32)


def _dot_tn(a, b):
    return lax.dot_general(a, b, (((0,), (0,)), ((), ())), preferred_element_type=F32)


def _sigmoid(x):
    return jax.nn.sigmoid(x)


def _ln_stats(r):
    mu = jnp.mean(r, axis=1, keepdims=True)
    xc = r - mu
    var = jnp.mean(xc * xc, axis=1, keepdims=True)
    rstd = lax.rsqrt(var + LN_EPS)
    return xc * rstd, rstd


def _ln_bwd(dy, xhat, rstd, g):
    dxh = dy * g
    m1 = jnp.mean(dxh, axis=1, keepdims=True)
    m2 = jnp.mean(dxh * xhat, axis=1, keepdims=True)
    return rstd * (dxh - m1 - xhat * m2)


def _colsum(v):
    return jnp.sum(v, axis=0, keepdims=True)


def _gelu(z):
    return 0.5 * z * (1.0 + lax.erf(z * (1.0 / math.sqrt(2.0))))


def _gelu_grad(z):
    cdf = 0.5 * (1.0 + lax.erf(z * (1.0 / math.sqrt(2.0))))
    pdf = jnp.exp(-0.5 * z * z) * (1.0 / math.sqrt(2.0 * math.pi))
    return cdf + z * pdf


def _shift_down(v, k, prev_rows):
    rolled = pltpu.roll(v, k, 0)
    head = rolled[0:8]
    rows = lax.broadcasted_iota(jnp.int32, head.shape, 0)
    for r in range(k):
        head = jnp.where(rows == r, prev_rows[k - 1 - r], head)
    return jnp.concatenate([head, rolled[8:]], axis=0)


def _shift_up(v, k, next_rows):
    tm = v.shape[0]
    rolled = pltpu.roll(v, tm - k, 0)
    tail = rolled[tm - 8:tm]
    rows = lax.broadcasted_iota(jnp.int32, tail.shape, 0)
    for r in range(k):
        tail = jnp.where(rows == 8 - k + r, next_rows[r], tail)
    return jnp.concatenate([rolled[0:tm - 8], tail], axis=0)


def _fill_shifted(base_scr, sh_scr):
    nrows = sh_scr.shape[1]
    for r in range(1, 8):
        sh_scr[r - 1, :, :] = base_scr[pl.ds(r, nrows), :]


def _tap(base_scr, sh_scr, off, r0, nrows, cols):
    q, r = divmod(off, 8)
    if r == 0:
        return base_scr[pl.ds(r0 + 8 * q, nrows), cols]
    return sh_scr[r - 1, pl.ds(r0 + 8 * q, nrows), cols]


def _pick_rows(r, c, itemsize, cap_bytes):
    best = None
    for t in range(16, r + 1, 16):
        if r % t == 0 and t * c * itemsize <= cap_bytes:
            best = t
    return best if best is not None else r


def _ffn_conv_cols(h, dw_ref, c0, cw, prev1, prev2):
    kw = dw_ref.shape[0]
    h1 = _shift_down(h, 1, [prev1])
    h2 = _shift_down(h, 2, [prev1, prev2])
    hc = dw_ref[kw - 1:kw, c0:c0 + cw] * h + dw_ref[kw - 2:kw - 1, c0:c0 + cw] * h1 + dw_ref[kw - 3:kw - 2, c0:c0 + cw] * h2
    return hc, h1, h2


def _ffn_fwd(xh1, g1, b1, wup, wdn, lw, fdw, l, alpha, name, hook=None):
    s, d = xh1.shape
    f2 = wup.shape[2]
    f = f2 // 2
    tm = min(TM_FFN, s)
    cw = min(CW_FFN, f)
    n, nck = s // tm, f // cw
    assert fdw.shape[1] == 3 and s % tm == 0 and f % cw == 0

    def body(xh_ref, g_ref, b_ref, wup_ref, dw_ref, wdn_ref, xo_ref, rs_ref, hs_ref, hcs_ref, carry):
        @pl.when(pl.program_id(0) == 0)
        def _():
            carry[...] = jnp.zeros_like(carry)

        x1 = xh_ref[...] * g_ref[...] + b_ref[...]
        xb = x1.astype(BF16)
        o = jnp.zeros((tm, d), F32)

        def up_proj(j):
            return [_dot(xb, wup_ref[:, half * f + j * cw:half * f + (j + 1) * cw]) for half in range(2)]

        ahead = up_proj(0)
        for j in range(nck):
            hh = ahead
            if j + 1 < nck:
                ahead = up_proj(j + 1)
            parts = []
            for half in range(2):
                c0 = half * f + j * cw
                h = hh[half]
                hs_ref[:, c0:c0 + cw] = h.astype(BF16)
                hc, _, _ = _ffn_conv_cols(h, dw_ref, c0, cw, carry[7:8, c0:c0 + cw], carry[6:7, c0:c0 + cw])
                carry[:, c0:c0 + cw] = h[tm - 8:tm, :]
                hcs_ref[:, c0:c0 + cw] = hc.astype(BF16)
                parts.append(hc)
            gg, vv = parts
            a = (gg * _sigmoid(gg) * vv).astype(BF16)
            o = o + _dot(a, wdn_ref[j * cw:(j + 1) * cw, :])
        xhat, rstd = _ln_stats(alpha * x1 + o)
        xo_ref[...] = xhat
        rs_ref[...] = rstd

    tile = pl.BlockSpec((tm, d), lambda i: (i, 0))
    return _hosted_call(
        body, hook, n,
        in_specs=[tile, _rowspec(d), _rowspec(d), _wspec(wup, lw),
                  pl.BlockSpec((None, 3, f2), lambda i: (l, 0, 0)), _wspec(wdn, lw)],
        out_specs=[tile, pl.BlockSpec((tm, 1), lambda i: (i, 0)), pl.BlockSpec((tm, f2), lambda i: (i, 0)),
                   pl.BlockSpec((tm, f2), lambda i: (i, 0))],
        out_shape=[jax.ShapeDtypeStruct((s, d), F32), jax.ShapeDtypeStruct((s, 1), F32),
                   jax.ShapeDtypeStruct((s, f2), BF16), jax.ShapeDtypeStruct((s, f2), BF16)],
        scratch_shapes=[pltpu.VMEM((8, f2), F32)],
        args=(xh1, g1, b1, wup, fdw, wdn), name=name)


def _ffn_bwd1(dx2, xh2, rstd2, g2, b2, hcs, wdn, lw, gwdn_buf, l, name, hook=None, loss_head=False):
    s, d = dx2.shape
    f2 = hcs.shape[1]
    f = f2 // 2
    tm = min(TM_FFN_BWD1, s)
    cw = min(CW_FFN, f)
    n, nck = s // tm, f // cw

    def body(dx_ref, xh_ref, rs_ref, g_ref, b_ref, hcs_ref, wdn_ref, buf_ref,
             dr_ref, dhc_ref, gwdn_ref, gg_ref, gb_ref, loss_ref):
        @pl.when(pl.program_id(0) == 0)
        def _():
            gwdn_ref[...] = jnp.zeros_like(gwdn_ref)
            gg_ref[...] = jnp.zeros_like(gg_ref)
            gb_ref[...] = jnp.zeros_like(gb_ref)
            loss_ref[...] = jnp.zeros_like(loss_ref)

        xh = xh_ref[...]
        if loss_head:
            err = xh * g_ref[...] + b_ref[...] - dx_ref[...]
            dx = err * (1.0 / d)
            loss_ref[...] += (0.5 / d) * jnp.sum(_colsum(err * err), axis=1, keepdims=True)
        else:
            dx = dx_ref[...]
        gg_ref[...] += _colsum(dx * xh)
        gb_ref[...] += _colsum(dx)
        dr = _ln_bwd(dx, xh, rs_ref[...], g_ref[...])
        dr_ref[...] = dr
        dob = dr.astype(BF16)

        def d_act(j):
            return _dot_nt(dob, wdn_ref[j * cw:(j + 1) * cw, :])

        da_ahead = d_act(0)
        for j in range(nck):
            da = da_ahead
            if j + 1 < nck:
                da_ahead = d_act(j + 1)
            gt = hcs_ref[:, j * cw:(j + 1) * cw].astype(F32)
            vv = hcs_ref[:, f + j * cw:f + (j + 1) * cw].astype(F32)
            sg = _sigmoid(gt)
            sl = gt * sg
            a = (sl * vv).astype(BF16)
            gwdn_ref[j * cw:(j + 1) * cw, :] += _dot_tn(a, dob)
            dhc_ref[:, j * cw:(j + 1) * cw] = (da * vv * (sg * (1.0 + gt * (1.0 - sg)))).astype(BF16)
            dhc_ref[:, f + j * cw:f + (j + 1) * cw] = (da * sl).astype(BF16)

    tile = pl.BlockSpec((tm, d), lambda i: (i, 0))
    wide = pl.BlockSpec((tm, f2), lambda i: (i, 0))
    nl = gwdn_buf.shape[0]
    return _hosted_call(
        body, hook, n,
        in_specs=[tile, tile, pl.BlockSpec((tm, 1), lambda i: (i, 0)), _rowspec(d), _rowspec(d), wide,
                  _wspec(wdn, lw), pl.BlockSpec(memory_space=pl.ANY)],
        out_specs=[tile, wide, pl.BlockSpec((None, f, d), lambda i: (l, 0, 0)), _rowspec(d), _rowspec(d),
                   pl.BlockSpec((1, 1), lambda i: (0, 0))],
        out_shape=[jax.ShapeDtypeStruct((s, d), F32), jax.ShapeDtypeStruct((s, f2), BF16),
                   jax.ShapeDtypeStruct((nl, f, d), F32),
                   jax.ShapeDtypeStruct((1, d), F32), jax.ShapeDtypeStruct((1, d), F32),
                   jax.ShapeDtypeStruct((1, 1), F32)],
        scratch_shapes=[], args=(dx2, xh2, rstd2, g2, b2, hcs, wdn, gwdn_buf), name=name, aliases={7: 2})


def _bwd_in(dp, dres, w, l, alpha, name):
    s, d = dres.shape
    nn = dp.shape[1]
    tm = min(TM_BWD_IN, s)
    n = s // tm
    tile = pl.BlockSpec((tm, d), lambda i: (i, 0))

    def body(dp_ref, dres_ref, w_ref, o_ref):
        o_ref[...] = alpha * dres_ref[...] + _dot_nt(dp_ref[...], w_ref[...])

    return pl.pallas_call(
        body, grid=(n,),
        in_specs=[pl.BlockSpec((tm, nn), lambda i: (i, 0)), tile, _wspec(w, l)],
        out_specs=tile, out_shape=jax.ShapeDtypeStruct((s, d), F32),
        compiler_params=_cparams(parallel=True), name=name,
    )(dp, dres, w)


def _ffn_bwd2(dhc, hs, dres, wup, lw, fdw, l, xh, rstd, g, alpha, name):
    s, d = dres.shape
    f2 = dhc.shape[1]
    tm = min(TM_BWD_IN, s)
    n = s // tm
    hb = FFN_HALO
    cw = min(CW_FFN_BWD2, f2)
    nck = f2 // cw
    halo_blocks = tm // hb
    assert f2 % cw == 0 and fdw.shape[1] == 3

    def body(dhc_ref, halo_ref, hs_ref, dres_ref, w_ref, dw_ref, xh_ref, rs_ref, g_ref,
             o_ref, dh_ref, gdw_ref, gg_ref, gb_ref):
        i = pl.program_id(0)

        @pl.when(i == 0)
        def _():
            gdw_ref[...] = jnp.zeros_like(gdw_ref)
            gg_ref[...] = jnp.zeros_like(gg_ref)
            gb_ref[...] = jnp.zeros_like(gb_ref)

        has_next = i < n - 1
        dx = alpha * dres_ref[...]
        for j in range(nck):
            c0 = j * cw
            dc = dhc_ref[:, c0:c0 + cw].astype(F32)
            hal = jnp.where(has_next, halo_ref[:, c0:c0 + cw].astype(F32), 0.0)
            nxt = [hal[0:1], hal[1:2]]
            u1 = _shift_up(dc, 1, nxt[:1])
            u2 = _shift_up(dc, 2, nxt)
            h = hs_ref[:, c0:c0 + cw].astype(F32)
            gdw_ref[2:3, c0:c0 + cw] += _colsum(dc * h)
            gdw_ref[1:2, c0:c0 + cw] += _colsum(u1 * h)
            gdw_ref[0:1, c0:c0 + cw] += _colsum(u2 * h)
            dh = (dw_ref[2:3, c0:c0 + cw] * dc + dw_ref[1:2, c0:c0 + cw] * u1
                  + dw_ref[0:1, c0:c0 + cw] * u2).astype(BF16)
            dh_ref[:, c0:c0 + cw] = dh
            dx = dx + _dot_nt(dh, w_ref[:, c0:c0 + cw])
        xhv = xh_ref[...]
        gg_ref[...] += _colsum(dx * xhv)
        gb_ref[...] += _colsum(dx)
        o_ref[...] = _ln_bwd(dx, xhv, rs_ref[...], g_ref[...])

    tile = pl.BlockSpec((tm, d), lambda i: (i, 0))
    wide = pl.BlockSpec((tm, f2), lambda i: (i, 0))
    return pl.pallas_call(
        body, grid=(n,),
        in_specs=[wide, pl.BlockSpec((hb, f2), lambda i: (jnp.minimum((i + 1) * halo_blocks, s // hb - 1), 0)),
                  wide, tile, _wspec(wup, lw), pl.BlockSpec((None, 3, f2), lambda i: (l, 0, 0)), tile,
                  pl.BlockSpec((tm, 1), lambda i: (i, 0)), _rowspec(d)],
        out_specs=[tile, wide, pl.BlockSpec((3, f2), lambda i: (0, 0)), _rowspec(d), _rowspec(d)],
        out_shape=[jax.ShapeDtypeStruct((s, d), F32), jax.ShapeDtypeStruct((s, f2), BF16),
                   jax.ShapeDtypeStruct((3, f2), F32),
                   jax.ShapeDtypeStruct((1, d), F32), jax.ShapeDtypeStruct((1, d), F32)],
        compiler_params=_cparams(), name=name,
    )(dhc, dhc, hs, dres, wup, fdw, xh, rstd, g)


def _mm_tn(a, ga, ba, bm, buf, l, name):
    s, k = a.shape
    nn = bm.shape[1]
    ts = min(TS_MM_TN, s)
    tn = nn // N_CHIPS if nn > 1024 else nn
    nj, ns = nn // tn, s // ts

    def body(a_ref, g_ref, b_ref, bm_ref, buf_ref, o_ref):
        @pl.when(pl.program_id(1) == 0)
        def _():
            o_ref[...] = jnp.zeros_like(o_ref)

        ab = (a_ref[...] * g_ref[...] + b_ref[...]).astype(BF16)
        o_ref[...] += _dot_tn(ab, bm_ref[...])

    return pl.pallas_call(
        body, grid=(nj, ns),
        in_specs=[pl.BlockSpec((ts, k), lambda j, t: (t, 0)), _rowspec(k), _rowspec(k),
                  pl.BlockSpec((ts, tn), lambda j, t: (t, j)), pl.BlockSpec(memory_space=pl.ANY)],
        out_specs=pl.BlockSpec((None, k, tn), lambda j, t: (l, 0, j)),
        out_shape=jax.ShapeDtypeStruct(buf.shape, F32),
        input_output_aliases={4: 0},
        compiler_params=_cparams(2), name=name,
    )(a, ga, ba, bm, buf)


def _conv_fwd(xin, gin, bin_, win, wout, lw, adw, l, adwb, lng, lnb, alpha, name, hook=None):
    s, d = xin.shape
    kw = adw.shape[1]
    hb = CONV_HALO
    tm = min(TM_CONV, s)
    n = s // tm
    assert kw - 1 <= hb <= tm

    def body(x_ref, g_ref, b_ref, win_ref, dw_ref, dwb_ref, lng_ref, lnb_ref, wout_ref,
             xo_ref, rs_ref, p_ref, chat_ref, rsc_ref, u_scr, u8_scr):
        @pl.when(pl.program_id(0) == 0)
        def _():
            u_scr[0:hb, :] = jnp.zeros((hb, d), F32)

        x = x_ref[...] * g_ref[...] + b_ref[...]
        pm = _dot(x.astype(BF16), win_ref[...])
        p_ref[...] = pm.astype(BF16)
        u = pm[:, :d] * _sigmoid(pm[:, d:])
        u_scr[hb:hb + tm, :] = u
        _fill_shifted(u_scr, u8_scr)
        acc = dwb_ref[...] + dw_ref[kw - 1:kw, :] * u
        for k in range(kw - 1):
            acc = acc + dw_ref[k:k + 1, :] * _tap(u_scr, u8_scr, hb - (kw - 1) + k, 0, tm, slice(None))
        u_scr[0:hb, :] = u_scr[tm:tm + hb, :]
        chat, rstdc = _ln_stats(acc)
        chat_ref[...] = chat.astype(BF16)
        rsc_ref[...] = rstdc
        nv = chat * lng_ref[...] + lnb_ref[...]
        sv = (nv * _sigmoid(nv)).astype(BF16)
        xhat, rstd = _ln_stats(alpha * x + _dot(sv, wout_ref[...]))
        xo_ref[...] = xhat
        rs_ref[...] = rstd

    tile = pl.BlockSpec((tm, d), lambda i: (i, 0))
    col = pl.BlockSpec((tm, 1), lambda i: (i, 0))
    return _hosted_call(
        body, hook, n,
        in_specs=[tile, _rowspec(d), _rowspec(d), _wspec(win, lw),
                  pl.BlockSpec((None, kw, d), lambda i: (l, 0, 0)), _rowspec(d), _rowspec(d), _rowspec(d),
                  _wspec(wout, lw)],
        out_specs=[tile, col, pl.BlockSpec((tm, 2 * d), lambda i: (i, 0)), tile, col],
        out_shape=[jax.ShapeDtypeStruct((s, d), F32), jax.ShapeDtypeStruct((s, 1), F32),
                   jax.ShapeDtypeStruct((s, 2 * d), BF16), jax.ShapeDtypeStruct((s, d), BF16),
                   jax.ShapeDtypeStruct((s, 1), F32)],
        scratch_shapes=[pltpu.VMEM((tm + hb, d), F32), pltpu.VMEM((7, tm + hb - 8, d), F32)],
        args=(xin, gin, bin_, win, adw, adwb, lng, lnb, wout), name=name)


def _conv_bwd1(dr1, chat, rstdc, p, wout, lw, adw, lt, lng, lnb, gwout_buf, l, name, hook=None):
    s, d = dr1.shape
    kw = adw.shape[1]
    hb = CONV_HALO
    tm = min(TM_CONV, s)
    n = s // tm
    halo_blocks = tm // hb
    rbl = CONV_ROW_BLOCK

    def body(dr_ref, chat_ref, rsc_ref, p_ref, halo_ref, wout_ref, dw_ref, lng_ref, lnb_ref, buf_ref,
             dp_ref, gwout_ref, gdw_ref, gdwb_ref, glng_ref, glnb_ref, u_scr, dc_scr, u8_scr, dc8_scr):
        i = pl.program_id(0)
        t = n - 1 - i

        @pl.when(i == 0)
        def _():
            dc_scr[tm:tm + hb, :] = jnp.zeros((hb, d), F32)
            gwout_ref[...] = jnp.zeros_like(gwout_ref)
            gdw_ref[...] = jnp.zeros_like(gdw_ref)
            gdwb_ref[...] = jnp.zeros_like(gdwb_ref)
            glng_ref[...] = jnp.zeros_like(glng_ref)
            glnb_ref[...] = jnp.zeros_like(glnb_ref)

        dob = dr_ref[...].astype(BF16)
        chat = chat_ref[...].astype(F32)
        lng = lng_ref[...]
        nv = chat * lng + lnb_ref[...]
        sgn = _sigmoid(nv)
        gwout_ref[...] += _dot_tn((nv * sgn).astype(BF16), dob)
        dn = _dot_nt(dob, wout_ref[...]) * (sgn * (1.0 + nv * (1.0 - sgn)))
        glng_ref[...] += _colsum(dn * chat)
        glnb_ref[...] += _colsum(dn)
        dc = _ln_bwd(dn, chat, rsc_ref[...], lng)
        gdwb_ref[...] += _colsum(dc)

        pm = p_ref[...].astype(F32)
        a = pm[:, :d]
        sg = _sigmoid(pm[:, d:])
        ph = halo_ref[...].astype(F32)
        u_scr[0:hb, :] = jnp.where(t > 0, ph[:, :d] * _sigmoid(ph[:, d:]), 0.0)
        u_scr[hb:hb + tm, :] = a * sg
        dc_scr[0:tm, :] = dc
        _fill_shifted(u_scr, u8_scr)
        _fill_shifted(dc_scr, dc8_scr)
        du = dw_ref[kw - 1:kw, :] * dc
        for k in range(kw - 1):
            du = du + dw_ref[k:k + 1, :] * _tap(dc_scr, dc8_scr, kw - 1 - k, 0, tm, slice(None))
        for cb in range(d // LANES):
            cols = pl.ds(cb * LANES, LANES)

            def rows_step(rb, accs, cols=cols):
                r0 = pl.multiple_of(rb * rbl, rbl)
                dcb = dc_scr[pl.ds(r0, rbl), cols]
                out = []
                for k in range(kw):
                    prod = dcb * _tap(u_scr, u8_scr, hb - (kw - 1) + k, r0, rbl, cols)
                    part = prod[0:8]
                    for g8 in range(1, rbl // 8):
                        part = part + prod[8 * g8:8 * g8 + 8]
                    out.append(accs[k] + part)
                return tuple(out)

            accs = lax.fori_loop(0, tm // rbl, rows_step, tuple(jnp.zeros((8, LANES), F32) for _ in range(kw)))
            for k in range(kw):
                gdw_ref[k:k + 1, cols] += _colsum(accs[k])
        dc_scr[tm:tm + hb, :] = dc[0:hb, :]
        dp_ref[:, :d] = (du * sg).astype(BF16)
        dp_ref[:, d:] = (du * a * sg * (1.0 - sg)).astype(BF16)

    tile = pl.BlockSpec((tm, d), lambda i: (n - 1 - i, 0))
    col = pl.BlockSpec((tm, 1), lambda i: (n - 1 - i, 0))
    nl = gwout_buf.shape[0]
    return _hosted_call(
        body, hook, n,
        in_specs=[tile, tile, col, pl.BlockSpec((tm, 2 * d), lambda i: (n - 1 - i, 0)),
                  pl.BlockSpec((hb, 2 * d), lambda i: (jnp.maximum((n - 1 - i) * halo_blocks - 1, 0), 0)),
                  _wspec(wout, lw), pl.BlockSpec((None, kw, d), lambda i: (lt, 0, 0)), _rowspec(d), _rowspec(d),
                  pl.BlockSpec(memory_space=pl.ANY)],
        out_specs=[pl.BlockSpec((tm, 2 * d), lambda i: (n - 1 - i, 0)),
                   pl.BlockSpec((None, d, d), lambda i: (l, 0, 0)),
                   pl.BlockSpec((kw, d), lambda i: (0, 0)), _rowspec(d), _rowspec(d), _rowspec(d)],
        out_shape=[jax.ShapeDtypeStruct((s, 2 * d), BF16), jax.ShapeDtypeStruct((nl, d, d), F32),
                   jax.ShapeDtypeStruct((kw, d), F32), jax.ShapeDtypeStruct((1, d), F32),
                   jax.ShapeDtypeStruct((1, d), F32), jax.ShapeDtypeStruct((1, d), F32)],
        scratch_shapes=[pltpu.VMEM((tm + hb, d), F32), pltpu.VMEM((tm + hb, d), F32),
                        pltpu.VMEM((7, tm + hb - 8, d), F32), pltpu.VMEM((7, tm + hb - 8, d), F32)],
        args=(dr1, chat, rstdc, p, p, wout, adw, lng, lnb, gwout_buf), name=name, aliases={9: 1})


def _sgu_gate(vn, wm_ref, bs_ref, s_scr, tm, nh):
    for ch in range(tm // CHUNK):
        r0 = ch * CHUNK
        for h in range(nh):
            c0 = h * CHUNK
            s_scr[r0:r0 + CHUNK, c0:c0 + CHUNK] = (
                _dot(wm_ref[h], vn[r0:r0 + CHUNK, c0:c0 + CHUNK]) + bs_ref[:, c0:c0 + CHUNK])


def _sgu_fwd(xin, gin, bin_, win, lg, lb, wm, bs_exp, wout, alpha, name):
    s, d = xin.shape
    nh = wm.shape[0]
    tm = min(TM_SGU, s)
    n = s // tm
    assert tm % CHUNK == 0 and nh * CHUNK == d

    def body(x_ref, g_ref, b_ref, win_ref, lg_ref, lb_ref, wm_ref, bs_ref, wout_ref,
             xo_ref, rs_ref, zp_ref, s_scr):
        x = x_ref[...] * g_ref[...] + b_ref[...]
        zp = _dot(x.astype(BF16), win_ref[...])
        zp_ref[...] = zp.astype(BF16)
        z = _gelu(zp)
        vhat, _ = _ln_stats(z[:, d:])
        vn = (vhat * lg_ref[...] + lb_ref[...]).astype(BF16)
        _sgu_gate(vn, wm_ref, bs_ref, s_scr, tm, nh)
        q = (z[:, :d] * s_scr[...]).astype(BF16)
        xhat, rstd = _ln_stats(alpha * x + _dot(q, wout_ref[...]))
        xo_ref[...] = xhat
        rs_ref[...] = rstd

    tile = pl.BlockSpec((tm, d), lambda i: (i, 0))
    return pl.pallas_call(
        body, grid=(n,),
        in_specs=[tile, _rowspec(d), _rowspec(d), _wspec(win, 0), _rowspec(d), _rowspec(d),
                  _resident((nh, CHUNK, CHUNK), lambda i: (0, 0, 0)),
                  _resident((CHUNK, d), lambda i: (0, 0)), _wspec(wout, 0)],
        out_specs=[tile, pl.BlockSpec((tm, 1), lambda i: (i, 0)), pl.BlockSpec((tm, 2 * d), lambda i: (i, 0))],
        out_shape=[jax.ShapeDtypeStruct((s, d), F32), jax.ShapeDtypeStruct((s, 1), F32),
                   jax.ShapeDtypeStruct((s, 2 * d), BF16)],
        scratch_shapes=[pltpu.VMEM((tm, d), F32)],
        compiler_params=_cparams(parallel=True), name=name,
    )(xin, gin, bin_, win, lg, lb, wm, bs_exp, wout)


def _sgu_bwd1(dr1, zp, wout, lg, lb, wm, wmt, bs_exp, gwout_buf, name):
    s, d = dr1.shape
    nh = wm.shape[0]
    tm = min(TM_SGU, s)
    n = s // tm

    def body(dr_ref, zp_ref, wout_ref, lg_ref, lb_ref, wm_ref, wmt_ref, bs_ref, buf_ref,
             dzp_ref, gwout_ref, gws_ref, gbs_ref, glg_ref, glb_ref, s_scr, dvn_scr, bs_acc):
        i = pl.program_id(0)

        @pl.when(i == 0)
        def _():
            gwout_ref[...] = jnp.zeros_like(gwout_ref)
            gws_ref[...] = jnp.zeros_like(gws_ref)
            glg_ref[...] = jnp.zeros_like(glg_ref)
            glb_ref[...] = jnp.zeros_like(glb_ref)
            bs_acc[...] = jnp.zeros_like(bs_acc)

        dob = dr_ref[...].astype(BF16)
        zp = zp_ref[...].astype(F32)
        z = _gelu(zp)
        u = z[:, :d]
        lg = lg_ref[...]
        vhat, rstdv = _ln_stats(z[:, d:])
        vn = (vhat * lg + lb_ref[...]).astype(BF16)
        _sgu_gate(vn, wm_ref, bs_ref, s_scr, tm, nh)
        sv = s_scr[...]
        gwout_ref[...] += _dot_tn((u * sv).astype(BF16), dob)
        dq = _dot_nt(dob, wout_ref[...])
        ds = dq * u
        dsb = ds.astype(BF16)
        part = jnp.zeros((CHUNK, d), F32)
        for ch in range(tm // CHUNK):
            r0 = ch * CHUNK
            part = part + ds[r0:r0 + CHUNK, :]
            for h in range(nh):
                c0 = h * CHUNK
                blk = dsb[r0:r0 + CHUNK, c0:c0 + CHUNK]
                gws_ref[h] += _dot_nt(blk, vn[r0:r0 + CHUNK, c0:c0 + CHUNK])
                dvn_scr[r0:r0 + CHUNK, c0:c0 + CHUNK] = _dot(wmt_ref[h], blk)
        bs_acc[...] += part
        dvn = dvn_scr[...]
        glg_ref[...] += _colsum(dvn * vhat)
        glb_ref[...] += _colsum(dvn)
        dv = _ln_bwd(dvn, vhat, rstdv, lg)
        gp = _gelu_grad(zp)
        dzp_ref[:, :d] = (dq * sv * gp[:, :d]).astype(BF16)
        dzp_ref[:, d:] = (dv * gp[:, d:]).astype(BF16)

        @pl.when(i == n - 1)
        def _():
            rows = lax.broadcasted_iota(jnp.int32, (CHUNK, CHUNK), 0)
            cols = lax.broadcasted_iota(jnp.int32, (CHUNK, CHUNK), 1)
            tril = (cols <= rows).astype(F32)
            acc = bs_acc[...]
            for h in range(nh):
                gws_ref[h] = gws_ref[h] * tril
                gbs_ref[:, h:h + 1] = jnp.sum(acc[:, h * CHUNK:(h + 1) * CHUNK], axis=1, keepdims=True)

    tile = pl.BlockSpec((tm, d), lambda i: (i, 0))
    wide = pl.BlockSpec((tm, 2 * d), lambda i: (i, 0))
    hspec = _resident((nh, CHUNK, CHUNK), lambda i: (0, 0, 0))
    return pl.pallas_call(
        body, grid=(n,),
        in_specs=[tile, wide, _wspec(wout, 0), _rowspec(d), _rowspec(d), hspec, hspec,
                  _resident((CHUNK, d), lambda i: (0, 0)), pl.BlockSpec(memory_space=pl.ANY)],
        out_specs=[wide, pl.BlockSpec((None, d, d), lambda i: (0, 0, 0)),
                   pl.BlockSpec((nh, CHUNK, CHUNK), lambda i: (0, 0, 0)),
                   pl.BlockSpec((CHUNK, nh), lambda i: (0, 0)), _rowspec(d), _rowspec(d)],
        out_shape=[jax.ShapeDtypeStruct((s, 2 * d), BF16), jax.ShapeDtypeStruct(gwout_buf.shape, F32),
                   jax.ShapeDtypeStruct((nh, CHUNK, CHUNK), F32), jax.ShapeDtypeStruct((CHUNK, nh), F32),
                   jax.ShapeDtypeStruct((1, d), F32), jax.ShapeDtypeStruct((1, d), F32)],
        scratch_shapes=[pltpu.VMEM((tm, d), F32), pltpu.VMEM((tm, d), F32), pltpu.VMEM((CHUNK, d), F32)],
        input_output_aliases={8: 1},
        compiler_params=_cparams(), name=name,
    )(dr1, zp, wout, lg, lb, wm, wmt, bs_exp, gwout_buf)


def _pool_counts(t0, tm, w):
    pos = t0 + lax.broadcasted_iota(jnp.int32, (tm, 1), 0)
    return jnp.minimum(pos + 1, w).astype(F32)


def _pool_fwd(xin, gin, bin_, win, wg, scale, wout, alpha, name):
    s, d = xin.shape
    ng, dg = wg.shape[0], wg.shape[1]
    hb = POOL_HALO
    tm = min(TM_POOL, s)
    n = s // tm
    assert ng == len(POOL_WINDOWS) and ng * dg == d and max(POOL_WINDOWS) <= hb

    def body(x_ref, g_ref, b_ref, win_ref, wg_ref, sc_ref, wout_ref, xo_ref, rs_ref, ys_ref, y_scr, z_scr):
        i = pl.program_id(0)

        @pl.when(i == 0)
        def _():
            y_scr[0:hb, :] = jnp.zeros((hb, d), F32)

        x = x_ref[...] * g_ref[...] + b_ref[...]
        y = _dot(x.astype(BF16), win_ref[...])
        ys_ref[...] = y.astype(BF16)
        y_scr[hb:hb + tm, :] = y
        for g, w in enumerate(POOL_WINDOWS):
            c0 = g * dg
            acc = y[:, c0:c0 + dg]
            for dd in range(1, w):
                acc = acc + y_scr[pl.ds(hb - dd, tm), c0:c0 + dg]
            pg = acc / _pool_counts(i * tm, tm, w) - y[:, c0:c0 + dg]
            z_scr[:, c0:c0 + dg] = _dot(pg.astype(BF16), wg_ref[g])
        y_scr[0:hb, :] = y_scr[tm:tm + hb, :]
        zz = (z_scr[...] * sc_ref[...]).astype(BF16)
        xhat, rstd = _ln_stats(alpha * x + _dot(zz, wout_ref[...]))
        xo_ref[...] = xhat
        rs_ref[...] = rstd

    tile = pl.BlockSpec((tm, d), lambda i: (i, 0))
    return pl.pallas_call(
        body, grid=(n,),
        in_specs=[tile, _rowspec(d), _rowspec(d), _wspec(win, 0),
                  _resident((ng, dg, dg), lambda i: (0, 0, 0)), _rowspec(d), _wspec(wout, 0)],
        out_specs=[tile, pl.BlockSpec((tm, 1), lambda i: (i, 0)), tile],
        out_shape=[jax.ShapeDtypeStruct((s, d), F32), jax.ShapeDtypeStruct((s, 1), F32),
                   jax.ShapeDtypeStruct((s, d), BF16)],
        scratch_shapes=[pltpu.VMEM((tm + hb, d), F32), pltpu.VMEM((tm, d), F32)],
        compiler_params=_cparams(), name=name,
    )(xin, gin, bin_, win, wg, scale, wout)


def _pool_bwd1(dr1, ys, wout, wg, scale, gwout_buf, name):
    s, d = dr1.shape
    ng, dg = wg.shape[0], wg.shape[1]
    hb = POOL_HALO
    tm = min(TM_POOL, s)
    n = s // tm
    halo_blocks = tm // hb

    def body(dr_ref, ys_ref, halo_ref, wout_ref, wg_ref, sc_ref, buf_ref,
             dy_ref, gwout_ref, gwg_ref, gsc_ref, y_scr, e_scr, z_scr, dp_scr):
        i = pl.program_id(0)
        t = n - 1 - i

        @pl.when(i == 0)
        def _():
            e_scr[tm:tm + hb, :] = jnp.zeros((hb, d), F32)
            gwout_ref[...] = jnp.zeros_like(gwout_ref)
            gwg_ref[...] = jnp.zeros_like(gwg_ref)
            gsc_ref[...] = jnp.zeros_like(gsc_ref)

        dob = dr_ref[...].astype(BF16)
        y = ys_ref[...].astype(F32)
        y_scr[0:hb, :] = jnp.where(t > 0, halo_ref[...].astype(F32), 0.0)
        y_scr[hb:hb + tm, :] = y
        pgs = []
        for g, w in enumerate(POOL_WINDOWS):
            c0 = g * dg
            acc = y[:, c0:c0 + dg]
            for dd in range(1, w):
                acc = acc + y_scr[pl.ds(hb - dd, tm), c0:c0 + dg]
            pg = (acc / _pool_counts(t * tm, tm, w) - y[:, c0:c0 + dg]).astype(BF16)
            pgs.append(pg)
            z_scr[:, c0:c0 + dg] = _dot(pg, wg_ref[g])
        zpre = z_scr[...]
        sc = sc_ref[...]
        gwout_ref[...] += _dot_tn((zpre * sc).astype(BF16), dob)
        dz = _dot_nt(dob, wout_ref[...])
        gsc_ref[...] += _colsum(dz * zpre)
        dzpre = (dz * sc).astype(BF16)
        for g, w in enumerate(POOL_WINDOWS):
            c0 = g * dg
            dzg = dzpre[:, c0:c0 + dg]
            gwg_ref[g] += _dot_tn(pgs[g], dzg)
            dp = _dot_nt(dzg, wg_ref[g])
            dp_scr[:, c0:c0 + dg] = dp
            e_scr[0:tm, c0:c0 + dg] = dp / _pool_counts(t * tm, tm, w)
        for g, w in enumerate(POOL_WINDOWS):
            c0 = g * dg
            acc = e_scr[0:tm, c0:c0 + dg]
            for dd in range(1, w):
                acc = acc + e_scr[pl.ds(dd, tm), c0:c0 + dg]
            dy_ref[:, c0:c0 + dg] = (acc - dp_scr[:, c0:c0 + dg]).astype(BF16)
        e_scr[tm:tm + hb, :] = e_scr[0:hb, :]

    tile = pl.BlockSpec((tm, d), lambda i: (n - 1 - i, 0))
    return pl.pallas_call(
        body, grid=(n,),
        in_specs=[tile, tile,
                  pl.BlockSpec((hb, d), lambda i: (jnp.maximum((n - 1 - i) * halo_blocks - 1, 0), 0)),
                  _wspec(wout, 0), _resident((ng, dg, dg), lambda i: (0, 0, 0)), _rowspec(d),
                  pl.BlockSpec(memory_space=pl.ANY)],
        out_specs=[tile, pl.BlockSpec((None, d, d), lambda i: (0, 0, 0)),
                   pl.BlockSpec((ng, dg, dg), lambda i: (0, 0, 0)), _rowspec(d)],
        out_shape=[jax.ShapeDtypeStruct((s, d), BF16), jax.ShapeDtypeStruct(gwout_buf.shape, F32),
                   jax.ShapeDtypeStruct((ng, dg, dg), F32), jax.ShapeDtypeStruct((1, d), F32)],
        scratch_shapes=[pltpu.VMEM((tm + hb, d), F32), pltpu.VMEM((tm + hb, d), F32),
                        pltpu.VMEM((tm, d), F32), pltpu.VMEM((tm, d), F32)],
        input_output_aliases={6: 1},
        compiler_params=_cparams(), name=name,
    )(dr1, ys, ys, wout, wg, scale, gwout_buf)


def _elementwise(fn, ins, out_dtypes, name):
    shape = ins[0].shape
    c = shape[-1]
    r = math.prod(shape[:-1])
    tr = _pick_rows(r, c, 4, 1 << 20)

    def body(*refs):
        vals = fn(*[ref[...] for ref in refs[:len(ins)]])
        for ref, v in zip(refs[len(ins):], vals):
            ref[...] = v.astype(ref.dtype)

    spec = pl.BlockSpec((tr, c), lambda i: (i, 0))
    outs = pl.pallas_call(
        body, grid=(r // tr,),
        in_specs=[spec] * len(ins), out_specs=[spec] * len(out_dtypes),
        out_shape=[jax.ShapeDtypeStruct((r, c), dt) for dt in out_dtypes],
        compiler_params=_cparams(parallel=True), name=name,
    )(*[a.reshape(r, c) for a in ins])
    return [o.reshape(shape) for o in outs]


def _prefetch_call(body, grid, in_specs, out_specs, out_shape, name, aliases=None):
    return pl.pallas_call(
        body,
        grid_spec=pltpu.PrefetchScalarGridSpec(num_scalar_prefetch=1, grid=grid, in_specs=in_specs, out_specs=out_specs),
        out_shape=out_shape, input_output_aliases=aliases or {},
        compiler_params=_cparams(len(grid), parallel=True), name=name)


def _cast_into_full(w3, l0, l, kind, chip1, name):
    _, r, c = w3.shape
    tr = _pick_rows(r, c, 4, 1 << 20)

    def body(k_ref, w_ref, o_ref):
        o_ref[...] = w_ref[...].astype(BF16)

    if kind == "row":
        out_spec = pl.BlockSpec((None, None, tr, c), lambda a, j, k: (a, k[0], j, 0))
    else:
        out_spec = pl.BlockSpec((None, tr, c), lambda a, j, k: (a, j, k[0]))
    return _prefetch_call(
        body, (l, r // tr), [pl.BlockSpec((None, tr, c), lambda a, j, k: (a + l0, j, 0))], out_spec,
        jax.ShapeDtypeStruct(_full_shape(kind, (l, r, c)), BF16), name)(chip1, w3)


def _pair_sum(g, got, kind, core1, name):
    if kind == "row":
        l, nc, sr, c = g.shape
        g5, got3 = g.reshape(l * nc, 2, sr // 2, c), got.reshape(l * nc, sr // 2, c)
    else:
        l, r, c = g.shape
        g5, got3 = g.reshape(l, 2, r // 2, c), got
    a, _, hr, c = g5.shape
    tr = _pick_rows(hr, c, 4, 1 << 20)

    def body(c_ref, g_ref, t_ref, o_ref):
        o_ref[...] = (g_ref[...] + t_ref[...]).astype(BF16)

    half = pl.BlockSpec((None, tr, c), lambda i, j, cc: (i, j, 0))
    out = _prefetch_call(
        body, (a, hr // tr), [pl.BlockSpec((None, None, tr, c), lambda i, j, cc: (i, cc[0], j, 0)), half], half,
        jax.ShapeDtypeStruct(got3.shape, BF16), name)(core1, g5, got3)
    return out.reshape(got.shape)


def _chip_sum(t, rb, kind, chip1, name):
    _, l, hr, sc = rb.shape
    tr = _pick_rows(hr, sc, 4, 1 << 19)

    def body(k_ref, t_ref, rb_ref, o_ref):
        acc = t_ref[...].astype(F32)
        for r in range(N_CHIPS - 1):
            acc = acc + rb_ref[r].astype(F32)
        o_ref[...] = acc

    if kind == "row":
        t_spec = pl.BlockSpec((None, None, tr, sc), lambda a, j, k: (a, k[0], j, 0))
    else:
        t_spec = pl.BlockSpec((None, tr, sc), lambda a, j, k: (a, j, k[0]))
    return _prefetch_call(
        body, (l, hr // tr),
        [t_spec, pl.BlockSpec((N_CHIPS - 1, None, tr, sc), lambda a, j, k: (0, a, j, 0))],
        pl.BlockSpec((None, tr, sc), lambda a, j, k: (a, j, 0)),
        jax.ShapeDtypeStruct((l, hr, sc), F32), name)(chip1, t, rb)


def _adamw_math(w_, g_, m_, v_):
    m2 = ADAM_B1 * m_ + (1.0 - ADAM_B1) * g_
    v2 = ADAM_B2 * v_ + (1.0 - ADAM_B2) * (g_ * g_)
    m_hat = m2 / (1.0 - ADAM_B1 ** ADAM_STEP)
    v_hat = v2 / (1.0 - ADAM_B2 ** ADAM_STEP)
    delta = -ADAM_LR * (m_hat / (jnp.sqrt(v_hat) + ADAM_EPS) + ADAM_WD * w_)
    return delta, m2, v2


def _adamw_big(w, m, v, own, other, core1, name, l0=0, into=None):
    lg, hr, c = own.shape
    lw = w.shape[0]
    view = lambda a: a.reshape(lw, 2, hr, c)
    tr = _pick_rows(hr, c, 4, 1 << 20)
    n_keep = 0 if into is None else len(into)

    def body(c_ref, w_ref, m_ref, v_ref, own_ref, oth_ref, *rest):
        g_ref, d_ref, m2_ref, v2_ref = rest[n_keep:]
        g = jnp.where(pl.program_id(1) == c_ref[0], own_ref[...], oth_ref[...])
        g_ref[...] = g
        d_ref[...], m2_ref[...], v2_ref[...] = _adamw_math(w_ref[...], g, m_ref[...], v_ref[...])

    s4 = pl.BlockSpec((None, None, tr, c), lambda a, h, j, cc: (a + l0, h, j, 0))
    s3 = pl.BlockSpec((None, tr, c), lambda a, h, j, cc: (a, j, 0))
    aliases = {6 + k: k for k in range(n_keep)}
    return _prefetch_call(
        body, (lg, 2, hr // tr), [s4, s4, s4, s3, s3] + [ANY] * n_keep, [s4] * 4,
        [jax.ShapeDtypeStruct((lw, 2, hr, c), F32)] * 4, name, aliases,
    )(core1, view(w), view(m), view(v), own, other, *(into or []))


def _sum_devices(own, gathered, me1, name):
    r, c = own.shape
    tr = _pick_rows(r, c, 4, 1 << 17)

    def body(me_ref, own_ref, g_ref, o_ref):
        acc = None
        for k in range(N_DEV):
            v = jnp.where(me_ref[0] == k, own_ref[...], g_ref[k])
            acc = v if acc is None else acc + v
        o_ref[...] = acc

    return _prefetch_call(
        body, (r // tr,),
        [pl.BlockSpec((tr, c), lambda i, m: (i, 0)), pl.BlockSpec((N_DEV, tr, c), lambda i, m: (0, i, 0))],
        pl.BlockSpec((tr, c), lambda i, m: (i, 0)), jax.ShapeDtypeStruct((r, c), F32), name)(me1, own, gathered)


def _adamw(w, g, m, v, name):
    return _elementwise(_adamw_math, [w, g, m, v], [F32, F32, F32], name)


ANY = pl.BlockSpec(memory_space=pl.ANY)


def _mesh_pos():
    return lax.axis_index("x"), lax.axis_index("y"), lax.axis_index("c")


def _chip_peers(x, y, c):
    out = []
    for r in (1, 2, 3):
        px = 1 - x if r & 2 else x
        py = 1 - y if r & 1 else y
        out.append((2 * px + py, (px, py, c)))
    return out


def _full_shape(kind, shard_shape):
    l, r, c = shard_shape
    return (l, N_CHIPS, r, c) if kind == "row" else (l, r, N_CHIPS * c)


def _full_piece(ref, kind, k, h, hr, sc):
    rows = pl.ds(pl.multiple_of(h * hr, SUBLANES_BF16), hr)
    if kind == "row":
        return ref.at[:, k, rows, :]
    return ref.at[:, rows, pl.ds(pl.multiple_of(k * sc, LANES), sc)]


def _remote(src, dst, ssem, rsem, dev):
    return pltpu.make_async_remote_copy(src_ref=src, dst_ref=dst, send_sem=ssem, recv_sem=rsem,
                                        device_id=dev, device_id_type=MESH)


DMA_CHUNK_BYTES = 1 << 20
DMA_MAX_CHUNKS = 32


def _chunk_views(src, dst):
    axis = len(src.shape) - 2
    rows = src.shape[axis]
    nbytes = math.prod(src.shape) * jnp.dtype(src.dtype).itemsize
    n = max(1, min(DMA_MAX_CHUNKS, nbytes // DMA_CHUNK_BYTES))
    while n > 1 and (rows % n or (rows // n) % SUBLANES_BF16):
        n -= 1
    cr = rows // n
    out = []
    for i in range(n):
        idx = (slice(None),) * axis + (pl.ds(i * cr, cr), slice(None))
        out.append((src.at[idx], dst.at[idx]))
    return out


def _start_remote(src, dst, ssem, rsem, dev):
    for s, t in _chunk_views(src, dst):
        _remote(s, t, ssem, rsem, dev).start()
    return _remote(src, dst, ssem, rsem, dev)


def _allgather_steps(fulls, kinds):
    nw = len(fulls)

    def dims(a, kind):
        return (a.shape[2] // 2, a.shape[3]) if kind == "row" else (a.shape[1] // 2, a.shape[2] // N_CHIPS)

    hrs = [dims(a, k)[0] for a, k in zip(fulls, kinds)]
    scs = [dims(a, k)[1] for a, k in zip(fulls, kinds)]

    def piece(ref, w, k, h):
        return _full_piece(ref, kinds[w], k, h, hrs[w], scs[w])

    def copies1(src, dst, sems, start):
        x, y, c = _mesh_pos()
        k_me = 2 * x + y
        out = []
        for w in range(nw):
            for r, (kj, dev) in enumerate(_chip_peers(x, y, c)):
                args = (sems[0].at[3 * w + r], sems[1].at[3 * w + r], dev)
                if start:
                    out.append(_start_remote(piece(src[w], w, k_me, c), piece(dst[w], w, k_me, c), *args))
                else:
                    out.append(_remote(piece(src[w], w, k_me, c), piece(dst[w], w, kj, c), *args))
        return out

    def copies2(src, dst, sems, start):
        x, y, c = _mesh_pos()
        out = []
        for w in range(nw):
            for r, (kj, _) in enumerate(_chip_peers(x, y, c)):
                args = (sems[0].at[3 * w + r], sems[1].at[3 * w + r], (x, y, 1 - c))
                if start:
                    out.append(_start_remote(piece(src[w], w, kj, c), piece(dst[w], w, kj, c), *args))
                else:
                    out.append(_remote(piece(src[w], w, kj, 1 - c), piece(dst[w], w, kj, 1 - c), *args))
        return out

    def finish(copies):
        def fn(src, dst, sems):
            for cp in copies(src, dst, sems, False):
                cp.wait_recv()
            for cp in copies(src, dst, sems, False):
                cp.wait_send()
        return fn

    step1 = (lambda s, d, m: copies1(s, d, m, True), finish(copies1))
    step2 = (lambda s, d, m: copies2(s, d, m, True), finish(copies2))
    return step1, step2


def _exchange_hook(arrays, step, sem_len, out_shapes=None):
    shapes = out_shapes or [jax.ShapeDtypeStruct(a.shape, a.dtype) for a in arrays]
    return dict(arrays=list(arrays), out_shapes=shapes, in_place=out_shapes is None, sem_len=sem_len,
                first=step[0], last=step[1])


def _combine_hooks(h1, h2):
    assert h1["in_place"] == h2["in_place"]
    n1, o1, s1, s2 = len(h1["arrays"]), len(h1["out_shapes"]), h1["sem_len"], h2["sem_len"]

    def both(which):
        def fn(src, dst, sems):
            h1[which](src[:n1], dst[:o1], [s.at[pl.ds(0, s1)] for s in sems])
            h2[which](src[n1:], dst[o1:], [s.at[pl.ds(s1, s2)] for s in sems])
        return fn

    return dict(arrays=h1["arrays"] + h2["arrays"], out_shapes=h1["out_shapes"] + h2["out_shapes"],
                in_place=h1["in_place"], sem_len=s1 + s2, first=both("first"), last=both("last"))


def _exchange_call(hook, name):
    nh, nho = len(hook["arrays"]), len(hook["out_shapes"])

    def body(*refs):
        h_in, h_out, sems = refs[:nh], refs[nh:nh + nho], refs[nh + nho:]
        hook["first"](h_in, h_out, sems)
        hook["last"](h_in, h_out, sems)

    return pl.pallas_call(
        body, in_specs=[ANY] * nh, out_specs=[ANY] * nho, out_shape=hook["out_shapes"],
        scratch_shapes=[pltpu.SemaphoreType.DMA((hook["sem_len"],))] * 2,
        input_output_aliases={k: k for k in range(nh)} if hook["in_place"] else {}, name=name,
    )(*hook["arrays"])


def _allgather_weights(fulls, kinds):
    nw = len(fulls)
    step1, step2 = _allgather_steps(fulls, kinds)

    def body(*refs):
        mine, fu = refs[:nw], refs[nw:2 * nw]
        sems1, sems2 = refs[2 * nw:2 * nw + 2], refs[2 * nw + 2:]
        step1[0](mine, fu, sems1)
        step1[1](mine, fu, sems1)
        step2[0](fu, fu, sems2)
        step2[1](fu, fu, sems2)

    return pl.pallas_call(
        body,
        in_specs=[ANY] * nw, out_specs=[ANY] * nw,
        out_shape=[jax.ShapeDtypeStruct(a.shape, a.dtype) for a in fulls],
        scratch_shapes=[pltpu.SemaphoreType.DMA((3 * nw,))] * 4,
        input_output_aliases={w: w for w in range(nw)},
        name="allgather_weights",
    )(*fulls)


def _hosted(body, n_in, n_out, hook, n_steps):
    if hook is None:
        return body
    nh, nho = len(hook["arrays"]), len(hook["out_shapes"])

    def wrapped(*refs):
        ins, h_in = refs[:n_in], refs[n_in:n_in + nh]
        outs = refs[n_in + nh:n_in + nh + n_out]
        h_out = refs[n_in + nh + n_out:n_in + nh + n_out + nho]
        rest = refs[n_in + nh + n_out + nho:]
        scr, sems = rest[:-2], rest[-2:]
        i = pl.program_id(0)

        @pl.when(i == 0)
        def _():
            hook["first"](h_in, h_out, sems)

        body(*ins, *outs, *scr)

        @pl.when(i == n_steps - 1)
        def _():
            hook["last"](h_in, h_out, sems)

    return wrapped


def _hosted_call(body, hook, n_steps, in_specs, out_specs, out_shape, scratch_shapes, args, name, aliases=None):
    n_in, n_out = len(in_specs), len(out_specs)
    aliases = dict(aliases or {})
    if hook is not None:
        nh = len(hook["arrays"])
        in_specs = list(in_specs) + [ANY] * nh
        out_specs = list(out_specs) + [ANY] * len(hook["out_shapes"])
        out_shape = list(out_shape) + list(hook["out_shapes"])
        scratch_shapes = list(scratch_shapes) + [pltpu.SemaphoreType.DMA((hook["sem_len"],))] * 2
        if hook["in_place"]:
            aliases.update({n_in + k: n_out + k for k in range(nh)})
        args = list(args) + hook["arrays"]
    outs = pl.pallas_call(
        _hosted(body, n_in, n_out, hook, n_steps), grid=(n_steps,),
        in_specs=in_specs, out_specs=out_specs, out_shape=out_shape, scratch_shapes=scratch_shapes,
        input_output_aliases=aliases, compiler_params=_cparams(), name=name,
    )(*args)
    return outs[:n_out], outs[n_out:]


def _pair_exchange(copies):
    def finish(src, dst, sems):
        for cp in copies(src, dst, sems, False):
            cp.wait_recv()
        for cp in copies(src, dst, sems, False):
            cp.wait_send()
    return (lambda s, d, m: copies(s, d, m, True), finish)


def _rs_pair(fulls, kinds):
    nw = len(fulls)

    def half_all(ref, kind, h):
        if kind == "row":
            hr = ref.shape[2] // 2
            return ref.at[:, :, pl.ds(pl.multiple_of(h * hr, SUBLANES_BF16), hr), :]
        hr = ref.shape[1] // 2
        return ref.at[:, pl.ds(pl.multiple_of(h * hr, SUBLANES_BF16), hr), :]

    def half_shape(kind, shape):
        if kind == "row":
            return (shape[0], shape[1], shape[2] // 2, shape[3])
        return (shape[0], shape[1] // 2, shape[2])

    def copies(g, got, sems, start):
        x, y, c = _mesh_pos()
        make = _start_remote if start else _remote
        return [make(half_all(g[w], kinds[w], 1 - c), got[w], sems[0].at[w], sems[1].at[w], (x, y, 1 - c))
                for w in range(nw)]

    shapes = [jax.ShapeDtypeStruct(half_shape(k, a.shape), a.dtype) for k, a in zip(kinds, fulls)]
    return _exchange_hook(fulls, _pair_exchange(copies), nw, shapes)


def _rs_chips(parts, kinds):
    nw = len(parts)

    def slot(ref, kind, k):
        if kind == "row":
            return ref.at[:, k]
        sc = ref.shape[2] // N_CHIPS
        return ref.at[:, :, pl.ds(pl.multiple_of(k * sc, LANES), sc)]

    def slot_shape(kind, shape):
        if kind == "row":
            return (shape[0], shape[2], shape[3])
        return (shape[0], shape[1], shape[2] // N_CHIPS)

    def copies(t, rb, sems, start):
        x, y, c = _mesh_pos()
        make = _start_remote if start else _remote
        return [make(slot(t[w], kinds[w], kj), rb[w].at[r], sems[0].at[3 * w + r], sems[1].at[3 * w + r], dev)
                for w in range(nw) for r, (kj, dev) in enumerate(_chip_peers(x, y, c))]

    shapes = [jax.ShapeDtypeStruct((N_CHIPS - 1,) + slot_shape(k, a.shape), a.dtype) for k, a in zip(kinds, parts)]
    return _exchange_hook(parts, _pair_exchange(copies), 3 * nw, shapes)


def _rs_join(halves):
    nw = len(halves)

    def copies(src, dst, sems, start):
        x, y, c = _mesh_pos()
        make = _start_remote if start else _remote
        return [make(src[w], dst[w], sems[0].at[w], sems[1].at[w], (x, y, 1 - c)) for w in range(nw)]

    return _exchange_hook(halves, _pair_exchange(copies), nw,
                          [jax.ShapeDtypeStruct(a.shape, a.dtype) for a in halves])


def _allgather_small(buf, name):
    def body(in_ref, out_ref, ssem, rsem):
        x, y, c = _mesh_pos()
        me = 4 * x + 2 * y + c
        cps, waits = [], []
        for r in range(1, N_DEV):
            px = 1 - x if r & 4 else x
            py = 1 - y if r & 2 else y
            pc = 1 - c if r & 1 else c
            cp = _remote(in_ref, out_ref.at[me], ssem.at[r - 1], rsem.at[r - 1], (px, py, pc))
            cp.start()
            cps.append(cp)
            waits.append(_remote(in_ref, out_ref.at[4 * px + 2 * py + pc], ssem.at[r - 1], rsem.at[r - 1], (px, py, pc)))
        for wt in waits:
            wt.wait_recv()
        for cp in cps:
            cp.wait_send()

    return pl.pallas_call(
        body, in_specs=[ANY], out_specs=ANY,
        out_shape=jax.ShapeDtypeStruct((N_DEV,) + buf.shape, buf.dtype),
        scratch_shapes=[pltpu.SemaphoreType.DMA((N_DEV - 1,))] * 2,
        name=name,
    )(buf)


def _pack(arrs):
    flat = jnp.concatenate([a.reshape(-1).astype(F32) for a in arrs])
    rows = -(-flat.shape[0] // (LANES * 16)) * 16
    return jnp.pad(flat, (0, rows * LANES - flat.shape[0])).reshape(rows, LANES)


def _unpack(buf, shapes):
    flat = buf.reshape(-1)
    out, off = [], 0
    for shp in shapes:
        nel = math.prod(shp)
        out.append(flat[off:off + nel].reshape(shp))
        off += nel
    return out


BIG = ("a_w_in", "a_w_out", "b_w_in", "b_w_out", "c_w_in", "c_w_grp", "c_w_out", "f_w_up", "f_w_down")
BIG_KIND = {"a_w_in": "col", "a_w_out": "row", "b_w_in": "col", "b_w_out": "row", "c_w_in": "row",
            "c_w_grp": "row", "c_w_out": "row", "f_w_up": "col", "f_w_down": "row"}
FIRST_LAYER = ("a_w_in", "a_w_out", "f_w_up", "f_w_down")
SHARDED_SMALL =("a_dw", "a_dw_b", "a_ln_g", "a_ln_b", "c_scale", "f_dw")
REPLICATED = ("b_ln_g", "b_ln_b", "b_ws", "b_bs", "ln1_g", "ln1_b", "ln2_g", "ln2_b")
WEIGHTS = ("a_w_in", "a_dw", "a_dw_b", "a_ln_g", "a_ln_b", "a_w_out", "b_w_in", "b_ln_g", "b_ln_b", "b_ws", "b_bs",
           "b_w_out", "c_w_in", "c_w_grp", "c_scale", "c_w_out", "f_w_up", "f_dw", "f_w_down",
           "ln1_g", "ln1_b", "ln2_g", "ln2_b")


def _as3d(a):
    return a.reshape((-1,) + a.shape[-2:])


def kernel(x, a_w_in, a_dw, a_dw_b, a_ln_g, a_ln_b, a_w_out, b_w_in, b_ln_g, b_ln_b, b_ws, b_bs, b_w_out, c_w_in, c_w_grp, c_scale, c_w_out, f_w_up, f_dw, f_w_down, ln1_g, ln1_b, ln2_g, ln2_b, loss_target, m_a_w_in, m_a_dw, m_a_dw_b, m_a_ln_g, m_a_ln_b, m_a_w_out, m_b_w_in, m_b_ln_g, m_b_ln_b, m_b_ws, m_b_bs, m_b_w_out, m_c_w_in, m_c_w_grp, m_c_scale, m_c_w_out, m_f_w_up, m_f_dw, m_f_w_down, m_ln1_g, m_ln1_b, m_ln2_g, m_ln2_b, v_a_w_in, v_a_dw, v_a_dw_b, v_a_ln_g, v_a_ln_b, v_a_w_out, v_b_w_in, v_b_ln_g, v_b_ln_b, v_b_ws, v_b_bs, v_b_w_out, v_c_w_in, v_c_w_grp, v_c_scale, v_c_w_out, v_f_w_up, v_f_dw, v_f_w_down, v_ln1_g, v_ln1_b, v_ln2_g, v_ln2_b):
    w = dict(a_w_in=a_w_in, a_dw=a_dw, a_dw_b=a_dw_b, a_ln_g=a_ln_g, a_ln_b=a_ln_b, a_w_out=a_w_out, b_w_in=b_w_in, b_ln_g=b_ln_g, b_ln_b=b_ln_b, b_ws=b_ws, b_bs=b_bs, b_w_out=b_w_out, c_w_in=c_w_in, c_w_grp=c_w_grp, c_scale=c_scale, c_w_out=c_w_out, f_w_up=f_w_up, f_dw=f_dw, f_w_down=f_w_down, ln1_g=ln1_g, ln1_b=ln1_b, ln2_g=ln2_g, ln2_b=ln2_b)
    mom = dict(a_w_in=m_a_w_in, a_dw=m_a_dw, a_dw_b=m_a_dw_b, a_ln_g=m_a_ln_g, a_ln_b=m_a_ln_b, a_w_out=m_a_w_out, b_w_in=m_b_w_in, b_ln_g=m_b_ln_g, b_ln_b=m_b_ln_b, b_ws=m_b_ws, b_bs=m_b_bs, b_w_out=m_b_w_out, c_w_in=m_c_w_in, c_w_grp=m_c_w_grp, c_scale=m_c_scale, c_w_out=m_c_w_out, f_w_up=m_f_w_up, f_dw=m_f_dw, f_w_down=m_f_w_down, ln1_g=m_ln1_g, ln1_b=m_ln1_b, ln2_g=m_ln2_g, ln2_b=m_ln2_b)
    var = dict(a_w_in=v_a_w_in, a_dw=v_a_dw, a_dw_b=v_a_dw_b, a_ln_g=v_a_ln_g, a_ln_b=v_a_ln_b, a_w_out=v_a_w_out, b_w_in=v_b_w_in, b_ln_g=v_b_ln_g, b_ln_b=v_b_ln_b, b_ws=v_b_ws, b_bs=v_b_bs, b_w_out=v_b_w_out, c_w_in=v_c_w_in, c_w_grp=v_c_w_grp, c_scale=v_c_scale, c_w_out=v_c_w_out, f_w_up=v_f_w_up, f_dw=v_f_dw, f_w_down=v_f_w_down, ln1_g=v_ln1_g, ln1_b=v_ln1_b, ln2_g=v_ln2_g, ln2_b=v_ln2_b)

    depth = ln1_g.shape[0]
    d = x.shape[-1]
    alpha = float((2 * depth) ** 0.25)
    chip = 2 * lax.axis_index("x") + lax.axis_index("y")
    chip1 = chip.astype(jnp.int32).reshape(1)
    core1 = lax.axis_index("c").astype(jnp.int32).reshape(1)

    assert depth == 4
    early = [(k, 0, 1) for k in FIRST_LAYER]
    late = [(k, 1, w[k].shape[0] - 1) for k in FIRST_LAYER] + [(k, 0, _as3d(w[k]).shape[0]) for k in BIG if k not in FIRST_LAYER]
    groups = {
        "up_front": [("a_w_in", 0, 1), ("a_w_out", 0, 1)],
        "soon": [("f_w_up", 0, 1), ("f_w_down", 0, 1)],
        "mid": [("f_w_up", 1, 2), ("f_w_down", 1, 2)] + [(k, 0, _as3d(w[k]).shape[0]) for k in BIG if k not in FIRST_LAYER],
        "last": [("a_w_in", 1, 1), ("a_w_out", 1, 1), ("f_w_up", 3, 1), ("f_w_down", 3, 1)],
    }
    where = {(k, l0 + t): (name, t) for name, group in groups.items() for k, l0, nl in group for t in range(nl)}

    def group_kinds(group):
        return [BIG_KIND[k] for k, _, _ in group]

    def cast_group(name):
        return [_cast_into_full(_as3d(w[k]), l0, nl, BIG_KIND[k], chip1, f"cast_{name}_{k}") for k, l0, nl in groups[name]]

    def as_stacks(name, arrays):
        return {k: (a.reshape(a.shape[0], -1, a.shape[-1]) if BIG_KIND[k] == "row" else a)
                for (k, _, _), a in zip(groups[name], arrays)}

    def ag_hook(name, arrays, step):
        steps = _allgather_steps(arrays, group_kinds(groups[name]))
        return _exchange_hook(arrays, steps[step], 3 * len(arrays))

    full = {"up_front": as_stacks("up_front", _allgather_weights(cast_group("up_front"), group_kinds(groups["up_front"])))}

    def weight(k, l):
        name, idx = where[(k, l)]
        return full[name][k], idx

    small_all = _allgather_small(_pack([w[k] for k in SHARDED_SMALL]), "allgather_small_params")
    other_core = 1 - lax.axis_index("c")
    per_chip = [_unpack(lax.dynamic_index_in_dim(small_all, 2 * k + other_core, keepdims=False),
                        [w[n].shape for n in SHARDED_SMALL]) for k in range(N_CHIPS)]
    fs = {n: jnp.concatenate([per_chip[k][i] for k in range(N_CHIPS)], axis=-1) for i, n in enumerate(SHARDED_SMALL)}

    nh = b_ws.shape[1]
    tril = jnp.tril(jnp.ones((CHUNK, CHUNK), F32))
    wm = (b_ws[0] * tril).astype(BF16)
    wmt = jnp.swapaxes(wm, 1, 2)
    bs_exp = jnp.repeat(jnp.transpose(b_bs[0]), CHUNK, axis=1)

    xh, g, b = x[0], jnp.ones((1, d), F32), jnp.zeros((1, d), F32)
    saved = []
    for i in range(depth):
        kind, j = i % 3, i // 3
        rec = dict(xin=xh, gin=g, bin=b)
        if kind == 0:
            hook = None
            if i == 0:
                n_soon = len(groups["soon"])
                hook = _combine_hooks(ag_hook("soon", cast_group("soon"), 0), ag_hook("mid", cast_group("mid"), 0))
            (xh1, rstd1, p, chat, rstdc), landed = _conv_fwd(
                xh, g, b, weight("a_w_in", j)[0], weight("a_w_out", j)[0], weight("a_w_in", j)[1], fs["a_dw"], j,
                fs["a_dw_b"][j:j + 1], fs["a_ln_g"][j:j + 1], fs["a_ln_b"][j:j + 1], alpha, f"conv_fwd_{i}", hook)
            if i == 0:
                full["soon"] = as_stacks("soon", _exchange_call(ag_hook("soon", landed[:n_soon], 1), "allgather_soon_d2d"))
                mid_landed = landed[n_soon:]
            rec.update(p=p, chat=chat, rstdc=rstdc)
        elif kind == 1:
            xh1, rstd1, zp = _sgu_fwd(xh, g, b, full["mid"]["b_w_in"], b_ln_g, b_ln_b, wm, bs_exp,
                                      full["mid"]["b_w_out"], alpha, f"sgu_fwd_{i}")
            rec.update(zp=zp)
        else:
            xh1, rstd1, ys = _pool_fwd(xh, g, b, full["mid"]["c_w_in"], full["mid"]["c_w_grp"], fs["c_scale"],
                                       full["mid"]["c_w_out"], alpha, f"pool_fwd_{i}")
            rec.update(ys=ys)
        hook = None
        if i == 0:
            hook = _combine_hooks(ag_hook("mid", mid_landed, 1), ag_hook("last", cast_group("last"), 0))
        elif i == 1:
            hook = ag_hook("last", last_landed, 1)
        (xh2, rstd2, hs, hcs), passed_on = _ffn_fwd(
            xh1, ln1_g[i:i + 1], ln1_b[i:i + 1], weight("f_w_up", i)[0], weight("f_w_down", i)[0],
            weight("f_w_up", i)[1], fs["f_dw"], i, alpha, f"ffn_fwd_{i}", hook)
        if i == 0:
            full["mid"] = as_stacks("mid", passed_on[:len(mid_landed)])
            last_landed = passed_on[len(mid_landed):]
        elif i == 1:
            full["last"] = as_stacks("last", passed_on)
        rec.update(xh1=xh1, rstd1=rstd1, xh2=xh2, rstd2=rstd2, hs=hs, hcs=hcs)
        saved.append(rec)
        xh, g, b = xh2, ln2_g[i:i + 1], ln2_b[i:i + 1]

    dxo = loss_target[0]
    assert depth >= 3

    def stack_shape(k, nl):
        _, r, c = _as3d(w[k]).shape
        return (nl, N_CHIPS * r, c) if BIG_KIND[k] == "row" else (nl, r, N_CHIPS * c)

    g_first = {k: lax.empty(stack_shape(k, nl), F32) for k, _, nl in early}
    g_rest = {k: lax.empty(stack_shape(k, nl), F32) for k, _, nl in late if k != "c_w_grp"}

    def gslot(k, l):
        if k in FIRST_LAYER:
            return (g_first, 0) if l == 0 else (g_rest, l - 1)
        return g_rest, l

    def rs_views(group, store):
        out = []
        for k, _, _ in group:
            a = store[k]
            out.append(a.reshape(a.shape[0], N_CHIPS, -1, a.shape[-1]) if BIG_KIND[k] == "row" else a)
        return out

    def pair_sums(group, views, got, tag):
        return [_pair_sum(a, t, BIG_KIND[k], core1, f"rs_pair_sum_{tag}_{k}") for (k, _, _), a, t in zip(group, views, got)]

    def reduce_and_join(group, pair, from_chips, tag):
        half = [_chip_sum(t, rb, BIG_KIND[k], chip1, f"rs_chip_sum_{tag}_{k}")
                for (k, _, _), t, rb in zip(group, pair, from_chips)]
        return half, _exchange_call(_rs_join(half), f"rs_join_{tag}")

    ffn0, mix0 = groups["soon"], groups["up_front"]
    late_kinds = group_kinds(late)
    gs = {k: [None] * w[k].shape[0] for k in ("a_dw", "a_dw_b", "a_ln_g", "a_ln_b", "f_dw", "ln1_g", "ln1_b", "ln2_g", "ln2_b")}
    for i in reversed(range(depth)):
        kind, j = i % 3, i // 3
        rec = saved[i]
        hook = None
        if i == 0:
            late_views = rs_views(late, g_rest)
            hook = _rs_pair(late_views, late_kinds)
        st, idx = gslot("f_w_down", i)
        (dr2, dhc, st["f_w_down"], gs["ln2_g"][i], gs["ln2_b"][i], loss_term), got = _ffn_bwd1(
            dxo, rec["xh2"], rec["rstd2"], ln2_g[i:i + 1], ln2_b[i:i + 1], rec["hcs"], *weight("f_w_down", i),
            st["f_w_down"], idx, f"ffn_bwd1_{i}", hook, loss_head=(i == depth - 1))
        if i == depth - 1:
            loss = lax.psum(loss_term[0, 0], ("x", "y", "c"))
        if i == 0:
            late_pair = pair_sums(late, late_views, got, "rest")
        dr1, dh, gs["f_dw"][i], gs["ln1_g"][i], gs["ln1_b"][i] = _ffn_bwd2(
            dhc, rec["hs"], dr2, *weight("f_w_up", i), fs["f_dw"], i, rec["xh1"], rec["rstd1"], ln1_g[i:i + 1],
            alpha, f"ffn_bwd2_{i}")
        st, idx = gslot("f_w_up", i)
        st["f_w_up"] = _mm_tn(rec["xh1"], ln1_g[i:i + 1], ln1_b[i:i + 1], dh, st["f_w_up"], idx, f"grad_w_up_{i}")
        if kind == 0:
            hook = None
            if i == 0:
                ffn0_views = rs_views(ffn0, g_first)
                ffn0_got = _exchange_call(_rs_pair(ffn0_views, group_kinds(ffn0)), "rs_pair_ffn0")
                ffn0_pair = pair_sums(ffn0, ffn0_views, ffn0_got, "ffn0")
                hook = _combine_hooks(_rs_chips(late_pair, late_kinds), _rs_chips(ffn0_pair, group_kinds(ffn0)))
            st, idx = gslot("a_w_out", j)
            (dp, st["a_w_out"], gs["a_dw"][j], gs["a_dw_b"][j], gs["a_ln_g"][j], gs["a_ln_b"][j]), landed = _conv_bwd1(
                dr1, rec["chat"], rec["rstdc"], rec["p"], *weight("a_w_out", j), fs["a_dw"], j,
                fs["a_ln_g"][j:j + 1], fs["a_ln_b"][j:j + 1], st["a_w_out"], idx, f"conv_bwd1_{i}", hook)
            if i == 0:
                late_from_chips, ffn0_from_chips = landed[:len(late)], landed[len(late):]
            win_name, lidx = "a_w_in", j
        elif kind == 1:
            dp, g_rest["b_w_out"], g_ws, g_bs_t, g_blg, g_blb = _sgu_bwd1(
                dr1, rec["zp"], full["mid"]["b_w_out"], b_ln_g, b_ln_b, wm, wmt, bs_exp, g_rest["b_w_out"],
                f"sgu_bwd1_{i}")
            win_name, lidx = "b_w_in", 0
        else:
            dp, g_rest["c_w_out"], g_rest["c_w_grp"], g_cscale = _pool_bwd1(
                dr1, rec["ys"], full["mid"]["c_w_out"], full["mid"]["c_w_grp"], fs["c_scale"], g_rest["c_w_out"],
                f"pool_bwd1_{i}")
            win_name, lidx = "c_w_in", 0
        dxo = _bwd_in(dp, dr1, *weight(win_name, lidx), alpha, f"mixer_bwd2_{i}")
        st, idx = gslot(win_name, lidx)
        st[win_name] = _mm_tn(rec["xin"], rec["gin"], rec["bin"], dp, st[win_name], idx, f"grad_w_in_{i}")
    grad_x = dxo[None]

    late_half, late_other = reduce_and_join(late, late_pair, late_from_chips, "rest")
    ffn0_half, ffn0_other = reduce_and_join(ffn0, ffn0_pair, ffn0_from_chips, "ffn0")
    mix0_views = rs_views(mix0, g_first)
    mix0_got = _exchange_call(_rs_pair(mix0_views, group_kinds(mix0)), "rs_pair_mix0")
    mix0_pair = pair_sums(mix0, mix0_views, mix0_got, "mix0")
    mix0_from_chips = _exchange_call(_rs_chips(mix0_pair, group_kinds(mix0)), "rs_chips_mix0")
    mix0_half, mix0_other = reduce_and_join(mix0, mix0_pair, mix0_from_chips, "mix0")

    updates = {}
    for (k, l0, _), own, oth in zip(late, late_half, late_other):
        updates[k] = _adamw_big(_as3d(w[k]), _as3d(mom[k]), _as3d(var[k]), own, oth, core1, f"adamw_rest_{k}", l0)
    for (k, l0, _), own, oth in zip(ffn0 + mix0, ffn0_half + mix0_half, ffn0_other + mix0_other):
        updates[k] = _adamw_big(_as3d(w[k]), _as3d(mom[k]), _as3d(var[k]), own, oth, core1, f"adamw_first_{k}",
                                l0, into=updates[k])
    grads, delta, new_m, new_v = {}, {}, {}, {}
    for k in BIG:
        grads[k], delta[k], new_m[k], new_v[k] = [o.reshape(w[k].shape) for o in updates[k]]

    small_full = {
        "a_dw": jnp.stack(gs["a_dw"]), "a_dw_b": jnp.concatenate(gs["a_dw_b"]), "a_ln_g": jnp.concatenate(gs["a_ln_g"]),
        "a_ln_b": jnp.concatenate(gs["a_ln_b"]), "c_scale": g_cscale, "f_dw": jnp.stack(gs["f_dw"]),
        "b_ln_g": g_blg, "b_ln_b": g_blb, "b_ws": g_ws[None], "b_bs": jnp.transpose(g_bs_t)[None],
        "ln1_g": jnp.concatenate(gs["ln1_g"]), "ln1_b": jnp.concatenate(gs["ln1_b"]),
        "ln2_g": jnp.concatenate(gs["ln2_g"]), "ln2_b": jnp.concatenate(gs["ln2_b"]),
    }
    small_names = SHARDED_SMALL + REPLICATED
    small_shapes = [small_full[n].shape for n in small_names]
    small_packed = _pack([small_full[n] for n in small_names])
    gathered_small = _allgather_small(small_packed, "allgather_small_grads")
    me1 = (2 * chip + lax.axis_index("c")).astype(jnp.int32).reshape(1)
    summed = _unpack(_sum_devices(small_packed, gathered_small, me1, "small_grad_sum"), small_shapes)
    for n, a in zip(small_names, summed):
        if n in SHARDED_SMALL:
            cs = w[n].shape[-1]
            a = lax.dynamic_slice_in_dim(a, chip * cs, cs, axis=a.ndim - 1)
        grads[n] = a

    shapes = [w[n].shape for n in small_names]
    ds_, ms_, vs_ = _adamw(_pack([w[n] for n in small_names]), _pack([grads[n] for n in small_names]),
                           _pack([mom[n] for n in small_names]), _pack([var[n] for n in small_names]), "adamw_small")
    for n, a, bb, cc in zip(small_names, _unpack(ds_, shapes), _unpack(ms_, shapes), _unpack(vs_, shapes)):
        delta[n], new_m[n], new_v[n] = a, bb, cc

    return (loss, grad_x, *[grads[n] for n in WEIGHTS], *[delta[n] for n in WEIGHTS],
            *[new_m[n] for n in WEIGHTS], *[new_v[n] for n in WEIGHTS])
```

```python
import math

import jax
import jax.numpy as jnp
from jax import lax
from jax.experimental import pallas as pl
from jax.experimental.pallas import tpu as pltpu

F32 = jnp.float32
BF16 = jnp.bfloat16

LN_EPS = 1e-5
POOL_WINDOWS = (2, 4, 8, 16)
CHUNK = 128
ADAM_LR = 0.001
ADAM_B1 = 0.9
ADAM_B2 = 0.999
ADAM_EPS = 1e-08
ADAM_WD = 0.01
ADAM_STEP = 10

LANES = 128
SUBLANES_BF16 = 16
N_CHIPS = 4
N_DEV = 8
VMEM_LIMIT = 60 * 1024 * 1024

TM_FFN = 512
TM_FFN_BWD1 = 256
TM_CONV = 256
TM_SGU = 512
TM_POOL = 512
TM_BWD_IN = 512
TS_MM_TN = 2048
CW_FFN = 256
CW_FFN_BWD2 = 512
CONV_HALO = 32
CONV_ROW_BLOCK = 64
POOL_HALO = 16
FFN_HALO = 16

MESH = pl.DeviceIdType.MESH


def _cparams(n_grid=1, parallel=False):
    sem = ("parallel" if parallel else "arbitrary",) * n_grid
    return pltpu.CompilerParams(dimension_semantics=sem, vmem_limit_bytes=VMEM_LIMIT)


def _resident(block, imap):
    return pl.BlockSpec(block, imap, pipeline_mode=pl.Buffered(1))


def _wspec(w, l):
    _, r, c = w.shape
    return _resident((None, r, c), lambda *_: (l, 0, 0))


def _rowspec(d):
    return pl.BlockSpec((1, d), lambda *_: (0, 0))


def _dot(a, b):
    return jnp.dot(a, b, preferred_element_type=F32)


def _dot_nt(a, b):
    return lax.dot_general(a, b, (((1,), (1,)), ((), ())), preferred_element_type=F32)


def _dot_tn(a, b):
    return lax.dot_general(a, b, (((0,), (0,)), ((), ())), preferred_element_type=F32)


def _sigmoid(x):
    return jax.nn.sigmoid(x)


def _ln_stats(r):
    mu = jnp.mean(r, axis=1, keepdims=True)
    xc = r - mu
    var = jnp.mean(xc * xc, axis=1, keepdims=True)
    rstd = lax.rsqrt(var + LN_EPS)
    return xc * rstd, rstd


def _ln_bwd(dy, xhat, rstd, g):
    dxh = dy * g
    m1 = jnp.mean(dxh, axis=1, keepdims=True)
    m2 = jnp.mean(dxh * xhat, axis=1, keepdims=True)
    return rstd * (dxh - m1 - xhat * m2)


def _colsum(v):
    return jnp.sum(v, axis=0, keepdims=True)


def _gelu(z):
    return 0.5 * z * (1.0 + lax.erf(z * (1.0 / math.sqrt(2.0))))


def _gelu_grad(z):
    cdf = 0.5 * (1.0 + lax.erf(z * (1.0 / math.sqrt(2.0))))
    pdf = jnp.exp(-0.5 * z * z) * (1.0 / math.sqrt(2.0 * math.pi))
    return cdf + z * pdf


def _shift_down(v, k, prev_rows):
    rolled = pltpu.roll(v, k, 0)
    head = rolled[0:8]
    rows = lax.broadcasted_iota(jnp.int32, head.shape, 0)
    for r in range(k):
        head = jnp.where(rows == r, prev_rows[k - 1 - r], head)
    return jnp.concatenate([head, rolled[8:]], axis=0)


def _shift_up(v, k, next_rows):
    tm = v.shape[0]
    rolled = pltpu.roll(v, tm - k, 0)
    tail = rolled[tm - 8:tm]
    rows = lax.broadcasted_iota(jnp.int32, tail.shape, 0)
    for r in range(k):
        tail = jnp.where(rows == 8 - k + r, next_rows[r], tail)
    return jnp.concatenate([rolled[0:tm - 8], tail], axis=0)


def _fill_shifted(base_scr, sh_scr):
    nrows = sh_scr.shape[1]
    for r in range(1, 8):
        sh_scr[r - 1, :, :] = base_scr[pl.ds(r, nrows), :]


def _tap(base_scr, sh_scr, off, r0, nrows, cols):
    q, r = divmod(off, 8)
    if r == 0:
        return base_scr[pl.ds(r0 + 8 * q, nrows), cols]
    return sh_scr[r - 1, pl.ds(r0 + 8 * q, nrows), cols]


def _pick_rows(r, c, itemsize, cap_bytes):
    best = None
    for t in range(16, r + 1, 16):
        if r % t == 0 and t * c * itemsize <= cap_bytes:
            best = t
    return best if best is not None else r


def _ffn_conv_cols(h, dw_ref, c0, cw, prev1, prev2):
    kw = dw_ref.shape[0]
    h1 = _shift_down(h, 1, [prev1])
    h2 = _shift_down(h, 2, [prev1, prev2])
    hc = dw_ref[kw - 1:kw, c0:c0 + cw] * h + dw_ref[kw - 2:kw - 1, c0:c0 + cw] * h1 + dw_ref[kw - 3:kw - 2, c0:c0 + cw] * h2
    return hc, h1, h2


def _ffn_fwd(xh1, g1, b1, wup, wdn, lw, fdw, l, alpha, name, hook=None):
    s, d = xh1.shape
    f2 = wup.shape[2]
    f = f2 // 2
    tm = min(TM_FFN, s)
    cw = min(CW_FFN, f)
    n, nck = s // tm, f // cw
    assert fdw.shape[1] == 3 and s % tm == 0 and f % cw == 0

    def body(xh_ref, g_ref, b_ref, wup_ref, dw_ref, wdn_ref, xo_ref, rs_ref, hs_ref, hcs_ref, carry):
        @pl.when(pl.program_id(0) == 0)
        def _():
            carry[...] = jnp.zeros_like(carry)

        x1 = xh_ref[...] * g_ref[...] + b_ref[...]
        xb = x1.astype(BF16)
        o = jnp.zeros((tm, d), F32)

        def up_proj(j):
            return [_dot(xb, wup_ref[:, half * f + j * cw:half * f + (j + 1) * cw]) for half in range(2)]

        ahead = up_proj(0)
        for j in range(nck):
            hh = ahead
            if j + 1 < nck:
                ahead = up_proj(j + 1)
            parts = []
            for half in range(2):
                c0 = half * f + j * cw
                h = hh[half]
                hs_ref[:, c0:c0 + cw] = h.astype(BF16)
                hc, _, _ = _ffn_conv_cols(h, dw_ref, c0, cw, carry[7:8, c0:c0 + cw], carry[6:7, c0:c0 + cw])
                carry[:, c0:c0 + cw] = h[tm - 8:tm, :]
                hcs_ref[:, c0:c0 + cw] = hc.astype(BF16)
                parts.append(hc)
            gg, vv = parts
            a = (gg * _sigmoid(gg) * vv).astype(BF16)
            o = o + _dot(a, wdn_ref[j * cw:(j + 1) * cw, :])
        xhat, rstd = _ln_stats(alpha * x1 + o)
        xo_ref[...] = xhat
        rs_ref[...] = rstd

    tile = pl.BlockSpec((tm, d), lambda i: (i, 0))
    return _hosted_call(
        body, hook, n,
        in_specs=[tile, _rowspec(d), _rowspec(d), _wspec(wup, lw),
                  pl.BlockSpec((None, 3, f2), lambda i: (l, 0, 0)), _wspec(wdn, lw)],
        out_specs=[tile, pl.BlockSpec((tm, 1), lambda i: (i, 0)), pl.BlockSpec((tm, f2), lambda i: (i, 0)),
                   pl.BlockSpec((tm, f2), lambda i: (i, 0))],
        out_shape=[jax.ShapeDtypeStruct((s, d), F32), jax.ShapeDtypeStruct((s, 1), F32),
                   jax.ShapeDtypeStruct((s, f2), BF16), jax.ShapeDtypeStruct((s, f2), BF16)],
        scratch_shapes=[pltpu.VMEM((8, f2), F32)],
        args=(xh1, g1, b1, wup, fdw, wdn), name=name)


def _ffn_bwd1(dx2, xh2, rstd2, g2, b2, hcs, wdn, lw, gwdn_buf, l, name, hook=None, loss_head=False):
    s, d = dx2.shape
    f2 = hcs.shape[1]
    f = f2 // 2
    tm = min(TM_FFN_BWD1, s)
    cw = min(CW_FFN, f)
    n, nck = s // tm, f // cw

    def body(dx_ref, xh_ref, rs_ref, g_ref, b_ref, hcs_ref, wdn_ref, buf_ref,
             dr_ref, dhc_ref, gwdn_ref, gg_ref, gb_ref, loss_ref):
        @pl.when(pl.program_id(0) == 0)
        def _():
            gwdn_ref[...] = jnp.zeros_like(gwdn_ref)
            gg_ref[...] = jnp.zeros_like(gg_ref)
            gb_ref[...] = jnp.zeros_like(gb_ref)
            loss_ref[...] = jnp.zeros_like(loss_ref)

        xh = xh_ref[...]
        if loss_head:
            err = xh * g_ref[...] + b_ref[...] - dx_ref[...]
            dx = err * (1.0 / d)
            loss_ref[...] += (0.5 / d) * jnp.sum(_colsum(err * err), axis=1, keepdims=True)
        else:
            dx = dx_ref[...]
        gg_ref[...] += _colsum(dx * xh)
        gb_ref[...] += _colsum(dx)
        dr = _ln_bwd(dx, xh, rs_ref[...], g_ref[...])
        dr_ref[...] = dr
        dob = dr.astype(BF16)

        def d_act(j):
            return _dot_nt(dob, wdn_ref[j * cw:(j + 1) * cw, :])

        da_ahead = d_act(0)
        for j in range(nck):
            da = da_ahead
            if j + 1 < nck:
                da_ahead = d_act(j + 1)
            gt = hcs_ref[:, j * cw:(j + 1) * cw].astype(F32)
            vv = hcs_ref[:, f + j * cw:f + (j + 1) * cw].astype(F32)
            sg = _sigmoid(gt)
            sl = gt * sg
            a = (sl * vv).astype(BF16)
            gwdn_ref[j * cw:(j + 1) * cw, :] += _dot_tn(a, dob)
            dhc_ref[:, j * cw:(j + 1) * cw] = (da * vv * (sg * (1.0 + gt * (1.0 - sg)))).astype(BF16)
            dhc_ref[:, f + j * cw:f + (j + 1) * cw] = (da * sl).astype(BF16)

    tile = pl.BlockSpec((tm, d), lambda i: (i, 0))
    wide = pl.BlockSpec((tm, f2), lambda i: (i, 0))
    nl = gwdn_buf.shape[0]
    return _hosted_call(
        body, hook, n,
        in_specs=[tile, tile, pl.BlockSpec((tm, 1), lambda i: (i, 0)), _rowspec(d), _rowspec(d), wide,
                  _wspec(wdn, lw), pl.BlockSpec(memory_space=pl.ANY)],
        out_specs=[tile, wide, pl.BlockSpec((None, f, d), lambda i: (l, 0, 0)), _rowspec(d), _rowspec(d),
                   pl.BlockSpec((1, 1), lambda i: (0, 0))],
        out_shape=[jax.ShapeDtypeStruct((s, d), F32), jax.ShapeDtypeStruct((s, f2), BF16),
                   jax.ShapeDtypeStruct((nl, f, d), F32),
                   jax.ShapeDtypeStruct((1, d), F32), jax.ShapeDtypeStruct((1, d), F32),
                   jax.ShapeDtypeStruct((1, 1), F32)],
        scratch_shapes=[], args=(dx2, xh2, rstd2, g2, b2, hcs, wdn, gwdn_buf), name=name, aliases={7: 2})


def _bwd_in(dp, dres, w, l, alpha, name):
    s, d = dres.shape
    nn = dp.shape[1]
    tm = min(TM_BWD_IN, s)
    n = s // tm
    tile = pl.BlockSpec((tm, d), lambda i: (i, 0))

    def body(dp_ref, dres_ref, w_ref, o_ref):
        o_ref[...] = alpha * dres_ref[...] + _dot_nt(dp_ref[...], w_ref[...])

    return pl.pallas_call(
        body, grid=(n,),
        in_specs=[pl.BlockSpec((tm, nn), lambda i: (i, 0)), tile, _wspec(w, l)],
        out_specs=tile, out_shape=jax.ShapeDtypeStruct((s, d), F32),
        compiler_params=_cparams(parallel=True), name=name,
    )(dp, dres, w)


def _ffn_bwd2(dhc, hs, dres, wup, lw, fdw, l, xh, rstd, g, alpha, name):
    s, d = dres.shape
    f2 = dhc.shape[1]
    tm = min(TM_BWD_IN, s)
    n = s // tm
    hb = FFN_HALO
    cw = min(CW_FFN_BWD2, f2)
    nck = f2 // cw
    halo_blocks = tm // hb
    assert f2 % cw == 0 and fdw.shape[1] == 3

    def body(dhc_ref, halo_ref, hs_ref, dres_ref, w_ref, dw_ref, xh_ref, rs_ref, g_ref,
             o_ref, dh_ref, gdw_ref, gg_ref, gb_ref):
        i = pl.program_id(0)

        @pl.when(i == 0)
        def _():
            gdw_ref[...] = jnp.zeros_like(gdw_ref)
            gg_ref[...] = jnp.zeros_like(gg_ref)
            gb_ref[...] = jnp.zeros_like(gb_ref)

        has_next = i < n - 1
        dx = alpha * dres_ref[...]
        for j in range(nck):
            c0 = j * cw
            dc = dhc_ref[:, c0:c0 + cw].astype(F32)
            hal = jnp.where(has_next, halo_ref[:, c0:c0 + cw].astype(F32), 0.0)
            nxt = [hal[0:1], hal[1:2]]
            u1 = _shift_up(dc, 1, nxt[:1])
            u2 = _shift_up(dc, 2, nxt)
            h = hs_ref[:, c0:c0 + cw].astype(F32)
            gdw_ref[2:3, c0:c0 + cw] += _colsum(dc * h)
            gdw_ref[1:2, c0:c0 + cw] += _colsum(u1 * h)
            gdw_ref[0:1, c0:c0 + cw] += _colsum(u2 * h)
            dh = (dw_ref[2:3, c0:c0 + cw] * dc + dw_ref[1:2, c0:c0 + cw] * u1
                  + dw_ref[0:1, c0:c0 + cw] * u2).astype(BF16)
            dh_ref[:, c0:c0 + cw] = dh
            dx = dx + _dot_nt(dh, w_ref[:, c0:c0 + cw])
        xhv = xh_ref[...]
        gg_ref[...] += _colsum(dx * xhv)
        gb_ref[...] += _colsum(dx)
        o_ref[...] = _ln_bwd(dx, xhv, rs_ref[...], g_ref[...])

    tile = pl.BlockSpec((tm, d), lambda i: (i, 0))
    wide = pl.BlockSpec((tm, f2), lambda i: (i, 0))
    return pl.pallas_call(
        body, grid=(n,),
        in_specs=[wide, pl.BlockSpec((hb, f2), lambda i: (jnp.minimum((i + 1) * halo_blocks, s // hb - 1), 0)),
                  wide, tile, _wspec(wup, lw), pl.BlockSpec((None, 3, f2), lambda i: (l, 0, 0)), tile,
                  pl.BlockSpec((tm, 1), lambda i: (i, 0)), _rowspec(d)],
        out_specs=[tile, wide, pl.BlockSpec((3, f2), lambda i: (0, 0)), _rowspec(d), _rowspec(d)],
        out_shape=[jax.ShapeDtypeStruct((s, d), F32), jax.ShapeDtypeStruct((s, f2), BF16),
                   jax.ShapeDtypeStruct((3, f2), F32),
                   jax.ShapeDtypeStruct((1, d), F32), jax.ShapeDtypeStruct((1, d), F32)],
        compiler_params=_cparams(), name=name,
    )(dhc, dhc, hs, dres, wup, fdw, xh, rstd, g)


def _mm_tn(a, ga, ba, bm, buf, l, name):
    s, k = a.shape
    nn = bm.shape[1]
    ts = min(TS_MM_TN, s)
    tn = nn // N_CHIPS if nn > 1024 else nn
    nj, ns = nn // tn, s // ts

    def body(a_ref, g_ref, b_ref, bm_ref, buf_ref, o_ref):
        @pl.when(pl.program_id(1) == 0)
        def _():
            o_ref[...] = jnp.zeros_like(o_ref)

        ab = (a_ref[...] * g_ref[...] + b_ref[...]).astype(BF16)
        o_ref[...] += _dot_tn(ab, bm_ref[...])

    return pl.pallas_call(
        body, grid=(nj, ns),
        in_specs=[pl.BlockSpec((ts, k), lambda j, t: (t, 0)), _rowspec(k), _rowspec(k),
                  pl.BlockSpec((ts, tn), lambda j, t: (t, j)), pl.BlockSpec(memory_space=pl.ANY)],
        out_specs=pl.BlockSpec((None, k, tn), lambda j, t: (l, 0, j)),
        out_shape=jax.ShapeDtypeStruct(buf.shape, F32),
        input_output_aliases={4: 0},
        compiler_params=_cparams(2), name=name,
    )(a, ga, ba, bm, buf)


def _conv_fwd(xin, gin, bin_, win, wout, lw, adw, l, adwb, lng, lnb, alpha, name, hook=None):
    s, d = xin.shape
    kw = adw.shape[1]
    hb = CONV_HALO
    tm = min(TM_CONV, s)
    n = s // tm
    assert kw - 1 <= hb <= tm

    def body(x_ref, g_ref, b_ref, win_ref, dw_ref, dwb_ref, lng_ref, lnb_ref, wout_ref,
             xo_ref, rs_ref, p_ref, chat_ref, rsc_ref, u_scr, u8_scr):
        @pl.when(pl.program_id(0) == 0)
        def _():
            u_scr[0:hb, :] = jnp.zeros((hb, d), F32)

        x = x_ref[...] * g_ref[...] + b_ref[...]
        pm = _dot(x.astype(BF16), win_ref[...])
        p_ref[...] = pm.astype(BF16)
        u = pm[:, :d] * _sigmoid(pm[:, d:])
        u_scr[hb:hb + tm, :] = u
        _fill_shifted(u_scr, u8_scr)
        acc = dwb_ref[...] + dw_ref[kw - 1:kw, :] * u
        for k in range(kw - 1):
            acc = acc + dw_ref[k:k + 1, :] * _tap(u_scr, u8_scr, hb - (kw - 1) + k, 0, tm, slice(None))
        u_scr[0:hb, :] = u_scr[tm:tm + hb, :]
        chat, rstdc = _ln_stats(acc)
        chat_ref[...] = chat.astype(BF16)
        rsc_ref[...] = rstdc
        nv = chat * lng_ref[...] + lnb_ref[...]
        sv = (nv * _sigmoid(nv)).astype(BF16)
        xhat, rstd = _ln_stats(alpha * x + _dot(sv, wout_ref[...]))
        xo_ref[...] = xhat
        rs_ref[...] = rstd

    tile = pl.BlockSpec((tm, d), lambda i: (i, 0))
    col = pl.BlockSpec((tm, 1), lambda i: (i, 0))
    return _hosted_call(
        body, hook, n,
        in_specs=[tile, _rowspec(d), _rowspec(d), _wspec(win, lw),
                  pl.BlockSpec((None, kw, d), lambda i: (l, 0, 0)), _rowspec(d), _rowspec(d), _rowspec(d),
                  _wspec(wout, lw)],
        out_specs=[tile, col, pl.BlockSpec((tm, 2 * d), lambda i: (i, 0)), tile, col],
        out_shape=[jax.ShapeDtypeStruct((s, d), F32), jax.ShapeDtypeStruct((s, 1), F32),
                   jax.ShapeDtypeStruct((s, 2 * d), BF16), jax.ShapeDtypeStruct((s, d), BF16),
                   jax.ShapeDtypeStruct((s, 1), F32)],
        scratch_shapes=[pltpu.VMEM((tm + hb, d), F32), pltpu.VMEM((7, tm + hb - 8, d), F32)],
        args=(xin, gin, bin_, win, adw, adwb, lng, lnb, wout), name=name)


def _conv_bwd1(dr1, chat, rstdc, p, wout, lw, adw, lt, lng, lnb, gwout_buf, l, name, hook=None):
    s, d = dr1.shape
    kw = adw.shape[1]
    hb = CONV_HALO
    tm = min(TM_CONV, s)
    n = s // tm
    halo_blocks = tm // hb
    rbl = CONV_ROW_BLOCK

    def body(dr_ref, chat_ref, rsc_ref, p_ref, halo_ref, wout_ref, dw_ref, lng_ref, lnb_ref, buf_ref,
             dp_ref, gwout_ref, gdw_ref, gdwb_ref, glng_ref, glnb_ref, u_scr, dc_scr, u8_scr, dc8_scr):
        i = pl.program_id(0)
        t = n - 1 - i

        @pl.when(i == 0)
        def _():
            dc_scr[tm:tm + hb, :] = jnp.zeros((hb, d), F32)
            gwout_ref[...] = jnp.zeros_like(gwout_ref)
            gdw_ref[...] = jnp.zeros_like(gdw_ref)
            gdwb_ref[...] = jnp.zeros_like(gdwb_ref)
            glng_ref[...] = jnp.zeros_like(glng_ref)
            glnb_ref[...] = jnp.zeros_like(glnb_ref)

        dob = dr_ref[...].astype(BF16)
        chat = chat_ref[...].astype(F32)
        lng = lng_ref[...]
        nv = chat * lng + lnb_ref[...]
        sgn = _sigmoid(nv)
        gwout_ref[...] += _dot_tn((nv * sgn).astype(BF16), dob)
        dn = _dot_nt(dob, wout_ref[...]) * (sgn * (1.0 + nv * (1.0 - sgn)))
        glng_ref[...] += _colsum(dn * chat)
        glnb_ref[...] += _colsum(dn)
        dc = _ln_bwd(dn, chat, rsc_ref[...], lng)
        gdwb_ref[...] += _colsum(dc)

        pm = p_ref[...].astype(F32)
        a = pm[:, :d]
        sg = _sigmoid(pm[:, d:])
        ph = halo_ref[...].astype(F32)
        u_scr[0:hb, :] = jnp.where(t > 0, ph[:, :d] * _sigmoid(ph[:, d:]), 0.0)
        u_scr[hb:hb + tm, :] = a * sg
        dc_scr[0:tm, :] = dc
        _fill_shifted(u_scr, u8_scr)
        _fill_shifted(dc_scr, dc8_scr)
        du = dw_ref[kw - 1:kw, :] * dc
        for k in range(kw - 1):
            du = du + dw_ref[k:k + 1, :] * _tap(dc_scr, dc8_scr, kw - 1 - k, 0, tm, slice(None))
        for cb in range(d // LANES):
            cols = pl.ds(cb * LANES, LANES)

            def rows_step(rb, accs, cols=cols):
                r0 = pl.multiple_of(rb * rbl, rbl)
                dcb = dc_scr[pl.ds(r0, rbl), cols]
                out = []
                for k in range(kw):
                    prod = dcb * _tap(u_scr, u8_scr, hb - (kw - 1) + k, r0, rbl, cols)
                    part = prod[0:8]
                    for g8 in range(1, rbl // 8):
                        part = part + prod[8 * g8:8 * g8 + 8]
                    out.append(accs[k] + part)
                return tuple(out)

            accs = lax.fori_loop(0, tm // rbl, rows_step, tuple(jnp.zeros((8, LANES), F32) for _ in range(kw)))
            for k in range(kw):
                gdw_ref[k:k + 1, cols] += _colsum(accs[k])
        dc_scr[tm:tm + hb, :] = dc[0:hb, :]
        dp_ref[:, :d] = (du * sg).astype(BF16)
        dp_ref[:, d:] = (du * a * sg * (1.0 - sg)).astype(BF16)

    tile = pl.BlockSpec((tm, d), lambda i: (n - 1 - i, 0))
    col = pl.BlockSpec((tm, 1), lambda i: (n - 1 - i, 0))
    nl = gwout_buf.shape[0]
    return _hosted_call(
        body, hook, n,
        in_specs=[tile, tile, col, pl.BlockSpec((tm, 2 * d), lambda i: (n - 1 - i, 0)),
                  pl.BlockSpec((hb, 2 * d), lambda i: (jnp.maximum((n - 1 - i) * halo_blocks - 1, 0), 0)),
                  _wspec(wout, lw), pl.BlockSpec((None, kw, d), lambda i: (lt, 0, 0)), _rowspec(d), _rowspec(d),
                  pl.BlockSpec(memory_space=pl.ANY)],
        out_specs=[pl.BlockSpec((tm, 2 * d), lambda i: (n - 1 - i, 0)),
                   pl.BlockSpec((None, d, d), lambda i: (l, 0, 0)),
                   pl.BlockSpec((kw, d), lambda i: (0, 0)), _rowspec(d), _rowspec(d), _rowspec(d)],
        out_shape=[jax.ShapeDtypeStruct((s, 2 * d), BF16), jax.ShapeDtypeStruct((nl, d, d), F32),
                   jax.ShapeDtypeStruct((kw, d), F32), jax.ShapeDtypeStruct((1, d), F32),
                   jax.ShapeDtypeStruct((1, d), F32), jax.ShapeDtypeStruct((1, d), F32)],
        scratch_shapes=[pltpu.VMEM((tm + hb, d), F32), pltpu.VMEM((tm + hb, d), F32),
                        pltpu.VMEM((7, tm + hb - 8, d), F32), pltpu.VMEM((7, tm + hb - 8, d), F32)],
        args=(dr1, chat, rstdc, p, p, wout, adw, lng, lnb, gwout_buf), name=name, aliases={9: 1})


def _sgu_gate(vn, wm_ref, bs_ref, s_scr, tm, nh):
    for ch in range(tm // CHUNK):
        r0 = ch * CHUNK
        for h in range(nh):
            c0 = h * CHUNK
            s_scr[r0:r0 + CHUNK, c0:c0 + CHUNK] = (
                _dot(wm_ref[h], vn[r0:r0 + CHUNK, c0:c0 + CHUNK]) + bs_ref[:, c0:c0 + CHUNK])


def _sgu_fwd(xin, gin, bin_, win, lg, lb, wm, bs_exp, wout, alpha, name):
    s, d = xin.shape
    nh = wm.shape[0]
    tm = min(TM_SGU, s)
    n = s // tm
    assert tm % CHUNK == 0 and nh * CHUNK == d

    def body(x_ref, g_ref, b_ref, win_ref, lg_ref, lb_ref, wm_ref, bs_ref, wout_ref,
             xo_ref, rs_ref, zp_ref, s_scr):
        x = x_ref[...] * g_ref[...] + b_ref[...]
        zp = _dot(x.astype(BF16), win_ref[...])
        zp_ref[...] = zp.astype(BF16)
        z = _gelu(zp)
        vhat, _ = _ln_stats(z[:, d:])
        vn = (vhat * lg_ref[...] + lb_ref[...]).astype(BF16)
        _sgu_gate(vn, wm_ref, bs_ref, s_scr, tm, nh)
        q = (z[:, :d] * s_scr[...]).astype(BF16)
        xhat, rstd = _ln_stats(alpha * x + _dot(q, wout_ref[...]))
        xo_ref[...] = xhat
        rs_ref[...] = rstd

    tile = pl.BlockSpec((tm, d), lambda i: (i, 0))
    return pl.pallas_call(
        body, grid=(n,),
        in_specs=[tile, _rowspec(d), _rowspec(d), _wspec(win, 0), _rowspec(d), _rowspec(d),
                  _resident((nh, CHUNK, CHUNK), lambda i: (0, 0, 0)),
                  _resident((CHUNK, d), lambda i: (0, 0)), _wspec(wout, 0)],
        out_specs=[tile, pl.BlockSpec((tm, 1), lambda i: (i, 0)), pl.BlockSpec((tm, 2 * d), lambda i: (i, 0))],
        out_shape=[jax.ShapeDtypeStruct((s, d), F32), jax.ShapeDtypeStruct((s, 1), F32),
                   jax.ShapeDtypeStruct((s, 2 * d), BF16)],
        scratch_shapes=[pltpu.VMEM((tm, d), F32)],
        compiler_params=_cparams(parallel=True), name=name,
    )(xin, gin, bin_, win, lg, lb, wm, bs_exp, wout)


def _sgu_bwd1(dr1, zp, wout, lg, lb, wm, wmt, bs_exp, gwout_buf, name):
    s, d = dr1.shape
    nh = wm.shape[0]
    tm = min(TM_SGU, s)
    n = s // tm

    def body(dr_ref, zp_ref, wout_ref, lg_ref, lb_ref, wm_ref, wmt_ref, bs_ref, buf_ref,
             dzp_ref, gwout_ref, gws_ref, gbs_ref, glg_ref, glb_ref, s_scr, dvn_scr, bs_acc):
        i = pl.program_id(0)

        @pl.when(i == 0)
        def _():
            gwout_ref[...] = jnp.zeros_like(gwout_ref)
            gws_ref[...] = jnp.zeros_like(gws_ref)
            glg_ref[...] = jnp.zeros_like(glg_ref)
            glb_ref[...] = jnp.zeros_like(glb_ref)
            bs_acc[...] = jnp.zeros_like(bs_acc)

        dob = dr_ref[...].astype(BF16)
        zp = zp_ref[...].astype(F32)
        z = _gelu(zp)
        u = z[:, :d]
        lg = lg_ref[...]
        vhat, rstdv = _ln_stats(z[:, d:])
        vn = (vhat * lg + lb_ref[...]).astype(BF16)
        _sgu_gate(vn, wm_ref, bs_ref, s_scr, tm, nh)
        sv = s_scr[...]
        gwout_ref[...] += _dot_tn((u * sv).astype(BF16), dob)
        dq = _dot_nt(dob, wout_ref[...])
        ds = dq * u
        dsb = ds.astype(BF16)
        part = jnp.zeros((CHUNK, d), F32)
        for ch in range(tm // CHUNK):
            r0 = ch * CHUNK
            part = part + ds[r0:r0 + CHUNK, :]
            for h in range(nh):
                c0 = h * CHUNK
                blk = dsb[r0:r0 + CHUNK, c0:c0 + CHUNK]
                gws_ref[h] += _dot_nt(blk, vn[r0:r0 + CHUNK, c0:c0 + CHUNK])
                dvn_scr[r0:r0 + CHUNK, c0:c0 + CHUNK] = _dot(wmt_ref[h], blk)
        bs_acc[...] += part
        dvn = dvn_scr[...]
        glg_ref[...] += _colsum(dvn * vhat)
        glb_ref[...] += _colsum(dvn)
        dv = _ln_bwd(dvn, vhat, rstdv, lg)
        gp = _gelu_grad(zp)
        dzp_ref[:, :d] = (dq * sv * gp[:, :d]).astype(BF16)
        dzp_ref[:, d:] = (dv * gp[:, d:]).astype(BF16)

        @pl.when(i == n - 1)
        def _():
            rows = lax.broadcasted_iota(jnp.int32, (CHUNK, CHUNK), 0)
            cols = lax.broadcasted_iota(jnp.int32, (CHUNK, CHUNK), 1)
            tril = (cols <= rows).astype(F32)
            acc = bs_acc[...]
            for h in range(nh):
                gws_ref[h] = gws_ref[h] * tril
                gbs_ref[:, h:h + 1] = jnp.sum(acc[:, h * CHUNK:(h + 1) * CHUNK], axis=1, keepdims=True)

    tile = pl.BlockSpec((tm, d), lambda i: (i, 0))
    wide = pl.BlockSpec((tm, 2 * d), lambda i: (i, 0))
    hspec = _resident((nh, CHUNK, CHUNK), lambda i: (0, 0, 0))
    return pl.pallas_call(
        body, grid=(n,),
        in_specs=[tile, wide, _wspec(wout, 0), _rowspec(d), _rowspec(d), hspec, hspec,
                  _resident((CHUNK, d), lambda i: (0, 0)), pl.BlockSpec(memory_space=pl.ANY)],
        out_specs=[wide, pl.BlockSpec((None, d, d), lambda i: (0, 0, 0)),
                   pl.BlockSpec((nh, CHUNK, CHUNK), lambda i: (0, 0, 0)),
                   pl.BlockSpec((CHUNK, nh), lambda i: (0, 0)), _rowspec(d), _rowspec(d)],
        out_shape=[jax.ShapeDtypeStruct((s, 2 * d), BF16), jax.ShapeDtypeStruct(gwout_buf.shape, F32),
                   jax.ShapeDtypeStruct((nh, CHUNK, CHUNK), F32), jax.ShapeDtypeStruct((CHUNK, nh), F32),
                   jax.ShapeDtypeStruct((1, d), F32), jax.ShapeDtypeStruct((1, d), F32)],
        scratch_shapes=[pltpu.VMEM((tm, d), F32), pltpu.VMEM((tm, d), F32), pltpu.VMEM((CHUNK, d), F32)],
        input_output_aliases={8: 1},
        compiler_params=_cparams(), name=name,
    )(dr1, zp, wout, lg, lb, wm, wmt, bs_exp, gwout_buf)


def _pool_counts(t0, tm, w):
    pos = t0 + lax.broadcasted_iota(jnp.int32, (tm, 1), 0)
    return jnp.minimum(pos + 1, w).astype(F32)


def _pool_fwd(xin, gin, bin_, win, wg, scale, wout, alpha, name):
    s, d = xin.shape
    ng, dg = wg.shape[0], wg.shape[1]
    hb = POOL_HALO
    tm = min(TM_POOL, s)
    n = s // tm
    assert ng == len(POOL_WINDOWS) and ng * dg == d and max(POOL_WINDOWS) <= hb

    def body(x_ref, g_ref, b_ref, win_ref, wg_ref, sc_ref, wout_ref, xo_ref, rs_ref, ys_ref, y_scr, z_scr):
        i = pl.program_id(0)

        @pl.when(i == 0)
        def _():
            y_scr[0:hb, :] = jnp.zeros((hb, d), F32)

        x = x_ref[...] * g_ref[...] + b_ref[...]
        y = _dot(x.astype(BF16), win_ref[...])
        ys_ref[...] = y.astype(BF16)
        y_scr[hb:hb + tm, :] = y
        for g, w in enumerate(POOL_WINDOWS):
            c0 = g * dg
            acc = y[:, c0:c0 + dg]
            for dd in range(1, w):
                acc = acc + y_scr[pl.ds(hb - dd, tm), c0:c0 + dg]
            pg = acc / _pool_counts(i * tm, tm, w) - y[:, c0:c0 + dg]
            z_scr[:, c0:c0 + dg] = _dot(pg.astype(BF16), wg_ref[g])
        y_scr[0:hb, :] = y_scr[tm:tm + hb, :]
        zz = (z_scr[...] * sc_ref[...]).astype(BF16)
        xhat, rstd = _ln_stats(alpha * x + _dot(zz, wout_ref[...]))
        xo_ref[...] = xhat
        rs_ref[...] = rstd

    tile = pl.BlockSpec((tm, d), lambda i: (i, 0))
    return pl.pallas_call(
        body, grid=(n,),
        in_specs=[tile, _rowspec(d), _rowspec(d), _wspec(win, 0),
                  _resident((ng, dg, dg), lambda i: (0, 0, 0)), _rowspec(d), _wspec(wout, 0)],
        out_specs=[tile, pl.BlockSpec((tm, 1), lambda i: (i, 0)), tile],
        out_shape=[jax.ShapeDtypeStruct((s, d), F32), jax.ShapeDtypeStruct((s, 1), F32),
                   jax.ShapeDtypeStruct((s, d), BF16)],
        scratch_shapes=[pltpu.VMEM((tm + hb, d), F32), pltpu.VMEM((tm, d), F32)],
        compiler_params=_cparams(), name=name,
    )(xin, gin, bin_, win, wg, scale, wout)


def _pool_bwd1(dr1, ys, wout, wg, scale, gwout_buf, name):
    s, d = dr1.shape
    ng, dg = wg.shape[0], wg.shape[1]
    hb = POOL_HALO
    tm = min(TM_POOL, s)
    n = s // tm
    halo_blocks = tm // hb

    def body(dr_ref, ys_ref, halo_ref, wout_ref, wg_ref, sc_ref, buf_ref,
             dy_ref, gwout_ref, gwg_ref, gsc_ref, y_scr, e_scr, z_scr, dp_scr):
        i = pl.program_id(0)
        t = n - 1 - i

        @pl.when(i == 0)
        def _():
            e_scr[tm:tm + hb, :] = jnp.zeros((hb, d), F32)
            gwout_ref[...] = jnp.zeros_like(gwout_ref)
            gwg_ref[...] = jnp.zeros_like(gwg_ref)
            gsc_ref[...] = jnp.zeros_like(gsc_ref)

        dob = dr_ref[...].astype(BF16)
        y = ys_ref[...].astype(F32)
        y_scr[0:hb, :] = jnp.where(t > 0, halo_ref[...].astype(F32), 0.0)
        y_scr[hb:hb + tm, :] = y
        pgs = []
        for g, w in enumerate(POOL_WINDOWS):
            c0 = g * dg
            acc = y[:, c0:c0 + dg]
            for dd in range(1, w):
                acc = acc + y_scr[pl.ds(hb - dd, tm), c0:c0 + dg]
            pg = (acc / _pool_counts(t * tm, tm, w) - y[:, c0:c0 + dg]).astype(BF16)
            pgs.append(pg)
            z_scr[:, c0:c0 + dg] = _dot(pg, wg_ref[g])
        zpre = z_scr[...]
        sc = sc_ref[...]
        gwout_ref[...] += _dot_tn((zpre * sc).astype(BF16), dob)
        dz = _dot_nt(dob, wout_ref[...])
        gsc_ref[...] += _colsum(dz * zpre)
        dzpre = (dz * sc).astype(BF16)
        for g, w in enumerate(POOL_WINDOWS):
            c0 = g * dg
            dzg = dzpre[:, c0:c0 + dg]
            gwg_ref[g] += _dot_tn(pgs[g], dzg)
            dp = _dot_nt(dzg, wg_ref[g])
            dp_scr[:, c0:c0 + dg] = dp
            e_scr[0:tm, c0:c0 + dg] = dp / _pool_counts(t * tm, tm, w)
        for g, w in enumerate(POOL_WINDOWS):
            c0 = g * dg
            acc = e_scr[0:tm, c0:c0 + dg]
            for dd in range(1, w):
                acc = acc + e_scr[pl.ds(dd, tm), c0:c0 + dg]
            dy_ref[:, c0:c0 + dg] = (acc - dp_scr[:, c0:c0 + dg]).astype(BF16)
        e_scr[tm:tm + hb, :] = e_scr[0:hb, :]

    tile = pl.BlockSpec((tm, d), lambda i: (n - 1 - i, 0))
    return pl.pallas_call(
        body, grid=(n,),
        in_specs=[tile, tile,
                  pl.BlockSpec((hb, d), lambda i: (jnp.maximum((n - 1 - i) * halo_blocks - 1, 0), 0)),
                  _wspec(wout, 0), _resident((ng, dg, dg), lambda i: (0, 0, 0)), _rowspec(d),
                  pl.BlockSpec(memory_space=pl.ANY)],
        out_specs=[tile, pl.BlockSpec((None, d, d), lambda i: (0, 0, 0)),
                   pl.BlockSpec((ng, dg, dg), lambda i: (0, 0, 0)), _rowspec(d)],
        out_shape=[jax.ShapeDtypeStruct((s, d), BF16), jax.ShapeDtypeStruct(gwout_buf.shape, F32),
                   jax.ShapeDtypeStruct((ng, dg, dg), F32), jax.ShapeDtypeStruct((1, d), F32)],
        scratch_shapes=[pltpu.VMEM((tm + hb, d), F32), pltpu.VMEM((tm + hb, d), F32),
                        pltpu.VMEM((tm, d), F32), pltpu.VMEM((tm, d), F32)],
        input_output_aliases={6: 1},
        compiler_params=_cparams(), name=name,
    )(dr1, ys, ys, wout, wg, scale, gwout_buf)


def _elementwise(fn, ins, out_dtypes, name):
    shape = ins[0].shape
    c = shape[-1]
    r = math.prod(shape[:-1])
    tr = _pick_rows(r, c, 4, 1 << 20)

    def body(*refs):
        vals = fn(*[ref[...] for ref in refs[:len(ins)]])
        for ref, v in zip(refs[len(ins):], vals):
            ref[...] = v.astype(ref.dtype)

    spec = pl.BlockSpec((tr, c), lambda i: (i, 0))
    outs = pl.pallas_call(
        body, grid=(r // tr,),
        in_specs=[spec] * len(ins), out_specs=[spec] * len(out_dtypes),
        out_shape=[jax.ShapeDtypeStruct((r, c), dt) for dt in out_dtypes],
        compiler_params=_cparams(parallel=True), name=name,
    )(*[a.reshape(r, c) for a in ins])
    return [o.reshape(shape) for o in outs]


def _prefetch_call(body, grid, in_specs, out_specs, out_shape, name, aliases=None):
    return pl.pallas_call(
        body,
        grid_spec=pltpu.PrefetchScalarGridSpec(num_scalar_prefetch=1, grid=grid, in_specs=in_specs, out_specs=out_specs),
        out_shape=out_shape, input_output_aliases=aliases or {},
        compiler_params=_cparams(len(grid), parallel=True), name=name)


def _cast_into_full(w3, l0, l, kind, chip1, name):
    _, r, c = w3.shape
    tr = _pick_rows(r, c, 4, 1 << 20)

    def body(k_ref, w_ref, o_ref):
        o_ref[...] = w_ref[...].astype(BF16)

    if kind == "row":
        out_spec = pl.BlockSpec((None, None, tr, c), lambda a, j, k: (a, k[0], j, 0))
    else:
        out_spec = pl.BlockSpec((None, tr, c), lambda a, j, k: (a, j, k[0]))
    return _prefetch_call(
        body, (l, r // tr), [pl.BlockSpec((None, tr, c), lambda a, j, k: (a + l0, j, 0))], out_spec,
        jax.ShapeDtypeStruct(_full_shape(kind, (l, r, c)), BF16), name)(chip1, w3)


def _pair_sum(g, got, kind, core1, name):
    if kind == "row":
        l, nc, sr, c = g.shape
        g5, got3 = g.reshape(l * nc, 2, sr // 2, c), got.reshape(l * nc, sr // 2, c)
    else:
        l, r, c = g.shape
        g5, got3 = g.reshape(l, 2, r // 2, c), got
    a, _, hr, c = g5.shape
    tr = _pick_rows(hr, c, 4, 1 << 20)

    def body(c_ref, g_ref, t_ref, o_ref):
        o_ref[...] = (g_ref[...] + t_ref[...]).astype(BF16)

    half = pl.BlockSpec((None, tr, c), lambda i, j, cc: (i, j, 0))
    out = _prefetch_call(
        body, (a, hr // tr), [pl.BlockSpec((None, None, tr, c), lambda i, j, cc: (i, cc[0], j, 0)), half], half,
        jax.ShapeDtypeStruct(got3.shape, BF16), name)(core1, g5, got3)
    return out.reshape(got.shape)


def _chip_sum(t, rb, kind, chip1, name):
    _, l, hr, sc = rb.shape
    tr = _pick_rows(hr, sc, 4, 1 << 19)

    def body(k_ref, t_ref, rb_ref, o_ref):
        acc = t_ref[...].astype(F32)
        for r in range(N_CHIPS - 1):
            acc = acc + rb_ref[r].astype(F32)
        o_ref[...] = acc

    if kind == "row":
        t_spec = pl.BlockSpec((None, None, tr, sc), lambda a, j, k: (a, k[0], j, 0))
    else:
        t_spec = pl.BlockSpec((None, tr, sc), lambda a, j, k: (a, j, k[0]))
    return _prefetch_call(
        body, (l, hr // tr),
        [t_spec, pl.BlockSpec((N_CHIPS - 1, None, tr, sc), lambda a, j, k: (0, a, j, 0))],
        pl.BlockSpec((None, tr, sc), lambda a, j, k: (a, j, 0)),
        jax.ShapeDtypeStruct((l, hr, sc), F32), name)(chip1, t, rb)


def _adamw_math(w_, g_, m_, v_):
    m2 = ADAM_B1 * m_ + (1.0 - ADAM_B1) * g_
    v2 = ADAM_B2 * v_ + (1.0 - ADAM_B2) * (g_ * g_)
    m_hat = m2 / (1.0 - ADAM_B1 ** ADAM_STEP)
    v_hat = v2 / (1.0 - ADAM_B2 ** ADAM_STEP)
    delta = -ADAM_LR * (m_hat / (jnp.sqrt(v_hat) + ADAM_EPS) + ADAM_WD * w_)
    return delta, m2, v2


def _adamw_big(w, m, v, own, other, core1, name, l0=0, into=None):
    lg, hr, c = own.shape
    lw = w.shape[0]
    view = lambda a: a.reshape(lw, 2, hr, c)
    tr = _pick_rows(hr, c, 4, 1 << 20)
    n_keep = 0 if into is None else len(into)

    def body(c_ref, w_ref, m_ref, v_ref, own_ref, oth_ref, *rest):
        g_ref, d_ref, m2_ref, v2_ref = rest[n_keep:]
        g = jnp.where(pl.program_id(1) == c_ref[0], own_ref[...], oth_ref[...])
        g_ref[...] = g
        d_ref[...], m2_ref[...], v2_ref[...] = _adamw_math(w_ref[...], g, m_ref[...], v_ref[...])

    s4 = pl.BlockSpec((None, None, tr, c), lambda a, h, j, cc: (a + l0, h, j, 0))
    s3 = pl.BlockSpec((None, tr, c), lambda a, h, j, cc: (a, j, 0))
    aliases = {6 + k: k for k in range(n_keep)}
    return _prefetch_call(
        body, (lg, 2, hr // tr), [s4, s4, s4, s3, s3] + [ANY] * n_keep, [s4] * 4,
        [jax.ShapeDtypeStruct((lw, 2, hr, c), F32)] * 4, name, aliases,
    )(core1, view(w), view(m), view(v), own, other, *(into or []))


def _sum_devices(own, gathered, me1, name):
    r, c = own.shape
    tr = _pick_rows(r, c, 4, 1 << 17)

    def body(me_ref, own_ref, g_ref, o_ref):
        acc = None
        for k in range(N_DEV):
            v = jnp.where(me_ref[0] == k, own_ref[...], g_ref[k])
            acc = v if acc is None else acc + v
        o_ref[...] = acc

    return _prefetch_call(
        body, (r // tr,),
        [pl.BlockSpec((tr, c), lambda i, m: (i, 0)), pl.BlockSpec((N_DEV, tr, c), lambda i, m: (0, i, 0))],
        pl.BlockSpec((tr, c), lambda i, m: (i, 0)), jax.ShapeDtypeStruct((r, c), F32), name)(me1, own, gathered)


def _adamw(w, g, m, v, name):
    return _elementwise(_adamw_math, [w, g, m, v], [F32, F32, F32], name)


ANY = pl.BlockSpec(memory_space=pl.ANY)


def _mesh_pos():
    return lax.axis_index("x"), lax.axis_index("y"), lax.axis_index("c")


def _chip_peers(x, y, c):
    out = []
    for r in (1, 2, 3):
        px = 1 - x if r & 2 else x
        py = 1 - y if r & 1 else y
        out.append((2 * px + py, (px, py, c)))
    return out


def _full_shape(kind, shard_shape):
    l, r, c = shard_shape
    return (l, N_CHIPS, r, c) if kind == "row" else (l, r, N_CHIPS * c)


def _full_piece(ref, kind, k, h, hr, sc):
    rows = pl.ds(pl.multiple_of(h * hr, SUBLANES_BF16), hr)
    if kind == "row":
        return ref.at[:, k, rows, :]
    return ref.at[:, rows, pl.ds(pl.multiple_of(k * sc, LANES), sc)]


def _remote(src, dst, ssem, rsem, dev):
    return pltpu.make_async_remote_copy(src_ref=src, dst_ref=dst, send_sem=ssem, recv_sem=rsem,
                                        device_id=dev, device_id_type=MESH)


DMA_CHUNK_BYTES = 1 << 20
DMA_MAX_CHUNKS = 32


def _chunk_views(src, dst):
    axis = len(src.shape) - 2
    rows = src.shape[axis]
    nbytes = math.prod(src.shape) * jnp.dtype(src.dtype).itemsize
    n = max(1, min(DMA_MAX_CHUNKS, nbytes // DMA_CHUNK_BYTES))
    while n > 1 and (rows % n or (rows // n) % SUBLANES_BF16):
        n -= 1
    cr = rows // n
    out = []
    for i in range(n):
        idx = (slice(None),) * axis + (pl.ds(i * cr, cr), slice(None))
        out.append((src.at[idx], dst.at[idx]))
    return out


def _start_remote(src, dst, ssem, rsem, dev):
    for s, t in _chunk_views(src, dst):
        _remote(s, t, ssem, rsem, dev).start()
    return _remote(src, dst, ssem, rsem, dev)


def _allgather_steps(fulls, kinds):
    nw = len(fulls)

    def dims(a, kind):
        return (a.shape[2] // 2, a.shape[3]) if kind == "row" else (a.shape[1] // 2, a.shape[2] // N_CHIPS)

    hrs = [dims(a, k)[0] for a, k in zip(fulls, kinds)]
    scs = [dims(a, k)[1] for a, k in zip(fulls, kinds)]

    def piece(ref, w, k, h):
        return _full_piece(ref, kinds[w], k, h, hrs[w], scs[w])

    def copies1(src, dst, sems, start):
        x, y, c = _mesh_pos()
        k_me = 2 * x + y
        out = []
        for w in range(nw):
            for r, (kj, dev) in enumerate(_chip_peers(x, y, c)):
                args = (sems[0].at[3 * w + r], sems[1].at[3 * w + r], dev)
                if start:
                    out.append(_start_remote(piece(src[w], w, k_me, c), piece(dst[w], w, k_me, c), *args))
                else:
                    out.append(_remote(piece(src[w], w, k_me, c), piece(dst[w], w, kj, c), *args))
        return out

    def copies2(src, dst, sems, start):
        x, y, c = _mesh_pos()
        out = []
        for w in range(nw):
            for r, (kj, _) in enumerate(_chip_peers(x, y, c)):
                args = (sems[0].at[3 * w + r], sems[1].at[3 * w + r], (x, y, 1 - c))
                if start:
                    out.append(_start_remote(piece(src[w], w, kj, c), piece(dst[w], w, kj, c), *args))
                else:
                    out.append(_remote(piece(src[w], w, kj, 1 - c), piece(dst[w], w, kj, 1 - c), *args))
        return out

    def finish(copies):
        def fn(src, dst, sems):
            for cp in copies(src, dst, sems, False):
                cp.wait_recv()
            for cp in copies(src, dst, sems, False):
                cp.wait_send()
        return fn

    step1 = (lambda s, d, m: copies1(s, d, m, True), finish(copies1))
    step2 = (lambda s, d, m: copies2(s, d, m, True), finish(copies2))
    return step1, step2


def _exchange_hook(arrays, step, sem_len, out_shapes=None):
    shapes = out_shapes or [jax.ShapeDtypeStruct(a.shape, a.dtype) for a in arrays]
    return dict(arrays=list(arrays), out_shapes=shapes, in_place=out_shapes is None, sem_len=sem_len,
                first=step[0], last=step[1])


def _combine_hooks(h1, h2):
    assert h1["in_place"] == h2["in_place"]
    n1, o1, s1, s2 = len(h1["arrays"]), len(h1["out_shapes"]), h1["sem_len"], h2["sem_len"]

    def both(which):
        def fn(src, dst, sems):
            h1[which](src[:n1], dst[:o1], [s.at[pl.ds(0, s1)] for s in sems])
            h2[which](src[n1:], dst[o1:], [s.at[pl.ds(s1, s2)] for s in sems])
        return fn

    return dict(arrays=h1["arrays"] + h2["arrays"], out_shapes=h1["out_shapes"] + h2["out_shapes"],
                in_place=h1["in_place"], sem_len=s1 + s2, first=both("first"), last=both("last"))


def _exchange_call(hook, name):
    nh, nho = len(hook["arrays"]), len(hook["out_shapes"])

    def body(*refs):
        h_in, h_out, sems = refs[:nh], refs[nh:nh + nho], refs[nh + nho:]
        hook["first"](h_in, h_out, sems)
        hook["last"](h_in, h_out, sems)

    return pl.pallas_call(
        body, in_specs=[ANY] * nh, out_specs=[ANY] * nho, out_shape=hook["out_shapes"],
        scratch_shapes=[pltpu.SemaphoreType.DMA((hook["sem_len"],))] * 2,
        input_output_aliases={k: k for k in range(nh)} if hook["in_place"] else {}, name=name,
    )(*hook["arrays"])


def _allgather_weights(fulls, kinds):
    nw = len(fulls)
    step1, step2 = _allgather_steps(fulls, kinds)

    def body(*refs):
        mine, fu = refs[:nw], refs[nw:2 * nw]
        sems1, sems2 = refs[2 * nw:2 * nw + 2], refs[2 * nw + 2:]
        step1[0](mine, fu, sems1)
        step1[1](mine, fu, sems1)
        step2[0](fu, fu, sems2)
        step2[1](fu, fu, sems2)

    return pl.pallas_call(
        body,
        in_specs=[ANY] * nw, out_specs=[ANY] * nw,
        out_shape=[jax.ShapeDtypeStruct(a.shape, a.dtype) for a in fulls],
        scratch_shapes=[pltpu.SemaphoreType.DMA((3 * nw,))] * 4,
        input_output_aliases={w: w for w in range(nw)},
        name="allgather_weights",
    )(*fulls)


def _hosted(body, n_in, n_out, hook, n_steps):
    if hook is None:
        return body
    nh, nho = len(hook["arrays"]), len(hook["out_shapes"])

    def wrapped(*refs):
        ins, h_in = refs[:n_in], refs[n_in:n_in + nh]
        outs = refs[n_in + nh:n_in + nh + n_out]
        h_out = refs[n_in + nh + n_out:n_in + nh + n_out + nho]
        rest = refs[n_in + nh + n_out + nho:]
        scr, sems = rest[:-2], rest[-2:]
        i = pl.program_id(0)

        @pl.when(i == 0)
        def _():
            hook["first"](h_in, h_out, sems)

        body(*ins, *outs, *scr)

        @pl.when(i == n_steps - 1)
        def _():
            hook["last"](h_in, h_out, sems)

    return wrapped


def _hosted_call(body, hook, n_steps, in_specs, out_specs, out_shape, scratch_shapes, args, name, aliases=None):
    n_in, n_out = len(in_specs), len(out_specs)
    aliases = dict(aliases or {})
    if hook is not None:
        nh = len(hook["arrays"])
        in_specs = list(in_specs) + [ANY] * nh
        out_specs = list(out_specs) + [ANY] * len(hook["out_shapes"])
        out_shape = list(out_shape) + list(hook["out_shapes"])
        scratch_shapes = list(scratch_shapes) + [pltpu.SemaphoreType.DMA((hook["sem_len"],))] * 2
        if hook["in_place"]:
            aliases.update({n_in + k: n_out + k for k in range(nh)})
        args = list(args) + hook["arrays"]
    outs = pl.pallas_call(
        _hosted(body, n_in, n_out, hook, n_steps), grid=(n_steps,),
        in_specs=in_specs, out_specs=out_specs, out_shape=out_shape, scratch_shapes=scratch_shapes,
        input_output_aliases=aliases, compiler_params=_cparams(), name=name,
    )(*args)
    return outs[:n_out], outs[n_out:]


def _pair_exchange(copies):
    def finish(src, dst, sems):
        for cp in copies(src, dst, sems, False):
            cp.wait_recv()
        for cp in copies(src, dst, sems, False):
            cp.wait_send()
    return (lambda s, d, m: copies(s, d, m, True), finish)


def _rs_pair(fulls, kinds):
    nw = len(fulls)

    def half_all(ref, kind, h):
        if kind == "row":
            hr = ref.shape[2] // 2
            return ref.at[:, :, pl.ds(pl.multiple_of(h * hr, SUBLANES_BF16), hr), :]
        hr = ref.shape[1] // 2
        return ref.at[:, pl.ds(pl.multiple_of(h * hr, SUBLANES_BF16), hr), :]

    def half_shape(kind, shape):
        if kind == "row":
            return (shape[0], shape[1], shape[2] // 2, shape[3])
        return (shape[0], shape[1] // 2, shape[2])

    def copies(g, got, sems, start):
        x, y, c = _mesh_pos()
        make = _start_remote if start else _remote
        return [make(half_all(g[w], kinds[w], 1 - c), got[w], sems[0].at[w], sems[1].at[w], (x, y, 1 - c))
                for w in range(nw)]

    shapes = [jax.ShapeDtypeStruct(half_shape(k, a.shape), a.dtype) for k, a in zip(kinds, fulls)]
    return _exchange_hook(fulls, _pair_exchange(copies), nw, shapes)


def _rs_chips(parts, kinds):
    nw = len(parts)

    def slot(ref, kind, k):
        if kind == "row":
            return ref.at[:, k]
        sc = ref.shape[2] // N_CHIPS
        return ref.at[:, :, pl.ds(pl.multiple_of(k * sc, LANES), sc)]

    def slot_shape(kind, shape):
        if kind == "row":
            return (shape[0], shape[2], shape[3])
        return (shape[0], shape[1], shape[2] // N_CHIPS)

    def copies(t, rb, sems, start):
        x, y, c = _mesh_pos()
        make = _start_remote if start else _remote
        return [make(slot(t[w], kinds[w], kj), rb[w].at[r], sems[0].at[3 * w + r], sems[1].at[3 * w + r], dev)
                for w in range(nw) for r, (kj, dev) in enumerate(_chip_peers(x, y, c))]

    shapes = [jax.ShapeDtypeStruct((N_CHIPS - 1,) + slot_shape(k, a.shape), a.dtype) for k, a in zip(kinds, parts)]
    return _exchange_hook(parts, _pair_exchange(copies), 3 * nw, shapes)


def _rs_join(halves):
    nw = len(halves)

    def copies(src, dst, sems, start):
        x, y, c = _mesh_pos()
        make = _start_remote if start else _remote
        return [make(src[w], dst[w], sems[0].at[w], sems[1].at[w], (x, y, 1 - c)) for w in range(nw)]

    return _exchange_hook(halves, _pair_exchange(copies), nw,
                          [jax.ShapeDtypeStruct(a.shape, a.dtype) for a in halves])


def _allgather_small(buf, name):
    def body(in_ref, out_ref, ssem, rsem):
        x, y, c = _mesh_pos()
        me = 4 * x + 2 * y + c
        cps, waits = [], []
        for r in range(1, N_DEV):
            px = 1 - x if r & 4 else x
            py = 1 - y if r & 2 else y
            pc = 1 - c if r & 1 else c
            cp = _remote(in_ref, out_ref.at[me], ssem.at[r - 1], rsem.at[r - 1], (px, py, pc))
            cp.start()
            cps.append(cp)
            waits.append(_remote(in_ref, out_ref.at[4 * px + 2 * py + pc], ssem.at[r - 1], rsem.at[r - 1], (px, py, pc)))
        for wt in waits:
            wt.wait_recv()
        for cp in cps:
            cp.wait_send()

    return pl.pallas_call(
        body, in_specs=[ANY], out_specs=ANY,
        out_shape=jax.ShapeDtypeStruct((N_DEV,) + buf.shape, buf.dtype),
        scratch_shapes=[pltpu.SemaphoreType.DMA((N_DEV - 1,))] * 2,
        name=name,
    )(buf)


def _pack(arrs):
    flat = jnp.concatenate([a.reshape(-1).astype(F32) for a in arrs])
    rows = -(-flat.shape[0] // (LANES * 16)) * 16
    return jnp.pad(flat, (0, rows * LANES - flat.shape[0])).reshape(rows, LANES)


def _unpack(buf, shapes):
    flat = buf.reshape(-1)
    out, off = [], 0
    for shp in shapes:
        nel = math.prod(shp)
        out.append(flat[off:off + nel].reshape(shp))
        off += nel
    return out


BIG = ("a_w_in", "a_w_out", "b_w_in", "b_w_out", "c_w_in", "c_w_grp", "c_w_out", "f_w_up", "f_w_down")
BIG_KIND = {"a_w_in": "col", "a_w_out": "row", "b_w_in": "col", "b_w_out": "row", "c_w_in": "row",
            "c_w_grp": "row", "c_w_out": "row", "f_w_up": "col", "f_w_down": "row"}
FIRST_LAYER = ("a_w_in", "a_w_out", "f_w_up", "f_w_down")
SHARDED_SMALL =("a_dw", "a_dw_b", "a_ln_g", "a_ln_b", "c_scale", "f_dw")
REPLICATED = ("b_ln_g", "b_ln_b", "b_ws", "b_bs", "ln1_g", "ln1_b", "ln2_g", "ln2_b")
WEIGHTS = ("a_w_in", "a_dw", "a_dw_b", "a_ln_g", "a_ln_b", "a_w_out", "b_w_in", "b_ln_g", "b_ln_b", "b_ws", "b_bs",
           "b_w_out", "c_w_in", "c_w_grp", "c_scale", "c_w_out", "f_w_up", "f_dw", "f_w_down",
           "ln1_g", "ln1_b", "ln2_g", "ln2_b")


def _as3d(a):
    return a.reshape((-1,) + a.shape[-2:])


def kernel(x, a_w_in, a_dw, a_dw_b, a_ln_g, a_ln_b, a_w_out, b_w_in, b_ln_g, b_ln_b, b_ws, b_bs, b_w_out, c_w_in, c_w_grp, c_scale, c_w_out, f_w_up, f_dw, f_w_down, ln1_g, ln1_b, ln2_g, ln2_b, loss_target, m_a_w_in, m_a_dw, m_a_dw_b, m_a_ln_g, m_a_ln_b, m_a_w_out, m_b_w_in, m_b_ln_g, m_b_ln_b, m_b_ws, m_b_bs, m_b_w_out, m_c_w_in, m_c_w_grp, m_c_scale, m_c_w_out, m_f_w_up, m_f_dw, m_f_w_down, m_ln1_g, m_ln1_b, m_ln2_g, m_ln2_b, v_a_w_in, v_a_dw, v_a_dw_b, v_a_ln_g, v_a_ln_b, v_a_w_out, v_b_w_in, v_b_ln_g, v_b_ln_b, v_b_ws, v_b_bs, v_b_w_out, v_c_w_in, v_c_w_grp, v_c_scale, v_c_w_out, v_f_w_up, v_f_dw, v_f_w_down, v_ln1_g, v_ln1_b, v_ln2_g, v_ln2_b):
    w = dict(a_w_in=a_w_in, a_dw=a_dw, a_dw_b=a_dw_b, a_ln_g=a_ln_g, a_ln_b=a_ln_b, a_w_out=a_w_out, b_w_in=b_w_in, b_ln_g=b_ln_g, b_ln_b=b_ln_b, b_ws=b_ws, b_bs=b_bs, b_w_out=b_w_out, c_w_in=c_w_in, c_w_grp=c_w_grp, c_scale=c_scale, c_w_out=c_w_out, f_w_up=f_w_up, f_dw=f_dw, f_w_down=f_w_down, ln1_g=ln1_g, ln1_b=ln1_b, ln2_g=ln2_g, ln2_b=ln2_b)
    mom = dict(a_w_in=m_a_w_in, a_dw=m_a_dw, a_dw_b=m_a_dw_b, a_ln_g=m_a_ln_g, a_ln_b=m_a_ln_b, a_w_out=m_a_w_out, b_w_in=m_b_w_in, b_ln_g=m_b_ln_g, b_ln_b=m_b_ln_b, b_ws=m_b_ws, b_bs=m_b_bs, b_w_out=m_b_w_out, c_w_in=m_c_w_in, c_w_grp=m_c_w_grp, c_scale=m_c_scale, c_w_out=m_c_w_out, f_w_up=m_f_w_up, f_dw=m_f_dw, f_w_down=m_f_w_down, ln1_g=m_ln1_g, ln1_b=m_ln1_b, ln2_g=m_ln2_g, ln2_b=m_ln2_b)
    var = dict(a_w_in=v_a_w_in, a_dw=v_a_dw, a_dw_b=v_a_dw_b, a_ln_g=v_a_ln_g, a_ln_b=v_a_ln_b, a_w_out=v_a_w_out, b_w_in=v_b_w_in, b_ln_g=v_b_ln_g, b_ln_b=v_b_ln_b, b_ws=v_b_ws, b_bs=v_b_bs, b_w_out=v_b_w_out, c_w_in=v_c_w_in, c_w_grp=v_c_w_grp, c_scale=v_c_scale, c_w_out=v_c_w_out, f_w_up=v_f_w_up, f_dw=v_f_dw, f_w_down=v_f_w_down, ln1_g=v_ln1_g, ln1_b=v_ln1_b, ln2_g=v_ln2_g, ln2_b=v_ln2_b)

    depth = ln1_g.shape[0]
    d = x.shape[-1]
    alpha = float((2 * depth) ** 0.25)
    chip = 2 * lax.axis_index("x") + lax.axis_index("y")
    chip1 = chip.astype(jnp.int32).reshape(1)
    core1 = lax.axis_index("c").astype(jnp.int32).reshape(1)

    assert depth == 4
    early = [(k, 0, 1) for k in FIRST_LAYER]
    late = [(k, 1, w[k].shape[0] - 1) for k in FIRST_LAYER] + [(k, 0, _as3d(w[k]).shape[0]) for k in BIG if k not in FIRST_LAYER]
    groups = {
        "up_front": [("a_w_in", 0, 1), ("a_w_out", 0, 1)],
        "soon": [("f_w_up", 0, 1), ("f_w_down", 0, 1)],
        "mid": [("f_w_up", 1, 2), ("f_w_down", 1, 2)] + [(k, 0, _as3d(w[k]).shape[0]) for k in BIG if k not in FIRST_LAYER],
        "last": [("a_w_in", 1, 1), ("a_w_out", 1, 1), ("f_w_up", 3, 1), ("f_w_down", 3, 1)],
    }
    where = {(k, l0 + t): (name, t) for name, group in groups.items() for k, l0, nl in group for t in range(nl)}

    def group_kinds(group):
        return [BIG_KIND[k] for k, _, _ in group]

    def cast_group(name):
        return [_cast_into_full(_as3d(w[k]), l0, nl, BIG_KIND[k], chip1, f"cast_{name}_{k}") for k, l0, nl in groups[name]]

    def as_stacks(name, arrays):
        return {k: (a.reshape(a.shape[0], -1, a.shape[-1]) if BIG_KIND[k] == "row" else a)
                for (k, _, _), a in zip(groups[name], arrays)}

    def ag_hook(name, arrays, step):
        steps = _allgather_steps(arrays, group_kinds(groups[name]))
        return _exchange_hook(arrays, steps[step], 3 * len(arrays))

    full = {"up_front": as_stacks("up_front", _allgather_weights(cast_group("up_front"), group_kinds(groups["up_front"])))}

    def weight(k, l):
        name, idx = where[(k, l)]
        return full[name][k], idx

    small_all = _allgather_small(_pack([w[k] for k in SHARDED_SMALL]), "allgather_small_params")
    other_core = 1 - lax.axis_index("c")
    per_chip = [_unpack(lax.dynamic_index_in_dim(small_all, 2 * k + other_core, keepdims=False),
                        [w[n].shape for n in SHARDED_SMALL]) for k in range(N_CHIPS)]
    fs = {n: jnp.concatenate([per_chip[k][i] for k in range(N_CHIPS)], axis=-1) for i, n in enumerate(SHARDED_SMALL)}

    nh = b_ws.shape[1]
    tril = jnp.tril(jnp.ones((CHUNK, CHUNK), F32))
    wm = (b_ws[0] * tril).astype(BF16)
    wmt = jnp.swapaxes(wm, 1, 2)
    bs_exp = jnp.repeat(jnp.transpose(b_bs[0]), CHUNK, axis=1)

    xh, g, b = x[0], jnp.ones((1, d), F32), jnp.zeros((1, d), F32)
    saved = []
    for i in range(depth):
        kind, j = i % 3, i // 3
        rec = dict(xin=xh, gin=g, bin=b)
        if kind == 0:
            hook = None
            if i == 0:
                n_soon = len(groups["soon"])
                hook = _combine_hooks(ag_hook("soon", cast_group("soon"), 0), ag_hook("mid", cast_group("mid"), 0))
            (xh1, rstd1, p, chat, rstdc), landed = _conv_fwd(
                xh, g, b, weight("a_w_in", j)[0], weight("a_w_out", j)[0], weight("a_w_in", j)[1], fs["a_dw"], j,
                fs["a_dw_b"][j:j + 1], fs["a_ln_g"][j:j + 1], fs["a_ln_b"][j:j + 1], alpha, f"conv_fwd_{i}", hook)
            if i == 0:
                full["soon"] = as_stacks("soon", _exchange_call(ag_hook("soon", landed[:n_soon], 1), "allgather_soon_d2d"))
                mid_landed = landed[n_soon:]
            rec.update(p=p, chat=chat, rstdc=rstdc)
        elif kind == 1:
            xh1, rstd1, zp = _sgu_fwd(xh, g, b, full["mid"]["b_w_in"], b_ln_g, b_ln_b, wm, bs_exp,
                                      full["mid"]["b_w_out"], alpha, f"sgu_fwd_{i}")
            rec.update(zp=zp)
        else:
            xh1, rstd1, ys = _pool_fwd(xh, g, b, full["mid"]["c_w_in"], full["mid"]["c_w_grp"], fs["c_scale"],
                                       full["mid"]["c_w_out"], alpha, f"pool_fwd_{i}")
            rec.update(ys=ys)
        hook = None
        if i == 0:
            hook = _combine_hooks(ag_hook("mid", mid_landed, 1), ag_hook("last", cast_group("last"), 0))
        elif i == 1:
            hook = ag_hook("last", last_landed, 1)
        (xh2, rstd2, hs, hcs), passed_on = _ffn_fwd(
            xh1, ln1_g[i:i + 1], ln1_b[i:i + 1], weight("f_w_up", i)[0], weight("f_w_down", i)[0],
            weight("f_w_up", i)[1], fs["f_dw"], i, alpha, f"ffn_fwd_{i}", hook)
        if i == 0:
            full["mid"] = as_stacks("mid", passed_on[:len(mid_landed)])
            last_landed = passed_on[len(mid_landed):]
        elif i == 1:
            full["last"] = as_stacks("last", passed_on)
        rec.update(xh1=xh1, rstd1=rstd1, xh2=xh2, rstd2=rstd2, hs=hs, hcs=hcs)
        saved.append(rec)
        xh, g, b = xh2, ln2_g[i:i + 1], ln2_b[i:i + 1]

    dxo = loss_target[0]
    assert depth >= 3

    def stack_shape(k, nl):
        _, r, c = _as3d(w[k]).shape
        return (nl, N_CHIPS * r, c) if BIG_KIND[k] == "row" else (nl, r, N_CHIPS * c)

    g_first = {k: lax.empty(stack_shape(k, nl), F32) for k, _, nl in early}
    g_rest = {k: lax.empty(stack_shape(k, nl), F32) for k, _, nl in late if k != "c_w_grp"}

    def gslot(k, l):
        if k in FIRST_LAYER:
            return (g_first, 0) if l == 0 else (g_rest, l - 1)
        return g_rest, l

    def rs_views(group, store):
        out = []
        for k, _, _ in group:
            a = store[k]
            out.append(a.reshape(a.shape[0], N_CHIPS, -1, a.shape[-1]) if BIG_KIND[k] == "row" else a)
        return out

    def pair_sums(group, views, got, tag):
        return [_pair_sum(a, t, BIG_KIND[k], core1, f"rs_pair_sum_{tag}_{k}") for (k, _, _), a, t in zip(group, views, got)]

    def reduce_and_join(group, pair, from_chips, tag):
        half = [_chip_sum(t, rb, BIG_KIND[k], chip1, f"rs_chip_sum_{tag}_{k}")
                for (k, _, _), t, rb in zip(group, pair, from_chips)]
        return half, _exchange_call(_rs_join(half), f"rs_join_{tag}")

    ffn0, mix0 = groups["soon"], groups["up_front"]
    late_kinds = group_kinds(late)
    gs = {k: [None] * w[k].shape[0] for k in ("a_dw", "a_dw_b", "a_ln_g", "a_ln_b", "f_dw", "ln1_g", "ln1_b", "ln2_g", "ln2_b")}
    for i in reversed(range(depth)):
        kind, j = i % 3, i // 3
        rec = saved[i]
        hook = None
        if i == 0:
            late_views = rs_views(late, g_rest)
            hook = _rs_pair(late_views, late_kinds)
        st, idx = gslot("f_w_down", i)
        (dr2, dhc, st["f_w_down"], gs["ln2_g"][i], gs["ln2_b"][i], loss_term), got = _ffn_bwd1(
            dxo, rec["xh2"], rec["rstd2"], ln2_g[i:i + 1], ln2_b[i:i + 1], rec["hcs"], *weight("f_w_down", i),
            st["f_w_down"], idx, f"ffn_bwd1_{i}", hook, loss_head=(i == depth - 1))
        if i == depth - 1:
            loss = lax.psum(loss_term[0, 0], ("x", "y", "c"))
        if i == 0:
            late_pair = pair_sums(late, late_views, got, "rest")
        dr1, dh, gs["f_dw"][i], gs["ln1_g"][i], gs["ln1_b"][i] = _ffn_bwd2(
            dhc, rec["hs"], dr2, *weight("f_w_up", i), fs["f_dw"], i, rec["xh1"], rec["rstd1"], ln1_g[i:i + 1],
            alpha, f"ffn_bwd2_{i}")
        st, idx = gslot("f_w_up", i)
        st["f_w_up"] = _mm_tn(rec["xh1"], ln1_g[i:i + 1], ln1_b[i:i + 1], dh, st["f_w_up"], idx, f"grad_w_up_{i}")
        if kind == 0:
            hook = None
            if i == 0:
                ffn0_views = rs_views(ffn0, g_first)
                ffn0_got = _exchange_call(_rs_pair(ffn0_views, group_kinds(ffn0)), "rs_pair_ffn0")
                ffn0_pair = pair_sums(ffn0, ffn0_views, ffn0_got, "ffn0")
                hook = _combine_hooks(_rs_chips(late_pair, late_kinds), _rs_chips(ffn0_pair, group_kinds(ffn0)))
            st, idx = gslot("a_w_out", j)
            (dp, st["a_w_out"], gs["a_dw"][j], gs["a_dw_b"][j], gs["a_ln_g"][j], gs["a_ln_b"][j]), landed = _conv_bwd1(
                dr1, rec["chat"], rec["rstdc"], rec["p"], *weight("a_w_out", j), fs["a_dw"], j,
                fs["a_ln_g"][j:j + 1], fs["a_ln_b"][j:j + 1], st["a_w_out"], idx, f"conv_bwd1_{i}", hook)
            if i == 0:
                late_from_chips, ffn0_from_chips = landed[:len(late)], landed[len(late):]
            win_name, lidx = "a_w_in", j
        elif kind == 1:
            dp, g_rest["b_w_out"], g_ws, g_bs_t, g_blg, g_blb = _sgu_bwd1(
                dr1, rec["zp"], full["mid"]["b_w_out"], b_ln_g, b_ln_b, wm, wmt, bs_exp, g_rest["b_w_out"],
                f"sgu_bwd1_{i}")
            win_name, lidx = "b_w_in", 0
        else:
            dp, g_rest["c_w_out"], g_rest["c_w_grp"], g_cscale = _pool_bwd1(
                dr1, rec["ys"], full["mid"]["c_w_out"], full["mid"]["c_w_grp"], fs["c_scale"], g_rest["c_w_out"],
                f"pool_bwd1_{i}")
            win_name, lidx = "c_w_in", 0
        dxo = _bwd_in(dp, dr1, *weight(win_name, lidx), alpha, f"mixer_bwd2_{i}")
        st, idx = gslot(win_name, lidx)
        st[win_name] = _mm_tn(rec["xin"], rec["gin"], rec["bin"], dp, st[win_name], idx, f"grad_w_in_{i}")
    grad_x = dxo[None]

    late_half, late_other = reduce_and_join(late, late_pair, late_from_chips, "rest")
    ffn0_half, ffn0_other = reduce_and_join(ffn0, ffn0_pair, ffn0_from_chips, "ffn0")
    mix0_views = rs_views(mix0, g_first)
    mix0_got = _exchange_call(_rs_pair(mix0_views, group_kinds(mix0)), "rs_pair_mix0")
    mix0_pair = pair_sums(mix0, mix0_views, mix0_got, "mix0")
    mix0_from_chips = _exchange_call(_rs_chips(mix0_pair, group_kinds(mix0)), "rs_chips_mix0")
    mix0_half, mix0_other = reduce_and_join(mix0, mix0_pair, mix0_from_chips, "mix0")

    updates = {}
    for (k, l0, _), own, oth in zip(late, late_half, late_other):
        updates[k] = _adamw_big(_as3d(w[k]), _as3d(mom[k]), _as3d(var[k]), own, oth, core1, f"adamw_rest_{k}", l0)
    for (k, l0, _), own, oth in zip(ffn0 + mix0, ffn0_half + mix0_half, ffn0_other + mix0_other):
        updates[k] = _adamw_big(_as3d(w[k]), _as3d(mom[k]), _as3d(var[k]), own, oth, core1, f"adamw_first_{k}",
                                l0, into=updates[k])
    grads, delta, new_m, new_v = {}, {}, {}, {}
    for k in BIG:
        grads[k], delta[k], new_m[k], new_v[k] = [o.reshape(w[k].shape) for o in updates[k]]

    small_full = {
        "a_dw": jnp.stack(gs["a_dw"]), "a_dw_b": jnp.concatenate(gs["a_dw_b"]), "a_ln_g": jnp.concatenate(gs["a_ln_g"]),
        "a_ln_b": jnp.concatenate(gs["a_ln_b"]), "c_scale": g_cscale, "f_dw": jnp.stack(gs["f_dw"]),
        "b_ln_g": g_blg, "b_ln_b": g_blb, "b_ws": g_ws[None], "b_bs": jnp.transpose(g_bs_t)[None],
        "ln1_g": jnp.concatenate(gs["ln1_g"]), "ln1_b": jnp.concatenate(gs["ln1_b"]),
        "ln2_g": jnp.concatenate(gs["ln2_g"]), "ln2_b": jnp.concatenate(gs["ln2_b"]),
    }
    small_names = SHARDED_SMALL + REPLICATED
    small_shapes = [small_full[n].shape for n in small_names]
    small_packed = _pack([small_full[n] for n in small_names])
    gathered_small = _allgather_small(small_packed, "allgather_small_grads")
    me1 = (2 * chip + lax.axis_index("c")).astype(jnp.int32).reshape(1)
    summed = _unpack(_sum_devices(small_packed, gathered_small, me1, "small_grad_sum"), small_shapes)
    for n, a in zip(small_names, summed):
        if n in SHARDED_SMALL:
            cs = w[n].shape[-1]
            a = lax.dynamic_slice_in_dim(a, chip * cs, cs, axis=a.ndim - 1)
        grads[n] = a

    shapes = [w[n].shape for n in small_names]
    ds_, ms_, vs_ = _adamw(_pack([w[n] for n in small_names]), _pack([grads[n] for n in small_names]),
                           _pack([mom[n] for n in small_names]), _pack([var[n] for n in small_names]), "adamw_small")
    for n, a, bb, cc in zip(small_names, _unpack(ds_, shapes), _unpack(ms_, shapes), _unpack(vs_, shapes)):
        delta[n], new_m[n], new_v[n] = a, bb, cc

    return (loss, grad_x, *[grads[n] for n in WEIGHTS], *[delta[n] for n in WEIGHTS],
            *[new_m[n] for n in WEIGHTS], *[new_v[n] for n in WEIGHTS])
```

```python
import math

import jax
import jax.numpy as jnp
from jax import lax
from jax.experimental import pallas as pl
from jax.experimental.pallas import tpu as pltpu

F32 = jnp.float32
BF16 = jnp.bfloat16

LN_EPS = 1e-5
POOL_WINDOWS = (2, 4, 8, 16)
CHUNK = 128
ADAM_LR = 0.001
ADAM_B1 = 0.9
ADAM_B2 = 0.999
ADAM_EPS = 1e-08
ADAM_WD = 0.01
ADAM_STEP = 10

LANES = 128
SUBLANES_BF16 = 16
N_CHIPS = 4
N_DEV = 8
VMEM_LIMIT = 60 * 1024 * 1024

TM_FFN = 512
TM_FFN_BWD1 = 256
TM_CONV = 512
TM_SGU = 512
TM_POOL = 512
TM_BWD_IN = 512
TS_MM_TN = 2048
CW_FFN = 256
CW_FFN_BWD2 = 512
CONV_HALO = 32
CONV_ROW_BLOCK = 64
POOL_HALO = 16
FFN_HALO = 16

MESH = pl.DeviceIdType.MESH


def _cparams(n_grid=1, parallel=False):
    sem = ("parallel" if parallel else "arbitrary",) * n_grid
    return pltpu.CompilerParams(dimension_semantics=sem, vmem_limit_bytes=VMEM_LIMIT)


def _resident(block, imap):
    return pl.BlockSpec(block, imap, pipeline_mode=pl.Buffered(1))


def _wspec(w, l):
    _, r, c = w.shape
    return _resident((None, r, c), lambda *_: (l, 0, 0))


def _rowspec(d):
    return pl.BlockSpec((1, d), lambda *_: (0, 0))


def _dot(a, b):
    return jnp.dot(a, b, preferred_element_type=F32)


def _dot_nt(a, b):
    return lax.dot_general(a, b, (((1,), (1,)), ((), ())), preferred_element_type=F32)


def _dot_tn(a, b):
    return lax.dot_general(a, b, (((0,), (0,)), ((), ())), preferred_element_type=F32)


def _sigmoid(x):
    return jax.nn.sigmoid(x)


def _ln_stats(r):
    mu = jnp.mean(r, axis=1, keepdims=True)
    xc = r - mu
    var = jnp.mean(xc * xc, axis=1, keepdims=True)
    rstd = lax.rsqrt(var + LN_EPS)
    return xc * rstd, rstd


def _ln_bwd(dy, xhat, rstd, g):
    dxh = dy * g
    m1 = jnp.mean(dxh, axis=1, keepdims=True)
    m2 = jnp.mean(dxh * xhat, axis=1, keepdims=True)
    return rstd * (dxh - m1 - xhat * m2)


def _colsum(v):
    return jnp.sum(v, axis=0, keepdims=True)


def _gelu(z):
    return 0.5 * z * (1.0 + lax.erf(z * (1.0 / math.sqrt(2.0))))


def _gelu_grad(z):
    cdf = 0.5 * (1.0 + lax.erf(z * (1.0 / math.sqrt(2.0))))
    pdf = jnp.exp(-0.5 * z * z) * (1.0 / math.sqrt(2.0 * math.pi))
    return cdf + z * pdf


def _shift_down(v, k, prev_rows):
    rolled = pltpu.roll(v, k, 0)
    head = rolled[0:8]
    rows = lax.broadcasted_iota(jnp.int32, head.shape, 0)
    for r in range(k):
        head = jnp.where(rows == r, prev_rows[k - 1 - r], head)
    return jnp.concatenate([head, rolled[8:]], axis=0)


def _shift_up(v, k, next_rows):
    tm = v.shape[0]
    rolled = pltpu.roll(v, tm - k, 0)
    tail = rolled[tm - 8:tm]
    rows = lax.broadcasted_iota(jnp.int32, tail.shape, 0)
    for r in range(k):
        tail = jnp.where(rows == 8 - k + r, next_rows[r], tail)
    return jnp.concatenate([rolled[0:tm - 8], tail], axis=0)


def _fill_shifted(base_scr, sh_scr):
    nrows = sh_scr.shape[1]
    for r in range(1, 8):
        sh_scr[r - 1, :, :] = base_scr[pl.ds(r, nrows), :]


def _tap(base_scr, sh_scr, off, r0, nrows, cols):
    q, r = divmod(off, 8)
    if r == 0:
        return base_scr[pl.ds(r0 + 8 * q, nrows), cols]
    return sh_scr[r - 1, pl.ds(r0 + 8 * q, nrows), cols]


def _pick_rows(r, c, itemsize, cap_bytes):
    best = None
    for t in range(16, r + 1, 16):
        if r % t == 0 and t * c * itemsize <= cap_bytes:
            best = t
    return best if best is not None else r


def _ffn_conv_cols(h, dw_ref, c0, cw, prev1, prev2):
    kw = dw_ref.shape[0]
    h1 = _shift_down(h, 1, [prev1])
    h2 = _shift_down(h, 2, [prev1, prev2])
    hc = dw_ref[kw - 1:kw, c0:c0 + cw] * h + dw_ref[kw - 2:kw - 1, c0:c0 + cw] * h1 + dw_ref[kw - 3:kw - 2, c0:c0 + cw] * h2
    return hc, h1, h2


def _ffn_fwd(xh1, g1, b1, wup, wdn, lw, fdw, l, alpha, name, hook=None):
    s, d = xh1.shape
    f2 = wup.shape[2]
    f = f2 // 2
    tm = min(TM_FFN, s)
    cw = min(CW_FFN, f)
    n, nck = s // tm, f // cw
    assert fdw.shape[1] == 3 and s % tm == 0 and f % cw == 0

    def body(xh_ref, g_ref, b_ref, wup_ref, dw_ref, wdn_ref, xo_ref, rs_ref, hs_ref, hcs_ref, carry):
        @pl.when(pl.program_id(0) == 0)
        def _():
            carry[...] = jnp.zeros_like(carry)

        x1 = xh_ref[...] * g_ref[...] + b_ref[...]
        xb = x1.astype(BF16)
        o = jnp.zeros((tm, d), F32)

        def up_proj(j):
            return [_dot(xb, wup_ref[:, half * f + j * cw:half * f + (j + 1) * cw]) for half in range(2)]

        ahead = up_proj(0)
        for j in range(nck):
            hh = ahead
            if j + 1 < nck:
                ahead = up_proj(j + 1)
            parts = []
            for half in range(2):
                c0 = half * f + j * cw
                h = hh[half]
                hs_ref[:, c0:c0 + cw] = h.astype(BF16)
                hc, _, _ = _ffn_conv_cols(h, dw_ref, c0, cw, carry[7:8, c0:c0 + cw], carry[6:7, c0:c0 + cw])
                carry[:, c0:c0 + cw] = h[tm - 8:tm, :]
                hcs_ref[:, c0:c0 + cw] = hc.astype(BF16)
                parts.append(hc)
            gg, vv = parts
            a = (gg * _sigmoid(gg) * vv).astype(BF16)
            o = o + _dot(a, wdn_ref[j * cw:(j + 1) * cw, :])
        xhat, rstd = _ln_stats(alpha * x1 + o)
        xo_ref[...] = xhat
        rs_ref[...] = rstd

    tile = pl.BlockSpec((tm, d), lambda i: (i, 0))
    return _hosted_call(
        body, hook, n,
        in_specs=[tile, _rowspec(d), _rowspec(d), _wspec(wup, lw),
                  pl.BlockSpec((None, 3, f2), lambda i: (l, 0, 0)), _wspec(wdn, lw)],
        out_specs=[tile, pl.BlockSpec((tm, 1), lambda i: (i, 0)), pl.BlockSpec((tm, f2), lambda i: (i, 0)),
                   pl.BlockSpec((tm, f2), lambda i: (i, 0))],
        out_shape=[jax.ShapeDtypeStruct((s, d), F32), jax.ShapeDtypeStruct((s, 1), F32),
                   jax.ShapeDtypeStruct((s, f2), BF16), jax.ShapeDtypeStruct((s, f2), BF16)],
        scratch_shapes=[pltpu.VMEM((8, f2), F32)],
        args=(xh1, g1, b1, wup, fdw, wdn), name=name)


def _ffn_bwd1(dx2, xh2, rstd2, g2, b2, hcs, wdn, lw, gwdn_buf, l, name, hook=None, loss_head=False):
    s, d = dx2.shape
    f2 = hcs.shape[1]
    f = f2 // 2
    tm = min(TM_FFN_BWD1, s)
    cw = min(CW_FFN, f)
    n, nck = s // tm, f // cw

    def body(dx_ref, xh_ref, rs_ref, g_ref, b_ref, hcs_ref, wdn_ref, buf_ref,
             dr_ref, dhc_ref, gwdn_ref, gg_ref, gb_ref, loss_ref):
        @pl.when(pl.program_id(0) == 0)
        def _():
            gwdn_ref[...] = jnp.zeros_like(gwdn_ref)
            gg_ref[...] = jnp.zeros_like(gg_ref)
            gb_ref[...] = jnp.zeros_like(gb_ref)
            loss_ref[...] = jnp.zeros_like(loss_ref)

        xh = xh_ref[...]
        if loss_head:
            err = xh * g_ref[...] + b_ref[...] - dx_ref[...]
            dx = err * (1.0 / d)
            loss_ref[...] += (0.5 / d) * jnp.sum(_colsum(err * err), axis=1, keepdims=True)
        else:
            dx = dx_ref[...]
        gg_ref[...] += _colsum(dx * xh)
        gb_ref[...] += _colsum(dx)
        dr = _ln_bwd(dx, xh, rs_ref[...], g_ref[...])
        dr_ref[...] = dr
        dob = dr.astype(BF16)

        def d_act(j):
            return _dot_nt(dob, wdn_ref[j * cw:(j + 1) * cw, :])

        da_ahead = d_act(0)
        for j in range(nck):
            da = da_ahead
            if j + 1 < nck:
                da_ahead = d_act(j + 1)
            gt = hcs_ref[:, j * cw:(j + 1) * cw].astype(F32)
            vv = hcs_ref[:, f + j * cw:f + (j + 1) * cw].astype(F32)
            sg = _sigmoid(gt)
            sl = gt * sg
            a = (sl * vv).astype(BF16)
            gwdn_ref[j * cw:(j + 1) * cw, :] += _dot_tn(a, dob)
            dhc_ref[:, j * cw:(j + 1) * cw] = (da * vv * (sg * (1.0 + gt * (1.0 - sg)))).astype(BF16)
            dhc_ref[:, f + j * cw:f + (j + 1) * cw] = (da * sl).astype(BF16)

    tile = pl.BlockSpec((tm, d), lambda i: (i, 0))
    wide = pl.BlockSpec((tm, f2), lambda i: (i, 0))
    nl = gwdn_buf.shape[0]
    return _hosted_call(
        body, hook, n,
        in_specs=[tile, tile, pl.BlockSpec((tm, 1), lambda i: (i, 0)), _rowspec(d), _rowspec(d), wide,
                  _wspec(wdn, lw), pl.BlockSpec(memory_space=pl.ANY)],
        out_specs=[tile, wide, pl.BlockSpec((None, f, d), lambda i: (l, 0, 0)), _rowspec(d), _rowspec(d),
                   pl.BlockSpec((1, 1), lambda i: (0, 0))],
        out_shape=[jax.ShapeDtypeStruct((s, d), F32), jax.ShapeDtypeStruct((s, f2), BF16),
                   jax.ShapeDtypeStruct((nl, f, d), F32),
                   jax.ShapeDtypeStruct((1, d), F32), jax.ShapeDtypeStruct((1, d), F32),
                   jax.ShapeDtypeStruct((1, 1), F32)],
        scratch_shapes=[], args=(dx2, xh2, rstd2, g2, b2, hcs, wdn, gwdn_buf), name=name, aliases={7: 2})


def _bwd_in(dp, dres, w, l, alpha, name):
    s, d = dres.shape
    nn = dp.shape[1]
    tm = min(TM_BWD_IN, s)
    n = s // tm
    tile = pl.BlockSpec((tm, d), lambda i: (i, 0))

    def body(dp_ref, dres_ref, w_ref, o_ref):
        o_ref[...] = alpha * dres_ref[...] + _dot_nt(dp_ref[...], w_ref[...])

    return pl.pallas_call(
        body, grid=(n,),
        in_specs=[pl.BlockSpec((tm, nn), lambda i: (i, 0)), tile, _wspec(w, l)],
        out_specs=tile, out_shape=jax.ShapeDtypeStruct((s, d), F32),
        compiler_params=_cparams(parallel=True), name=name,
    )(dp, dres, w)


def _ffn_bwd2(dhc, hs, dres, wup, lw, fdw, l, xh, rstd, g, alpha, name):
    s, d = dres.shape
    f2 = dhc.shape[1]
    tm = min(TM_BWD_IN, s)
    n = s // tm
    hb = FFN_HALO
    cw = min(CW_FFN_BWD2, f2)
    nck = f2 // cw
    halo_blocks = tm // hb
    assert f2 % cw == 0 and fdw.shape[1] == 3

    def body(dhc_ref, halo_ref, hs_ref, dres_ref, w_ref, dw_ref, xh_ref, rs_ref, g_ref,
             o_ref, dh_ref, gdw_ref, gg_ref, gb_ref):
        i = pl.program_id(0)

        @pl.when(i == 0)
        def _():
            gdw_ref[...] = jnp.zeros_like(gdw_ref)
            gg_ref[...] = jnp.zeros_like(gg_ref)
            gb_ref[...] = jnp.zeros_like(gb_ref)

        has_next = i < n - 1
        dx = alpha * dres_ref[...]
        for j in range(nck):
            c0 = j * cw
            dc = dhc_ref[:, c0:c0 + cw].astype(F32)
            hal = jnp.where(has_next, halo_ref[:, c0:c0 + cw].astype(F32), 0.0)
            nxt = [hal[0:1], hal[1:2]]
            u1 = _shift_up(dc, 1, nxt[:1])
            u2 = _shift_up(dc, 2, nxt)
            h = hs_ref[:, c0:c0 + cw].astype(F32)
            gdw_ref[2:3, c0:c0 + cw] += _colsum(dc * h)
            gdw_ref[1:2, c0:c0 + cw] += _colsum(u1 * h)
            gdw_ref[0:1, c0:c0 + cw] += _colsum(u2 * h)
            dh = (dw_ref[2:3, c0:c0 + cw] * dc + dw_ref[1:2, c0:c0 + cw] * u1
                  + dw_ref[0:1, c0:c0 + cw] * u2).astype(BF16)
            dh_ref[:, c0:c0 + cw] = dh
            dx = dx + _dot_nt(dh, w_ref[:, c0:c0 + cw])
        xhv = xh_ref[...]
        gg_ref[...] += _colsum(dx * xhv)
        gb_ref[...] += _colsum(dx)
        o_ref[...] = _ln_bwd(dx, xhv, rs_ref[...], g_ref[...])

    tile = pl.BlockSpec((tm, d), lambda i: (i, 0))
    wide = pl.BlockSpec((tm, f2), lambda i: (i, 0))
    return pl.pallas_call(
        body, grid=(n,),
        in_specs=[wide, pl.BlockSpec((hb, f2), lambda i: (jnp.minimum((i + 1) * halo_blocks, s // hb - 1), 0)),
                  wide, tile, _wspec(wup, lw), pl.BlockSpec((None, 3, f2), lambda i: (l, 0, 0)), tile,
                  pl.BlockSpec((tm, 1), lambda i: (i, 0)), _rowspec(d)],
        out_specs=[tile, wide, pl.BlockSpec((3, f2), lambda i: (0, 0)), _rowspec(d), _rowspec(d)],
        out_shape=[jax.ShapeDtypeStruct((s, d), F32), jax.ShapeDtypeStruct((s, f2), BF16),
                   jax.ShapeDtypeStruct((3, f2), F32),
                   jax.ShapeDtypeStruct((1, d), F32), jax.ShapeDtypeStruct((1, d), F32)],
        compiler_params=_cparams(), name=name,
    )(dhc, dhc, hs, dres, wup, fdw, xh, rstd, g)


def _mm_tn(a, ga, ba, bm, buf, l, name):
    s, k = a.shape
    nn = bm.shape[1]
    ts = min(TS_MM_TN, s)
    tn = nn // N_CHIPS if nn > 1024 else nn
    nj, ns = nn // tn, s // ts

    def body(a_ref, g_ref, b_ref, bm_ref, buf_ref, o_ref):
        @pl.when(pl.program_id(1) == 0)
        def _():
            o_ref[...] = jnp.zeros_like(o_ref)

        ab = (a_ref[...] * g_ref[...] + b_ref[...]).astype(BF16)
        o_ref[...] += _dot_tn(ab, bm_ref[...])

    return pl.pallas_call(
        body, grid=(nj, ns),
        in_specs=[pl.BlockSpec((ts, k), lambda j, t: (t, 0)), _rowspec(k), _rowspec(k),
                  pl.BlockSpec((ts, tn), lambda j, t: (t, j)), pl.BlockSpec(memory_space=pl.ANY)],
        out_specs=pl.BlockSpec((None, k, tn), lambda j, t: (l, 0, j)),
        out_shape=jax.ShapeDtypeStruct(buf.shape, F32),
        input_output_aliases={4: 0},
        compiler_params=_cparams(2), name=name,
    )(a, ga, ba, bm, buf)


def _conv_fwd(xin, gin, bin_, win, wout, lw, adw, l, adwb, lng, lnb, alpha, name, hook=None):
    s, d = xin.shape
    kw = adw.shape[1]
    hb = CONV_HALO
    tm = min(TM_CONV, s)
    n = s // tm
    assert kw - 1 <= hb <= tm

    def body(x_ref, g_ref, b_ref, win_ref, dw_ref, dwb_ref, lng_ref, lnb_ref, wout_ref,
             xo_ref, rs_ref, p_ref, chat_ref, rsc_ref, u_scr, u8_scr):
        @pl.when(pl.program_id(0) == 0)
        def _():
            u_scr[0:hb, :] = jnp.zeros((hb, d), F32)

        x = x_ref[...] * g_ref[...] + b_ref[...]
        pm = _dot(x.astype(BF16), win_ref[...])
        p_ref[...] = pm.astype(BF16)
        u = pm[:, :d] * _sigmoid(pm[:, d:])
        u_scr[hb:hb + tm, :] = u
        _fill_shifted(u_scr, u8_scr)
        acc = dwb_ref[...] + dw_ref[kw - 1:kw, :] * u
        for k in range(kw - 1):
            acc = acc + dw_ref[k:k + 1, :] * _tap(u_scr, u8_scr, hb - (kw - 1) + k, 0, tm, slice(None))
        u_scr[0:hb, :] = u_scr[tm:tm + hb, :]
        chat, rstdc = _ln_stats(acc)
        chat_ref[...] = chat.astype(BF16)
        rsc_ref[...] = rstdc
        nv = chat * lng_ref[...] + lnb_ref[...]
        sv = (nv * _sigmoid(nv)).astype(BF16)
        xhat, rstd = _ln_stats(alpha * x + _dot(sv, wout_ref[...]))
        xo_ref[...] = xhat
        rs_ref[...] = rstd

    tile = pl.BlockSpec((tm, d), lambda i: (i, 0))
    col = pl.BlockSpec((tm, 1), lambda i: (i, 0))
    return _hosted_call(
        body, hook, n,
        in_specs=[tile, _rowspec(d), _rowspec(d), _wspec(win, lw),
                  pl.BlockSpec((None, kw, d), lambda i: (l, 0, 0)), _rowspec(d), _rowspec(d), _rowspec(d),
                  _wspec(wout, lw)],
        out_specs=[tile, col, pl.BlockSpec((tm, 2 * d), lambda i: (i, 0)), tile, col],
        out_shape=[jax.ShapeDtypeStruct((s, d), F32), jax.ShapeDtypeStruct((s, 1), F32),
                   jax.ShapeDtypeStruct((s, 2 * d), BF16), jax.ShapeDtypeStruct((s, d), BF16),
                   jax.ShapeDtypeStruct((s, 1), F32)],
        scratch_shapes=[pltpu.VMEM((tm + hb, d), F32), pltpu.VMEM((7, tm + hb - 8, d), F32)],
        args=(xin, gin, bin_, win, adw, adwb, lng, lnb, wout), name=name)


def _conv_bwd1(dr1, chat, rstdc, p, wout, lw, adw, lt, lng, lnb, gwout_buf, l, name, hook=None):
    s, d = dr1.shape
    kw = adw.shape[1]
    hb = CONV_HALO
    tm = min(TM_CONV, s)
    n = s // tm
    halo_blocks = tm // hb
    rbl = CONV_ROW_BLOCK

    def body(dr_ref, chat_ref, rsc_ref, p_ref, halo_ref, wout_ref, dw_ref, lng_ref, lnb_ref, buf_ref,
             dp_ref, gwout_ref, gdw_ref, gdwb_ref, glng_ref, glnb_ref, u_scr, dc_scr, u8_scr, dc8_scr):
        i = pl.program_id(0)
        t = n - 1 - i

        @pl.when(i == 0)
        def _():
            dc_scr[tm:tm + hb, :] = jnp.zeros((hb, d), F32)
            gwout_ref[...] = jnp.zeros_like(gwout_ref)
            gdw_ref[...] = jnp.zeros_like(gdw_ref)
            gdwb_ref[...] = jnp.zeros_like(gdwb_ref)
            glng_ref[...] = jnp.zeros_like(glng_ref)
            glnb_ref[...] = jnp.zeros_like(glnb_ref)

        dob = dr_ref[...].astype(BF16)
        chat = chat_ref[...].astype(F32)
        lng = lng_ref[...]
        nv = chat * lng + lnb_ref[...]
        sgn = _sigmoid(nv)
        gwout_ref[...] += _dot_tn((nv * sgn).astype(BF16), dob)
        dn = _dot_nt(dob, wout_ref[...]) * (sgn * (1.0 + nv * (1.0 - sgn)))
        glng_ref[...] += _colsum(dn * chat)
        glnb_ref[...] += _colsum(dn)
        dc = _ln_bwd(dn, chat, rsc_ref[...], lng)
        gdwb_ref[...] += _colsum(dc)

        pm = p_ref[...].astype(F32)
        a = pm[:, :d]
        sg = _sigmoid(pm[:, d:])
        ph = halo_ref[...].astype(F32)
        u_scr[0:hb, :] = jnp.where(t > 0, ph[:, :d] * _sigmoid(ph[:, d:]), 0.0)
        u_scr[hb:hb + tm, :] = a * sg
        dc_scr[0:tm, :] = dc
        _fill_shifted(u_scr, u8_scr)
        _fill_shifted(dc_scr, dc8_scr)
        du = dw_ref[kw - 1:kw, :] * dc
        for k in range(kw - 1):
            du = du + dw_ref[k:k + 1, :] * _tap(dc_scr, dc8_scr, kw - 1 - k, 0, tm, slice(None))
        for cb in range(d // LANES):
            cols = pl.ds(cb * LANES, LANES)

            def rows_step(rb, accs, cols=cols):
                r0 = pl.multiple_of(rb * rbl, rbl)
                dcb = dc_scr[pl.ds(r0, rbl), cols]
                out = []
                for k in range(kw):
                    prod = dcb * _tap(u_scr, u8_scr, hb - (kw - 1) + k, r0, rbl, cols)
                    part = prod[0:8]
                    for g8 in range(1, rbl // 8):
                        part = part + prod[8 * g8:8 * g8 + 8]
                    out.append(accs[k] + part)
                return tuple(out)

            accs = lax.fori_loop(0, tm // rbl, rows_step, tuple(jnp.zeros((8, LANES), F32) for _ in range(kw)))
            for k in range(kw):
                gdw_ref[k:k + 1, cols] += _colsum(accs[k])
        dc_scr[tm:tm + hb, :] = dc[0:hb, :]
        dp_ref[:, :d] = (du * sg).astype(BF16)
        dp_ref[:, d:] = (du * a * sg * (1.0 - sg)).astype(BF16)

    tile = pl.BlockSpec((tm, d), lambda i: (n - 1 - i, 0))
    col = pl.BlockSpec((tm, 1), lambda i: (n - 1 - i, 0))
    nl = gwout_buf.shape[0]
    return _hosted_call(
        body, hook, n,
        in_specs=[tile, tile, col, pl.BlockSpec((tm, 2 * d), lambda i: (n - 1 - i, 0)),
                  pl.BlockSpec((hb, 2 * d), lambda i: (jnp.maximum((n - 1 - i) * halo_blocks - 1, 0), 0)),
                  _wspec(wout, lw), pl.BlockSpec((None, kw, d), lambda i: (lt, 0, 0)), _rowspec(d), _rowspec(d),
                  pl.BlockSpec(memory_space=pl.ANY)],
        out_specs=[pl.BlockSpec((tm, 2 * d), lambda i: (n - 1 - i, 0)),
                   pl.BlockSpec((None, d, d), lambda i: (l, 0, 0)),
                   pl.BlockSpec((kw, d), lambda i: (0, 0)), _rowspec(d), _rowspec(d), _rowspec(d)],
        out_shape=[jax.ShapeDtypeStruct((s, 2 * d), BF16), jax.ShapeDtypeStruct((nl, d, d), F32),
                   jax.ShapeDtypeStruct((kw, d), F32), jax.ShapeDtypeStruct((1, d), F32),
                   jax.ShapeDtypeStruct((1, d), F32), jax.ShapeDtypeStruct((1, d), F32)],
        scratch_shapes=[pltpu.VMEM((tm + hb, d), F32), pltpu.VMEM((tm + hb, d), F32),
                        pltpu.VMEM((7, tm + hb - 8, d), F32), pltpu.VMEM((7, tm + hb - 8, d), F32)],
        args=(dr1, chat, rstdc, p, p, wout, adw, lng, lnb, gwout_buf), name=name, aliases={9: 1})


def _sgu_gate(vn, wm_ref, bs_ref, s_scr, tm, nh):
    for ch in range(tm // CHUNK):
        r0 = ch * CHUNK
        for h in range(nh):
            c0 = h * CHUNK
            s_scr[r0:r0 + CHUNK, c0:c0 + CHUNK] = (
                _dot(wm_ref[h], vn[r0:r0 + CHUNK, c0:c0 + CHUNK]) + bs_ref[:, c0:c0 + CHUNK])


def _sgu_fwd(xin, gin, bin_, win, lg, lb, wm, bs_exp, wout, alpha, name):
    s, d = xin.shape
    nh = wm.shape[0]
    tm = min(TM_SGU, s)
    n = s // tm
    assert tm % CHUNK == 0 and nh * CHUNK == d

    def body(x_ref, g_ref, b_ref, win_ref, lg_ref, lb_ref, wm_ref, bs_ref, wout_ref,
             xo_ref, rs_ref, zp_ref, s_scr):
        x = x_ref[...] * g_ref[...] + b_ref[...]
        zp = _dot(x.astype(BF16), win_ref[...])
        zp_ref[...] = zp.astype(BF16)
        z = _gelu(zp)
        vhat, _ = _ln_stats(z[:, d:])
        vn = (vhat * lg_ref[...] + lb_ref[...]).astype(BF16)
        _sgu_gate(vn, wm_ref, bs_ref, s_scr, tm, nh)
        q = (z[:, :d] * s_scr[...]).astype(BF16)
        xhat, rstd = _ln_stats(alpha * x + _dot(q, wout_ref[...]))
        xo_ref[...] = xhat
        rs_ref[...] = rstd

    tile = pl.BlockSpec((tm, d), lambda i: (i, 0))
    return pl.pallas_call(
        body, grid=(n,),
        in_specs=[tile, _rowspec(d), _rowspec(d), _wspec(win, 0), _rowspec(d), _rowspec(d),
                  _resident((nh, CHUNK, CHUNK), lambda i: (0, 0, 0)),
                  _resident((CHUNK, d), lambda i: (0, 0)), _wspec(wout, 0)],
        out_specs=[tile, pl.BlockSpec((tm, 1), lambda i: (i, 0)), pl.BlockSpec((tm, 2 * d), lambda i: (i, 0))],
        out_shape=[jax.ShapeDtypeStruct((s, d), F32), jax.ShapeDtypeStruct((s, 1), F32),
                   jax.ShapeDtypeStruct((s, 2 * d), BF16)],
        scratch_shapes=[pltpu.VMEM((tm, d), F32)],
        compiler_params=_cparams(parallel=True), name=name,
    )(xin, gin, bin_, win, lg, lb, wm, bs_exp, wout)


def _sgu_bwd1(dr1, zp, wout, lg, lb, wm, wmt, bs_exp, gwout_buf, name):
    s, d = dr1.shape
    nh = wm.shape[0]
    tm = min(TM_SGU, s)
    n = s // tm

    def body(dr_ref, zp_ref, wout_ref, lg_ref, lb_ref, wm_ref, wmt_ref, bs_ref, buf_ref,
             dzp_ref, gwout_ref, gws_ref, gbs_ref, glg_ref, glb_ref, s_scr, dvn_scr, bs_acc):
        i = pl.program_id(0)

        @pl.when(i == 0)
        def _():
            gwout_ref[...] = jnp.zeros_like(gwout_ref)
            gws_ref[...] = jnp.zeros_like(gws_ref)
            glg_ref[...] = jnp.zeros_like(glg_ref)
            glb_ref[...] = jnp.zeros_like(glb_ref)
            bs_acc[...] = jnp.zeros_like(bs_acc)

        dob = dr_ref[...].astype(BF16)
        zp = zp_ref[...].astype(F32)
        z = _gelu(zp)
        u = z[:, :d]
        lg = lg_ref[...]
        vhat, rstdv = _ln_stats(z[:, d:])
        vn = (vhat * lg + lb_ref[...]).astype(BF16)
        _sgu_gate(vn, wm_ref, bs_ref, s_scr, tm, nh)
        sv = s_scr[...]
        gwout_ref[...] += _dot_tn((u * sv).astype(BF16), dob)
        dq = _dot_nt(dob, wout_ref[...])
        ds = dq * u
        dsb = ds.astype(BF16)
        part = jnp.zeros((CHUNK, d), F32)
        for ch in range(tm // CHUNK):
            r0 = ch * CHUNK
            part = part + ds[r0:r0 + CHUNK, :]
            for h in range(nh):
                c0 = h * CHUNK
                blk = dsb[r0:r0 + CHUNK, c0:c0 + CHUNK]
                gws_ref[h] += _dot_nt(blk, vn[r0:r0 + CHUNK, c0:c0 + CHUNK])
                dvn_scr[r0:r0 + CHUNK, c0:c0 + CHUNK] = _dot(wmt_ref[h], blk)
        bs_acc[...] += part
        dvn = dvn_scr[...]
        glg_ref[...] += _colsum(dvn * vhat)
        glb_ref[...] += _colsum(dvn)
        dv = _ln_bwd(dvn, vhat, rstdv, lg)
        gp = _gelu_grad(zp)
        dzp_ref[:, :d] = (dq * sv * gp[:, :d]).astype(BF16)
        dzp_ref[:, d:] = (dv * gp[:, d:]).astype(BF16)

        @pl.when(i == n - 1)
        def _():
            rows = lax.broadcasted_iota(jnp.int32, (CHUNK, CHUNK), 0)
            cols = lax.broadcasted_iota(jnp.int32, (CHUNK, CHUNK), 1)
            tril = (cols <= rows).astype(F32)
            acc = bs_acc[...]
            for h in range(nh):
                gws_ref[h] = gws_ref[h] * tril
                gbs_ref[:, h:h + 1] = jnp.sum(acc[:, h * CHUNK:(h + 1) * CHUNK], axis=1, keepdims=True)

    tile = pl.BlockSpec((tm, d), lambda i: (i, 0))
    wide = pl.BlockSpec((tm, 2 * d), lambda i: (i, 0))
    hspec = _resident((nh, CHUNK, CHUNK), lambda i: (0, 0, 0))
    return pl.pallas_call(
        body, grid=(n,),
        in_specs=[tile, wide, _wspec(wout, 0), _rowspec(d), _rowspec(d), hspec, hspec,
                  _resident((CHUNK, d), lambda i: (0, 0)), pl.BlockSpec(memory_space=pl.ANY)],
        out_specs=[wide, pl.BlockSpec((None, d, d), lambda i: (0, 0, 0)),
                   pl.BlockSpec((nh, CHUNK, CHUNK), lambda i: (0, 0, 0)),
                   pl.BlockSpec((CHUNK, nh), lambda i: (0, 0)), _rowspec(d), _rowspec(d)],
        out_shape=[jax.ShapeDtypeStruct((s, 2 * d), BF16), jax.ShapeDtypeStruct(gwout_buf.shape, F32),
                   jax.ShapeDtypeStruct((nh, CHUNK, CHUNK), F32), jax.ShapeDtypeStruct((CHUNK, nh), F32),
                   jax.ShapeDtypeStruct((1, d), F32), jax.ShapeDtypeStruct((1, d), F32)],
        scratch_shapes=[pltpu.VMEM((tm, d), F32), pltpu.VMEM((tm, d), F32), pltpu.VMEM((CHUNK, d), F32)],
        input_output_aliases={8: 1},
        compiler_params=_cparams(), name=name,
    )(dr1, zp, wout, lg, lb, wm, wmt, bs_exp, gwout_buf)


def _pool_counts(t0, tm, w):
    pos = t0 + lax.broadcasted_iota(jnp.int32, (tm, 1), 0)
    return jnp.minimum(pos + 1, w).astype(F32)


def _pool_fwd(xin, gin, bin_, win, wg, scale, wout, alpha, name):
    s, d = xin.shape
    ng, dg = wg.shape[0], wg.shape[1]
    hb = POOL_HALO
    tm = min(TM_POOL, s)
    n = s // tm
    assert ng == len(POOL_WINDOWS) and ng * dg == d and max(POOL_WINDOWS) <= hb

    def body(x_ref, g_ref, b_ref, win_ref, wg_ref, sc_ref, wout_ref, xo_ref, rs_ref, ys_ref, y_scr, z_scr):
        i = pl.program_id(0)

        @pl.when(i == 0)
        def _():
            y_scr[0:hb, :] = jnp.zeros((hb, d), F32)

        x = x_ref[...] * g_ref[...] + b_ref[...]
        y = _dot(x.astype(BF16), win_ref[...])
        ys_ref[...] = y.astype(BF16)
        y_scr[hb:hb + tm, :] = y
        for g, w in enumerate(POOL_WINDOWS):
            c0 = g * dg
            acc = y[:, c0:c0 + dg]
            for dd in range(1, w):
                acc = acc + y_scr[pl.ds(hb - dd, tm), c0:c0 + dg]
            pg = acc / _pool_counts(i * tm, tm, w) - y[:, c0:c0 + dg]
            z_scr[:, c0:c0 + dg] = _dot(pg.astype(BF16), wg_ref[g])
        y_scr[0:hb, :] = y_scr[tm:tm + hb, :]
        zz = (z_scr[...] * sc_ref[...]).astype(BF16)
        xhat, rstd = _ln_stats(alpha * x + _dot(zz, wout_ref[...]))
        xo_ref[...] = xhat
        rs_ref[...] = rstd

    tile = pl.BlockSpec((tm, d), lambda i: (i, 0))
    return pl.pallas_call(
        body, grid=(n,),
        in_specs=[tile, _rowspec(d), _rowspec(d), _wspec(win, 0),
                  _resident((ng, dg, dg), lambda i: (0, 0, 0)), _rowspec(d), _wspec(wout, 0)],
        out_specs=[tile, pl.BlockSpec((tm, 1), lambda i: (i, 0)), tile],
        out_shape=[jax.ShapeDtypeStruct((s, d), F32), jax.ShapeDtypeStruct((s, 1), F32),
                   jax.ShapeDtypeStruct((s, d), BF16)],
        scratch_shapes=[pltpu.VMEM((tm + hb, d), F32), pltpu.VMEM((tm, d), F32)],
        compiler_params=_cparams(), name=name,
    )(xin, gin, bin_, win, wg, scale, wout)


def _pool_bwd1(dr1, ys, wout, wg, scale, gwout_buf, name):
    s, d = dr1.shape
    ng, dg = wg.shape[0], wg.shape[1]
    hb = POOL_HALO
    tm = min(TM_POOL, s)
    n = s // tm
    halo_blocks = tm // hb

    def body(dr_ref, ys_ref, halo_ref, wout_ref, wg_ref, sc_ref, buf_ref,
             dy_ref, gwout_ref, gwg_ref, gsc_ref, y_scr, e_scr, z_scr, dp_scr):
        i = pl.program_id(0)
        t = n - 1 - i

        @pl.when(i == 0)
        def _():
            e_scr[tm:tm + hb, :] = jnp.zeros((hb, d), F32)
            gwout_ref[...] = jnp.zeros_like(gwout_ref)
            gwg_ref[...] = jnp.zeros_like(gwg_ref)
            gsc_ref[...] = jnp.zeros_like(gsc_ref)

        dob = dr_ref[...].astype(BF16)
        y = ys_ref[...].astype(F32)
        y_scr[0:hb, :] = jnp.where(t > 0, halo_ref[...].astype(F32), 0.0)
        y_scr[hb:hb + tm, :] = y
        pgs = []
        for g, w in enumerate(POOL_WINDOWS):
            c0 = g * dg
            acc = y[:, c0:c0 + dg]
            for dd in range(1, w):
                acc = acc + y_scr[pl.ds(hb - dd, tm), c0:c0 + dg]
            pg = (acc / _pool_counts(t * tm, tm, w) - y[:, c0:c0 + dg]).astype(BF16)
            pgs.append(pg)
            z_scr[:, c0:c0 + dg] = _dot(pg, wg_ref[g])
        zpre = z_scr[...]
        sc = sc_ref[...]
        gwout_ref[...] += _dot_tn((zpre * sc).astype(BF16), dob)
        dz = _dot_nt(dob, wout_ref[...])
        gsc_ref[...] += _colsum(dz * zpre)
        dzpre = (dz * sc).astype(BF16)
        for g, w in enumerate(POOL_WINDOWS):
            c0 = g * dg
            dzg = dzpre[:, c0:c0 + dg]
            gwg_ref[g] += _dot_tn(pgs[g], dzg)
            dp = _dot_nt(dzg, wg_ref[g])
            dp_scr[:, c0:c0 + dg] = dp
            e_scr[0:tm, c0:c0 + dg] = dp / _pool_counts(t * tm, tm, w)
        for g, w in enumerate(POOL_WINDOWS):
            c0 = g * dg
            acc = e_scr[0:tm, c0:c0 + dg]
            for dd in range(1, w):
                acc = acc + e_scr[pl.ds(dd, tm), c0:c0 + dg]
            dy_ref[:, c0:c0 + dg] = (acc - dp_scr[:, c0:c0 + dg]).astype(BF16)
        e_scr[tm:tm + hb, :] = e_scr[0:hb, :]

    tile = pl.BlockSpec((tm, d), lambda i: (n - 1 - i, 0))
    return pl.pallas_call(
        body, grid=(n,),
        in_specs=[tile, tile,
                  pl.BlockSpec((hb, d), lambda i: (jnp.maximum((n - 1 - i) * halo_blocks - 1, 0), 0)),
                  _wspec(wout, 0), _resident((ng, dg, dg), lambda i: (0, 0, 0)), _rowspec(d),
                  pl.BlockSpec(memory_space=pl.ANY)],
        out_specs=[tile, pl.BlockSpec((None, d, d), lambda i: (0, 0, 0)),
                   pl.BlockSpec((ng, dg, dg), lambda i: (0, 0, 0)), _rowspec(d)],
        out_shape=[jax.ShapeDtypeStruct((s, d), BF16), jax.ShapeDtypeStruct(gwout_buf.shape, F32),
                   jax.ShapeDtypeStruct((ng, dg, dg), F32), jax.ShapeDtypeStruct((1, d), F32)],
        scratch_shapes=[pltpu.VMEM((tm + hb, d), F32), pltpu.VMEM((tm + hb, d), F32),
                        pltpu.VMEM((tm, d), F32), pltpu.VMEM((tm, d), F32)],
        input_output_aliases={6: 1},
        compiler_params=_cparams(), name=name,
    )(dr1, ys, ys, wout, wg, scale, gwout_buf)


def _elementwise(fn, ins, out_dtypes, name):
    shape = ins[0].shape
    c = shape[-1]
    r = math.prod(shape[:-1])
    tr = _pick_rows(r, c, 4, 1 << 20)

    def body(*refs):
        vals = fn(*[ref[...] for ref in refs[:len(ins)]])
        for ref, v in zip(refs[len(ins):], vals):
            ref[...] = v.astype(ref.dtype)

    spec = pl.BlockSpec((tr, c), lambda i: (i, 0))
    outs = pl.pallas_call(
        body, grid=(r // tr,),
        in_specs=[spec] * len(ins), out_specs=[spec] * len(out_dtypes),
        out_shape=[jax.ShapeDtypeStruct((r, c), dt) for dt in out_dtypes],
        compiler_params=_cparams(parallel=True), name=name,
    )(*[a.reshape(r, c) for a in ins])
    return [o.reshape(shape) for o in outs]


def _prefetch_call(body, grid, in_specs, out_specs, out_shape, name, aliases=None):
    return pl.pallas_call(
        body,
        grid_spec=pltpu.PrefetchScalarGridSpec(num_scalar_prefetch=1, grid=grid, in_specs=in_specs, out_specs=out_specs),
        out_shape=out_shape, input_output_aliases=aliases or {},
        compiler_params=_cparams(len(grid), parallel=True), name=name)


def _cast_into_full(w3, l0, l, kind, chip1, name):
    _, r, c = w3.shape
    tr = _pick_rows(r, c, 4, 1 << 20)

    def body(k_ref, w_ref, o_ref):
        o_ref[...] = w_ref[...].astype(BF16)

    if kind == "row":
        out_spec = pl.BlockSpec((None, None, tr, c), lambda a, j, k: (a, k[0], j, 0))
    else:
        out_spec = pl.BlockSpec((None, tr, c), lambda a, j, k: (a, j, k[0]))
    return _prefetch_call(
        body, (l, r // tr), [pl.BlockSpec((None, tr, c), lambda a, j, k: (a + l0, j, 0))], out_spec,
        jax.ShapeDtypeStruct(_full_shape(kind, (l, r, c)), BF16), name)(chip1, w3)


def _pair_sum(g, got, kind, core1, name):
    if kind == "row":
        l, nc, sr, c = g.shape
        g5, got3 = g.reshape(l * nc, 2, sr // 2, c), got.reshape(l * nc, sr // 2, c)
    else:
        l, r, c = g.shape
        g5, got3 = g.reshape(l, 2, r // 2, c), got
    a, _, hr, c = g5.shape
    tr = _pick_rows(hr, c, 4, 1 << 20)

    def body(c_ref, g_ref, t_ref, o_ref):
        o_ref[...] = (g_ref[...] + t_ref[...]).astype(BF16)

    half = pl.BlockSpec((None, tr, c), lambda i, j, cc: (i, j, 0))
    out = _prefetch_call(
        body, (a, hr // tr), [pl.BlockSpec((None, None, tr, c), lambda i, j, cc: (i, cc[0], j, 0)), half], half,
        jax.ShapeDtypeStruct(got3.shape, BF16), name)(core1, g5, got3)
    return out.reshape(got.shape)


def _chip_sum(t, rb, kind, chip1, name):
    _, l, hr, sc = rb.shape
    tr = _pick_rows(hr, sc, 4, 1 << 19)

    def body(k_ref, t_ref, rb_ref, o_ref):
        acc = t_ref[...].astype(F32)
        for r in range(N_CHIPS - 1):
            acc = acc + rb_ref[r].astype(F32)
        o_ref[...] = acc

    if kind == "row":
        t_spec = pl.BlockSpec((None, None, tr, sc), lambda a, j, k: (a, k[0], j, 0))
    else:
        t_spec = pl.BlockSpec((None, tr, sc), lambda a, j, k: (a, j, k[0]))
    return _prefetch_call(
        body, (l, hr // tr),
        [t_spec, pl.BlockSpec((N_CHIPS - 1, None, tr, sc), lambda a, j, k: (0, a, j, 0))],
        pl.BlockSpec((None, tr, sc), lambda a, j, k: (a, j, 0)),
        jax.ShapeDtypeStruct((l, hr, sc), F32), name)(chip1, t, rb)


def _adamw_math(w_, g_, m_, v_):
    m2 = ADAM_B1 * m_ + (1.0 - ADAM_B1) * g_
    v2 = ADAM_B2 * v_ + (1.0 - ADAM_B2) * (g_ * g_)
    m_hat = m2 / (1.0 - ADAM_B1 ** ADAM_STEP)
    v_hat = v2 / (1.0 - ADAM_B2 ** ADAM_STEP)
    delta = -ADAM_LR * (m_hat / (jnp.sqrt(v_hat) + ADAM_EPS) + ADAM_WD * w_)
    return delta, m2, v2


def _adamw_big(w, m, v, own, other, core1, name, l0=0, into=None):
    lg, hr, c = own.shape
    lw = w.shape[0]
    view = lambda a: a.reshape(lw, 2, hr, c)
    tr = _pick_rows(hr, c, 4, 1 << 20)
    n_keep = 0 if into is None else len(into)

    def body(c_ref, w_ref, m_ref, v_ref, own_ref, oth_ref, *rest):
        g_ref, d_ref, m2_ref, v2_ref = rest[n_keep:]
        g = jnp.where(pl.program_id(1) == c_ref[0], own_ref[...], oth_ref[...])
        g_ref[...] = g
        d_ref[...], m2_ref[...], v2_ref[...] = _adamw_math(w_ref[...], g, m_ref[...], v_ref[...])

    s4 = pl.BlockSpec((None, None, tr, c), lambda a, h, j, cc: (a + l0, h, j, 0))
    s3 = pl.BlockSpec((None, tr, c), lambda a, h, j, cc: (a, j, 0))
    aliases = {6 + k: k for k in range(n_keep)}
    return _prefetch_call(
        body, (lg, 2, hr // tr), [s4, s4, s4, s3, s3] + [ANY] * n_keep, [s4] * 4,
        [jax.ShapeDtypeStruct((lw, 2, hr, c), F32)] * 4, name, aliases,
    )(core1, view(w), view(m), view(v), own, other, *(into or []))


def _sum_devices(own, gathered, me1, name):
    r, c = own.shape
    tr = _pick_rows(r, c, 4, 1 << 17)

    def body(me_ref, own_ref, g_ref, o_ref):
        acc = None
        for k in range(N_DEV):
            v = jnp.where(me_ref[0] == k, own_ref[...], g_ref[k])
            acc = v if acc is None else acc + v
        o_ref[...] = acc

    return _prefetch_call(
        body, (r // tr,),
        [pl.BlockSpec((tr, c), lambda i, m: (i, 0)), pl.BlockSpec((N_DEV, tr, c), lambda i, m: (0, i, 0))],
        pl.BlockSpec((tr, c), lambda i, m: (i, 0)), jax.ShapeDtypeStruct((r, c), F32), name)(me1, own, gathered)


def _adamw(w, g, m, v, name):
    return _elementwise(_adamw_math, [w, g, m, v], [F32, F32, F32], name)


ANY = pl.BlockSpec(memory_space=pl.ANY)


def _mesh_pos():
    return lax.axis_index("x"), lax.axis_index("y"), lax.axis_index("c")


def _chip_peers(x, y, c):
    out = []
    for r in (1, 2, 3):
        px = 1 - x if r & 2 else x
        py = 1 - y if r & 1 else y
        out.append((2 * px + py, (px, py, c)))
    return out


def _full_shape(kind, shard_shape):
    l, r, c = shard_shape
    return (l, N_CHIPS, r, c) if kind == "row" else (l, r, N_CHIPS * c)


def _full_piece(ref, kind, k, h, hr, sc):
    rows = pl.ds(pl.multiple_of(h * hr, SUBLANES_BF16), hr)
    if kind == "row":
        return ref.at[:, k, rows, :]
    return ref.at[:, rows, pl.ds(pl.multiple_of(k * sc, LANES), sc)]


def _remote(src, dst, ssem, rsem, dev):
    return pltpu.make_async_remote_copy(src_ref=src, dst_ref=dst, send_sem=ssem, recv_sem=rsem,
                                        device_id=dev, device_id_type=MESH)


DMA_CHUNK_BYTES = 1 << 20
DMA_MAX_CHUNKS = 32


def _chunk_views(src, dst):
    axis = len(src.shape) - 2
    rows = src.shape[axis]
    nbytes = math.prod(src.shape) * jnp.dtype(src.dtype).itemsize
    n = max(1, min(DMA_MAX_CHUNKS, nbytes // DMA_CHUNK_BYTES))
    while n > 1 and (rows % n or (rows // n) % SUBLANES_BF16):
        n -= 1
    cr = rows // n
    out = []
    for i in range(n):
        idx = (slice(None),) * axis + (pl.ds(i * cr, cr), slice(None))
        out.append((src.at[idx], dst.at[idx]))
    return out


def _start_remote(src, dst, ssem, rsem, dev):
    for s, t in _chunk_views(src, dst):
        _remote(s, t, ssem, rsem, dev).start()
    return _remote(src, dst, ssem, rsem, dev)


def _allgather_steps(fulls, kinds):
    nw = len(fulls)

    def dims(a, kind):
        return (a.shape[2] // 2, a.shape[3]) if kind == "row" else (a.shape[1] // 2, a.shape[2] // N_CHIPS)

    hrs = [dims(a, k)[0] for a, k in zip(fulls, kinds)]
    scs = [dims(a, k)[1] for a, k in zip(fulls, kinds)]

    def piece(ref, w, k, h):
        return _full_piece(ref, kinds[w], k, h, hrs[w], scs[w])

    def copies1(src, dst, sems, start):
        x, y, c = _mesh_pos()
        k_me = 2 * x + y
        out = []
        for w in range(nw):
            for r, (kj, dev) in enumerate(_chip_peers(x, y, c)):
                args = (sems[0].at[3 * w + r], sems[1].at[3 * w + r], dev)
                if start:
                    out.append(_start_remote(piece(src[w], w, k_me, c), piece(dst[w], w, k_me, c), *args))
                else:
                    out.append(_remote(piece(src[w], w, k_me, c), piece(dst[w], w, kj, c), *args))
        return out

    def copies2(src, dst, sems, start):
        x, y, c = _mesh_pos()
        out = []
        for w in range(nw):
            for r, (kj, _) in enumerate(_chip_peers(x, y, c)):
                args = (sems[0].at[3 * w + r], sems[1].at[3 * w + r], (x, y, 1 - c))
                if start:
                    out.append(_start_remote(piece(src[w], w, kj, c), piece(dst[w], w, kj, c), *args))
                else:
                    out.append(_remote(piece(src[w], w, kj, 1 - c), piece(dst[w], w, kj, 1 - c), *args))
        return out

    def finish(copies):
        def fn(src, dst, sems):
            for cp in copies(src, dst, sems, False):
                cp.wait_recv()
            for cp in copies(src, dst, sems, False):
                cp.wait_send()
        return fn

    step1 = (lambda s, d, m: copies1(s, d, m, True), finish(copies1))
    step2 = (lambda s, d, m: copies2(s, d, m, True), finish(copies2))
    return step1, step2


def _exchange_hook(arrays, step, sem_len, out_shapes=None):
    shapes = out_shapes or [jax.ShapeDtypeStruct(a.shape, a.dtype) for a in arrays]
    return dict(arrays=list(arrays), out_shapes=shapes, in_place=out_shapes is None, sem_len=sem_len,
                first=step[0], last=step[1])


def _combine_hooks(h1, h2):
    assert h1["in_place"] == h2["in_place"]
    n1, o1, s1, s2 = len(h1["arrays"]), len(h1["out_shapes"]), h1["sem_len"], h2["sem_len"]

    def both(which):
        def fn(src, dst, sems):
            h1[which](src[:n1], dst[:o1], [s.at[pl.ds(0, s1)] for s in sems])
            h2[which](src[n1:], dst[o1:], [s.at[pl.ds(s1, s2)] for s in sems])
        return fn

    return dict(arrays=h1["arrays"] + h2["arrays"], out_shapes=h1["out_shapes"] + h2["out_shapes"],
                in_place=h1["in_place"], sem_len=s1 + s2, first=both("first"), last=both("last"))


def _exchange_call(hook, name):
    nh, nho = len(hook["arrays"]), len(hook["out_shapes"])

    def body(*refs):
        h_in, h_out, sems = refs[:nh], refs[nh:nh + nho], refs[nh + nho:]
        hook["first"](h_in, h_out, sems)
        hook["last"](h_in, h_out, sems)

    return pl.pallas_call(
        body, in_specs=[ANY] * nh, out_specs=[ANY] * nho, out_shape=hook["out_shapes"],
        scratch_shapes=[pltpu.SemaphoreType.DMA((hook["sem_len"],))] * 2,
        input_output_aliases={k: k for k in range(nh)} if hook["in_place"] else {}, name=name,
    )(*hook["arrays"])


def _allgather_weights(fulls, kinds):
    nw = len(fulls)
    step1, step2 = _allgather_steps(fulls, kinds)

    def body(*refs):
        mine, fu = refs[:nw], refs[nw:2 * nw]
        sems1, sems2 = refs[2 * nw:2 * nw + 2], refs[2 * nw + 2:]
        step1[0](mine, fu, sems1)
        step1[1](mine, fu, sems1)
        step2[0](fu, fu, sems2)
        step2[1](fu, fu, sems2)

    return pl.pallas_call(
        body,
        in_specs=[ANY] * nw, out_specs=[ANY] * nw,
        out_shape=[jax.ShapeDtypeStruct(a.shape, a.dtype) for a in fulls],
        scratch_shapes=[pltpu.SemaphoreType.DMA((3 * nw,))] * 4,
        input_output_aliases={w: w for w in range(nw)},
        name="allgather_weights",
    )(*fulls)


def _hosted(body, n_in, n_out, hook, n_steps):
    if hook is None:
        return body
    nh, nho = len(hook["arrays"]), len(hook["out_shapes"])

    def wrapped(*refs):
        ins, h_in = refs[:n_in], refs[n_in:n_in + nh]
        outs = refs[n_in + nh:n_in + nh + n_out]
        h_out = refs[n_in + nh + n_out:n_in + nh + n_out + nho]
        rest = refs[n_in + nh + n_out + nho:]
        scr, sems = rest[:-2], rest[-2:]
        i = pl.program_id(0)

        @pl.when(i == 0)
        def _():
            hook["first"](h_in, h_out, sems)

        body(*ins, *outs, *scr)

        @pl.when(i == n_steps - 1)
        def _():
            hook["last"](h_in, h_out, sems)

    return wrapped


def _hosted_call(body, hook, n_steps, in_specs, out_specs, out_shape, scratch_shapes, args, name, aliases=None):
    n_in, n_out = len(in_specs), len(out_specs)
    aliases = dict(aliases or {})
    if hook is not None:
        nh = len(hook["arrays"])
        in_specs = list(in_specs) + [ANY] * nh
        out_specs = list(out_specs) + [ANY] * len(hook["out_shapes"])
        out_shape = list(out_shape) + list(hook["out_shapes"])
        scratch_shapes = list(scratch_shapes) + [pltpu.SemaphoreType.DMA((hook["sem_len"],))] * 2
        if hook["in_place"]:
            aliases.update({n_in + k: n_out + k for k in range(nh)})
        args = list(args) + hook["arrays"]
    outs = pl.pallas_call(
        _hosted(body, n_in, n_out, hook, n_steps), grid=(n_steps,),
        in_specs=in_specs, out_specs=out_specs, out_shape=out_shape, scratch_shapes=scratch_shapes,
        input_output_aliases=aliases, compiler_params=_cparams(), name=name,
    )(*args)
    return outs[:n_out], outs[n_out:]


def _pair_exchange(copies):
    def finish(src, dst, sems):
        for cp in copies(src, dst, sems, False):
            cp.wait_recv()
        for cp in copies(src, dst, sems, False):
            cp.wait_send()
    return (lambda s, d, m: copies(s, d, m, True), finish)


def _rs_pair(fulls, kinds):
    nw = len(fulls)

    def half_all(ref, kind, h):
        if kind == "row":
            hr = ref.shape[2] // 2
            return ref.at[:, :, pl.ds(pl.multiple_of(h * hr, SUBLANES_BF16), hr), :]
        hr = ref.shape[1] // 2
        return ref.at[:, pl.ds(pl.multiple_of(h * hr, SUBLANES_BF16), hr), :]

    def half_shape(kind, shape):
        if kind == "row":
            return (shape[0], shape[1], shape[2] // 2, shape[3])
        return (shape[0], shape[1] // 2, shape[2])

    def copies(g, got, sems, start):
        x, y, c = _mesh_pos()
        make = _start_remote if start else _remote
        return [make(half_all(g[w], kinds[w], 1 - c), got[w], sems[0].at[w], sems[1].at[w], (x, y, 1 - c))
                for w in range(nw)]

    shapes = [jax.ShapeDtypeStruct(half_shape(k, a.shape), a.dtype) for k, a in zip(kinds, fulls)]
    return _exchange_hook(fulls, _pair_exchange(copies), nw, shapes)


def _rs_chips(parts, kinds):
    nw = len(parts)

    def slot(ref, kind, k):
        if kind == "row":
            return ref.at[:, k]
        sc = ref.shape[2] // N_CHIPS
        return ref.at[:, :, pl.ds(pl.multiple_of(k * sc, LANES), sc)]

    def slot_shape(kind, shape):
        if kind == "row":
            return (shape[0], shape[2], shape[3])
        return (shape[0], shape[1], shape[2] // N_CHIPS)

    def copies(t, rb, sems, start):
        x, y, c = _mesh_pos()
        make = _start_remote if start else _remote
        return [make(slot(t[w], kinds[w], kj), rb[w].at[r], sems[0].at[3 * w + r], sems[1].at[3 * w + r], dev)
                for w in range(nw) for r, (kj, dev) in enumerate(_chip_peers(x, y, c))]

    shapes = [jax.ShapeDtypeStruct((N_CHIPS - 1,) + slot_shape(k, a.shape), a.dtype) for k, a in zip(kinds, parts)]
    return _exchange_hook(parts, _pair_exchange(copies), 3 * nw, shapes)


def _rs_join(halves):
    nw = len(halves)

    def copies(src, dst, sems, start):
        x, y, c = _mesh_pos()
        make = _start_remote if start else _remote
        return [make(src[w], dst[w], sems[0].at[w], sems[1].at[w], (x, y, 1 - c)) for w in range(nw)]

    return _exchange_hook(halves, _pair_exchange(copies), nw,
                          [jax.ShapeDtypeStruct(a.shape, a.dtype) for a in halves])


def _allgather_small(buf, name):
    def body(in_ref, out_ref, ssem, rsem):
        x, y, c = _mesh_pos()
        me = 4 * x + 2 * y + c
        cps, waits = [], []
        for r in range(1, N_DEV):
            px = 1 - x if r & 4 else x
            py = 1 - y if r & 2 else y
            pc = 1 - c if r & 1 else c
            cp = _remote(in_ref, out_ref.at[me], ssem.at[r - 1], rsem.at[r - 1], (px, py, pc))
            cp.start()
            cps.append(cp)
            waits.append(_remote(in_ref, out_ref.at[4 * px + 2 * py + pc], ssem.at[r - 1], rsem.at[r - 1], (px, py, pc)))
        for wt in waits:
            wt.wait_recv()
        for cp in cps:
            cp.wait_send()

    return pl.pallas_call(
        body, in_specs=[ANY], out_specs=ANY,
        out_shape=jax.ShapeDtypeStruct((N_DEV,) + buf.shape, buf.dtype),
        scratch_shapes=[pltpu.SemaphoreType.DMA((N_DEV - 1,))] * 2,
        name=name,
    )(buf)


def _pack(arrs):
    flat = jnp.concatenate([a.reshape(-1).astype(F32) for a in arrs])
    rows = -(-flat.shape[0] // (LANES * 16)) * 16
    return jnp.pad(flat, (0, rows * LANES - flat.shape[0])).reshape(rows, LANES)


def _unpack(buf, shapes):
    flat = buf.reshape(-1)
    out, off = [], 0
    for shp in shapes:
        nel = math.prod(shp)
        out.append(flat[off:off + nel].reshape(shp))
        off += nel
    return out


BIG = ("a_w_in", "a_w_out", "b_w_in", "b_w_out", "c_w_in", "c_w_grp", "c_w_out", "f_w_up", "f_w_down")
BIG_KIND = {"a_w_in": "col", "a_w_out": "row", "b_w_in": "col", "b_w_out": "row", "c_w_in": "row",
            "c_w_grp": "row", "c_w_out": "row", "f_w_up": "col", "f_w_down": "row"}
FIRST_LAYER = ("a_w_in", "a_w_out", "f_w_up", "f_w_down")
SHARDED_SMALL =("a_dw", "a_dw_b", "a_ln_g", "a_ln_b", "c_scale", "f_dw")
REPLICATED = ("b_ln_g", "b_ln_b", "b_ws", "b_bs", "ln1_g", "ln1_b", "ln2_g", "ln2_b")
WEIGHTS = ("a_w_in", "a_dw", "a_dw_b", "a_ln_g", "a_ln_b", "a_w_out", "b_w_in", "b_ln_g", "b_ln_b", "b_ws", "b_bs",
           "b_w_out", "c_w_in", "c_w_grp", "c_scale", "c_w_out", "f_w_up", "f_dw", "f_w_down",
           "ln1_g", "ln1_b", "ln2_g", "ln2_b")


def _as3d(a):
    return a.reshape((-1,) + a.shape[-2:])


def kernel(x, a_w_in, a_dw, a_dw_b, a_ln_g, a_ln_b, a_w_out, b_w_in, b_ln_g, b_ln_b, b_ws, b_bs, b_w_out, c_w_in, c_w_grp, c_scale, c_w_out, f_w_up, f_dw, f_w_down, ln1_g, ln1_b, ln2_g, ln2_b, loss_target, m_a_w_in, m_a_dw, m_a_dw_b, m_a_ln_g, m_a_ln_b, m_a_w_out, m_b_w_in, m_b_ln_g, m_b_ln_b, m_b_ws, m_b_bs, m_b_w_out, m_c_w_in, m_c_w_grp, m_c_scale, m_c_w_out, m_f_w_up, m_f_dw, m_f_w_down, m_ln1_g, m_ln1_b, m_ln2_g, m_ln2_b, v_a_w_in, v_a_dw, v_a_dw_b, v_a_ln_g, v_a_ln_b, v_a_w_out, v_b_w_in, v_b_ln_g, v_b_ln_b, v_b_ws, v_b_bs, v_b_w_out, v_c_w_in, v_c_w_grp, v_c_scale, v_c_w_out, v_f_w_up, v_f_dw, v_f_w_down, v_ln1_g, v_ln1_b, v_ln2_g, v_ln2_b):
    w = dict(a_w_in=a_w_in, a_dw=a_dw, a_dw_b=a_dw_b, a_ln_g=a_ln_g, a_ln_b=a_ln_b, a_w_out=a_w_out, b_w_in=b_w_in, b_ln_g=b_ln_g, b_ln_b=b_ln_b, b_ws=b_ws, b_bs=b_bs, b_w_out=b_w_out, c_w_in=c_w_in, c_w_grp=c_w_grp, c_scale=c_scale, c_w_out=c_w_out, f_w_up=f_w_up, f_dw=f_dw, f_w_down=f_w_down, ln1_g=ln1_g, ln1_b=ln1_b, ln2_g=ln2_g, ln2_b=ln2_b)
    mom = dict(a_w_in=m_a_w_in, a_dw=m_a_dw, a_dw_b=m_a_dw_b, a_ln_g=m_a_ln_g, a_ln_b=m_a_ln_b, a_w_out=m_a_w_out, b_w_in=m_b_w_in, b_ln_g=m_b_ln_g, b_ln_b=m_b_ln_b, b_ws=m_b_ws, b_bs=m_b_bs, b_w_out=m_b_w_out, c_w_in=m_c_w_in, c_w_grp=m_c_w_grp, c_scale=m_c_scale, c_w_out=m_c_w_out, f_w_up=m_f_w_up, f_dw=m_f_dw, f_w_down=m_f_w_down, ln1_g=m_ln1_g, ln1_b=m_ln1_b, ln2_g=m_ln2_g, ln2_b=m_ln2_b)
    var = dict(a_w_in=v_a_w_in, a_dw=v_a_dw, a_dw_b=v_a_dw_b, a_ln_g=v_a_ln_g, a_ln_b=v_a_ln_b, a_w_out=v_a_w_out, b_w_in=v_b_w_in, b_ln_g=v_b_ln_g, b_ln_b=v_b_ln_b, b_ws=v_b_ws, b_bs=v_b_bs, b_w_out=v_b_w_out, c_w_in=v_c_w_in, c_w_grp=v_c_w_grp, c_scale=v_c_scale, c_w_out=v_c_w_out, f_w_up=v_f_w_up, f_dw=v_f_dw, f_w_down=v_f_w_down, ln1_g=v_ln1_g, ln1_b=v_ln1_b, ln2_g=v_ln2_g, ln2_b=v_ln2_b)

    depth = ln1_g.shape[0]
    d = x.shape[-1]
    alpha = float((2 * depth) ** 0.25)
    chip = 2 * lax.axis_index("x") + lax.axis_index("y")
    chip1 = chip.astype(jnp.int32).reshape(1)
    core1 = lax.axis_index("c").astype(jnp.int32).reshape(1)

    assert depth == 4
    early = [(k, 0, 1) for k in FIRST_LAYER]
    late = [(k, 1, w[k].shape[0] - 1) for k in FIRST_LAYER] + [(k, 0, _as3d(w[k]).shape[0]) for k in BIG if k not in FIRST_LAYER]
    groups = {
        "up_front": [("a_w_in", 0, 1), ("a_w_out", 0, 1)],
        "soon": [("f_w_up", 0, 1), ("f_w_down", 0, 1)],
        "mid": [("f_w_up", 1, 2), ("f_w_down", 1, 2)] + [(k, 0, _as3d(w[k]).shape[0]) for k in BIG if k not in FIRST_LAYER],
        "last": [("a_w_in", 1, 1), ("a_w_out", 1, 1), ("f_w_up", 3, 1), ("f_w_down", 3, 1)],
    }
    where = {(k, l0 + t): (name, t) for name, group in groups.items() for k, l0, nl in group for t in range(nl)}

    def group_kinds(group):
        return [BIG_KIND[k] for k, _, _ in group]

    def cast_group(name):
        return [_cast_into_full(_as3d(w[k]), l0, nl, BIG_KIND[k], chip1, f"cast_{name}_{k}") for k, l0, nl in groups[name]]

    def as_stacks(name, arrays):
        return {k: (a.reshape(a.shape[0], -1, a.shape[-1]) if BIG_KIND[k] == "row" else a)
                for (k, _, _), a in zip(groups[name], arrays)}

    def ag_hook(name, arrays, step):
        steps = _allgather_steps(arrays, group_kinds(groups[name]))
        return _exchange_hook(arrays, steps[step], 3 * len(arrays))

    full = {"up_front": as_stacks("up_front", _allgather_weights(cast_group("up_front"), group_kinds(groups["up_front"])))}

    def weight(k, l):
        name, idx = where[(k, l)]
        return full[name][k], idx

    small_all = _allgather_small(_pack([w[k] for k in SHARDED_SMALL]), "allgather_small_params")
    other_core = 1 - lax.axis_index("c")
    per_chip = [_unpack(lax.dynamic_index_in_dim(small_all, 2 * k + other_core, keepdims=False),
                        [w[n].shape for n in SHARDED_SMALL]) for k in range(N_CHIPS)]
    fs = {n: jnp.concatenate([per_chip[k][i] for k in range(N_CHIPS)], axis=-1) for i, n in enumerate(SHARDED_SMALL)}

    nh = b_ws.shape[1]
    tril = jnp.tril(jnp.ones((CHUNK, CHUNK), F32))
    wm = (b_ws[0] * tril).astype(BF16)
    wmt = jnp.swapaxes(wm, 1, 2)
    bs_exp = jnp.repeat(jnp.transpose(b_bs[0]), CHUNK, axis=1)

    xh, g, b = x[0], jnp.ones((1, d), F32), jnp.zeros((1, d), F32)
    saved = []
    for i in range(depth):
        kind, j = i % 3, i // 3
        rec = dict(xin=xh, gin=g, bin=b)
        if kind == 0:
            hook = None
            if i == 0:
                n_soon = len(groups["soon"])
                hook = _combine_hooks(ag_hook("soon", cast_group("soon"), 0), ag_hook("mid", cast_group("mid"), 0))
            (xh1, rstd1, p, chat, rstdc), landed = _conv_fwd(
                xh, g, b, weight("a_w_in", j)[0], weight("a_w_out", j)[0], weight("a_w_in", j)[1], fs["a_dw"], j,
                fs["a_dw_b"][j:j + 1], fs["a_ln_g"][j:j + 1], fs["a_ln_b"][j:j + 1], alpha, f"conv_fwd_{i}", hook)
            if i == 0:
                full["soon"] = as_stacks("soon", _exchange_call(ag_hook("soon", landed[:n_soon], 1), "allgather_soon_d2d"))
                mid_landed = landed[n_soon:]
            rec.update(p=p, chat=chat, rstdc=rstdc)
        elif kind == 1:
            xh1, rstd1, zp = _sgu_fwd(xh, g, b, full["mid"]["b_w_in"], b_ln_g, b_ln_b, wm, bs_exp,
                                      full["mid"]["b_w_out"], alpha, f"sgu_fwd_{i}")
            rec.update(zp=zp)
        else:
            xh1, rstd1, ys = _pool_fwd(xh, g, b, full["mid"]["c_w_in"], full["mid"]["c_w_grp"], fs["c_scale"],
                                       full["mid"]["c_w_out"], alpha, f"pool_fwd_{i}")
            rec.update(ys=ys)
        hook = None
        if i == 0:
            hook = _combine_hooks(ag_hook("mid", mid_landed, 1), ag_hook("last", cast_group("last"), 0))
        elif i == 1:
            hook = ag_hook("last", last_landed, 1)
        (xh2, rstd2, hs, hcs), passed_on = _ffn_fwd(
            xh1, ln1_g[i:i + 1], ln1_b[i:i + 1], weight("f_w_up", i)[0], weight("f_w_down", i)[0],
            weight("f_w_up", i)[1], fs["f_dw"], i, alpha, f"ffn_fwd_{i}", hook)
        if i == 0:
            full["mid"] = as_stacks("mid", passed_on[:len(mid_landed)])
            last_landed = passed_on[len(mid_landed):]
        elif i == 1:
            full["last"] = as_stacks("last", passed_on)
        rec.update(xh1=xh1, rstd1=rstd1, xh2=xh2, rstd2=rstd2, hs=hs, hcs=hcs)
        saved.append(rec)
        xh, g, b = xh2, ln2_g[i:i + 1], ln2_b[i:i + 1]

    dxo = loss_target[0]
    assert depth >= 3

    def stack_shape(k, nl):
        _, r, c = _as3d(w[k]).shape
        return (nl, N_CHIPS * r, c) if BIG_KIND[k] == "row" else (nl, r, N_CHIPS * c)

    g_first = {k: lax.empty(stack_shape(k, nl), F32) for k, _, nl in early}
    g_rest = {k: lax.empty(stack_shape(k, nl), F32) for k, _, nl in late if k != "c_w_grp"}

    def gslot(k, l):
        if k in FIRST_LAYER:
            return (g_first, 0) if l == 0 else (g_rest, l - 1)
        return g_rest, l

    def rs_views(group, store):
        out = []
        for k, _, _ in group:
            a = store[k]
            out.append(a.reshape(a.shape[0], N_CHIPS, -1, a.shape[-1]) if BIG_KIND[k] == "row" else a)
        return out

    def pair_sums(group, views, got, tag):
        return [_pair_sum(a, t, BIG_KIND[k], core1, f"rs_pair_sum_{tag}_{k}") for (k, _, _), a, t in zip(group, views, got)]

    def reduce_and_join(group, pair, from_chips, tag):
        half = [_chip_sum(t, rb, BIG_KIND[k], chip1, f"rs_chip_sum_{tag}_{k}")
                for (k, _, _), t, rb in zip(group, pair, from_chips)]
        return half, _exchange_call(_rs_join(half), f"rs_join_{tag}")

    ffn0, mix0 = groups["soon"], groups["up_front"]
    late_kinds = group_kinds(late)
    gs = {k: [None] * w[k].shape[0] for k in ("a_dw", "a_dw_b", "a_ln_g", "a_ln_b", "f_dw", "ln1_g", "ln1_b", "ln2_g", "ln2_b")}
    for i in reversed(range(depth)):
        kind, j = i % 3, i // 3
        rec = saved[i]
        hook = None
        if i == 0:
            late_views = rs_views(late, g_rest)
            hook = _rs_pair(late_views, late_kinds)
        st, idx = gslot("f_w_down", i)
        (dr2, dhc, st["f_w_down"], gs["ln2_g"][i], gs["ln2_b"][i], loss_term), got = _ffn_bwd1(
            dxo, rec["xh2"], rec["rstd2"], ln2_g[i:i + 1], ln2_b[i:i + 1], rec["hcs"], *weight("f_w_down", i),
            st["f_w_down"], idx, f"ffn_bwd1_{i}", hook, loss_head=(i == depth - 1))
        if i == depth - 1:
            loss = lax.psum(loss_term[0, 0], ("x", "y", "c"))
        if i == 0:
            late_pair = pair_sums(late, late_views, got, "rest")
        dr1, dh, gs["f_dw"][i], gs["ln1_g"][i], gs["ln1_b"][i] = _ffn_bwd2(
            dhc, rec["hs"], dr2, *weight("f_w_up", i), fs["f_dw"], i, rec["xh1"], rec["rstd1"], ln1_g[i:i + 1],
            alpha, f"ffn_bwd2_{i}")
        st, idx = gslot("f_w_up", i)
        st["f_w_up"] = _mm_tn(rec["xh1"], ln1_g[i:i + 1], ln1_b[i:i + 1], dh, st["f_w_up"], idx, f"grad_w_up_{i}")
        if kind == 0:
            hook = None
            if i == 0:
                ffn0_views = rs_views(ffn0, g_first)
                ffn0_got = _exchange_call(_rs_pair(ffn0_views, group_kinds(ffn0)), "rs_pair_ffn0")
                ffn0_pair = pair_sums(ffn0, ffn0_views, ffn0_got, "ffn0")
                hook = _combine_hooks(_rs_chips(late_pair, late_kinds), _rs_chips(ffn0_pair, group_kinds(ffn0)))
            st, idx = gslot("a_w_out", j)
            (dp, st["a_w_out"], gs["a_dw"][j], gs["a_dw_b"][j], gs["a_ln_g"][j], gs["a_ln_b"][j]), landed = _conv_bwd1(
                dr1, rec["chat"], rec["rstdc"], rec["p"], *weight("a_w_out", j), fs["a_dw"], j,
                fs["a_ln_g"][j:j + 1], fs["a_ln_b"][j:j + 1], st["a_w_out"], idx, f"conv_bwd1_{i}", hook)
            if i == 0:
                late_from_chips, ffn0_from_chips = landed[:len(late)], landed[len(late):]
            win_name, lidx = "a_w_in", j
        elif kind == 1:
            dp, g_rest["b_w_out"], g_ws, g_bs_t, g_blg, g_blb = _sgu_bwd1(
                dr1, rec["zp"], full["mid"]["b_w_out"], b_ln_g, b_ln_b, wm, wmt, bs_exp, g_rest["b_w_out"],
                f"sgu_bwd1_{i}")
            win_name, lidx = "b_w_in", 0
        else:
            dp, g_rest["c_w_out"], g_rest["c_w_grp"], g_cscale = _pool_bwd1(
                dr1, rec["ys"], full["mid"]["c_w_out"], full["mid"]["c_w_grp"], fs["c_scale"], g_rest["c_w_out"],
                f"pool_bwd1_{i}")
            win_name, lidx = "c_w_in", 0
        dxo = _bwd_in(dp, dr1, *weight(win_name, lidx), alpha, f"mixer_bwd2_{i}")
        st, idx = gslot(win_name, lidx)
        st[win_name] = _mm_tn(rec["xin"], rec["gin"], rec["bin"], dp, st[win_name], idx, f"grad_w_in_{i}")
    grad_x = dxo[None]

    late_half, late_other = reduce_and_join(late, late_pair, late_from_chips, "rest")
    ffn0_half, ffn0_other = reduce_and_join(ffn0, ffn0_pair, ffn0_from_chips, "ffn0")
    mix0_views = rs_views(mix0, g_first)
    mix0_got = _exchange_call(_rs_pair(mix0_views, group_kinds(mix0)), "rs_pair_mix0")
    mix0_pair = pair_sums(mix0, mix0_views, mix0_got, "mix0")
    mix0_from_chips = _exchange_call(_rs_chips(mix0_pair, group_kinds(mix0)), "rs_chips_mix0")
    mix0_half, mix0_other = reduce_and_join(mix0, mix0_pair, mix0_from_chips, "mix0")

    updates = {}
    for (k, l0, _), own, oth in zip(late, late_half, late_other):
        updates[k] = _adamw_big(_as3d(w[k]), _as3d(mom[k]), _as3d(var[k]), own, oth, core1, f"adamw_rest_{k}", l0)
    for (k, l0, _), own, oth in zip(ffn0 + mix0, ffn0_half + mix0_half, ffn0_other + mix0_other):
        updates[k] = _adamw_big(_as3d(w[k]), _as3d(mom[k]), _as3d(var[k]), own, oth, core1, f"adamw_first_{k}",
                                l0, into=updates[k])
    grads, delta, new_m, new_v = {}, {}, {}, {}
    for k in BIG:
        grads[k], delta[k], new_m[k], new_v[k] = [o.reshape(w[k].shape) for o in updates[k]]

    small_full = {
        "a_dw": jnp.stack(gs["a_dw"]), "a_dw_b": jnp.concatenate(gs["a_dw_b"]), "a_ln_g": jnp.concatenate(gs["a_ln_g"]),
        "a_ln_b": jnp.concatenate(gs["a_ln_b"]), "c_scale": g_cscale, "f_dw": jnp.stack(gs["f_dw"]),
        "b_ln_g": g_blg, "b_ln_b": g_blb, "b_ws": g_ws[None], "b_bs": jnp.transpose(g_bs_t)[None],
        "ln1_g": jnp.concatenate(gs["ln1_g"]), "ln1_b": jnp.concatenate(gs["ln1_b"]),
        "ln2_g": jnp.concatenate(gs["ln2_g"]), "ln2_b": jnp.concatenate(gs["ln2_b"]),
    }
    small_names = SHARDED_SMALL + REPLICATED
    small_shapes = [small_full[n].shape for n in small_names]
    small_packed = _pack([small_full[n] for n in small_names])
    gathered_small = _allgather_small(small_packed, "allgather_small_grads")
    me1 = (2 * chip + lax.axis_index("c")).astype(jnp.int32).reshape(1)
    summed = _unpack(_sum_devices(small_packed, gathered_small, me1, "small_grad_sum"), small_shapes)
    for n, a in zip(small_names, summed):
        if n in SHARDED_SMALL:
            cs = w[n].shape[-1]
            a = lax.dynamic_slice_in_dim(a, chip * cs, cs, axis=a.ndim - 1)
        grads[n] = a

    shapes = [w[n].shape for n in small_names]
    ds_, ms_, vs_ = _adamw(_pack([w[n] for n in small_names]), _pack([grads[n] for n in small_names]),
                           _pack([mom[n] for n in small_names]), _pack([var[n] for n in small_names]), "adamw_small")
    for n, a, bb, cc in zip(small_names, _unpack(ds_, shapes), _unpack(ms_, shapes), _unpack(vs_, shapes)):
        delta[n], new_m[n], new_v[n] = a, bb, cc

    return (loss, grad_x, *[grads[n] for n in WEIGHTS], *[delta[n] for n in WEIGHTS],
            *[new_m[n] for n in WEIGHTS], *[new_v[n] for n in WEIGHTS])
```
